```python
import jax, jax.numpy as jnp
from jax import lax
import numpy as np

D_MODEL = 1024
BATCH = 8
SEQ = 16384
DEPTH = 1

PLE_DIM = 256
GRID_W = 64
ATTN_HEAD_DIM = 64
ATTN_HEADS = (D_MODEL // 2) // ATTN_HEAD_DIM
ATTN_KV_HEADS = ATTN_HEADS // 4
RET_HEAD_DIM = 128
RET_HEADS = (D_MODEL // 2) // RET_HEAD_DIM
ATTN_Q_W = ATTN_HEADS * ATTN_HEAD_DIM
ATTN_KV_W = ATTN_KV_HEADS * ATTN_HEAD_DIM
RET_W = RET_HEADS * RET_HEAD_DIM
IN_SPLITS = (ATTN_Q_W, ATTN_KV_W, ATTN_KV_W, RET_W, RET_W, RET_W, RET_W, D_MODEL, D_MODEL)
IN_W = ATTN_Q_W + 2 * ATTN_KV_W + 4 * RET_W + 2 * D_MODEL
D_FF = 4 * D_MODEL
Q_BLOCK = 128
RET_CHUNK = 128
ROPE_THETA = 10000.0
NORM_EPS = 1e-6
GN_EPS = 1e-5

kernel_name = "hybrid_gqa_retention_gated_encoder"


def rms_norm(x, gain):
    x32 = x.astype(jnp.float32)
    y = x32 * lax.rsqrt(jnp.mean(x32 * x32, axis=-1, keepdims=True) + NORM_EPS)
    return (y * gain.astype(jnp.float32)).astype(x.dtype)


def axial_rope_tables(seq_len, head_dim):
    rows = seq_len // GRID_W
    n_axis = head_dim // 4
    freqs = ROPE_THETA ** (-jnp.arange(n_axis, dtype=jnp.float32) / n_axis)
    row = jnp.repeat(jnp.arange(rows, dtype=jnp.float32), GRID_W)
    col = jnp.tile(jnp.arange(GRID_W, dtype=jnp.float32), rows)
    ang = jnp.concatenate([row[:, None] * freqs, col[:, None] * freqs], axis=-1)
    return jnp.cos(ang), jnp.sin(ang)


def apply_rope(x, cos, sin):
    x32 = x.astype(jnp.float32)
    x1, x2 = jnp.split(x32, 2, axis=-1)
    c = cos[None, :, None, :]
    s = sin[None, :, None, :]
    return jnp.concatenate([x1 * c - x2 * s, x1 * s + x2 * c], axis=-1).astype(x.dtype)


def gqa_attention(q, k, v):
    b, s, _, hd = q.shape
    g = ATTN_HEADS // ATTN_KV_HEADS
    nb = s // Q_BLOCK
    qb = q.reshape(b, nb, Q_BLOCK, ATTN_KV_HEADS, g, hd).transpose(1, 0, 3, 4, 2, 5)
    kt = k.transpose(0, 2, 1, 3)
    vt = v.transpose(0, 2, 1, 3)
    scale = hd ** -0.5

    def block(qi):
        sc = jnp.einsum('bkgqd,bksd->bkgqs', qi, kt).astype(jnp.float32) * scale
        pr = jax.nn.softmax(sc, axis=-1).astype(vt.dtype)
        return jnp.einsum('bkgqs,bksd->bkgqd', pr, vt)

    o = lax.map(block, qb)
    return o.transpose(1, 0, 4, 2, 3, 5).reshape(b, s, ATTN_HEADS * hd)


def retention_direction(q, k, v, log_gamma, include_diag):
    b, h, s, dk = q.shape
    dv = v.shape[-1]
    c = RET_CHUNK
    nc = s // c
    qc = q.reshape(b, h, nc, c, dk)
    kc = k.reshape(b, h, nc, c, dk)
    vc = v.reshape(b, h, nc, c, dv)
    pos = jnp.arange(c, dtype=jnp.float32)
    lg = log_gamma[:, None]
    diff = pos[:, None] - pos[None, :]
    mask = (diff >= 0) if include_diag else (diff > 0)
    dmask = jnp.where(mask[None], jnp.exp(lg[:, :, None] * jnp.maximum(diff, 0.0)[None]), 0.0)
    scores = jnp.einsum('bhnid,bhnjd->bhnij', qc, kc) * dmask[None, :, None]
    y = jnp.einsum('bhnij,bhnje->bhnie', scores, vc)
    k_dec = jnp.exp(lg * (c - 1 - pos))
    kv = jnp.einsum('bhnjd,bhnje->nbhde', kc * k_dec[None, :, None, :, None], vc)
    chunk_decay = jnp.exp(log_gamma * c)[None, :, None, None]

    def step(state, kv_n):
        return state * chunk_decay + kv_n, state

    _, prev = lax.scan(step, jnp.zeros((b, h, dk, dv), jnp.float32), kv)
    q_dec = jnp.exp(lg * (pos + 1.0))
    y = y + jnp.einsum('bhnid,nbhde->bhnie', qc * q_dec[None, :, None, :, None], prev)
    return y.reshape(b, h, s, dv)


def bidirectional_retention(q, k, v, decay_logit):
    log_g = jax.nn.log_sigmoid(decay_logit.astype(jnp.float32))
    qt = q.astype(jnp.float32).transpose(0, 2, 1, 3) * (RET_HEAD_DIM ** -0.5)
    kt = k.astype(jnp.float32).transpose(0, 2, 1, 3)
    vt = v.astype(jnp.float32).transpose(0, 2, 1, 3)
    fwd = retention_direction(qt, kt, vt, log_g[0], True)
    flip = lambda a: jnp.flip(a, axis=2)
    bwd = flip(retention_direction(flip(qt), flip(kt), flip(vt), log_g[1], False))
    return fwd + bwd


def hybrid_layer(x, p_i, mix_norm, w_in, attn_q_norm, attn_k_norm, ret_decay_logit, ret_norm_gain,
                 w_attn_o, w_ret_o, w_out, mlp_norm, w_up, w_down, ple_norm, w_ple_gate, w_ple,
                 rope_a, rope_r):
    b, s, _ = x.shape
    dt = x.dtype
    h = rms_norm(x, mix_norm)
    proj = h @ w_in
    split_idx = [int(i) for i in np.cumsum(IN_SPLITS)[:-1]]
    aq, ak, av, rq, rk, rv, rg, gate_a, gate_r = jnp.split(proj, split_idx, axis=-1)

    aq = rms_norm(aq.reshape(b, s, ATTN_HEADS, ATTN_HEAD_DIM), attn_q_norm)
    ak = rms_norm(ak.reshape(b, s, ATTN_KV_HEADS, ATTN_HEAD_DIM), attn_k_norm)
    av = av.reshape(b, s, ATTN_KV_HEADS, ATTN_HEAD_DIM)
    aq = apply_rope(aq, *rope_a)
    ak = apply_rope(ak, *rope_a)
    attn_out = gqa_attention(aq, ak, av).astype(dt) @ w_attn_o

    rq = apply_rope(rq.reshape(b, s, RET_HEADS, RET_HEAD_DIM), *rope_r)
    rk = apply_rope(rk.reshape(b, s, RET_HEADS, RET_HEAD_DIM), *rope_r)
    rv = rv.reshape(b, s, RET_HEADS, RET_HEAD_DIM)
    ry = bidirectional_retention(rq, rk, rv, ret_decay_logit)
    mu = jnp.mean(ry, axis=-1, keepdims=True)
    var = jnp.mean(jnp.square(ry - mu), axis=-1, keepdims=True)
    ry = ((ry - mu) * lax.rsqrt(var + GN_EPS)).transpose(0, 2, 1, 3).reshape(b, s, RET_W)
    ry = ry * ret_norm_gain.astype(jnp.float32) * jax.nn.silu(rg.astype(jnp.float32))
    ret_out = ry.astype(dt) @ w_ret_o

    merged = jax.nn.sigmoid(gate_a) * attn_out + jax.nn.sigmoid(gate_r) * ret_out
    x = x + merged @ w_out

    hm = rms_norm(x, mlp_norm)
    x = x + jnp.square(jax.nn.relu(hm @ w_up)) @ w_down

    gate = jax.nn.sigmoid(rms_norm(x, ple_norm) @ w_ple_gate)
    x = x + gate * (p_i.astype(dt) @ w_ple)
    return x


def _fwd_setup_inputs(seed: int = 0) -> dict:
    key = jax.random.key(seed)
    ks = jax.random.split(key, 20)
    f32 = jnp.float32

    def w(k, shape, fan_in):
        return jax.random.normal(k, shape, f32) * (fan_in ** -0.5)

    def gain(k, shape):
        return 1.0 + 0.05 * jax.random.normal(k, shape, f32)

    gamma0 = 1.0 - 2.0 ** (-5.0 - jnp.arange(RET_HEADS, dtype=f32))
    logit0 = jnp.log(gamma0) - jnp.log1p(-gamma0)
    decay_logit = logit0[None, None, :] + 0.05 * jax.random.normal(ks[5], (DEPTH, 2, RET_HEADS), f32)

    return {
        "x": jax.random.normal(ks[0], (BATCH, SEQ, D_MODEL), f32),
        "p": jax.random.normal(ks[1], (DEPTH, BATCH, SEQ, PLE_DIM), f32),
        "mix_norm": gain(ks[2], (DEPTH, D_MODEL)),
        "w_in": w(ks[3], (DEPTH, D_MODEL, IN_W), D_MODEL),
        "attn_q_norm": gain(ks[4], (DEPTH, ATTN_HEAD_DIM)),
        "attn_k_norm": gain(ks[6], (DEPTH, ATTN_HEAD_DIM)),
        "ret_decay_logit": decay_logit,
        "ret_norm_gain": gain(ks[7], (DEPTH, RET_W)),
        "w_attn_o": w(ks[8], (DEPTH, ATTN_Q_W, D_MODEL), ATTN_Q_W),
        "w_ret_o": w(ks[9], (DEPTH, RET_W, D_MODEL), RET_W),
        "w_out": w(ks[10], (DEPTH, D_MODEL, D_MODEL), D_MODEL),
        "mlp_norm": gain(ks[11], (DEPTH, D_MODEL)),
        "w_up": w(ks[12], (DEPTH, D_MODEL, D_FF), D_MODEL),
        "w_down": w(ks[13], (DEPTH, D_FF, D_MODEL), D_FF),
        "ple_norm": gain(ks[14], (DEPTH, D_MODEL)),
        "w_ple_gate": w(ks[15], (DEPTH, D_MODEL, D_MODEL), D_MODEL),
        "w_ple": w(ks[16], (DEPTH, PLE_DIM, D_MODEL), PLE_DIM),
        "final_norm": gain(ks[17], (D_MODEL,)),
    }


def _fwd_reference(x, p, mix_norm, w_in, attn_q_norm, attn_k_norm, ret_decay_logit, ret_norm_gain,
              w_attn_o, w_ret_o, w_out, mlp_norm, w_up, w_down, ple_norm, w_ple_gate, w_ple,
              final_norm):
    seq_len = x.shape[1]
    rope_a = axial_rope_tables(seq_len, ATTN_HEAD_DIM)
    rope_r = axial_rope_tables(seq_len, RET_HEAD_DIM)
    for i in range(DEPTH):
        x = hybrid_layer(x, p[i], mix_norm[i], w_in[i], attn_q_norm[i], attn_k_norm[i],
                         ret_decay_logit[i], ret_norm_gain[i], w_attn_o[i], w_ret_o[i], w_out[i],
                         mlp_norm[i], w_up[i], w_down[i], ple_norm[i], w_ple_gate[i], w_ple[i],
                         rope_a, rope_r)
    return rms_norm(x, final_norm)


import jax as _jax
import jax.numpy as _jnp

TWIN_FORMAT = 'train_step'
FWD_PARAMS = ['x', 'p', 'mix_norm', 'w_in', 'attn_q_norm', 'attn_k_norm', 'ret_decay_logit', 'ret_norm_gain', 'w_attn_o', 'w_ret_o', 'w_out', 'mlp_norm', 'w_up', 'w_down', 'ple_norm', 'w_ple_gate', 'w_ple', 'final_norm']
TWIN_WEIGHTS = ['mix_norm', 'w_in', 'attn_q_norm', 'attn_k_norm', 'ret_decay_logit', 'ret_norm_gain', 'w_attn_o', 'w_ret_o', 'w_out', 'mlp_norm', 'w_up', 'w_down', 'ple_norm', 'w_ple_gate', 'w_ple', 'final_norm']
TWIN_DIFF_INPUT = 'x'
TWIN_INPUTS = ['x', 'p', 'mix_norm', 'w_in', 'attn_q_norm', 'attn_k_norm', 'ret_decay_logit', 'ret_norm_gain', 'w_attn_o', 'w_ret_o', 'w_out', 'mlp_norm', 'w_up', 'w_down', 'ple_norm', 'w_ple_gate', 'w_ple', 'final_norm', 'loss_target', 'm_mix_norm', 'm_w_in', 'm_attn_q_norm', 'm_attn_k_norm', 'm_ret_decay_logit', 'm_ret_norm_gain', 'm_w_attn_o', 'm_w_ret_o', 'm_w_out', 'm_mlp_norm', 'm_w_up', 'm_w_down', 'm_ple_norm', 'm_w_ple_gate', 'm_w_ple', 'm_final_norm', 'v_mix_norm', 'v_w_in', 'v_attn_q_norm', 'v_attn_k_norm', 'v_ret_decay_logit', 'v_ret_norm_gain', 'v_w_attn_o', 'v_w_ret_o', 'v_w_out', 'v_mlp_norm', 'v_w_up', 'v_w_down', 'v_ple_norm', 'v_w_ple_gate', 'v_w_ple', 'v_final_norm']
TWIN_OUTPUTS = ['loss', 'grad_x', 'grad_mix_norm', 'grad_w_in', 'grad_attn_q_norm', 'grad_attn_k_norm', 'grad_ret_decay_logit', 'grad_ret_norm_gain', 'grad_w_attn_o', 'grad_w_ret_o', 'grad_w_out', 'grad_mlp_norm', 'grad_w_up', 'grad_w_down', 'grad_ple_norm', 'grad_w_ple_gate', 'grad_w_ple', 'grad_final_norm', 'delta_mix_norm', 'delta_w_in', 'delta_attn_q_norm', 'delta_attn_k_norm', 'delta_ret_decay_logit', 'delta_ret_norm_gain', 'delta_w_attn_o', 'delta_w_ret_o', 'delta_w_out', 'delta_mlp_norm', 'delta_w_up', 'delta_w_down', 'delta_ple_norm', 'delta_w_ple_gate', 'delta_w_ple', 'delta_final_norm', 'new_m_mix_norm', 'new_m_w_in', 'new_m_attn_q_norm', 'new_m_attn_k_norm', 'new_m_ret_decay_logit', 'new_m_ret_norm_gain', 'new_m_w_attn_o', 'new_m_w_ret_o', 'new_m_w_out', 'new_m_mlp_norm', 'new_m_w_up', 'new_m_w_down', 'new_m_ple_norm', 'new_m_w_ple_gate', 'new_m_w_ple', 'new_m_final_norm', 'new_v_mix_norm', 'new_v_w_in', 'new_v_attn_q_norm', 'new_v_attn_k_norm', 'new_v_ret_decay_logit', 'new_v_ret_norm_gain', 'new_v_w_attn_o', 'new_v_w_ret_o', 'new_v_w_out', 'new_v_mlp_norm', 'new_v_w_up', 'new_v_w_down', 'new_v_ple_norm', 'new_v_w_ple_gate', 'new_v_w_ple', 'new_v_final_norm']
TWIN_LEAF_KINDS = {'loss': 'loss', 'grad_x': 'grad_x', 'grad_mix_norm': 'grad_w', 'grad_w_in': 'grad_w', 'grad_attn_q_norm': 'grad_w', 'grad_attn_k_norm': 'grad_w', 'grad_ret_decay_logit': 'grad_w', 'grad_ret_norm_gain': 'grad_w', 'grad_w_attn_o': 'grad_w', 'grad_w_ret_o': 'grad_w', 'grad_w_out': 'grad_w', 'grad_mlp_norm': 'grad_w', 'grad_w_up': 'grad_w', 'grad_w_down': 'grad_w', 'grad_ple_norm': 'grad_w', 'grad_w_ple_gate': 'grad_w', 'grad_w_ple': 'grad_w', 'grad_final_norm': 'grad_w', 'delta_mix_norm': 'delta_w', 'delta_w_in': 'delta_w', 'delta_attn_q_norm': 'delta_w', 'delta_attn_k_norm': 'delta_w', 'delta_ret_decay_logit': 'delta_w', 'delta_ret_norm_gain': 'delta_w', 'delta_w_attn_o': 'delta_w', 'delta_w_ret_o': 'delta_w', 'delta_w_out': 'delta_w', 'delta_mlp_norm': 'delta_w', 'delta_w_up': 'delta_w', 'delta_w_down': 'delta_w', 'delta_ple_norm': 'delta_w', 'delta_w_ple_gate': 'delta_w', 'delta_w_ple': 'delta_w', 'delta_final_norm': 'delta_w', 'new_m_mix_norm': 'new_m', 'new_m_w_in': 'new_m', 'new_m_attn_q_norm': 'new_m', 'new_m_attn_k_norm': 'new_m', 'new_m_ret_decay_logit': 'new_m', 'new_m_ret_norm_gain': 'new_m', 'new_m_w_attn_o': 'new_m', 'new_m_w_ret_o': 'new_m', 'new_m_w_out': 'new_m', 'new_m_mlp_norm': 'new_m', 'new_m_w_up': 'new_m', 'new_m_w_down': 'new_m', 'new_m_ple_norm': 'new_m', 'new_m_w_ple_gate': 'new_m', 'new_m_w_ple': 'new_m', 'new_m_final_norm': 'new_m', 'new_v_mix_norm': 'new_v', 'new_v_w_in': 'new_v', 'new_v_attn_q_norm': 'new_v', 'new_v_attn_k_norm': 'new_v', 'new_v_ret_decay_logit': 'new_v', 'new_v_ret_norm_gain': 'new_v', 'new_v_w_attn_o': 'new_v', 'new_v_w_ret_o': 'new_v', 'new_v_w_out': 'new_v', 'new_v_mlp_norm': 'new_v', 'new_v_w_up': 'new_v', 'new_v_w_down': 'new_v', 'new_v_ple_norm': 'new_v', 'new_v_w_ple_gate': 'new_v', 'new_v_w_ple': 'new_v', 'new_v_final_norm': 'new_v'}


def _forward(args):
    return _fwd_reference(*[args[k] for k in FWD_PARAMS])


def _output_shape():
    def fwd():
        inp = _fwd_setup_inputs(0)
        return _fwd_reference(*[inp[k] for k in FWD_PARAMS])
    out = _jax.eval_shape(fwd)
    return out.shape, out.dtype

N_MICROBATCH = 1
ADAM_LR = 0.001
ADAM_B1 = 0.9
ADAM_B2 = 0.999
ADAM_EPS = 1e-08
ADAM_WD = 0.01
ADAM_STEP = 10
PER_EXAMPLE_BATCH_AXIS = {'x': 0, 'p': 1, 'loss_target': 0}
SHARED_INPUTS = []
_WEIGHT_DTYPES = {'mix_norm': _jnp.float32, 'w_in': _jnp.float32, 'attn_q_norm': _jnp.float32, 'attn_k_norm': _jnp.float32, 'ret_decay_logit': _jnp.float32, 'ret_norm_gain': _jnp.float32, 'w_attn_o': _jnp.float32, 'w_ret_o': _jnp.float32, 'w_out': _jnp.float32, 'mlp_norm': _jnp.float32, 'w_up': _jnp.float32, 'w_down': _jnp.float32, 'ple_norm': _jnp.float32, 'w_ple_gate': _jnp.float32, 'w_ple': _jnp.float32, 'final_norm': _jnp.float32}
MOMENT_SCALE = {'mix_norm': 2.568518e-01, 'w_in': 1.097235e-01, 'attn_q_norm': 7.129062e-02, 'attn_k_norm': 7.744167e-02, 'ret_decay_logit': 8.543493e-01, 'ret_norm_gain': 1.651031e-01, 'w_attn_o': 2.014836e-02, 'w_ret_o': 1.161000e-01, 'w_out': 1.167559e-01, 'mlp_norm': 3.097892e-01, 'w_up': 1.537654e-01, 'w_down': 6.469279e-01, 'ple_norm': 6.801238e-02, 'w_ple_gate': 7.691502e-02, 'w_ple': 1.182855e-01, 'final_norm': 1.290963e+02}


def _to_microbatches(a, axis):
    t = _jnp.moveaxis(a, axis, 0)
    t = t.reshape((N_MICROBATCH, t.shape[0] // N_MICROBATCH) + t.shape[1:])
    return _jnp.moveaxis(t, 1, axis + 1)


def setup_inputs(seed: int = 0) -> dict:
    inp = _fwd_setup_inputs(seed)
    key = _jax.random.fold_in(_jax.random.key(seed), 7919)
    shape, _ = _output_shape()
    out = dict(inp)
    out["loss_target"] = _jax.random.normal(_jax.random.fold_in(key, 0), shape, _jnp.float32)
    for i, name in enumerate(TWIN_WEIGHTS):
        w = inp[name].astype(_jnp.float32)
        if MOMENT_SCALE is None:
            s = _jnp.sqrt(_jnp.mean(_jnp.square(w)) + 1e-30)
        else:
            s = MOMENT_SCALE[name]
        km, kv = _jax.random.split(_jax.random.fold_in(key, i + 1))
        out[name] = w
        out["m_" + name] = s * _jax.random.normal(km, w.shape, _jnp.float32)
        out["v_" + name] = (s * s) * _jax.random.uniform(kv, w.shape, _jnp.float32, 0.5, 1.5)
    if N_MICROBATCH > 1:
        for name, axis in PER_EXAMPLE_BATCH_AXIS.items():
            out[name] = _to_microbatches(out[name], axis)
    return {'x': out['x'], 'p': out['p'], 'mix_norm': out['mix_norm'], 'w_in': out['w_in'], 'attn_q_norm': out['attn_q_norm'], 'attn_k_norm': out['attn_k_norm'], 'ret_decay_logit': out['ret_decay_logit'], 'ret_norm_gain': out['ret_norm_gain'], 'w_attn_o': out['w_attn_o'], 'w_ret_o': out['w_ret_o'], 'w_out': out['w_out'], 'mlp_norm': out['mlp_norm'], 'w_up': out['w_up'], 'w_down': out['w_down'], 'ple_norm': out['ple_norm'], 'w_ple_gate': out['w_ple_gate'], 'w_ple': out['w_ple'], 'final_norm': out['final_norm'], 'loss_target': out['loss_target'], 'm_mix_norm': out['m_mix_norm'], 'm_w_in': out['m_w_in'], 'm_attn_q_norm': out['m_attn_q_norm'], 'm_attn_k_norm': out['m_attn_k_norm'], 'm_ret_decay_logit': out['m_ret_decay_logit'], 'm_ret_norm_gain': out['m_ret_norm_gain'], 'm_w_attn_o': out['m_w_attn_o'], 'm_w_ret_o': out['m_w_ret_o'], 'm_w_out': out['m_w_out'], 'm_mlp_norm': out['m_mlp_norm'], 'm_w_up': out['m_w_up'], 'm_w_down': out['m_w_down'], 'm_ple_norm': out['m_ple_norm'], 'm_w_ple_gate': out['m_w_ple_gate'], 'm_w_ple': out['m_w_ple'], 'm_final_norm': out['m_final_norm'], 'v_mix_norm': out['v_mix_norm'], 'v_w_in': out['v_w_in'], 'v_attn_q_norm': out['v_attn_q_norm'], 'v_attn_k_norm': out['v_attn_k_norm'], 'v_ret_decay_logit': out['v_ret_decay_logit'], 'v_ret_norm_gain': out['v_ret_norm_gain'], 'v_w_attn_o': out['v_w_attn_o'], 'v_w_ret_o': out['v_w_ret_o'], 'v_w_out': out['v_w_out'], 'v_mlp_norm': out['v_mlp_norm'], 'v_w_up': out['v_w_up'], 'v_w_down': out['v_w_down'], 'v_ple_norm': out['v_ple_norm'], 'v_w_ple_gate': out['v_w_ple_gate'], 'v_w_ple': out['v_w_ple'], 'v_final_norm': out['v_final_norm']}


def _loss(weights, diff, rest, loss_target):
    with _jax.named_scope("forward"):
        args = {**rest, TWIN_DIFF_INPUT: diff, **{k: w.astype(_WEIGHT_DTYPES[k]) for k, w in weights.items()}}
        y = _forward(args)
    with _jax.named_scope("loss_head"):
        err = _jnp.square(y.astype(_jnp.float32) - loss_target)
        return 0.5 * _jnp.sum(_jnp.mean(err, axis=-1)) if err.ndim else 0.5 * err


def _adamw(w, g, m, v):
    m = ADAM_B1 * m + (1.0 - ADAM_B1) * g
    v = ADAM_B2 * v + (1.0 - ADAM_B2) * _jnp.square(g)
    m_hat = m / (1.0 - ADAM_B1 ** ADAM_STEP)
    v_hat = v / (1.0 - ADAM_B2 ** ADAM_STEP)
    delta = -ADAM_LR * (m_hat / (_jnp.sqrt(v_hat) + ADAM_EPS) + ADAM_WD * w)
    return delta, m, v


def reference(x, p, mix_norm, w_in, attn_q_norm, attn_k_norm, ret_decay_logit, ret_norm_gain, w_attn_o, w_ret_o, w_out, mlp_norm, w_up, w_down, ple_norm, w_ple_gate, w_ple, final_norm, loss_target, m_mix_norm, m_w_in, m_attn_q_norm, m_attn_k_norm, m_ret_decay_logit, m_ret_norm_gain, m_w_attn_o, m_w_ret_o, m_w_out, m_mlp_norm, m_w_up, m_w_down, m_ple_norm, m_w_ple_gate, m_w_ple, m_final_norm, v_mix_norm, v_w_in, v_attn_q_norm, v_attn_k_norm, v_ret_decay_logit, v_ret_norm_gain, v_w_attn_o, v_w_ret_o, v_w_out, v_mlp_norm, v_w_up, v_w_down, v_ple_norm, v_w_ple_gate, v_w_ple, v_final_norm):
    given = dict(x=x, p=p, mix_norm=mix_norm, w_in=w_in, attn_q_norm=attn_q_norm, attn_k_norm=attn_k_norm, ret_decay_logit=ret_decay_logit, ret_norm_gain=ret_norm_gain, w_attn_o=w_attn_o, w_ret_o=w_ret_o, w_out=w_out, mlp_norm=mlp_norm, w_up=w_up, w_down=w_down, ple_norm=ple_norm, w_ple_gate=w_ple_gate, w_ple=w_ple, final_norm=final_norm, loss_target=loss_target, m_mix_norm=m_mix_norm, m_w_in=m_w_in, m_attn_q_norm=m_attn_q_norm, m_attn_k_norm=m_attn_k_norm, m_ret_decay_logit=m_ret_decay_logit, m_ret_norm_gain=m_ret_norm_gain, m_w_attn_o=m_w_attn_o, m_w_ret_o=m_w_ret_o, m_w_out=m_w_out, m_mlp_norm=m_mlp_norm, m_w_up=m_w_up, m_w_down=m_w_down, m_ple_norm=m_ple_norm, m_w_ple_gate=m_w_ple_gate, m_w_ple=m_w_ple, m_final_norm=m_final_norm, v_mix_norm=v_mix_norm, v_w_in=v_w_in, v_attn_q_norm=v_attn_q_norm, v_attn_k_norm=v_attn_k_norm, v_ret_decay_logit=v_ret_decay_logit, v_ret_norm_gain=v_ret_norm_gain, v_w_attn_o=v_w_attn_o, v_w_ret_o=v_w_ret_o, v_w_out=v_w_out, v_mlp_norm=v_mlp_norm, v_w_up=v_w_up, v_w_down=v_w_down, v_ple_norm=v_ple_norm, v_w_ple_gate=v_w_ple_gate, v_w_ple=v_w_ple, v_final_norm=v_final_norm)
    weights = {n: given[n] for n in TWIN_WEIGHTS}
    shared = {n: given[n] for n in SHARED_INPUTS}
    per_example = {n: given[n] for n in ['x', 'p']}
    grad_fn = _jax.value_and_grad(_loss, argnums=(0, 1))

    def one_microbatch(ex, loss_target):
        ex = dict(ex)
        diff = ex.pop(TWIN_DIFF_INPUT)
        return grad_fn(weights, diff, {**shared, **ex}, loss_target)

    if N_MICROBATCH == 1:
        loss, (grad_w, grad_x) = one_microbatch(per_example, given["loss_target"])
    else:
        def body(carry, xs):
            loss_sum, grad_sum = carry
            l_k, (gw_k, gx_k) = one_microbatch(xs[0], xs[1])
            with _jax.named_scope("update"):
                return (loss_sum + l_k, _jax.tree.map(_jnp.add, grad_sum, gw_k)), gx_k

        init = (_jnp.zeros((), _jnp.float32), _jax.tree.map(_jnp.zeros_like, weights))
        (loss, grad_w), grad_x = _jax.lax.scan(body, init, (per_example, given["loss_target"]))
    with _jax.named_scope("update"):
        delta_w, new_m, new_v = {}, {}, {}
        for n in TWIN_WEIGHTS:
            delta_w[n], new_m[n], new_v[n] = _adamw(weights[n], grad_w[n], given["m_" + n], given["v_" + n])
    return (loss, grad_x, *[grad_w[n] for n in TWIN_WEIGHTS], *[delta_w[n] for n in TWIN_WEIGHTS],
            *[new_m[n] for n in TWIN_WEIGHTS], *[new_v[n] for n in TWIN_WEIGHTS])
```

```python
import functools
import math

import jax
import jax.numpy as jnp
from jax import lax
from jax.experimental import pallas as pl
from jax.experimental.pallas import tpu as pltpu

F32 = jnp.float32
BF16 = jnp.bfloat16
MESH = pl.DeviceIdType.MESH

D_MODEL = 1024
PLE_DIM = 256
GRID_W = 64
ATTN_HEAD_DIM = 64
ATTN_HEADS = 8
ATTN_KV_HEADS = 2
ATTN_GROUP = ATTN_HEADS // ATTN_KV_HEADS
RET_HEAD_DIM = 128
RET_HEADS = 4
ATTN_Q_W = 512
ATTN_KV_W = 128
RET_W = 512
IN_W = 4864
D_FF = 4096
RET_CHUNK = 128
ROPE_THETA = 10000.0
NORM_EPS = 1e-6
GN_EPS = 1e-5
ATTN_SCALE = ATTN_HEAD_DIM ** -0.5
RET_SCALE = RET_HEAD_DIM ** -0.5

C_AQ, C_AK, C_AV, C_RQ, C_RK, C_RV, C_RG, C_GA, C_GR = 0, 512, 640, 768, 1280, 1792, 2304, 2816, 3840

ADAM_LR = 0.001
ADAM_B1 = 0.9
ADAM_B2 = 0.999
ADAM_EPS = 1e-08
ADAM_WD = 0.01
ADAM_STEP = 10

LANES = 128
VMEM_LIMIT = 56 << 20
SEQ_TILE = 512

BIG = (("w_in", 1), ("w_attn_o", 1), ("w_ret_o", 1), ("w_out", 0), ("w_up", 1), ("w_down", 0), ("w_ple_gate", 0), ("w_ple", 1))
SMALL_ROWS = 8
SMALL = {"mix_norm": (0, 0, 1024), "mlp_norm": (1, 0, 1024), "ple_norm": (2, 0, 1024), "final_norm": (3, 0, 1024),
         "ret_norm_gain": (4, 0, 512), "attn_q_norm": (4, 512, 64), "attn_k_norm": (4, 576, 64), "ret_decay_logit": (4, 640, 8)}


def _cparams(sem=None, vmem=VMEM_LIMIT):
    return pltpu.CompilerParams(dimension_semantics=sem, vmem_limit_bytes=vmem)


def _mm(name, a, b, *, ta=False, tb=False, tm, tn, tk, out_dtypes=(F32,), epi=None, epi_ins=(), consts=()):
    if ta:
        kdim, m = a.shape
    else:
        m, kdim = a.shape
    n = b.shape[0] if tb else b.shape[1]
    tm, tn, tk = min(tm, m), min(tn, n), min(tk, kdim)
    assert m % tm == 0 and n % tn == 0 and kdim % tk == 0, (name, m, n, kdim, tm, tn, tk)
    nk = kdim // tk
    n_e, n_c, n_o = len(epi_ins), len(consts), len(out_dtypes)

    def body(*refs):
        a_ref, b_ref = refs[0], refs[1]
        e_refs = refs[2:2 + n_e]
        c_refs = refs[2 + n_e:2 + n_e + n_c]
        o_refs = refs[2 + n_e + n_c:2 + n_e + n_c + n_o]
        acc_ref = refs[2 + n_e + n_c + n_o] if nk > 1 else None
        k = pl.program_id(2)
        av = a_ref[...].astype(BF16)
        bv = b_ref[...].astype(BF16)
        dims = (((0,) if ta else (1,), (1,) if tb else (0,)), ((), ()))
        part = lax.dot_general(av, bv, dims, preferred_element_type=F32)

        def finish(acc):
            vals = epi(acc, e_refs, c_refs) if epi is not None else (acc,)
            for o_ref, v in zip(o_refs, vals):
                o_ref[...] = v.astype(o_ref.dtype)

        if nk == 1:
            finish(part)
        else:
            @pl.when(k == 0)
            def _():
                acc_ref[...] = part

            @pl.when(k > 0)
            def _():
                acc_ref[...] += part

            @pl.when(k == nk - 1)
            def _():
                finish(acc_ref[...])

    a_spec = pl.BlockSpec((tk, tm), lambda i, j, k: (k, i)) if ta else pl.BlockSpec((tm, tk), lambda i, j, k: (i, k))
    b_spec = pl.BlockSpec((tn, tk), lambda i, j, k: (j, k)) if tb else pl.BlockSpec((tk, tn), lambda i, j, k: (k, j))
    o_spec = pl.BlockSpec((tm, tn), lambda i, j, k: (i, j))
    c_specs = [pl.BlockSpec(c.shape, lambda i, j, k, nd=c.ndim: (0,) * nd) for c in consts]
    outs = pl.pallas_call(
        body, name=name,
        grid=(m // tm, n // tn, nk),
        in_specs=[a_spec, b_spec] + [o_spec] * n_e + c_specs,
        out_specs=[o_spec] * n_o,
        out_shape=[jax.ShapeDtypeStruct((m, n), dt) for dt in out_dtypes],
        scratch_shapes=[pltpu.VMEM((tm, tn), F32)] if nk > 1 else [],
        compiler_params=_cparams(("parallel", "parallel", "arbitrary")),
    )(a, b, *epi_ins, *consts)
    return outs[0] if n_o == 1 else outs


def _rows(arr, tr):
    return (arr, pl.BlockSpec((tr, arr.shape[1]), lambda i: (i, 0)))


def _win(arr, tr, start, width):
    bw = math.gcd(start, width) if start else width
    assert bw % LANES == 0
    return [(arr, pl.BlockSpec((tr, bw), lambda i, cb=start // bw + p: (i, cb))) for p in range(width // bw)]


def _ct(arr):
    return (arr, pl.BlockSpec((None,) + arr.shape[1:], lambda i: (i, 0, 0)))


def _whole(arr):
    return (arr, pl.BlockSpec(arr.shape, lambda i, nd=arr.ndim: (0,) * nd))


def _cat(refs):
    vals = [r[...] for r in refs]
    return vals[0] if len(vals) == 1 else jnp.concatenate(vals, axis=1)


def _seqtiled(name, fn, n_tiles, ins, outs, acc_widths=()):
    n_i, n_o, n_a = len(ins), len(outs), len(acc_widths)

    def body(*refs):
        i_refs, o_refs, a_refs = refs[:n_i], refs[n_i:n_i + n_o], refs[n_i + n_o:]
        if n_a:
            @pl.when(pl.program_id(0) == 0)
            def _():
                for r in a_refs:
                    r[...] = jnp.zeros(r.shape, F32)
        fn(list(i_refs), list(o_refs), list(a_refs))

    res = pl.pallas_call(
        body, name=name, grid=(n_tiles,),
        in_specs=[s for _, s in ins],
        out_specs=[s for _, _, s in outs] + [pl.BlockSpec((8, w), lambda i: (0, 0)) for w in acc_widths],
        out_shape=[jax.ShapeDtypeStruct(sh, dt) for sh, dt, _ in outs] + [jax.ShapeDtypeStruct((8, w), F32) for w in acc_widths],
        compiler_params=_cparams(("arbitrary",)),
    )(*[a for a, _ in ins])
    return res


def _acc_add(acc_ref, val):
    acc_ref[0:1, :] += jnp.sum(val, axis=0, keepdims=True)


def _out_rows(s, w, dt, tr):
    return ((s, w), dt, pl.BlockSpec((tr, w), lambda i: (i, 0)))


def _out_ct(s, w, dt, t):
    return ((s // t, w, t), dt, pl.BlockSpec((None, w, t), lambda i: (i, 0, 0)))


def _rms_fwd(x, gain):
    r = lax.rsqrt(jnp.mean(x * x, axis=-1, keepdims=True) + NORM_EPS)
    return x * r * gain


def _rms_bwd(dy, x, gain):
    r = lax.rsqrt(jnp.mean(x * x, axis=-1, keepdims=True) + NORM_EPS)
    xn = x * r
    dyg = dy * gain
    dx = r * (dyg - xn * jnp.mean(dyg * xn, axis=-1, keepdims=True))
    return dx, dy * xn


def _seg_mean(y, hd):
    w = y.shape[1]
    pieces = []
    for s in range(0, w, LANES):
        v = y[:, s:s + LANES]
        tot = jnp.sum(v, axis=1, keepdims=True)
        if hd == LANES:
            pieces.append(jnp.broadcast_to(tot, v.shape))
        else:
            low = lax.broadcasted_iota(jnp.int32, v.shape, 1) < hd
            lo = jnp.sum(jnp.where(low, v, 0.0), axis=1, keepdims=True)
            pieces.append(jnp.where(low, lo, tot - lo))
    out = pieces[0] if len(pieces) == 1 else jnp.concatenate(pieces, axis=1)
    return out * (1.0 / hd)


def _tile_lanes(t, w):
    return t if w == t.shape[1] else jnp.concatenate([t] * (w // t.shape[1]), axis=1)


def _swap_halves(x, hd):
    w = x.shape[1]
    half = hd // 2
    lane = lax.broadcasted_iota(jnp.int32, x.shape, 1)
    return jnp.where((lane % hd) < half, pltpu.roll(x, w - half, 1), pltpu.roll(x, half, 1))


def _rope(x, cos, sin_signed, hd):
    w = x.shape[1]
    return x * _tile_lanes(cos, w) + _swap_halves(x, hd) * _tile_lanes(sin_signed, w)


def _rope_t(dy, cos, sin_signed, hd):
    w = dy.shape[1]
    return dy * _tile_lanes(cos, w) + _swap_halves(dy * _tile_lanes(sin_signed, w), hd)


def _headnorm_fwd(x, gain_w, hd):
    r = lax.rsqrt(_seg_mean(x * x, hd) + NORM_EPS)
    return x * r * gain_w


def _headnorm_bwd(dy, x, gain_w, hd):
    r = lax.rsqrt(_seg_mean(x * x, hd) + NORM_EPS)
    xn = x * r
    dyg = dy * gain_w
    return r * (dyg - xn * _seg_mean(dyg * xn, hd)), dy * xn


def _sigmoid(x):
    return 1.0 / (1.0 + jnp.exp(-x))


def _rope_tables(seq_len, head_dim):
    rows = seq_len // GRID_W
    n_axis = head_dim // 4
    freqs = ROPE_THETA ** (-jnp.arange(n_axis, dtype=F32) / n_axis)
    row = jnp.repeat(jnp.arange(rows, dtype=F32), GRID_W)
    col = jnp.tile(jnp.arange(GRID_W, dtype=F32), rows)
    ang = jnp.concatenate([row[:, None] * freqs, col[:, None] * freqs], axis=-1)
    cos, sin = jnp.cos(ang), jnp.sin(ang)
    reps = LANES // head_dim
    return jnp.tile(jnp.concatenate([cos, cos], axis=-1), (1, reps)), jnp.tile(jnp.concatenate([-sin, sin], axis=-1), (1, reps))


def _stage_norm_in(x, gain):
    s = x.shape[0]
    tr = min(SEQ_TILE, s)

    def fn(i, o, a):
        o[0][...] = _rms_fwd(i[0][...], i[1][...]).astype(BF16)

    return _seqtiled("norm_in", fn, s // tr, [_rows(x, tr), _whole(gain)], [_out_rows(s, D_MODEL, BF16, tr)])[0]


def _stage_qkv(proj, tabs, gq_w, gk_w):
    s = proj.shape[0]
    t = min(SEQ_TILE, s)
    ca, sa, cr, sr = tabs
    ins = (_win(proj, t, C_AQ, ATTN_Q_W) + _win(proj, t, C_AK, ATTN_KV_W) + _win(proj, t, C_AV, ATTN_KV_W)
           + _win(proj, t, C_RQ, RET_W) + _win(proj, t, C_RK, RET_W)
           + [_rows(ca, t), _rows(sa, t), _rows(cr, t), _rows(sr, t), _whole(gq_w), _whole(gk_w)])

    def fn(i, o, a):
        aq, ak, av = i[0][...], i[1][...], i[2][...]
        rq, rk = _cat(i[3:5]), _cat(i[5:7])
        ca_, sa_, cr_, sr_ = i[7][...], i[8][...], i[9][...], i[10][...]
        qr = _rope(_headnorm_fwd(aq, i[11][...], ATTN_HEAD_DIM), ca_, sa_, ATTN_HEAD_DIM) * ATTN_SCALE
        kr = _rope(_headnorm_fwd(ak, i[12][...], ATTN_HEAD_DIM), ca_, sa_, ATTN_HEAD_DIM)
        qt = qr.T.astype(BF16)
        zeros = jnp.zeros((ATTN_HEAD_DIM, t), BF16)
        for h in range(ATTN_HEADS):
            g = h // ATTN_GROUP
            blk = qt[h * ATTN_HEAD_DIM:(h + 1) * ATTN_HEAD_DIM, :]
            o[0][h * LANES + g * ATTN_HEAD_DIM:h * LANES + (g + 1) * ATTN_HEAD_DIM, :] = blk
            o[0][h * LANES + (1 - g) * ATTN_HEAD_DIM:h * LANES + (2 - g) * ATTN_HEAD_DIM, :] = zeros
        o[1][...] = kr.astype(BF16)
        o[2][...] = kr.T.astype(BF16)
        o[3][...] = av.astype(BF16)
        o[4][...] = av.T.astype(BF16)
        o[5][...] = _rope(rq, cr_, sr_, RET_HEAD_DIM) * RET_SCALE
        o[6][...] = _rope(rk, cr_, sr_, RET_HEAD_DIM)

    outs = [_out_ct(s, ATTN_HEADS * LANES, BF16, t), _out_rows(s, ATTN_KV_W, BF16, t), _out_ct(s, ATTN_KV_W, BF16, t),
            _out_rows(s, ATTN_KV_W, BF16, t), _out_ct(s, ATTN_KV_W, BF16, t), _out_rows(s, RET_W, F32, t), _out_rows(s, RET_W, F32, t)]
    return _seqtiled("qkv_prep", fn, s // t, ins, outs)


def _groupnorm_gate(ry, rg, gain):
    mu = _seg_mean(ry, RET_HEAD_DIM)
    d = ry - mu
    rs = lax.rsqrt(_seg_mean(d * d, RET_HEAD_DIM) + GN_EPS)
    return d * rs, rs, _sigmoid(rg)


def _stage_mix_post(ry_f, ry_b, proj, o_ct, gain):
    s = proj.shape[0]
    t = min(SEQ_TILE, s)
    ins = [_rows(ry_f, t), _rows(ry_b, t)] + _win(proj, t, C_RG, RET_W) + [_ct(o_ct), _whole(gain)]

    def fn(i, o, a):
        ry = i[0][...] + i[1][...]
        rg = _cat(i[2:4])
        gn, _, sg = _groupnorm_gate(ry, rg, None)
        o[0][...] = (gn * i[5][...] * (rg * sg)).astype(BF16)
        o[1][...] = i[4][...].astype(F32).T.astype(BF16)

    return _seqtiled("mix_post", fn, s // t, ins, [_out_rows(s, RET_W, BF16, t), _out_rows(s, ATTN_Q_W, BF16, t)])


def _stage_merge(proj, a_out, r_out):
    s = proj.shape[0]
    tr = min(SEQ_TILE, s)
    ins = _win(proj, tr, C_GA, D_MODEL) + _win(proj, tr, C_GR, D_MODEL) + [_rows(a_out, tr), _rows(r_out, tr)]
    na = len(_win(proj, tr, C_GA, D_MODEL))

    def fn(i, o, a):
        ga, gr = _cat(i[:na]), _cat(i[na:2 * na])
        o[0][...] = (_sigmoid(ga) * i[2 * na][...] + _sigmoid(gr) * i[2 * na + 1][...]).astype(BF16)

    return _seqtiled("merge", fn, s // tr, ins, [_out_rows(s, D_MODEL, BF16, tr)])[0]


def _stage_head(zg, pe, x2, target, g_final):
    s = x2.shape[0]
    tr = min(SEQ_TILE // 2, s)
    ins = [_rows(zg, tr), _rows(pe, tr), _rows(x2, tr), _rows(target, tr), _whole(g_final)]

    def fn(i, o, a):
        gt = _sigmoid(i[0][...])
        pe_ = i[1][...]
        x3 = i[2][...] + gt * pe_
        gf = i[4][...]
        r3 = lax.rsqrt(jnp.mean(x3 * x3, axis=-1, keepdims=True) + NORM_EPS)
        x3n = x3 * r3
        e = x3n * gf - i[3][...]
        _acc_add(a[0], e * e)
        dy = e * (1.0 / D_MODEL)
        _acc_add(a[1], dy * x3n)
        dyg = dy * gf
        dx3 = r3 * (dyg - x3n * jnp.mean(dyg * x3n, axis=-1, keepdims=True))
        o[0][...] = dx3
        o[1][...] = (dx3 * pe_ * gt * (1.0 - gt)).astype(BF16)
        o[2][...] = (dx3 * gt).astype(BF16)

    outs = [_out_rows(s, D_MODEL, F32, tr), _out_rows(s, D_MODEL, BF16, tr), _out_rows(s, D_MODEL, BF16, tr)]
    return _seqtiled("head", fn, s // tr, ins, outs, acc_widths=(D_MODEL, D_MODEL))


def _stage_norm_bwd(name, dh, x, dres, gain):
    s = x.shape[0]
    tr = min(SEQ_TILE // 2, s)

    def fn(i, o, a):
        dx, dg = _rms_bwd(i[0][...], i[1][...], i[3][...])
        o[0][...] = i[2][...] + dx
        _acc_add(a[0], dg)

    return _seqtiled(name, fn, s // tr, [_rows(dh, tr), _rows(x, tr), _rows(dres, tr), _whole(gain)],
                     [_out_rows(s, D_MODEL, F32, tr)], acc_widths=(D_MODEL,))


def _stage_merge_bwd(proj, dmerged, a_out, r_out):
    s = proj.shape[0]
    tr = min(SEQ_TILE // 2, s)
    wins = _win(proj, tr, C_GA, D_MODEL)
    na = len(wins)
    ins = wins + _win(proj, tr, C_GR, D_MODEL) + [_rows(dmerged, tr), _rows(a_out, tr), _rows(r_out, tr)]

    def fn(i, o, a):
        sa, sr = _sigmoid(_cat(i[:na])), _sigmoid(_cat(i[na:2 * na]))
        dm = i[2 * na][...]
        o[0][...] = (dm * sa).astype(BF16)
        o[1][...] = (dm * sr).astype(BF16)
        o[2][...] = (dm * i[2 * na + 1][...] * sa * (1.0 - sa)).astype(BF16)
        o[3][...] = (dm * i[2 * na + 2][...] * sr * (1.0 - sr)).astype(BF16)

    return _seqtiled("merge_bwd", fn, s // tr, ins, [_out_rows(s, D_MODEL, BF16, tr)] * 4)


def _stage_mix_post_bwd(dattn, attn_rows, drz, ry_f, ry_b, proj, gain):
    s = proj.shape[0]
    t = min(SEQ_TILE, s)
    ins = ([_rows(dattn, t), _rows(attn_rows, t), _rows(drz, t), _rows(ry_f, t), _rows(ry_b, t)]
           + _win(proj, t, C_RG, RET_W) + [_whole(gain)])

    def fn(i, o, a):
        da = i[0][...]
        dat = da.T
        prod_t = (da * i[1][...].astype(F32)).T
        dat_b = dat.astype(BF16)
        zeros = jnp.zeros((ATTN_HEAD_DIM, t), BF16)
        for h in range(ATTN_HEADS):
            g = h // ATTN_GROUP
            o[0][h * LANES + g * ATTN_HEAD_DIM:h * LANES + (g + 1) * ATTN_HEAD_DIM, :] = dat_b[h * ATTN_HEAD_DIM:(h + 1) * ATTN_HEAD_DIM, :]
            o[0][h * LANES + (1 - g) * ATTN_HEAD_DIM:h * LANES + (2 - g) * ATTN_HEAD_DIM, :] = zeros
            o[1][h] = jnp.sum(prod_t[h * ATTN_HEAD_DIM:(h + 1) * ATTN_HEAD_DIM, :], axis=0, keepdims=True)
        ry = i[3][...] + i[4][...]
        rg = _cat(i[5:7])
        gain_ = i[7][...]
        gn, rs, sg = _groupnorm_gate(ry, rg, None)
        dz = i[2][...]
        silu = rg * sg
        _acc_add(a[0], dz * gn * silu)
        dgn = dz * gain_ * silu
        o[2][...] = rs * (dgn - _seg_mean(dgn, RET_HEAD_DIM) - gn * _seg_mean(dgn * gn, RET_HEAD_DIM))
        o[3][...] = (dz * gn * gain_ * (sg * (1.0 + rg * (1.0 - sg)))).astype(BF16)

    outs = [_out_ct(s, ATTN_HEADS * LANES, BF16, t),
            ((ATTN_HEADS, s // t, 1, t), F32, pl.BlockSpec((ATTN_HEADS, None, 1, t), lambda i: (0, i, 0, 0))),
            _out_rows(s, RET_W, F32, t), _out_rows(s, RET_W, BF16, t)]
    return _seqtiled("mix_post_bwd", fn, s // t, ins, outs, acc_widths=(RET_W,))


def _stage_dproj(proj, dq_ct, dk8, dv8, rgrads, drg, dga, dgr, tabs, gq_w, gk_w):
    s = proj.shape[0]
    t = min(SEQ_TILE, s)
    ca, sa, cr, sr = tabs
    kv8 = pl.BlockSpec((ATTN_HEADS, t, ATTN_KV_W), lambda i: (0, i, 0))
    ins = (_win(proj, t, C_AQ, ATTN_Q_W) + _win(proj, t, C_AK, ATTN_KV_W) + [_ct(dq_ct), (dk8, kv8), (dv8, kv8)]
           + [_rows(g, t) for g in rgrads] + [_rows(drg, t), _rows(dga, t), _rows(dgr, t)]
           + [_rows(ca, t), _rows(sa, t), _rows(cr, t), _rows(sr, t), _whole(gq_w), _whole(gk_w)])

    def fn(i, o, a):
        aq, ak = i[0][...], i[1][...]
        dq_f, dk_f, dv_f, dq_b, dk_b, dv_b = (r[...] for r in i[5:11])
        ca_, sa_, cr_, sr_ = i[14][...], i[15][...], i[16][...], i[17][...]
        dqn = _rope_t(i[2][...].T * ATTN_SCALE, ca_, sa_, ATTN_HEAD_DIM)
        daq, gq_rows = _headnorm_bwd(dqn, aq, i[18][...], ATTN_HEAD_DIM)
        dkn = _rope_t(jnp.sum(i[3][...], axis=0), ca_, sa_, ATTN_HEAD_DIM)
        dak, gk_rows = _headnorm_bwd(dkn, ak, i[19][...], ATTN_HEAD_DIM)
        _acc_add(a[0], gq_rows)
        _acc_add(a[1], gk_rows)
        out = o[0]
        out[:, C_AQ:C_AQ + ATTN_Q_W] = daq.astype(BF16)
        out[:, C_AK:C_AK + ATTN_KV_W] = dak.astype(BF16)
        out[:, C_AV:C_AV + ATTN_KV_W] = jnp.sum(i[4][...], axis=0).astype(BF16)
        out[:, C_RQ:C_RQ + RET_W] = _rope_t((dq_f + dq_b) * RET_SCALE, cr_, sr_, RET_HEAD_DIM).astype(BF16)
        out[:, C_RK:C_RK + RET_W] = _rope_t(dk_f + dk_b, cr_, sr_, RET_HEAD_DIM).astype(BF16)
        out[:, C_RV:C_RV + RET_W] = (dv_f + dv_b).astype(BF16)
        out[:, C_RG:C_RG + RET_W] = i[11][...]
        out[:, C_GA:C_GA + D_MODEL] = i[12][...]
        out[:, C_GR:C_GR + D_MODEL] = i[13][...]

    return _seqtiled("dproj", fn, s // t, ins, [_out_rows(s, IN_W, BF16, t)], acc_widths=(ATTN_Q_W, ATTN_KV_W))


def _attn_fwd(q_ct, k_rows, v_ct):
    nq, _, t = q_ct.shape
    s = nq * t
    nk = nq

    def body(q_ref, k_ref, v_ref, o_ref, lse_ref):
        qt = q_ref[...]

        def step(j, carry):
            m, l, acc = carry
            kj = k_ref[pl.ds(pl.multiple_of(j * t, t), t), :]
            st = jnp.dot(kj, qt, preferred_element_type=F32)
            m_new = jnp.maximum(m, jnp.max(st, axis=0, keepdims=True))
            alpha = jnp.exp(m - m_new)
            pt = jnp.exp(st - m_new)
            l = alpha * l + jnp.sum(pt, axis=0, keepdims=True)
            acc = alpha * acc + jnp.dot(v_ref[j], pt.astype(BF16), preferred_element_type=F32)
            return m_new, l, acc

        init = (jnp.full((1, t), -1e30, F32), jnp.zeros((1, t), F32), jnp.zeros((ATTN_HEAD_DIM, t), F32))
        m, l, acc = lax.fori_loop(0, nk, step, init)
        o_ref[...] = (acc / l).astype(BF16)
        lse_ref[...] = m + jnp.log(l)

    return pl.pallas_call(
        body, name="attn_fwd", grid=(ATTN_HEADS, nq),
        in_specs=[pl.BlockSpec((None, LANES, t), lambda h, i: (i, h, 0)),
                  pl.BlockSpec((s, ATTN_KV_W), lambda h, i: (0, 0)),
                  pl.BlockSpec((nk, ATTN_HEAD_DIM, t), lambda h, i: (0, h // ATTN_GROUP, 0))],
        out_specs=[pl.BlockSpec((None, ATTN_HEAD_DIM, t), lambda h, i: (i, h, 0)),
                   pl.BlockSpec((None, None, 1, t), lambda h, i: (h, i, 0, 0))],
        out_shape=[jax.ShapeDtypeStruct((nq, ATTN_Q_W, t), BF16), jax.ShapeDtypeStruct((ATTN_HEADS, nq, 1, t), F32)],
        compiler_params=_cparams(("parallel", "parallel")),
    )(q_ct, k_rows, v_ct)


def _attn_bwd(q_ct, do_ct, lse, delta, k_rows, v_rows, k_ct):
    nq, _, t = q_ct.shape
    s = nq * t
    nk = nq

    def body(q_ref, do_ref, lse_ref, delta_ref, k_ref, v_ref, kt_ref, dq_ref, dk_ref, dv_ref, dk_acc, dv_acc):
        j = pl.program_id(1)

        @pl.when(j == 0)
        def _():
            dq_ref[...] = jnp.zeros(dq_ref.shape, F32)

        kj, vj, ktj = k_ref[...], v_ref[...], kt_ref[...]
        dk_acc[...] = jnp.zeros(dk_acc.shape, F32)
        dv_acc[...] = jnp.zeros(dv_acc.shape, F32)
        nt = (((1,), (1,)), ((), ()))

        def step(i, carry):
            qt, dot_ = q_ref[i], do_ref[i]
            st = jnp.dot(kj, qt, preferred_element_type=F32)
            pt = jnp.exp(st - lse_ref[i])
            dpt = jnp.dot(vj, dot_, preferred_element_type=F32)
            dst = (pt * (dpt - delta_ref[i])).astype(BF16)
            dv_acc[...] += lax.dot_general(pt.astype(BF16), dot_, nt, preferred_element_type=F32)
            dk_acc[...] += lax.dot_general(dst, qt, nt, preferred_element_type=F32)
            dq_ref[i] += jnp.dot(ktj, dst, preferred_element_type=F32)
            return carry

        lax.fori_loop(0, nq, step, 0)
        dk_ref[...] = dk_acc[...]
        dv_ref[...] = dv_acc[...]

    per_head = pl.BlockSpec((nq, LANES, t), lambda h, j: (0, h, 0))
    stat = pl.BlockSpec((None, nq, 1, t), lambda h, j: (h, 0, 0, 0))
    kv_rows = pl.BlockSpec((t, ATTN_KV_W), lambda h, j: (j, 0))
    kv_out = pl.BlockSpec((None, t, ATTN_KV_W), lambda h, j: (h, j, 0))
    return pl.pallas_call(
        body, name="attn_bwd", grid=(ATTN_HEADS, nk),
        in_specs=[per_head, per_head, stat, stat, kv_rows, kv_rows,
                  pl.BlockSpec((None, ATTN_HEAD_DIM, t), lambda h, j: (j, h // ATTN_GROUP, 0))],
        out_specs=[pl.BlockSpec((nq, ATTN_HEAD_DIM, t), lambda h, j: (0, h, 0)), kv_out, kv_out],
        out_shape=[jax.ShapeDtypeStruct((nq, ATTN_Q_W, t), F32), jax.ShapeDtypeStruct((ATTN_HEADS, s, ATTN_KV_W), F32),
                   jax.ShapeDtypeStruct((ATTN_HEADS, s, ATTN_KV_W), F32)],
        scratch_shapes=[pltpu.VMEM((t, ATTN_KV_W), F32), pltpu.VMEM((t, ATTN_KV_W), F32)],
        compiler_params=_cparams(("parallel", "arbitrary")),
    )(q_ct, do_ct, lse, delta, k_rows, v_rows, k_ct)


def _log_sigmoid(x):
    t = jnp.exp(-jnp.abs(x))
    log1p_t = jnp.where(t < 1e-2, t * (1.0 - t * (0.5 - t * (1.0 / 3.0))), jnp.log(1.0 + t))
    return jnp.minimum(x, 0.0) - log1p_t


def _decay_tables(logit, backward):
    c = RET_CHUNK
    lam = _log_sigmoid(jnp.full((c, c), logit, F32))
    ii = lax.broadcasted_iota(jnp.int32, (c, c), 0).astype(F32)
    jj = lax.broadcasted_iota(jnp.int32, (c, c), 1).astype(F32)
    if not backward:
        dist, dist_t = jnp.maximum(ii - jj, 0.0), jnp.maximum(jj - ii, 0.0)
        mask, mask_t = ii >= jj, jj >= ii
        e_q, e_k = ii + 1.0, (c - 1.0) - ii
    else:
        dist, dist_t = jnp.maximum(jj - ii, 0.0), jnp.maximum(ii - jj, 0.0)
        mask, mask_t = jj > ii, ii > jj
        e_q, e_k = c - ii, ii
    return dict(
        d=jnp.where(mask, jnp.exp(lam * dist), 0.0), d_t=jnp.where(mask_t, jnp.exp(lam * dist_t), 0.0), dist=dist,
        qdec=jnp.exp(lam * e_q), kdec=jnp.exp(lam * e_k), e_q=e_q, e_k=e_k, gam=jnp.exp(lam * c))


def _nt(a, b):
    return lax.dot_general(a, b, (((1,), (1,)), ((), ())), preferred_element_type=F32)


def _ret_fwd(name, logits, q, k, proj, backward):
    s = q.shape[0]
    c = RET_CHUNK
    nc = s // c
    d_ix = 1 if backward else 0
    chunk = (lambda n: nc - 1 - n) if backward else (lambda n: n)
    vwin = _win(proj, c, C_RV, RET_W)
    nv = len(vwin)
    vw = RET_W // nv

    def body(lg_ref, q_ref, k_ref, *rest):
        v_refs, (y_ref, st_ref, state) = rest[:nv], rest[nv:]

        @pl.when(pl.program_id(0) == 0)
        def _():
            state[...] = jnp.zeros(state.shape, F32)

        for h in range(RET_HEADS):
            tb = _decay_tables(lg_ref[d_ix, h], backward)
            sl = slice(h * RET_HEAD_DIM, (h + 1) * RET_HEAD_DIM)
            qh, kh = q_ref[:, sl], k_ref[:, sl]
            off = h * RET_HEAD_DIM
            vh = v_refs[off // vw][:, off % vw:off % vw + RET_HEAD_DIM]
            vb = vh.astype(BF16)
            a = _nt(qh.astype(BF16), kh.astype(BF16)) * tb["d"]
            sh = state[h]
            st_ref[h] = sh
            y_ref[:, sl] = (jnp.dot(a.astype(BF16), vb, preferred_element_type=F32)
                            + jnp.dot((qh * tb["qdec"]).astype(BF16), sh.astype(BF16), preferred_element_type=F32))
            state[h] = tb["gam"] * sh + jnp.dot((kh * tb["kdec"]).T.astype(BF16), vb, preferred_element_type=F32)

    rows = pl.BlockSpec((c, RET_W), lambda n: (chunk(n), 0))
    v_specs = [pl.BlockSpec(sp.block_shape, lambda n, im=sp.index_map: (chunk(n), im(0)[1])) for _, sp in vwin]
    return pl.pallas_call(
        body, name=name, grid=(nc,),
        in_specs=[pl.BlockSpec(memory_space=pltpu.SMEM), rows, rows] + v_specs,
        out_specs=[rows, pl.BlockSpec((None, RET_HEADS, RET_HEAD_DIM, RET_HEAD_DIM), lambda n: (chunk(n), 0, 0, 0))],
        out_shape=[jax.ShapeDtypeStruct((s, RET_W), F32), jax.ShapeDtypeStruct((nc, RET_HEADS, RET_HEAD_DIM, RET_HEAD_DIM), F32)],
        scratch_shapes=[pltpu.VMEM((RET_HEADS, RET_HEAD_DIM, RET_HEAD_DIM), F32)],
        compiler_params=_cparams(("arbitrary",)),
    )(logits, q, k, *[a for a, _ in vwin])


def _ret_bwd(name, logits, q, k, proj, dy, states, backward):
    s = q.shape[0]
    c = RET_CHUNK
    nc = s // c
    d_ix = 1 if backward else 0
    chunk = (lambda n: n) if backward else (lambda n: nc - 1 - n)
    vwin = _win(proj, c, C_RV, RET_W)
    nv = len(vwin)
    vw = RET_W // nv

    def body(lg_ref, q_ref, k_ref, dy_ref, st_ref, *rest):
        v_refs, (dq_ref, dk_ref, dv_ref, dl_ref, dstate, lacc) = rest[:nv], rest[nv:]
        n = pl.program_id(0)

        @pl.when(n == 0)
        def _():
            dstate[...] = jnp.zeros(dstate.shape, F32)
            lacc[...] = jnp.zeros(lacc.shape, F32)

        for h in range(RET_HEADS):
            tb = _decay_tables(lg_ref[d_ix, h], backward)
            sl = slice(h * RET_HEAD_DIM, (h + 1) * RET_HEAD_DIM)
            qh, kh, dyh = q_ref[:, sl], k_ref[:, sl], dy_ref[:, sl]
            off = h * RET_HEAD_DIM
            vh = v_refs[off // vw][:, off % vw:off % vw + RET_HEAD_DIM]
            qb, kb, vb, dyb = qh.astype(BF16), kh.astype(BF16), vh.astype(BF16), dyh.astype(BF16)
            sh, dsh = st_ref[h], dstate[h]
            shb, dshb = sh.astype(BF16), dsh.astype(BF16)
            qk = _nt(qb, kb)
            g = _nt(dyb, vb) * tb["d"]
            a_t = _nt(kb, qb) * tb["d_t"]
            g_t = _nt(vb, dyb) * tb["d_t"]
            qd, kd = qh * tb["qdec"], kh * tb["kdec"]
            dqd = _nt(dyb, shb)
            dkd = _nt(vb, dshb)
            dq_ref[:, sl] = jnp.dot(g.astype(BF16), kb, preferred_element_type=F32) + dqd * tb["qdec"]
            dk_ref[:, sl] = jnp.dot(g_t.astype(BF16), qb, preferred_element_type=F32) + dkd * tb["kdec"]
            dv_ref[:, sl] = (jnp.dot(a_t.astype(BF16), dyb, preferred_element_type=F32)
                             + jnp.dot(kd.astype(BF16), dshb, preferred_element_type=F32))
            lacc[h] += (tb["dist"] * qk * g + tb["e_q"] * qd * dqd + tb["e_k"] * kd * dkd
                        + float(c) * tb["gam"] * dsh * sh)
            dstate[h] = tb["gam"] * dsh + jnp.dot(qd.T.astype(BF16), dyb, preferred_element_type=F32)

        @pl.when(n == nc - 1)
        def _():
            for h in range(RET_HEADS):
                dl_ref[h] = jnp.zeros((8, LANES), F32) + jnp.sum(lacc[h])

    rows = pl.BlockSpec((c, RET_W), lambda n: (chunk(n), 0))
    v_specs = [pl.BlockSpec(sp.block_shape, lambda n, im=sp.index_map: (chunk(n), im(0)[1])) for _, sp in vwin]
    hmat = (RET_HEADS, RET_HEAD_DIM, RET_HEAD_DIM)
    return pl.pallas_call(
        body, name=name, grid=(nc,),
        in_specs=[pl.BlockSpec(memory_space=pltpu.SMEM), rows, rows, rows,
                  pl.BlockSpec((None,) + hmat, lambda n: (chunk(n), 0, 0, 0))] + v_specs,
        out_specs=[rows, rows, rows, pl.BlockSpec((RET_HEADS, 8, LANES), lambda n: (0, 0, 0))],
        out_shape=[jax.ShapeDtypeStruct((s, RET_W), F32)] * 3 + [jax.ShapeDtypeStruct((RET_HEADS, 8, LANES), F32)],
        scratch_shapes=[pltpu.VMEM(hmat, F32), pltpu.VMEM(hmat, F32)],
        compiler_params=_cparams(("arbitrary",)),
    )(logits, q, k, dy, states, *[a for a, _ in vwin])


def _local_step(x, p, target, w, small):
    s = x.shape[0]
    tabs = _rope_tables(s, ATTN_HEAD_DIM) + _rope_tables(s, RET_HEAD_DIM)
    g_mix, g_mlp, g_ple = small["mix_norm"][None, :], small["mlp_norm"][None, :], small["ple_norm"][None, :]
    g_final, g_ret = small["final_norm"][None, :], small["ret_norm_gain"][None, :]
    gq_w = jnp.tile(small["attn_q_norm"], ATTN_HEADS)[None, :]
    gk_w = jnp.tile(small["attn_k_norm"], ATTN_KV_HEADS)[None, :]
    logits = small["ret_decay_logit"]

    hb = _stage_norm_in(x, g_mix)
    proj = _mm("in_proj", hb, w["w_in"], tm=512, tn=IN_W // 2, tk=1024)
    q_ct, k_rows, k_ct, v_rows, v_ct, rq, rk = _stage_qkv(proj, tabs, gq_w, gk_w)
    o_ct, lse = _attn_fwd(q_ct, k_rows, v_ct)
    ry_f, st_f = _ret_fwd("ret_fwd_f", logits, rq, rk, proj, False)
    ry_b, st_b = _ret_fwd("ret_fwd_b", logits, rq, rk, proj, True)
    rz, attn_rows = _stage_mix_post(ry_f, ry_b, proj, o_ct, g_ret)
    a_out = _mm("attn_o", attn_rows, w["w_attn_o"], tm=1024, tn=1024, tk=512)
    r_out = _mm("ret_o", rz, w["w_ret_o"], tm=1024, tn=1024, tk=512)
    merged = _stage_merge(proj, a_out, r_out)

    def epi_res_norm(acc, e, c):
        xr = e[0][...] + acc
        return xr, _rms_fwd(xr, c[0][...])

    x1, hm = _mm("out_proj", merged, w["w_out"], tm=512, tn=1024, tk=1024, out_dtypes=(F32, BF16),
                 epi=epi_res_norm, epi_ins=(x,), consts=(g_mlp,))

    def epi_relu2(acc, e, c):
        r = jnp.maximum(acc, 0.0)
        return acc, r * r

    u, act = _mm("mlp_up", hm, w["w_up"], tm=512, tn=2048, tk=1024, out_dtypes=(F32, BF16), epi=epi_relu2)
    x2, hp = _mm("mlp_down", act, w["w_down"], tm=512, tn=1024, tk=2048, out_dtypes=(F32, BF16),
                 epi=epi_res_norm, epi_ins=(x1,), consts=(g_ple,))
    zg = _mm("ple_gate", hp, w["w_ple_gate"], tm=1024, tn=1024, tk=1024)
    pe = _mm("ple_emb", p, w["w_ple"], tm=1024, tn=1024, tk=256)
    dx3, dzg, dpe, loss_cols, g_final_p = _stage_head(zg, pe, x2, target, g_final)
    loss_sum = 0.5 / D_MODEL * jnp.sum(loss_cols)

    gw = {}
    gw["w_ple"] = _mm("g_w_ple", p, dpe, ta=True, tm=256, tn=1024, tk=2048)
    gw["w_ple_gate"] = _mm("g_w_ple_gate", hp, dzg, ta=True, tm=1024, tn=1024, tk=2048)
    dhp = _mm("d_hp", dzg, w["w_ple_gate"], tb=True, tm=1024, tn=1024, tk=1024)
    dx2, g_ple_p = _stage_norm_bwd("ple_norm_bwd", dhp, x2, dx3, g_ple)

    def epi_relu2_bwd(acc, e, c):
        return (acc * (2.0 * jnp.maximum(e[0][...], 0.0)),)

    du = _mm("d_u", dx2, w["w_down"], tb=True, tm=512, tn=2048, tk=1024, out_dtypes=(BF16,), epi=epi_relu2_bwd, epi_ins=(u,))
    gw["w_down"] = _mm("g_w_down", act, dx2, ta=True, tm=1024, tn=1024, tk=2048)
    gw["w_up"] = _mm("g_w_up", hm, du, ta=True, tm=1024, tn=1024, tk=2048)
    dhm = _mm("d_hm", du, w["w_up"], tb=True, tm=512, tn=1024, tk=2048)
    dx1, g_mlp_p = _stage_norm_bwd("mlp_norm_bwd", dhm, x1, dx2, g_mlp)
    dmerged = _mm("d_merged", dx1, w["w_out"], tb=True, tm=1024, tn=1024, tk=1024)
    gw["w_out"] = _mm("g_w_out", merged, dx1, ta=True, tm=1024, tn=1024, tk=2048)
    dao, dro, dga, dgr = _stage_merge_bwd(proj, dmerged, a_out, r_out)
    gw["w_attn_o"] = _mm("g_w_attn_o", attn_rows, dao, ta=True, tm=512, tn=1024, tk=2048)
    gw["w_ret_o"] = _mm("g_w_ret_o", rz, dro, ta=True, tm=512, tn=1024, tk=2048)
    dattn = _mm("d_attn", dao, w["w_attn_o"], tb=True, tm=1024, tn=512, tk=1024)
    drz = _mm("d_rz", dro, w["w_ret_o"], tb=True, tm=1024, tn=512, tk=1024)
    do_ct, delta, dry, drg, g_ret_p = _stage_mix_post_bwd(dattn, attn_rows, drz, ry_f, ry_b, proj, g_ret)
    dq_f, dk_f, dv_f, dl_f = _ret_bwd("ret_bwd_f", logits, rq, rk, proj, dry, st_f, False)
    dq_b, dk_b, dv_b, dl_b = _ret_bwd("ret_bwd_b", logits, rq, rk, proj, dry, st_b, True)
    dq_ct, dk8, dv8 = _attn_bwd(q_ct, do_ct, lse, delta, k_rows, v_rows, k_ct)
    dproj, gq_p, gk_p = _stage_dproj(proj, dq_ct, dk8, dv8, (dq_f, dk_f, dv_f, dq_b, dk_b, dv_b), drg, dga, dgr, tabs, gq_w, gk_w)
    gw["w_in"] = _mm("g_w_in", hb, dproj, ta=True, tm=512, tn=IN_W // 2, tk=1024)
    dh = _mm("d_h", dproj, w["w_in"], tb=True, tm=512, tn=1024, tk=IN_W // 2)
    grad_x, g_mix_p = _stage_norm_bwd("mix_norm_bwd", dh, x, dx1, g_mix)

    gs = {
        "mix_norm": g_mix_p[0], "mlp_norm": g_mlp_p[0], "ple_norm": g_ple_p[0], "final_norm": g_final_p[0],
        "ret_norm_gain": g_ret_p[0],
        "attn_q_norm": jnp.sum(gq_p[0].reshape(ATTN_HEADS, ATTN_HEAD_DIM), axis=0),
        "attn_k_norm": jnp.sum(gk_p[0].reshape(ATTN_KV_HEADS, ATTN_HEAD_DIM), axis=0),
        "ret_decay_logit": jnp.stack([dl_f[:, 0, 0], dl_b[:, 0, 0]]),
    }
    return loss_sum, grad_x, gw, gs


PACK_COLS = 1024
N_CHIPS = 4
HALF_ROWS = 2048


def _pack_shard(parts):
    return jnp.concatenate([parts[n].reshape(-1, PACK_COLS) for n, _ in BIG], axis=0)


def _unpack_shard(slab, shapes):
    out, r = {}, 0
    for n, _ in BIG:
        rows = math.prod(shapes[n]) // PACK_COLS
        out[n] = slab[r:r + rows].reshape(shapes[n])
        r += rows
    return out


def _shard_of(full, axis, sidx):
    size = full.shape[axis] // N_CHIPS
    return lax.slice_in_dim(full, sidx * size, (sidx + 1) * size, axis=axis)


def _position():
    x, y, c = lax.axis_index("x"), lax.axis_index("y"), lax.axis_index("c")
    return x, y, c


def _other_chips(x, y):
    return [(1 - x, y), (x, 1 - y), (1 - x, 1 - y)]


ANY = pl.BlockSpec(memory_space=pl.ANY)


def _gather_weights(slab):
    rows = slab.shape[0]
    half = rows // 2

    def body(in_ref, out_ref, send_sems, recv_sems, local_sem):
        x, y, c = _position()
        me_chip = 2 * x + y
        chips = _other_chips(x, y)

        def piece(chip, core):
            return out_ref.at[2 * chip[0] + chip[1], pl.ds(core * half, half), :]

        def copy(k, chip, core, to, src=None):
            return pltpu.make_async_remote_copy(
                src_ref=piece(chip, core) if src is None else src, dst_ref=piece(chip, core),
                send_sem=send_sems.at[k], recv_sem=recv_sems.at[k], device_id=to, device_id_type=MESH)

        mine = pltpu.make_async_copy(in_ref, out_ref.at[me_chip], local_sem)
        mine.start()
        first = [copy(j, (x, y), c, (*chip, c), src=in_ref.at[pl.ds(c * half, half), :]) for j, chip in enumerate(chips)]
        for cp in first:
            cp.start()
        passed = [copy(3 + j, chip, c, (x, y, 1 - c)) for j, chip in enumerate(chips)]
        for j, chip in enumerate(chips):
            copy(j, chip, c, (x, y, c)).wait_recv()
            passed[j].start()
        for j, chip in enumerate(chips):
            copy(3 + j, chip, 1 - c, (x, y, c)).wait_recv()
        for cp in first + passed:
            cp.wait_send()
        mine.wait()

    return pl.pallas_call(
        body, name="gather_weights", in_specs=[ANY], out_specs=ANY,
        out_shape=jax.ShapeDtypeStruct((N_CHIPS,) + slab.shape, slab.dtype),
        scratch_shapes=[pltpu.SemaphoreType.DMA((6,)), pltpu.SemaphoreType.DMA((6,)), pltpu.SemaphoreType.DMA],
    )(slab)


def _exchange_halves(g):
    def body(g_ref, out_ref, send_sem, recv_sem):
        x, y, c = _position()
        cp = pltpu.make_async_remote_copy(src_ref=g_ref.at[1 - c], dst_ref=out_ref, send_sem=send_sem, recv_sem=recv_sem,
                                          device_id=(x, y, 1 - c), device_id_type=MESH)
        cp.start()
        cp.wait()

    return pl.pallas_call(
        body, name="exchange_halves", in_specs=[ANY], out_specs=ANY,
        out_shape=jax.ShapeDtypeStruct(g.shape[1:], g.dtype),
        scratch_shapes=[pltpu.SemaphoreType.DMA, pltpu.SemaphoreType.DMA],
    )(g)


def _add_my_half(g, r1, c_idx):
    tr = 256
    nt = g.shape[2] // tr

    def body(c_ref, g_ref, r_ref, o_ref):
        o_ref[...] = g_ref[...] + r_ref[...]

    blk = (None, tr, PACK_COLS)
    return pl.pallas_call(
        body, name="add_my_half",
        grid_spec=pltpu.PrefetchScalarGridSpec(
            num_scalar_prefetch=1, grid=(N_CHIPS, nt),
            in_specs=[pl.BlockSpec((None,) + blk, lambda s, i, c_ref: (c_ref[0], s, i, 0)),
                      pl.BlockSpec(blk, lambda s, i, c_ref: (s, i, 0))],
            out_specs=pl.BlockSpec(blk, lambda s, i, c_ref: (s, i, 0))),
        out_shape=jax.ShapeDtypeStruct(g.shape[1:], F32),
        compiler_params=_cparams(("parallel", "parallel")),
    )(c_idx, g, r1)


def _scatter_to_chips(part):
    def body(p_ref, out_ref, send_sems, recv_sems, local_sem):
        x, y, c = _position()
        me_chip = 2 * x + y
        chips = _other_chips(x, y)
        mine = pltpu.make_async_copy(p_ref.at[me_chip], out_ref.at[me_chip], local_sem)
        mine.start()
        sends = [pltpu.make_async_remote_copy(
            src_ref=p_ref.at[2 * chip[0] + chip[1]], dst_ref=out_ref.at[me_chip], send_sem=send_sems.at[j], recv_sem=recv_sems.at[j],
            device_id=(*chip, c), device_id_type=MESH) for j, chip in enumerate(chips)]
        for cp in sends:
            cp.start()
        for j, chip in enumerate(chips):
            pltpu.make_async_remote_copy(
                src_ref=p_ref.at[me_chip], dst_ref=out_ref.at[2 * chip[0] + chip[1]], send_sem=send_sems.at[j], recv_sem=recv_sems.at[j],
                device_id=(*chip, c), device_id_type=MESH).wait_recv()
        for cp in sends:
            cp.wait_send()
        mine.wait()

    return pl.pallas_call(
        body, name="scatter_to_chips", in_specs=[ANY], out_specs=ANY,
        out_shape=jax.ShapeDtypeStruct(part.shape, part.dtype),
        scratch_shapes=[pltpu.SemaphoreType.DMA((3,)), pltpu.SemaphoreType.DMA((3,)), pltpu.SemaphoreType.DMA],
    )(part)


def _sum_chips(r2):
    tr = 256

    def body(r_ref, o_ref):
        o_ref[...] = ((r_ref[0] + r_ref[1]) + r_ref[2]) + r_ref[3]

    return pl.pallas_call(
        body, name="sum_chips", grid=(r2.shape[1] // tr,),
        in_specs=[pl.BlockSpec((N_CHIPS, tr, PACK_COLS), lambda i: (0, i, 0))],
        out_specs=pl.BlockSpec((tr, PACK_COLS), lambda i: (i, 0)),
        out_shape=jax.ShapeDtypeStruct(r2.shape[1:], F32),
        compiler_params=_cparams(("parallel",)),
    )(r2)


def _join_halves(red):
    def body(r_ref, out_ref, send_sem, recv_sem, local_sem):
        x, y, c = _position()
        mine = pltpu.make_async_copy(r_ref, out_ref.at[c], local_sem)
        mine.start()
        cp = pltpu.make_async_remote_copy(src_ref=r_ref, dst_ref=out_ref.at[c], send_sem=send_sem, recv_sem=recv_sem,
                                          device_id=(x, y, 1 - c), device_id_type=MESH)
        cp.start()
        pltpu.make_async_remote_copy(src_ref=r_ref, dst_ref=out_ref.at[1 - c], send_sem=send_sem, recv_sem=recv_sem,
                                     device_id=(x, y, 1 - c), device_id_type=MESH).wait_recv()
        cp.wait_send()
        mine.wait()

    return pl.pallas_call(
        body, name="join_halves", in_specs=[ANY], out_specs=ANY,
        out_shape=jax.ShapeDtypeStruct((2,) + red.shape, red.dtype),
        scratch_shapes=[pltpu.SemaphoreType.DMA, pltpu.SemaphoreType.DMA, pltpu.SemaphoreType.DMA],
    )(red)


def _adamw_math(w, g, m, v):
    m = ADAM_B1 * m + (1.0 - ADAM_B1) * g
    v = ADAM_B2 * v + (1.0 - ADAM_B2) * (g * g)
    m_hat = m / (1.0 - ADAM_B1 ** ADAM_STEP)
    v_hat = v / (1.0 - ADAM_B2 ** ADAM_STEP)
    delta = -ADAM_LR * (m_hat / (jnp.sqrt(v_hat) + ADAM_EPS) + ADAM_WD * w)
    return delta, m, v


def _adamw(w, g, m, v):
    tr = 256

    def body(w_ref, g_ref, m_ref, v_ref, d_ref, nm_ref, nv_ref):
        d_ref[...], nm_ref[...], nv_ref[...] = _adamw_math(w_ref[...], g_ref[...], m_ref[...], v_ref[...])

    blk = pl.BlockSpec((tr, PACK_COLS), lambda i: (i, 0))
    return pl.pallas_call(
        body, name="adamw", grid=(w.shape[0] // tr,), in_specs=[blk] * 4, out_specs=[blk] * 3,
        out_shape=[jax.ShapeDtypeStruct(w.shape, F32)] * 3, compiler_params=_cparams(("parallel",)),
    )(w, g, m, v)


def _small_step(gpk, wpk, mpk, vpk):
    row, col, width = SMALL["ret_decay_logit"]

    def body(g_ref, w_ref, m_ref, v_ref, og_ref, od_ref, om_ref, ov_ref, gbuf, send_sems, recv_sems):
        x, y, c = _position()
        me = 4 * x + 2 * y + c
        gbuf[me] = g_ref[...]
        sends = []
        for k in range(1, 8):
            to = (x ^ (k >> 2), y ^ ((k >> 1) & 1), c ^ (k & 1))
            cp = pltpu.make_async_remote_copy(src_ref=g_ref, dst_ref=gbuf.at[me], send_sem=send_sems.at[k - 1],
                                              recv_sem=recv_sems.at[k - 1], device_id=to, device_id_type=MESH)
            cp.start()
            sends.append(cp)
        for k in range(1, 8):
            frm = me ^ k
            pltpu.make_async_remote_copy(src_ref=g_ref, dst_ref=gbuf.at[frm], send_sem=send_sems.at[k - 1],
                                         recv_sem=recv_sems.at[k - 1], device_id=(x, y, c), device_id_type=MESH).wait_recv()
        for cp in sends:
            cp.wait_send()
        tot = gbuf[0]
        for d in range(1, 8):
            tot = tot + gbuf[d]
        w = w_ref[...]
        r_i = lax.broadcasted_iota(jnp.int32, w.shape, 0)
        c_i = lax.broadcasted_iota(jnp.int32, w.shape, 1)
        is_logit = (r_i == row) & (c_i >= col) & (c_i < col + width)
        g = jnp.where(is_logit, tot * _sigmoid(-w), tot)
        og_ref[...] = g
        od_ref[...], om_ref[...], ov_ref[...] = _adamw_math(w, g, m_ref[...], v_ref[...])

    vm = pl.BlockSpec(memory_space=pltpu.VMEM)
    shp = jax.ShapeDtypeStruct(gpk.shape, F32)
    return pl.pallas_call(
        body, name="small_step", in_specs=[vm] * 4, out_specs=[vm] * 4, out_shape=[shp] * 4,
        scratch_shapes=[pltpu.VMEM((8,) + gpk.shape, F32), pltpu.SemaphoreType.DMA((7,)), pltpu.SemaphoreType.DMA((7,))],
    )(gpk, wpk, mpk, vpk)


def _pack_small(parts):
    rows = [[] for _ in range(SMALL_ROWS)]
    for n, (r, col, width) in sorted(SMALL.items(), key=lambda kv: (kv[1][0], kv[1][1])):
        rows[r].append((col, parts[n].reshape(-1).astype(F32)))
    out = []
    for r in range(SMALL_ROWS):
        segs, pos = [], 0
        for col, vec in rows[r]:
            assert col == pos
            segs.append(vec)
            pos += vec.shape[0]
        if pos < PACK_COLS:
            segs.append(jnp.zeros((PACK_COLS - pos,), F32))
        out.append(jnp.concatenate(segs))
    return jnp.stack(out)


def _unpack_small(pk, shapes):
    return {n: pk[r, col:col + width].reshape(shapes[n]) for n, (r, col, width) in SMALL.items()}


WEIGHTS = ("mix_norm", "w_in", "attn_q_norm", "attn_k_norm", "ret_decay_logit", "ret_norm_gain", "w_attn_o", "w_ret_o", "w_out",
           "mlp_norm", "w_up", "w_down", "ple_norm", "w_ple_gate", "w_ple", "final_norm")


def kernel(x, p, mix_norm, w_in, attn_q_norm, attn_k_norm, ret_decay_logit, ret_norm_gain, w_attn_o, w_ret_o, w_out, mlp_norm, w_up, w_down, ple_norm, w_ple_gate, w_ple, final_norm, loss_target, m_mix_norm, m_w_in, m_attn_q_norm, m_attn_k_norm, m_ret_decay_logit, m_ret_norm_gain, m_w_attn_o, m_w_ret_o, m_w_out, m_mlp_norm, m_w_up, m_w_down, m_ple_norm, m_w_ple_gate, m_w_ple, m_final_norm, v_mix_norm, v_w_in, v_attn_q_norm, v_attn_k_norm, v_ret_decay_logit, v_ret_norm_gain, v_w_attn_o, v_w_ret_o, v_w_out, v_mlp_norm, v_w_up, v_w_down, v_ple_norm, v_w_ple_gate, v_w_ple, v_final_norm):
    args = dict(locals())
    wts = {n: args[n] for n in WEIGHTS}
    ms = {n: args["m_" + n] for n in WEIGHTS}
    vs = {n: args["v_" + n] for n in WEIGHTS}
    shapes = {n: wts[n].shape for n in WEIGHTS}
    big_names = [n for n, _ in BIG]
    xi, yi, ci = _position()
    c_idx = ci.astype(jnp.int32).reshape(1)

    slab_w = _pack_shard({n: wts[n][0] for n in big_names})
    gathered = _gather_weights(slab_w.astype(BF16))
    full = {}
    for n, axis in BIG:
        per_chip = [_unpack_shard(gathered[k], {m_: shapes[m_][1:] for m_ in big_names})[n] for k in range(N_CHIPS)]
        full[n] = jnp.concatenate(per_chip, axis=axis)
    small = {n: wts[n].reshape(wts[n].shape[1:] if wts[n].ndim > 1 else wts[n].shape) for n in SMALL}

    loss_part, grad_x, gw, gs = _local_step(x[0], p[0, 0], loss_target[0], full, small)
    loss = lax.psum(loss_part, ("x", "y", "c"))

    slabs = jnp.stack([_pack_shard({n: _shard_of(gw[n], axis, k) for n, axis in BIG}) for k in range(N_CHIPS)])
    halves = slabs.reshape(N_CHIPS, 2, HALF_ROWS, PACK_COLS).transpose(1, 0, 2, 3)
    chip_part = _add_my_half(halves, _exchange_halves(halves), c_idx)
    reduced = _join_halves(_sum_chips(_scatter_to_chips(chip_part))).reshape(2 * HALF_ROWS, PACK_COLS)
    delta_b, newm_b, newv_b = _adamw(slab_w, reduced, _pack_shard({n: ms[n][0] for n in big_names}), _pack_shard({n: vs[n][0] for n in big_names}))
    shard_shapes = {n: shapes[n] for n in big_names}
    big_out = [_unpack_shard(a, shard_shapes) for a in (reduced, delta_b, newm_b, newv_b)]

    sm_out = _small_step(_pack_small(gs), _pack_small({n: wts[n] for n in SMALL}), _pack_small({n: ms[n] for n in SMALL}),
                         _pack_small({n: vs[n] for n in SMALL}))
    small_out = [_unpack_small(a, {n: shapes[n] for n in SMALL}) for a in sm_out]

    outs = [loss, grad_x[None]]
    for kind in range(4):
        for n in WEIGHTS:
            outs.append(small_out[kind][n] if n in SMALL else big_out[kind][n])
    return tuple(outs)
```

```python
import functools
import math

import jax
import jax.numpy as jnp
from jax import lax
from jax.experimental import pallas as pl
from jax.experimental.pallas import tpu as pltpu

F32 = jnp.float32
BF16 = jnp.bfloat16
MESH = pl.DeviceIdType.MESH

D_MODEL = 1024
PLE_DIM = 256
GRID_W = 64
ATTN_HEAD_DIM = 64
ATTN_HEADS = 8
ATTN_KV_HEADS = 2
ATTN_GROUP = ATTN_HEADS // ATTN_KV_HEADS
RET_HEAD_DIM = 128
RET_HEADS = 4
ATTN_Q_W = 512
ATTN_KV_W = 128
RET_W = 512
IN_W = 4864
D_FF = 4096
RET_CHUNK = 128
ROPE_THETA = 10000.0
NORM_EPS = 1e-6
GN_EPS = 1e-5
ATTN_SCALE = ATTN_HEAD_DIM ** -0.5
LOG2E = math.log2(math.e)
Q_FOLD = ATTN_SCALE * LOG2E
RET_SCALE = RET_HEAD_DIM ** -0.5

C_AQ, C_AK, C_AV, C_RQ, C_RK, C_RV, C_RG, C_GA, C_GR = 0, 512, 640, 768, 1280, 1792, 2304, 2816, 3840

ADAM_LR = 0.001
ADAM_B1 = 0.9
ADAM_B2 = 0.999
ADAM_EPS = 1e-08
ADAM_WD = 0.01
ADAM_STEP = 10

LANES = 128
VMEM_LIMIT = 56 << 20
SEQ_TILE = 512

BIG = (("w_in", 1), ("w_attn_o", 1), ("w_ret_o", 1), ("w_out", 0), ("w_up", 1), ("w_down", 0), ("w_ple_gate", 0), ("w_ple", 1))
SMALL_ROWS = 8
SMALL = {"mix_norm": (0, 0, 1024), "mlp_norm": (1, 0, 1024), "ple_norm": (2, 0, 1024), "final_norm": (3, 0, 1024),
         "ret_norm_gain": (4, 0, 512), "attn_q_norm": (4, 512, 64), "attn_k_norm": (4, 576, 64), "ret_decay_logit": (4, 640, 8)}


def _seq_tile(s):
    return min(SEQ_TILE, s // 2)


def _cparams(sem=None, vmem=VMEM_LIMIT):
    return pltpu.CompilerParams(dimension_semantics=sem, vmem_limit_bytes=vmem)


def _mm(name, a, b, *, ta=False, tb=False, tm, tn, tk, out_dtypes=(F32,), epi=None, epi_ins=(), consts=()):
    if ta:
        kdim, m = a.shape
    else:
        m, kdim = a.shape
    n = b.shape[0] if tb else b.shape[1]
    tm, tn, tk = min(tm, m), min(tn, n), min(tk, kdim)
    assert m % tm == 0 and n % tn == 0 and kdim % tk == 0, (name, m, n, kdim, tm, tn, tk)
    nk = kdim // tk
    n_e, n_c, n_o = len(epi_ins), len(consts), len(out_dtypes)

    def body(*refs):
        a_ref, b_ref = refs[0], refs[1]
        e_refs = refs[2:2 + n_e]
        c_refs = refs[2 + n_e:2 + n_e + n_c]
        o_refs = refs[2 + n_e + n_c:2 + n_e + n_c + n_o]
        acc_ref = refs[2 + n_e + n_c + n_o] if nk > 1 else None
        k = pl.program_id(2)
        av = a_ref[...].astype(BF16)
        bv = b_ref[...].astype(BF16)
        dims = (((0,) if ta else (1,), (1,) if tb else (0,)), ((), ()))
        part = lax.dot_general(av, bv, dims, preferred_element_type=F32)

        def finish(acc):
            vals = epi(acc, e_refs, c_refs) if epi is not None else (acc,)
            for o_ref, v in zip(o_refs, vals):
                o_ref[...] = v.astype(o_ref.dtype)

        if nk == 1:
            finish(part)
        else:
            @pl.when(k == 0)
            def _():
                acc_ref[...] = part

            @pl.when(k > 0)
            def _():
                acc_ref[...] += part

            @pl.when(k == nk - 1)
            def _():
                finish(acc_ref[...])

    a_spec = pl.BlockSpec((tk, tm), lambda i, j, k: (k, i)) if ta else pl.BlockSpec((tm, tk), lambda i, j, k: (i, k))
    b_spec = pl.BlockSpec((tn, tk), lambda i, j, k: (j, k)) if tb else pl.BlockSpec((tk, tn), lambda i, j, k: (k, j))
    o_spec = pl.BlockSpec((tm, tn), lambda i, j, k: (i, j))
    c_specs = [pl.BlockSpec(c.shape, lambda i, j, k, nd=c.ndim: (0,) * nd) for c in consts]
    outs = pl.pallas_call(
        body, name=name,
        grid=(m // tm, n // tn, nk),
        in_specs=[a_spec, b_spec] + [o_spec] * n_e + c_specs,
        out_specs=[o_spec] * n_o,
        out_shape=[jax.ShapeDtypeStruct((m, n), dt) for dt in out_dtypes],
        scratch_shapes=[pltpu.VMEM((tm, tn), F32)] if nk > 1 else [],
        compiler_params=_cparams(("parallel", "parallel", "arbitrary")),
    )(a, b, *epi_ins, *consts)
    return outs[0] if n_o == 1 else outs


def _rows(arr, tr):
    return (arr, pl.BlockSpec((tr, arr.shape[1]), lambda i: (i, 0)))


def _win(arr, tr, start, width):
    bw = math.gcd(start, width) if start else width
    assert bw % LANES == 0
    return [(arr, pl.BlockSpec((tr, bw), lambda i, cb=start // bw + p: (i, cb))) for p in range(width // bw)]


def _ct(arr):
    return (arr, pl.BlockSpec((None,) + arr.shape[1:], lambda i: (i, 0, 0)))


def _whole(arr):
    return (arr, pl.BlockSpec(arr.shape, lambda i, nd=arr.ndim: (0,) * nd))


def _cat(refs):
    vals = [r[...] for r in refs]
    return vals[0] if len(vals) == 1 else jnp.concatenate(vals, axis=1)


def _seqtiled(name, fn, n_tiles, ins, outs, acc_widths=()):
    n_i, n_o, n_a = len(ins), len(outs), len(acc_widths)

    def body(*refs):
        i_refs, o_refs, a_refs = refs[:n_i], refs[n_i:n_i + n_o], refs[n_i + n_o:]
        if n_a:
            @pl.when(pl.program_id(0) == 0)
            def _():
                for r in a_refs:
                    r[...] = jnp.zeros(r.shape, F32)
        fn(list(i_refs), list(o_refs), list(a_refs))

    res = pl.pallas_call(
        body, name=name, grid=(n_tiles,),
        in_specs=[s for _, s in ins],
        out_specs=[s for _, _, s in outs] + [pl.BlockSpec((8, w), lambda i: (0, 0)) for w in acc_widths],
        out_shape=[jax.ShapeDtypeStruct(sh, dt) for sh, dt, _ in outs] + [jax.ShapeDtypeStruct((8, w), F32) for w in acc_widths],
        compiler_params=_cparams(("arbitrary",)),
    )(*[a for a, _ in ins])
    return res


def _acc_add(acc_ref, val):
    acc_ref[0:1, :] += jnp.sum(val, axis=0, keepdims=True)


def _out_rows(s, w, dt, tr):
    return ((s, w), dt, pl.BlockSpec((tr, w), lambda i: (i, 0)))


def _out_ct(s, w, dt, t):
    return ((s // t, w, t), dt, pl.BlockSpec((None, w, t), lambda i: (i, 0, 0)))


def _rms_fwd(x, gain):
    r = lax.rsqrt(jnp.mean(x * x, axis=-1, keepdims=True) + NORM_EPS)
    return x * r * gain


def _rms_bwd(dy, x, gain):
    r = lax.rsqrt(jnp.mean(x * x, axis=-1, keepdims=True) + NORM_EPS)
    xn = x * r
    dyg = dy * gain
    dx = r * (dyg - xn * jnp.mean(dyg * xn, axis=-1, keepdims=True))
    return dx, dy * xn


def _seg_mean(y, hd):
    w = y.shape[1]
    pieces = []
    for s in range(0, w, LANES):
        v = y[:, s:s + LANES]
        tot = jnp.sum(v, axis=1, keepdims=True)
        if hd == LANES:
            pieces.append(jnp.broadcast_to(tot, v.shape))
        else:
            low = lax.broadcasted_iota(jnp.int32, v.shape, 1) < hd
            lo = jnp.sum(jnp.where(low, v, 0.0), axis=1, keepdims=True)
            pieces.append(jnp.where(low, lo, tot - lo))
    out = pieces[0] if len(pieces) == 1 else jnp.concatenate(pieces, axis=1)
    return out * (1.0 / hd)


def _tile_lanes(t, w):
    return t if w == t.shape[1] else jnp.concatenate([t] * (w // t.shape[1]), axis=1)


def _swap_halves(x, hd):
    w = x.shape[1]
    half = hd // 2
    lane = lax.broadcasted_iota(jnp.int32, x.shape, 1)
    return jnp.where((lane % hd) < half, pltpu.roll(x, w - half, 1), pltpu.roll(x, half, 1))


def _rope(x, cos, sin_signed, hd):
    w = x.shape[1]
    return x * _tile_lanes(cos, w) + _swap_halves(x, hd) * _tile_lanes(sin_signed, w)


def _rope_t(dy, cos, sin_signed, hd):
    w = dy.shape[1]
    return dy * _tile_lanes(cos, w) + _swap_halves(dy * _tile_lanes(sin_signed, w), hd)


def _headnorm_fwd(x, gain_w, hd):
    r = lax.rsqrt(_seg_mean(x * x, hd) + NORM_EPS)
    return x * r * gain_w


def _headnorm_bwd(dy, x, gain_w, hd):
    r = lax.rsqrt(_seg_mean(x * x, hd) + NORM_EPS)
    xn = x * r
    dyg = dy * gain_w
    return r * (dyg - xn * _seg_mean(dyg * xn, hd)), dy * xn


def _sigmoid(x):
    return 1.0 / (1.0 + jnp.exp(-x))


def _rope_tables(seq_len, head_dim):
    rows = seq_len // GRID_W
    n_axis = head_dim // 4
    freqs = ROPE_THETA ** (-jnp.arange(n_axis, dtype=F32) / n_axis)
    row = jnp.repeat(jnp.arange(rows, dtype=F32), GRID_W)
    col = jnp.tile(jnp.arange(GRID_W, dtype=F32), rows)
    ang = jnp.concatenate([row[:, None] * freqs, col[:, None] * freqs], axis=-1)
    cos, sin = jnp.cos(ang), jnp.sin(ang)
    reps = LANES // head_dim
    return jnp.tile(jnp.concatenate([cos, cos], axis=-1), (1, reps)), jnp.tile(jnp.concatenate([-sin, sin], axis=-1), (1, reps))


def _stage_norm_in(x, gain):
    s = x.shape[0]
    tr = min(SEQ_TILE, s)

    def fn(i, o, a):
        o[0][...] = _rms_fwd(i[0][...], i[1][...]).astype(BF16)

    return _seqtiled("norm_in", fn, s // tr, [_rows(x, tr), _whole(gain)], [_out_rows(s, D_MODEL, BF16, tr)])[0]


def _stage_qkv(proj, tabs, gq_w, gk_w):
    s = proj.shape[0]
    t = _seq_tile(s)
    ca, sa, cr, sr = tabs
    ins = (_win(proj, t, C_AQ, ATTN_Q_W) + _win(proj, t, C_AK, ATTN_KV_W) + _win(proj, t, C_AV, ATTN_KV_W)
           + _win(proj, t, C_RQ, RET_W) + _win(proj, t, C_RK, RET_W)
           + [_rows(ca, t), _rows(sa, t), _rows(cr, t), _rows(sr, t), _whole(gq_w), _whole(gk_w)])

    def fn(i, o, a):
        aq, ak, av = i[0][...], i[1][...], i[2][...]
        rq, rk = _cat(i[3:5]), _cat(i[5:7])
        ca_, sa_, cr_, sr_ = i[7][...], i[8][...], i[9][...], i[10][...]
        qr = _rope(_headnorm_fwd(aq, i[11][...], ATTN_HEAD_DIM), ca_, sa_, ATTN_HEAD_DIM) * Q_FOLD
        kr = _rope(_headnorm_fwd(ak, i[12][...], ATTN_HEAD_DIM), ca_, sa_, ATTN_HEAD_DIM)
        qt = qr.T.astype(BF16)
        zeros = jnp.zeros((ATTN_HEAD_DIM, t), BF16)
        for h in range(ATTN_HEADS):
            g = h // ATTN_GROUP
            blk = qt[h * ATTN_HEAD_DIM:(h + 1) * ATTN_HEAD_DIM, :]
            o[0][h * LANES + g * ATTN_HEAD_DIM:h * LANES + (g + 1) * ATTN_HEAD_DIM, :] = blk
            o[0][h * LANES + (1 - g) * ATTN_HEAD_DIM:h * LANES + (2 - g) * ATTN_HEAD_DIM, :] = zeros
        o[1][...] = kr.astype(BF16)
        o[2][...] = kr.T.astype(BF16)
        o[3][...] = av.astype(BF16)
        o[4][...] = av.T.astype(BF16)
        o[5][...] = _rope(rq, cr_, sr_, RET_HEAD_DIM) * RET_SCALE
        o[6][...] = _rope(rk, cr_, sr_, RET_HEAD_DIM)

    outs = [_out_ct(s, ATTN_HEADS * LANES, BF16, t), _out_rows(s, ATTN_KV_W, BF16, t), _out_ct(s, ATTN_KV_W, BF16, t),
            _out_rows(s, ATTN_KV_W, BF16, t), _out_ct(s, ATTN_KV_W, BF16, t), _out_rows(s, RET_W, F32, t), _out_rows(s, RET_W, F32, t)]
    return _seqtiled("qkv_prep", fn, s // t, ins, outs)


def _groupnorm_gate(ry, rg, gain):
    mu = _seg_mean(ry, RET_HEAD_DIM)
    d = ry - mu
    rs = lax.rsqrt(_seg_mean(d * d, RET_HEAD_DIM) + GN_EPS)
    return d * rs, rs, _sigmoid(rg)


def _stage_mix_post(ry_f, ry_b, proj, o_ct, gain):
    s = proj.shape[0]
    t = _seq_tile(s)
    ins = [_rows(ry_f, t), _rows(ry_b, t)] + _win(proj, t, C_RG, RET_W) + [_ct(o_ct), _whole(gain)]

    def fn(i, o, a):
        ry = i[0][...] + i[1][...]
        rg = _cat(i[2:4])
        gn, _, sg = _groupnorm_gate(ry, rg, None)
        o[0][...] = (gn * i[5][...] * (rg * sg)).astype(BF16)
        o[1][...] = i[4][...].astype(F32).T.astype(BF16)

    return _seqtiled("mix_post", fn, s // t, ins, [_out_rows(s, RET_W, BF16, t), _out_rows(s, ATTN_Q_W, BF16, t)])


def _stage_merge(proj, a_out, r_out):
    s = proj.shape[0]
    tr = min(SEQ_TILE, s)
    ins = _win(proj, tr, C_GA, D_MODEL) + _win(proj, tr, C_GR, D_MODEL) + [_rows(a_out, tr), _rows(r_out, tr)]
    na = len(_win(proj, tr, C_GA, D_MODEL))

    def fn(i, o, a):
        ga, gr = _cat(i[:na]), _cat(i[na:2 * na])
        o[0][...] = (_sigmoid(ga) * i[2 * na][...] + _sigmoid(gr) * i[2 * na + 1][...]).astype(BF16)

    return _seqtiled("merge", fn, s // tr, ins, [_out_rows(s, D_MODEL, BF16, tr)])[0]


def _stage_head(zg, pe, x2, target, g_final):
    s = x2.shape[0]
    tr = min(SEQ_TILE // 2, s)
    ins = [_rows(zg, tr), _rows(pe, tr), _rows(x2, tr), _rows(target, tr), _whole(g_final)]

    def fn(i, o, a):
        gt = _sigmoid(i[0][...])
        pe_ = i[1][...]
        x3 = i[2][...] + gt * pe_
        gf = i[4][...]
        r3 = lax.rsqrt(jnp.mean(x3 * x3, axis=-1, keepdims=True) + NORM_EPS)
        x3n = x3 * r3
        e = x3n * gf - i[3][...]
        _acc_add(a[0], e * e)
        dy = e * (1.0 / D_MODEL)
        _acc_add(a[1], dy * x3n)
        dyg = dy * gf
        dx3 = r3 * (dyg - x3n * jnp.mean(dyg * x3n, axis=-1, keepdims=True))
        o[0][...] = dx3
        o[1][...] = (dx3 * pe_ * gt * (1.0 - gt)).astype(BF16)
        o[2][...] = (dx3 * gt).astype(BF16)

    outs = [_out_rows(s, D_MODEL, F32, tr), _out_rows(s, D_MODEL, BF16, tr), _out_rows(s, D_MODEL, BF16, tr)]
    return _seqtiled("head", fn, s // tr, ins, outs, acc_widths=(D_MODEL, D_MODEL))


def _stage_norm_bwd(name, dh, x, dres, gain):
    s = x.shape[0]
    tr = min(SEQ_TILE // 2, s)

    def fn(i, o, a):
        dx, dg = _rms_bwd(i[0][...], i[1][...], i[3][...])
        o[0][...] = i[2][...] + dx
        _acc_add(a[0], dg)

    return _seqtiled(name, fn, s // tr, [_rows(dh, tr), _rows(x, tr), _rows(dres, tr), _whole(gain)],
                     [_out_rows(s, D_MODEL, F32, tr)], acc_widths=(D_MODEL,))


def _stage_merge_bwd(proj, dmerged, a_out, r_out):
    s = proj.shape[0]
    tr = min(SEQ_TILE // 2, s)
    wins = _win(proj, tr, C_GA, D_MODEL)
    na = len(wins)
    ins = wins + _win(proj, tr, C_GR, D_MODEL) + [_rows(dmerged, tr), _rows(a_out, tr), _rows(r_out, tr)]

    def fn(i, o, a):
        sa, sr = _sigmoid(_cat(i[:na])), _sigmoid(_cat(i[na:2 * na]))
        dm = i[2 * na][...]
        o[0][...] = (dm * sa).astype(BF16)
        o[1][...] = (dm * sr).astype(BF16)
        o[2][...] = (dm * i[2 * na + 1][...] * sa * (1.0 - sa)).astype(BF16)
        o[3][...] = (dm * i[2 * na + 2][...] * sr * (1.0 - sr)).astype(BF16)

    return _seqtiled("merge_bwd", fn, s // tr, ins, [_out_rows(s, D_MODEL, BF16, tr)] * 4)


def _stage_mix_post_bwd(dattn, attn_rows, drz, ry_f, ry_b, proj, gain):
    s = proj.shape[0]
    t = _seq_tile(s)
    ins = ([_rows(dattn, t), _rows(attn_rows, t), _rows(drz, t), _rows(ry_f, t), _rows(ry_b, t)]
           + _win(proj, t, C_RG, RET_W) + [_whole(gain)])

    def fn(i, o, a):
        da = i[0][...]
        dat = da.T
        prod_t = (da * i[1][...].astype(F32)).T
        dat_b = dat.astype(BF16)
        zeros = jnp.zeros((ATTN_HEAD_DIM, t), BF16)
        for h in range(ATTN_HEADS):
            g = h // ATTN_GROUP
            o[0][h * LANES + g * ATTN_HEAD_DIM:h * LANES + (g + 1) * ATTN_HEAD_DIM, :] = dat_b[h * ATTN_HEAD_DIM:(h + 1) * ATTN_HEAD_DIM, :]
            o[0][h * LANES + (1 - g) * ATTN_HEAD_DIM:h * LANES + (2 - g) * ATTN_HEAD_DIM, :] = zeros
            o[1][h] = jnp.sum(prod_t[h * ATTN_HEAD_DIM:(h + 1) * ATTN_HEAD_DIM, :], axis=0, keepdims=True)
        ry = i[3][...] + i[4][...]
        rg = _cat(i[5:7])
        gain_ = i[7][...]
        gn, rs, sg = _groupnorm_gate(ry, rg, None)
        dz = i[2][...]
        silu = rg * sg
        _acc_add(a[0], dz * gn * silu)
        dgn = dz * gain_ * silu
        o[2][...] = rs * (dgn - _seg_mean(dgn, RET_HEAD_DIM) - gn * _seg_mean(dgn * gn, RET_HEAD_DIM))
        o[3][...] = (dz * gn * gain_ * (sg * (1.0 + rg * (1.0 - sg)))).astype(BF16)

    outs = [_out_ct(s, ATTN_HEADS * LANES, BF16, t),
            ((ATTN_HEADS, s // t, 1, t), F32, pl.BlockSpec((ATTN_HEADS, None, 1, t), lambda i: (0, i, 0, 0))),
            _out_rows(s, RET_W, F32, t), _out_rows(s, RET_W, BF16, t)]
    return _seqtiled("mix_post_bwd", fn, s // t, ins, outs, acc_widths=(RET_W,))


def _stage_dproj(proj, dq_ct, dk8, dv8, rgrads, drg, dga, dgr, tabs, gq_w, gk_w):
    s = proj.shape[0]
    t = _seq_tile(s)
    ca, sa, cr, sr = tabs
    kv8 = pl.BlockSpec((ATTN_HEADS, t, ATTN_KV_W), lambda i: (0, i, 0))
    ins = (_win(proj, t, C_AQ, ATTN_Q_W) + _win(proj, t, C_AK, ATTN_KV_W) + [_ct(dq_ct), (dk8, kv8), (dv8, kv8)]
           + [_rows(g, t) for g in rgrads] + [_rows(drg, t), _rows(dga, t), _rows(dgr, t)]
           + [_rows(ca, t), _rows(sa, t), _rows(cr, t), _rows(sr, t), _whole(gq_w), _whole(gk_w)])

    def fn(i, o, a):
        aq, ak = i[0][...], i[1][...]
        dq_f, dk_f, dv_f, dq_b, dk_b, dv_b = (r[...] for r in i[5:11])
        ca_, sa_, cr_, sr_ = i[14][...], i[15][...], i[16][...], i[17][...]
        dqn = _rope_t(i[2][...].T * ATTN_SCALE, ca_, sa_, ATTN_HEAD_DIM)
        daq, gq_rows = _headnorm_bwd(dqn, aq, i[18][...], ATTN_HEAD_DIM)
        dkn = _rope_t(jnp.sum(i[3][...], axis=0) * (1.0 / LOG2E), ca_, sa_, ATTN_HEAD_DIM)
        dak, gk_rows = _headnorm_bwd(dkn, ak, i[19][...], ATTN_HEAD_DIM)
        _acc_add(a[0], gq_rows)
        _acc_add(a[1], gk_rows)
        out = o[0]
        out[:, C_AQ:C_AQ + ATTN_Q_W] = daq.astype(BF16)
        out[:, C_AK:C_AK + ATTN_KV_W] = dak.astype(BF16)
        out[:, C_AV:C_AV + ATTN_KV_W] = jnp.sum(i[4][...], axis=0).astype(BF16)
        out[:, C_RQ:C_RQ + RET_W] = _rope_t((dq_f + dq_b) * RET_SCALE, cr_, sr_, RET_HEAD_DIM).astype(BF16)
        out[:, C_RK:C_RK + RET_W] = _rope_t(dk_f + dk_b, cr_, sr_, RET_HEAD_DIM).astype(BF16)
        out[:, C_RV:C_RV + RET_W] = (dv_f + dv_b).astype(BF16)
        out[:, C_RG:C_RG + RET_W] = i[11][...]
        out[:, C_GA:C_GA + D_MODEL] = i[12][...]
        out[:, C_GR:C_GR + D_MODEL] = i[13][...]

    return _seqtiled("dproj", fn, s // t, ins, [_out_rows(s, IN_W, BF16, t)], acc_widths=(ATTN_Q_W, ATTN_KV_W))


def _attn_fwd(q_ct, k_rows, v_ct):
    nq, _, t = q_ct.shape
    s = nq * t
    nk = nq
    assert nk % 2 == 0

    def body(q_ref, k_ref, v_ref, o_ref, lse_ref, s0, s1, p0, p1):
        sbuf, pbuf = (s0, s1), (p0, p1)
        qt = q_ref[...]

        def scores(j, slot):
            kj = k_ref[pl.ds(pl.multiple_of(j * t, t), t), :]
            st = jnp.dot(kj, qt, preferred_element_type=F32)
            sbuf[slot][...] = st
            return jnp.max(st, axis=0, keepdims=True)

        def probs(slot, cmax, m, l):
            m_new = jnp.maximum(m, cmax)
            alpha = jnp.exp2(m - m_new)
            pt = jnp.exp2(sbuf[slot][...] - m_new)
            pbuf[slot][...] = pt.astype(BF16)
            return m_new, alpha * l + jnp.sum(pt, axis=0, keepdims=True), alpha

        def values(j, slot, alpha, acc):
            return alpha * acc + jnp.dot(v_ref[j], pbuf[slot][...], preferred_element_type=F32)

        m = jnp.full((1, t), -1e30, F32)
        l = jnp.zeros((1, t), F32)
        acc = jnp.zeros((ATTN_HEAD_DIM, t), F32)
        cmax0 = scores(0, 0)
        cmax1 = scores(1, 1)
        m, l, alpha0 = probs(0, cmax0, m, l)

        def trip(n, carry):
            m, l, acc, cmax_b, alpha_c = carry
            c = 2 * n
            acc = values(c, 0, alpha_c, acc)
            m, l, alpha1 = probs(1, cmax_b, m, l)
            cmax2 = scores(c + 2, 0)
            acc = values(c + 1, 1, alpha1, acc)
            m, l, alpha2 = probs(0, cmax2, m, l)
            cmax3 = scores(c + 3, 1)
            return m, l, acc, cmax3, alpha2

        m, l, acc, cmax_b, alpha_c = lax.fori_loop(0, nk // 2 - 1, trip, (m, l, acc, cmax1, alpha0))
        acc = values(nk - 2, 0, alpha_c, acc)
        m, l, alpha1 = probs(1, cmax_b, m, l)
        acc = values(nk - 1, 1, alpha1, acc)
        o_ref[...] = (acc / l).astype(BF16)
        lse_ref[...] = m + jnp.log2(l)

    return pl.pallas_call(
        body, name="attn_fwd", grid=(ATTN_HEADS, nq),
        in_specs=[pl.BlockSpec((None, LANES, t), lambda h, i: (i, h, 0)),
                  pl.BlockSpec((s, ATTN_KV_W), lambda h, i: (0, 0)),
                  pl.BlockSpec((nk, ATTN_HEAD_DIM, t), lambda h, i: (0, h // ATTN_GROUP, 0))],
        out_specs=[pl.BlockSpec((None, ATTN_HEAD_DIM, t), lambda h, i: (i, h, 0)),
                   pl.BlockSpec((None, None, 1, t), lambda h, i: (h, i, 0, 0))],
        out_shape=[jax.ShapeDtypeStruct((nq, ATTN_Q_W, t), BF16), jax.ShapeDtypeStruct((ATTN_HEADS, nq, 1, t), F32)],
        scratch_shapes=[pltpu.VMEM((t, t), F32), pltpu.VMEM((t, t), F32), pltpu.VMEM((t, t), BF16), pltpu.VMEM((t, t), BF16)],
        compiler_params=_cparams(("parallel", "parallel")),
    )(q_ct, k_rows, v_ct)


def _attn_bwd(q_ct, do_ct, lse, delta, k_rows, v_rows, k_ct):
    nq, _, t = q_ct.shape
    s = nq * t
    nk = nq

    assert nq % 2 == 0

    def body(q_ref, do_ref, lse_ref, delta_ref, k_ref, v_ref, kt_ref, dq_ref, dk_ref, dv_ref, dk_acc, dv_acc,
             sb0, sb1, db0, db1, pb0, pb1, gb0, gb1):
        j = pl.program_id(1)
        sb, db, pb, gb = (sb0, sb1), (db0, db1), (pb0, pb1), (gb0, gb1)

        @pl.when(j == 0)
        def _():
            dq_ref[...] = jnp.zeros(dq_ref.shape, F32)

        kj, vj, ktj = k_ref[...], v_ref[...], kt_ref[...]
        dk_acc[...] = jnp.zeros(dk_acc.shape, F32)
        dv_acc[...] = jnp.zeros(dv_acc.shape, F32)

        def products(i, slot):
            sb[slot][...] = jnp.dot(kj, q_ref[i], preferred_element_type=F32)
            db[slot][...] = jnp.dot(vj, do_ref[i], preferred_element_type=F32)

        def cotangents(i, slot):
            pt = jnp.exp2(sb[slot][...] - lse_ref[i])
            pb[slot][...] = pt.astype(BF16)
            gb[slot][...] = (pt * (db[slot][...] - delta_ref[i])).astype(BF16)

        def accumulate(i, slot):
            dst = gb[slot][...]
            dv_acc[...] += _nt(pb[slot][...], do_ref[i])
            dk_acc[...] += _nt(dst, q_ref[i])
            dq_ref[i] += jnp.dot(ktj, dst, preferred_element_type=F32)

        products(0, 0)
        products(1, 1)
        cotangents(0, 0)

        def trip(n, carry):
            c = 2 * n
            accumulate(c, 0)
            cotangents(c + 1, 1)
            products(c + 2, 0)
            accumulate(c + 1, 1)
            cotangents(c + 2, 0)
            products(c + 3, 1)
            return carry

        lax.fori_loop(0, nq // 2 - 1, trip, 0)
        accumulate(nq - 2, 0)
        cotangents(nq - 1, 1)
        accumulate(nq - 1, 1)
        dk_ref[...] = dk_acc[...]
        dv_ref[...] = dv_acc[...]

    per_head = pl.BlockSpec((nq, LANES, t), lambda h, j: (0, h, 0))
    stat = pl.BlockSpec((None, nq, 1, t), lambda h, j: (h, 0, 0, 0))
    kv_rows = pl.BlockSpec((t, ATTN_KV_W), lambda h, j: (j, 0))
    kv_out = pl.BlockSpec((None, t, ATTN_KV_W), lambda h, j: (h, j, 0))
    return pl.pallas_call(
        body, name="attn_bwd", grid=(ATTN_HEADS, nk),
        in_specs=[per_head, per_head, stat, stat, kv_rows, kv_rows,
                  pl.BlockSpec((None, ATTN_HEAD_DIM, t), lambda h, j: (j, h // ATTN_GROUP, 0))],
        out_specs=[pl.BlockSpec((nq, ATTN_HEAD_DIM, t), lambda h, j: (0, h, 0)), kv_out, kv_out],
        out_shape=[jax.ShapeDtypeStruct((nq, ATTN_Q_W, t), F32), jax.ShapeDtypeStruct((ATTN_HEADS, s, ATTN_KV_W), F32),
                   jax.ShapeDtypeStruct((ATTN_HEADS, s, ATTN_KV_W), F32)],
        scratch_shapes=([pltpu.VMEM((t, ATTN_KV_W), F32)] * 2 + [pltpu.VMEM((t, t), F32)] * 4 + [pltpu.VMEM((t, t), BF16)] * 4),
        compiler_params=_cparams(("parallel", "arbitrary")),
    )(q_ct, do_ct, lse, delta, k_rows, v_rows, k_ct)


def _log_sigmoid(x):
    t = jnp.exp(-jnp.abs(x))
    log1p_t = jnp.where(t < 1e-2, t * (1.0 - t * (0.5 - t * (1.0 / 3.0))), jnp.log(1.0 + t))
    return jnp.minimum(x, 0.0) - log1p_t


def _decay_tables(logit, backward):
    c = RET_CHUNK
    lam = _log_sigmoid(jnp.full((c, c), logit, F32))
    ii = lax.broadcasted_iota(jnp.int32, (c, c), 0).astype(F32)
    jj = lax.broadcasted_iota(jnp.int32, (c, c), 1).astype(F32)
    if not backward:
        dist, dist_t = jnp.maximum(ii - jj, 0.0), jnp.maximum(jj - ii, 0.0)
        mask, mask_t = ii >= jj, jj >= ii
        e_q, e_k = ii + 1.0, (c - 1.0) - ii
    else:
        dist, dist_t = jnp.maximum(jj - ii, 0.0), jnp.maximum(ii - jj, 0.0)
        mask, mask_t = jj > ii, ii > jj
        e_q, e_k = c - ii, ii
    return dict(
        d=jnp.where(mask, jnp.exp(lam * dist), 0.0), d_t=jnp.where(mask_t, jnp.exp(lam * dist_t), 0.0), dist=dist,
        qdec=jnp.exp(lam * e_q), kdec=jnp.exp(lam * e_k), e_q=e_q, e_k=e_k, gam=jnp.exp(lam * c))


def _nt(a, b):
    return lax.dot_general(a, b, (((1,), (1,)), ((), ())), preferred_element_type=F32)


def _ret_fwd(name, logits, q, k, proj, backward):
    s = q.shape[0]
    c = RET_CHUNK
    nc = s // c
    d_ix = 1 if backward else 0
    chunk = (lambda n: nc - 1 - n) if backward else (lambda n: n)
    vwin = _win(proj, c, C_RV, RET_W)
    nv = len(vwin)
    vw = RET_W // nv

    def body(lg_ref, q_ref, k_ref, *rest):
        v_refs, (y_ref, st_ref, state) = rest[:nv], rest[nv:]

        @pl.when(pl.program_id(0) == 0)
        def _():
            state[...] = jnp.zeros(state.shape, F32)

        for h in range(RET_HEADS):
            tb = _decay_tables(lg_ref[d_ix, h], backward)
            sl = slice(h * RET_HEAD_DIM, (h + 1) * RET_HEAD_DIM)
            qh, kh = q_ref[:, sl], k_ref[:, sl]
            off = h * RET_HEAD_DIM
            vh = v_refs[off // vw][:, off % vw:off % vw + RET_HEAD_DIM]
            vb = vh.astype(BF16)
            a = _nt(qh.astype(BF16), kh.astype(BF16)) * tb["d"]
            sh = state[h]
            st_ref[h] = sh
            y_ref[:, sl] = (jnp.dot(a.astype(BF16), vb, preferred_element_type=F32)
                            + jnp.dot((qh * tb["qdec"]).astype(BF16), sh.astype(BF16), preferred_element_type=F32))
            state[h] = tb["gam"] * sh + jnp.dot((kh * tb["kdec"]).T.astype(BF16), vb, preferred_element_type=F32)

    rows = pl.BlockSpec((c, RET_W), lambda n: (chunk(n), 0))
    v_specs = [pl.BlockSpec(sp.block_shape, lambda n, im=sp.index_map: (chunk(n), im(0)[1])) for _, sp in vwin]
    return pl.pallas_call(
        body, name=name, grid=(nc,),
        in_specs=[pl.BlockSpec(memory_space=pltpu.SMEM), rows, rows] + v_specs,
        out_specs=[rows, pl.BlockSpec((None, RET_HEADS, RET_HEAD_DIM, RET_HEAD_DIM), lambda n: (chunk(n), 0, 0, 0))],
        out_shape=[jax.ShapeDtypeStruct((s, RET_W), F32), jax.ShapeDtypeStruct((nc, RET_HEADS, RET_HEAD_DIM, RET_HEAD_DIM), F32)],
        scratch_shapes=[pltpu.VMEM((RET_HEADS, RET_HEAD_DIM, RET_HEAD_DIM), F32)],
        compiler_params=_cparams(("arbitrary",)),
    )(logits, q, k, *[a for a, _ in vwin])


def _ret_bwd(name, logits, q, k, proj, dy, states, backward):
    s = q.shape[0]
    c = RET_CHUNK
    nc = s // c
    d_ix = 1 if backward else 0
    chunk = (lambda n: n) if backward else (lambda n: nc - 1 - n)
    vwin = _win(proj, c, C_RV, RET_W)
    nv = len(vwin)
    vw = RET_W // nv

    def body(lg_ref, q_ref, k_ref, dy_ref, st_ref, *rest):
        v_refs, (dq_ref, dk_ref, dv_ref, dl_ref, dstate, lacc) = rest[:nv], rest[nv:]
        n = pl.program_id(0)

        @pl.when(n == 0)
        def _():
            dstate[...] = jnp.zeros(dstate.shape, F32)
            lacc[...] = jnp.zeros(lacc.shape, F32)

        for h in range(RET_HEADS):
            tb = _decay_tables(lg_ref[d_ix, h], backward)
            sl = slice(h * RET_HEAD_DIM, (h + 1) * RET_HEAD_DIM)
            qh, kh, dyh = q_ref[:, sl], k_ref[:, sl], dy_ref[:, sl]
            off = h * RET_HEAD_DIM
            vh = v_refs[off // vw][:, off % vw:off % vw + RET_HEAD_DIM]
            qb, kb, vb, dyb = qh.astype(BF16), kh.astype(BF16), vh.astype(BF16), dyh.astype(BF16)
            sh, dsh = st_ref[h], dstate[h]
            shb, dshb = sh.astype(BF16), dsh.astype(BF16)
            qk = _nt(qb, kb)
            g = _nt(dyb, vb) * tb["d"]
            a_t = _nt(kb, qb) * tb["d_t"]
            g_t = _nt(vb, dyb) * tb["d_t"]
            qd, kd = qh * tb["qdec"], kh * tb["kdec"]
            dqd = _nt(dyb, shb)
            dkd = _nt(vb, dshb)
            dq_ref[:, sl] = jnp.dot(g.astype(BF16), kb, preferred_element_type=F32) + dqd * tb["qdec"]
            dk_ref[:, sl] = jnp.dot(g_t.astype(BF16), qb, preferred_element_type=F32) + dkd * tb["kdec"]
            dv_ref[:, sl] = (jnp.dot(a_t.astype(BF16), dyb, preferred_element_type=F32)
                             + jnp.dot(kd.astype(BF16), dshb, preferred_element_type=F32))
            lacc[h] += (tb["dist"] * qk * g + tb["e_q"] * qd * dqd + tb["e_k"] * kd * dkd
                        + float(c) * tb["gam"] * dsh * sh)
            dstate[h] = tb["gam"] * dsh + jnp.dot(qd.T.astype(BF16), dyb, preferred_element_type=F32)

        @pl.when(n == nc - 1)
        def _():
            for h in range(RET_HEADS):
                dl_ref[h] = jnp.zeros((8, LANES), F32) + jnp.sum(lacc[h])

    rows = pl.BlockSpec((c, RET_W), lambda n: (chunk(n), 0))
    v_specs = [pl.BlockSpec(sp.block_shape, lambda n, im=sp.index_map: (chunk(n), im(0)[1])) for _, sp in vwin]
    hmat = (RET_HEADS, RET_HEAD_DIM, RET_HEAD_DIM)
    return pl.pallas_call(
        body, name=name, grid=(nc,),
        in_specs=[pl.BlockSpec(memory_space=pltpu.SMEM), rows, rows, rows,
                  pl.BlockSpec((None,) + hmat, lambda n: (chunk(n), 0, 0, 0))] + v_specs,
        out_specs=[rows, rows, rows, pl.BlockSpec((RET_HEADS, 8, LANES), lambda n: (0, 0, 0))],
        out_shape=[jax.ShapeDtypeStruct((s, RET_W), F32)] * 3 + [jax.ShapeDtypeStruct((RET_HEADS, 8, LANES), F32)],
        scratch_shapes=[pltpu.VMEM(hmat, F32), pltpu.VMEM(hmat, F32)],
        compiler_params=_cparams(("arbitrary",)),
    )(logits, q, k, dy, states, *[a for a, _ in vwin])


def _local_step(x, p, target, w, small):
    s = x.shape[0]
    tabs = _rope_tables(s, ATTN_HEAD_DIM) + _rope_tables(s, RET_HEAD_DIM)
    g_mix, g_mlp, g_ple = small["mix_norm"][None, :], small["mlp_norm"][None, :], small["ple_norm"][None, :]
    g_final, g_ret = small["final_norm"][None, :], small["ret_norm_gain"][None, :]
    gq_w = jnp.tile(small["attn_q_norm"], ATTN_HEADS)[None, :]
    gk_w = jnp.tile(small["attn_k_norm"], ATTN_KV_HEADS)[None, :]
    logits = small["ret_decay_logit"]

    hb = _stage_norm_in(x, g_mix)
    proj = _mm("in_proj", hb, w["w_in"], tm=512, tn=IN_W // 2, tk=1024)
    q_ct, k_rows, k_ct, v_rows, v_ct, rq, rk = _stage_qkv(proj, tabs, gq_w, gk_w)
    o_ct, lse = _attn_fwd(q_ct, k_rows, v_ct)
    ry_f, st_f = _ret_fwd("ret_fwd_f", logits, rq, rk, proj, False)
    ry_b, st_b = _ret_fwd("ret_fwd_b", logits, rq, rk, proj, True)
    rz, attn_rows = _stage_mix_post(ry_f, ry_b, proj, o_ct, g_ret)
    a_out = _mm("attn_o", attn_rows, w["w_attn_o"], tm=1024, tn=1024, tk=512)
    r_out = _mm("ret_o", rz, w["w_ret_o"], tm=1024, tn=1024, tk=512)
    merged = _stage_merge(proj, a_out, r_out)

    def epi_res_norm(acc, e, c):
        xr = e[0][...] + acc
        return xr, _rms_fwd(xr, c[0][...])

    x1, hm = _mm("out_proj", merged, w["w_out"], tm=512, tn=1024, tk=1024, out_dtypes=(F32, BF16),
                 epi=epi_res_norm, epi_ins=(x,), consts=(g_mlp,))

    def epi_relu2(acc, e, c):
        r = jnp.maximum(acc, 0.0)
        return acc, r * r

    u, act = _mm("mlp_up", hm, w["w_up"], tm=512, tn=2048, tk=1024, out_dtypes=(F32, BF16), epi=epi_relu2)
    x2, hp = _mm("mlp_down", act, w["w_down"], tm=512, tn=1024, tk=2048, out_dtypes=(F32, BF16),
                 epi=epi_res_norm, epi_ins=(x1,), consts=(g_ple,))
    zg = _mm("ple_gate", hp, w["w_ple_gate"], tm=1024, tn=1024, tk=1024)
    pe = _mm("ple_emb", p, w["w_ple"], tm=1024, tn=1024, tk=256)
    dx3, dzg, dpe, loss_cols, g_final_p = _stage_head(zg, pe, x2, target, g_final)
    loss_sum = 0.5 / D_MODEL * jnp.sum(loss_cols)

    gw = {}
    gw["w_ple"] = _mm("g_w_ple", p, dpe, ta=True, tm=256, tn=1024, tk=2048)
    gw["w_ple_gate"] = _mm("g_w_ple_gate", hp, dzg, ta=True, tm=1024, tn=1024, tk=2048)
    dhp = _mm("d_hp", dzg, w["w_ple_gate"], tb=True, tm=1024, tn=1024, tk=1024)
    dx2, g_ple_p = _stage_norm_bwd("ple_norm_bwd", dhp, x2, dx3, g_ple)

    def epi_relu2_bwd(acc, e, c):
        return (acc * (2.0 * jnp.maximum(e[0][...], 0.0)),)

    du = _mm("d_u", dx2, w["w_down"], tb=True, tm=512, tn=2048, tk=1024, out_dtypes=(BF16,), epi=epi_relu2_bwd, epi_ins=(u,))
    gw["w_down"] = _mm("g_w_down", act, dx2, ta=True, tm=1024, tn=1024, tk=2048)
    gw["w_up"] = _mm("g_w_up", hm, du, ta=True, tm=1024, tn=1024, tk=2048)
    dhm = _mm("d_hm", du, w["w_up"], tb=True, tm=512, tn=1024, tk=2048)
    dx1, g_mlp_p = _stage_norm_bwd("mlp_norm_bwd", dhm, x1, dx2, g_mlp)
    dmerged = _mm("d_merged", dx1, w["w_out"], tb=True, tm=1024, tn=1024, tk=1024)
    gw["w_out"] = _mm("g_w_out", merged, dx1, ta=True, tm=1024, tn=1024, tk=2048)
    dao, dro, dga, dgr = _stage_merge_bwd(proj, dmerged, a_out, r_out)
    gw["w_attn_o"] = _mm("g_w_attn_o", attn_rows, dao, ta=True, tm=512, tn=1024, tk=2048)
    gw["w_ret_o"] = _mm("g_w_ret_o", rz, dro, ta=True, tm=512, tn=1024, tk=2048)
    dattn = _mm("d_attn", dao, w["w_attn_o"], tb=True, tm=1024, tn=512, tk=1024)
    drz = _mm("d_rz", dro, w["w_ret_o"], tb=True, tm=1024, tn=512, tk=1024)
    do_ct, delta, dry, drg, g_ret_p = _stage_mix_post_bwd(dattn, attn_rows, drz, ry_f, ry_b, proj, g_ret)
    dq_f, dk_f, dv_f, dl_f = _ret_bwd("ret_bwd_f", logits, rq, rk, proj, dry, st_f, False)
    dq_b, dk_b, dv_b, dl_b = _ret_bwd("ret_bwd_b", logits, rq, rk, proj, dry, st_b, True)
    dq_ct, dk8, dv8 = _attn_bwd(q_ct, do_ct, lse, delta, k_rows, v_rows, k_ct)
    dproj, gq_p, gk_p = _stage_dproj(proj, dq_ct, dk8, dv8, (dq_f, dk_f, dv_f, dq_b, dk_b, dv_b), drg, dga, dgr, tabs, gq_w, gk_w)
    gw["w_in"] = _mm("g_w_in", hb, dproj, ta=True, tm=512, tn=IN_W // 2, tk=1024)
    dh = _mm("d_h", dproj, w["w_in"], tb=True, tm=512, tn=1024, tk=IN_W // 2)
    grad_x, g_mix_p = _stage_norm_bwd("mix_norm_bwd", dh, x, dx1, g_mix)

    gs = {
        "mix_norm": g_mix_p[0], "mlp_norm": g_mlp_p[0], "ple_norm": g_ple_p[0], "final_norm": g_final_p[0],
        "ret_norm_gain": g_ret_p[0],
        "attn_q_norm": jnp.sum(gq_p[0].reshape(ATTN_HEADS, ATTN_HEAD_DIM), axis=0),
        "attn_k_norm": jnp.sum(gk_p[0].reshape(ATTN_KV_HEADS, ATTN_HEAD_DIM), axis=0),
        "ret_decay_logit": jnp.stack([dl_f[:, 0, 0], dl_b[:, 0, 0]]),
    }
    return loss_sum, grad_x, gw, gs


PACK_COLS = 1024
N_CHIPS = 4
HALF_ROWS = 2048


def _pack_shard(parts):
    return jnp.concatenate([parts[n].reshape(-1, PACK_COLS) for n, _ in BIG], axis=0)


def _unpack_shard(slab, shapes):
    out, r = {}, 0
    for n, _ in BIG:
        rows = math.prod(shapes[n]) // PACK_COLS
        out[n] = slab[r:r + rows].reshape(shapes[n])
        r += rows
    return out


def _shard_of(full, axis, sidx):
    size = full.shape[axis] // N_CHIPS
    return lax.slice_in_dim(full, sidx * size, (sidx + 1) * size, axis=axis)


def _position():
    x, y, c = lax.axis_index("x"), lax.axis_index("y"), lax.axis_index("c")
    return x, y, c


def _other_chips(x, y):
    return [(1 - x, y), (x, 1 - y), (1 - x, 1 - y)]


ANY = pl.BlockSpec(memory_space=pl.ANY)


def _gather_weights(slab):
    rows = slab.shape[0]
    half = rows // 2

    def body(in_ref, out_ref, send_sems, recv_sems):
        x, y, c = _position()
        chips = _other_chips(x, y)

        def piece(chip, core):
            return out_ref.at[2 * chip[0] + chip[1], pl.ds(core * half, half), :]

        def copy(k, chip, core, to, src=None):
            return pltpu.make_async_remote_copy(
                src_ref=piece(chip, core) if src is None else src, dst_ref=piece(chip, core),
                send_sem=send_sems.at[k], recv_sem=recv_sems.at[k], device_id=to, device_id_type=MESH)

        first = [copy(j, (x, y), c, (*chip, c), src=in_ref.at[pl.ds(c * half, half), :]) for j, chip in enumerate(chips)]
        for cp in first:
            cp.start()
        passed = [copy(3 + j, chip, c, (x, y, 1 - c)) for j, chip in enumerate(chips)]
        for j, chip in enumerate(chips):
            copy(j, chip, c, (x, y, c)).wait_recv()
            passed[j].start()
        for j, chip in enumerate(chips):
            copy(3 + j, chip, 1 - c, (x, y, c)).wait_recv()
        for cp in first + passed:
            cp.wait_send()

    return pl.pallas_call(
        body, name="gather_weights", in_specs=[ANY], out_specs=ANY,
        out_shape=jax.ShapeDtypeStruct((N_CHIPS,) + slab.shape, slab.dtype),
        scratch_shapes=[pltpu.SemaphoreType.DMA((6,)), pltpu.SemaphoreType.DMA((6,))],
    )(slab)


def _exchange_halves(g):
    def body(g_ref, out_ref, send_sem, recv_sem):
        x, y, c = _position()
        cp = pltpu.make_async_remote_copy(src_ref=g_ref.at[1 - c], dst_ref=out_ref, send_sem=send_sem, recv_sem=recv_sem,
                                          device_id=(x, y, 1 - c), device_id_type=MESH)
        cp.start()
        cp.wait()

    return pl.pallas_call(
        body, name="exchange_halves", in_specs=[ANY], out_specs=ANY,
        out_shape=jax.ShapeDtypeStruct(g.shape[1:], g.dtype),
        scratch_shapes=[pltpu.SemaphoreType.DMA, pltpu.SemaphoreType.DMA],
    )(g)


def _add_my_half(g, r1, c_idx):
    tr = 256
    nt = g.shape[2] // tr

    def body(c_ref, g_ref, r_ref, o_ref):
        o_ref[...] = g_ref[...] + r_ref[...]

    blk = (None, tr, PACK_COLS)
    return pl.pallas_call(
        body, name="add_my_half",
        grid_spec=pltpu.PrefetchScalarGridSpec(
            num_scalar_prefetch=1, grid=(N_CHIPS, nt),
            in_specs=[pl.BlockSpec((None,) + blk, lambda s, i, c_ref: (c_ref[0], s, i, 0)),
                      pl.BlockSpec(blk, lambda s, i, c_ref: (s, i, 0))],
            out_specs=pl.BlockSpec(blk, lambda s, i, c_ref: (s, i, 0))),
        out_shape=jax.ShapeDtypeStruct(g.shape[1:], F32),
        compiler_params=_cparams(("parallel", "parallel")),
    )(c_idx, g, r1)


def _scatter_to_chips(part):
    def body(p_ref, out_ref, send_sems, recv_sems):
        x, y, c = _position()
        chips = _other_chips(x, y)
        sends = [pltpu.make_async_remote_copy(
            src_ref=p_ref.at[2 * chip[0] + chip[1]], dst_ref=out_ref.at[j], send_sem=send_sems.at[j], recv_sem=recv_sems.at[j],
            device_id=(*chip, c), device_id_type=MESH) for j, chip in enumerate(chips)]
        for cp in sends:
            cp.start()
        for cp in sends:
            cp.wait()

    return pl.pallas_call(
        body, name="scatter_to_chips", in_specs=[ANY], out_specs=ANY,
        out_shape=jax.ShapeDtypeStruct((N_CHIPS - 1,) + part.shape[1:], part.dtype),
        scratch_shapes=[pltpu.SemaphoreType.DMA((3,)), pltpu.SemaphoreType.DMA((3,))],
    )(part)


def _sum_chips(part, r2, chip_idx):
    tr = 256

    def body(c_ref, p_ref, r_ref, o_ref):
        o_ref[...] = ((p_ref[...] + r_ref[0]) + r_ref[1]) + r_ref[2]

    return pl.pallas_call(
        body, name="sum_chips",
        grid_spec=pltpu.PrefetchScalarGridSpec(
            num_scalar_prefetch=1, grid=(r2.shape[1] // tr,),
            in_specs=[pl.BlockSpec((None, tr, PACK_COLS), lambda i, c_ref: (c_ref[0], i, 0)),
                      pl.BlockSpec((N_CHIPS - 1, tr, PACK_COLS), lambda i, c_ref: (0, i, 0))],
            out_specs=pl.BlockSpec((tr, PACK_COLS), lambda i, c_ref: (i, 0))),
        out_shape=jax.ShapeDtypeStruct(r2.shape[1:], F32),
        compiler_params=_cparams(("parallel",)),
    )(chip_idx, part, r2)


def _join_halves(red):
    def body(r_ref, out_ref, send_sem, recv_sem):
        x, y, c = _position()
        cp = pltpu.make_async_remote_copy(src_ref=r_ref, dst_ref=out_ref, send_sem=send_sem, recv_sem=recv_sem,
                                          device_id=(x, y, 1 - c), device_id_type=MESH)
        cp.start()
        cp.wait()

    return pl.pallas_call(
        body, name="join_halves", in_specs=[ANY], out_specs=ANY,
        out_shape=jax.ShapeDtypeStruct(red.shape, red.dtype),
        scratch_shapes=[pltpu.SemaphoreType.DMA, pltpu.SemaphoreType.DMA],
    )(red)


def _adamw_math(w, g, m, v):
    m = ADAM_B1 * m + (1.0 - ADAM_B1) * g
    v = ADAM_B2 * v + (1.0 - ADAM_B2) * (g * g)
    m_hat = m / (1.0 - ADAM_B1 ** ADAM_STEP)
    v_hat = v / (1.0 - ADAM_B2 ** ADAM_STEP)
    delta = -ADAM_LR * (m_hat / (jnp.sqrt(v_hat) + ADAM_EPS) + ADAM_WD * w)
    return delta, m, v


def _adamw(w, g, m, v):
    tr = 256

    def body(w_ref, g_ref, m_ref, v_ref, d_ref, nm_ref, nv_ref):
        d_ref[...], nm_ref[...], nv_ref[...] = _adamw_math(w_ref[...], g_ref[...], m_ref[...], v_ref[...])

    blk = pl.BlockSpec((tr, PACK_COLS), lambda i: (i, 0))
    return pl.pallas_call(
        body, name="adamw", grid=(w.shape[0] // tr,), in_specs=[blk] * 4, out_specs=[blk] * 3,
        out_shape=[jax.ShapeDtypeStruct(w.shape, F32)] * 3, compiler_params=_cparams(("parallel",)),
    )(w, g, m, v)


def _small_step(gpk, wpk, mpk, vpk):
    row, col, width = SMALL["ret_decay_logit"]

    def body(g_ref, w_ref, m_ref, v_ref, og_ref, od_ref, om_ref, ov_ref, gbuf, send_sems, recv_sems):
        x, y, c = _position()
        me = 4 * x + 2 * y + c
        gbuf[me] = g_ref[...]
        sends = []
        for k in range(1, 8):
            to = (x ^ (k >> 2), y ^ ((k >> 1) & 1), c ^ (k & 1))
            cp = pltpu.make_async_remote_copy(src_ref=g_ref, dst_ref=gbuf.at[me], send_sem=send_sems.at[k - 1],
                                              recv_sem=recv_sems.at[k - 1], device_id=to, device_id_type=MESH)
            cp.start()
            sends.append(cp)
        for k in range(1, 8):
            frm = me ^ k
            pltpu.make_async_remote_copy(src_ref=g_ref, dst_ref=gbuf.at[frm], send_sem=send_sems.at[k - 1],
                                         recv_sem=recv_sems.at[k - 1], device_id=(x, y, c), device_id_type=MESH).wait_recv()
        for cp in sends:
            cp.wait_send()
        tot = gbuf[0]
        for d in range(1, 8):
            tot = tot + gbuf[d]
        w = w_ref[...]
        r_i = lax.broadcasted_iota(jnp.int32, w.shape, 0)
        c_i = lax.broadcasted_iota(jnp.int32, w.shape, 1)
        is_logit = (r_i == row) & (c_i >= col) & (c_i < col + width)
        g = jnp.where(is_logit, tot * _sigmoid(-w), tot)
        og_ref[...] = g
        od_ref[...], om_ref[...], ov_ref[...] = _adamw_math(w, g, m_ref[...], v_ref[...])

    vm = pl.BlockSpec(memory_space=pltpu.VMEM)
    shp = jax.ShapeDtypeStruct(gpk.shape, F32)
    return pl.pallas_call(
        body, name="small_step", in_specs=[vm] * 4, out_specs=[vm] * 4, out_shape=[shp] * 4,
        scratch_shapes=[pltpu.VMEM((8,) + gpk.shape, F32), pltpu.SemaphoreType.DMA((7,)), pltpu.SemaphoreType.DMA((7,))],
    )(gpk, wpk, mpk, vpk)


def _pack_small(parts):
    rows = [[] for _ in range(SMALL_ROWS)]
    for n, (r, col, width) in sorted(SMALL.items(), key=lambda kv: (kv[1][0], kv[1][1])):
        rows[r].append((col, parts[n].reshape(-1).astype(F32)))
    out = []
    for r in range(SMALL_ROWS):
        segs, pos = [], 0
        for col, vec in rows[r]:
            assert col == pos
            segs.append(vec)
            pos += vec.shape[0]
        if pos < PACK_COLS:
            segs.append(jnp.zeros((PACK_COLS - pos,), F32))
        out.append(jnp.concatenate(segs))
    return jnp.stack(out)


def _unpack_small(pk, shapes):
    return {n: pk[r, col:col + width].reshape(shapes[n]) for n, (r, col, width) in SMALL.items()}


WEIGHTS = ("mix_norm", "w_in", "attn_q_norm", "attn_k_norm", "ret_decay_logit", "ret_norm_gain", "w_attn_o", "w_ret_o", "w_out",
           "mlp_norm", "w_up", "w_down", "ple_norm", "w_ple_gate", "w_ple", "final_norm")


def kernel(x, p, mix_norm, w_in, attn_q_norm, attn_k_norm, ret_decay_logit, ret_norm_gain, w_attn_o, w_ret_o, w_out, mlp_norm, w_up, w_down, ple_norm, w_ple_gate, w_ple, final_norm, loss_target, m_mix_norm, m_w_in, m_attn_q_norm, m_attn_k_norm, m_ret_decay_logit, m_ret_norm_gain, m_w_attn_o, m_w_ret_o, m_w_out, m_mlp_norm, m_w_up, m_w_down, m_ple_norm, m_w_ple_gate, m_w_ple, m_final_norm, v_mix_norm, v_w_in, v_attn_q_norm, v_attn_k_norm, v_ret_decay_logit, v_ret_norm_gain, v_w_attn_o, v_w_ret_o, v_w_out, v_mlp_norm, v_w_up, v_w_down, v_ple_norm, v_w_ple_gate, v_w_ple, v_final_norm):
    args = dict(locals())
    wts = {n: args[n] for n in WEIGHTS}
    ms = {n: args["m_" + n] for n in WEIGHTS}
    vs = {n: args["v_" + n] for n in WEIGHTS}
    shapes = {n: wts[n].shape for n in WEIGHTS}
    big_names = [n for n, _ in BIG]
    xi, yi, ci = _position()
    c_idx = ci.astype(jnp.int32).reshape(1)

    slab_w = _pack_shard({n: wts[n][0] for n in big_names})
    chip_idx = (2 * xi + yi).astype(jnp.int32)
    slab_b = slab_w.astype(BF16)
    gathered = lax.dynamic_update_slice(_gather_weights(slab_b), slab_b[None], (chip_idx, 0, 0))
    full = {}
    for n, axis in BIG:
        per_chip = [_unpack_shard(gathered[k], {m_: shapes[m_][1:] for m_ in big_names})[n] for k in range(N_CHIPS)]
        full[n] = jnp.concatenate(per_chip, axis=axis)
    small = {n: wts[n].reshape(wts[n].shape[1:] if wts[n].ndim > 1 else wts[n].shape) for n in SMALL}

    loss_part, grad_x, gw, gs = _local_step(x[0], p[0, 0], loss_target[0], full, small)
    loss = lax.psum(loss_part, ("x", "y", "c"))

    slabs = jnp.stack([_pack_shard({n: _shard_of(gw[n], axis, k) for n, axis in BIG}) for k in range(N_CHIPS)])
    halves = slabs.reshape(N_CHIPS, 2, HALF_ROWS, PACK_COLS).transpose(1, 0, 2, 3)
    chip_part = _add_my_half(halves, _exchange_halves(halves), c_idx)
    mine = _sum_chips(chip_part, _scatter_to_chips(chip_part), chip_idx.reshape(1))
    both = jnp.stack([mine, _join_halves(mine)])
    reduced = jnp.where(ci == 0, both, both[::-1]).reshape(2 * HALF_ROWS, PACK_COLS)
    delta_b, newm_b, newv_b = _adamw(slab_w, reduced, _pack_shard({n: ms[n][0] for n in big_names}), _pack_shard({n: vs[n][0] for n in big_names}))
    shard_shapes = {n: shapes[n] for n in big_names}
    big_out = [_unpack_shard(a, shard_shapes) for a in (reduced, delta_b, newm_b, newv_b)]

    sm_out = _small_step(_pack_small(gs), _pack_small({n: wts[n] for n in SMALL}), _pack_small({n: ms[n] for n in SMALL}),
                         _pack_small({n: vs[n] for n in SMALL}))
    small_out = [_unpack_small(a, {n: shapes[n] for n in SMALL}) for a in sm_out]

    outs = [loss, grad_x[None]]
    for kind in range(4):
        for n in WEIGHTS:
            outs.append(small_out[kind][n] if n in SMALL else big_out[kind][n])
    return tuple(outs)
```

```python
import functools
import math

import jax
import jax.numpy as jnp
from jax import lax
from jax.experimental import pallas as pl
from jax.experimental.pallas import tpu as pltpu

F32 = jnp.float32
BF16 = jnp.bfloat16
MESH = pl.DeviceIdType.MESH

D_MODEL = 1024
PLE_DIM = 256
GRID_W = 64
ATTN_HEAD_DIM = 64
ATTN_HEADS = 8
ATTN_KV_HEADS = 2
ATTN_GROUP = ATTN_HEADS // ATTN_KV_HEADS
RET_HEAD_DIM = 128
RET_HEADS = 4
ATTN_Q_W = 512
ATTN_KV_W = 128
RET_W = 512
IN_W = 4864
D_FF = 4096
RET_CHUNK = 128
ROPE_THETA = 10000.0
NORM_EPS = 1e-6
GN_EPS = 1e-5
ATTN_SCALE = ATTN_HEAD_DIM ** -0.5
LOG2E = math.log2(math.e)
Q_FOLD = ATTN_SCALE * LOG2E
RET_SCALE = RET_HEAD_DIM ** -0.5

C_AQ, C_AK, C_AV, C_RQ, C_RK, C_RV, C_RG, C_GA, C_GR = 0, 512, 640, 768, 1280, 1792, 2304, 2816, 3840

ADAM_LR = 0.001
ADAM_B1 = 0.9
ADAM_B2 = 0.999
ADAM_EPS = 1e-08
ADAM_WD = 0.01
ADAM_STEP = 10

LANES = 128
VMEM_LIMIT = 56 << 20
SEQ_TILE = 512

BIG = (("w_in", 1), ("w_attn_o", 1), ("w_ret_o", 1), ("w_out", 0), ("w_up", 1), ("w_down", 0), ("w_ple_gate", 0), ("w_ple", 1))
SMALL_ROWS = 8
SMALL = {"mix_norm": (0, 0, 1024), "mlp_norm": (1, 0, 1024), "ple_norm": (2, 0, 1024), "final_norm": (3, 0, 1024),
         "ret_norm_gain": (4, 0, 512), "attn_q_norm": (4, 512, 64), "attn_k_norm": (4, 576, 64), "ret_decay_logit": (4, 640, 8)}


def _seq_tile(s):
    return min(SEQ_TILE, s // 2)


def _cparams(sem=None, vmem=VMEM_LIMIT):
    return pltpu.CompilerParams(dimension_semantics=sem, vmem_limit_bytes=vmem)


def _mm(name, a, b, *, ta=False, tb=False, tm, tn, tk, out_dtypes=(F32,), epi=None, epi_ins=(), consts=()):
    if ta:
        kdim, m = a.shape
    else:
        m, kdim = a.shape
    n = b.shape[0] if tb else b.shape[1]
    tm, tn, tk = min(tm, m), min(tn, n), min(tk, kdim)
    assert m % tm == 0 and n % tn == 0 and kdim % tk == 0, (name, m, n, kdim, tm, tn, tk)
    nk = kdim // tk
    n_e, n_c, n_o = len(epi_ins), len(consts), len(out_dtypes)

    def body(*refs):
        a_ref, b_ref = refs[0], refs[1]
        e_refs = refs[2:2 + n_e]
        c_refs = refs[2 + n_e:2 + n_e + n_c]
        o_refs = refs[2 + n_e + n_c:2 + n_e + n_c + n_o]
        acc_ref = refs[2 + n_e + n_c + n_o] if nk > 1 else None
        k = pl.program_id(2)
        av = a_ref[...].astype(BF16)
        bv = b_ref[...].astype(BF16)
        dims = (((0,) if ta else (1,), (1,) if tb else (0,)), ((), ()))
        part = lax.dot_general(av, bv, dims, preferred_element_type=F32)

        def finish(acc):
            vals = epi(acc, e_refs, c_refs) if epi is not None else (acc,)
            for o_ref, v in zip(o_refs, vals):
                o_ref[...] = v.astype(o_ref.dtype)

        if nk == 1:
            finish(part)
        else:
            @pl.when(k == 0)
            def _():
                acc_ref[...] = part

            @pl.when(k > 0)
            def _():
                acc_ref[...] += part

            @pl.when(k == nk - 1)
            def _():
                finish(acc_ref[...])

    a_spec = pl.BlockSpec((tk, tm), lambda i, j, k: (k, i)) if ta else pl.BlockSpec((tm, tk), lambda i, j, k: (i, k))
    b_spec = pl.BlockSpec((tn, tk), lambda i, j, k: (j, k)) if tb else pl.BlockSpec((tk, tn), lambda i, j, k: (k, j))
    o_spec = pl.BlockSpec((tm, tn), lambda i, j, k: (i, j))
    c_specs = [pl.BlockSpec(c.shape, lambda i, j, k, nd=c.ndim: (0,) * nd) for c in consts]
    outs = pl.pallas_call(
        body, name=name,
        grid=(m // tm, n // tn, nk),
        in_specs=[a_spec, b_spec] + [o_spec] * n_e + c_specs,
        out_specs=[o_spec] * n_o,
        out_shape=[jax.ShapeDtypeStruct((m, n), dt) for dt in out_dtypes],
        scratch_shapes=[pltpu.VMEM((tm, tn), F32)] if nk > 1 else [],
        compiler_params=_cparams(("parallel", "parallel", "arbitrary")),
    )(a, b, *epi_ins, *consts)
    return outs[0] if n_o == 1 else outs


def _rows(arr, tr):
    return (arr, pl.BlockSpec((tr, arr.shape[1]), lambda i: (i, 0)))


def _win(arr, tr, start, width):
    bw = math.gcd(start, width) if start else width
    assert bw % LANES == 0
    return [(arr, pl.BlockSpec((tr, bw), lambda i, cb=start // bw + p: (i, cb))) for p in range(width // bw)]


def _ct(arr):
    return (arr, pl.BlockSpec((None,) + arr.shape[1:], lambda i: (i, 0, 0)))


def _whole(arr):
    return (arr, pl.BlockSpec(arr.shape, lambda i, nd=arr.ndim: (0,) * nd))


def _cat(refs):
    vals = [r[...] for r in refs]
    return vals[0] if len(vals) == 1 else jnp.concatenate(vals, axis=1)


def _seqtiled(name, fn, n_tiles, ins, outs, acc_widths=()):
    n_i, n_o, n_a = len(ins), len(outs), len(acc_widths)

    def body(*refs):
        i_refs, o_refs, a_refs = refs[:n_i], refs[n_i:n_i + n_o], refs[n_i + n_o:]
        if n_a:
            @pl.when(pl.program_id(0) == 0)
            def _():
                for r in a_refs:
                    r[...] = jnp.zeros(r.shape, F32)
        fn(list(i_refs), list(o_refs), list(a_refs))

    res = pl.pallas_call(
        body, name=name, grid=(n_tiles,),
        in_specs=[s for _, s in ins],
        out_specs=[s for _, _, s in outs] + [pl.BlockSpec((8, w), lambda i: (0, 0)) for w in acc_widths],
        out_shape=[jax.ShapeDtypeStruct(sh, dt) for sh, dt, _ in outs] + [jax.ShapeDtypeStruct((8, w), F32) for w in acc_widths],
        compiler_params=_cparams(("arbitrary",)),
    )(*[a for a, _ in ins])
    return res


def _acc_add(acc_ref, val):
    acc_ref[0:1, :] += jnp.sum(val, axis=0, keepdims=True)


def _out_rows(s, w, dt, tr):
    return ((s, w), dt, pl.BlockSpec((tr, w), lambda i: (i, 0)))


def _out_ct(s, w, dt, t):
    return ((s // t, w, t), dt, pl.BlockSpec((None, w, t), lambda i: (i, 0, 0)))


def _rms_fwd(x, gain):
    r = lax.rsqrt(jnp.mean(x * x, axis=-1, keepdims=True) + NORM_EPS)
    return x * r * gain


def _rms_bwd(dy, x, gain):
    r = lax.rsqrt(jnp.mean(x * x, axis=-1, keepdims=True) + NORM_EPS)
    xn = x * r
    dyg = dy * gain
    dx = r * (dyg - xn * jnp.mean(dyg * xn, axis=-1, keepdims=True))
    return dx, dy * xn


def _seg_mean(y, hd):
    w = y.shape[1]
    pieces = []
    for s in range(0, w, LANES):
        v = y[:, s:s + LANES]
        tot = jnp.sum(v, axis=1, keepdims=True)
        if hd == LANES:
            pieces.append(jnp.broadcast_to(tot, v.shape))
        else:
            low = lax.broadcasted_iota(jnp.int32, v.shape, 1) < hd
            lo = jnp.sum(jnp.where(low, v, 0.0), axis=1, keepdims=True)
            pieces.append(jnp.where(low, lo, tot - lo))
    out = pieces[0] if len(pieces) == 1 else jnp.concatenate(pieces, axis=1)
    return out * (1.0 / hd)


def _tile_lanes(t, w):
    return t if w == t.shape[1] else jnp.concatenate([t] * (w // t.shape[1]), axis=1)


def _swap_halves(x, hd):
    w = x.shape[1]
    half = hd // 2
    lane = lax.broadcasted_iota(jnp.int32, x.shape, 1)
    return jnp.where((lane % hd) < half, pltpu.roll(x, w - half, 1), pltpu.roll(x, half, 1))


def _rope(x, cos, sin_signed, hd):
    w = x.shape[1]
    return x * _tile_lanes(cos, w) + _swap_halves(x, hd) * _tile_lanes(sin_signed, w)


def _rope_t(dy, cos, sin_signed, hd):
    w = dy.shape[1]
    return dy * _tile_lanes(cos, w) + _swap_halves(dy * _tile_lanes(sin_signed, w), hd)


def _headnorm_fwd(x, gain_w, hd):
    r = lax.rsqrt(_seg_mean(x * x, hd) + NORM_EPS)
    return x * r * gain_w


def _headnorm_bwd(dy, x, gain_w, hd):
    r = lax.rsqrt(_seg_mean(x * x, hd) + NORM_EPS)
    xn = x * r
    dyg = dy * gain_w
    return r * (dyg - xn * _seg_mean(dyg * xn, hd)), dy * xn


def _sigmoid(x):
    return 1.0 / (1.0 + jnp.exp(-x))


def _rope_tables(seq_len, head_dim):
    rows = seq_len // GRID_W
    n_axis = head_dim // 4
    freqs = ROPE_THETA ** (-jnp.arange(n_axis, dtype=F32) / n_axis)
    row = jnp.repeat(jnp.arange(rows, dtype=F32), GRID_W)
    col = jnp.tile(jnp.arange(GRID_W, dtype=F32), rows)
    ang = jnp.concatenate([row[:, None] * freqs, col[:, None] * freqs], axis=-1)
    cos, sin = jnp.cos(ang), jnp.sin(ang)
    reps = LANES // head_dim
    return jnp.tile(jnp.concatenate([cos, cos], axis=-1), (1, reps)), jnp.tile(jnp.concatenate([-sin, sin], axis=-1), (1, reps))


def _stage_norm_in(x, gain):
    s = x.shape[0]
    tr = min(SEQ_TILE, s)

    def fn(i, o, a):
        o[0][...] = _rms_fwd(i[0][...], i[1][...]).astype(BF16)

    return _seqtiled("norm_in", fn, s // tr, [_rows(x, tr), _whole(gain)], [_out_rows(s, D_MODEL, BF16, tr)])[0]


def _stage_qkv(proj, tabs, gq_w, gk_w):
    s = proj.shape[0]
    t = _seq_tile(s)
    ca, sa, cr, sr = tabs
    ins = (_win(proj, t, C_AQ, ATTN_Q_W) + _win(proj, t, C_AK, ATTN_KV_W) + _win(proj, t, C_AV, ATTN_KV_W)
           + _win(proj, t, C_RQ, RET_W) + _win(proj, t, C_RK, RET_W)
           + [_rows(ca, t), _rows(sa, t), _rows(cr, t), _rows(sr, t), _whole(gq_w), _whole(gk_w)])

    def fn(i, o, a):
        aq, ak, av = i[0][...], i[1][...], i[2][...]
        rq, rk = _cat(i[3:5]), _cat(i[5:7])
        ca_, sa_, cr_, sr_ = i[7][...], i[8][...], i[9][...], i[10][...]
        qr = _rope(_headnorm_fwd(aq, i[11][...], ATTN_HEAD_DIM), ca_, sa_, ATTN_HEAD_DIM) * Q_FOLD
        kr = _rope(_headnorm_fwd(ak, i[12][...], ATTN_HEAD_DIM), ca_, sa_, ATTN_HEAD_DIM)
        qt = qr.T.astype(BF16)
        zeros = jnp.zeros((ATTN_HEAD_DIM, t), BF16)
        for h in range(ATTN_HEADS):
            g = h // ATTN_GROUP
            blk = qt[h * ATTN_HEAD_DIM:(h + 1) * ATTN_HEAD_DIM, :]
            o[0][h * LANES + g * ATTN_HEAD_DIM:h * LANES + (g + 1) * ATTN_HEAD_DIM, :] = blk
            o[0][h * LANES + (1 - g) * ATTN_HEAD_DIM:h * LANES + (2 - g) * ATTN_HEAD_DIM, :] = zeros
        o[1][...] = kr.astype(BF16)
        o[2][...] = kr.T.astype(BF16)
        o[3][...] = av.astype(BF16)
        o[4][...] = av.T.astype(BF16)
        o[5][...] = _rope(rq, cr_, sr_, RET_HEAD_DIM) * RET_SCALE
        o[6][...] = _rope(rk, cr_, sr_, RET_HEAD_DIM)

    outs = [_out_ct(s, ATTN_HEADS * LANES, BF16, t), _out_rows(s, ATTN_KV_W, BF16, t), _out_ct(s, ATTN_KV_W, BF16, t),
            _out_rows(s, ATTN_KV_W, BF16, t), _out_ct(s, ATTN_KV_W, BF16, t), _out_rows(s, RET_W, F32, t), _out_rows(s, RET_W, F32, t)]
    return _seqtiled("qkv_prep", fn, s // t, ins, outs)


def _groupnorm_gate(ry, rg, gain):
    mu = _seg_mean(ry, RET_HEAD_DIM)
    d = ry - mu
    rs = lax.rsqrt(_seg_mean(d * d, RET_HEAD_DIM) + GN_EPS)
    return d * rs, rs, _sigmoid(rg)


def _stage_mix_post(ry_f, ry_b, proj, o_ct, gain):
    s = proj.shape[0]
    t = _seq_tile(s)
    ins = [_rows(ry_f, t), _rows(ry_b, t)] + _win(proj, t, C_RG, RET_W) + [_ct(o_ct), _whole(gain)]

    def fn(i, o, a):
        ry = i[0][...] + i[1][...]
        rg = _cat(i[2:4])
        gn, _, sg = _groupnorm_gate(ry, rg, None)
        o[0][...] = (gn * i[5][...] * (rg * sg)).astype(BF16)
        o[1][...] = i[4][...].astype(F32).T.astype(BF16)

    return _seqtiled("mix_post", fn, s // t, ins, [_out_rows(s, RET_W, BF16, t), _out_rows(s, ATTN_Q_W, BF16, t)])


def _stage_merge(proj, a_out, r_out):
    s = proj.shape[0]
    tr = min(SEQ_TILE, s)
    ins = _win(proj, tr, C_GA, D_MODEL) + _win(proj, tr, C_GR, D_MODEL) + [_rows(a_out, tr), _rows(r_out, tr)]
    na = len(_win(proj, tr, C_GA, D_MODEL))

    def fn(i, o, a):
        ga, gr = _cat(i[:na]), _cat(i[na:2 * na])
        o[0][...] = (_sigmoid(ga) * i[2 * na][...] + _sigmoid(gr) * i[2 * na + 1][...]).astype(BF16)

    return _seqtiled("merge", fn, s // tr, ins, [_out_rows(s, D_MODEL, BF16, tr)])[0]


def _stage_head(zg, pe, x2, target, g_final):
    s = x2.shape[0]
    tr = min(SEQ_TILE // 2, s)
    ins = [_rows(zg, tr), _rows(pe, tr), _rows(x2, tr), _rows(target, tr), _whole(g_final)]

    def fn(i, o, a):
        gt = _sigmoid(i[0][...])
        pe_ = i[1][...]
        x3 = i[2][...] + gt * pe_
        gf = i[4][...]
        r3 = lax.rsqrt(jnp.mean(x3 * x3, axis=-1, keepdims=True) + NORM_EPS)
        x3n = x3 * r3
        e = x3n * gf - i[3][...]
        _acc_add(a[0], e * e)
        dy = e * (1.0 / D_MODEL)
        _acc_add(a[1], dy * x3n)
        dyg = dy * gf
        dx3 = r3 * (dyg - x3n * jnp.mean(dyg * x3n, axis=-1, keepdims=True))
        o[0][...] = dx3
        o[1][...] = (dx3 * pe_ * gt * (1.0 - gt)).astype(BF16)
        o[2][...] = (dx3 * gt).astype(BF16)

    outs = [_out_rows(s, D_MODEL, F32, tr), _out_rows(s, D_MODEL, BF16, tr), _out_rows(s, D_MODEL, BF16, tr)]
    return _seqtiled("head", fn, s // tr, ins, outs, acc_widths=(D_MODEL, D_MODEL))


def _stage_norm_bwd(name, dh, x, dres, gain):
    s = x.shape[0]
    tr = min(SEQ_TILE // 2, s)

    def fn(i, o, a):
        dx, dg = _rms_bwd(i[0][...], i[1][...], i[3][...])
        o[0][...] = i[2][...] + dx
        _acc_add(a[0], dg)

    return _seqtiled(name, fn, s // tr, [_rows(dh, tr), _rows(x, tr), _rows(dres, tr), _whole(gain)],
                     [_out_rows(s, D_MODEL, F32, tr)], acc_widths=(D_MODEL,))


def _stage_merge_bwd(proj, dmerged, a_out, r_out):
    s = proj.shape[0]
    tr = min(SEQ_TILE // 2, s)
    wins = _win(proj, tr, C_GA, D_MODEL)
    na = len(wins)
    ins = wins + _win(proj, tr, C_GR, D_MODEL) + [_rows(dmerged, tr), _rows(a_out, tr), _rows(r_out, tr)]

    def fn(i, o, a):
        sa, sr = _sigmoid(_cat(i[:na])), _sigmoid(_cat(i[na:2 * na]))
        dm = i[2 * na][...]
        o[0][...] = (dm * sa).astype(BF16)
        o[1][...] = (dm * sr).astype(BF16)
        o[2][...] = (dm * i[2 * na + 1][...] * sa * (1.0 - sa)).astype(BF16)
        o[3][...] = (dm * i[2 * na + 2][...] * sr * (1.0 - sr)).astype(BF16)

    return _seqtiled("merge_bwd", fn, s // tr, ins, [_out_rows(s, D_MODEL, BF16, tr)] * 4)


def _stage_mix_post_bwd(dattn, attn_rows, drz, ry_f, ry_b, proj, gain):
    s = proj.shape[0]
    t = _seq_tile(s)
    ins = ([_rows(dattn, t), _rows(attn_rows, t), _rows(drz, t), _rows(ry_f, t), _rows(ry_b, t)]
           + _win(proj, t, C_RG, RET_W) + [_whole(gain)])

    def fn(i, o, a):
        da = i[0][...]
        dat = da.T
        prod_t = (da * i[1][...].astype(F32)).T
        dat_b = dat.astype(BF16)
        zeros = jnp.zeros((ATTN_HEAD_DIM, t), BF16)
        for h in range(ATTN_HEADS):
            g = h // ATTN_GROUP
            o[0][h * LANES + g * ATTN_HEAD_DIM:h * LANES + (g + 1) * ATTN_HEAD_DIM, :] = dat_b[h * ATTN_HEAD_DIM:(h + 1) * ATTN_HEAD_DIM, :]
            o[0][h * LANES + (1 - g) * ATTN_HEAD_DIM:h * LANES + (2 - g) * ATTN_HEAD_DIM, :] = zeros
            o[1][h] = jnp.sum(prod_t[h * ATTN_HEAD_DIM:(h + 1) * ATTN_HEAD_DIM, :], axis=0, keepdims=True)
        ry = i[3][...] + i[4][...]
        rg = _cat(i[5:7])
        gain_ = i[7][...]
        gn, rs, sg = _groupnorm_gate(ry, rg, None)
        dz = i[2][...]
        silu = rg * sg
        _acc_add(a[0], dz * gn * silu)
        dgn = dz * gain_ * silu
        o[2][...] = rs * (dgn - _seg_mean(dgn, RET_HEAD_DIM) - gn * _seg_mean(dgn * gn, RET_HEAD_DIM))
        o[3][...] = (dz * gn * gain_ * (sg * (1.0 + rg * (1.0 - sg)))).astype(BF16)

    outs = [_out_ct(s, ATTN_HEADS * LANES, BF16, t),
            ((ATTN_HEADS, s // t, 1, t), F32, pl.BlockSpec((ATTN_HEADS, None, 1, t), lambda i: (0, i, 0, 0))),
            _out_rows(s, RET_W, F32, t), _out_rows(s, RET_W, BF16, t)]
    return _seqtiled("mix_post_bwd", fn, s // t, ins, outs, acc_widths=(RET_W,))


def _stage_dproj(proj, dq_ct, dk8, dv8, rgrads, drg, dga, dgr, tabs, gq_w, gk_w):
    s = proj.shape[0]
    t = _seq_tile(s)
    ca, sa, cr, sr = tabs
    kv8 = pl.BlockSpec((ATTN_HEADS, t, ATTN_KV_W), lambda i: (0, i, 0))
    ins = (_win(proj, t, C_AQ, ATTN_Q_W) + _win(proj, t, C_AK, ATTN_KV_W) + [_ct(dq_ct), (dk8, kv8), (dv8, kv8)]
           + [_rows(g, t) for g in rgrads] + [_rows(drg, t), _rows(dga, t), _rows(dgr, t)]
           + [_rows(ca, t), _rows(sa, t), _rows(cr, t), _rows(sr, t), _whole(gq_w), _whole(gk_w)])

    def fn(i, o, a):
        aq, ak = i[0][...], i[1][...]
        dq_f, dk_f, dv_f, dq_b, dk_b, dv_b = (r[...] for r in i[5:11])
        ca_, sa_, cr_, sr_ = i[14][...], i[15][...], i[16][...], i[17][...]
        dqn = _rope_t(i[2][...].T * ATTN_SCALE, ca_, sa_, ATTN_HEAD_DIM)
        daq, gq_rows = _headnorm_bwd(dqn, aq, i[18][...], ATTN_HEAD_DIM)
        dkn = _rope_t(jnp.sum(i[3][...], axis=0) * (1.0 / LOG2E), ca_, sa_, ATTN_HEAD_DIM)
        dak, gk_rows = _headnorm_bwd(dkn, ak, i[19][...], ATTN_HEAD_DIM)
        _acc_add(a[0], gq_rows)
        _acc_add(a[1], gk_rows)
        out = o[0]
        out[:, C_AQ:C_AQ + ATTN_Q_W] = daq.astype(BF16)
        out[:, C_AK:C_AK + ATTN_KV_W] = dak.astype(BF16)
        out[:, C_AV:C_AV + ATTN_KV_W] = jnp.sum(i[4][...], axis=0).astype(BF16)
        out[:, C_RQ:C_RQ + RET_W] = _rope_t((dq_f + dq_b) * RET_SCALE, cr_, sr_, RET_HEAD_DIM).astype(BF16)
        out[:, C_RK:C_RK + RET_W] = _rope_t(dk_f + dk_b, cr_, sr_, RET_HEAD_DIM).astype(BF16)
        out[:, C_RV:C_RV + RET_W] = (dv_f + dv_b).astype(BF16)
        out[:, C_RG:C_RG + RET_W] = i[11][...]
        out[:, C_GA:C_GA + D_MODEL] = i[12][...]
        out[:, C_GR:C_GR + D_MODEL] = i[13][...]

    return _seqtiled("dproj", fn, s // t, ins, [_out_rows(s, IN_W, BF16, t)], acc_widths=(ATTN_Q_W, ATTN_KV_W))


def _attn_fwd(q_ct, k_rows, v_ct):
    nq, _, t = q_ct.shape
    s = nq * t
    nk = nq
    assert nk % 2 == 0
    n_par = 2

    def body(q_ref, k_ref, v_ref, o_ref, lse_ref, *bufs):
        sbuf = (bufs[0:2], bufs[2:4])
        pbuf = (bufs[4:6], bufs[6:8])

        def scores(w, j, slot):
            kj = k_ref[pl.ds(pl.multiple_of(j * t, t), t), :]
            st = jnp.dot(kj, q_ref[w], preferred_element_type=F32)
            sbuf[w][slot][...] = st
            return jnp.max(st, axis=0, keepdims=True)

        def probs(w, slot, cmax, m, l):
            m_new = jnp.maximum(m, cmax)
            alpha = jnp.exp2(m - m_new)
            pt = jnp.exp2(sbuf[w][slot][...] - m_new)
            pbuf[w][slot][...] = pt.astype(BF16)
            return m_new, alpha * l + jnp.sum(pt, axis=0, keepdims=True), alpha

        def values(w, j, slot, alpha, acc):
            return alpha * acc + jnp.dot(v_ref[j], pbuf[w][slot][...], preferred_element_type=F32)

        init = []
        for w in range(n_par):
            m = jnp.full((1, t), -1e30, F32)
            l = jnp.zeros((1, t), F32)
            cmax0 = scores(w, 0, 0)
            cmax1 = scores(w, 1, 1)
            m, l, alpha0 = probs(w, 0, cmax0, m, l)
            init.append((m, l, jnp.zeros((ATTN_HEAD_DIM, t), F32), cmax1, alpha0))

        def trip(n, carry):
            c = 2 * n
            out = []
            for w in range(n_par):
                m, l, acc, cmax_b, alpha_c = carry[w]
                acc = values(w, c, 0, alpha_c, acc)
                m, l, alpha1 = probs(w, 1, cmax_b, m, l)
                cmax2 = scores(w, c + 2, 0)
                acc = values(w, c + 1, 1, alpha1, acc)
                m, l, alpha2 = probs(w, 0, cmax2, m, l)
                cmax3 = scores(w, c + 3, 1)
                out.append((m, l, acc, cmax3, alpha2))
            return tuple(out)

        res = lax.fori_loop(0, nk // 2 - 1, trip, tuple(init))
        for w in range(n_par):
            m, l, acc, cmax_b, alpha_c = res[w]
            acc = values(w, nk - 2, 0, alpha_c, acc)
            m, l, alpha1 = probs(w, 1, cmax_b, m, l)
            acc = values(w, nk - 1, 1, alpha1, acc)
            o_ref[w] = (acc / l).astype(BF16)
            lse_ref[w] = m + jnp.log2(l)

    return pl.pallas_call(
        body, name="attn_fwd", grid=(ATTN_HEADS, nq // n_par),
        in_specs=[pl.BlockSpec((n_par, LANES, t), lambda h, i: (i, h, 0)),
                  pl.BlockSpec((s, ATTN_KV_W), lambda h, i: (0, 0)),
                  pl.BlockSpec((nk, ATTN_HEAD_DIM, t), lambda h, i: (0, h // ATTN_GROUP, 0))],
        out_specs=[pl.BlockSpec((n_par, ATTN_HEAD_DIM, t), lambda h, i: (i, h, 0)),
                   pl.BlockSpec((None, n_par, 1, t), lambda h, i: (h, i, 0, 0))],
        out_shape=[jax.ShapeDtypeStruct((nq, ATTN_Q_W, t), BF16), jax.ShapeDtypeStruct((ATTN_HEADS, nq, 1, t), F32)],
        scratch_shapes=[pltpu.VMEM((t, t), F32)] * (2 * n_par) + [pltpu.VMEM((t, t), BF16)] * (2 * n_par),
        compiler_params=_cparams(("parallel", "parallel")),
    )(q_ct, k_rows, v_ct)


def _attn_bwd(q_ct, do_ct, lse, delta, k_rows, v_rows, k_ct):
    nq, _, t = q_ct.shape
    s = nq * t
    nk = nq

    assert nq % 2 == 0

    def body(q_ref, do_ref, lse_ref, delta_ref, k_ref, v_ref, kt_ref, dq_ref, dk_ref, dv_ref, dk_acc, dv_acc,
             sb0, sb1, db0, db1, pb0, pb1, gb0, gb1):
        j = pl.program_id(1)
        sb, db, pb, gb = (sb0, sb1), (db0, db1), (pb0, pb1), (gb0, gb1)

        @pl.when(j == 0)
        def _():
            dq_ref[...] = jnp.zeros(dq_ref.shape, F32)

        kj, vj, ktj = k_ref[...], v_ref[...], kt_ref[...]
        dk_acc[...] = jnp.zeros(dk_acc.shape, F32)
        dv_acc[...] = jnp.zeros(dv_acc.shape, F32)

        def products(i, slot):
            sb[slot][...] = jnp.dot(kj, q_ref[i], preferred_element_type=F32)
            db[slot][...] = jnp.dot(vj, do_ref[i], preferred_element_type=F32)

        def cotangents(i, slot):
            pt = jnp.exp2(sb[slot][...] - lse_ref[i])
            pb[slot][...] = pt.astype(BF16)
            gb[slot][...] = (pt * (db[slot][...] - delta_ref[i])).astype(BF16)

        def accumulate(i, slot):
            dst = gb[slot][...]
            dv_acc[...] += _nt(pb[slot][...], do_ref[i])
            dk_acc[...] += _nt(dst, q_ref[i])
            dq_ref[i] += jnp.dot(ktj, dst, preferred_element_type=F32)

        products(0, 0)
        products(1, 1)
        cotangents(0, 0)

        def trip(n, carry):
            c = 2 * n
            accumulate(c, 0)
            cotangents(c + 1, 1)
            products(c + 2, 0)
            accumulate(c + 1, 1)
            cotangents(c + 2, 0)
            products(c + 3, 1)
            return carry

        lax.fori_loop(0, nq // 2 - 1, trip, 0)
        accumulate(nq - 2, 0)
        cotangents(nq - 1, 1)
        accumulate(nq - 1, 1)
        dk_ref[...] = dk_acc[...]
        dv_ref[...] = dv_acc[...]

    per_head = pl.BlockSpec((nq, LANES, t), lambda h, j: (0, h, 0))
    stat = pl.BlockSpec((None, nq, 1, t), lambda h, j: (h, 0, 0, 0))
    kv_rows = pl.BlockSpec((t, ATTN_KV_W), lambda h, j: (j, 0))
    kv_out = pl.BlockSpec((None, t, ATTN_KV_W), lambda h, j: (h, j, 0))
    return pl.pallas_call(
        body, name="attn_bwd", grid=(ATTN_HEADS, nk),
        in_specs=[per_head, per_head, stat, stat, kv_rows, kv_rows,
                  pl.BlockSpec((None, ATTN_HEAD_DIM, t), lambda h, j: (j, h // ATTN_GROUP, 0))],
        out_specs=[pl.BlockSpec((nq, ATTN_HEAD_DIM, t), lambda h, j: (0, h, 0)), kv_out, kv_out],
        out_shape=[jax.ShapeDtypeStruct((nq, ATTN_Q_W, t), F32), jax.ShapeDtypeStruct((ATTN_HEADS, s, ATTN_KV_W), F32),
                   jax.ShapeDtypeStruct((ATTN_HEADS, s, ATTN_KV_W), F32)],
        scratch_shapes=([pltpu.VMEM((t, ATTN_KV_W), F32)] * 2 + [pltpu.VMEM((t, t), F32)] * 4 + [pltpu.VMEM((t, t), BF16)] * 4),
        compiler_params=_cparams(("parallel", "arbitrary")),
    )(q_ct, do_ct, lse, delta, k_rows, v_rows, k_ct)


def _log_sigmoid(x):
    t = jnp.exp(-jnp.abs(x))
    log1p_t = jnp.where(t < 1e-2, t * (1.0 - t * (0.5 - t * (1.0 / 3.0))), jnp.log(1.0 + t))
    return jnp.minimum(x, 0.0) - log1p_t


def _decay_tables(logit, backward):
    c = RET_CHUNK
    lam = _log_sigmoid(jnp.full((c, c), logit, F32))
    ii = lax.broadcasted_iota(jnp.int32, (c, c), 0).astype(F32)
    jj = lax.broadcasted_iota(jnp.int32, (c, c), 1).astype(F32)
    if not backward:
        dist, dist_t = jnp.maximum(ii - jj, 0.0), jnp.maximum(jj - ii, 0.0)
        mask, mask_t = ii >= jj, jj >= ii
        e_q, e_k = ii + 1.0, (c - 1.0) - ii
    else:
        dist, dist_t = jnp.maximum(jj - ii, 0.0), jnp.maximum(ii - jj, 0.0)
        mask, mask_t = jj > ii, ii > jj
        e_q, e_k = c - ii, ii
    return dict(
        d=jnp.where(mask, jnp.exp(lam * dist), 0.0), d_t=jnp.where(mask_t, jnp.exp(lam * dist_t), 0.0), dist=dist,
        qdec=jnp.exp(lam * e_q), kdec=jnp.exp(lam * e_k), e_q=e_q, e_k=e_k, gam=jnp.exp(lam * c))


def _nt(a, b):
    return lax.dot_general(a, b, (((1,), (1,)), ((), ())), preferred_element_type=F32)


def _ret_fwd(name, logits, q, k, proj, backward):
    s = q.shape[0]
    c = RET_CHUNK
    nc = s // c
    d_ix = 1 if backward else 0
    chunk = (lambda n: nc - 1 - n) if backward else (lambda n: n)
    vwin = _win(proj, c, C_RV, RET_W)
    nv = len(vwin)
    vw = RET_W // nv

    def body(lg_ref, q_ref, k_ref, *rest):
        v_refs, (y_ref, st_ref, state) = rest[:nv], rest[nv:]

        @pl.when(pl.program_id(0) == 0)
        def _():
            state[...] = jnp.zeros(state.shape, F32)

        for h in range(RET_HEADS):
            tb = _decay_tables(lg_ref[d_ix, h], backward)
            sl = slice(h * RET_HEAD_DIM, (h + 1) * RET_HEAD_DIM)
            qh, kh = q_ref[:, sl], k_ref[:, sl]
            off = h * RET_HEAD_DIM
            vh = v_refs[off // vw][:, off % vw:off % vw + RET_HEAD_DIM]
            vb = vh.astype(BF16)
            a = _nt(qh.astype(BF16), kh.astype(BF16)) * tb["d"]
            sh = state[h]
            st_ref[h] = sh
            y_ref[:, sl] = (jnp.dot(a.astype(BF16), vb, preferred_element_type=F32)
                            + jnp.dot((qh * tb["qdec"]).astype(BF16), sh.astype(BF16), preferred_element_type=F32))
            state[h] = tb["gam"] * sh + jnp.dot((kh * tb["kdec"]).T.astype(BF16), vb, preferred_element_type=F32)

    rows = pl.BlockSpec((c, RET_W), lambda n: (chunk(n), 0))
    v_specs = [pl.BlockSpec(sp.block_shape, lambda n, im=sp.index_map: (chunk(n), im(0)[1])) for _, sp in vwin]
    return pl.pallas_call(
        body, name=name, grid=(nc,),
        in_specs=[pl.BlockSpec(memory_space=pltpu.SMEM), rows, rows] + v_specs,
        out_specs=[rows, pl.BlockSpec((None, RET_HEADS, RET_HEAD_DIM, RET_HEAD_DIM), lambda n: (chunk(n), 0, 0, 0))],
        out_shape=[jax.ShapeDtypeStruct((s, RET_W), F32), jax.ShapeDtypeStruct((nc, RET_HEADS, RET_HEAD_DIM, RET_HEAD_DIM), F32)],
        scratch_shapes=[pltpu.VMEM((RET_HEADS, RET_HEAD_DIM, RET_HEAD_DIM), F32)],
        compiler_params=_cparams(("arbitrary",)),
    )(logits, q, k, *[a for a, _ in vwin])


def _ret_bwd(name, logits, q, k, proj, dy, states, backward):
    s = q.shape[0]
    c = RET_CHUNK
    nc = s // c
    d_ix = 1 if backward else 0
    chunk = (lambda n: n) if backward else (lambda n: nc - 1 - n)
    vwin = _win(proj, c, C_RV, RET_W)
    nv = len(vwin)
    vw = RET_W // nv

    def body(lg_ref, q_ref, k_ref, dy_ref, st_ref, *rest):
        v_refs, (dq_ref, dk_ref, dv_ref, dl_ref, dstate, lacc) = rest[:nv], rest[nv:]
        n = pl.program_id(0)

        @pl.when(n == 0)
        def _():
            dstate[...] = jnp.zeros(dstate.shape, F32)
            lacc[...] = jnp.zeros(lacc.shape, F32)

        for h in range(RET_HEADS):
            tb = _decay_tables(lg_ref[d_ix, h], backward)
            sl = slice(h * RET_HEAD_DIM, (h + 1) * RET_HEAD_DIM)
            qh, kh, dyh = q_ref[:, sl], k_ref[:, sl], dy_ref[:, sl]
            off = h * RET_HEAD_DIM
            vh = v_refs[off // vw][:, off % vw:off % vw + RET_HEAD_DIM]
            qb, kb, vb, dyb = qh.astype(BF16), kh.astype(BF16), vh.astype(BF16), dyh.astype(BF16)
            sh, dsh = st_ref[h], dstate[h]
            shb, dshb = sh.astype(BF16), dsh.astype(BF16)
            qk = _nt(qb, kb)
            g = _nt(dyb, vb) * tb["d"]
            a_t = _nt(kb, qb) * tb["d_t"]
            g_t = _nt(vb, dyb) * tb["d_t"]
            qd, kd = qh * tb["qdec"], kh * tb["kdec"]
            dqd = _nt(dyb, shb)
            dkd = _nt(vb, dshb)
            dq_ref[:, sl] = jnp.dot(g.astype(BF16), kb, preferred_element_type=F32) + dqd * tb["qdec"]
            dk_ref[:, sl] = jnp.dot(g_t.astype(BF16), qb, preferred_element_type=F32) + dkd * tb["kdec"]
            dv_ref[:, sl] = (jnp.dot(a_t.astype(BF16), dyb, preferred_element_type=F32)
                             + jnp.dot(kd.astype(BF16), dshb, preferred_element_type=F32))
            lacc[h] += (tb["dist"] * qk * g + tb["e_q"] * qd * dqd + tb["e_k"] * kd * dkd
                        + float(c) * tb["gam"] * dsh * sh)
            dstate[h] = tb["gam"] * dsh + jnp.dot(qd.T.astype(BF16), dyb, preferred_element_type=F32)

        @pl.when(n == nc - 1)
        def _():
            for h in range(RET_HEADS):
                dl_ref[h] = jnp.zeros((8, LANES), F32) + jnp.sum(lacc[h])

    rows = pl.BlockSpec((c, RET_W), lambda n: (chunk(n), 0))
    v_specs = [pl.BlockSpec(sp.block_shape, lambda n, im=sp.index_map: (chunk(n), im(0)[1])) for _, sp in vwin]
    hmat = (RET_HEADS, RET_HEAD_DIM, RET_HEAD_DIM)
    return pl.pallas_call(
        body, name=name, grid=(nc,),
        in_specs=[pl.BlockSpec(memory_space=pltpu.SMEM), rows, rows, rows,
                  pl.BlockSpec((None,) + hmat, lambda n: (chunk(n), 0, 0, 0))] + v_specs,
        out_specs=[rows, rows, rows, pl.BlockSpec((RET_HEADS, 8, LANES), lambda n: (0, 0, 0))],
        out_shape=[jax.ShapeDtypeStruct((s, RET_W), F32)] * 3 + [jax.ShapeDtypeStruct((RET_HEADS, 8, LANES), F32)],
        scratch_shapes=[pltpu.VMEM(hmat, F32), pltpu.VMEM(hmat, F32)],
        compiler_params=_cparams(("arbitrary",)),
    )(logits, q, k, dy, states, *[a for a, _ in vwin])


def _local_step(x, p, target, w, small):
    s = x.shape[0]
    tabs = _rope_tables(s, ATTN_HEAD_DIM) + _rope_tables(s, RET_HEAD_DIM)
    g_mix, g_mlp, g_ple = small["mix_norm"][None, :], small["mlp_norm"][None, :], small["ple_norm"][None, :]
    g_final, g_ret = small["final_norm"][None, :], small["ret_norm_gain"][None, :]
    gq_w = jnp.tile(small["attn_q_norm"], ATTN_HEADS)[None, :]
    gk_w = jnp.tile(small["attn_k_norm"], ATTN_KV_HEADS)[None, :]
    logits = small["ret_decay_logit"]

    hb = _stage_norm_in(x, g_mix)
    proj = _mm("in_proj", hb, w["w_in"], tm=512, tn=IN_W // 2, tk=1024)
    q_ct, k_rows, k_ct, v_rows, v_ct, rq, rk = _stage_qkv(proj, tabs, gq_w, gk_w)
    o_ct, lse = _attn_fwd(q_ct, k_rows, v_ct)
    ry_f, st_f = _ret_fwd("ret_fwd_f", logits, rq, rk, proj, False)
    ry_b, st_b = _ret_fwd("ret_fwd_b", logits, rq, rk, proj, True)
    rz, attn_rows = _stage_mix_post(ry_f, ry_b, proj, o_ct, g_ret)
    a_out = _mm("attn_o", attn_rows, w["w_attn_o"], tm=1024, tn=1024, tk=512)
    r_out = _mm("ret_o", rz, w["w_ret_o"], tm=1024, tn=1024, tk=512)
    merged = _stage_merge(proj, a_out, r_out)

    def epi_res_norm(acc, e, c):
        xr = e[0][...] + acc
        return xr, _rms_fwd(xr, c[0][...])

    x1, hm = _mm("out_proj", merged, w["w_out"], tm=512, tn=1024, tk=1024, out_dtypes=(F32, BF16),
                 epi=epi_res_norm, epi_ins=(x,), consts=(g_mlp,))

    def epi_relu2(acc, e, c):
        r = jnp.maximum(acc, 0.0)
        return acc, r * r

    u, act = _mm("mlp_up", hm, w["w_up"], tm=512, tn=2048, tk=1024, out_dtypes=(F32, BF16), epi=epi_relu2)
    x2, hp = _mm("mlp_down", act, w["w_down"], tm=512, tn=1024, tk=2048, out_dtypes=(F32, BF16),
                 epi=epi_res_norm, epi_ins=(x1,), consts=(g_ple,))
    zg = _mm("ple_gate", hp, w["w_ple_gate"], tm=1024, tn=1024, tk=1024)
    pe = _mm("ple_emb", p, w["w_ple"], tm=1024, tn=1024, tk=256)
    dx3, dzg, dpe, loss_cols, g_final_p = _stage_head(zg, pe, x2, target, g_final)
    loss_sum = 0.5 / D_MODEL * jnp.sum(loss_cols)

    gw = {}
    gw["w_ple"] = _mm("g_w_ple", p, dpe, ta=True, tm=256, tn=1024, tk=2048)
    gw["w_ple_gate"] = _mm("g_w_ple_gate", hp, dzg, ta=True, tm=1024, tn=1024, tk=2048)
    dhp = _mm("d_hp", dzg, w["w_ple_gate"], tb=True, tm=1024, tn=1024, tk=1024)
    dx2, g_ple_p = _stage_norm_bwd("ple_norm_bwd", dhp, x2, dx3, g_ple)

    def epi_relu2_bwd(acc, e, c):
        return (acc * (2.0 * jnp.maximum(e[0][...], 0.0)),)

    du = _mm("d_u", dx2, w["w_down"], tb=True, tm=512, tn=2048, tk=1024, out_dtypes=(BF16,), epi=epi_relu2_bwd, epi_ins=(u,))
    gw["w_down"] = _mm("g_w_down", act, dx2, ta=True, tm=1024, tn=1024, tk=2048)
    gw["w_up"] = _mm("g_w_up", hm, du, ta=True, tm=1024, tn=1024, tk=2048)
    dhm = _mm("d_hm", du, w["w_up"], tb=True, tm=512, tn=1024, tk=2048)
    dx1, g_mlp_p = _stage_norm_bwd("mlp_norm_bwd", dhm, x1, dx2, g_mlp)
    dmerged = _mm("d_merged", dx1, w["w_out"], tb=True, tm=1024, tn=1024, tk=1024)
    gw["w_out"] = _mm("g_w_out", merged, dx1, ta=True, tm=1024, tn=1024, tk=2048)
    dao, dro, dga, dgr = _stage_merge_bwd(proj, dmerged, a_out, r_out)
    gw["w_attn_o"] = _mm("g_w_attn_o", attn_rows, dao, ta=True, tm=512, tn=1024, tk=2048)
    gw["w_ret_o"] = _mm("g_w_ret_o", rz, dro, ta=True, tm=512, tn=1024, tk=2048)
    dattn = _mm("d_attn", dao, w["w_attn_o"], tb=True, tm=1024, tn=512, tk=1024)
    drz = _mm("d_rz", dro, w["w_ret_o"], tb=True, tm=1024, tn=512, tk=1024)
    do_ct, delta, dry, drg, g_ret_p = _stage_mix_post_bwd(dattn, attn_rows, drz, ry_f, ry_b, proj, g_ret)
    dq_f, dk_f, dv_f, dl_f = _ret_bwd("ret_bwd_f", logits, rq, rk, proj, dry, st_f, False)
    dq_b, dk_b, dv_b, dl_b = _ret_bwd("ret_bwd_b", logits, rq, rk, proj, dry, st_b, True)
    dq_ct, dk8, dv8 = _attn_bwd(q_ct, do_ct, lse, delta, k_rows, v_rows, k_ct)
    dproj, gq_p, gk_p = _stage_dproj(proj, dq_ct, dk8, dv8, (dq_f, dk_f, dv_f, dq_b, dk_b, dv_b), drg, dga, dgr, tabs, gq_w, gk_w)
    gw["w_in"] = _mm("g_w_in", hb, dproj, ta=True, tm=512, tn=IN_W // 2, tk=1024)
    dh = _mm("d_h", dproj, w["w_in"], tb=True, tm=512, tn=1024, tk=IN_W // 2)
    grad_x, g_mix_p = _stage_norm_bwd("mix_norm_bwd", dh, x, dx1, g_mix)

    gs = {
        "mix_norm": g_mix_p[0], "mlp_norm": g_mlp_p[0], "ple_norm": g_ple_p[0], "final_norm": g_final_p[0],
        "ret_norm_gain": g_ret_p[0],
        "attn_q_norm": jnp.sum(gq_p[0].reshape(ATTN_HEADS, ATTN_HEAD_DIM), axis=0),
        "attn_k_norm": jnp.sum(gk_p[0].reshape(ATTN_KV_HEADS, ATTN_HEAD_DIM), axis=0),
        "ret_decay_logit": jnp.stack([dl_f[:, 0, 0], dl_b[:, 0, 0]]),
    }
    return loss_sum, grad_x, gw, gs


PACK_COLS = 1024
N_CHIPS = 4
HALF_ROWS = 2048


def _pack_shard(parts):
    return jnp.concatenate([parts[n].reshape(-1, PACK_COLS) for n, _ in BIG], axis=0)


def _unpack_shard(slab, shapes):
    out, r = {}, 0
    for n, _ in BIG:
        rows = math.prod(shapes[n]) // PACK_COLS
        out[n] = slab[r:r + rows].reshape(shapes[n])
        r += rows
    return out


def _shard_of(full, axis, sidx):
    size = full.shape[axis] // N_CHIPS
    return lax.slice_in_dim(full, sidx * size, (sidx + 1) * size, axis=axis)


def _position():
    x, y, c = lax.axis_index("x"), lax.axis_index("y"), lax.axis_index("c")
    return x, y, c


def _other_chips(x, y):
    return [(1 - x, y), (x, 1 - y), (1 - x, 1 - y)]


ANY = pl.BlockSpec(memory_space=pl.ANY)


def _gather_weights(slab):
    rows = slab.shape[0]
    half = rows // 2

    def body(in_ref, out_ref, send_sems, recv_sems):
        x, y, c = _position()
        chips = _other_chips(x, y)

        def piece(chip, core):
            return out_ref.at[2 * chip[0] + chip[1], pl.ds(core * half, half), :]

        def copy(k, chip, core, to, src=None):
            return pltpu.make_async_remote_copy(
                src_ref=piece(chip, core) if src is None else src, dst_ref=piece(chip, core),
                send_sem=send_sems.at[k], recv_sem=recv_sems.at[k], device_id=to, device_id_type=MESH)

        first = [copy(j, (x, y), c, (*chip, c), src=in_ref.at[pl.ds(c * half, half), :]) for j, chip in enumerate(chips)]
        for cp in first:
            cp.start()
        passed = [copy(3 + j, chip, c, (x, y, 1 - c)) for j, chip in enumerate(chips)]
        for j, chip in enumerate(chips):
            copy(j, chip, c, (x, y, c)).wait_recv()
            passed[j].start()
        for j, chip in enumerate(chips):
            copy(3 + j, chip, 1 - c, (x, y, c)).wait_recv()
        for cp in first + passed:
            cp.wait_send()

    return pl.pallas_call(
        body, name="gather_weights", in_specs=[ANY], out_specs=ANY,
        out_shape=jax.ShapeDtypeStruct((N_CHIPS,) + slab.shape, slab.dtype),
        scratch_shapes=[pltpu.SemaphoreType.DMA((6,)), pltpu.SemaphoreType.DMA((6,))],
    )(slab)


def _exchange_halves(g):
    def body(g_ref, out_ref, send_sem, recv_sem):
        x, y, c = _position()
        cp = pltpu.make_async_remote_copy(src_ref=g_ref.at[1 - c], dst_ref=out_ref, send_sem=send_sem, recv_sem=recv_sem,
                                          device_id=(x, y, 1 - c), device_id_type=MESH)
        cp.start()
        cp.wait()

    return pl.pallas_call(
        body, name="exchange_halves", in_specs=[ANY], out_specs=ANY,
        out_shape=jax.ShapeDtypeStruct(g.shape[1:], g.dtype),
        scratch_shapes=[pltpu.SemaphoreType.DMA, pltpu.SemaphoreType.DMA],
    )(g)


def _add_my_half(g, r1, c_idx):
    tr = 256
    nt = g.shape[2] // tr

    def body(c_ref, g_ref, r_ref, o_ref, ob_ref):
        tot = g_ref[...] + r_ref[...]
        o_ref[...] = tot
        ob_ref[...] = tot.astype(BF16)

    blk = (None, tr, PACK_COLS)
    spec = pl.BlockSpec(blk, lambda s, i, c_ref: (s, i, 0))
    return pl.pallas_call(
        body, name="add_my_half",
        grid_spec=pltpu.PrefetchScalarGridSpec(
            num_scalar_prefetch=1, grid=(N_CHIPS, nt),
            in_specs=[pl.BlockSpec((None,) + blk, lambda s, i, c_ref: (c_ref[0], s, i, 0)), spec],
            out_specs=[spec, spec]),
        out_shape=[jax.ShapeDtypeStruct(g.shape[1:], F32), jax.ShapeDtypeStruct(g.shape[1:], BF16)],
        compiler_params=_cparams(("parallel", "parallel")),
    )(c_idx, g, r1)


def _scatter_to_chips(part):
    def body(p_ref, out_ref, send_sems, recv_sems):
        x, y, c = _position()
        chips = _other_chips(x, y)
        sends = [pltpu.make_async_remote_copy(
            src_ref=p_ref.at[2 * chip[0] + chip[1]], dst_ref=out_ref.at[j], send_sem=send_sems.at[j], recv_sem=recv_sems.at[j],
            device_id=(*chip, c), device_id_type=MESH) for j, chip in enumerate(chips)]
        for cp in sends:
            cp.start()
        for cp in sends:
            cp.wait()

    return pl.pallas_call(
        body, name="scatter_to_chips", in_specs=[ANY], out_specs=ANY,
        out_shape=jax.ShapeDtypeStruct((N_CHIPS - 1,) + part.shape[1:], part.dtype),
        scratch_shapes=[pltpu.SemaphoreType.DMA((3,)), pltpu.SemaphoreType.DMA((3,))],
    )(part)


def _sum_chips(part, r2, chip_idx):
    tr = 256

    def body(c_ref, p_ref, r_ref, o_ref):
        o_ref[...] = ((p_ref[...] + r_ref[0]) + r_ref[1]) + r_ref[2]

    return pl.pallas_call(
        body, name="sum_chips",
        grid_spec=pltpu.PrefetchScalarGridSpec(
            num_scalar_prefetch=1, grid=(r2.shape[1] // tr,),
            in_specs=[pl.BlockSpec((None, tr, PACK_COLS), lambda i, c_ref: (c_ref[0], i, 0)),
                      pl.BlockSpec((N_CHIPS - 1, tr, PACK_COLS), lambda i, c_ref: (0, i, 0))],
            out_specs=pl.BlockSpec((tr, PACK_COLS), lambda i, c_ref: (i, 0))),
        out_shape=jax.ShapeDtypeStruct(r2.shape[1:], F32),
        compiler_params=_cparams(("parallel",)),
    )(chip_idx, part, r2)


def _join_halves(red):
    def body(r_ref, out_ref, send_sem, recv_sem):
        x, y, c = _position()
        cp = pltpu.make_async_remote_copy(src_ref=r_ref, dst_ref=out_ref, send_sem=send_sem, recv_sem=recv_sem,
                                          device_id=(x, y, 1 - c), device_id_type=MESH)
        cp.start()
        cp.wait()

    return pl.pallas_call(
        body, name="join_halves", in_specs=[ANY], out_specs=ANY,
        out_shape=jax.ShapeDtypeStruct(red.shape, red.dtype),
        scratch_shapes=[pltpu.SemaphoreType.DMA, pltpu.SemaphoreType.DMA],
    )(red)


def _adamw_math(w, g, m, v):
    m = ADAM_B1 * m + (1.0 - ADAM_B1) * g
    v = ADAM_B2 * v + (1.0 - ADAM_B2) * (g * g)
    m_hat = m / (1.0 - ADAM_B1 ** ADAM_STEP)
    v_hat = v / (1.0 - ADAM_B2 ** ADAM_STEP)
    delta = -ADAM_LR * (m_hat / (jnp.sqrt(v_hat) + ADAM_EPS) + ADAM_WD * w)
    return delta, m, v


def _adamw(w, g, m, v):
    tr = 256

    def body(w_ref, g_ref, m_ref, v_ref, d_ref, nm_ref, nv_ref):
        d_ref[...], nm_ref[...], nv_ref[...] = _adamw_math(w_ref[...], g_ref[...], m_ref[...], v_ref[...])

    blk = pl.BlockSpec((tr, PACK_COLS), lambda i: (i, 0))
    return pl.pallas_call(
        body, name="adamw", grid=(w.shape[0] // tr,), in_specs=[blk] * 4, out_specs=[blk] * 3,
        out_shape=[jax.ShapeDtypeStruct(w.shape, F32)] * 3, compiler_params=_cparams(("parallel",)),
    )(w, g, m, v)


def _small_step(gpk, wpk, mpk, vpk):
    row, col, width = SMALL["ret_decay_logit"]

    def body(g_ref, w_ref, m_ref, v_ref, og_ref, od_ref, om_ref, ov_ref, gbuf, send_sems, recv_sems):
        x, y, c = _position()
        me = 4 * x + 2 * y + c
        gbuf[me] = g_ref[...]
        sends = []
        for k in range(1, 8):
            to = (x ^ (k >> 2), y ^ ((k >> 1) & 1), c ^ (k & 1))
            cp = pltpu.make_async_remote_copy(src_ref=g_ref, dst_ref=gbuf.at[me], send_sem=send_sems.at[k - 1],
                                              recv_sem=recv_sems.at[k - 1], device_id=to, device_id_type=MESH)
            cp.start()
            sends.append(cp)
        for k in range(1, 8):
            frm = me ^ k
            pltpu.make_async_remote_copy(src_ref=g_ref, dst_ref=gbuf.at[frm], send_sem=send_sems.at[k - 1],
                                         recv_sem=recv_sems.at[k - 1], device_id=(x, y, c), device_id_type=MESH).wait_recv()
        for cp in sends:
            cp.wait_send()
        tot = gbuf[0]
        for d in range(1, 8):
            tot = tot + gbuf[d]
        w = w_ref[...]
        r_i = lax.broadcasted_iota(jnp.int32, w.shape, 0)
        c_i = lax.broadcasted_iota(jnp.int32, w.shape, 1)
        is_logit = (r_i == row) & (c_i >= col) & (c_i < col + width)
        g = jnp.where(is_logit, tot * _sigmoid(-w), tot)
        og_ref[...] = g
        od_ref[...], om_ref[...], ov_ref[...] = _adamw_math(w, g, m_ref[...], v_ref[...])

    vm = pl.BlockSpec(memory_space=pltpu.VMEM)
    shp = jax.ShapeDtypeStruct(gpk.shape, F32)
    return pl.pallas_call(
        body, name="small_step", in_specs=[vm] * 4, out_specs=[vm] * 4, out_shape=[shp] * 4,
        scratch_shapes=[pltpu.VMEM((8,) + gpk.shape, F32), pltpu.SemaphoreType.DMA((7,)), pltpu.SemaphoreType.DMA((7,))],
    )(gpk, wpk, mpk, vpk)


def _pack_small(parts):
    rows = [[] for _ in range(SMALL_ROWS)]
    for n, (r, col, width) in sorted(SMALL.items(), key=lambda kv: (kv[1][0], kv[1][1])):
        rows[r].append((col, parts[n].reshape(-1).astype(F32)))
    out = []
    for r in range(SMALL_ROWS):
        segs, pos = [], 0
        for col, vec in rows[r]:
            assert col == pos
            segs.append(vec)
            pos += vec.shape[0]
        if pos < PACK_COLS:
            segs.append(jnp.zeros((PACK_COLS - pos,), F32))
        out.append(jnp.concatenate(segs))
    return jnp.stack(out)


def _unpack_small(pk, shapes):
    return {n: pk[r, col:col + width].reshape(shapes[n]) for n, (r, col, width) in SMALL.items()}


WEIGHTS = ("mix_norm", "w_in", "attn_q_norm", "attn_k_norm", "ret_decay_logit", "ret_norm_gain", "w_attn_o", "w_ret_o", "w_out",
           "mlp_norm", "w_up", "w_down", "ple_norm", "w_ple_gate", "w_ple", "final_norm")


def kernel(x, p, mix_norm, w_in, attn_q_norm, attn_k_norm, ret_decay_logit, ret_norm_gain, w_attn_o, w_ret_o, w_out, mlp_norm, w_up, w_down, ple_norm, w_ple_gate, w_ple, final_norm, loss_target, m_mix_norm, m_w_in, m_attn_q_norm, m_attn_k_norm, m_ret_decay_logit, m_ret_norm_gain, m_w_attn_o, m_w_ret_o, m_w_out, m_mlp_norm, m_w_up, m_w_down, m_ple_norm, m_w_ple_gate, m_w_ple, m_final_norm, v_mix_norm, v_w_in, v_attn_q_norm, v_attn_k_norm, v_ret_decay_logit, v_ret_norm_gain, v_w_attn_o, v_w_ret_o, v_w_out, v_mlp_norm, v_w_up, v_w_down, v_ple_norm, v_w_ple_gate, v_w_ple, v_final_norm):
    args = dict(locals())
    wts = {n: args[n] for n in WEIGHTS}
    ms = {n: args["m_" + n] for n in WEIGHTS}
    vs = {n: args["v_" + n] for n in WEIGHTS}
    shapes = {n: wts[n].shape for n in WEIGHTS}
    big_names = [n for n, _ in BIG]
    xi, yi, ci = _position()
    c_idx = ci.astype(jnp.int32).reshape(1)

    slab_w = _pack_shard({n: wts[n][0] for n in big_names})
    chip_idx = (2 * xi + yi).astype(jnp.int32)
    slab_b = slab_w.astype(BF16)
    gathered = lax.dynamic_update_slice(_gather_weights(slab_b), slab_b[None], (chip_idx, 0, 0))
    full = {}
    for n, axis in BIG:
        per_chip = [_unpack_shard(gathered[k], {m_: shapes[m_][1:] for m_ in big_names})[n] for k in range(N_CHIPS)]
        full[n] = jnp.concatenate(per_chip, axis=axis)
    small = {n: wts[n].reshape(wts[n].shape[1:] if wts[n].ndim > 1 else wts[n].shape) for n in SMALL}

    loss_part, grad_x, gw, gs = _local_step(x[0], p[0, 0], loss_target[0], full, small)
    loss = lax.psum(loss_part, ("x", "y", "c"))

    slabs = jnp.stack([_pack_shard({n: _shard_of(gw[n], axis, k) for n, axis in BIG}) for k in range(N_CHIPS)])
    halves = slabs.reshape(N_CHIPS, 2, HALF_ROWS, PACK_COLS).transpose(1, 0, 2, 3)
    chip_part, chip_part_b = _add_my_half(halves, _exchange_halves(halves), c_idx)
    mine = _sum_chips(chip_part, _scatter_to_chips(chip_part_b), chip_idx.reshape(1))
    both = jnp.stack([mine, _join_halves(mine)])
    reduced = jnp.where(ci == 0, both, both[::-1]).reshape(2 * HALF_ROWS, PACK_COLS)
    delta_b, newm_b, newv_b = _adamw(slab_w, reduced, _pack_shard({n: ms[n][0] for n in big_names}), _pack_shard({n: vs[n][0] for n in big_names}))
    shard_shapes = {n: shapes[n] for n in big_names}
    big_out = [_unpack_shard(a, shard_shapes) for a in (reduced, delta_b, newm_b, newv_b)]

    sm_out = _small_step(_pack_small(gs), _pack_small({n: wts[n] for n in SMALL}), _pack_small({n: ms[n] for n in SMALL}),
                         _pack_small({n: vs[n] for n in SMALL}))
    small_out = [_unpack_small(a, {n: shapes[n] for n in SMALL}) for a in sm_out]

    outs = [loss, grad_x[None]]
    for kind in range(4):
        for n in WEIGHTS:
            outs.append(small_out[kind][n] if n in SMALL else big_out[kind][n])
    return tuple(outs)
```

```python
import functools
import math

import jax
import jax.numpy as jnp
from jax import lax
from jax.experimental import pallas as pl
from jax.experimental.pallas import tpu as pltpu

F32 = jnp.float32
BF16 = jnp.bfloat16
MESH = pl.DeviceIdType.MESH

D_MODEL = 1024
PLE_DIM = 256
GRID_W = 64
ATTN_HEAD_DIM = 64
ATTN_HEADS = 8
ATTN_KV_HEADS = 2
ATTN_GROUP = ATTN_HEADS // ATTN_KV_HEADS
RET_HEAD_DIM = 128
RET_HEADS = 4
ATTN_Q_W = 512
ATTN_KV_W = 128
RET_W = 512
IN_W = 4864
D_FF = 4096
RET_CHUNK = 128
ROPE_THETA = 10000.0
NORM_EPS = 1e-6
GN_EPS = 1e-5
ATTN_SCALE = ATTN_HEAD_DIM ** -0.5
LOG2E = math.log2(math.e)
Q_FOLD = ATTN_SCALE * LOG2E
RET_SCALE = RET_HEAD_DIM ** -0.5

C_AQ, C_AK, C_AV, C_RQ, C_RK, C_RV, C_RG, C_GA, C_GR = 0, 512, 640, 768, 1280, 1792, 2304, 2816, 3840

ADAM_LR = 0.001
ADAM_B1 = 0.9
ADAM_B2 = 0.999
ADAM_EPS = 1e-08
ADAM_WD = 0.01
ADAM_STEP = 10

LANES = 128
VMEM_LIMIT = 56 << 20
SEQ_TILE = 512

BIG = (("w_in", 1), ("w_attn_o", 1), ("w_ret_o", 1), ("w_out", 0), ("w_up", 1), ("w_down", 0), ("w_ple_gate", 0), ("w_ple", 1))
SMALL_ROWS = 8
SMALL = {"mix_norm": (0, 0, 1024), "mlp_norm": (1, 0, 1024), "ple_norm": (2, 0, 1024), "final_norm": (3, 0, 1024),
         "ret_norm_gain": (4, 0, 512), "attn_q_norm": (4, 512, 64), "attn_k_norm": (4, 576, 64), "ret_decay_logit": (4, 640, 8)}


def _seq_tile(s):
    return min(SEQ_TILE, s // 2)


def _cparams(sem=None, vmem=VMEM_LIMIT):
    return pltpu.CompilerParams(dimension_semantics=sem, vmem_limit_bytes=vmem)


def _mm(name, a, b, *, ta=False, tb=False, tm, tn, tk, out_dtypes=(F32,), epi=None, epi_ins=(), consts=()):
    if ta:
        kdim, m = a.shape
    else:
        m, kdim = a.shape
    n = b.shape[0] if tb else b.shape[1]
    tm, tn, tk = min(tm, m), min(tn, n), min(tk, kdim)
    assert m % tm == 0 and n % tn == 0 and kdim % tk == 0, (name, m, n, kdim, tm, tn, tk)
    nk = kdim // tk
    n_e, n_c, n_o = len(epi_ins), len(consts), len(out_dtypes)

    def body(*refs):
        a_ref, b_ref = refs[0], refs[1]
        e_refs = refs[2:2 + n_e]
        c_refs = refs[2 + n_e:2 + n_e + n_c]
        o_refs = refs[2 + n_e + n_c:2 + n_e + n_c + n_o]
        acc_ref = refs[2 + n_e + n_c + n_o] if nk > 1 else None
        k = pl.program_id(2)
        av = a_ref[...].astype(BF16)
        bv = b_ref[...].astype(BF16)
        dims = (((0,) if ta else (1,), (1,) if tb else (0,)), ((), ()))
        part = lax.dot_general(av, bv, dims, preferred_element_type=F32)

        def finish(acc):
            vals = epi(acc, e_refs, c_refs) if epi is not None else (acc,)
            for o_ref, v in zip(o_refs, vals):
                o_ref[...] = v.astype(o_ref.dtype)

        if nk == 1:
            finish(part)
        else:
            @pl.when(k == 0)
            def _():
                acc_ref[...] = part

            @pl.when(k > 0)
            def _():
                acc_ref[...] += part

            @pl.when(k == nk - 1)
            def _():
                finish(acc_ref[...])

    a_spec = pl.BlockSpec((tk, tm), lambda i, j, k: (k, i)) if ta else pl.BlockSpec((tm, tk), lambda i, j, k: (i, k))
    b_spec = pl.BlockSpec((tn, tk), lambda i, j, k: (j, k)) if tb else pl.BlockSpec((tk, tn), lambda i, j, k: (k, j))
    o_spec = pl.BlockSpec((tm, tn), lambda i, j, k: (i, j))
    c_specs = [pl.BlockSpec(c.shape, lambda i, j, k, nd=c.ndim: (0,) * nd) for c in consts]
    outs = pl.pallas_call(
        body, name=name,
        grid=(m // tm, n // tn, nk),
        in_specs=[a_spec, b_spec] + [o_spec] * n_e + c_specs,
        out_specs=[o_spec] * n_o,
        out_shape=[jax.ShapeDtypeStruct((m, n), dt) for dt in out_dtypes],
        scratch_shapes=[pltpu.VMEM((tm, tn), F32)] if nk > 1 else [],
        compiler_params=_cparams(("parallel", "parallel", "arbitrary")),
    )(a, b, *epi_ins, *consts)
    return outs[0] if n_o == 1 else outs


def _rows(arr, tr):
    return (arr, pl.BlockSpec((tr, arr.shape[1]), lambda i: (i, 0)))


def _win(arr, tr, start, width):
    bw = math.gcd(start, width) if start else width
    assert bw % LANES == 0
    return [(arr, pl.BlockSpec((tr, bw), lambda i, cb=start // bw + p: (i, cb))) for p in range(width // bw)]


def _ct(arr):
    return (arr, pl.BlockSpec((None,) + arr.shape[1:], lambda i: (i, 0, 0)))


def _whole(arr):
    return (arr, pl.BlockSpec(arr.shape, lambda i, nd=arr.ndim: (0,) * nd))


def _cat(refs):
    vals = [r[...] for r in refs]
    return vals[0] if len(vals) == 1 else jnp.concatenate(vals, axis=1)


def _seqtiled(name, fn, n_tiles, ins, outs, acc_widths=()):
    n_i, n_o, n_a = len(ins), len(outs), len(acc_widths)

    def body(*refs):
        i_refs, o_refs, a_refs = refs[:n_i], refs[n_i:n_i + n_o], refs[n_i + n_o:]
        if n_a:
            @pl.when(pl.program_id(0) == 0)
            def _():
                for r in a_refs:
                    r[...] = jnp.zeros(r.shape, F32)
        fn(list(i_refs), list(o_refs), list(a_refs))

    res = pl.pallas_call(
        body, name=name, grid=(n_tiles,),
        in_specs=[s for _, s in ins],
        out_specs=[s for _, _, s in outs] + [pl.BlockSpec((8, w), lambda i: (0, 0)) for w in acc_widths],
        out_shape=[jax.ShapeDtypeStruct(sh, dt) for sh, dt, _ in outs] + [jax.ShapeDtypeStruct((8, w), F32) for w in acc_widths],
        compiler_params=_cparams(("arbitrary",)),
    )(*[a for a, _ in ins])
    return res


def _acc_add(acc_ref, val):
    acc_ref[0:1, :] += jnp.sum(val, axis=0, keepdims=True)


def _out_rows(s, w, dt, tr):
    return ((s, w), dt, pl.BlockSpec((tr, w), lambda i: (i, 0)))


def _out_ct(s, w, dt, t):
    return ((s // t, w, t), dt, pl.BlockSpec((None, w, t), lambda i: (i, 0, 0)))


def _rms_fwd(x, gain):
    r = lax.rsqrt(jnp.mean(x * x, axis=-1, keepdims=True) + NORM_EPS)
    return x * r * gain


def _rms_bwd(dy, x, gain):
    r = lax.rsqrt(jnp.mean(x * x, axis=-1, keepdims=True) + NORM_EPS)
    xn = x * r
    dyg = dy * gain
    dx = r * (dyg - xn * jnp.mean(dyg * xn, axis=-1, keepdims=True))
    return dx, dy * xn


def _seg_mean(y, hd):
    w = y.shape[1]
    pieces = []
    for s in range(0, w, LANES):
        v = y[:, s:s + LANES]
        tot = jnp.sum(v, axis=1, keepdims=True)
        if hd == LANES:
            pieces.append(jnp.broadcast_to(tot, v.shape))
        else:
            low = lax.broadcasted_iota(jnp.int32, v.shape, 1) < hd
            lo = jnp.sum(jnp.where(low, v, 0.0), axis=1, keepdims=True)
            pieces.append(jnp.where(low, lo, tot - lo))
    out = pieces[0] if len(pieces) == 1 else jnp.concatenate(pieces, axis=1)
    return out * (1.0 / hd)


def _tile_lanes(t, w):
    return t if w == t.shape[1] else jnp.concatenate([t] * (w // t.shape[1]), axis=1)


def _swap_halves(x, hd):
    w = x.shape[1]
    half = hd // 2
    lane = lax.broadcasted_iota(jnp.int32, x.shape, 1)
    return jnp.where((lane % hd) < half, pltpu.roll(x, w - half, 1), pltpu.roll(x, half, 1))


def _rope(x, cos, sin_signed, hd):
    w = x.shape[1]
    return x * _tile_lanes(cos, w) + _swap_halves(x, hd) * _tile_lanes(sin_signed, w)


def _rope_t(dy, cos, sin_signed, hd):
    w = dy.shape[1]
    return dy * _tile_lanes(cos, w) + _swap_halves(dy * _tile_lanes(sin_signed, w), hd)


def _headnorm_fwd(x, gain_w, hd):
    r = lax.rsqrt(_seg_mean(x * x, hd) + NORM_EPS)
    return x * r * gain_w


def _headnorm_bwd(dy, x, gain_w, hd):
    r = lax.rsqrt(_seg_mean(x * x, hd) + NORM_EPS)
    xn = x * r
    dyg = dy * gain_w
    return r * (dyg - xn * _seg_mean(dyg * xn, hd)), dy * xn


def _sigmoid(x):
    return 1.0 / (1.0 + jnp.exp(-x))


def _rope_tables(seq_len, head_dim):
    rows = seq_len // GRID_W
    n_axis = head_dim // 4
    freqs = ROPE_THETA ** (-jnp.arange(n_axis, dtype=F32) / n_axis)
    row = jnp.repeat(jnp.arange(rows, dtype=F32), GRID_W)
    col = jnp.tile(jnp.arange(GRID_W, dtype=F32), rows)
    ang = jnp.concatenate([row[:, None] * freqs, col[:, None] * freqs], axis=-1)
    cos, sin = jnp.cos(ang), jnp.sin(ang)
    reps = LANES // head_dim
    return jnp.tile(jnp.concatenate([cos, cos], axis=-1), (1, reps)), jnp.tile(jnp.concatenate([-sin, sin], axis=-1), (1, reps))


def _stage_norm_in(x, gain):
    s = x.shape[0]
    tr = min(SEQ_TILE, s)

    def fn(i, o, a):
        o[0][...] = _rms_fwd(i[0][...], i[1][...]).astype(BF16)

    return _seqtiled("norm_in", fn, s // tr, [_rows(x, tr), _whole(gain)], [_out_rows(s, D_MODEL, BF16, tr)])[0]


def _stage_qkv(proj, tabs, gq_w, gk_w):
    s = proj.shape[0]
    t = _seq_tile(s)
    ca, sa, cr, sr = tabs
    ins = (_win(proj, t, C_AQ, ATTN_Q_W) + _win(proj, t, C_AK, ATTN_KV_W) + _win(proj, t, C_AV, ATTN_KV_W)
           + _win(proj, t, C_RQ, RET_W) + _win(proj, t, C_RK, RET_W)
           + [_rows(ca, t), _rows(sa, t), _rows(cr, t), _rows(sr, t), _whole(gq_w), _whole(gk_w)])

    def fn(i, o, a):
        aq, ak, av = i[0][...], i[1][...], i[2][...]
        rq, rk = _cat(i[3:5]), _cat(i[5:7])
        ca_, sa_, cr_, sr_ = i[7][...], i[8][...], i[9][...], i[10][...]
        qr = _rope(_headnorm_fwd(aq, i[11][...], ATTN_HEAD_DIM), ca_, sa_, ATTN_HEAD_DIM) * Q_FOLD
        kr = _rope(_headnorm_fwd(ak, i[12][...], ATTN_HEAD_DIM), ca_, sa_, ATTN_HEAD_DIM)
        qt = qr.T.astype(BF16)
        zeros = jnp.zeros((ATTN_HEAD_DIM, t), BF16)
        for h in range(ATTN_HEADS):
            g = h // ATTN_GROUP
            blk = qt[h * ATTN_HEAD_DIM:(h + 1) * ATTN_HEAD_DIM, :]
            o[0][h * LANES + g * ATTN_HEAD_DIM:h * LANES + (g + 1) * ATTN_HEAD_DIM, :] = blk
            o[0][h * LANES + (1 - g) * ATTN_HEAD_DIM:h * LANES + (2 - g) * ATTN_HEAD_DIM, :] = zeros
        o[1][...] = kr.astype(BF16)
        o[2][...] = kr.T.astype(BF16)
        o[3][...] = av.astype(BF16)
        o[4][...] = av.T.astype(BF16)
        o[5][...] = _rope(rq, cr_, sr_, RET_HEAD_DIM) * RET_SCALE
        o[6][...] = _rope(rk, cr_, sr_, RET_HEAD_DIM)

    outs = [_out_ct(s, ATTN_HEADS * LANES, BF16, t), _out_rows(s, ATTN_KV_W, BF16, t), _out_ct(s, ATTN_KV_W, BF16, t),
            _out_rows(s, ATTN_KV_W, BF16, t), _out_ct(s, ATTN_KV_W, BF16, t), _out_rows(s, RET_W, F32, t), _out_rows(s, RET_W, F32, t)]
    return _seqtiled("qkv_prep", fn, s // t, ins, outs)


def _groupnorm_gate(ry, rg, gain):
    mu = _seg_mean(ry, RET_HEAD_DIM)
    d = ry - mu
    rs = lax.rsqrt(_seg_mean(d * d, RET_HEAD_DIM) + GN_EPS)
    return d * rs, rs, _sigmoid(rg)


def _stage_mix_post(ry_f, ry_b, proj, o_ct, gain):
    s = proj.shape[0]
    t = _seq_tile(s)
    ins = [_rows(ry_f, t), _rows(ry_b, t)] + _win(proj, t, C_RG, RET_W) + [_ct(o_ct), _whole(gain)]

    def fn(i, o, a):
        ry = i[0][...] + i[1][...]
        rg = _cat(i[2:4])
        gn, _, sg = _groupnorm_gate(ry, rg, None)
        o[0][...] = (gn * i[5][...] * (rg * sg)).astype(BF16)
        o[1][...] = i[4][...].astype(F32).T.astype(BF16)

    return _seqtiled("mix_post", fn, s // t, ins, [_out_rows(s, RET_W, BF16, t), _out_rows(s, ATTN_Q_W, BF16, t)])


def _stage_merge(proj, a_out, r_out):
    s = proj.shape[0]
    tr = min(SEQ_TILE, s)
    ins = _win(proj, tr, C_GA, D_MODEL) + _win(proj, tr, C_GR, D_MODEL) + [_rows(a_out, tr), _rows(r_out, tr)]
    na = len(_win(proj, tr, C_GA, D_MODEL))

    def fn(i, o, a):
        ga, gr = _cat(i[:na]), _cat(i[na:2 * na])
        o[0][...] = (_sigmoid(ga) * i[2 * na][...] + _sigmoid(gr) * i[2 * na + 1][...]).astype(BF16)

    return _seqtiled("merge", fn, s // tr, ins, [_out_rows(s, D_MODEL, BF16, tr)])[0]


def _stage_head(zg, pe, x2, target, g_final):
    s = x2.shape[0]
    tr = min(SEQ_TILE // 2, s)
    ins = [_rows(zg, tr), _rows(pe, tr), _rows(x2, tr), _rows(target, tr), _whole(g_final)]

    def fn(i, o, a):
        gt = _sigmoid(i[0][...])
        pe_ = i[1][...]
        x3 = i[2][...] + gt * pe_
        gf = i[4][...]
        r3 = lax.rsqrt(jnp.mean(x3 * x3, axis=-1, keepdims=True) + NORM_EPS)
        x3n = x3 * r3
        e = x3n * gf - i[3][...]
        _acc_add(a[0], e * e)
        dy = e * (1.0 / D_MODEL)
        _acc_add(a[1], dy * x3n)
        dyg = dy * gf
        dx3 = r3 * (dyg - x3n * jnp.mean(dyg * x3n, axis=-1, keepdims=True))
        o[0][...] = dx3
        o[1][...] = (dx3 * pe_ * gt * (1.0 - gt)).astype(BF16)
        o[2][...] = (dx3 * gt).astype(BF16)

    outs = [_out_rows(s, D_MODEL, F32, tr), _out_rows(s, D_MODEL, BF16, tr), _out_rows(s, D_MODEL, BF16, tr)]
    return _seqtiled("head", fn, s // tr, ins, outs, acc_widths=(D_MODEL, D_MODEL))


def _stage_norm_bwd(name, dh, x, dres, gain):
    s = x.shape[0]
    tr = min(SEQ_TILE // 2, s)

    def fn(i, o, a):
        dx, dg = _rms_bwd(i[0][...], i[1][...], i[3][...])
        o[0][...] = i[2][...] + dx
        _acc_add(a[0], dg)

    return _seqtiled(name, fn, s // tr, [_rows(dh, tr), _rows(x, tr), _rows(dres, tr), _whole(gain)],
                     [_out_rows(s, D_MODEL, F32, tr)], acc_widths=(D_MODEL,))


def _stage_merge_bwd(proj, dmerged, a_out, r_out):
    s = proj.shape[0]
    tr = min(SEQ_TILE // 2, s)
    wins = _win(proj, tr, C_GA, D_MODEL)
    na = len(wins)
    ins = wins + _win(proj, tr, C_GR, D_MODEL) + [_rows(dmerged, tr), _rows(a_out, tr), _rows(r_out, tr)]

    def fn(i, o, a):
        sa, sr = _sigmoid(_cat(i[:na])), _sigmoid(_cat(i[na:2 * na]))
        dm = i[2 * na][...]
        o[0][...] = (dm * sa).astype(BF16)
        o[1][...] = (dm * sr).astype(BF16)
        o[2][...] = (dm * i[2 * na + 1][...] * sa * (1.0 - sa)).astype(BF16)
        o[3][...] = (dm * i[2 * na + 2][...] * sr * (1.0 - sr)).astype(BF16)

    return _seqtiled("merge_bwd", fn, s // tr, ins, [_out_rows(s, D_MODEL, BF16, tr)] * 4)


def _stage_mix_post_bwd(dattn, attn_rows, drz, ry_f, ry_b, proj, gain):
    s = proj.shape[0]
    t = _seq_tile(s)
    ins = ([_rows(dattn, t), _rows(attn_rows, t), _rows(drz, t), _rows(ry_f, t), _rows(ry_b, t)]
           + _win(proj, t, C_RG, RET_W) + [_whole(gain)])

    def fn(i, o, a):
        da = i[0][...]
        dat = da.T
        prod_t = (da * i[1][...].astype(F32)).T
        dat_b = dat.astype(BF16)
        zeros = jnp.zeros((ATTN_HEAD_DIM, t), BF16)
        for h in range(ATTN_HEADS):
            g = h // ATTN_GROUP
            o[0][h * LANES + g * ATTN_HEAD_DIM:h * LANES + (g + 1) * ATTN_HEAD_DIM, :] = dat_b[h * ATTN_HEAD_DIM:(h + 1) * ATTN_HEAD_DIM, :]
            o[0][h * LANES + (1 - g) * ATTN_HEAD_DIM:h * LANES + (2 - g) * ATTN_HEAD_DIM, :] = zeros
            o[1][h] = jnp.sum(prod_t[h * ATTN_HEAD_DIM:(h + 1) * ATTN_HEAD_DIM, :], axis=0, keepdims=True)
        ry = i[3][...] + i[4][...]
        rg = _cat(i[5:7])
        gain_ = i[7][...]
        gn, rs, sg = _groupnorm_gate(ry, rg, None)
        dz = i[2][...]
        silu = rg * sg
        _acc_add(a[0], dz * gn * silu)
        dgn = dz * gain_ * silu
        o[2][...] = rs * (dgn - _seg_mean(dgn, RET_HEAD_DIM) - gn * _seg_mean(dgn * gn, RET_HEAD_DIM))
        o[3][...] = (dz * gn * gain_ * (sg * (1.0 + rg * (1.0 - sg)))).astype(BF16)

    outs = [_out_ct(s, ATTN_HEADS * LANES, BF16, t),
            ((ATTN_HEADS, s // t, 1, t), F32, pl.BlockSpec((ATTN_HEADS, None, 1, t), lambda i: (0, i, 0, 0))),
            _out_rows(s, RET_W, F32, t), _out_rows(s, RET_W, BF16, t)]
    return _seqtiled("mix_post_bwd", fn, s // t, ins, outs, acc_widths=(RET_W,))


def _stage_dproj(proj, dq_ct, dk8, dv8, rgrads, drg, dga, dgr, tabs, gq_w, gk_w):
    s = proj.shape[0]
    t = _seq_tile(s)
    ca, sa, cr, sr = tabs
    kv8 = pl.BlockSpec((ATTN_HEADS, t, ATTN_KV_W), lambda i: (0, i, 0))
    ins = (_win(proj, t, C_AQ, ATTN_Q_W) + _win(proj, t, C_AK, ATTN_KV_W) + [_ct(dq_ct), (dk8, kv8), (dv8, kv8)]
           + [_rows(g, t) for g in rgrads] + [_rows(drg, t), _rows(dga, t), _rows(dgr, t)]
           + [_rows(ca, t), _rows(sa, t), _rows(cr, t), _rows(sr, t), _whole(gq_w), _whole(gk_w)])

    def fn(i, o, a):
        aq, ak = i[0][...], i[1][...]
        dq_f, dk_f, dv_f, dq_b, dk_b, dv_b = (r[...] for r in i[5:11])
        ca_, sa_, cr_, sr_ = i[14][...], i[15][...], i[16][...], i[17][...]
        dqn = _rope_t(i[2][...].T * ATTN_SCALE, ca_, sa_, ATTN_HEAD_DIM)
        daq, gq_rows = _headnorm_bwd(dqn, aq, i[18][...], ATTN_HEAD_DIM)
        dkn = _rope_t(jnp.sum(i[3][...].astype(F32), axis=0) * (1.0 / LOG2E), ca_, sa_, ATTN_HEAD_DIM)
        dak, gk_rows = _headnorm_bwd(dkn, ak, i[19][...], ATTN_HEAD_DIM)
        _acc_add(a[0], gq_rows)
        _acc_add(a[1], gk_rows)
        out = o[0]
        out[:, C_AQ:C_AQ + ATTN_Q_W] = daq.astype(BF16)
        out[:, C_AK:C_AK + ATTN_KV_W] = dak.astype(BF16)
        out[:, C_AV:C_AV + ATTN_KV_W] = jnp.sum(i[4][...].astype(F32), axis=0).astype(BF16)
        out[:, C_RQ:C_RQ + RET_W] = _rope_t((dq_f + dq_b) * RET_SCALE, cr_, sr_, RET_HEAD_DIM).astype(BF16)
        out[:, C_RK:C_RK + RET_W] = _rope_t(dk_f + dk_b, cr_, sr_, RET_HEAD_DIM).astype(BF16)
        out[:, C_RV:C_RV + RET_W] = (dv_f + dv_b).astype(BF16)
        out[:, C_RG:C_RG + RET_W] = i[11][...]
        out[:, C_GA:C_GA + D_MODEL] = i[12][...]
        out[:, C_GR:C_GR + D_MODEL] = i[13][...]

    return _seqtiled("dproj", fn, s // t, ins, [_out_rows(s, IN_W, BF16, t)], acc_widths=(ATTN_Q_W, ATTN_KV_W))


def _attn_fwd(q_ct, k_rows, v_ct):
    nq, _, t = q_ct.shape
    s = nq * t
    nk = nq
    assert nk % 2 == 0
    n_par = 2

    def body(q_ref, k_ref, v_ref, o_ref, lse_ref, *bufs):
        sbuf = (bufs[0:2], bufs[2:4])
        pbuf = (bufs[4:6], bufs[6:8])

        def scores(w, j, slot):
            kj = k_ref[pl.ds(pl.multiple_of(j * t, t), t), :]
            st = jnp.dot(kj, q_ref[w], preferred_element_type=F32)
            sbuf[w][slot][...] = st
            return jnp.max(st, axis=0, keepdims=True)

        def probs(w, slot, cmax, m, l):
            m_new = jnp.maximum(m, cmax)
            alpha = jnp.exp2(m - m_new)
            pt = jnp.exp2(sbuf[w][slot][...] - m_new)
            pbuf[w][slot][...] = pt.astype(BF16)
            return m_new, alpha * l + jnp.sum(pt, axis=0, keepdims=True), alpha

        def values(w, j, slot, alpha, acc):
            return alpha * acc + jnp.dot(v_ref[j], pbuf[w][slot][...], preferred_element_type=F32)

        init = []
        for w in range(n_par):
            m = jnp.full((1, t), -1e30, F32)
            l = jnp.zeros((1, t), F32)
            cmax0 = scores(w, 0, 0)
            cmax1 = scores(w, 1, 1)
            m, l, alpha0 = probs(w, 0, cmax0, m, l)
            init.append((m, l, jnp.zeros((ATTN_HEAD_DIM, t), F32), cmax1, alpha0))

        def trip(n, carry):
            c = 2 * n
            out = []
            for w in range(n_par):
                m, l, acc, cmax_b, alpha_c = carry[w]
                acc = values(w, c, 0, alpha_c, acc)
                m, l, alpha1 = probs(w, 1, cmax_b, m, l)
                cmax2 = scores(w, c + 2, 0)
                acc = values(w, c + 1, 1, alpha1, acc)
                m, l, alpha2 = probs(w, 0, cmax2, m, l)
                cmax3 = scores(w, c + 3, 1)
                out.append((m, l, acc, cmax3, alpha2))
            return tuple(out)

        res = lax.fori_loop(0, nk // 2 - 1, trip, tuple(init))
        for w in range(n_par):
            m, l, acc, cmax_b, alpha_c = res[w]
            acc = values(w, nk - 2, 0, alpha_c, acc)
            m, l, alpha1 = probs(w, 1, cmax_b, m, l)
            acc = values(w, nk - 1, 1, alpha1, acc)
            o_ref[w] = (acc / l).astype(BF16)
            lse_ref[w] = m + jnp.log2(l)

    return pl.pallas_call(
        body, name="attn_fwd", grid=(ATTN_HEADS, nq // n_par),
        in_specs=[pl.BlockSpec((n_par, LANES, t), lambda h, i: (i, h, 0)),
                  pl.BlockSpec((s, ATTN_KV_W), lambda h, i: (0, 0)),
                  pl.BlockSpec((nk, ATTN_HEAD_DIM, t), lambda h, i: (0, h // ATTN_GROUP, 0))],
        out_specs=[pl.BlockSpec((n_par, ATTN_HEAD_DIM, t), lambda h, i: (i, h, 0)),
                   pl.BlockSpec((None, n_par, 1, t), lambda h, i: (h, i, 0, 0))],
        out_shape=[jax.ShapeDtypeStruct((nq, ATTN_Q_W, t), BF16), jax.ShapeDtypeStruct((ATTN_HEADS, nq, 1, t), F32)],
        scratch_shapes=[pltpu.VMEM((t, t), F32)] * (2 * n_par) + [pltpu.VMEM((t, t), BF16)] * (2 * n_par),
        compiler_params=_cparams(("parallel", "parallel")),
    )(q_ct, k_rows, v_ct)


def _attn_bwd(q_ct, do_ct, lse, delta, k_rows, v_rows, k_ct):
    nq, _, t = q_ct.shape
    s = nq * t
    nk = nq

    assert nq % 2 == 0

    def body(q_ref, do_ref, lse_ref, delta_ref, k_ref, v_ref, kt_ref, dq_ref, dk_ref, dv_ref, dk_acc, dv_acc,
             sb0, sb1, db0, db1, pb0, pb1, gb0, gb1):
        j = pl.program_id(1)
        sb, db, pb, gb = (sb0, sb1), (db0, db1), (pb0, pb1), (gb0, gb1)

        @pl.when(j == 0)
        def _():
            dq_ref[...] = jnp.zeros(dq_ref.shape, F32)

        kj, vj, ktj = k_ref[...], v_ref[...], kt_ref[...]
        dk_acc[...] = jnp.zeros(dk_acc.shape, F32)
        dv_acc[...] = jnp.zeros(dv_acc.shape, F32)

        def products(i, slot):
            sb[slot][...] = jnp.dot(kj, q_ref[i], preferred_element_type=F32)
            db[slot][...] = jnp.dot(vj, do_ref[i], preferred_element_type=F32)

        def cotangents(i, slot):
            pt = jnp.exp2(sb[slot][...] - lse_ref[i])
            pb[slot][...] = pt.astype(BF16)
            gb[slot][...] = (pt * (db[slot][...] - delta_ref[i])).astype(BF16)

        def accumulate(i, slot):
            dst = gb[slot][...]
            dv_acc[...] += _nt(pb[slot][...], do_ref[i])
            dk_acc[...] += _nt(dst, q_ref[i])
            dq_ref[i] += jnp.dot(ktj, dst, preferred_element_type=F32)

        products(0, 0)
        products(1, 1)
        cotangents(0, 0)

        def trip(n, carry):
            c = 2 * n
            accumulate(c, 0)
            cotangents(c + 1, 1)
            products(c + 2, 0)
            accumulate(c + 1, 1)
            cotangents(c + 2, 0)
            products(c + 3, 1)
            return carry

        lax.fori_loop(0, nq // 2 - 1, trip, 0)
        accumulate(nq - 2, 0)
        cotangents(nq - 1, 1)
        accumulate(nq - 1, 1)
        dk_ref[...] = dk_acc[...].astype(dk_ref.dtype)
        dv_ref[...] = dv_acc[...].astype(dv_ref.dtype)

    per_head = pl.BlockSpec((nq, LANES, t), lambda h, j: (0, h, 0))
    stat = pl.BlockSpec((None, nq, 1, t), lambda h, j: (h, 0, 0, 0))
    kv_rows = pl.BlockSpec((t, ATTN_KV_W), lambda h, j: (j, 0))
    kv_out = pl.BlockSpec((None, t, ATTN_KV_W), lambda h, j: (h, j, 0))
    return pl.pallas_call(
        body, name="attn_bwd", grid=(ATTN_HEADS, nk),
        in_specs=[per_head, per_head, stat, stat, kv_rows, kv_rows,
                  pl.BlockSpec((None, ATTN_HEAD_DIM, t), lambda h, j: (j, h // ATTN_GROUP, 0))],
        out_specs=[pl.BlockSpec((nq, ATTN_HEAD_DIM, t), lambda h, j: (0, h, 0)), kv_out, kv_out],
        out_shape=[jax.ShapeDtypeStruct((nq, ATTN_Q_W, t), F32), jax.ShapeDtypeStruct((ATTN_HEADS, s, ATTN_KV_W), BF16),
                   jax.ShapeDtypeStruct((ATTN_HEADS, s, ATTN_KV_W), BF16)],
        scratch_shapes=([pltpu.VMEM((t, ATTN_KV_W), F32)] * 2 + [pltpu.VMEM((t, t), F32)] * 4 + [pltpu.VMEM((t, t), BF16)] * 4),
        compiler_params=_cparams(("parallel", "arbitrary")),
    )(q_ct, do_ct, lse, delta, k_rows, v_rows, k_ct)


def _log_sigmoid(x):
    t = jnp.exp(-jnp.abs(x))
    log1p_t = jnp.where(t < 1e-2, t * (1.0 - t * (0.5 - t * (1.0 / 3.0))), jnp.log(1.0 + t))
    return jnp.minimum(x, 0.0) - log1p_t


def _decay_tables(logit, backward):
    c = RET_CHUNK
    lam = _log_sigmoid(jnp.full((c, c), logit, F32))
    ii = lax.broadcasted_iota(jnp.int32, (c, c), 0).astype(F32)
    jj = lax.broadcasted_iota(jnp.int32, (c, c), 1).astype(F32)
    if not backward:
        dist, dist_t = jnp.maximum(ii - jj, 0.0), jnp.maximum(jj - ii, 0.0)
        mask, mask_t = ii >= jj, jj >= ii
        e_q, e_k = ii + 1.0, (c - 1.0) - ii
    else:
        dist, dist_t = jnp.maximum(jj - ii, 0.0), jnp.maximum(ii - jj, 0.0)
        mask, mask_t = jj > ii, ii > jj
        e_q, e_k = c - ii, ii
    return dict(
        d=jnp.where(mask, jnp.exp(lam * dist), 0.0), d_t=jnp.where(mask_t, jnp.exp(lam * dist_t), 0.0), dist=dist,
        qdec=jnp.exp(lam * e_q), kdec=jnp.exp(lam * e_k), e_q=e_q, e_k=e_k, gam=jnp.exp(lam * c))


def _nt(a, b):
    return lax.dot_general(a, b, (((1,), (1,)), ((), ())), preferred_element_type=F32)


def _ret_fwd(logits, q, k, proj):
    s = q.shape[0]
    c = RET_CHUNK
    nc = s // c
    chunk = (lambda n: n, lambda n: nc - 1 - n)
    vwin = _win(proj, c, C_RV, RET_W)
    nv = len(vwin)
    vw = RET_W // nv
    per = 2 + nv

    def body(lg_ref, *refs):
        ins, outs, states = refs[:2 * per], refs[2 * per:2 * per + 4], refs[2 * per + 4:]

        @pl.when(pl.program_id(0) == 0)
        def _():
            for st in states:
                st[...] = jnp.zeros(st.shape, F32)

        for h in range(RET_HEADS):
            for d in range(2):
                q_ref, k_ref, v_refs = ins[d * per], ins[d * per + 1], ins[d * per + 2:(d + 1) * per]
                y_ref, st_ref, state = outs[2 * d], outs[2 * d + 1], states[d]
                tb = _decay_tables(lg_ref[d, h], bool(d))
                sl = slice(h * RET_HEAD_DIM, (h + 1) * RET_HEAD_DIM)
                qh, kh = q_ref[:, sl], k_ref[:, sl]
                off = h * RET_HEAD_DIM
                vb = v_refs[off // vw][:, off % vw:off % vw + RET_HEAD_DIM].astype(BF16)
                a = _nt(qh.astype(BF16), kh.astype(BF16)) * tb["d"]
                sh = state[h]
                st_ref[h] = sh
                y_ref[:, sl] = (jnp.dot(a.astype(BF16), vb, preferred_element_type=F32)
                                + jnp.dot((qh * tb["qdec"]).astype(BF16), sh.astype(BF16), preferred_element_type=F32))
                state[h] = tb["gam"] * sh + jnp.dot((kh * tb["kdec"]).T.astype(BF16), vb, preferred_element_type=F32)

    hmat = (RET_HEADS, RET_HEAD_DIM, RET_HEAD_DIM)
    in_specs, out_specs, args = [pl.BlockSpec(memory_space=pltpu.SMEM)], [], [logits]
    for d in range(2):
        rows = pl.BlockSpec((c, RET_W), lambda n, d=d: (chunk[d](n), 0))
        in_specs += [rows, rows] + [pl.BlockSpec(sp.block_shape, lambda n, d=d, cb=sp.index_map(0)[1]: (chunk[d](n), cb)) for _, sp in vwin]
        args += [q, k] + [a for a, _ in vwin]
        out_specs += [rows, pl.BlockSpec((None,) + hmat, lambda n, d=d: (chunk[d](n), 0, 0, 0))]
    return pl.pallas_call(
        body, name="ret_fwd", grid=(nc,), in_specs=in_specs, out_specs=out_specs,
        out_shape=[jax.ShapeDtypeStruct((s, RET_W), F32), jax.ShapeDtypeStruct((nc,) + hmat, F32)] * 2,
        scratch_shapes=[pltpu.VMEM(hmat, F32)] * 2,
        compiler_params=_cparams(("arbitrary",)),
    )(*args)


def _ret_bwd(logits, q, k, proj, dy, st_f, st_b):
    s = q.shape[0]
    c = RET_CHUNK
    nc = s // c
    chunk = (lambda n: nc - 1 - n, lambda n: n)
    vwin = _win(proj, c, C_RV, RET_W)
    nv = len(vwin)
    vw = RET_W // nv
    per = 4 + nv

    def body(lg_ref, *refs):
        ins, outs, scr = refs[:2 * per], refs[2 * per:2 * per + 8], refs[2 * per + 8:]
        n = pl.program_id(0)

        @pl.when(n == 0)
        def _():
            for r in scr:
                r[...] = jnp.zeros(r.shape, F32)

        for h in range(RET_HEADS):
            for d in range(2):
                q_ref, k_ref, dy_ref, st_ref = ins[d * per:d * per + 4]
                v_refs = ins[d * per + 4:(d + 1) * per]
                dq_ref, dk_ref, dv_ref = outs[4 * d:4 * d + 3]
                dstate, lacc = scr[2 * d], scr[2 * d + 1]
                tb = _decay_tables(lg_ref[d, h], bool(d))
                sl = slice(h * RET_HEAD_DIM, (h + 1) * RET_HEAD_DIM)
                qh, kh, dyh = q_ref[:, sl], k_ref[:, sl], dy_ref[:, sl]
                off = h * RET_HEAD_DIM
                vb = v_refs[off // vw][:, off % vw:off % vw + RET_HEAD_DIM].astype(BF16)
                qb, kb, dyb = qh.astype(BF16), kh.astype(BF16), dyh.astype(BF16)
                sh, dsh = st_ref[h], dstate[h]
                shb, dshb = sh.astype(BF16), dsh.astype(BF16)
                qk = _nt(qb, kb)
                g = _nt(dyb, vb) * tb["d"]
                a_t = _nt(kb, qb) * tb["d_t"]
                g_t = _nt(vb, dyb) * tb["d_t"]
                qd, kd = qh * tb["qdec"], kh * tb["kdec"]
                dqd = _nt(dyb, shb)
                dkd = _nt(vb, dshb)
                dq_ref[:, sl] = jnp.dot(g.astype(BF16), kb, preferred_element_type=F32) + dqd * tb["qdec"]
                dk_ref[:, sl] = jnp.dot(g_t.astype(BF16), qb, preferred_element_type=F32) + dkd * tb["kdec"]
                dv_ref[:, sl] = (jnp.dot(a_t.astype(BF16), dyb, preferred_element_type=F32)
                                 + jnp.dot(kd.astype(BF16), dshb, preferred_element_type=F32))
                lacc[h] += (tb["dist"] * qk * g + tb["e_q"] * qd * dqd + tb["e_k"] * kd * dkd
                            + float(c) * tb["gam"] * dsh * sh)
                dstate[h] = tb["gam"] * dsh + jnp.dot(qd.T.astype(BF16), dyb, preferred_element_type=F32)

        @pl.when(n == nc - 1)
        def _():
            for d in range(2):
                for h in range(RET_HEADS):
                    outs[4 * d + 3][h] = jnp.zeros((8, LANES), F32) + jnp.sum(scr[2 * d + 1][h])

    hmat = (RET_HEADS, RET_HEAD_DIM, RET_HEAD_DIM)
    in_specs, out_specs, args = [pl.BlockSpec(memory_space=pltpu.SMEM)], [], [logits]
    for d, states in enumerate((st_f, st_b)):
        rows = pl.BlockSpec((c, RET_W), lambda n, d=d: (chunk[d](n), 0))
        in_specs += ([rows, rows, rows, pl.BlockSpec((None,) + hmat, lambda n, d=d: (chunk[d](n), 0, 0, 0))]
                     + [pl.BlockSpec(sp.block_shape, lambda n, d=d, cb=sp.index_map(0)[1]: (chunk[d](n), cb)) for _, sp in vwin])
        args += [q, k, dy, states] + [a for a, _ in vwin]
        out_specs += [rows, rows, rows, pl.BlockSpec((RET_HEADS, 8, LANES), lambda n: (0, 0, 0))]
    return pl.pallas_call(
        body, name="ret_bwd", grid=(nc,), in_specs=in_specs, out_specs=out_specs,
        out_shape=([jax.ShapeDtypeStruct((s, RET_W), F32)] * 3 + [jax.ShapeDtypeStruct((RET_HEADS, 8, LANES), F32)]) * 2,
        scratch_shapes=[pltpu.VMEM(hmat, F32)] * 4,
        compiler_params=_cparams(("arbitrary",)),
    )(*args)


def _local_step(x, p, target, w, small):
    s = x.shape[0]
    tabs = _rope_tables(s, ATTN_HEAD_DIM) + _rope_tables(s, RET_HEAD_DIM)
    g_mix, g_mlp, g_ple = small["mix_norm"][None, :], small["mlp_norm"][None, :], small["ple_norm"][None, :]
    g_final, g_ret = small["final_norm"][None, :], small["ret_norm_gain"][None, :]
    gq_w = jnp.tile(small["attn_q_norm"], ATTN_HEADS)[None, :]
    gk_w = jnp.tile(small["attn_k_norm"], ATTN_KV_HEADS)[None, :]
    logits = small["ret_decay_logit"]

    hb = _stage_norm_in(x, g_mix)
    proj = _mm("in_proj", hb, w["w_in"], tm=512, tn=IN_W // 2, tk=1024)
    q_ct, k_rows, k_ct, v_rows, v_ct, rq, rk = _stage_qkv(proj, tabs, gq_w, gk_w)
    o_ct, lse = _attn_fwd(q_ct, k_rows, v_ct)
    ry_f, st_f, ry_b, st_b = _ret_fwd(logits, rq, rk, proj)
    rz, attn_rows = _stage_mix_post(ry_f, ry_b, proj, o_ct, g_ret)
    a_out = _mm("attn_o", attn_rows, w["w_attn_o"], tm=1024, tn=1024, tk=512, out_dtypes=(BF16,))
    r_out = _mm("ret_o", rz, w["w_ret_o"], tm=1024, tn=1024, tk=512, out_dtypes=(BF16,))
    merged = _stage_merge(proj, a_out, r_out)

    def epi_res_norm(acc, e, c):
        xr = e[0][...] + acc
        return xr, _rms_fwd(xr, c[0][...])

    x1, hm = _mm("out_proj", merged, w["w_out"], tm=512, tn=1024, tk=1024, out_dtypes=(F32, BF16),
                 epi=epi_res_norm, epi_ins=(x,), consts=(g_mlp,))

    def epi_relu2(acc, e, c):
        r = jnp.maximum(acc, 0.0)
        return (r * r,)

    act = _mm("mlp_up", hm, w["w_up"], tm=512, tn=2048, tk=1024, out_dtypes=(BF16,), epi=epi_relu2)
    x2, hp = _mm("mlp_down", act, w["w_down"], tm=512, tn=1024, tk=2048, out_dtypes=(F32, BF16),
                 epi=epi_res_norm, epi_ins=(x1,), consts=(g_ple,))
    zg = _mm("ple_gate", hp, w["w_ple_gate"], tm=1024, tn=1024, tk=1024)
    pe = _mm("ple_emb", p, w["w_ple"], tm=1024, tn=1024, tk=256)
    dx3, dzg, dpe, loss_cols, g_final_p = _stage_head(zg, pe, x2, target, g_final)
    loss_sum = 0.5 / D_MODEL * jnp.sum(loss_cols)

    gw = {}
    gw["w_ple"] = _mm("g_w_ple", p, dpe, ta=True, tm=256, tn=1024, tk=2048)
    gw["w_ple_gate"] = _mm("g_w_ple_gate", hp, dzg, ta=True, tm=1024, tn=1024, tk=2048)
    dhp = _mm("d_hp", dzg, w["w_ple_gate"], tb=True, tm=1024, tn=1024, tk=1024)
    dx2, g_ple_p = _stage_norm_bwd("ple_norm_bwd", dhp, x2, dx3, g_ple)

    def epi_relu2_bwd(acc, e, c):
        return (acc * (2.0 * jnp.sqrt(e[0][...].astype(F32))),)

    du = _mm("d_u", dx2, w["w_down"], tb=True, tm=512, tn=2048, tk=1024, out_dtypes=(BF16,), epi=epi_relu2_bwd, epi_ins=(act,))
    gw["w_down"] = _mm("g_w_down", act, dx2, ta=True, tm=1024, tn=1024, tk=2048)
    gw["w_up"] = _mm("g_w_up", hm, du, ta=True, tm=1024, tn=1024, tk=2048)
    dhm = _mm("d_hm", du, w["w_up"], tb=True, tm=512, tn=1024, tk=2048)
    dx1, g_mlp_p = _stage_norm_bwd("mlp_norm_bwd", dhm, x1, dx2, g_mlp)
    dmerged = _mm("d_merged", dx1, w["w_out"], tb=True, tm=1024, tn=1024, tk=1024)
    gw["w_out"] = _mm("g_w_out", merged, dx1, ta=True, tm=1024, tn=1024, tk=2048)
    dao, dro, dga, dgr = _stage_merge_bwd(proj, dmerged, a_out, r_out)
    gw["w_attn_o"] = _mm("g_w_attn_o", attn_rows, dao, ta=True, tm=512, tn=1024, tk=2048)
    gw["w_ret_o"] = _mm("g_w_ret_o", rz, dro, ta=True, tm=512, tn=1024, tk=2048)
    dattn = _mm("d_attn", dao, w["w_attn_o"], tb=True, tm=1024, tn=512, tk=1024)
    drz = _mm("d_rz", dro, w["w_ret_o"], tb=True, tm=1024, tn=512, tk=1024)
    do_ct, delta, dry, drg, g_ret_p = _stage_mix_post_bwd(dattn, attn_rows, drz, ry_f, ry_b, proj, g_ret)
    dq_f, dk_f, dv_f, dl_f, dq_b, dk_b, dv_b, dl_b = _ret_bwd(logits, rq, rk, proj, dry, st_f, st_b)
    dq_ct, dk8, dv8 = _attn_bwd(q_ct, do_ct, lse, delta, k_rows, v_rows, k_ct)
    dproj, gq_p, gk_p = _stage_dproj(proj, dq_ct, dk8, dv8, (dq_f, dk_f, dv_f, dq_b, dk_b, dv_b), drg, dga, dgr, tabs, gq_w, gk_w)
    gw["w_in"] = _mm("g_w_in", hb, dproj, ta=True, tm=512, tn=IN_W // 2, tk=1024)
    dh = _mm("d_h", dproj, w["w_in"], tb=True, tm=512, tn=1024, tk=IN_W // 2)
    grad_x, g_mix_p = _stage_norm_bwd("mix_norm_bwd", dh, x, dx1, g_mix)

    gs = {
        "mix_norm": g_mix_p[0], "mlp_norm": g_mlp_p[0], "ple_norm": g_ple_p[0], "final_norm": g_final_p[0],
        "ret_norm_gain": g_ret_p[0],
        "attn_q_norm": jnp.sum(gq_p[0].reshape(ATTN_HEADS, ATTN_HEAD_DIM), axis=0),
        "attn_k_norm": jnp.sum(gk_p[0].reshape(ATTN_KV_HEADS, ATTN_HEAD_DIM), axis=0),
        "ret_decay_logit": jnp.stack([dl_f[:, 0, 0], dl_b[:, 0, 0]]),
    }
    return loss_sum, grad_x, gw, gs


PACK_COLS = 1024
N_CHIPS = 4
HALF_ROWS = 2048


def _pack_shard(parts):
    return jnp.concatenate([parts[n].reshape(-1, PACK_COLS) for n, _ in BIG], axis=0)


def _unpack_shard(slab, shapes):
    out, r = {}, 0
    for n, _ in BIG:
        rows = math.prod(shapes[n]) // PACK_COLS
        out[n] = slab[r:r + rows].reshape(shapes[n])
        r += rows
    return out


def _shard_of(full, axis, sidx):
    size = full.shape[axis] // N_CHIPS
    return lax.slice_in_dim(full, sidx * size, (sidx + 1) * size, axis=axis)


def _position():
    x, y, c = lax.axis_index("x"), lax.axis_index("y"), lax.axis_index("c")
    return x, y, c


def _other_chips(x, y):
    return [(1 - x, y), (x, 1 - y), (1 - x, 1 - y)]


ANY = pl.BlockSpec(memory_space=pl.ANY)


def _gather_weights(slab):
    rows = slab.shape[0]
    half = rows // 2

    def body(in_ref, out_ref, send_sems, recv_sems):
        x, y, c = _position()
        chips = _other_chips(x, y)

        def piece(chip, core):
            return out_ref.at[2 * chip[0] + chip[1], pl.ds(core * half, half), :]

        def copy(k, chip, core, to, src=None):
            return pltpu.make_async_remote_copy(
                src_ref=piece(chip, core) if src is None else src, dst_ref=piece(chip, core),
                send_sem=send_sems.at[k], recv_sem=recv_sems.at[k], device_id=to, device_id_type=MESH)

        first = [copy(j, (x, y), c, (*chip, c), src=in_ref.at[pl.ds(c * half, half), :]) for j, chip in enumerate(chips)]
        for cp in first:
            cp.start()
        passed = [copy(3 + j, chip, c, (x, y, 1 - c)) for j, chip in enumerate(chips)]
        for j, chip in enumerate(chips):
            copy(j, chip, c, (x, y, c)).wait_recv()
            passed[j].start()
        for j, chip in enumerate(chips):
            copy(3 + j, chip, 1 - c, (x, y, c)).wait_recv()
        for cp in first + passed:
            cp.wait_send()

    return pl.pallas_call(
        body, name="gather_weights", in_specs=[ANY], out_specs=ANY,
        out_shape=jax.ShapeDtypeStruct((N_CHIPS,) + slab.shape, slab.dtype),
        scratch_shapes=[pltpu.SemaphoreType.DMA((6,)), pltpu.SemaphoreType.DMA((6,))],
    )(slab)


def _exchange_halves(g):
    def body(g_ref, out_ref, send_sem, recv_sem):
        x, y, c = _position()
        cp = pltpu.make_async_remote_copy(src_ref=g_ref.at[1 - c], dst_ref=out_ref, send_sem=send_sem, recv_sem=recv_sem,
                                          device_id=(x, y, 1 - c), device_id_type=MESH)
        cp.start()
        cp.wait()

    return pl.pallas_call(
        body, name="exchange_halves", in_specs=[ANY], out_specs=ANY,
        out_shape=jax.ShapeDtypeStruct(g.shape[1:], g.dtype),
        scratch_shapes=[pltpu.SemaphoreType.DMA, pltpu.SemaphoreType.DMA],
    )(g)


def _add_my_half(g, r1, c_idx):
    tr = 256
    nt = g.shape[2] // tr

    def body(c_ref, g_ref, r_ref, o_ref, ob_ref):
        tot = g_ref[...] + r_ref[...]
        o_ref[...] = tot
        ob_ref[...] = tot.astype(BF16)

    blk = (None, tr, PACK_COLS)
    spec = pl.BlockSpec(blk, lambda s, i, c_ref: (s, i, 0))
    return pl.pallas_call(
        body, name="add_my_half",
        grid_spec=pltpu.PrefetchScalarGridSpec(
            num_scalar_prefetch=1, grid=(N_CHIPS, nt),
            in_specs=[pl.BlockSpec((None,) + blk, lambda s, i, c_ref: (c_ref[0], s, i, 0)), spec],
            out_specs=[spec, spec]),
        out_shape=[jax.ShapeDtypeStruct(g.shape[1:], F32), jax.ShapeDtypeStruct(g.shape[1:], BF16)],
        compiler_params=_cparams(("parallel", "parallel")),
    )(c_idx, g, r1)


def _scatter_to_chips(part):
    def body(p_ref, out_ref, send_sems, recv_sems):
        x, y, c = _position()
        chips = _other_chips(x, y)
        sends = [pltpu.make_async_remote_copy(
            src_ref=p_ref.at[2 * chip[0] + chip[1]], dst_ref=out_ref.at[j], send_sem=send_sems.at[j], recv_sem=recv_sems.at[j],
            device_id=(*chip, c), device_id_type=MESH) for j, chip in enumerate(chips)]
        for cp in sends:
            cp.start()
        for cp in sends:
            cp.wait()

    return pl.pallas_call(
        body, name="scatter_to_chips", in_specs=[ANY], out_specs=ANY,
        out_shape=jax.ShapeDtypeStruct((N_CHIPS - 1,) + part.shape[1:], part.dtype),
        scratch_shapes=[pltpu.SemaphoreType.DMA((3,)), pltpu.SemaphoreType.DMA((3,))],
    )(part)


def _sum_chips(part, r2, chip_idx):
    tr = 256

    def body(c_ref, p_ref, r_ref, o_ref):
        o_ref[...] = ((p_ref[...] + r_ref[0]) + r_ref[1]) + r_ref[2]

    return pl.pallas_call(
        body, name="sum_chips",
        grid_spec=pltpu.PrefetchScalarGridSpec(
            num_scalar_prefetch=1, grid=(r2.shape[1] // tr,),
            in_specs=[pl.BlockSpec((None, tr, PACK_COLS), lambda i, c_ref: (c_ref[0], i, 0)),
                      pl.BlockSpec((N_CHIPS - 1, tr, PACK_COLS), lambda i, c_ref: (0, i, 0))],
            out_specs=pl.BlockSpec((tr, PACK_COLS), lambda i, c_ref: (i, 0))),
        out_shape=jax.ShapeDtypeStruct(r2.shape[1:], F32),
        compiler_params=_cparams(("parallel",)),
    )(chip_idx, part, r2)


def _join_halves(red):
    def body(r_ref, out_ref, send_sem, recv_sem):
        x, y, c = _position()
        cp = pltpu.make_async_remote_copy(src_ref=r_ref, dst_ref=out_ref, send_sem=send_sem, recv_sem=recv_sem,
                                          device_id=(x, y, 1 - c), device_id_type=MESH)
        cp.start()
        cp.wait()

    return pl.pallas_call(
        body, name="join_halves", in_specs=[ANY], out_specs=ANY,
        out_shape=jax.ShapeDtypeStruct(red.shape, red.dtype),
        scratch_shapes=[pltpu.SemaphoreType.DMA, pltpu.SemaphoreType.DMA],
    )(red)


def _adamw_math(w, g, m, v):
    m = ADAM_B1 * m + (1.0 - ADAM_B1) * g
    v = ADAM_B2 * v + (1.0 - ADAM_B2) * (g * g)
    m_hat = m / (1.0 - ADAM_B1 ** ADAM_STEP)
    v_hat = v / (1.0 - ADAM_B2 ** ADAM_STEP)
    delta = -ADAM_LR * (m_hat / (jnp.sqrt(v_hat) + ADAM_EPS) + ADAM_WD * w)
    return delta, m, v


def _adamw(w, g, m, v):
    tr = 256

    def body(w_ref, g_ref, m_ref, v_ref, d_ref, nm_ref, nv_ref):
        d_ref[...], nm_ref[...], nv_ref[...] = _adamw_math(w_ref[...], g_ref[...], m_ref[...], v_ref[...])

    blk = pl.BlockSpec((tr, PACK_COLS), lambda i: (i, 0))
    return pl.pallas_call(
        body, name="adamw", grid=(w.shape[0] // tr,), in_specs=[blk] * 4, out_specs=[blk] * 3,
        out_shape=[jax.ShapeDtypeStruct(w.shape, F32)] * 3, compiler_params=_cparams(("parallel",)),
    )(w, g, m, v)


def _small_step(gpk, wpk, mpk, vpk):
    row, col, width = SMALL["ret_decay_logit"]

    def body(g_ref, w_ref, m_ref, v_ref, og_ref, od_ref, om_ref, ov_ref, gbuf, send_sems, recv_sems):
        x, y, c = _position()
        me = 4 * x + 2 * y + c
        gbuf[me] = g_ref[...]
        sends = []
        for k in range(1, 8):
            to = (x ^ (k >> 2), y ^ ((k >> 1) & 1), c ^ (k & 1))
            cp = pltpu.make_async_remote_copy(src_ref=g_ref, dst_ref=gbuf.at[me], send_sem=send_sems.at[k - 1],
                                              recv_sem=recv_sems.at[k - 1], device_id=to, device_id_type=MESH)
            cp.start()
            sends.append(cp)
        for k in range(1, 8):
            frm = me ^ k
            pltpu.make_async_remote_copy(src_ref=g_ref, dst_ref=gbuf.at[frm], send_sem=send_sems.at[k - 1],
                                         recv_sem=recv_sems.at[k - 1], device_id=(x, y, c), device_id_type=MESH).wait_recv()
        for cp in sends:
            cp.wait_send()
        tot = gbuf[0]
        for d in range(1, 8):
            tot = tot + gbuf[d]
        w = w_ref[...]
        r_i = lax.broadcasted_iota(jnp.int32, w.shape, 0)
        c_i = lax.broadcasted_iota(jnp.int32, w.shape, 1)
        is_logit = (r_i == row) & (c_i >= col) & (c_i < col + width)
        g = jnp.where(is_logit, tot * _sigmoid(-w), tot)
        og_ref[...] = g
        od_ref[...], om_ref[...], ov_ref[...] = _adamw_math(w, g, m_ref[...], v_ref[...])

    vm = pl.BlockSpec(memory_space=pltpu.VMEM)
    shp = jax.ShapeDtypeStruct(gpk.shape, F32)
    return pl.pallas_call(
        body, name="small_step", in_specs=[vm] * 4, out_specs=[vm] * 4, out_shape=[shp] * 4,
        scratch_shapes=[pltpu.VMEM((8,) + gpk.shape, F32), pltpu.SemaphoreType.DMA((7,)), pltpu.SemaphoreType.DMA((7,))],
    )(gpk, wpk, mpk, vpk)


def _pack_small(parts):
    rows = [[] for _ in range(SMALL_ROWS)]
    for n, (r, col, width) in sorted(SMALL.items(), key=lambda kv: (kv[1][0], kv[1][1])):
        rows[r].append((col, parts[n].reshape(-1).astype(F32)))
    out = []
    for r in range(SMALL_ROWS):
        segs, pos = [], 0
        for col, vec in rows[r]:
            assert col == pos
            segs.append(vec)
            pos += vec.shape[0]
        if pos < PACK_COLS:
            segs.append(jnp.zeros((PACK_COLS - pos,), F32))
        out.append(jnp.concatenate(segs))
    return jnp.stack(out)


def _unpack_small(pk, shapes):
    return {n: pk[r, col:col + width].reshape(shapes[n]) for n, (r, col, width) in SMALL.items()}


WEIGHTS = ("mix_norm", "w_in", "attn_q_norm", "attn_k_norm", "ret_decay_logit", "ret_norm_gain", "w_attn_o", "w_ret_o", "w_out",
           "mlp_norm", "w_up", "w_down", "ple_norm", "w_ple_gate", "w_ple", "final_norm")


def kernel(x, p, mix_norm, w_in, attn_q_norm, attn_k_norm, ret_decay_logit, ret_norm_gain, w_attn_o, w_ret_o, w_out, mlp_norm, w_up, w_down, ple_norm, w_ple_gate, w_ple, final_norm, loss_target, m_mix_norm, m_w_in, m_attn_q_norm, m_attn_k_norm, m_ret_decay_logit, m_ret_norm_gain, m_w_attn_o, m_w_ret_o, m_w_out, m_mlp_norm, m_w_up, m_w_down, m_ple_norm, m_w_ple_gate, m_w_ple, m_final_norm, v_mix_norm, v_w_in, v_attn_q_norm, v_attn_k_norm, v_ret_decay_logit, v_ret_norm_gain, v_w_attn_o, v_w_ret_o, v_w_out, v_mlp_norm, v_w_up, v_w_down, v_ple_norm, v_w_ple_gate, v_w_ple, v_final_norm):
    args = dict(locals())
    wts = {n: args[n] for n in WEIGHTS}
    ms = {n: args["m_" + n] for n in WEIGHTS}
    vs = {n: args["v_" + n] for n in WEIGHTS}
    shapes = {n: wts[n].shape for n in WEIGHTS}
    big_names = [n for n, _ in BIG]
    xi, yi, ci = _position()
    c_idx = ci.astype(jnp.int32).reshape(1)

    slab_w = _pack_shard({n: wts[n][0] for n in big_names})
    chip_idx = (2 * xi + yi).astype(jnp.int32)
    slab_b = slab_w.astype(BF16)
    gathered = lax.dynamic_update_slice(_gather_weights(slab_b), slab_b[None], (chip_idx, 0, 0))
    full = {}
    for n, axis in BIG:
        per_chip = [_unpack_shard(gathered[k], {m_: shapes[m_][1:] for m_ in big_names})[n] for k in range(N_CHIPS)]
        full[n] = jnp.concatenate(per_chip, axis=axis)
    small = {n: wts[n].reshape(wts[n].shape[1:] if wts[n].ndim > 1 else wts[n].shape) for n in SMALL}

    loss_part, grad_x, gw, gs = _local_step(x[0], p[0, 0], loss_target[0], full, small)
    loss = lax.psum(loss_part, ("x", "y", "c"))

    slabs = jnp.stack([_pack_shard({n: _shard_of(gw[n], axis, k) for n, axis in BIG}) for k in range(N_CHIPS)])
    halves = slabs.reshape(N_CHIPS, 2, HALF_ROWS, PACK_COLS).transpose(1, 0, 2, 3)
    chip_part, chip_part_b = _add_my_half(halves, _exchange_halves(halves), c_idx)
    mine = _sum_chips(chip_part, _scatter_to_chips(chip_part_b), chip_idx.reshape(1))
    both = jnp.stack([mine, _join_halves(mine)])
    reduced = jnp.where(ci == 0, both, both[::-1]).reshape(2 * HALF_ROWS, PACK_COLS)
    delta_b, newm_b, newv_b = _adamw(slab_w, reduced, _pack_shard({n: ms[n][0] for n in big_names}), _pack_shard({n: vs[n][0] for n in big_names}))
    shard_shapes = {n: shapes[n] for n in big_names}
    big_out = [_unpack_shard(a, shard_shapes) for a in (reduced, delta_b, newm_b, newv_b)]

    sm_out = _small_step(_pack_small(gs), _pack_small({n: wts[n] for n in SMALL}), _pack_small({n: ms[n] for n in SMALL}),
                         _pack_small({n: vs[n] for n in SMALL}))
    small_out = [_unpack_small(a, {n: shapes[n] for n in SMALL}) for a in sm_out]

    outs = [loss, grad_x[None]]
    for kind in range(4):
        for n in WEIGHTS:
            outs.append(small_out[kind][n] if n in SMALL else big_out[kind][n])
    return tuple(outs)
```

```python
import functools
import math

import jax
import jax.numpy as jnp
from jax import lax
from jax.experimental import pallas as pl
from jax.experimental.pallas import tpu as pltpu

F32 = jnp.float32
BF16 = jnp.bfloat16
MESH = pl.DeviceIdType.MESH

D_MODEL = 1024
PLE_DIM = 256
GRID_W = 64
ATTN_HEAD_DIM = 64
ATTN_HEADS = 8
ATTN_KV_HEADS = 2
ATTN_GROUP = ATTN_HEADS // ATTN_KV_HEADS
RET_HEAD_DIM = 128
RET_HEADS = 4
ATTN_Q_W = 512
ATTN_KV_W = 128
RET_W = 512
IN_W = 4864
D_FF = 4096
RET_CHUNK = 128
ROPE_THETA = 10000.0
NORM_EPS = 1e-6
GN_EPS = 1e-5
ATTN_SCALE = ATTN_HEAD_DIM ** -0.5
LOG2E = math.log2(math.e)
Q_FOLD = ATTN_SCALE * LOG2E
RET_SCALE = RET_HEAD_DIM ** -0.5

C_AQ, C_AK, C_AV, C_RQ, C_RK, C_RV, C_RG, C_GA, C_GR = 0, 512, 640, 768, 1280, 1792, 2304, 2816, 3840

ADAM_LR = 0.001
ADAM_B1 = 0.9
ADAM_B2 = 0.999
ADAM_EPS = 1e-08
ADAM_WD = 0.01
ADAM_STEP = 10

LANES = 128
VMEM_LIMIT = 56 << 20
SEQ_TILE = 512

BIG = (("w_in", 1), ("w_attn_o", 1), ("w_ret_o", 1), ("w_out", 0), ("w_up", 1), ("w_down", 0), ("w_ple_gate", 0), ("w_ple", 1))
SMALL_ROWS = 8
SMALL = {"mix_norm": (0, 0, 1024), "mlp_norm": (1, 0, 1024), "ple_norm": (2, 0, 1024), "final_norm": (3, 0, 1024),
         "ret_norm_gain": (4, 0, 512), "attn_q_norm": (4, 512, 64), "attn_k_norm": (4, 576, 64), "ret_decay_logit": (4, 640, 8)}


def _seq_tile(s):
    return min(SEQ_TILE, s // 2)


def _cparams(sem=None, vmem=VMEM_LIMIT):
    return pltpu.CompilerParams(dimension_semantics=sem, vmem_limit_bytes=vmem)


def _mm(name, a, b, *, ta=False, tb=False, tm, tn, tk, out_dtypes=(F32,), epi=None, epi_ins=(), consts=(), n_sums=0):
    if ta:
        kdim, m = a.shape
    else:
        m, kdim = a.shape
    n = b.shape[0] if tb else b.shape[1]
    tm, tn, tk = min(tm, m), min(tn, n), min(tk, kdim)
    assert m % tm == 0 and n % tn == 0 and kdim % tk == 0, (name, m, n, kdim, tm, tn, tk)
    nk = kdim // tk
    n_e, n_c, n_o = len(epi_ins), len(consts), len(out_dtypes)
    assert n_sums == 0 or tn == n

    def body(*refs):
        a_ref, b_ref = refs[0], refs[1]
        e_refs = refs[2:2 + n_e]
        c_refs = refs[2 + n_e:2 + n_e + n_c]
        o_refs = refs[2 + n_e + n_c:2 + n_e + n_c + n_o]
        s_refs = refs[2 + n_e + n_c + n_o:2 + n_e + n_c + n_o + n_sums]
        acc_ref = refs[2 + n_e + n_c + n_o + n_sums] if nk > 1 else None
        k = pl.program_id(2)
        if n_sums:
            @pl.when((pl.program_id(0) == 0) & (k == 0))
            def _():
                for r in s_refs:
                    r[...] = jnp.zeros(r.shape, F32)
        av = a_ref[...].astype(BF16)
        bv = b_ref[...].astype(BF16)
        dims = (((0,) if ta else (1,), (1,) if tb else (0,)), ((), ()))
        part = lax.dot_general(av, bv, dims, preferred_element_type=F32)

        def finish(acc):
            vals = epi(acc, e_refs, c_refs) if epi is not None else (acc,)
            for o_ref, v in zip(o_refs, vals[:n_o]):
                o_ref[...] = v.astype(o_ref.dtype)
            for s_ref, v in zip(s_refs, vals[n_o:]):
                _acc_add(s_ref, v)

        if nk == 1:
            finish(part)
        else:
            @pl.when(k == 0)
            def _():
                acc_ref[...] = part

            @pl.when(k > 0)
            def _():
                acc_ref[...] += part

            @pl.when(k == nk - 1)
            def _():
                finish(acc_ref[...])

    a_spec = pl.BlockSpec((tk, tm), lambda i, j, k: (k, i)) if ta else pl.BlockSpec((tm, tk), lambda i, j, k: (i, k))
    b_spec = pl.BlockSpec((tn, tk), lambda i, j, k: (j, k)) if tb else pl.BlockSpec((tk, tn), lambda i, j, k: (k, j))
    o_spec = pl.BlockSpec((tm, tn), lambda i, j, k: (i, j))
    c_specs = [pl.BlockSpec(c.shape, lambda i, j, k, nd=c.ndim: (0,) * nd) for c in consts]
    outs = pl.pallas_call(
        body, name=name,
        grid=(m // tm, n // tn, nk),
        in_specs=[a_spec, b_spec] + [o_spec] * n_e + c_specs,
        out_specs=[o_spec] * n_o + [pl.BlockSpec((8, n), lambda i, j, k: (0, 0))] * n_sums,
        out_shape=[jax.ShapeDtypeStruct((m, n), dt) for dt in out_dtypes] + [jax.ShapeDtypeStruct((8, n), F32)] * n_sums,
        scratch_shapes=[pltpu.VMEM((tm, tn), F32)] if nk > 1 else [],
        compiler_params=_cparams(("arbitrary",) * 3 if n_sums else ("parallel", "parallel", "arbitrary")),
    )(a, b, *epi_ins, *consts)
    return outs[0] if n_o + n_sums == 1 else outs


def _rows(arr, tr):
    return (arr, pl.BlockSpec((tr, arr.shape[1]), lambda i: (i, 0)))


def _win(arr, tr, start, width):
    bw = math.gcd(start, width) if start else width
    assert bw % LANES == 0
    return [(arr, pl.BlockSpec((tr, bw), lambda i, cb=start // bw + p: (i, cb))) for p in range(width // bw)]


def _ct(arr):
    return (arr, pl.BlockSpec((None,) + arr.shape[1:], lambda i: (i, 0, 0)))


def _whole(arr):
    return (arr, pl.BlockSpec(arr.shape, lambda i, nd=arr.ndim: (0,) * nd))


def _cat(refs):
    vals = [r[...] for r in refs]
    return vals[0] if len(vals) == 1 else jnp.concatenate(vals, axis=1)


def _seqtiled(name, fn, n_tiles, ins, outs, acc_widths=()):
    n_i, n_o, n_a = len(ins), len(outs), len(acc_widths)

    def body(*refs):
        i_refs, o_refs, a_refs = refs[:n_i], refs[n_i:n_i + n_o], refs[n_i + n_o:]
        if n_a:
            @pl.when(pl.program_id(0) == 0)
            def _():
                for r in a_refs:
                    r[...] = jnp.zeros(r.shape, F32)
        fn(list(i_refs), list(o_refs), list(a_refs))

    res = pl.pallas_call(
        body, name=name, grid=(n_tiles,),
        in_specs=[s for _, s in ins],
        out_specs=[s for _, _, s in outs] + [pl.BlockSpec((8, w), lambda i: (0, 0)) for w in acc_widths],
        out_shape=[jax.ShapeDtypeStruct(sh, dt) for sh, dt, _ in outs] + [jax.ShapeDtypeStruct((8, w), F32) for w in acc_widths],
        compiler_params=_cparams(("arbitrary",)),
    )(*[a for a, _ in ins])
    return res


def _acc_add(acc_ref, val):
    acc_ref[0:1, :] += jnp.sum(val, axis=0, keepdims=True)


def _out_rows(s, w, dt, tr):
    return ((s, w), dt, pl.BlockSpec((tr, w), lambda i: (i, 0)))


def _out_ct(s, w, dt, t):
    return ((s // t, w, t), dt, pl.BlockSpec((None, w, t), lambda i: (i, 0, 0)))


def _rms_fwd(x, gain):
    r = lax.rsqrt(jnp.mean(x * x, axis=-1, keepdims=True) + NORM_EPS)
    return x * r * gain


def _rms_bwd(dy, x, gain):
    r = lax.rsqrt(jnp.mean(x * x, axis=-1, keepdims=True) + NORM_EPS)
    xn = x * r
    dyg = dy * gain
    dx = r * (dyg - xn * jnp.mean(dyg * xn, axis=-1, keepdims=True))
    return dx, dy * xn


def _seg_mean(y, hd):
    w = y.shape[1]
    pieces = []
    for s in range(0, w, LANES):
        v = y[:, s:s + LANES]
        tot = jnp.sum(v, axis=1, keepdims=True)
        if hd == LANES:
            pieces.append(jnp.broadcast_to(tot, v.shape))
        else:
            low = lax.broadcasted_iota(jnp.int32, v.shape, 1) < hd
            lo = jnp.sum(jnp.where(low, v, 0.0), axis=1, keepdims=True)
            pieces.append(jnp.where(low, lo, tot - lo))
    out = pieces[0] if len(pieces) == 1 else jnp.concatenate(pieces, axis=1)
    return out * (1.0 / hd)


def _tile_lanes(t, w):
    return t if w == t.shape[1] else jnp.concatenate([t] * (w // t.shape[1]), axis=1)


def _swap_halves(x, hd):
    w = x.shape[1]
    half = hd // 2
    lane = lax.broadcasted_iota(jnp.int32, x.shape, 1)
    return jnp.where((lane % hd) < half, pltpu.roll(x, w - half, 1), pltpu.roll(x, half, 1))


def _rope(x, cos, sin_signed, hd):
    w = x.shape[1]
    return x * _tile_lanes(cos, w) + _swap_halves(x, hd) * _tile_lanes(sin_signed, w)


def _rope_t(dy, cos, sin_signed, hd):
    w = dy.shape[1]
    return dy * _tile_lanes(cos, w) + _swap_halves(dy * _tile_lanes(sin_signed, w), hd)


def _headnorm_fwd(x, gain_w, hd):
    r = lax.rsqrt(_seg_mean(x * x, hd) + NORM_EPS)
    return x * r * gain_w


def _headnorm_bwd(dy, x, gain_w, hd):
    r = lax.rsqrt(_seg_mean(x * x, hd) + NORM_EPS)
    xn = x * r
    dyg = dy * gain_w
    return r * (dyg - xn * _seg_mean(dyg * xn, hd)), dy * xn


def _sigmoid(x):
    return 1.0 / (1.0 + jnp.exp(-x))


def _rope_tables(seq_len, head_dim):
    rows = seq_len // GRID_W
    n_axis = head_dim // 4
    freqs = ROPE_THETA ** (-jnp.arange(n_axis, dtype=F32) / n_axis)
    ang_r = jnp.arange(rows, dtype=F32)[:, None] * freqs
    ang_c = jnp.arange(GRID_W, dtype=F32)[:, None] * freqs

    def expand(by_row, by_col):
        r = jnp.broadcast_to(by_row[:, None, :], (rows, GRID_W, n_axis))
        c = jnp.broadcast_to(by_col[None, :, :], (rows, GRID_W, n_axis))
        return jnp.concatenate([r, c], axis=-1).reshape(seq_len, 2 * n_axis)

    cos, sin = expand(jnp.cos(ang_r), jnp.cos(ang_c)), expand(jnp.sin(ang_r), jnp.sin(ang_c))
    reps = LANES // head_dim
    return jnp.tile(jnp.concatenate([cos, cos], axis=-1), (1, reps)), jnp.tile(jnp.concatenate([-sin, sin], axis=-1), (1, reps))


def _stage_norm_in(x, gain):
    s = x.shape[0]
    tr = min(SEQ_TILE, s)

    def fn(i, o, a):
        o[0][...] = _rms_fwd(i[0][...], i[1][...]).astype(BF16)

    return _seqtiled("norm_in", fn, s // tr, [_rows(x, tr), _whole(gain)], [_out_rows(s, D_MODEL, BF16, tr)])[0]


def _stage_qkv(proj, tabs, gq_w, gk_w):
    s = proj.shape[0]
    t = _seq_tile(s)
    ca, sa, cr, sr = tabs
    ins = (_win(proj, t, C_AQ, ATTN_Q_W) + _win(proj, t, C_AK, ATTN_KV_W) + _win(proj, t, C_AV, ATTN_KV_W)
           + _win(proj, t, C_RQ, RET_W) + _win(proj, t, C_RK, RET_W)
           + [_rows(ca, t), _rows(sa, t), _rows(cr, t), _rows(sr, t), _whole(gq_w), _whole(gk_w)])

    def fn(i, o, a):
        aq, ak, av = i[0][...], i[1][...], i[2][...]
        rq, rk = _cat(i[3:5]), _cat(i[5:7])
        ca_, sa_, cr_, sr_ = i[7][...], i[8][...], i[9][...], i[10][...]
        qr = _rope(_headnorm_fwd(aq, i[11][...], ATTN_HEAD_DIM), ca_, sa_, ATTN_HEAD_DIM) * Q_FOLD
        kr = _rope(_headnorm_fwd(ak, i[12][...], ATTN_HEAD_DIM), ca_, sa_, ATTN_HEAD_DIM)
        qt = qr.T.astype(BF16)
        zeros = jnp.zeros((ATTN_HEAD_DIM, t), BF16)
        for h in range(ATTN_HEADS):
            g = h // ATTN_GROUP
            blk = qt[h * ATTN_HEAD_DIM:(h + 1) * ATTN_HEAD_DIM, :]
            o[0][h * LANES + g * ATTN_HEAD_DIM:h * LANES + (g + 1) * ATTN_HEAD_DIM, :] = blk
            o[0][h * LANES + (1 - g) * ATTN_HEAD_DIM:h * LANES + (2 - g) * ATTN_HEAD_DIM, :] = zeros
        o[1][...] = kr.astype(BF16)
        o[2][...] = kr.T.astype(BF16)
        o[3][...] = av.astype(BF16)
        o[4][...] = av.T.astype(BF16)
        o[5][...] = _rope(rq, cr_, sr_, RET_HEAD_DIM) * RET_SCALE
        o[6][...] = _rope(rk, cr_, sr_, RET_HEAD_DIM)

    outs = [_out_ct(s, ATTN_HEADS * LANES, BF16, t), _out_rows(s, ATTN_KV_W, BF16, t), _out_ct(s, ATTN_KV_W, BF16, t),
            _out_rows(s, ATTN_KV_W, BF16, t), _out_ct(s, ATTN_KV_W, BF16, t), _out_rows(s, RET_W, F32, t), _out_rows(s, RET_W, F32, t)]
    return _seqtiled("qkv_prep", fn, s // t, ins, outs)


def _groupnorm_gate(ry, rg, gain):
    mu = _seg_mean(ry, RET_HEAD_DIM)
    d = ry - mu
    rs = lax.rsqrt(_seg_mean(d * d, RET_HEAD_DIM) + GN_EPS)
    return d * rs, rs, _sigmoid(rg)


def _stage_mix_post(ry_f, ry_b, proj, o_ct, gain):
    s = proj.shape[0]
    t = _seq_tile(s)
    ins = [_rows(ry_f, t), _rows(ry_b, t)] + _win(proj, t, C_RG, RET_W) + [_ct(o_ct), _whole(gain)]

    def fn(i, o, a):
        ry = i[0][...] + i[1][...]
        rg = _cat(i[2:4])
        gn, _, sg = _groupnorm_gate(ry, rg, None)
        o[0][...] = (gn * i[5][...] * (rg * sg)).astype(BF16)
        o[1][...] = i[4][...].astype(F32).T.astype(BF16)

    return _seqtiled("mix_post", fn, s // t, ins, [_out_rows(s, RET_W, BF16, t), _out_rows(s, ATTN_Q_W, BF16, t)])


def _stage_merge(proj, a_out, r_out):
    s = proj.shape[0]
    tr = min(SEQ_TILE, s)
    ins = _win(proj, tr, C_GA, D_MODEL) + _win(proj, tr, C_GR, D_MODEL) + [_rows(a_out, tr), _rows(r_out, tr)]
    na = len(_win(proj, tr, C_GA, D_MODEL))

    def fn(i, o, a):
        ga, gr = _cat(i[:na]), _cat(i[na:2 * na])
        o[0][...] = (_sigmoid(ga) * i[2 * na][...] + _sigmoid(gr) * i[2 * na + 1][...]).astype(BF16)

    return _seqtiled("merge", fn, s // tr, ins, [_out_rows(s, D_MODEL, BF16, tr)])[0]


def _stage_head(zg, pe, x2, target, g_final):
    s = x2.shape[0]
    tr = min(SEQ_TILE // 2, s)
    ins = [_rows(zg, tr), _rows(pe, tr), _rows(x2, tr), _rows(target, tr), _whole(g_final)]

    def fn(i, o, a):
        gt = _sigmoid(i[0][...])
        pe_ = i[1][...]
        x3 = i[2][...] + gt * pe_
        gf = i[4][...]
        r3 = lax.rsqrt(jnp.mean(x3 * x3, axis=-1, keepdims=True) + NORM_EPS)
        x3n = x3 * r3
        e = x3n * gf - i[3][...]
        _acc_add(a[0], e * e)
        dy = e * (1.0 / D_MODEL)
        _acc_add(a[1], dy * x3n)
        dyg = dy * gf
        dx3 = r3 * (dyg - x3n * jnp.mean(dyg * x3n, axis=-1, keepdims=True))
        o[0][...] = dx3
        o[1][...] = (dx3 * pe_ * gt * (1.0 - gt)).astype(BF16)
        o[2][...] = (dx3 * gt).astype(BF16)

    outs = [_out_rows(s, D_MODEL, F32, tr), _out_rows(s, D_MODEL, BF16, tr), _out_rows(s, D_MODEL, BF16, tr)]
    return _seqtiled("head", fn, s // tr, ins, outs, acc_widths=(D_MODEL, D_MODEL))


def _stage_merge_bwd(proj, dmerged, a_out, r_out):
    s = proj.shape[0]
    tr = min(SEQ_TILE // 2, s)
    wins = _win(proj, tr, C_GA, D_MODEL)
    na = len(wins)
    ins = wins + _win(proj, tr, C_GR, D_MODEL) + [_rows(dmerged, tr), _rows(a_out, tr), _rows(r_out, tr)]

    def fn(i, o, a):
        sa, sr = _sigmoid(_cat(i[:na])), _sigmoid(_cat(i[na:2 * na]))
        dm = i[2 * na][...]
        o[0][...] = (dm * sa).astype(BF16)
        o[1][...] = (dm * sr).astype(BF16)
        o[2][...] = (dm * i[2 * na + 1][...] * sa * (1.0 - sa)).astype(BF16)
        o[3][...] = (dm * i[2 * na + 2][...] * sr * (1.0 - sr)).astype(BF16)

    return _seqtiled("merge_bwd", fn, s // tr, ins, [_out_rows(s, D_MODEL, BF16, tr)] * 4)


def _stage_mix_post_bwd(dattn, attn_rows, drz, ry_f, ry_b, proj, gain):
    s = proj.shape[0]
    t = _seq_tile(s)
    ins = ([_rows(dattn, t), _rows(attn_rows, t), _rows(drz, t), _rows(ry_f, t), _rows(ry_b, t)]
           + _win(proj, t, C_RG, RET_W) + [_whole(gain)])

    def fn(i, o, a):
        da = i[0][...]
        dat = da.T
        prod_t = (da * i[1][...].astype(F32)).T
        dat_b = dat.astype(BF16)
        zeros = jnp.zeros((ATTN_HEAD_DIM, t), BF16)
        for h in range(ATTN_HEADS):
            g = h // ATTN_GROUP
            o[0][h * LANES + g * ATTN_HEAD_DIM:h * LANES + (g + 1) * ATTN_HEAD_DIM, :] = dat_b[h * ATTN_HEAD_DIM:(h + 1) * ATTN_HEAD_DIM, :]
            o[0][h * LANES + (1 - g) * ATTN_HEAD_DIM:h * LANES + (2 - g) * ATTN_HEAD_DIM, :] = zeros
            o[1][h] = jnp.sum(prod_t[h * ATTN_HEAD_DIM:(h + 1) * ATTN_HEAD_DIM, :], axis=0, keepdims=True)
        ry = i[3][...] + i[4][...]
        rg = _cat(i[5:7])
        gain_ = i[7][...]
        gn, rs, sg = _groupnorm_gate(ry, rg, None)
        dz = i[2][...]
        silu = rg * sg
        _acc_add(a[0], dz * gn * silu)
        dgn = dz * gain_ * silu
        o[2][...] = rs * (dgn - _seg_mean(dgn, RET_HEAD_DIM) - gn * _seg_mean(dgn * gn, RET_HEAD_DIM))
        o[3][...] = (dz * gn * gain_ * (sg * (1.0 + rg * (1.0 - sg)))).astype(BF16)

    outs = [_out_ct(s, ATTN_HEADS * LANES, BF16, t),
            ((ATTN_HEADS, s // t, 1, t), F32, pl.BlockSpec((ATTN_HEADS, None, 1, t), lambda i: (0, i, 0, 0))),
            _out_rows(s, RET_W, F32, t), _out_rows(s, RET_W, BF16, t)]
    return _seqtiled("mix_post_bwd", fn, s // t, ins, outs, acc_widths=(RET_W,))


def _stage_dproj(proj, dq_ct, dk8, dv8, rgrads, drg, dga, dgr, tabs, gq_w, gk_w):
    s = proj.shape[0]
    t = _seq_tile(s)
    ca, sa, cr, sr = tabs
    kv8 = pl.BlockSpec((ATTN_HEADS, t, ATTN_KV_W), lambda i: (0, i, 0))
    ins = (_win(proj, t, C_AQ, ATTN_Q_W) + _win(proj, t, C_AK, ATTN_KV_W) + [_ct(dq_ct), (dk8, kv8), (dv8, kv8)]
           + [_rows(g, t) for g in rgrads] + [_rows(drg, t), _rows(dga, t), _rows(dgr, t)]
           + [_rows(ca, t), _rows(sa, t), _rows(cr, t), _rows(sr, t), _whole(gq_w), _whole(gk_w)])

    def fn(i, o, a):
        aq, ak = i[0][...], i[1][...]
        dq_f, dk_f, dv_f, dq_b, dk_b, dv_b = (r[...] for r in i[5:11])
        ca_, sa_, cr_, sr_ = i[14][...], i[15][...], i[16][...], i[17][...]
        dqn = _rope_t(i[2][...].T * ATTN_SCALE, ca_, sa_, ATTN_HEAD_DIM)
        daq, gq_rows = _headnorm_bwd(dqn, aq, i[18][...], ATTN_HEAD_DIM)
        dkn = _rope_t(jnp.sum(i[3][...].astype(F32), axis=0) * (1.0 / LOG2E), ca_, sa_, ATTN_HEAD_DIM)
        dak, gk_rows = _headnorm_bwd(dkn, ak, i[19][...], ATTN_HEAD_DIM)
        _acc_add(a[0], gq_rows)
        _acc_add(a[1], gk_rows)
        out = o[0]
        out[:, C_AQ:C_AQ + ATTN_Q_W] = daq.astype(BF16)
        out[:, C_AK:C_AK + ATTN_KV_W] = dak.astype(BF16)
        out[:, C_AV:C_AV + ATTN_KV_W] = jnp.sum(i[4][...].astype(F32), axis=0).astype(BF16)
        out[:, C_RQ:C_RQ + RET_W] = _rope_t((dq_f + dq_b) * RET_SCALE, cr_, sr_, RET_HEAD_DIM).astype(BF16)
        out[:, C_RK:C_RK + RET_W] = _rope_t(dk_f + dk_b, cr_, sr_, RET_HEAD_DIM).astype(BF16)
        out[:, C_RV:C_RV + RET_W] = (dv_f + dv_b).astype(BF16)
        out[:, C_RG:C_RG + RET_W] = i[11][...]
        out[:, C_GA:C_GA + D_MODEL] = i[12][...]
        out[:, C_GR:C_GR + D_MODEL] = i[13][...]

    return _seqtiled("dproj", fn, s // t, ins, [_out_rows(s, IN_W, BF16, t)], acc_widths=(ATTN_Q_W, ATTN_KV_W))


def _attn_fwd(q_ct, k_rows, v_ct):
    nq, _, t = q_ct.shape
    s = nq * t
    nk = nq
    assert nk % 2 == 0
    n_par = 2

    def body(q_ref, k_ref, v_ref, o_ref, lse_ref, *bufs):
        sbuf = (bufs[0:2], bufs[2:4])
        pbuf = (bufs[4:6], bufs[6:8])

        def scores(w, j, slot):
            kj = k_ref[pl.ds(pl.multiple_of(j * t, t), t), :]
            st = jnp.dot(kj, q_ref[w], preferred_element_type=F32)
            sbuf[w][slot][...] = st
            return jnp.max(st, axis=0, keepdims=True)

        def probs(w, slot, cmax, m, l):
            m_new = jnp.maximum(m, cmax)
            alpha = jnp.exp2(m - m_new)
            pt = jnp.exp2(sbuf[w][slot][...] - m_new)
            pbuf[w][slot][...] = pt.astype(BF16)
            return m_new, alpha * l + jnp.sum(pt, axis=0, keepdims=True), alpha

        def values(w, j, slot, alpha, acc):
            return alpha * acc + jnp.dot(v_ref[j], pbuf[w][slot][...], preferred_element_type=F32)

        init = []
        for w in range(n_par):
            m = jnp.full((1, t), -1e30, F32)
            l = jnp.zeros((1, t), F32)
            cmax0 = scores(w, 0, 0)
            cmax1 = scores(w, 1, 1)
            m, l, alpha0 = probs(w, 0, cmax0, m, l)
            init.append((m, l, jnp.zeros((ATTN_HEAD_DIM, t), F32), cmax1, alpha0))

        def trip(n, carry):
            c = 2 * n
            out = []
            for w in range(n_par):
                m, l, acc, cmax_b, alpha_c = carry[w]
                acc = values(w, c, 0, alpha_c, acc)
                m, l, alpha1 = probs(w, 1, cmax_b, m, l)
                cmax2 = scores(w, c + 2, 0)
                acc = values(w, c + 1, 1, alpha1, acc)
                m, l, alpha2 = probs(w, 0, cmax2, m, l)
                cmax3 = scores(w, c + 3, 1)
                out.append((m, l, acc, cmax3, alpha2))
            return tuple(out)

        res = lax.fori_loop(0, nk // 2 - 1, trip, tuple(init))
        for w in range(n_par):
            m, l, acc, cmax_b, alpha_c = res[w]
            acc = values(w, nk - 2, 0, alpha_c, acc)
            m, l, alpha1 = probs(w, 1, cmax_b, m, l)
            acc = values(w, nk - 1, 1, alpha1, acc)
            o_ref[w] = (acc / l).astype(BF16)
            lse_ref[w] = m + jnp.log2(l)

    return pl.pallas_call(
        body, name="attn_fwd", grid=(ATTN_HEADS, nq // n_par),
        in_specs=[pl.BlockSpec((n_par, LANES, t), lambda h, i: (i, h, 0)),
                  pl.BlockSpec((s, ATTN_KV_W), lambda h, i: (0, 0)),
                  pl.BlockSpec((nk, ATTN_HEAD_DIM, t), lambda h, i: (0, h // ATTN_GROUP, 0))],
        out_specs=[pl.BlockSpec((n_par, ATTN_HEAD_DIM, t), lambda h, i: (i, h, 0)),
                   pl.BlockSpec((None, n_par, 1, t), lambda h, i: (h, i, 0, 0))],
        out_shape=[jax.ShapeDtypeStruct((nq, ATTN_Q_W, t), BF16), jax.ShapeDtypeStruct((ATTN_HEADS, nq, 1, t), F32)],
        scratch_shapes=[pltpu.VMEM((t, t), F32)] * (2 * n_par) + [pltpu.VMEM((t, t), BF16)] * (2 * n_par),
        compiler_params=_cparams(("parallel", "parallel")),
    )(q_ct, k_rows, v_ct)


def _attn_bwd(q_ct, do_ct, lse, delta, k_rows, v_rows, k_ct):
    nq, _, t = q_ct.shape
    s = nq * t
    nk = nq

    assert nq % 2 == 0

    def body(q_ref, do_ref, lse_ref, delta_ref, k_ref, v_ref, kt_ref, dq_ref, dk_ref, dv_ref, dk_acc, dv_acc,
             sb0, sb1, db0, db1, pb0, pb1, gb0, gb1):
        j = pl.program_id(1)
        sb, db, pb, gb = (sb0, sb1), (db0, db1), (pb0, pb1), (gb0, gb1)

        @pl.when(j == 0)
        def _():
            dq_ref[...] = jnp.zeros(dq_ref.shape, F32)

        kj, vj, ktj = k_ref[...], v_ref[...], kt_ref[...]
        dk_acc[...] = jnp.zeros(dk_acc.shape, F32)
        dv_acc[...] = jnp.zeros(dv_acc.shape, F32)

        def products(i, slot):
            sb[slot][...] = jnp.dot(kj, q_ref[i], preferred_element_type=F32)
            db[slot][...] = jnp.dot(vj, do_ref[i], preferred_element_type=F32)

        def cotangents(i, slot):
            pt = jnp.exp2(sb[slot][...] - lse_ref[i])
            pb[slot][...] = pt.astype(BF16)
            gb[slot][...] = (pt * (db[slot][...] - delta_ref[i])).astype(BF16)

        def accumulate(i, slot):
            dst = gb[slot][...]
            dv_acc[...] += _nt(pb[slot][...], do_ref[i])
            dk_acc[...] += _nt(dst, q_ref[i])
            dq_ref[i] += jnp.dot(ktj, dst, preferred_element_type=F32)

        products(0, 0)
        products(1, 1)
        cotangents(0, 0)

        def trip(n, carry):
            c = 2 * n
            accumulate(c, 0)
            cotangents(c + 1, 1)
            products(c + 2, 0)
            accumulate(c + 1, 1)
            cotangents(c + 2, 0)
            products(c + 3, 1)
            return carry

        lax.fori_loop(0, nq // 2 - 1, trip, 0)
        accumulate(nq - 2, 0)
        cotangents(nq - 1, 1)
        accumulate(nq - 1, 1)
        dk_ref[...] = dk_acc[...].astype(dk_ref.dtype)
        dv_ref[...] = dv_acc[...].astype(dv_ref.dtype)

    per_head = pl.BlockSpec((nq, LANES, t), lambda h, j: (0, h, 0))
    stat = pl.BlockSpec((None, nq, 1, t), lambda h, j: (h, 0, 0, 0))
    kv_rows = pl.BlockSpec((t, ATTN_KV_W), lambda h, j: (j, 0))
    kv_out = pl.BlockSpec((None, t, ATTN_KV_W), lambda h, j: (h, j, 0))
    return pl.pallas_call(
        body, name="attn_bwd", grid=(ATTN_HEADS, nk),
        in_specs=[per_head, per_head, stat, stat, kv_rows, kv_rows,
                  pl.BlockSpec((None, ATTN_HEAD_DIM, t), lambda h, j: (j, h // ATTN_GROUP, 0))],
        out_specs=[pl.BlockSpec((nq, ATTN_HEAD_DIM, t), lambda h, j: (0, h, 0)), kv_out, kv_out],
        out_shape=[jax.ShapeDtypeStruct((nq, ATTN_Q_W, t), F32), jax.ShapeDtypeStruct((ATTN_HEADS, s, ATTN_KV_W), BF16),
                   jax.ShapeDtypeStruct((ATTN_HEADS, s, ATTN_KV_W), BF16)],
        scratch_shapes=([pltpu.VMEM((t, ATTN_KV_W), F32)] * 2 + [pltpu.VMEM((t, t), F32)] * 4 + [pltpu.VMEM((t, t), BF16)] * 4),
        compiler_params=_cparams(("parallel", "arbitrary")),
    )(q_ct, do_ct, lse, delta, k_rows, v_rows, k_ct)


def _log_sigmoid(x):
    t = jnp.exp(-jnp.abs(x))
    log1p_t = jnp.where(t < 1e-2, t * (1.0 - t * (0.5 - t * (1.0 / 3.0))), jnp.log(1.0 + t))
    return jnp.minimum(x, 0.0) - log1p_t


def _decay_tables(logit, backward):
    c = RET_CHUNK
    lam = _log_sigmoid(jnp.full((c, c), logit, F32))
    ii = lax.broadcasted_iota(jnp.int32, (c, c), 0).astype(F32)
    jj = lax.broadcasted_iota(jnp.int32, (c, c), 1).astype(F32)
    if not backward:
        dist, dist_t = jnp.maximum(ii - jj, 0.0), jnp.maximum(jj - ii, 0.0)
        mask, mask_t = ii >= jj, jj >= ii
        e_q, e_k = ii + 1.0, (c - 1.0) - ii
    else:
        dist, dist_t = jnp.maximum(jj - ii, 0.0), jnp.maximum(ii - jj, 0.0)
        mask, mask_t = jj > ii, ii > jj
        e_q, e_k = c - ii, ii
    return dict(
        d=jnp.where(mask, jnp.exp(lam * dist), 0.0), d_t=jnp.where(mask_t, jnp.exp(lam * dist_t), 0.0), dist=dist,
        qdec=jnp.exp(lam * e_q), kdec=jnp.exp(lam * e_k), e_q=e_q, e_k=e_k, gam=jnp.exp(lam * c))


def _nt(a, b):
    return lax.dot_general(a, b, (((1,), (1,)), ((), ())), preferred_element_type=F32)


def _ret_fwd(logits, q, k, proj):
    s = q.shape[0]
    c = RET_CHUNK
    nc = s // c
    chunk = (lambda n: n, lambda n: nc - 1 - n)
    vwin = _win(proj, c, C_RV, RET_W)
    nv = len(vwin)
    vw = RET_W // nv
    per = 2 + nv

    def body(lg_ref, *refs):
        ins, outs, states = refs[:2 * per], refs[2 * per:2 * per + 4], refs[2 * per + 4:]

        @pl.when(pl.program_id(0) == 0)
        def _():
            for st in states:
                st[...] = jnp.zeros(st.shape, F32)

        for h in range(RET_HEADS):
            for d in range(2):
                q_ref, k_ref, v_refs = ins[d * per], ins[d * per + 1], ins[d * per + 2:(d + 1) * per]
                y_ref, st_ref, state = outs[2 * d], outs[2 * d + 1], states[d]
                tb = _decay_tables(lg_ref[d, h], bool(d))
                sl = slice(h * RET_HEAD_DIM, (h + 1) * RET_HEAD_DIM)
                qh, kh = q_ref[:, sl], k_ref[:, sl]
                off = h * RET_HEAD_DIM
                vb = v_refs[off // vw][:, off % vw:off % vw + RET_HEAD_DIM].astype(BF16)
                a = _nt(qh.astype(BF16), kh.astype(BF16)) * tb["d"]
                sh = state[h]
                st_ref[h] = sh
                y_ref[:, sl] = (jnp.dot(a.astype(BF16), vb, preferred_element_type=F32)
                                + jnp.dot((qh * tb["qdec"]).astype(BF16), sh.astype(BF16), preferred_element_type=F32))
                state[h] = tb["gam"] * sh + jnp.dot((kh * tb["kdec"]).T.astype(BF16), vb, preferred_element_type=F32)

    hmat = (RET_HEADS, RET_HEAD_DIM, RET_HEAD_DIM)
    in_specs, out_specs, args = [pl.BlockSpec(memory_space=pltpu.SMEM)], [], [logits]
    for d in range(2):
        rows = pl.BlockSpec((c, RET_W), lambda n, d=d: (chunk[d](n), 0))
        in_specs += [rows, rows] + [pl.BlockSpec(sp.block_shape, lambda n, d=d, cb=sp.index_map(0)[1]: (chunk[d](n), cb)) for _, sp in vwin]
        args += [q, k] + [a for a, _ in vwin]
        out_specs += [rows, pl.BlockSpec((None,) + hmat, lambda n, d=d: (chunk[d](n), 0, 0, 0))]
    return pl.pallas_call(
        body, name="ret_fwd", grid=(nc,), in_specs=in_specs, out_specs=out_specs,
        out_shape=[jax.ShapeDtypeStruct((s, RET_W), F32), jax.ShapeDtypeStruct((nc,) + hmat, F32)] * 2,
        scratch_shapes=[pltpu.VMEM(hmat, F32)] * 2,
        compiler_params=_cparams(("arbitrary",)),
    )(*args)


def _ret_bwd(logits, q, k, proj, dy, st_f, st_b):
    s = q.shape[0]
    c = RET_CHUNK
    nc = s // c
    chunk = (lambda n: nc - 1 - n, lambda n: n)
    vwin = _win(proj, c, C_RV, RET_W)
    nv = len(vwin)
    vw = RET_W // nv
    per = 4 + nv

    def body(lg_ref, *refs):
        ins, outs, scr = refs[:2 * per], refs[2 * per:2 * per + 8], refs[2 * per + 8:]
        n = pl.program_id(0)

        @pl.when(n == 0)
        def _():
            for r in scr:
                r[...] = jnp.zeros(r.shape, F32)

        for h in range(RET_HEADS):
            for d in range(2):
                q_ref, k_ref, dy_ref, st_ref = ins[d * per:d * per + 4]
                v_refs = ins[d * per + 4:(d + 1) * per]
                dq_ref, dk_ref, dv_ref = outs[4 * d:4 * d + 3]
                dstate, lacc = scr[2 * d], scr[2 * d + 1]
                tb = _decay_tables(lg_ref[d, h], bool(d))
                sl = slice(h * RET_HEAD_DIM, (h + 1) * RET_HEAD_DIM)
                qh, kh, dyh = q_ref[:, sl], k_ref[:, sl], dy_ref[:, sl]
                off = h * RET_HEAD_DIM
                vb = v_refs[off // vw][:, off % vw:off % vw + RET_HEAD_DIM].astype(BF16)
                qb, kb, dyb = qh.astype(BF16), kh.astype(BF16), dyh.astype(BF16)
                sh, dsh = st_ref[h], dstate[h]
                shb, dshb = sh.astype(BF16), dsh.astype(BF16)
                qk = _nt(qb, kb)
                g = _nt(dyb, vb) * tb["d"]
                a_t = _nt(kb, qb) * tb["d_t"]
                g_t = _nt(vb, dyb) * tb["d_t"]
                qd, kd = qh * tb["qdec"], kh * tb["kdec"]
                dqd = _nt(dyb, shb)
                dkd = _nt(vb, dshb)
                dq_ref[:, sl] = jnp.dot(g.astype(BF16), kb, preferred_element_type=F32) + dqd * tb["qdec"]
                dk_ref[:, sl] = jnp.dot(g_t.astype(BF16), qb, preferred_element_type=F32) + dkd * tb["kdec"]
                dv_ref[:, sl] = (jnp.dot(a_t.astype(BF16), dyb, preferred_element_type=F32)
                                 + jnp.dot(kd.astype(BF16), dshb, preferred_element_type=F32))
                lacc[h] += (tb["dist"] * qk * g + tb["e_q"] * qd * dqd + tb["e_k"] * kd * dkd
                            + float(c) * tb["gam"] * dsh * sh)
                dstate[h] = tb["gam"] * dsh + jnp.dot(qd.T.astype(BF16), dyb, preferred_element_type=F32)

        @pl.when(n == nc - 1)
        def _():
            for d in range(2):
                for h in range(RET_HEADS):
                    outs[4 * d + 3][h] = jnp.zeros((8, LANES), F32) + jnp.sum(scr[2 * d + 1][h])

    hmat = (RET_HEADS, RET_HEAD_DIM, RET_HEAD_DIM)
    in_specs, out_specs, args = [pl.BlockSpec(memory_space=pltpu.SMEM)], [], [logits]
    for d, states in enumerate((st_f, st_b)):
        rows = pl.BlockSpec((c, RET_W), lambda n, d=d: (chunk[d](n), 0))
        in_specs += ([rows, rows, rows, pl.BlockSpec((None,) + hmat, lambda n, d=d: (chunk[d](n), 0, 0, 0))]
                     + [pl.BlockSpec(sp.block_shape, lambda n, d=d, cb=sp.index_map(0)[1]: (chunk[d](n), cb)) for _, sp in vwin])
        args += [q, k, dy, states] + [a for a, _ in vwin]
        out_specs += [rows, rows, rows, pl.BlockSpec((RET_HEADS, 8, LANES), lambda n: (0, 0, 0))]
    return pl.pallas_call(
        body, name="ret_bwd", grid=(nc,), in_specs=in_specs, out_specs=out_specs,
        out_shape=([jax.ShapeDtypeStruct((s, RET_W), F32)] * 3 + [jax.ShapeDtypeStruct((RET_HEADS, 8, LANES), F32)]) * 2,
        scratch_shapes=[pltpu.VMEM(hmat, F32)] * 4,
        compiler_params=_cparams(("arbitrary",)),
    )(*args)


def _local_step(x, p, target, w, small):
    s = x.shape[0]
    tabs = _rope_tables(s, ATTN_HEAD_DIM) + _rope_tables(s, RET_HEAD_DIM)
    g_mix, g_mlp, g_ple = small["mix_norm"][None, :], small["mlp_norm"][None, :], small["ple_norm"][None, :]
    g_final, g_ret = small["final_norm"][None, :], small["ret_norm_gain"][None, :]
    gq_w = jnp.tile(small["attn_q_norm"], ATTN_HEADS)[None, :]
    gk_w = jnp.tile(small["attn_k_norm"], ATTN_KV_HEADS)[None, :]
    logits = small["ret_decay_logit"]

    hb = _stage_norm_in(x, g_mix)
    proj = _mm("in_proj", hb, w["w_in"], tm=512, tn=IN_W // 2, tk=1024)
    q_ct, k_rows, k_ct, v_rows, v_ct, rq, rk = _stage_qkv(proj, tabs, gq_w, gk_w)
    o_ct, lse = _attn_fwd(q_ct, k_rows, v_ct)
    ry_f, st_f, ry_b, st_b = _ret_fwd(logits, rq, rk, proj)
    rz, attn_rows = _stage_mix_post(ry_f, ry_b, proj, o_ct, g_ret)
    a_out = _mm("attn_o", attn_rows, w["w_attn_o"], tm=1024, tn=1024, tk=512, out_dtypes=(BF16,))
    r_out = _mm("ret_o", rz, w["w_ret_o"], tm=1024, tn=1024, tk=512, out_dtypes=(BF16,))
    merged = _stage_merge(proj, a_out, r_out)

    def epi_res_norm(acc, e, c):
        xr = e[0][...] + acc
        return xr, _rms_fwd(xr, c[0][...])

    x1, hm = _mm("out_proj", merged, w["w_out"], tm=512, tn=1024, tk=1024, out_dtypes=(F32, BF16),
                 epi=epi_res_norm, epi_ins=(x,), consts=(g_mlp,))

    def epi_relu2(acc, e, c):
        r = jnp.maximum(acc, 0.0)
        return (r * r,)

    act = _mm("mlp_up", hm, w["w_up"], tm=512, tn=2048, tk=1024, out_dtypes=(BF16,), epi=epi_relu2)
    x2, hp = _mm("mlp_down", act, w["w_down"], tm=512, tn=1024, tk=2048, out_dtypes=(F32, BF16),
                 epi=epi_res_norm, epi_ins=(x1,), consts=(g_ple,))
    zg = _mm("ple_gate", hp, w["w_ple_gate"], tm=1024, tn=1024, tk=1024)
    pe = _mm("ple_emb", p, w["w_ple"], tm=1024, tn=1024, tk=256)
    dx3, dzg, dpe, loss_cols, g_final_p = _stage_head(zg, pe, x2, target, g_final)
    loss_sum = 0.5 / D_MODEL * jnp.sum(loss_cols)

    gw = {}
    gw["w_ple"] = _mm("g_w_ple", p, dpe, ta=True, tm=256, tn=1024, tk=2048)
    gw["w_ple_gate"] = _mm("g_w_ple_gate", hp, dzg, ta=True, tm=1024, tn=1024, tk=2048)
    def epi_norm_bwd(acc, e, c):
        dx, dg = _rms_bwd(acc, e[0][...], c[0][...])
        return e[1][...] + dx, dg

    dx2, g_ple_p = _mm("d_hp", dzg, w["w_ple_gate"], tb=True, tm=512, tn=1024, tk=1024, epi=epi_norm_bwd, epi_ins=(x2, dx3),
                       consts=(g_ple,), n_sums=1)

    def epi_relu2_bwd(acc, e, c):
        return (acc * (2.0 * jnp.sqrt(e[0][...].astype(F32))),)

    du = _mm("d_u", dx2, w["w_down"], tb=True, tm=512, tn=2048, tk=1024, out_dtypes=(BF16,), epi=epi_relu2_bwd, epi_ins=(act,))
    gw["w_down"] = _mm("g_w_down", act, dx2, ta=True, tm=1024, tn=1024, tk=2048)
    gw["w_up"] = _mm("g_w_up", hm, du, ta=True, tm=1024, tn=1024, tk=2048)
    dx1, g_mlp_p = _mm("d_hm", du, w["w_up"], tb=True, tm=512, tn=1024, tk=2048, epi=epi_norm_bwd, epi_ins=(x1, dx2),
                       consts=(g_mlp,), n_sums=1)
    dmerged = _mm("d_merged", dx1, w["w_out"], tb=True, tm=1024, tn=1024, tk=1024)
    gw["w_out"] = _mm("g_w_out", merged, dx1, ta=True, tm=1024, tn=1024, tk=2048)
    dao, dro, dga, dgr = _stage_merge_bwd(proj, dmerged, a_out, r_out)
    gw["w_attn_o"] = _mm("g_w_attn_o", attn_rows, dao, ta=True, tm=512, tn=1024, tk=2048)
    gw["w_ret_o"] = _mm("g_w_ret_o", rz, dro, ta=True, tm=512, tn=1024, tk=2048)
    dattn = _mm("d_attn", dao, w["w_attn_o"], tb=True, tm=1024, tn=512, tk=1024)
    drz = _mm("d_rz", dro, w["w_ret_o"], tb=True, tm=1024, tn=512, tk=1024)
    do_ct, delta, dry, drg, g_ret_p = _stage_mix_post_bwd(dattn, attn_rows, drz, ry_f, ry_b, proj, g_ret)
    dq_f, dk_f, dv_f, dl_f, dq_b, dk_b, dv_b, dl_b = _ret_bwd(logits, rq, rk, proj, dry, st_f, st_b)
    dq_ct, dk8, dv8 = _attn_bwd(q_ct, do_ct, lse, delta, k_rows, v_rows, k_ct)
    dproj, gq_p, gk_p = _stage_dproj(proj, dq_ct, dk8, dv8, (dq_f, dk_f, dv_f, dq_b, dk_b, dv_b), drg, dga, dgr, tabs, gq_w, gk_w)
    gw["w_in"] = _mm("g_w_in", hb, dproj, ta=True, tm=512, tn=IN_W // 2, tk=1024)
    grad_x, g_mix_p = _mm("d_h", dproj, w["w_in"], tb=True, tm=512, tn=1024, tk=IN_W // 2, epi=epi_norm_bwd, epi_ins=(x, dx1),
                          consts=(g_mix,), n_sums=1)

    gs = {
        "mix_norm": g_mix_p[0], "mlp_norm": g_mlp_p[0], "ple_norm": g_ple_p[0], "final_norm": g_final_p[0],
        "ret_norm_gain": g_ret_p[0],
        "attn_q_norm": jnp.sum(gq_p[0].reshape(ATTN_HEADS, ATTN_HEAD_DIM), axis=0),
        "attn_k_norm": jnp.sum(gk_p[0].reshape(ATTN_KV_HEADS, ATTN_HEAD_DIM), axis=0),
        "ret_decay_logit": jnp.stack([dl_f[:, 0, 0], dl_b[:, 0, 0]]),
    }
    return loss_sum, grad_x, gw, gs


PACK_COLS = 1024
N_CHIPS = 4
HALF_ROWS = 2048


def _pack_shard(parts):
    return jnp.concatenate([parts[n].reshape(-1, PACK_COLS) for n, _ in BIG], axis=0)


def _unpack_shard(slab, shapes):
    out, r = {}, 0
    for n, _ in BIG:
        rows = math.prod(shapes[n]) // PACK_COLS
        out[n] = slab[r:r + rows].reshape(shapes[n])
        r += rows
    return out


def _shard_of(full, axis, sidx):
    size = full.shape[axis] // N_CHIPS
    return lax.slice_in_dim(full, sidx * size, (sidx + 1) * size, axis=axis)


def _position():
    x, y, c = lax.axis_index("x"), lax.axis_index("y"), lax.axis_index("c")
    return x, y, c


def _other_chips(x, y):
    return [(1 - x, y), (x, 1 - y), (1 - x, 1 - y)]


ANY = pl.BlockSpec(memory_space=pl.ANY)


def _gather_weights(slab):
    rows = slab.shape[0]
    half = rows // 2

    def body(in_ref, out_ref, send_sems, recv_sems):
        x, y, c = _position()
        chips = _other_chips(x, y)

        def piece(chip, core):
            return out_ref.at[2 * chip[0] + chip[1], pl.ds(core * half, half), :]

        def copy(k, chip, core, to, src=None):
            return pltpu.make_async_remote_copy(
                src_ref=piece(chip, core) if src is None else src, dst_ref=piece(chip, core),
                send_sem=send_sems.at[k], recv_sem=recv_sems.at[k], device_id=to, device_id_type=MESH)

        first = [copy(j, (x, y), c, (*chip, c), src=in_ref.at[pl.ds(c * half, half), :]) for j, chip in enumerate(chips)]
        for cp in first:
            cp.start()
        passed = [copy(3 + j, chip, c, (x, y, 1 - c)) for j, chip in enumerate(chips)]
        for j, chip in enumerate(chips):
            copy(j, chip, c, (x, y, c)).wait_recv()
            passed[j].start()
        for j, chip in enumerate(chips):
            copy(3 + j, chip, 1 - c, (x, y, c)).wait_recv()
        for cp in first + passed:
            cp.wait_send()

    return pl.pallas_call(
        body, name="gather_weights", in_specs=[ANY], out_specs=ANY,
        out_shape=jax.ShapeDtypeStruct((N_CHIPS,) + slab.shape, slab.dtype),
        scratch_shapes=[pltpu.SemaphoreType.DMA((6,)), pltpu.SemaphoreType.DMA((6,))],
    )(slab)


def _exchange_halves(g):
    def body(g_ref, out_ref, send_sem, recv_sem):
        x, y, c = _position()
        cp = pltpu.make_async_remote_copy(src_ref=g_ref.at[1 - c], dst_ref=out_ref, send_sem=send_sem, recv_sem=recv_sem,
                                          device_id=(x, y, 1 - c), device_id_type=MESH)
        cp.start()
        cp.wait()

    return pl.pallas_call(
        body, name="exchange_halves", in_specs=[ANY], out_specs=ANY,
        out_shape=jax.ShapeDtypeStruct(g.shape[1:], g.dtype),
        scratch_shapes=[pltpu.SemaphoreType.DMA, pltpu.SemaphoreType.DMA],
    )(g)


def _add_my_half(g, r1, c_idx):
    tr = 256
    nt = g.shape[2] // tr

    def body(c_ref, g_ref, r_ref, o_ref, ob_ref):
        tot = g_ref[...] + r_ref[...]
        o_ref[...] = tot
        ob_ref[...] = tot.astype(BF16)

    blk = (None, tr, PACK_COLS)
    spec = pl.BlockSpec(blk, lambda s, i, c_ref: (s, i, 0))
    return pl.pallas_call(
        body, name="add_my_half",
        grid_spec=pltpu.PrefetchScalarGridSpec(
            num_scalar_prefetch=1, grid=(N_CHIPS, nt),
            in_specs=[pl.BlockSpec((None,) + blk, lambda s, i, c_ref: (c_ref[0], s, i, 0)), spec],
            out_specs=[spec, spec]),
        out_shape=[jax.ShapeDtypeStruct(g.shape[1:], F32), jax.ShapeDtypeStruct(g.shape[1:], BF16)],
        compiler_params=_cparams(("parallel", "parallel")),
    )(c_idx, g, r1)


def _scatter_to_chips(part):
    def body(p_ref, out_ref, send_sems, recv_sems):
        x, y, c = _position()
        chips = _other_chips(x, y)
        sends = [pltpu.make_async_remote_copy(
            src_ref=p_ref.at[2 * chip[0] + chip[1]], dst_ref=out_ref.at[j], send_sem=send_sems.at[j], recv_sem=recv_sems.at[j],
            device_id=(*chip, c), device_id_type=MESH) for j, chip in enumerate(chips)]
        for cp in sends:
            cp.start()
        for cp in sends:
            cp.wait()

    return pl.pallas_call(
        body, name="scatter_to_chips", in_specs=[ANY], out_specs=ANY,
        out_shape=jax.ShapeDtypeStruct((N_CHIPS - 1,) + part.shape[1:], part.dtype),
        scratch_shapes=[pltpu.SemaphoreType.DMA((3,)), pltpu.SemaphoreType.DMA((3,))],
    )(part)


def _sum_chips(part, r2, chip_idx):
    tr = 256

    def body(c_ref, p_ref, r_ref, o_ref):
        o_ref[...] = ((p_ref[...] + r_ref[0]) + r_ref[1]) + r_ref[2]

    return pl.pallas_call(
        body, name="sum_chips",
        grid_spec=pltpu.PrefetchScalarGridSpec(
            num_scalar_prefetch=1, grid=(r2.shape[1] // tr,),
            in_specs=[pl.BlockSpec((None, tr, PACK_COLS), lambda i, c_ref: (c_ref[0], i, 0)),
                      pl.BlockSpec((N_CHIPS - 1, tr, PACK_COLS), lambda i, c_ref: (0, i, 0))],
            out_specs=pl.BlockSpec((tr, PACK_COLS), lambda i, c_ref: (i, 0))),
        out_shape=jax.ShapeDtypeStruct(r2.shape[1:], F32),
        compiler_params=_cparams(("parallel",)),
    )(chip_idx, part, r2)


def _join_halves(red):
    def body(r_ref, out_ref, send_sem, recv_sem):
        x, y, c = _position()
        cp = pltpu.make_async_remote_copy(src_ref=r_ref, dst_ref=out_ref, send_sem=send_sem, recv_sem=recv_sem,
                                          device_id=(x, y, 1 - c), device_id_type=MESH)
        cp.start()
        cp.wait()

    return pl.pallas_call(
        body, name="join_halves", in_specs=[ANY], out_specs=ANY,
        out_shape=jax.ShapeDtypeStruct(red.shape, red.dtype),
        scratch_shapes=[pltpu.SemaphoreType.DMA, pltpu.SemaphoreType.DMA],
    )(red)


def _adamw_math(w, g, m, v):
    m = ADAM_B1 * m + (1.0 - ADAM_B1) * g
    v = ADAM_B2 * v + (1.0 - ADAM_B2) * (g * g)
    m_hat = m / (1.0 - ADAM_B1 ** ADAM_STEP)
    v_hat = v / (1.0 - ADAM_B2 ** ADAM_STEP)
    delta = -ADAM_LR * (m_hat / (jnp.sqrt(v_hat) + ADAM_EPS) + ADAM_WD * w)
    return delta, m, v


def _adamw(name, w, g, m, v):
    tr = min(256, w.shape[0])

    def body(w_ref, g_ref, m_ref, v_ref, d_ref, nm_ref, nv_ref):
        d_ref[...], nm_ref[...], nv_ref[...] = _adamw_math(w_ref[...], g_ref[...], m_ref[...], v_ref[...])

    blk = pl.BlockSpec((tr, w.shape[1]), lambda i: (i, 0))
    return pl.pallas_call(
        body, name="adamw_" + name, grid=(w.shape[0] // tr,), in_specs=[blk] * 4, out_specs=[blk] * 3,
        out_shape=[jax.ShapeDtypeStruct(w.shape, F32)] * 3, compiler_params=_cparams(("parallel",)),
    )(w, g, m, v)


def _small_step(gpk, wpk, mpk, vpk):
    row, col, width = SMALL["ret_decay_logit"]

    def body(g_ref, w_ref, m_ref, v_ref, og_ref, od_ref, om_ref, ov_ref, gbuf, send_sems, recv_sems):
        x, y, c = _position()
        me = 4 * x + 2 * y + c
        gbuf[me] = g_ref[...]
        sends = []
        for k in range(1, 8):
            to = (x ^ (k >> 2), y ^ ((k >> 1) & 1), c ^ (k & 1))
            cp = pltpu.make_async_remote_copy(src_ref=g_ref, dst_ref=gbuf.at[me], send_sem=send_sems.at[k - 1],
                                              recv_sem=recv_sems.at[k - 1], device_id=to, device_id_type=MESH)
            cp.start()
            sends.append(cp)
        for k in range(1, 8):
            frm = me ^ k
            pltpu.make_async_remote_copy(src_ref=g_ref, dst_ref=gbuf.at[frm], send_sem=send_sems.at[k - 1],
                                         recv_sem=recv_sems.at[k - 1], device_id=(x, y, c), device_id_type=MESH).wait_recv()
        for cp in sends:
            cp.wait_send()
        tot = gbuf[0]
        for d in range(1, 8):
            tot = tot + gbuf[d]
        w = w_ref[...]
        r_i = lax.broadcasted_iota(jnp.int32, w.shape, 0)
        c_i = lax.broadcasted_iota(jnp.int32, w.shape, 1)
        is_logit = (r_i == row) & (c_i >= col) & (c_i < col + width)
        g = jnp.where(is_logit, tot * _sigmoid(-w), tot)
        og_ref[...] = g
        od_ref[...], om_ref[...], ov_ref[...] = _adamw_math(w, g, m_ref[...], v_ref[...])

    vm = pl.BlockSpec(memory_space=pltpu.VMEM)
    shp = jax.ShapeDtypeStruct(gpk.shape, F32)
    return pl.pallas_call(
        body, name="small_step", in_specs=[vm] * 4, out_specs=[vm] * 4, out_shape=[shp] * 4,
        scratch_shapes=[pltpu.VMEM((8,) + gpk.shape, F32), pltpu.SemaphoreType.DMA((7,)), pltpu.SemaphoreType.DMA((7,))],
    )(gpk, wpk, mpk, vpk)


def _pack_small(parts):
    rows = [[] for _ in range(SMALL_ROWS)]
    for n, (r, col, width) in sorted(SMALL.items(), key=lambda kv: (kv[1][0], kv[1][1])):
        rows[r].append((col, parts[n].reshape(-1).astype(F32)))
    out = []
    for r in range(SMALL_ROWS):
        segs, pos = [], 0
        for col, vec in rows[r]:
            assert col == pos
            segs.append(vec)
            pos += vec.shape[0]
        if pos < PACK_COLS:
            segs.append(jnp.zeros((PACK_COLS - pos,), F32))
        out.append(jnp.concatenate(segs))
    return jnp.stack(out)


def _unpack_small(pk, shapes):
    return {n: pk[r, col:col + width].reshape(shapes[n]) for n, (r, col, width) in SMALL.items()}


WEIGHTS = ("mix_norm", "w_in", "attn_q_norm", "attn_k_norm", "ret_decay_logit", "ret_norm_gain", "w_attn_o", "w_ret_o", "w_out",
           "mlp_norm", "w_up", "w_down", "ple_norm", "w_ple_gate", "w_ple", "final_norm")


def kernel(x, p, mix_norm, w_in, attn_q_norm, attn_k_norm, ret_decay_logit, ret_norm_gain, w_attn_o, w_ret_o, w_out, mlp_norm, w_up, w_down, ple_norm, w_ple_gate, w_ple, final_norm, loss_target, m_mix_norm, m_w_in, m_attn_q_norm, m_attn_k_norm, m_ret_decay_logit, m_ret_norm_gain, m_w_attn_o, m_w_ret_o, m_w_out, m_mlp_norm, m_w_up, m_w_down, m_ple_norm, m_w_ple_gate, m_w_ple, m_final_norm, v_mix_norm, v_w_in, v_attn_q_norm, v_attn_k_norm, v_ret_decay_logit, v_ret_norm_gain, v_w_attn_o, v_w_ret_o, v_w_out, v_mlp_norm, v_w_up, v_w_down, v_ple_norm, v_w_ple_gate, v_w_ple, v_final_norm):
    args = dict(locals())
    wts = {n: args[n] for n in WEIGHTS}
    ms = {n: args["m_" + n] for n in WEIGHTS}
    vs = {n: args["v_" + n] for n in WEIGHTS}
    shapes = {n: wts[n].shape for n in WEIGHTS}
    big_names = [n for n, _ in BIG]
    xi, yi, ci = _position()
    c_idx = ci.astype(jnp.int32).reshape(1)

    chip_idx = (2 * xi + yi).astype(jnp.int32)
    slab_b = _pack_shard({n: wts[n][0].astype(BF16) for n in big_names})
    gathered = lax.dynamic_update_slice(_gather_weights(slab_b), slab_b[None], (chip_idx, 0, 0))
    full = {}
    for n, axis in BIG:
        per_chip = [_unpack_shard(gathered[k], {m_: shapes[m_][1:] for m_ in big_names})[n] for k in range(N_CHIPS)]
        full[n] = jnp.concatenate(per_chip, axis=axis)
    small = {n: wts[n].reshape(wts[n].shape[1:] if wts[n].ndim > 1 else wts[n].shape) for n in SMALL}

    loss_part, grad_x, gw, gs = _local_step(x[0], p[0, 0], loss_target[0], full, small)
    loss = lax.psum(loss_part, ("x", "y", "c"))

    slabs = jnp.stack([_pack_shard({n: _shard_of(gw[n], axis, k) for n, axis in BIG}) for k in range(N_CHIPS)])
    halves = slabs.reshape(N_CHIPS, 2, HALF_ROWS, PACK_COLS).transpose(1, 0, 2, 3)
    chip_part, chip_part_b = _add_my_half(halves, _exchange_halves(halves), c_idx)
    mine = _sum_chips(chip_part, _scatter_to_chips(chip_part_b), chip_idx.reshape(1))
    both = jnp.stack([mine, _join_halves(mine)])
    reduced = jnp.where(ci == 0, both, both[::-1]).reshape(2 * HALF_ROWS, PACK_COLS)
    g_big = _unpack_shard(reduced, {n: shapes[n][1:] for n in big_names})
    big_out = [{}, {}, {}, {}]
    for n in big_names:
        big_out[0][n] = g_big[n][None]
        for kind, a in enumerate(_adamw(n, wts[n][0], g_big[n], ms[n][0], vs[n][0])):
            big_out[kind + 1][n] = a[None]

    sm_out = _small_step(_pack_small(gs), _pack_small({n: wts[n] for n in SMALL}), _pack_small({n: ms[n] for n in SMALL}),
                         _pack_small({n: vs[n] for n in SMALL}))
    small_out = [_unpack_small(a, {n: shapes[n] for n in SMALL}) for a in sm_out]

    outs = [loss, grad_x[None]]
    for kind in range(4):
        for n in WEIGHTS:
            outs.append(small_out[kind][n] if n in SMALL else big_out[kind][n])
    return tuple(outs)
```

```python
import functools
import math

import jax
import jax.numpy as jnp
from jax import lax
from jax.experimental import pallas as pl
from jax.experimental.pallas import tpu as pltpu

F32 = jnp.float32
BF16 = jnp.bfloat16
MESH = pl.DeviceIdType.MESH

D_MODEL = 1024
PLE_DIM = 256
GRID_W = 64
ATTN_HEAD_DIM = 64
ATTN_HEADS = 8
ATTN_KV_HEADS = 2
ATTN_GROUP = ATTN_HEADS // ATTN_KV_HEADS
RET_HEAD_DIM = 128
RET_HEADS = 4
ATTN_Q_W = 512
ATTN_KV_W = 128
RET_W = 512
IN_W = 4864
D_FF = 4096
RET_CHUNK = 128
ROPE_THETA = 10000.0
NORM_EPS = 1e-6
GN_EPS = 1e-5
ATTN_SCALE = ATTN_HEAD_DIM ** -0.5
LOG2E = math.log2(math.e)
Q_FOLD = ATTN_SCALE * LOG2E
RET_SCALE = RET_HEAD_DIM ** -0.5

C_AQ, C_AK, C_AV, C_RQ, C_RK, C_RV, C_RG, C_GA, C_GR = 0, 512, 640, 768, 1280, 1792, 2304, 2816, 3840

ADAM_LR = 0.001
ADAM_B1 = 0.9
ADAM_B2 = 0.999
ADAM_EPS = 1e-08
ADAM_WD = 0.01
ADAM_STEP = 10

LANES = 128
VMEM_LIMIT = 56 << 20
SEQ_TILE = 512

BIG = (("w_in", 1), ("w_attn_o", 1), ("w_ret_o", 1), ("w_out", 0), ("w_up", 1), ("w_down", 0), ("w_ple_gate", 0), ("w_ple", 1))
SMALL_ROWS = 8
SMALL = {"mix_norm": (0, 0, 1024), "mlp_norm": (1, 0, 1024), "ple_norm": (2, 0, 1024), "final_norm": (3, 0, 1024),
         "ret_norm_gain": (4, 0, 512), "attn_q_norm": (4, 512, 64), "attn_k_norm": (4, 576, 64), "ret_decay_logit": (4, 640, 8)}


def _seq_tile(s):
    return min(SEQ_TILE, s // 2)


def _cparams(sem=None, vmem=VMEM_LIMIT):
    return pltpu.CompilerParams(dimension_semantics=sem, vmem_limit_bytes=vmem)


def _mm(name, a, b, *, ta=False, tb=False, tm, tn, tk, out_dtypes=(F32,), epi=None, epi_ins=(), consts=(), n_sums=0):
    if ta:
        kdim, m = a.shape
    else:
        m, kdim = a.shape
    n = b.shape[0] if tb else b.shape[1]
    tm, tn, tk = min(tm, m), min(tn, n), min(tk, kdim)
    assert m % tm == 0 and n % tn == 0 and kdim % tk == 0, (name, m, n, kdim, tm, tn, tk)
    nk = kdim // tk
    n_e, n_c, n_o = len(epi_ins), len(consts), len(out_dtypes)
    assert n_sums == 0 or tn == n

    def body(*refs):
        a_ref, b_ref = refs[0], refs[1]
        e_refs = refs[2:2 + n_e]
        c_refs = refs[2 + n_e:2 + n_e + n_c]
        o_refs = refs[2 + n_e + n_c:2 + n_e + n_c + n_o]
        s_refs = refs[2 + n_e + n_c + n_o:2 + n_e + n_c + n_o + n_sums]
        acc_ref = refs[2 + n_e + n_c + n_o + n_sums] if nk > 1 else None
        k = pl.program_id(2)
        if n_sums:
            @pl.when((pl.program_id(0) == 0) & (k == 0))
            def _():
                for r in s_refs:
                    r[...] = jnp.zeros(r.shape, F32)
        av = a_ref[...].astype(BF16)
        bv = b_ref[...].astype(BF16)
        dims = (((0,) if ta else (1,), (1,) if tb else (0,)), ((), ()))
        part = lax.dot_general(av, bv, dims, preferred_element_type=F32)

        def finish(acc):
            vals = epi(acc, e_refs, c_refs) if epi is not None else (acc,)
            for o_ref, v in zip(o_refs, vals[:n_o]):
                o_ref[...] = v.astype(o_ref.dtype)
            for s_ref, v in zip(s_refs, vals[n_o:]):
                _acc_add(s_ref, v)

        if nk == 1:
            finish(part)
        else:
            @pl.when(k == 0)
            def _():
                acc_ref[...] = part

            @pl.when(k > 0)
            def _():
                acc_ref[...] += part

            @pl.when(k == nk - 1)
            def _():
                finish(acc_ref[...])

    a_spec = pl.BlockSpec((tk, tm), lambda i, j, k: (k, i)) if ta else pl.BlockSpec((tm, tk), lambda i, j, k: (i, k))
    b_spec = pl.BlockSpec((tn, tk), lambda i, j, k: (j, k)) if tb else pl.BlockSpec((tk, tn), lambda i, j, k: (k, j))
    o_spec = pl.BlockSpec((tm, tn), lambda i, j, k: (i, j))
    c_specs = [pl.BlockSpec(c.shape, lambda i, j, k, nd=c.ndim: (0,) * nd) for c in consts]
    outs = pl.pallas_call(
        body, name=name,
        grid=(m // tm, n // tn, nk),
        in_specs=[a_spec, b_spec] + [o_spec] * n_e + c_specs,
        out_specs=[o_spec] * n_o + [pl.BlockSpec((8, n), lambda i, j, k: (0, 0))] * n_sums,
        out_shape=[jax.ShapeDtypeStruct((m, n), dt) for dt in out_dtypes] + [jax.ShapeDtypeStruct((8, n), F32)] * n_sums,
        scratch_shapes=[pltpu.VMEM((tm, tn), F32)] if nk > 1 else [],
        compiler_params=_cparams(("arbitrary",) * 3 if n_sums else ("parallel", "parallel", "arbitrary")),
    )(a, b, *epi_ins, *consts)
    return outs[0] if n_o + n_sums == 1 else outs


def _rows(arr, tr):
    return (arr, pl.BlockSpec((tr, arr.shape[1]), lambda i: (i, 0)))


def _win(arr, tr, start, width):
    bw = math.gcd(start, width) if start else width
    assert bw % LANES == 0
    return [(arr, pl.BlockSpec((tr, bw), lambda i, cb=start // bw + p: (i, cb))) for p in range(width // bw)]


def _ct(arr):
    return (arr, pl.BlockSpec((None,) + arr.shape[1:], lambda i: (i, 0, 0)))


def _whole(arr):
    return (arr, pl.BlockSpec(arr.shape, lambda i, nd=arr.ndim: (0,) * nd))


def _cat(refs):
    vals = [r[...].astype(F32) for r in refs]
    return vals[0] if len(vals) == 1 else jnp.concatenate(vals, axis=1)


def _seqtiled(name, fn, n_tiles, ins, outs, acc_widths=()):
    n_i, n_o, n_a = len(ins), len(outs), len(acc_widths)

    def body(*refs):
        i_refs, o_refs, a_refs = refs[:n_i], refs[n_i:n_i + n_o], refs[n_i + n_o:]
        if n_a:
            @pl.when(pl.program_id(0) == 0)
            def _():
                for r in a_refs:
                    r[...] = jnp.zeros(r.shape, F32)
        fn(list(i_refs), list(o_refs), list(a_refs))

    res = pl.pallas_call(
        body, name=name, grid=(n_tiles,),
        in_specs=[s for _, s in ins],
        out_specs=[s for _, _, s in outs] + [pl.BlockSpec((8, w), lambda i: (0, 0)) for w in acc_widths],
        out_shape=[jax.ShapeDtypeStruct(sh, dt) for sh, dt, _ in outs] + [jax.ShapeDtypeStruct((8, w), F32) for w in acc_widths],
        compiler_params=_cparams(("arbitrary",)),
    )(*[a for a, _ in ins])
    return res


def _acc_add(acc_ref, val):
    acc_ref[0:1, :] += jnp.sum(val, axis=0, keepdims=True)


def _out_rows(s, w, dt, tr):
    return ((s, w), dt, pl.BlockSpec((tr, w), lambda i: (i, 0)))


def _out_ct(s, w, dt, t):
    return ((s // t, w, t), dt, pl.BlockSpec((None, w, t), lambda i: (i, 0, 0)))


def _rms_fwd(x, gain):
    r = lax.rsqrt(jnp.mean(x * x, axis=-1, keepdims=True) + NORM_EPS)
    return x * r * gain


def _rms_bwd(dy, x, gain):
    r = lax.rsqrt(jnp.mean(x * x, axis=-1, keepdims=True) + NORM_EPS)
    xn = x * r
    dyg = dy * gain
    dx = r * (dyg - xn * jnp.mean(dyg * xn, axis=-1, keepdims=True))
    return dx, dy * xn


def _seg_mean(y, hd):
    w = y.shape[1]
    pieces = []
    for s in range(0, w, LANES):
        v = y[:, s:s + LANES]
        tot = jnp.sum(v, axis=1, keepdims=True)
        if hd == LANES:
            pieces.append(jnp.broadcast_to(tot, v.shape))
        else:
            low = lax.broadcasted_iota(jnp.int32, v.shape, 1) < hd
            lo = jnp.sum(jnp.where(low, v, 0.0), axis=1, keepdims=True)
            pieces.append(jnp.where(low, lo, tot - lo))
    out = pieces[0] if len(pieces) == 1 else jnp.concatenate(pieces, axis=1)
    return out * (1.0 / hd)


def _tile_lanes(t, w):
    return t if w == t.shape[1] else jnp.concatenate([t] * (w // t.shape[1]), axis=1)


def _swap_halves(x, hd):
    w = x.shape[1]
    half = hd // 2
    lane = lax.broadcasted_iota(jnp.int32, x.shape, 1)
    return jnp.where((lane % hd) < half, pltpu.roll(x, w - half, 1), pltpu.roll(x, half, 1))


def _rope(x, cos, sin_signed, hd):
    w = x.shape[1]
    return x * _tile_lanes(cos, w) + _swap_halves(x, hd) * _tile_lanes(sin_signed, w)


def _rope_t(dy, cos, sin_signed, hd):
    w = dy.shape[1]
    return dy * _tile_lanes(cos, w) + _swap_halves(dy * _tile_lanes(sin_signed, w), hd)


def _headnorm_fwd(x, gain_w, hd):
    r = lax.rsqrt(_seg_mean(x * x, hd) + NORM_EPS)
    return x * r * gain_w


def _headnorm_bwd(dy, x, gain_w, hd):
    r = lax.rsqrt(_seg_mean(x * x, hd) + NORM_EPS)
    xn = x * r
    dyg = dy * gain_w
    return r * (dyg - xn * _seg_mean(dyg * xn, hd)), dy * xn


def _sigmoid(x):
    return 1.0 / (1.0 + jnp.exp(-x))


def _rope_tables(seq_len, head_dim):
    rows = seq_len // GRID_W
    n_axis = head_dim // 4
    freqs = ROPE_THETA ** (-jnp.arange(n_axis, dtype=F32) / n_axis)
    ang_r = jnp.arange(rows, dtype=F32)[:, None] * freqs
    ang_c = jnp.arange(GRID_W, dtype=F32)[:, None] * freqs

    def expand(by_row, by_col):
        r = jnp.broadcast_to(by_row[:, None, :], (rows, GRID_W, n_axis))
        c = jnp.broadcast_to(by_col[None, :, :], (rows, GRID_W, n_axis))
        return jnp.concatenate([r, c], axis=-1).reshape(seq_len, 2 * n_axis)

    cos, sin = expand(jnp.cos(ang_r), jnp.cos(ang_c)), expand(jnp.sin(ang_r), jnp.sin(ang_c))
    reps = LANES // head_dim
    return jnp.tile(jnp.concatenate([cos, cos], axis=-1), (1, reps)), jnp.tile(jnp.concatenate([-sin, sin], axis=-1), (1, reps))


def _stage_norm_in(x, gain):
    s = x.shape[0]
    tr = min(SEQ_TILE, s)

    def fn(i, o, a):
        o[0][...] = _rms_fwd(i[0][...], i[1][...]).astype(BF16)

    return _seqtiled("norm_in", fn, s // tr, [_rows(x, tr), _whole(gain)], [_out_rows(s, D_MODEL, BF16, tr)])[0]


def _stage_qkv(proj, tabs, gq_w, gk_w):
    s = proj.shape[0]
    t = _seq_tile(s)
    ca, sa, cr, sr = tabs
    ins = (_win(proj, t, C_AQ, ATTN_Q_W) + _win(proj, t, C_AK, ATTN_KV_W) + _win(proj, t, C_AV, ATTN_KV_W)
           + _win(proj, t, C_RQ, RET_W) + _win(proj, t, C_RK, RET_W)
           + [_rows(ca, t), _rows(sa, t), _rows(cr, t), _rows(sr, t), _whole(gq_w), _whole(gk_w)])

    def fn(i, o, a):
        aq, ak, av = (i[n][...].astype(F32) for n in range(3))
        rq, rk = _cat(i[3:5]), _cat(i[5:7])
        ca_, sa_, cr_, sr_ = i[7][...], i[8][...], i[9][...], i[10][...]
        qr = _rope(_headnorm_fwd(aq, i[11][...], ATTN_HEAD_DIM), ca_, sa_, ATTN_HEAD_DIM) * Q_FOLD
        kr = _rope(_headnorm_fwd(ak, i[12][...], ATTN_HEAD_DIM), ca_, sa_, ATTN_HEAD_DIM)
        qt = qr.T.astype(BF16)
        zeros = jnp.zeros((ATTN_HEAD_DIM, t), BF16)
        for h in range(ATTN_HEADS):
            g = h // ATTN_GROUP
            blk = qt[h * ATTN_HEAD_DIM:(h + 1) * ATTN_HEAD_DIM, :]
            o[0][h * LANES + g * ATTN_HEAD_DIM:h * LANES + (g + 1) * ATTN_HEAD_DIM, :] = blk
            o[0][h * LANES + (1 - g) * ATTN_HEAD_DIM:h * LANES + (2 - g) * ATTN_HEAD_DIM, :] = zeros
        o[1][...] = kr.astype(BF16)
        o[2][...] = kr.T.astype(BF16)
        o[3][...] = av.astype(BF16)
        o[4][...] = av.T.astype(BF16)
        o[5][...] = _rope(rq, cr_, sr_, RET_HEAD_DIM) * RET_SCALE
        o[6][...] = _rope(rk, cr_, sr_, RET_HEAD_DIM)

    outs = [_out_ct(s, ATTN_HEADS * LANES, BF16, t), _out_rows(s, ATTN_KV_W, BF16, t), _out_ct(s, ATTN_KV_W, BF16, t),
            _out_rows(s, ATTN_KV_W, BF16, t), _out_ct(s, ATTN_KV_W, BF16, t), _out_rows(s, RET_W, F32, t), _out_rows(s, RET_W, F32, t)]
    return _seqtiled("qkv_prep", fn, s // t, ins, outs)


def _groupnorm_gate(ry, rg, gain):
    mu = _seg_mean(ry, RET_HEAD_DIM)
    d = ry - mu
    rs = lax.rsqrt(_seg_mean(d * d, RET_HEAD_DIM) + GN_EPS)
    return d * rs, rs, _sigmoid(rg)


def _stage_mix_post(ry_f, ry_b, proj, o_ct, gain):
    s = proj.shape[0]
    t = _seq_tile(s)
    ins = [_rows(ry_f, t), _rows(ry_b, t)] + _win(proj, t, C_RG, RET_W) + [_ct(o_ct), _whole(gain)]

    def fn(i, o, a):
        ry = i[0][...] + i[1][...]
        rg = _cat(i[2:4])
        gn, _, sg = _groupnorm_gate(ry, rg, None)
        o[0][...] = (gn * i[5][...] * (rg * sg)).astype(BF16)
        o[1][...] = i[4][...].astype(F32).T.astype(BF16)

    return _seqtiled("mix_post", fn, s // t, ins, [_out_rows(s, RET_W, BF16, t), _out_rows(s, ATTN_Q_W, BF16, t)])


def _stage_merge(proj, a_out, r_out):
    s = proj.shape[0]
    tr = min(SEQ_TILE, s)
    ins = _win(proj, tr, C_GA, D_MODEL) + _win(proj, tr, C_GR, D_MODEL) + [_rows(a_out, tr), _rows(r_out, tr)]
    na = len(_win(proj, tr, C_GA, D_MODEL))

    def fn(i, o, a):
        ga, gr = _cat(i[:na]), _cat(i[na:2 * na])
        o[0][...] = (_sigmoid(ga) * i[2 * na][...] + _sigmoid(gr) * i[2 * na + 1][...]).astype(BF16)

    return _seqtiled("merge", fn, s // tr, ins, [_out_rows(s, D_MODEL, BF16, tr)])[0]


def _stage_head(zg, pe, x2, target, g_final):
    s = x2.shape[0]
    tr = min(SEQ_TILE // 2, s)
    ins = [_rows(zg, tr), _rows(pe, tr), _rows(x2, tr), _rows(target, tr), _whole(g_final)]

    def fn(i, o, a):
        gt = _sigmoid(i[0][...])
        pe_ = i[1][...]
        x3 = i[2][...] + gt * pe_
        gf = i[4][...]
        r3 = lax.rsqrt(jnp.mean(x3 * x3, axis=-1, keepdims=True) + NORM_EPS)
        x3n = x3 * r3
        e = x3n * gf - i[3][...]
        _acc_add(a[0], e * e)
        dy = e * (1.0 / D_MODEL)
        _acc_add(a[1], dy * x3n)
        dyg = dy * gf
        dx3 = r3 * (dyg - x3n * jnp.mean(dyg * x3n, axis=-1, keepdims=True))
        o[0][...] = dx3
        o[1][...] = (dx3 * pe_ * gt * (1.0 - gt)).astype(BF16)
        o[2][...] = (dx3 * gt).astype(BF16)

    outs = [_out_rows(s, D_MODEL, F32, tr), _out_rows(s, D_MODEL, BF16, tr), _out_rows(s, D_MODEL, BF16, tr)]
    return _seqtiled("head", fn, s // tr, ins, outs, acc_widths=(D_MODEL, D_MODEL))


def _stage_merge_bwd(proj, dmerged, a_out, r_out):
    s = proj.shape[0]
    tr = min(SEQ_TILE // 2, s)
    wins = _win(proj, tr, C_GA, D_MODEL)
    na = len(wins)
    ins = wins + _win(proj, tr, C_GR, D_MODEL) + [_rows(dmerged, tr), _rows(a_out, tr), _rows(r_out, tr)]

    def fn(i, o, a):
        sa, sr = _sigmoid(_cat(i[:na])), _sigmoid(_cat(i[na:2 * na]))
        dm = i[2 * na][...]
        o[0][...] = (dm * sa).astype(BF16)
        o[1][...] = (dm * sr).astype(BF16)
        o[2][...] = (dm * i[2 * na + 1][...] * sa * (1.0 - sa)).astype(BF16)
        o[3][...] = (dm * i[2 * na + 2][...] * sr * (1.0 - sr)).astype(BF16)

    return _seqtiled("merge_bwd", fn, s // tr, ins, [_out_rows(s, D_MODEL, BF16, tr)] * 4)


def _stage_mix_post_bwd(dattn, attn_rows, drz, ry_f, ry_b, proj, gain):
    s = proj.shape[0]
    t = _seq_tile(s)
    ins = ([_rows(dattn, t), _rows(attn_rows, t), _rows(drz, t), _rows(ry_f, t), _rows(ry_b, t)]
           + _win(proj, t, C_RG, RET_W) + [_whole(gain)])

    def fn(i, o, a):
        da = i[0][...]
        dat = da.T
        prod_t = (da * i[1][...].astype(F32)).T
        dat_b = dat.astype(BF16)
        zeros = jnp.zeros((ATTN_HEAD_DIM, t), BF16)
        for h in range(ATTN_HEADS):
            g = h // ATTN_GROUP
            o[0][h * LANES + g * ATTN_HEAD_DIM:h * LANES + (g + 1) * ATTN_HEAD_DIM, :] = dat_b[h * ATTN_HEAD_DIM:(h + 1) * ATTN_HEAD_DIM, :]
            o[0][h * LANES + (1 - g) * ATTN_HEAD_DIM:h * LANES + (2 - g) * ATTN_HEAD_DIM, :] = zeros
            o[1][h] = jnp.sum(prod_t[h * ATTN_HEAD_DIM:(h + 1) * ATTN_HEAD_DIM, :], axis=0, keepdims=True)
        ry = i[3][...] + i[4][...]
        rg = _cat(i[5:7])
        gain_ = i[7][...]
        gn, rs, sg = _groupnorm_gate(ry, rg, None)
        dz = i[2][...]
        silu = rg * sg
        _acc_add(a[0], dz * gn * silu)
        dgn = dz * gain_ * silu
        o[2][...] = rs * (dgn - _seg_mean(dgn, RET_HEAD_DIM) - gn * _seg_mean(dgn * gn, RET_HEAD_DIM))
        o[3][...] = (dz * gn * gain_ * (sg * (1.0 + rg * (1.0 - sg)))).astype(BF16)

    outs = [_out_ct(s, ATTN_HEADS * LANES, BF16, t),
            ((ATTN_HEADS, s // t, 1, t), F32, pl.BlockSpec((ATTN_HEADS, None, 1, t), lambda i: (0, i, 0, 0))),
            _out_rows(s, RET_W, F32, t), _out_rows(s, RET_W, BF16, t)]
    return _seqtiled("mix_post_bwd", fn, s // t, ins, outs, acc_widths=(RET_W,))


def _stage_dproj(proj, dq_ct, dk8, dv8, rgrads, drg, dga, dgr, tabs, gq_w, gk_w):
    s = proj.shape[0]
    t = _seq_tile(s)
    ca, sa, cr, sr = tabs
    kv8 = pl.BlockSpec((ATTN_HEADS, t, ATTN_KV_W), lambda i: (0, i, 0))
    ins = (_win(proj, t, C_AQ, ATTN_Q_W) + _win(proj, t, C_AK, ATTN_KV_W) + [_ct(dq_ct), (dk8, kv8), (dv8, kv8)]
           + [_rows(g, t) for g in rgrads] + [_rows(drg, t), _rows(dga, t), _rows(dgr, t)]
           + [_rows(ca, t), _rows(sa, t), _rows(cr, t), _rows(sr, t), _whole(gq_w), _whole(gk_w)])

    def fn(i, o, a):
        aq, ak = i[0][...].astype(F32), i[1][...].astype(F32)
        dq_f, dk_f, dv_f, dq_b, dk_b, dv_b = (r[...].astype(F32) for r in i[5:11])
        ca_, sa_, cr_, sr_ = i[14][...], i[15][...], i[16][...], i[17][...]
        dqn = _rope_t(i[2][...].T * ATTN_SCALE, ca_, sa_, ATTN_HEAD_DIM)
        daq, gq_rows = _headnorm_bwd(dqn, aq, i[18][...], ATTN_HEAD_DIM)
        dkn = _rope_t(jnp.sum(i[3][...].astype(F32), axis=0) * (1.0 / LOG2E), ca_, sa_, ATTN_HEAD_DIM)
        dak, gk_rows = _headnorm_bwd(dkn, ak, i[19][...], ATTN_HEAD_DIM)
        _acc_add(a[0], gq_rows)
        _acc_add(a[1], gk_rows)
        out = o[0]
        out[:, C_AQ:C_AQ + ATTN_Q_W] = daq.astype(BF16)
        out[:, C_AK:C_AK + ATTN_KV_W] = dak.astype(BF16)
        out[:, C_AV:C_AV + ATTN_KV_W] = jnp.sum(i[4][...].astype(F32), axis=0).astype(BF16)
        out[:, C_RQ:C_RQ + RET_W] = _rope_t((dq_f + dq_b) * RET_SCALE, cr_, sr_, RET_HEAD_DIM).astype(BF16)
        out[:, C_RK:C_RK + RET_W] = _rope_t(dk_f + dk_b, cr_, sr_, RET_HEAD_DIM).astype(BF16)
        out[:, C_RV:C_RV + RET_W] = (dv_f + dv_b).astype(BF16)
        out[:, C_RG:C_RG + RET_W] = i[11][...]
        out[:, C_GA:C_GA + D_MODEL] = i[12][...]
        out[:, C_GR:C_GR + D_MODEL] = i[13][...]

    return _seqtiled("dproj", fn, s // t, ins, [_out_rows(s, IN_W, BF16, t)], acc_widths=(ATTN_Q_W, ATTN_KV_W))


def _attn_fwd(q_ct, k_rows, v_ct):
    nq, _, t = q_ct.shape
    s = nq * t
    nk = nq
    assert nk % 2 == 0
    n_par = 2

    def body(q_ref, k_ref, v_ref, o_ref, lse_ref, *bufs):
        sbuf = (bufs[0:2], bufs[2:4])
        pbuf = (bufs[4:6], bufs[6:8])

        def scores(w, j, slot):
            kj = k_ref[pl.ds(pl.multiple_of(j * t, t), t), :]
            st = jnp.dot(kj, q_ref[w], preferred_element_type=F32)
            sbuf[w][slot][...] = st
            return jnp.max(st, axis=0, keepdims=True)

        def probs(w, slot, cmax, m, l):
            m_new = jnp.maximum(m, cmax)
            alpha = jnp.exp2(m - m_new)
            pt = jnp.exp2(sbuf[w][slot][...] - m_new)
            pbuf[w][slot][...] = pt.astype(BF16)
            return m_new, alpha * l + jnp.sum(pt, axis=0, keepdims=True), alpha

        def values(w, j, slot, alpha, acc):
            return alpha * acc + jnp.dot(v_ref[j], pbuf[w][slot][...], preferred_element_type=F32)

        init = []
        for w in range(n_par):
            m = jnp.full((1, t), -1e30, F32)
            l = jnp.zeros((1, t), F32)
            cmax0 = scores(w, 0, 0)
            cmax1 = scores(w, 1, 1)
            m, l, alpha0 = probs(w, 0, cmax0, m, l)
            init.append((m, l, jnp.zeros((ATTN_HEAD_DIM, t), F32), cmax1, alpha0))

        def trip(n, carry):
            c = 2 * n
            out = []
            for w in range(n_par):
                m, l, acc, cmax_b, alpha_c = carry[w]
                acc = values(w, c, 0, alpha_c, acc)
                m, l, alpha1 = probs(w, 1, cmax_b, m, l)
                cmax2 = scores(w, c + 2, 0)
                acc = values(w, c + 1, 1, alpha1, acc)
                m, l, alpha2 = probs(w, 0, cmax2, m, l)
                cmax3 = scores(w, c + 3, 1)
                out.append((m, l, acc, cmax3, alpha2))
            return tuple(out)

        res = lax.fori_loop(0, nk // 2 - 1, trip, tuple(init))
        for w in range(n_par):
            m, l, acc, cmax_b, alpha_c = res[w]
            acc = values(w, nk - 2, 0, alpha_c, acc)
            m, l, alpha1 = probs(w, 1, cmax_b, m, l)
            acc = values(w, nk - 1, 1, alpha1, acc)
            o_ref[w] = (acc / l).astype(BF16)
            lse_ref[w] = m + jnp.log2(l)

    return pl.pallas_call(
        body, name="attn_fwd", grid=(ATTN_HEADS, nq // n_par),
        in_specs=[pl.BlockSpec((n_par, LANES, t), lambda h, i: (i, h, 0)),
                  pl.BlockSpec((s, ATTN_KV_W), lambda h, i: (0, 0)),
                  pl.BlockSpec((nk, ATTN_HEAD_DIM, t), lambda h, i: (0, h // ATTN_GROUP, 0))],
        out_specs=[pl.BlockSpec((n_par, ATTN_HEAD_DIM, t), lambda h, i: (i, h, 0)),
                   pl.BlockSpec((None, n_par, 1, t), lambda h, i: (h, i, 0, 0))],
        out_shape=[jax.ShapeDtypeStruct((nq, ATTN_Q_W, t), BF16), jax.ShapeDtypeStruct((ATTN_HEADS, nq, 1, t), F32)],
        scratch_shapes=[pltpu.VMEM((t, t), F32)] * (2 * n_par) + [pltpu.VMEM((t, t), BF16)] * (2 * n_par),
        compiler_params=_cparams(("parallel", "parallel")),
    )(q_ct, k_rows, v_ct)


def _attn_bwd(q_ct, do_ct, lse, delta, k_rows, v_rows, k_ct):
    nq, _, t = q_ct.shape
    s = nq * t
    nk = nq

    assert nq % 2 == 0

    def body(q_ref, do_ref, lse_ref, delta_ref, k_ref, v_ref, kt_ref, dq_ref, dk_ref, dv_ref, dk_acc, dv_acc,
             sb0, sb1, db0, db1, pb0, pb1, gb0, gb1):
        j = pl.program_id(1)
        sb, db, pb, gb = (sb0, sb1), (db0, db1), (pb0, pb1), (gb0, gb1)

        @pl.when(j == 0)
        def _():
            dq_ref[...] = jnp.zeros(dq_ref.shape, F32)

        kj, vj, ktj = k_ref[...], v_ref[...], kt_ref[...]
        dk_acc[...] = jnp.zeros(dk_acc.shape, F32)
        dv_acc[...] = jnp.zeros(dv_acc.shape, F32)

        def products(i, slot):
            sb[slot][...] = jnp.dot(kj, q_ref[i], preferred_element_type=F32)
            db[slot][...] = jnp.dot(vj, do_ref[i], preferred_element_type=F32)

        def cotangents(i, slot):
            pt = jnp.exp2(sb[slot][...] - lse_ref[i])
            pb[slot][...] = pt.astype(BF16)
            gb[slot][...] = (pt * (db[slot][...] - delta_ref[i])).astype(BF16)

        def accumulate(i, slot):
            dst = gb[slot][...]
            dv_acc[...] += _nt(pb[slot][...], do_ref[i])
            dk_acc[...] += _nt(dst, q_ref[i])
            dq_ref[i] += jnp.dot(ktj, dst, preferred_element_type=F32)

        products(0, 0)
        products(1, 1)
        cotangents(0, 0)

        def trip(n, carry):
            c = 2 * n
            accumulate(c, 0)
            cotangents(c + 1, 1)
            products(c + 2, 0)
            accumulate(c + 1, 1)
            cotangents(c + 2, 0)
            products(c + 3, 1)
            return carry

        lax.fori_loop(0, nq // 2 - 1, trip, 0)
        accumulate(nq - 2, 0)
        cotangents(nq - 1, 1)
        accumulate(nq - 1, 1)
        dk_ref[...] = dk_acc[...].astype(dk_ref.dtype)
        dv_ref[...] = dv_acc[...].astype(dv_ref.dtype)

    per_head = pl.BlockSpec((nq, LANES, t), lambda h, j: (0, h, 0))
    stat = pl.BlockSpec((None, nq, 1, t), lambda h, j: (h, 0, 0, 0))
    kv_rows = pl.BlockSpec((t, ATTN_KV_W), lambda h, j: (j, 0))
    kv_out = pl.BlockSpec((None, t, ATTN_KV_W), lambda h, j: (h, j, 0))
    return pl.pallas_call(
        body, name="attn_bwd", grid=(ATTN_HEADS, nk),
        in_specs=[per_head, per_head, stat, stat, kv_rows, kv_rows,
                  pl.BlockSpec((None, ATTN_HEAD_DIM, t), lambda h, j: (j, h // ATTN_GROUP, 0))],
        out_specs=[pl.BlockSpec((nq, ATTN_HEAD_DIM, t), lambda h, j: (0, h, 0)), kv_out, kv_out],
        out_shape=[jax.ShapeDtypeStruct((nq, ATTN_Q_W, t), F32), jax.ShapeDtypeStruct((ATTN_HEADS, s, ATTN_KV_W), BF16),
                   jax.ShapeDtypeStruct((ATTN_HEADS, s, ATTN_KV_W), BF16)],
        scratch_shapes=([pltpu.VMEM((t, ATTN_KV_W), F32)] * 2 + [pltpu.VMEM((t, t), F32)] * 4 + [pltpu.VMEM((t, t), BF16)] * 4),
        compiler_params=_cparams(("parallel", "arbitrary")),
    )(q_ct, do_ct, lse, delta, k_rows, v_rows, k_ct)


def _log_sigmoid(x):
    t = jnp.exp(-jnp.abs(x))
    log1p_t = jnp.where(t < 1e-2, t * (1.0 - t * (0.5 - t * (1.0 / 3.0))), jnp.log(1.0 + t))
    return jnp.minimum(x, 0.0) - log1p_t


def _decay_tables(logit, backward):
    c = RET_CHUNK
    lam = _log_sigmoid(jnp.full((c, c), logit, F32))
    ii = lax.broadcasted_iota(jnp.int32, (c, c), 0).astype(F32)
    jj = lax.broadcasted_iota(jnp.int32, (c, c), 1).astype(F32)
    if not backward:
        dist, dist_t = jnp.maximum(ii - jj, 0.0), jnp.maximum(jj - ii, 0.0)
        mask, mask_t = ii >= jj, jj >= ii
        e_q, e_k = ii + 1.0, (c - 1.0) - ii
    else:
        dist, dist_t = jnp.maximum(jj - ii, 0.0), jnp.maximum(ii - jj, 0.0)
        mask, mask_t = jj > ii, ii > jj
        e_q, e_k = c - ii, ii
    return dict(
        d=jnp.where(mask, jnp.exp(lam * dist), 0.0), d_t=jnp.where(mask_t, jnp.exp(lam * dist_t), 0.0), dist=dist,
        qdec=jnp.exp(lam * e_q), kdec=jnp.exp(lam * e_k), e_q=e_q, e_k=e_k, gam=jnp.exp(lam * c))


def _nt(a, b):
    return lax.dot_general(a, b, (((1,), (1,)), ((), ())), preferred_element_type=F32)


RET_SUB = 2


def _ret_fwd(logits, q, k, proj):
    s = q.shape[0]
    c = RET_CHUNK
    sub = RET_SUB
    nb = s // (c * sub)
    block = (lambda n: n, lambda n: nb - 1 - n)
    order = (tuple(range(sub)), tuple(reversed(range(sub))))
    vwin = _win(proj, c * sub, C_RV, RET_W)
    nv = len(vwin)
    vw = RET_W // nv
    per = 2 + nv

    def body(lg_ref, *refs):
        ins, outs, states = refs[:2 * per], refs[2 * per:2 * per + 4], refs[2 * per + 4:]

        @pl.when(pl.program_id(0) == 0)
        def _():
            for st in states:
                st[...] = jnp.zeros(st.shape, F32)

        for h in range(RET_HEADS):
            for d in range(2):
                q_ref, k_ref, v_refs = ins[d * per], ins[d * per + 1], ins[d * per + 2:(d + 1) * per]
                y_ref, st_ref, state = outs[2 * d], outs[2 * d + 1], states[d]
                tb = _decay_tables(lg_ref[d, h], bool(d))
                sl = slice(h * RET_HEAD_DIM, (h + 1) * RET_HEAD_DIM)
                off = h * RET_HEAD_DIM
                sh = state[h]
                for u in order[d]:
                    rows = slice(u * c, (u + 1) * c)
                    qh, kh = q_ref[rows, sl], k_ref[rows, sl]
                    vb = v_refs[off // vw][rows, off % vw:off % vw + RET_HEAD_DIM].astype(BF16)
                    a = _nt(qh.astype(BF16), kh.astype(BF16)) * tb["d"]
                    st_ref[u, h] = sh
                    y_ref[rows, sl] = (jnp.dot(a.astype(BF16), vb, preferred_element_type=F32)
                                       + jnp.dot((qh * tb["qdec"]).astype(BF16), sh.astype(BF16), preferred_element_type=F32))
                    sh = tb["gam"] * sh + jnp.dot((kh * tb["kdec"]).T.astype(BF16), vb, preferred_element_type=F32)
                state[h] = sh

    hmat = (RET_HEADS, RET_HEAD_DIM, RET_HEAD_DIM)
    in_specs, out_specs, args = [pl.BlockSpec(memory_space=pltpu.SMEM)], [], [logits]
    for d in range(2):
        rows = pl.BlockSpec((c * sub, RET_W), lambda n, d=d: (block[d](n), 0))
        in_specs += [rows, rows] + [pl.BlockSpec(sp.block_shape, lambda n, d=d, cb=sp.index_map(0)[1]: (block[d](n), cb)) for _, sp in vwin]
        args += [q, k] + [a for a, _ in vwin]
        out_specs += [rows, pl.BlockSpec((sub,) + hmat, lambda n, d=d: (block[d](n), 0, 0, 0))]
    return pl.pallas_call(
        body, name="ret_fwd", grid=(nb,), in_specs=in_specs, out_specs=out_specs,
        out_shape=[jax.ShapeDtypeStruct((s, RET_W), F32), jax.ShapeDtypeStruct((nb * sub,) + hmat, F32)] * 2,
        scratch_shapes=[pltpu.VMEM(hmat, F32)] * 2,
        compiler_params=_cparams(("arbitrary",)),
    )(*args)


def _ret_bwd(logits, q, k, proj, dy, st_f, st_b):
    s = q.shape[0]
    c = RET_CHUNK
    sub = RET_SUB
    nb = s // (c * sub)
    block = (lambda n: nb - 1 - n, lambda n: n)
    order = (tuple(reversed(range(sub))), tuple(range(sub)))
    vwin = _win(proj, c * sub, C_RV, RET_W)
    nv = len(vwin)
    vw = RET_W // nv
    per = 4 + nv

    def body(lg_ref, *refs):
        ins, outs, scr = refs[:2 * per], refs[2 * per:2 * per + 8], refs[2 * per + 8:]
        n = pl.program_id(0)

        @pl.when(n == 0)
        def _():
            for r in scr:
                r[...] = jnp.zeros(r.shape, F32)

        for h in range(RET_HEADS):
            for d in range(2):
                q_ref, k_ref, dy_ref, st_ref = ins[d * per:d * per + 4]
                v_refs = ins[d * per + 4:(d + 1) * per]
                dq_ref, dk_ref, dv_ref = outs[4 * d:4 * d + 3]
                dstate, lacc = scr[2 * d], scr[2 * d + 1]
                tb = _decay_tables(lg_ref[d, h], bool(d))
                sl = slice(h * RET_HEAD_DIM, (h + 1) * RET_HEAD_DIM)
                off = h * RET_HEAD_DIM
                dsh = dstate[h]
                lsum = lacc[h]
                for u in order[d]:
                    rows = slice(u * c, (u + 1) * c)
                    qh, kh, dyh = q_ref[rows, sl], k_ref[rows, sl], dy_ref[rows, sl]
                    vb = v_refs[off // vw][rows, off % vw:off % vw + RET_HEAD_DIM].astype(BF16)
                    qb, kb, dyb = qh.astype(BF16), kh.astype(BF16), dyh.astype(BF16)
                    sh = st_ref[u, h]
                    shb, dshb = sh.astype(BF16), dsh.astype(BF16)
                    qk = _nt(qb, kb)
                    g = _nt(dyb, vb) * tb["d"]
                    a_t = _nt(kb, qb) * tb["d_t"]
                    g_t = _nt(vb, dyb) * tb["d_t"]
                    qd, kd = qh * tb["qdec"], kh * tb["kdec"]
                    dqd = _nt(dyb, shb)
                    dkd = _nt(vb, dshb)
                    dq_ref[rows, sl] = (jnp.dot(g.astype(BF16), kb, preferred_element_type=F32) + dqd * tb["qdec"]).astype(dq_ref.dtype)
                    dk_ref[rows, sl] = (jnp.dot(g_t.astype(BF16), qb, preferred_element_type=F32) + dkd * tb["kdec"]).astype(dk_ref.dtype)
                    dv_ref[rows, sl] = (jnp.dot(a_t.astype(BF16), dyb, preferred_element_type=F32)
                                        + jnp.dot(kd.astype(BF16), dshb, preferred_element_type=F32)).astype(dv_ref.dtype)
                    lsum = lsum + (tb["dist"] * qk * g + tb["e_q"] * qd * dqd + tb["e_k"] * kd * dkd
                                   + float(c) * tb["gam"] * dsh * sh)
                    dsh = tb["gam"] * dsh + jnp.dot(qd.T.astype(BF16), dyb, preferred_element_type=F32)
                dstate[h] = dsh
                lacc[h] = lsum

        @pl.when(n == nb - 1)
        def _():
            for d in range(2):
                for h in range(RET_HEADS):
                    outs[4 * d + 3][h] = jnp.zeros((8, LANES), F32) + jnp.sum(scr[2 * d + 1][h])

    hmat = (RET_HEADS, RET_HEAD_DIM, RET_HEAD_DIM)
    in_specs, out_specs, args = [pl.BlockSpec(memory_space=pltpu.SMEM)], [], [logits]
    for d, states in enumerate((st_f, st_b)):
        rows = pl.BlockSpec((c * sub, RET_W), lambda n, d=d: (block[d](n), 0))
        in_specs += ([rows, rows, rows, pl.BlockSpec((sub,) + hmat, lambda n, d=d: (block[d](n), 0, 0, 0))]
                     + [pl.BlockSpec(sp.block_shape, lambda n, d=d, cb=sp.index_map(0)[1]: (block[d](n), cb)) for _, sp in vwin])
        args += [q, k, dy, states] + [a for a, _ in vwin]
        out_specs += [rows, rows, rows, pl.BlockSpec((RET_HEADS, 8, LANES), lambda n: (0, 0, 0))]
    return pl.pallas_call(
        body, name="ret_bwd", grid=(nb,), in_specs=in_specs, out_specs=out_specs,
        out_shape=([jax.ShapeDtypeStruct((s, RET_W), BF16)] * 3 + [jax.ShapeDtypeStruct((RET_HEADS, 8, LANES), F32)]) * 2,
        scratch_shapes=[pltpu.VMEM(hmat, F32)] * 4,
        compiler_params=_cparams(("arbitrary",)),
    )(*args)


def _local_step(x, p, target, w, small):
    s = x.shape[0]
    tabs = _rope_tables(s, ATTN_HEAD_DIM) + _rope_tables(s, RET_HEAD_DIM)
    g_mix, g_mlp, g_ple = small["mix_norm"][None, :], small["mlp_norm"][None, :], small["ple_norm"][None, :]
    g_final, g_ret = small["final_norm"][None, :], small["ret_norm_gain"][None, :]
    gq_w = jnp.tile(small["attn_q_norm"], ATTN_HEADS)[None, :]
    gk_w = jnp.tile(small["attn_k_norm"], ATTN_KV_HEADS)[None, :]
    logits = small["ret_decay_logit"]

    hb = _stage_norm_in(x, g_mix)
    proj = _mm("in_proj", hb, w["w_in"], tm=512, tn=IN_W // 2, tk=1024, out_dtypes=(BF16,))
    q_ct, k_rows, k_ct, v_rows, v_ct, rq, rk = _stage_qkv(proj, tabs, gq_w, gk_w)
    o_ct, lse = _attn_fwd(q_ct, k_rows, v_ct)
    ry_f, st_f, ry_b, st_b = _ret_fwd(logits, rq, rk, proj)
    rz, attn_rows = _stage_mix_post(ry_f, ry_b, proj, o_ct, g_ret)
    a_out = _mm("attn_o", attn_rows, w["w_attn_o"], tm=1024, tn=1024, tk=512, out_dtypes=(BF16,))
    r_out = _mm("ret_o", rz, w["w_ret_o"], tm=1024, tn=1024, tk=512, out_dtypes=(BF16,))
    merged = _stage_merge(proj, a_out, r_out)

    def epi_res_norm(acc, e, c):
        xr = e[0][...] + acc
        return xr, _rms_fwd(xr, c[0][...])

    x1, hm = _mm("out_proj", merged, w["w_out"], tm=512, tn=1024, tk=1024, out_dtypes=(F32, BF16),
                 epi=epi_res_norm, epi_ins=(x,), consts=(g_mlp,))

    def epi_relu2(acc, e, c):
        r = jnp.maximum(acc, 0.0)
        return (r * r,)

    act = _mm("mlp_up", hm, w["w_up"], tm=512, tn=2048, tk=1024, out_dtypes=(BF16,), epi=epi_relu2)
    x2, hp = _mm("mlp_down", act, w["w_down"], tm=512, tn=1024, tk=2048, out_dtypes=(F32, BF16),
                 epi=epi_res_norm, epi_ins=(x1,), consts=(g_ple,))
    zg = _mm("ple_gate", hp, w["w_ple_gate"], tm=1024, tn=1024, tk=1024)
    pe = _mm("ple_emb", p, w["w_ple"], tm=1024, tn=1024, tk=256)
    dx3, dzg, dpe, loss_cols, g_final_p = _stage_head(zg, pe, x2, target, g_final)
    loss_sum = 0.5 / D_MODEL * jnp.sum(loss_cols)

    gw = {}
    gw["w_ple"] = _mm("g_w_ple", p, dpe, ta=True, tm=256, tn=1024, tk=2048)
    gw["w_ple_gate"] = _mm("g_w_ple_gate", hp, dzg, ta=True, tm=1024, tn=1024, tk=2048)
    def epi_norm_bwd(acc, e, c):
        dx, dg = _rms_bwd(acc, e[0][...], c[0][...])
        return e[1][...] + dx, dg

    dx2, g_ple_p = _mm("d_hp", dzg, w["w_ple_gate"], tb=True, tm=512, tn=1024, tk=1024, epi=epi_norm_bwd, epi_ins=(x2, dx3),
                       consts=(g_ple,), n_sums=1)

    def epi_relu2_bwd(acc, e, c):
        return (acc * (2.0 * jnp.sqrt(e[0][...].astype(F32))),)

    du = _mm("d_u", dx2, w["w_down"], tb=True, tm=512, tn=2048, tk=1024, out_dtypes=(BF16,), epi=epi_relu2_bwd, epi_ins=(act,))
    gw["w_down"] = _mm("g_w_down", act, dx2, ta=True, tm=1024, tn=1024, tk=2048)
    gw["w_up"] = _mm("g_w_up", hm, du, ta=True, tm=1024, tn=1024, tk=2048)
    dx1, g_mlp_p = _mm("d_hm", du, w["w_up"], tb=True, tm=512, tn=1024, tk=2048, epi=epi_norm_bwd, epi_ins=(x1, dx2),
                       consts=(g_mlp,), n_sums=1)
    dmerged = _mm("d_merged", dx1, w["w_out"], tb=True, tm=1024, tn=1024, tk=1024)
    gw["w_out"] = _mm("g_w_out", merged, dx1, ta=True, tm=1024, tn=1024, tk=2048)
    dao, dro, dga, dgr = _stage_merge_bwd(proj, dmerged, a_out, r_out)
    gw["w_attn_o"] = _mm("g_w_attn_o", attn_rows, dao, ta=True, tm=512, tn=1024, tk=2048)
    gw["w_ret_o"] = _mm("g_w_ret_o", rz, dro, ta=True, tm=512, tn=1024, tk=2048)
    dattn = _mm("d_attn", dao, w["w_attn_o"], tb=True, tm=1024, tn=512, tk=1024)
    drz = _mm("d_rz", dro, w["w_ret_o"], tb=True, tm=1024, tn=512, tk=1024)
    do_ct, delta, dry, drg, g_ret_p = _stage_mix_post_bwd(dattn, attn_rows, drz, ry_f, ry_b, proj, g_ret)
    dq_f, dk_f, dv_f, dl_f, dq_b, dk_b, dv_b, dl_b = _ret_bwd(logits, rq, rk, proj, dry, st_f, st_b)
    dq_ct, dk8, dv8 = _attn_bwd(q_ct, do_ct, lse, delta, k_rows, v_rows, k_ct)
    dproj, gq_p, gk_p = _stage_dproj(proj, dq_ct, dk8, dv8, (dq_f, dk_f, dv_f, dq_b, dk_b, dv_b), drg, dga, dgr, tabs, gq_w, gk_w)
    gw["w_in"] = _mm("g_w_in", hb, dproj, ta=True, tm=512, tn=IN_W // 2, tk=1024)
    grad_x, g_mix_p = _mm("d_h", dproj, w["w_in"], tb=True, tm=512, tn=1024, tk=IN_W // 2, epi=epi_norm_bwd, epi_ins=(x, dx1),
                          consts=(g_mix,), n_sums=1)

    gs = {
        "mix_norm": g_mix_p[0], "mlp_norm": g_mlp_p[0], "ple_norm": g_ple_p[0], "final_norm": g_final_p[0],
        "ret_norm_gain": g_ret_p[0],
        "attn_q_norm": jnp.sum(gq_p[0].reshape(ATTN_HEADS, ATTN_HEAD_DIM), axis=0),
        "attn_k_norm": jnp.sum(gk_p[0].reshape(ATTN_KV_HEADS, ATTN_HEAD_DIM), axis=0),
        "ret_decay_logit": jnp.stack([dl_f[:, 0, 0], dl_b[:, 0, 0]]),
    }
    return loss_sum, grad_x, gw, gs


PACK_COLS = 1024
N_CHIPS = 4
HALF_ROWS = 2048


def _pack_shard(parts):
    return jnp.concatenate([parts[n].reshape(-1, PACK_COLS) for n, _ in BIG], axis=0)


def _unpack_shard(slab, shapes):
    out, r = {}, 0
    for n, _ in BIG:
        rows = math.prod(shapes[n]) // PACK_COLS
        out[n] = slab[r:r + rows].reshape(shapes[n])
        r += rows
    return out


def _shard_of(full, axis, sidx):
    size = full.shape[axis] // N_CHIPS
    return lax.slice_in_dim(full, sidx * size, (sidx + 1) * size, axis=axis)


def _position():
    x, y, c = lax.axis_index("x"), lax.axis_index("y"), lax.axis_index("c")
    return x, y, c


def _other_chips(x, y):
    return [(1 - x, y), (x, 1 - y), (1 - x, 1 - y)]


ANY = pl.BlockSpec(memory_space=pl.ANY)


def _gather_weights(slab):
    rows = slab.shape[0]
    half = rows // 2

    def body(in_ref, out_ref, send_sems, recv_sems):
        x, y, c = _position()
        chips = _other_chips(x, y)

        def piece(chip, core):
            return out_ref.at[2 * chip[0] + chip[1], pl.ds(core * half, half), :]

        def copy(k, chip, core, to, src=None):
            return pltpu.make_async_remote_copy(
                src_ref=piece(chip, core) if src is None else src, dst_ref=piece(chip, core),
                send_sem=send_sems.at[k], recv_sem=recv_sems.at[k], device_id=to, device_id_type=MESH)

        first = [copy(j, (x, y), c, (*chip, c), src=in_ref.at[pl.ds(c * half, half), :]) for j, chip in enumerate(chips)]
        for cp in first:
            cp.start()
        passed = [copy(3 + j, chip, c, (x, y, 1 - c)) for j, chip in enumerate(chips)]
        for j, chip in enumerate(chips):
            copy(j, chip, c, (x, y, c)).wait_recv()
            passed[j].start()
        for j, chip in enumerate(chips):
            copy(3 + j, chip, 1 - c, (x, y, c)).wait_recv()
        for cp in first + passed:
            cp.wait_send()

    return pl.pallas_call(
        body, name="gather_weights", in_specs=[ANY], out_specs=ANY,
        out_shape=jax.ShapeDtypeStruct((N_CHIPS,) + slab.shape, slab.dtype),
        scratch_shapes=[pltpu.SemaphoreType.DMA((6,)), pltpu.SemaphoreType.DMA((6,))],
    )(slab)


def _exchange_halves(g):
    def body(g_ref, out_ref, send_sem, recv_sem):
        x, y, c = _position()
        cp = pltpu.make_async_remote_copy(src_ref=g_ref.at[1 - c], dst_ref=out_ref, send_sem=send_sem, recv_sem=recv_sem,
                                          device_id=(x, y, 1 - c), device_id_type=MESH)
        cp.start()
        cp.wait()

    return pl.pallas_call(
        body, name="exchange_halves", in_specs=[ANY], out_specs=ANY,
        out_shape=jax.ShapeDtypeStruct(g.shape[1:], g.dtype),
        scratch_shapes=[pltpu.SemaphoreType.DMA, pltpu.SemaphoreType.DMA],
    )(g)


def _add_my_half(g, r1, c_idx):
    tr = 256
    nt = g.shape[2] // tr

    def body(c_ref, g_ref, r_ref, o_ref, ob_ref):
        tot = g_ref[...] + r_ref[...]
        o_ref[...] = tot
        ob_ref[...] = tot.astype(BF16)

    blk = (None, tr, PACK_COLS)
    spec = pl.BlockSpec(blk, lambda s, i, c_ref: (s, i, 0))
    return pl.pallas_call(
        body, name="add_my_half",
        grid_spec=pltpu.PrefetchScalarGridSpec(
            num_scalar_prefetch=1, grid=(N_CHIPS, nt),
            in_specs=[pl.BlockSpec((None,) + blk, lambda s, i, c_ref: (c_ref[0], s, i, 0)), spec],
            out_specs=[spec, spec]),
        out_shape=[jax.ShapeDtypeStruct(g.shape[1:], F32), jax.ShapeDtypeStruct(g.shape[1:], BF16)],
        compiler_params=_cparams(("parallel", "parallel")),
    )(c_idx, g, r1)


def _scatter_to_chips(part):
    def body(p_ref, out_ref, send_sems, recv_sems):
        x, y, c = _position()
        chips = _other_chips(x, y)
        sends = [pltpu.make_async_remote_copy(
            src_ref=p_ref.at[2 * chip[0] + chip[1]], dst_ref=out_ref.at[j], send_sem=send_sems.at[j], recv_sem=recv_sems.at[j],
            device_id=(*chip, c), device_id_type=MESH) for j, chip in enumerate(chips)]
        for cp in sends:
            cp.start()
        for cp in sends:
            cp.wait()

    return pl.pallas_call(
        body, name="scatter_to_chips", in_specs=[ANY], out_specs=ANY,
        out_shape=jax.ShapeDtypeStruct((N_CHIPS - 1,) + part.shape[1:], part.dtype),
        scratch_shapes=[pltpu.SemaphoreType.DMA((3,)), pltpu.SemaphoreType.DMA((3,))],
    )(part)


def _sum_chips(part, r2, chip_idx):
    tr = 256

    def body(c_ref, p_ref, r_ref, o_ref):
        o_ref[...] = ((p_ref[...] + r_ref[0]) + r_ref[1]) + r_ref[2]

    return pl.pallas_call(
        body, name="sum_chips",
        grid_spec=pltpu.PrefetchScalarGridSpec(
            num_scalar_prefetch=1, grid=(r2.shape[1] // tr,),
            in_specs=[pl.BlockSpec((None, tr, PACK_COLS), lambda i, c_ref: (c_ref[0], i, 0)),
                      pl.BlockSpec((N_CHIPS - 1, tr, PACK_COLS), lambda i, c_ref: (0, i, 0))],
            out_specs=pl.BlockSpec((tr, PACK_COLS), lambda i, c_ref: (i, 0))),
        out_shape=jax.ShapeDtypeStruct(r2.shape[1:], F32),
        compiler_params=_cparams(("parallel",)),
    )(chip_idx, part, r2)


def _join_halves(red):
    def body(r_ref, out_ref, send_sem, recv_sem):
        x, y, c = _position()
        cp = pltpu.make_async_remote_copy(src_ref=r_ref, dst_ref=out_ref, send_sem=send_sem, recv_sem=recv_sem,
                                          device_id=(x, y, 1 - c), device_id_type=MESH)
        cp.start()
        cp.wait()

    return pl.pallas_call(
        body, name="join_halves", in_specs=[ANY], out_specs=ANY,
        out_shape=jax.ShapeDtypeStruct(red.shape, red.dtype),
        scratch_shapes=[pltpu.SemaphoreType.DMA, pltpu.SemaphoreType.DMA],
    )(red)


def _adamw_math(w, g, m, v):
    m = ADAM_B1 * m + (1.0 - ADAM_B1) * g
    v = ADAM_B2 * v + (1.0 - ADAM_B2) * (g * g)
    m_hat = m / (1.0 - ADAM_B1 ** ADAM_STEP)
    v_hat = v / (1.0 - ADAM_B2 ** ADAM_STEP)
    delta = -ADAM_LR * (m_hat / (jnp.sqrt(v_hat) + ADAM_EPS) + ADAM_WD * w)
    return delta, m, v


def _adamw(name, w, g, m, v):
    tr = min(256, w.shape[0])

    def body(w_ref, g_ref, m_ref, v_ref, d_ref, nm_ref, nv_ref):
        d_ref[...], nm_ref[...], nv_ref[...] = _adamw_math(w_ref[...], g_ref[...], m_ref[...], v_ref[...])

    blk = pl.BlockSpec((tr, w.shape[1]), lambda i: (i, 0))
    return pl.pallas_call(
        body, name="adamw_" + name, grid=(w.shape[0] // tr,), in_specs=[blk] * 4, out_specs=[blk] * 3,
        out_shape=[jax.ShapeDtypeStruct(w.shape, F32)] * 3, compiler_params=_cparams(("parallel",)),
    )(w, g, m, v)


def _small_step(gpk, wpk, mpk, vpk):
    row, col, width = SMALL["ret_decay_logit"]

    def body(g_ref, w_ref, m_ref, v_ref, og_ref, od_ref, om_ref, ov_ref, gbuf, send_sems, recv_sems):
        x, y, c = _position()
        me = 4 * x + 2 * y + c
        gbuf[me] = g_ref[...]
        sends = []
        for k in range(1, 8):
            to = (x ^ (k >> 2), y ^ ((k >> 1) & 1), c ^ (k & 1))
            cp = pltpu.make_async_remote_copy(src_ref=g_ref, dst_ref=gbuf.at[me], send_sem=send_sems.at[k - 1],
                                              recv_sem=recv_sems.at[k - 1], device_id=to, device_id_type=MESH)
            cp.start()
            sends.append(cp)
        for k in range(1, 8):
            frm = me ^ k
            pltpu.make_async_remote_copy(src_ref=g_ref, dst_ref=gbuf.at[frm], send_sem=send_sems.at[k - 1],
                                         recv_sem=recv_sems.at[k - 1], device_id=(x, y, c), device_id_type=MESH).wait_recv()
        for cp in sends:
            cp.wait_send()
        tot = gbuf[0]
        for d in range(1, 8):
            tot = tot + gbuf[d]
        w = w_ref[...]
        r_i = lax.broadcasted_iota(jnp.int32, w.shape, 0)
        c_i = lax.broadcasted_iota(jnp.int32, w.shape, 1)
        is_logit = (r_i == row) & (c_i >= col) & (c_i < col + width)
        g = jnp.where(is_logit, tot * _sigmoid(-w), tot)
        og_ref[...] = g
        od_ref[...], om_ref[...], ov_ref[...] = _adamw_math(w, g, m_ref[...], v_ref[...])

    vm = pl.BlockSpec(memory_space=pltpu.VMEM)
    shp = jax.ShapeDtypeStruct(gpk.shape, F32)
    return pl.pallas_call(
        body, name="small_step", in_specs=[vm] * 4, out_specs=[vm] * 4, out_shape=[shp] * 4,
        scratch_shapes=[pltpu.VMEM((8,) + gpk.shape, F32), pltpu.SemaphoreType.DMA((7,)), pltpu.SemaphoreType.DMA((7,))],
    )(gpk, wpk, mpk, vpk)


def _pack_small(parts):
    rows = [[] for _ in range(SMALL_ROWS)]
    for n, (r, col, width) in sorted(SMALL.items(), key=lambda kv: (kv[1][0], kv[1][1])):
        rows[r].append((col, parts[n].reshape(-1).astype(F32)))
    out = []
    for r in range(SMALL_ROWS):
        segs, pos = [], 0
        for col, vec in rows[r]:
            assert col == pos
            segs.append(vec)
            pos += vec.shape[0]
        if pos < PACK_COLS:
            segs.append(jnp.zeros((PACK_COLS - pos,), F32))
        out.append(jnp.concatenate(segs))
    return jnp.stack(out)


def _unpack_small(pk, shapes):
    return {n: pk[r, col:col + width].reshape(shapes[n]) for n, (r, col, width) in SMALL.items()}


WEIGHTS = ("mix_norm", "w_in", "attn_q_norm", "attn_k_norm", "ret_decay_logit", "ret_norm_gain", "w_attn_o", "w_ret_o", "w_out",
           "mlp_norm", "w_up", "w_down", "ple_norm", "w_ple_gate", "w_ple", "final_norm")


def kernel(x, p, mix_norm, w_in, attn_q_norm, attn_k_norm, ret_decay_logit, ret_norm_gain, w_attn_o, w_ret_o, w_out, mlp_norm, w_up, w_down, ple_norm, w_ple_gate, w_ple, final_norm, loss_target, m_mix_norm, m_w_in, m_attn_q_norm, m_attn_k_norm, m_ret_decay_logit, m_ret_norm_gain, m_w_attn_o, m_w_ret_o, m_w_out, m_mlp_norm, m_w_up, m_w_down, m_ple_norm, m_w_ple_gate, m_w_ple, m_final_norm, v_mix_norm, v_w_in, v_attn_q_norm, v_attn_k_norm, v_ret_decay_logit, v_ret_norm_gain, v_w_attn_o, v_w_ret_o, v_w_out, v_mlp_norm, v_w_up, v_w_down, v_ple_norm, v_w_ple_gate, v_w_ple, v_final_norm):
    args = dict(locals())
    wts = {n: args[n] for n in WEIGHTS}
    ms = {n: args["m_" + n] for n in WEIGHTS}
    vs = {n: args["v_" + n] for n in WEIGHTS}
    shapes = {n: wts[n].shape for n in WEIGHTS}
    big_names = [n for n, _ in BIG]
    xi, yi, ci = _position()
    c_idx = ci.astype(jnp.int32).reshape(1)

    chip_idx = (2 * xi + yi).astype(jnp.int32)
    slab_b = _pack_shard({n: wts[n][0].astype(BF16) for n in big_names})
    gathered = lax.dynamic_update_slice(_gather_weights(slab_b), slab_b[None], (chip_idx, 0, 0))
    full = {}
    for n, axis in BIG:
        per_chip = [_unpack_shard(gathered[k], {m_: shapes[m_][1:] for m_ in big_names})[n] for k in range(N_CHIPS)]
        full[n] = jnp.concatenate(per_chip, axis=axis)
    small = {n: wts[n].reshape(wts[n].shape[1:] if wts[n].ndim > 1 else wts[n].shape) for n in SMALL}

    loss_part, grad_x, gw, gs = _local_step(x[0], p[0, 0], loss_target[0], full, small)
    loss = lax.psum(loss_part, ("x", "y", "c"))

    slabs = jnp.stack([_pack_shard({n: _shard_of(gw[n], axis, k) for n, axis in BIG}) for k in range(N_CHIPS)])
    halves = slabs.reshape(N_CHIPS, 2, HALF_ROWS, PACK_COLS).transpose(1, 0, 2, 3)
    chip_part, chip_part_b = _add_my_half(halves, _exchange_halves(halves), c_idx)
    mine = _sum_chips(chip_part, _scatter_to_chips(chip_part_b), chip_idx.reshape(1))
    both = jnp.stack([mine, _join_halves(mine)])
    reduced = jnp.where(ci == 0, both, both[::-1]).reshape(2 * HALF_ROWS, PACK_COLS)
    g_big = _unpack_shard(reduced, {n: shapes[n][1:] for n in big_names})
    big_out = [{}, {}, {}, {}]
    for n in big_names:
        big_out[0][n] = g_big[n][None]
        for kind, a in enumerate(_adamw(n, wts[n][0], g_big[n], ms[n][0], vs[n][0])):
            big_out[kind + 1][n] = a[None]

    sm_out = _small_step(_pack_small(gs), _pack_small({n: wts[n] for n in SMALL}), _pack_small({n: ms[n] for n in SMALL}),
                         _pack_small({n: vs[n] for n in SMALL}))
    small_out = [_unpack_small(a, {n: shapes[n] for n in SMALL}) for a in sm_out]

    outs = [loss, grad_x[None]]
    for kind in range(4):
        for n in WEIGHTS:
            outs.append(small_out[kind][n] if n in SMALL else big_out[kind][n])
    return tuple(outs)
```

```python
import functools
import math

import jax
import jax.numpy as jnp
from jax import lax
from jax.experimental import pallas as pl
from jax.experimental.pallas import tpu as pltpu

F32 = jnp.float32
BF16 = jnp.bfloat16
MESH = pl.DeviceIdType.MESH

D_MODEL = 1024
PLE_DIM = 256
GRID_W = 64
ATTN_HEAD_DIM = 64
ATTN_HEADS = 8
ATTN_KV_HEADS = 2
ATTN_GROUP = ATTN_HEADS // ATTN_KV_HEADS
RET_HEAD_DIM = 128
RET_HEADS = 4
ATTN_Q_W = 512
ATTN_KV_W = 128
RET_W = 512
IN_W = 4864
D_FF = 4096
RET_CHUNK = 128
ROPE_THETA = 10000.0
NORM_EPS = 1e-6
GN_EPS = 1e-5
ATTN_SCALE = ATTN_HEAD_DIM ** -0.5
LOG2E = math.log2(math.e)
Q_FOLD = ATTN_SCALE * LOG2E
RET_SCALE = RET_HEAD_DIM ** -0.5

C_AQ, C_AK, C_AV, C_RQ, C_RK, C_RV, C_RG, C_GA, C_GR = 0, 512, 640, 768, 1280, 1792, 2304, 2816, 3840

ADAM_LR = 0.001
ADAM_B1 = 0.9
ADAM_B2 = 0.999
ADAM_EPS = 1e-08
ADAM_WD = 0.01
ADAM_STEP = 10

LANES = 128
VMEM_LIMIT = 56 << 20
SEQ_TILE = 512

BIG = (("w_in", 1), ("w_attn_o", 1), ("w_ret_o", 1), ("w_out", 0), ("w_up", 1), ("w_down", 0), ("w_ple_gate", 0), ("w_ple", 1))
SMALL_ROWS = 8
SMALL = {"mix_norm": (0, 0, 1024), "mlp_norm": (1, 0, 1024), "ple_norm": (2, 0, 1024), "final_norm": (3, 0, 1024),
         "ret_norm_gain": (4, 0, 512), "attn_q_norm": (4, 512, 64), "attn_k_norm": (4, 576, 64), "ret_decay_logit": (4, 640, 8)}


def _seq_tile(s):
    return min(SEQ_TILE, s // 2)


def _cparams(sem=None, vmem=VMEM_LIMIT):
    return pltpu.CompilerParams(dimension_semantics=sem, vmem_limit_bytes=vmem)


def _mm(name, a, b, *, ta=False, tb=False, tm, tn, tk, out_dtypes=(F32,), epi=None, epi_ins=(), consts=(), n_sums=0, j_outer=False):
    if ta:
        kdim, m = a.shape
    else:
        m, kdim = a.shape
    n = b.shape[0] if tb else b.shape[1]
    tm, tn, tk = min(tm, m), min(tn, n), min(tk, kdim)
    assert m % tm == 0 and n % tn == 0 and kdim % tk == 0, (name, m, n, kdim, tm, tn, tk)
    nk = kdim // tk
    n_e, n_c, n_o = len(epi_ins), len(consts), len(out_dtypes)
    assert n_sums == 0 or tn == n

    def body(*refs):
        a_ref, b_ref = refs[0], refs[1]
        e_refs = refs[2:2 + n_e]
        c_refs = refs[2 + n_e:2 + n_e + n_c]
        o_refs = refs[2 + n_e + n_c:2 + n_e + n_c + n_o]
        s_refs = refs[2 + n_e + n_c + n_o:2 + n_e + n_c + n_o + n_sums]
        acc_ref = refs[2 + n_e + n_c + n_o + n_sums] if nk > 1 else None
        k = pl.program_id(2)
        if n_sums:
            @pl.when((pl.program_id(1 if j_outer else 0) == 0) & (k == 0))
            def _():
                for r in s_refs:
                    r[...] = jnp.zeros(r.shape, F32)
        av = a_ref[...].astype(BF16)
        bv = b_ref[...].astype(BF16)
        dims = (((0,) if ta else (1,), (1,) if tb else (0,)), ((), ()))
        part = lax.dot_general(av, bv, dims, preferred_element_type=F32)

        def finish(acc):
            vals = epi(acc, e_refs, c_refs) if epi is not None else (acc,)
            for o_ref, v in zip(o_refs, vals[:n_o]):
                o_ref[...] = v.astype(o_ref.dtype)
            for s_ref, v in zip(s_refs, vals[n_o:]):
                _acc_add(s_ref, v)

        if nk == 1:
            finish(part)
        else:
            @pl.when(k == 0)
            def _():
                acc_ref[...] = part

            @pl.when(k > 0)
            def _():
                acc_ref[...] += part

            @pl.when(k == nk - 1)
            def _():
                finish(acc_ref[...])

    def spec(shape, index):
        return pl.BlockSpec(shape, (lambda j, i, k: index(i, j, k)) if j_outer else index)

    a_spec = spec((tk, tm), lambda i, j, k: (k, i)) if ta else spec((tm, tk), lambda i, j, k: (i, k))
    b_spec = spec((tn, tk), lambda i, j, k: (j, k)) if tb else spec((tk, tn), lambda i, j, k: (k, j))
    o_spec = spec((tm, tn), lambda i, j, k: (i, j))
    c_specs = [spec(c.shape, lambda i, j, k, nd=c.ndim: (0,) * nd) for c in consts]
    outs = pl.pallas_call(
        body, name=name,
        grid=(n // tn, m // tm, nk) if j_outer else (m // tm, n // tn, nk),
        in_specs=[a_spec, b_spec] + [o_spec] * n_e + c_specs,
        out_specs=[o_spec] * n_o + [spec((8, n), lambda i, j, k: (0, 0))] * n_sums,
        out_shape=[jax.ShapeDtypeStruct((m, n), dt) for dt in out_dtypes] + [jax.ShapeDtypeStruct((8, n), F32)] * n_sums,
        scratch_shapes=[pltpu.VMEM((tm, tn), F32)] if nk > 1 else [],
        compiler_params=_cparams(("arbitrary",) * 3 if n_sums else ("parallel", "parallel", "arbitrary")),
    )(a, b, *epi_ins, *consts)
    return outs[0] if n_o + n_sums == 1 else outs


def _rows(arr, tr):
    return (arr, pl.BlockSpec((tr, arr.shape[1]), lambda i: (i, 0)))


def _win(arr, tr, start, width):
    bw = math.gcd(start, width) if start else width
    assert bw % LANES == 0
    return [(arr, pl.BlockSpec((tr, bw), lambda i, cb=start // bw + p: (i, cb))) for p in range(width // bw)]


def _ct(arr):
    return (arr, pl.BlockSpec((None,) + arr.shape[1:], lambda i: (i, 0, 0)))


def _whole(arr):
    return (arr, pl.BlockSpec(arr.shape, lambda i, nd=arr.ndim: (0,) * nd))


def _cat(refs):
    vals = [r[...].astype(F32) for r in refs]
    return vals[0] if len(vals) == 1 else jnp.concatenate(vals, axis=1)


def _seqtiled(name, fn, n_tiles, ins, outs, acc_widths=()):
    n_i, n_o, n_a = len(ins), len(outs), len(acc_widths)

    def body(*refs):
        i_refs, o_refs, a_refs = refs[:n_i], refs[n_i:n_i + n_o], refs[n_i + n_o:]
        if n_a:
            @pl.when(pl.program_id(0) == 0)
            def _():
                for r in a_refs:
                    r[...] = jnp.zeros(r.shape, F32)
        fn(list(i_refs), list(o_refs), list(a_refs))

    res = pl.pallas_call(
        body, name=name, grid=(n_tiles,),
        in_specs=[s for _, s in ins],
        out_specs=[s for _, _, s in outs] + [pl.BlockSpec((8, w), lambda i: (0, 0)) for w in acc_widths],
        out_shape=[jax.ShapeDtypeStruct(sh, dt) for sh, dt, _ in outs] + [jax.ShapeDtypeStruct((8, w), F32) for w in acc_widths],
        compiler_params=_cparams(("arbitrary",)),
    )(*[a for a, _ in ins])
    return res


def _acc_add(acc_ref, val):
    acc_ref[0:1, :] += jnp.sum(val, axis=0, keepdims=True)


def _out_rows(s, w, dt, tr):
    return ((s, w), dt, pl.BlockSpec((tr, w), lambda i: (i, 0)))


def _out_ct(s, w, dt, t):
    return ((s // t, w, t), dt, pl.BlockSpec((None, w, t), lambda i: (i, 0, 0)))


def _rms_fwd(x, gain):
    r = lax.rsqrt(jnp.mean(x * x, axis=-1, keepdims=True) + NORM_EPS)
    return x * r * gain


def _rms_bwd(dy, x, gain):
    r = lax.rsqrt(jnp.mean(x * x, axis=-1, keepdims=True) + NORM_EPS)
    xn = x * r
    dyg = dy * gain
    dx = r * (dyg - xn * jnp.mean(dyg * xn, axis=-1, keepdims=True))
    return dx, dy * xn


def _seg_mean(y, hd):
    w = y.shape[1]
    pieces = []
    for s in range(0, w, LANES):
        v = y[:, s:s + LANES]
        tot = jnp.sum(v, axis=1, keepdims=True)
        if hd == LANES:
            pieces.append(jnp.broadcast_to(tot, v.shape))
        else:
            low = lax.broadcasted_iota(jnp.int32, v.shape, 1) < hd
            lo = jnp.sum(jnp.where(low, v, 0.0), axis=1, keepdims=True)
            pieces.append(jnp.where(low, lo, tot - lo))
    out = pieces[0] if len(pieces) == 1 else jnp.concatenate(pieces, axis=1)
    return out * (1.0 / hd)


def _tile_lanes(t, w):
    return t if w == t.shape[1] else jnp.concatenate([t] * (w // t.shape[1]), axis=1)


def _swap_halves(x, hd):
    w = x.shape[1]
    half = hd // 2
    lane = lax.broadcasted_iota(jnp.int32, x.shape, 1)
    return jnp.where((lane % hd) < half, pltpu.roll(x, w - half, 1), pltpu.roll(x, half, 1))


def _rope(x, cos, sin_signed, hd):
    w = x.shape[1]
    return x * _tile_lanes(cos, w) + _swap_halves(x, hd) * _tile_lanes(sin_signed, w)


def _rope_t(dy, cos, sin_signed, hd):
    w = dy.shape[1]
    return dy * _tile_lanes(cos, w) + _swap_halves(dy * _tile_lanes(sin_signed, w), hd)


def _headnorm_fwd(x, gain_w, hd):
    r = lax.rsqrt(_seg_mean(x * x, hd) + NORM_EPS)
    return x * r * gain_w


def _headnorm_bwd(dy, x, gain_w, hd):
    r = lax.rsqrt(_seg_mean(x * x, hd) + NORM_EPS)
    xn = x * r
    dyg = dy * gain_w
    return r * (dyg - xn * _seg_mean(dyg * xn, hd)), dy * xn


def _sigmoid(x):
    return 1.0 / (1.0 + jnp.exp(-x))


def _rope_tables(seq_len, head_dim):
    rows = seq_len // GRID_W
    n_axis = head_dim // 4
    freqs = ROPE_THETA ** (-jnp.arange(n_axis, dtype=F32) / n_axis)
    ang_r = jnp.arange(rows, dtype=F32)[:, None] * freqs
    ang_c = jnp.arange(GRID_W, dtype=F32)[:, None] * freqs

    def expand(by_row, by_col):
        r = jnp.broadcast_to(by_row[:, None, :], (rows, GRID_W, n_axis))
        c = jnp.broadcast_to(by_col[None, :, :], (rows, GRID_W, n_axis))
        return jnp.concatenate([r, c], axis=-1).reshape(seq_len, 2 * n_axis)

    cos, sin = expand(jnp.cos(ang_r), jnp.cos(ang_c)), expand(jnp.sin(ang_r), jnp.sin(ang_c))
    reps = LANES // head_dim
    return jnp.tile(jnp.concatenate([cos, cos], axis=-1), (1, reps)), jnp.tile(jnp.concatenate([-sin, sin], axis=-1), (1, reps))


def _stage_norm_in(x, gain):
    s = x.shape[0]
    tr = min(SEQ_TILE, s)

    def fn(i, o, a):
        o[0][...] = _rms_fwd(i[0][...], i[1][...]).astype(BF16)

    return _seqtiled("norm_in", fn, s // tr, [_rows(x, tr), _whole(gain)], [_out_rows(s, D_MODEL, BF16, tr)])[0]


def _stage_qkv(proj, tabs, gq_w, gk_w):
    s = proj.shape[0]
    t = _seq_tile(s)
    ca, sa, cr, sr = tabs
    ins = (_win(proj, t, C_AQ, ATTN_Q_W) + _win(proj, t, C_AK, ATTN_KV_W) + _win(proj, t, C_AV, ATTN_KV_W)
           + _win(proj, t, C_RQ, RET_W) + _win(proj, t, C_RK, RET_W)
           + [_rows(ca, t), _rows(sa, t), _rows(cr, t), _rows(sr, t), _whole(gq_w), _whole(gk_w)])

    def fn(i, o, a):
        aq, ak, av = (i[n][...].astype(F32) for n in range(3))
        rq, rk = _cat(i[3:5]), _cat(i[5:7])
        ca_, sa_, cr_, sr_ = i[7][...], i[8][...], i[9][...], i[10][...]
        qr = _rope(_headnorm_fwd(aq, i[11][...], ATTN_HEAD_DIM), ca_, sa_, ATTN_HEAD_DIM) * Q_FOLD
        kr = _rope(_headnorm_fwd(ak, i[12][...], ATTN_HEAD_DIM), ca_, sa_, ATTN_HEAD_DIM)
        qt = qr.T.astype(BF16)
        zeros = jnp.zeros((ATTN_HEAD_DIM, t), BF16)
        for h in range(ATTN_HEADS):
            g = h // ATTN_GROUP
            blk = qt[h * ATTN_HEAD_DIM:(h + 1) * ATTN_HEAD_DIM, :]
            o[0][h * LANES + g * ATTN_HEAD_DIM:h * LANES + (g + 1) * ATTN_HEAD_DIM, :] = blk
            o[0][h * LANES + (1 - g) * ATTN_HEAD_DIM:h * LANES + (2 - g) * ATTN_HEAD_DIM, :] = zeros
        o[1][...] = kr.astype(BF16)
        o[2][...] = kr.T.astype(BF16)
        o[3][...] = av.astype(BF16)
        o[4][...] = av.T.astype(BF16)
        o[5][...] = _rope(rq, cr_, sr_, RET_HEAD_DIM) * RET_SCALE
        o[6][...] = _rope(rk, cr_, sr_, RET_HEAD_DIM)

    outs = [_out_ct(s, ATTN_HEADS * LANES, BF16, t), _out_rows(s, ATTN_KV_W, BF16, t), _out_ct(s, ATTN_KV_W, BF16, t),
            _out_rows(s, ATTN_KV_W, BF16, t), _out_ct(s, ATTN_KV_W, BF16, t), _out_rows(s, RET_W, F32, t), _out_rows(s, RET_W, F32, t)]
    return _seqtiled("qkv_prep", fn, s // t, ins, outs)


def _groupnorm_gate(ry, rg, gain):
    mu = _seg_mean(ry, RET_HEAD_DIM)
    d = ry - mu
    rs = lax.rsqrt(_seg_mean(d * d, RET_HEAD_DIM) + GN_EPS)
    return d * rs, rs, _sigmoid(rg)


def _stage_mix_post(ry_f, ry_b, proj, o_ct, gain):
    s = proj.shape[0]
    t = _seq_tile(s)
    ins = [_rows(ry_f, t), _rows(ry_b, t)] + _win(proj, t, C_RG, RET_W) + [_ct(o_ct), _whole(gain)]

    def fn(i, o, a):
        ry = i[0][...] + i[1][...]
        rg = _cat(i[2:4])
        gn, _, sg = _groupnorm_gate(ry, rg, None)
        o[0][...] = (gn * i[5][...] * (rg * sg)).astype(BF16)
        o[1][...] = i[4][...].astype(F32).T.astype(BF16)

    return _seqtiled("mix_post", fn, s // t, ins, [_out_rows(s, RET_W, BF16, t), _out_rows(s, ATTN_Q_W, BF16, t)])


def _stage_merge(proj, a_out, r_out):
    s = proj.shape[0]
    tr = min(SEQ_TILE, s)
    ins = _win(proj, tr, C_GA, D_MODEL) + _win(proj, tr, C_GR, D_MODEL) + [_rows(a_out, tr), _rows(r_out, tr)]
    na = len(_win(proj, tr, C_GA, D_MODEL))

    def fn(i, o, a):
        ga, gr = _cat(i[:na]), _cat(i[na:2 * na])
        o[0][...] = (_sigmoid(ga) * i[2 * na][...] + _sigmoid(gr) * i[2 * na + 1][...]).astype(BF16)

    return _seqtiled("merge", fn, s // tr, ins, [_out_rows(s, D_MODEL, BF16, tr)])[0]


def _stage_head(zg, pe, x2, target, g_final):
    s = x2.shape[0]
    tr = min(SEQ_TILE // 2, s)
    ins = [_rows(zg, tr), _rows(pe, tr), _rows(x2, tr), _rows(target, tr), _whole(g_final)]

    def fn(i, o, a):
        gt = _sigmoid(i[0][...])
        pe_ = i[1][...]
        x3 = i[2][...] + gt * pe_
        gf = i[4][...]
        r3 = lax.rsqrt(jnp.mean(x3 * x3, axis=-1, keepdims=True) + NORM_EPS)
        x3n = x3 * r3
        e = x3n * gf - i[3][...]
        _acc_add(a[0], e * e)
        dy = e * (1.0 / D_MODEL)
        _acc_add(a[1], dy * x3n)
        dyg = dy * gf
        dx3 = r3 * (dyg - x3n * jnp.mean(dyg * x3n, axis=-1, keepdims=True))
        o[0][...] = dx3
        o[1][...] = (dx3 * pe_ * gt * (1.0 - gt)).astype(BF16)
        o[2][...] = (dx3 * gt).astype(BF16)

    outs = [_out_rows(s, D_MODEL, F32, tr), _out_rows(s, D_MODEL, BF16, tr), _out_rows(s, D_MODEL, BF16, tr)]
    return _seqtiled("head", fn, s // tr, ins, outs, acc_widths=(D_MODEL, D_MODEL))


def _stage_merge_bwd(proj, dmerged, a_out, r_out):
    s = proj.shape[0]
    tr = min(SEQ_TILE // 2, s)
    wins = _win(proj, tr, C_GA, D_MODEL)
    na = len(wins)
    ins = wins + _win(proj, tr, C_GR, D_MODEL) + [_rows(dmerged, tr), _rows(a_out, tr), _rows(r_out, tr)]

    def fn(i, o, a):
        sa, sr = _sigmoid(_cat(i[:na])), _sigmoid(_cat(i[na:2 * na]))
        dm = i[2 * na][...]
        o[0][...] = (dm * sa).astype(BF16)
        o[1][...] = (dm * sr).astype(BF16)
        o[2][...] = (dm * i[2 * na + 1][...] * sa * (1.0 - sa)).astype(BF16)
        o[3][...] = (dm * i[2 * na + 2][...] * sr * (1.0 - sr)).astype(BF16)

    return _seqtiled("merge_bwd", fn, s // tr, ins, [_out_rows(s, D_MODEL, BF16, tr)] * 4)


def _stage_mix_post_bwd(dattn, attn_rows, drz, ry_f, ry_b, proj, gain):
    s = proj.shape[0]
    t = _seq_tile(s)
    ins = ([_rows(dattn, t), _rows(attn_rows, t), _rows(drz, t), _rows(ry_f, t), _rows(ry_b, t)]
           + _win(proj, t, C_RG, RET_W) + [_whole(gain)])

    def fn(i, o, a):
        da = i[0][...]
        dat = da.T
        prod_t = (da * i[1][...].astype(F32)).T
        dat_b = dat.astype(BF16)
        zeros = jnp.zeros((ATTN_HEAD_DIM, t), BF16)
        for h in range(ATTN_HEADS):
            g = h // ATTN_GROUP
            o[0][h * LANES + g * ATTN_HEAD_DIM:h * LANES + (g + 1) * ATTN_HEAD_DIM, :] = dat_b[h * ATTN_HEAD_DIM:(h + 1) * ATTN_HEAD_DIM, :]
            o[0][h * LANES + (1 - g) * ATTN_HEAD_DIM:h * LANES + (2 - g) * ATTN_HEAD_DIM, :] = zeros
            o[1][h] = jnp.sum(prod_t[h * ATTN_HEAD_DIM:(h + 1) * ATTN_HEAD_DIM, :], axis=0, keepdims=True)
        ry = i[3][...] + i[4][...]
        rg = _cat(i[5:7])
        gain_ = i[7][...]
        gn, rs, sg = _groupnorm_gate(ry, rg, None)
        dz = i[2][...]
        silu = rg * sg
        _acc_add(a[0], dz * gn * silu)
        dgn = dz * gain_ * silu
        o[2][...] = rs * (dgn - _seg_mean(dgn, RET_HEAD_DIM) - gn * _seg_mean(dgn * gn, RET_HEAD_DIM))
        o[3][...] = (dz * gn * gain_ * (sg * (1.0 + rg * (1.0 - sg)))).astype(BF16)

    outs = [_out_ct(s, ATTN_HEADS * LANES, BF16, t),
            ((ATTN_HEADS, s // t, 1, t), F32, pl.BlockSpec((ATTN_HEADS, None, 1, t), lambda i: (0, i, 0, 0))),
            _out_rows(s, RET_W, F32, t), _out_rows(s, RET_W, BF16, t)]
    return _seqtiled("mix_post_bwd", fn, s // t, ins, outs, acc_widths=(RET_W,))


def _stage_dproj(proj, dq_ct, dk8, dv8, rgrads, drg, dga, dgr, tabs, gq_w, gk_w):
    s = proj.shape[0]
    t = _seq_tile(s)
    ca, sa, cr, sr = tabs
    kv8 = pl.BlockSpec((ATTN_HEADS, t, ATTN_KV_W), lambda i: (0, i, 0))
    ins = (_win(proj, t, C_AQ, ATTN_Q_W) + _win(proj, t, C_AK, ATTN_KV_W) + [_ct(dq_ct), (dk8, kv8), (dv8, kv8)]
           + [_rows(g, t) for g in rgrads] + [_rows(drg, t), _rows(dga, t), _rows(dgr, t)]
           + [_rows(ca, t), _rows(sa, t), _rows(cr, t), _rows(sr, t), _whole(gq_w), _whole(gk_w)])

    def fn(i, o, a):
        aq, ak = i[0][...].astype(F32), i[1][...].astype(F32)
        dq_f, dk_f, dv_f, dq_b, dk_b, dv_b = (r[...].astype(F32) for r in i[5:11])
        ca_, sa_, cr_, sr_ = i[14][...], i[15][...], i[16][...], i[17][...]
        dqn = _rope_t(i[2][...].T * ATTN_SCALE, ca_, sa_, ATTN_HEAD_DIM)
        daq, gq_rows = _headnorm_bwd(dqn, aq, i[18][...], ATTN_HEAD_DIM)
        dkn = _rope_t(jnp.sum(i[3][...].astype(F32), axis=0) * (1.0 / LOG2E), ca_, sa_, ATTN_HEAD_DIM)
        dak, gk_rows = _headnorm_bwd(dkn, ak, i[19][...], ATTN_HEAD_DIM)
        _acc_add(a[0], gq_rows)
        _acc_add(a[1], gk_rows)
        out = o[0]
        out[:, C_AQ:C_AQ + ATTN_Q_W] = daq.astype(BF16)
        out[:, C_AK:C_AK + ATTN_KV_W] = dak.astype(BF16)
        out[:, C_AV:C_AV + ATTN_KV_W] = jnp.sum(i[4][...].astype(F32), axis=0).astype(BF16)
        out[:, C_RQ:C_RQ + RET_W] = _rope_t((dq_f + dq_b) * RET_SCALE, cr_, sr_, RET_HEAD_DIM).astype(BF16)
        out[:, C_RK:C_RK + RET_W] = _rope_t(dk_f + dk_b, cr_, sr_, RET_HEAD_DIM).astype(BF16)
        out[:, C_RV:C_RV + RET_W] = (dv_f + dv_b).astype(BF16)
        out[:, C_RG:C_RG + RET_W] = i[11][...]
        out[:, C_GA:C_GA + D_MODEL] = i[12][...]
        out[:, C_GR:C_GR + D_MODEL] = i[13][...]

    return _seqtiled("dproj", fn, s // t, ins, [_out_rows(s, IN_W, BF16, t)], acc_widths=(ATTN_Q_W, ATTN_KV_W))


def _attn_fwd(q_ct, k_rows, v_ct):
    nq, _, t = q_ct.shape
    s = nq * t
    nk = nq
    assert nk % 2 == 0
    n_par = 2

    def body(q_ref, k_ref, v_ref, o_ref, lse_ref, *bufs):
        sbuf = (bufs[0:2], bufs[2:4])
        pbuf = (bufs[4:6], bufs[6:8])

        def scores(w, j, slot):
            kj = k_ref[pl.ds(pl.multiple_of(j * t, t), t), :]
            st = jnp.dot(kj, q_ref[w], preferred_element_type=F32)
            sbuf[w][slot][...] = st
            return jnp.max(st, axis=0, keepdims=True)

        def probs(w, slot, cmax, m, l):
            m_new = jnp.maximum(m, cmax)
            alpha = jnp.exp2(m - m_new)
            pt = jnp.exp2(sbuf[w][slot][...] - m_new)
            pbuf[w][slot][...] = pt.astype(BF16)
            return m_new, alpha * l + jnp.sum(pt, axis=0, keepdims=True), alpha

        def values(w, j, slot, alpha, acc):
            return alpha * acc + jnp.dot(v_ref[j], pbuf[w][slot][...], preferred_element_type=F32)

        init = []
        for w in range(n_par):
            m = jnp.full((1, t), -1e30, F32)
            l = jnp.zeros((1, t), F32)
            cmax0 = scores(w, 0, 0)
            cmax1 = scores(w, 1, 1)
            m, l, alpha0 = probs(w, 0, cmax0, m, l)
            init.append((m, l, jnp.zeros((ATTN_HEAD_DIM, t), F32), cmax1, alpha0))

        def trip(n, carry):
            c = 2 * n
            out = []
            for w in range(n_par):
                m, l, acc, cmax_b, alpha_c = carry[w]
                acc = values(w, c, 0, alpha_c, acc)
                m, l, alpha1 = probs(w, 1, cmax_b, m, l)
                cmax2 = scores(w, c + 2, 0)
                acc = values(w, c + 1, 1, alpha1, acc)
                m, l, alpha2 = probs(w, 0, cmax2, m, l)
                cmax3 = scores(w, c + 3, 1)
                out.append((m, l, acc, cmax3, alpha2))
            return tuple(out)

        res = lax.fori_loop(0, nk // 2 - 1, trip, tuple(init))
        for w in range(n_par):
            m, l, acc, cmax_b, alpha_c = res[w]
            acc = values(w, nk - 2, 0, alpha_c, acc)
            m, l, alpha1 = probs(w, 1, cmax_b, m, l)
            acc = values(w, nk - 1, 1, alpha1, acc)
            o_ref[w] = (acc / l).astype(BF16)
            lse_ref[w] = m + jnp.log2(l)

    return pl.pallas_call(
        body, name="attn_fwd", grid=(ATTN_HEADS, nq // n_par),
        in_specs=[pl.BlockSpec((n_par, LANES, t), lambda h, i: (i, h, 0)),
                  pl.BlockSpec((s, ATTN_KV_W), lambda h, i: (0, 0)),
                  pl.BlockSpec((nk, ATTN_HEAD_DIM, t), lambda h, i: (0, h // ATTN_GROUP, 0))],
        out_specs=[pl.BlockSpec((n_par, ATTN_HEAD_DIM, t), lambda h, i: (i, h, 0)),
                   pl.BlockSpec((None, n_par, 1, t), lambda h, i: (h, i, 0, 0))],
        out_shape=[jax.ShapeDtypeStruct((nq, ATTN_Q_W, t), BF16), jax.ShapeDtypeStruct((ATTN_HEADS, nq, 1, t), F32)],
        scratch_shapes=[pltpu.VMEM((t, t), F32)] * (2 * n_par) + [pltpu.VMEM((t, t), BF16)] * (2 * n_par),
        compiler_params=_cparams(("parallel", "parallel")),
    )(q_ct, k_rows, v_ct)


def _attn_bwd(q_ct, do_ct, lse, delta, k_rows, v_rows, k_ct):
    nq, _, t = q_ct.shape
    s = nq * t
    nk = nq

    assert nq % 2 == 0

    def body(q_ref, do_ref, lse_ref, delta_ref, k_ref, v_ref, kt_ref, dq_ref, dk_ref, dv_ref, dk_acc, dv_acc,
             sb0, sb1, db0, db1, pb0, pb1, gb0, gb1):
        j = pl.program_id(1)
        sb, db, pb, gb = (sb0, sb1), (db0, db1), (pb0, pb1), (gb0, gb1)

        @pl.when(j == 0)
        def _():
            dq_ref[...] = jnp.zeros(dq_ref.shape, F32)

        kj, vj, ktj = k_ref[...], v_ref[...], kt_ref[...]
        dk_acc[...] = jnp.zeros(dk_acc.shape, F32)
        dv_acc[...] = jnp.zeros(dv_acc.shape, F32)

        def products(i, slot):
            sb[slot][...] = jnp.dot(kj, q_ref[i], preferred_element_type=F32)
            db[slot][...] = jnp.dot(vj, do_ref[i], preferred_element_type=F32)

        def cotangents(i, slot):
            pt = jnp.exp2(sb[slot][...] - lse_ref[i])
            pb[slot][...] = pt.astype(BF16)
            gb[slot][...] = (pt * (db[slot][...] - delta_ref[i])).astype(BF16)

        def accumulate(i, slot):
            dst = gb[slot][...]
            dv_acc[...] += _nt(pb[slot][...], do_ref[i])
            dk_acc[...] += _nt(dst, q_ref[i])
            dq_ref[i] += jnp.dot(ktj, dst, preferred_element_type=F32)

        products(0, 0)
        products(1, 1)
        cotangents(0, 0)

        def trip(n, carry):
            c = 2 * n
            accumulate(c, 0)
            cotangents(c + 1, 1)
            products(c + 2, 0)
            accumulate(c + 1, 1)
            cotangents(c + 2, 0)
            products(c + 3, 1)
            return carry

        lax.fori_loop(0, nq // 2 - 1, trip, 0)
        accumulate(nq - 2, 0)
        cotangents(nq - 1, 1)
        accumulate(nq - 1, 1)
        dk_ref[...] = dk_acc[...].astype(dk_ref.dtype)
        dv_ref[...] = dv_acc[...].astype(dv_ref.dtype)

    per_head = pl.BlockSpec((nq, LANES, t), lambda h, j: (0, h, 0))
    stat = pl.BlockSpec((None, nq, 1, t), lambda h, j: (h, 0, 0, 0))
    kv_rows = pl.BlockSpec((t, ATTN_KV_W), lambda h, j: (j, 0))
    kv_out = pl.BlockSpec((None, t, ATTN_KV_W), lambda h, j: (h, j, 0))
    return pl.pallas_call(
        body, name="attn_bwd", grid=(ATTN_HEADS, nk),
        in_specs=[per_head, per_head, stat, stat, kv_rows, kv_rows,
                  pl.BlockSpec((None, ATTN_HEAD_DIM, t), lambda h, j: (j, h // ATTN_GROUP, 0))],
        out_specs=[pl.BlockSpec((nq, ATTN_HEAD_DIM, t), lambda h, j: (0, h, 0)), kv_out, kv_out],
        out_shape=[jax.ShapeDtypeStruct((nq, ATTN_Q_W, t), F32), jax.ShapeDtypeStruct((ATTN_HEADS, s, ATTN_KV_W), BF16),
                   jax.ShapeDtypeStruct((ATTN_HEADS, s, ATTN_KV_W), BF16)],
        scratch_shapes=([pltpu.VMEM((t, ATTN_KV_W), F32)] * 2 + [pltpu.VMEM((t, t), F32)] * 4 + [pltpu.VMEM((t, t), BF16)] * 4),
        compiler_params=_cparams(("parallel", "arbitrary")),
    )(q_ct, do_ct, lse, delta, k_rows, v_rows, k_ct)


def _log_sigmoid(x):
    t = jnp.exp(-jnp.abs(x))
    log1p_t = jnp.where(t < 1e-2, t * (1.0 - t * (0.5 - t * (1.0 / 3.0))), jnp.log(1.0 + t))
    return jnp.minimum(x, 0.0) - log1p_t


def _decay_tables(logit, backward):
    c = RET_CHUNK
    lam = _log_sigmoid(jnp.full((c, c), logit, F32))
    ii = lax.broadcasted_iota(jnp.int32, (c, c), 0).astype(F32)
    jj = lax.broadcasted_iota(jnp.int32, (c, c), 1).astype(F32)
    if not backward:
        dist, dist_t = jnp.maximum(ii - jj, 0.0), jnp.maximum(jj - ii, 0.0)
        mask, mask_t = ii >= jj, jj >= ii
        e_q, e_k = ii + 1.0, (c - 1.0) - ii
    else:
        dist, dist_t = jnp.maximum(jj - ii, 0.0), jnp.maximum(ii - jj, 0.0)
        mask, mask_t = jj > ii, ii > jj
        e_q, e_k = c - ii, ii
    return dict(
        d=jnp.where(mask, jnp.exp(lam * dist), 0.0), d_t=jnp.where(mask_t, jnp.exp(lam * dist_t), 0.0), dist=dist,
        qdec=jnp.exp(lam * e_q), kdec=jnp.exp(lam * e_k), e_q=e_q, e_k=e_k, gam=jnp.exp(lam * c))


def _nt(a, b):
    return lax.dot_general(a, b, (((1,), (1,)), ((), ())), preferred_element_type=F32)


RET_SUB = 2


def _ret_fwd(logits, q, k, proj):
    s = q.shape[0]
    c = RET_CHUNK
    sub = RET_SUB
    nb = s // (c * sub)
    block = (lambda n: n, lambda n: nb - 1 - n)
    order = (tuple(range(sub)), tuple(reversed(range(sub))))
    vwin = _win(proj, c * sub, C_RV, RET_W)
    nv = len(vwin)
    vw = RET_W // nv
    per = 2 + nv

    def body(lg_ref, *refs):
        ins, outs, states = refs[:2 * per], refs[2 * per:2 * per + 4], refs[2 * per + 4:]

        @pl.when(pl.program_id(0) == 0)
        def _():
            for st in states:
                st[...] = jnp.zeros(st.shape, F32)

        for h in range(RET_HEADS):
            for d in range(2):
                q_ref, k_ref, v_refs = ins[d * per], ins[d * per + 1], ins[d * per + 2:(d + 1) * per]
                y_ref, st_ref, state = outs[2 * d], outs[2 * d + 1], states[d]
                tb = _decay_tables(lg_ref[d, h], bool(d))
                sl = slice(h * RET_HEAD_DIM, (h + 1) * RET_HEAD_DIM)
                off = h * RET_HEAD_DIM
                sh = state[h]
                for u in order[d]:
                    rows = slice(u * c, (u + 1) * c)
                    qh, kh = q_ref[rows, sl], k_ref[rows, sl]
                    vb = v_refs[off // vw][rows, off % vw:off % vw + RET_HEAD_DIM].astype(BF16)
                    a = _nt(qh.astype(BF16), kh.astype(BF16)) * tb["d"]
                    st_ref[u, h] = sh
                    y_ref[rows, sl] = (jnp.dot(a.astype(BF16), vb, preferred_element_type=F32)
                                       + jnp.dot((qh * tb["qdec"]).astype(BF16), sh.astype(BF16), preferred_element_type=F32))
                    sh = tb["gam"] * sh + jnp.dot((kh * tb["kdec"]).T.astype(BF16), vb, preferred_element_type=F32)
                state[h] = sh

    hmat = (RET_HEADS, RET_HEAD_DIM, RET_HEAD_DIM)
    in_specs, out_specs, args = [pl.BlockSpec(memory_space=pltpu.SMEM)], [], [logits]
    for d in range(2):
        rows = pl.BlockSpec((c * sub, RET_W), lambda n, d=d: (block[d](n), 0))
        in_specs += [rows, rows] + [pl.BlockSpec(sp.block_shape, lambda n, d=d, cb=sp.index_map(0)[1]: (block[d](n), cb)) for _, sp in vwin]
        args += [q, k] + [a for a, _ in vwin]
        out_specs += [rows, pl.BlockSpec((sub,) + hmat, lambda n, d=d: (block[d](n), 0, 0, 0))]
    return pl.pallas_call(
        body, name="ret_fwd", grid=(nb,), in_specs=in_specs, out_specs=out_specs,
        out_shape=[jax.ShapeDtypeStruct((s, RET_W), F32), jax.ShapeDtypeStruct((nb * sub,) + hmat, F32)] * 2,
        scratch_shapes=[pltpu.VMEM(hmat, F32)] * 2,
        compiler_params=_cparams(("arbitrary",)),
    )(*args)


def _ret_bwd(logits, q, k, proj, dy, st_f, st_b):
    s = q.shape[0]
    c = RET_CHUNK
    sub = RET_SUB
    nb = s // (c * sub)
    block = (lambda n: nb - 1 - n, lambda n: n)
    order = (tuple(reversed(range(sub))), tuple(range(sub)))
    vwin = _win(proj, c * sub, C_RV, RET_W)
    nv = len(vwin)
    vw = RET_W // nv
    per = 4 + nv

    def body(lg_ref, *refs):
        ins, outs, scr = refs[:2 * per], refs[2 * per:2 * per + 8], refs[2 * per + 8:]
        n = pl.program_id(0)

        @pl.when(n == 0)
        def _():
            for r in scr:
                r[...] = jnp.zeros(r.shape, F32)

        for h in range(RET_HEADS):
            for d in range(2):
                q_ref, k_ref, dy_ref, st_ref = ins[d * per:d * per + 4]
                v_refs = ins[d * per + 4:(d + 1) * per]
                dq_ref, dk_ref, dv_ref = outs[4 * d:4 * d + 3]
                dstate, lacc = scr[2 * d], scr[2 * d + 1]
                tb = _decay_tables(lg_ref[d, h], bool(d))
                sl = slice(h * RET_HEAD_DIM, (h + 1) * RET_HEAD_DIM)
                off = h * RET_HEAD_DIM
                dsh = dstate[h]
                lsum = lacc[h]
                for u in order[d]:
                    rows = slice(u * c, (u + 1) * c)
                    qh, kh, dyh = q_ref[rows, sl], k_ref[rows, sl], dy_ref[rows, sl]
                    vb = v_refs[off // vw][rows, off % vw:off % vw + RET_HEAD_DIM].astype(BF16)
                    qb, kb, dyb = qh.astype(BF16), kh.astype(BF16), dyh.astype(BF16)
                    sh = st_ref[u, h]
                    shb, dshb = sh.astype(BF16), dsh.astype(BF16)
                    qk = _nt(qb, kb)
                    g = _nt(dyb, vb) * tb["d"]
                    a_t = _nt(kb, qb) * tb["d_t"]
                    g_t = _nt(vb, dyb) * tb["d_t"]
                    qd, kd = qh * tb["qdec"], kh * tb["kdec"]
                    dqd = _nt(dyb, shb)
                    dkd = _nt(vb, dshb)
                    dq_ref[rows, sl] = (jnp.dot(g.astype(BF16), kb, preferred_element_type=F32) + dqd * tb["qdec"]).astype(dq_ref.dtype)
                    dk_ref[rows, sl] = (jnp.dot(g_t.astype(BF16), qb, preferred_element_type=F32) + dkd * tb["kdec"]).astype(dk_ref.dtype)
                    dv_ref[rows, sl] = (jnp.dot(a_t.astype(BF16), dyb, preferred_element_type=F32)
                                        + jnp.dot(kd.astype(BF16), dshb, preferred_element_type=F32)).astype(dv_ref.dtype)
                    lsum = lsum + (tb["dist"] * qk * g + tb["e_q"] * qd * dqd + tb["e_k"] * kd * dkd
                                   + float(c) * tb["gam"] * dsh * sh)
                    dsh = tb["gam"] * dsh + jnp.dot(qd.T.astype(BF16), dyb, preferred_element_type=F32)
                dstate[h] = dsh
                lacc[h] = lsum

        @pl.when(n == nb - 1)
        def _():
            for d in range(2):
                for h in range(RET_HEADS):
                    outs[4 * d + 3][h] = jnp.zeros((8, LANES), F32) + jnp.sum(scr[2 * d + 1][h])

    hmat = (RET_HEADS, RET_HEAD_DIM, RET_HEAD_DIM)
    in_specs, out_specs, args = [pl.BlockSpec(memory_space=pltpu.SMEM)], [], [logits]
    for d, states in enumerate((st_f, st_b)):
        rows = pl.BlockSpec((c * sub, RET_W), lambda n, d=d: (block[d](n), 0))
        in_specs += ([rows, rows, rows, pl.BlockSpec((sub,) + hmat, lambda n, d=d: (block[d](n), 0, 0, 0))]
                     + [pl.BlockSpec(sp.block_shape, lambda n, d=d, cb=sp.index_map(0)[1]: (block[d](n), cb)) for _, sp in vwin])
        args += [q, k, dy, states] + [a for a, _ in vwin]
        out_specs += [rows, rows, rows, pl.BlockSpec((RET_HEADS, 8, LANES), lambda n: (0, 0, 0))]
    return pl.pallas_call(
        body, name="ret_bwd", grid=(nb,), in_specs=in_specs, out_specs=out_specs,
        out_shape=([jax.ShapeDtypeStruct((s, RET_W), BF16)] * 3 + [jax.ShapeDtypeStruct((RET_HEADS, 8, LANES), F32)]) * 2,
        scratch_shapes=[pltpu.VMEM(hmat, F32)] * 4,
        compiler_params=_cparams(("arbitrary",)),
    )(*args)


def _local_step(x, p, target, w, small):
    s = x.shape[0]
    tabs = _rope_tables(s, ATTN_HEAD_DIM) + _rope_tables(s, RET_HEAD_DIM)
    g_mix, g_mlp, g_ple = small["mix_norm"][None, :], small["mlp_norm"][None, :], small["ple_norm"][None, :]
    g_final, g_ret = small["final_norm"][None, :], small["ret_norm_gain"][None, :]
    gq_w = jnp.tile(small["attn_q_norm"], ATTN_HEADS)[None, :]
    gk_w = jnp.tile(small["attn_k_norm"], ATTN_KV_HEADS)[None, :]
    logits = small["ret_decay_logit"]

    hb = _stage_norm_in(x, g_mix)
    proj = _mm("in_proj", hb, w["w_in"], tm=512, tn=IN_W // 2, tk=1024, out_dtypes=(BF16,), j_outer=True)
    q_ct, k_rows, k_ct, v_rows, v_ct, rq, rk = _stage_qkv(proj, tabs, gq_w, gk_w)
    o_ct, lse = _attn_fwd(q_ct, k_rows, v_ct)
    ry_f, st_f, ry_b, st_b = _ret_fwd(logits, rq, rk, proj)
    rz, attn_rows = _stage_mix_post(ry_f, ry_b, proj, o_ct, g_ret)
    a_out = _mm("attn_o", attn_rows, w["w_attn_o"], tm=1024, tn=1024, tk=512, out_dtypes=(BF16,))
    r_out = _mm("ret_o", rz, w["w_ret_o"], tm=1024, tn=1024, tk=512, out_dtypes=(BF16,))
    merged = _stage_merge(proj, a_out, r_out)

    def epi_res_norm(acc, e, c):
        xr = e[0][...] + acc
        return xr, _rms_fwd(xr, c[0][...])

    x1, hm = _mm("out_proj", merged, w["w_out"], tm=512, tn=1024, tk=1024, out_dtypes=(F32, BF16),
                 epi=epi_res_norm, epi_ins=(x,), consts=(g_mlp,))

    def epi_relu2(acc, e, c):
        r = jnp.maximum(acc, 0.0)
        return (r * r,)

    act = _mm("mlp_up", hm, w["w_up"], tm=512, tn=2048, tk=1024, out_dtypes=(BF16,), epi=epi_relu2, j_outer=True)
    x2, hp = _mm("mlp_down", act, w["w_down"], tm=512, tn=1024, tk=D_FF, out_dtypes=(F32, BF16),
                 epi=epi_res_norm, epi_ins=(x1,), consts=(g_ple,))
    zg = _mm("ple_gate", hp, w["w_ple_gate"], tm=1024, tn=1024, tk=1024)
    pe = _mm("ple_emb", p, w["w_ple"], tm=1024, tn=1024, tk=256)
    dx3, dzg, dpe, loss_cols, g_final_p = _stage_head(zg, pe, x2, target, g_final)
    loss_sum = 0.5 / D_MODEL * jnp.sum(loss_cols)

    gw = {}
    gw["w_ple"] = _mm("g_w_ple", p, dpe, ta=True, tm=256, tn=1024, tk=2048)
    gw["w_ple_gate"] = _mm("g_w_ple_gate", hp, dzg, ta=True, tm=1024, tn=1024, tk=2048)
    def epi_norm_bwd(acc, e, c):
        dx, dg = _rms_bwd(acc, e[0][...], c[0][...])
        return e[1][...] + dx, dg

    dx2, g_ple_p = _mm("d_hp", dzg, w["w_ple_gate"], tb=True, tm=512, tn=1024, tk=1024, epi=epi_norm_bwd, epi_ins=(x2, dx3),
                       consts=(g_ple,), n_sums=1)

    def epi_relu2_bwd(acc, e, c):
        return (acc * (2.0 * jnp.sqrt(e[0][...].astype(F32))),)

    du = _mm("d_u", dx2, w["w_down"], tb=True, tm=512, tn=2048, tk=1024, out_dtypes=(BF16,), epi=epi_relu2_bwd, epi_ins=(act,),
             j_outer=True)
    gw["w_down"] = _mm("g_w_down", act, dx2, ta=True, tm=1024, tn=1024, tk=2048)
    gw["w_up"] = _mm("g_w_up", hm, du, ta=True, tm=1024, tn=1024, tk=2048)
    dx1, g_mlp_p = _mm("d_hm", du, w["w_up"], tb=True, tm=512, tn=1024, tk=D_FF, epi=epi_norm_bwd, epi_ins=(x1, dx2),
                       consts=(g_mlp,), n_sums=1)
    dmerged = _mm("d_merged", dx1, w["w_out"], tb=True, tm=1024, tn=1024, tk=1024)
    gw["w_out"] = _mm("g_w_out", merged, dx1, ta=True, tm=1024, tn=1024, tk=2048)
    dao, dro, dga, dgr = _stage_merge_bwd(proj, dmerged, a_out, r_out)
    gw["w_attn_o"] = _mm("g_w_attn_o", attn_rows, dao, ta=True, tm=512, tn=1024, tk=2048)
    gw["w_ret_o"] = _mm("g_w_ret_o", rz, dro, ta=True, tm=512, tn=1024, tk=2048)
    dattn = _mm("d_attn", dao, w["w_attn_o"], tb=True, tm=1024, tn=512, tk=1024)
    drz = _mm("d_rz", dro, w["w_ret_o"], tb=True, tm=1024, tn=512, tk=1024)
    do_ct, delta, dry, drg, g_ret_p = _stage_mix_post_bwd(dattn, attn_rows, drz, ry_f, ry_b, proj, g_ret)
    dq_f, dk_f, dv_f, dl_f, dq_b, dk_b, dv_b, dl_b = _ret_bwd(logits, rq, rk, proj, dry, st_f, st_b)
    dq_ct, dk8, dv8 = _attn_bwd(q_ct, do_ct, lse, delta, k_rows, v_rows, k_ct)
    dproj, gq_p, gk_p = _stage_dproj(proj, dq_ct, dk8, dv8, (dq_f, dk_f, dv_f, dq_b, dk_b, dv_b), drg, dga, dgr, tabs, gq_w, gk_w)
    gw["w_in"] = _mm("g_w_in", hb, dproj, ta=True, tm=1024, tn=IN_W // 2, tk=1024)
    grad_x, g_mix_p = _mm("d_h", dproj, w["w_in"], tb=True, tm=512, tn=1024, tk=IN_W, epi=epi_norm_bwd, epi_ins=(x, dx1),
                          consts=(g_mix,), n_sums=1)

    gs = {
        "mix_norm": g_mix_p[0], "mlp_norm": g_mlp_p[0], "ple_norm": g_ple_p[0], "final_norm": g_final_p[0],
        "ret_norm_gain": g_ret_p[0],
        "attn_q_norm": jnp.sum(gq_p[0].reshape(ATTN_HEADS, ATTN_HEAD_DIM), axis=0),
        "attn_k_norm": jnp.sum(gk_p[0].reshape(ATTN_KV_HEADS, ATTN_HEAD_DIM), axis=0),
        "ret_decay_logit": jnp.stack([dl_f[:, 0, 0], dl_b[:, 0, 0]]),
    }
    return loss_sum, grad_x, gw, gs


PACK_COLS = 1024
N_CHIPS = 4
HALF_ROWS = 2048


def _pack_shard(parts):
    return jnp.concatenate([parts[n].reshape(-1, PACK_COLS) for n, _ in BIG], axis=0)


def _unpack_shard(slab, shapes):
    out, r = {}, 0
    for n, _ in BIG:
        rows = math.prod(shapes[n]) // PACK_COLS
        out[n] = slab[r:r + rows].reshape(shapes[n])
        r += rows
    return out


def _shard_of(full, axis, sidx):
    size = full.shape[axis] // N_CHIPS
    return lax.slice_in_dim(full, sidx * size, (sidx + 1) * size, axis=axis)


def _position():
    x, y, c = lax.axis_index("x"), lax.axis_index("y"), lax.axis_index("c")
    return x, y, c


def _other_chips(x, y):
    return [(1 - x, y), (x, 1 - y), (1 - x, 1 - y)]


ANY = pl.BlockSpec(memory_space=pl.ANY)


def _gather_weights(slab):
    rows = slab.shape[0]
    half = rows // 2

    def body(in_ref, out_ref, send_sems, recv_sems):
        x, y, c = _position()
        chips = _other_chips(x, y)

        def piece(chip, core):
            return out_ref.at[2 * chip[0] + chip[1], pl.ds(core * half, half), :]

        def copy(k, chip, core, to, src=None):
            return pltpu.make_async_remote_copy(
                src_ref=piece(chip, core) if src is None else src, dst_ref=piece(chip, core),
                send_sem=send_sems.at[k], recv_sem=recv_sems.at[k], device_id=to, device_id_type=MESH)

        first = [copy(j, (x, y), c, (*chip, c), src=in_ref.at[pl.ds(c * half, half), :]) for j, chip in enumerate(chips)]
        for cp in first:
            cp.start()
        passed = [copy(3 + j, chip, c, (x, y, 1 - c)) for j, chip in enumerate(chips)]
        for j, chip in enumerate(chips):
            copy(j, chip, c, (x, y, c)).wait_recv()
            passed[j].start()
        for j, chip in enumerate(chips):
            copy(3 + j, chip, 1 - c, (x, y, c)).wait_recv()
        for cp in first + passed:
            cp.wait_send()

    return pl.pallas_call(
        body, name="gather_weights", in_specs=[ANY], out_specs=ANY,
        out_shape=jax.ShapeDtypeStruct((N_CHIPS,) + slab.shape, slab.dtype),
        scratch_shapes=[pltpu.SemaphoreType.DMA((6,)), pltpu.SemaphoreType.DMA((6,))],
    )(slab)


def _exchange_halves(g):
    def body(g_ref, out_ref, send_sem, recv_sem):
        x, y, c = _position()
        cp = pltpu.make_async_remote_copy(src_ref=g_ref.at[1 - c], dst_ref=out_ref, send_sem=send_sem, recv_sem=recv_sem,
                                          device_id=(x, y, 1 - c), device_id_type=MESH)
        cp.start()
        cp.wait()

    return pl.pallas_call(
        body, name="exchange_halves", in_specs=[ANY], out_specs=ANY,
        out_shape=jax.ShapeDtypeStruct(g.shape[1:], g.dtype),
        scratch_shapes=[pltpu.SemaphoreType.DMA, pltpu.SemaphoreType.DMA],
    )(g)


def _add_my_half(g, r1, c_idx):
    tr = 256
    nt = g.shape[2] // tr

    def body(c_ref, g_ref, r_ref, o_ref, ob_ref):
        tot = g_ref[...] + r_ref[...]
        o_ref[...] = tot
        ob_ref[...] = tot.astype(BF16)

    blk = (None, tr, PACK_COLS)
    spec = pl.BlockSpec(blk, lambda s, i, c_ref: (s, i, 0))
    return pl.pallas_call(
        body, name="add_my_half",
        grid_spec=pltpu.PrefetchScalarGridSpec(
            num_scalar_prefetch=1, grid=(N_CHIPS, nt),
            in_specs=[pl.BlockSpec((None,) + blk, lambda s, i, c_ref: (c_ref[0], s, i, 0)), spec],
            out_specs=[spec, spec]),
        out_shape=[jax.ShapeDtypeStruct(g.shape[1:], F32), jax.ShapeDtypeStruct(g.shape[1:], BF16)],
        compiler_params=_cparams(("parallel", "parallel")),
    )(c_idx, g, r1)


def _scatter_to_chips(part):
    def body(p_ref, out_ref, send_sems, recv_sems):
        x, y, c = _position()
        chips = _other_chips(x, y)
        sends = [pltpu.make_async_remote_copy(
            src_ref=p_ref.at[2 * chip[0] + chip[1]], dst_ref=out_ref.at[j], send_sem=send_sems.at[j], recv_sem=recv_sems.at[j],
            device_id=(*chip, c), device_id_type=MESH) for j, chip in enumerate(chips)]
        for cp in sends:
            cp.start()
        for cp in sends:
            cp.wait()

    return pl.pallas_call(
        body, name="scatter_to_chips", in_specs=[ANY], out_specs=ANY,
        out_shape=jax.ShapeDtypeStruct((N_CHIPS - 1,) + part.shape[1:], part.dtype),
        scratch_shapes=[pltpu.SemaphoreType.DMA((3,)), pltpu.SemaphoreType.DMA((3,))],
    )(part)


def _sum_chips(part, r2, chip_idx):
    tr = 256

    def body(c_ref, p_ref, r_ref, o_ref):
        o_ref[...] = ((p_ref[...] + r_ref[0]) + r_ref[1]) + r_ref[2]

    return pl.pallas_call(
        body, name="sum_chips",
        grid_spec=pltpu.PrefetchScalarGridSpec(
            num_scalar_prefetch=1, grid=(r2.shape[1] // tr,),
            in_specs=[pl.BlockSpec((None, tr, PACK_COLS), lambda i, c_ref: (c_ref[0], i, 0)),
                      pl.BlockSpec((N_CHIPS - 1, tr, PACK_COLS), lambda i, c_ref: (0, i, 0))],
            out_specs=pl.BlockSpec((tr, PACK_COLS), lambda i, c_ref: (i, 0))),
        out_shape=jax.ShapeDtypeStruct(r2.shape[1:], F32),
        compiler_params=_cparams(("parallel",)),
    )(chip_idx, part, r2)


def _join_halves(red):
    def body(r_ref, out_ref, send_sem, recv_sem):
        x, y, c = _position()
        cp = pltpu.make_async_remote_copy(src_ref=r_ref, dst_ref=out_ref, send_sem=send_sem, recv_sem=recv_sem,
                                          device_id=(x, y, 1 - c), device_id_type=MESH)
        cp.start()
        cp.wait()

    return pl.pallas_call(
        body, name="join_halves", in_specs=[ANY], out_specs=ANY,
        out_shape=jax.ShapeDtypeStruct(red.shape, red.dtype),
        scratch_shapes=[pltpu.SemaphoreType.DMA, pltpu.SemaphoreType.DMA],
    )(red)


def _adamw_math(w, g, m, v):
    m = ADAM_B1 * m + (1.0 - ADAM_B1) * g
    v = ADAM_B2 * v + (1.0 - ADAM_B2) * (g * g)
    m_hat = m / (1.0 - ADAM_B1 ** ADAM_STEP)
    v_hat = v / (1.0 - ADAM_B2 ** ADAM_STEP)
    delta = -ADAM_LR * (m_hat / (jnp.sqrt(v_hat) + ADAM_EPS) + ADAM_WD * w)
    return delta, m, v


def _adamw(name, w, g, m, v):
    tr = min(256, w.shape[0])

    def body(w_ref, g_ref, m_ref, v_ref, d_ref, nm_ref, nv_ref):
        d_ref[...], nm_ref[...], nv_ref[...] = _adamw_math(w_ref[...], g_ref[...], m_ref[...], v_ref[...])

    blk = pl.BlockSpec((tr, w.shape[1]), lambda i: (i, 0))
    return pl.pallas_call(
        body, name="adamw_" + name, grid=(w.shape[0] // tr,), in_specs=[blk] * 4, out_specs=[blk] * 3,
        out_shape=[jax.ShapeDtypeStruct(w.shape, F32)] * 3, compiler_params=_cparams(("parallel",)),
    )(w, g, m, v)


def _small_step(gpk, wpk, mpk, vpk):
    row, col, width = SMALL["ret_decay_logit"]

    def body(g_ref, w_ref, m_ref, v_ref, og_ref, od_ref, om_ref, ov_ref, gbuf, send_sems, recv_sems):
        x, y, c = _position()
        me = 4 * x + 2 * y + c
        gbuf[me] = g_ref[...]
        sends = []
        for k in range(1, 8):
            to = (x ^ (k >> 2), y ^ ((k >> 1) & 1), c ^ (k & 1))
            cp = pltpu.make_async_remote_copy(src_ref=g_ref, dst_ref=gbuf.at[me], send_sem=send_sems.at[k - 1],
                                              recv_sem=recv_sems.at[k - 1], device_id=to, device_id_type=MESH)
            cp.start()
            sends.append(cp)
        for k in range(1, 8):
            frm = me ^ k
            pltpu.make_async_remote_copy(src_ref=g_ref, dst_ref=gbuf.at[frm], send_sem=send_sems.at[k - 1],
                                         recv_sem=recv_sems.at[k - 1], device_id=(x, y, c), device_id_type=MESH).wait_recv()
        for cp in sends:
            cp.wait_send()
        tot = gbuf[0]
        for d in range(1, 8):
            tot = tot + gbuf[d]
        w = w_ref[...]
        r_i = lax.broadcasted_iota(jnp.int32, w.shape, 0)
        c_i = lax.broadcasted_iota(jnp.int32, w.shape, 1)
        is_logit = (r_i == row) & (c_i >= col) & (c_i < col + width)
        g = jnp.where(is_logit, tot * _sigmoid(-w), tot)
        og_ref[...] = g
        od_ref[...], om_ref[...], ov_ref[...] = _adamw_math(w, g, m_ref[...], v_ref[...])

    vm = pl.BlockSpec(memory_space=pltpu.VMEM)
    shp = jax.ShapeDtypeStruct(gpk.shape, F32)
    return pl.pallas_call(
        body, name="small_step", in_specs=[vm] * 4, out_specs=[vm] * 4, out_shape=[shp] * 4,
        scratch_shapes=[pltpu.VMEM((8,) + gpk.shape, F32), pltpu.SemaphoreType.DMA((7,)), pltpu.SemaphoreType.DMA((7,))],
    )(gpk, wpk, mpk, vpk)


def _pack_small(parts):
    rows = [[] for _ in range(SMALL_ROWS)]
    for n, (r, col, width) in sorted(SMALL.items(), key=lambda kv: (kv[1][0], kv[1][1])):
        rows[r].append((col, parts[n].reshape(-1).astype(F32)))
    out = []
    for r in range(SMALL_ROWS):
        segs, pos = [], 0
        for col, vec in rows[r]:
            assert col == pos
            segs.append(vec)
            pos += vec.shape[0]
        if pos < PACK_COLS:
            segs.append(jnp.zeros((PACK_COLS - pos,), F32))
        out.append(jnp.concatenate(segs))
    return jnp.stack(out)


def _unpack_small(pk, shapes):
    return {n: pk[r, col:col + width].reshape(shapes[n]) for n, (r, col, width) in SMALL.items()}


WEIGHTS = ("mix_norm", "w_in", "attn_q_norm", "attn_k_norm", "ret_decay_logit", "ret_norm_gain", "w_attn_o", "w_ret_o", "w_out",
           "mlp_norm", "w_up", "w_down", "ple_norm", "w_ple_gate", "w_ple", "final_norm")


def kernel(x, p, mix_norm, w_in, attn_q_norm, attn_k_norm, ret_decay_logit, ret_norm_gain, w_attn_o, w_ret_o, w_out, mlp_norm, w_up, w_down, ple_norm, w_ple_gate, w_ple, final_norm, loss_target, m_mix_norm, m_w_in, m_attn_q_norm, m_attn_k_norm, m_ret_decay_logit, m_ret_norm_gain, m_w_attn_o, m_w_ret_o, m_w_out, m_mlp_norm, m_w_up, m_w_down, m_ple_norm, m_w_ple_gate, m_w_ple, m_final_norm, v_mix_norm, v_w_in, v_attn_q_norm, v_attn_k_norm, v_ret_decay_logit, v_ret_norm_gain, v_w_attn_o, v_w_ret_o, v_w_out, v_mlp_norm, v_w_up, v_w_down, v_ple_norm, v_w_ple_gate, v_w_ple, v_final_norm):
    args = dict(locals())
    wts = {n: args[n] for n in WEIGHTS}
    ms = {n: args["m_" + n] for n in WEIGHTS}
    vs = {n: args["v_" + n] for n in WEIGHTS}
    shapes = {n: wts[n].shape for n in WEIGHTS}
    big_names = [n for n, _ in BIG]
    xi, yi, ci = _position()
    c_idx = ci.astype(jnp.int32).reshape(1)

    chip_idx = (2 * xi + yi).astype(jnp.int32)
    slab_b = _pack_shard({n: wts[n][0].astype(BF16) for n in big_names})
    gathered = lax.dynamic_update_slice(_gather_weights(slab_b), slab_b[None], (chip_idx, 0, 0))
    full = {}
    for n, axis in BIG:
        per_chip = [_unpack_shard(gathered[k], {m_: shapes[m_][1:] for m_ in big_names})[n] for k in range(N_CHIPS)]
        full[n] = jnp.concatenate(per_chip, axis=axis)
    small = {n: wts[n].reshape(wts[n].shape[1:] if wts[n].ndim > 1 else wts[n].shape) for n in SMALL}

    loss_part, grad_x, gw, gs = _local_step(x[0], p[0, 0], loss_target[0], full, small)
    loss = lax.psum(loss_part, ("x", "y", "c"))

    slabs = jnp.stack([_pack_shard({n: _shard_of(gw[n], axis, k) for n, axis in BIG}) for k in range(N_CHIPS)])
    halves = slabs.reshape(N_CHIPS, 2, HALF_ROWS, PACK_COLS).transpose(1, 0, 2, 3)
    chip_part, chip_part_b = _add_my_half(halves, _exchange_halves(halves), c_idx)
    mine = _sum_chips(chip_part, _scatter_to_chips(chip_part_b), chip_idx.reshape(1))
    both = jnp.stack([mine, _join_halves(mine)])
    reduced = jnp.where(ci == 0, both, both[::-1]).reshape(2 * HALF_ROWS, PACK_COLS)
    g_big = _unpack_shard(reduced, {n: shapes[n][1:] for n in big_names})
    big_out = [{}, {}, {}, {}]
    for n in big_names:
        big_out[0][n] = g_big[n][None]
        for kind, a in enumerate(_adamw(n, wts[n][0], g_big[n], ms[n][0], vs[n][0])):
            big_out[kind + 1][n] = a[None]

    sm_out = _small_step(_pack_small(gs), _pack_small({n: wts[n] for n in SMALL}), _pack_small({n: ms[n] for n in SMALL}),
                         _pack_small({n: vs[n] for n in SMALL}))
    small_out = [_unpack_small(a, {n: shapes[n] for n in SMALL}) for a in sm_out]

    outs = [loss, grad_x[None]]
    for kind in range(4):
        for n in WEIGHTS:
            outs.append(small_out[kind][n] if n in SMALL else big_out[kind][n])
    return tuple(outs)
```

```python
import functools
import math

import jax
import jax.numpy as jnp
from jax import lax
from jax.experimental import pallas as pl
from jax.experimental.pallas import tpu as pltpu

F32 = jnp.float32
BF16 = jnp.bfloat16
MESH = pl.DeviceIdType.MESH

D_MODEL = 1024
PLE_DIM = 256
GRID_W = 64
ATTN_HEAD_DIM = 64
ATTN_HEADS = 8
ATTN_KV_HEADS = 2
ATTN_GROUP = ATTN_HEADS // ATTN_KV_HEADS
RET_HEAD_DIM = 128
RET_HEADS = 4
ATTN_Q_W = 512
ATTN_KV_W = 128
RET_W = 512
IN_W = 4864
D_FF = 4096
RET_CHUNK = 128
ROPE_THETA = 10000.0
NORM_EPS = 1e-6
GN_EPS = 1e-5
ATTN_SCALE = ATTN_HEAD_DIM ** -0.5
LOG2E = math.log2(math.e)
Q_FOLD = ATTN_SCALE * LOG2E
RET_SCALE = RET_HEAD_DIM ** -0.5

C_AQ, C_AK, C_AV, C_RQ, C_RK, C_RV, C_RG, C_GA, C_GR = 0, 512, 640, 768, 1280, 1792, 2304, 2816, 3840

ADAM_LR = 0.001
ADAM_B1 = 0.9
ADAM_B2 = 0.999
ADAM_EPS = 1e-08
ADAM_WD = 0.01
ADAM_STEP = 10

LANES = 128
VMEM_LIMIT = 56 << 20
SEQ_TILE = 512

BIG = (("w_in", 1), ("w_attn_o", 1), ("w_ret_o", 1), ("w_out", 0), ("w_up", 1), ("w_down", 0), ("w_ple_gate", 0), ("w_ple", 1))
SMALL_ROWS = 8
SMALL = {"mix_norm": (0, 0, 1024), "mlp_norm": (1, 0, 1024), "ple_norm": (2, 0, 1024), "final_norm": (3, 0, 1024),
         "ret_norm_gain": (4, 0, 512), "attn_q_norm": (4, 512, 64), "attn_k_norm": (4, 576, 64), "ret_decay_logit": (4, 640, 8)}


def _seq_tile(s):
    return min(SEQ_TILE, s // 2)


def _cparams(sem=None, vmem=VMEM_LIMIT):
    return pltpu.CompilerParams(dimension_semantics=sem, vmem_limit_bytes=vmem)


def _mm(name, a, b, *, ta=False, tb=False, tm, tn, tk, out_dtypes=(F32,), epi=None, epi_ins=(), consts=(), n_sums=0, j_outer=False):
    if ta:
        kdim, m = a.shape
    else:
        m, kdim = a.shape
    n = b.shape[0] if tb else b.shape[1]
    tm, tn, tk = min(tm, m), min(tn, n), min(tk, kdim)
    assert m % tm == 0 and n % tn == 0 and kdim % tk == 0, (name, m, n, kdim, tm, tn, tk)
    nk = kdim // tk
    n_e, n_c, n_o = len(epi_ins), len(consts), len(out_dtypes)
    assert n_sums == 0 or tn == n

    def body(*refs):
        a_ref, b_ref = refs[0], refs[1]
        e_refs = refs[2:2 + n_e]
        c_refs = refs[2 + n_e:2 + n_e + n_c]
        o_refs = refs[2 + n_e + n_c:2 + n_e + n_c + n_o]
        s_refs = refs[2 + n_e + n_c + n_o:2 + n_e + n_c + n_o + n_sums]
        acc_ref = refs[2 + n_e + n_c + n_o + n_sums] if nk > 1 else None
        k = pl.program_id(2)
        if n_sums:
            @pl.when((pl.program_id(1 if j_outer else 0) == 0) & (k == 0))
            def _():
                for r in s_refs:
                    r[...] = jnp.zeros(r.shape, F32)
        av = a_ref[...].astype(BF16)
        bv = b_ref[...].astype(BF16)
        dims = (((0,) if ta else (1,), (1,) if tb else (0,)), ((), ()))
        part = lax.dot_general(av, bv, dims, preferred_element_type=F32)

        def finish(acc):
            vals = epi(acc, e_refs, c_refs) if epi is not None else (acc,)
            for o_ref, v in zip(o_refs, vals[:n_o]):
                o_ref[...] = v.astype(o_ref.dtype)
            for s_ref, v in zip(s_refs, vals[n_o:]):
                _acc_add(s_ref, v)

        if nk == 1:
            finish(part)
        else:
            @pl.when(k == 0)
            def _():
                acc_ref[...] = part

            @pl.when(k > 0)
            def _():
                acc_ref[...] += part

            @pl.when(k == nk - 1)
            def _():
                finish(acc_ref[...])

    def spec(shape, index):
        return pl.BlockSpec(shape, (lambda j, i, k: index(i, j, k)) if j_outer else index)

    a_spec = spec((tk, tm), lambda i, j, k: (k, i)) if ta else spec((tm, tk), lambda i, j, k: (i, k))
    b_spec = spec((tn, tk), lambda i, j, k: (j, k)) if tb else spec((tk, tn), lambda i, j, k: (k, j))
    o_spec = spec((tm, tn), lambda i, j, k: (i, j))
    c_specs = [spec(c.shape, lambda i, j, k, nd=c.ndim: (0,) * nd) for c in consts]
    outs = pl.pallas_call(
        body, name=name,
        grid=(n // tn, m // tm, nk) if j_outer else (m // tm, n // tn, nk),
        in_specs=[a_spec, b_spec] + [o_spec] * n_e + c_specs,
        out_specs=[o_spec] * n_o + [spec((8, n), lambda i, j, k: (0, 0))] * n_sums,
        out_shape=[jax.ShapeDtypeStruct((m, n), dt) for dt in out_dtypes] + [jax.ShapeDtypeStruct((8, n), F32)] * n_sums,
        scratch_shapes=[pltpu.VMEM((tm, tn), F32)] if nk > 1 else [],
        compiler_params=_cparams(("arbitrary",) * 3 if n_sums else ("parallel", "parallel", "arbitrary")),
    )(a, b, *epi_ins, *consts)
    return outs[0] if n_o + n_sums == 1 else outs


def _rows(arr, tr):
    return (arr, pl.BlockSpec((tr, arr.shape[1]), lambda i: (i, 0)))


def _win(arr, tr, start, width):
    bw = math.gcd(start, width) if start else width
    assert bw % LANES == 0
    return [(arr, pl.BlockSpec((tr, bw), lambda i, cb=start // bw + p: (i, cb))) for p in range(width // bw)]


def _ct(arr):
    return (arr, pl.BlockSpec((None,) + arr.shape[1:], lambda i: (i, 0, 0)))


def _whole(arr):
    return (arr, pl.BlockSpec(arr.shape, lambda i, nd=arr.ndim: (0,) * nd))


def _cat(refs):
    vals = [r[...].astype(F32) for r in refs]
    return vals[0] if len(vals) == 1 else jnp.concatenate(vals, axis=1)


def _seqtiled(name, fn, n_tiles, ins, outs, acc_widths=()):
    n_i, n_o, n_a = len(ins), len(outs), len(acc_widths)

    def body(*refs):
        i_refs, o_refs, a_refs = refs[:n_i], refs[n_i:n_i + n_o], refs[n_i + n_o:]
        if n_a:
            @pl.when(pl.program_id(0) == 0)
            def _():
                for r in a_refs:
                    r[...] = jnp.zeros(r.shape, F32)
        fn(list(i_refs), list(o_refs), list(a_refs))

    res = pl.pallas_call(
        body, name=name, grid=(n_tiles,),
        in_specs=[s for _, s in ins],
        out_specs=[s for _, _, s in outs] + [pl.BlockSpec((8, w), lambda i: (0, 0)) for w in acc_widths],
        out_shape=[jax.ShapeDtypeStruct(sh, dt) for sh, dt, _ in outs] + [jax.ShapeDtypeStruct((8, w), F32) for w in acc_widths],
        compiler_params=_cparams(("arbitrary",)),
    )(*[a for a, _ in ins])
    return res


def _acc_add(acc_ref, val):
    acc_ref[0:1, :] += jnp.sum(val, axis=0, keepdims=True)


def _out_rows(s, w, dt, tr):
    return ((s, w), dt, pl.BlockSpec((tr, w), lambda i: (i, 0)))


def _out_ct(s, w, dt, t):
    return ((s // t, w, t), dt, pl.BlockSpec((None, w, t), lambda i: (i, 0, 0)))


def _rms_fwd(x, gain):
    r = lax.rsqrt(jnp.mean(x * x, axis=-1, keepdims=True) + NORM_EPS)
    return x * r * gain


def _rms_bwd(dy, x, gain):
    r = lax.rsqrt(jnp.mean(x * x, axis=-1, keepdims=True) + NORM_EPS)
    xn = x * r
    dyg = dy * gain
    dx = r * (dyg - xn * jnp.mean(dyg * xn, axis=-1, keepdims=True))
    return dx, dy * xn


def _seg_mean(y, hd):
    w = y.shape[1]
    pieces = []
    for s in range(0, w, LANES):
        v = y[:, s:s + LANES]
        tot = jnp.sum(v, axis=1, keepdims=True)
        if hd == LANES:
            pieces.append(jnp.broadcast_to(tot, v.shape))
        else:
            low = lax.broadcasted_iota(jnp.int32, v.shape, 1) < hd
            lo = jnp.sum(jnp.where(low, v, 0.0), axis=1, keepdims=True)
            pieces.append(jnp.where(low, lo, tot - lo))
    out = pieces[0] if len(pieces) == 1 else jnp.concatenate(pieces, axis=1)
    return out * (1.0 / hd)


def _tile_lanes(t, w):
    return t if w == t.shape[1] else jnp.concatenate([t] * (w // t.shape[1]), axis=1)


def _swap_halves(x, hd):
    w = x.shape[1]
    half = hd // 2
    lane = lax.broadcasted_iota(jnp.int32, x.shape, 1)
    return jnp.where((lane % hd) < half, pltpu.roll(x, w - half, 1), pltpu.roll(x, half, 1))


def _rope(x, cos, sin_signed, hd):
    w = x.shape[1]
    return x * _tile_lanes(cos, w) + _swap_halves(x, hd) * _tile_lanes(sin_signed, w)


def _rope_t(dy, cos, sin_signed, hd):
    w = dy.shape[1]
    return dy * _tile_lanes(cos, w) + _swap_halves(dy * _tile_lanes(sin_signed, w), hd)


def _headnorm_fwd(x, gain_w, hd):
    r = lax.rsqrt(_seg_mean(x * x, hd) + NORM_EPS)
    return x * r * gain_w


def _headnorm_bwd(dy, x, gain_w, hd):
    r = lax.rsqrt(_seg_mean(x * x, hd) + NORM_EPS)
    xn = x * r
    dyg = dy * gain_w
    return r * (dyg - xn * _seg_mean(dyg * xn, hd)), dy * xn


def _sigmoid(x):
    return 1.0 / (1.0 + jnp.exp(-x))


def _rope_tables(seq_len, head_dim):
    rows = seq_len // GRID_W
    n_axis = head_dim // 4
    freqs = ROPE_THETA ** (-jnp.arange(n_axis, dtype=F32) / n_axis)
    ang_r = jnp.arange(rows, dtype=F32)[:, None] * freqs
    ang_c = jnp.arange(GRID_W, dtype=F32)[:, None] * freqs

    def expand(by_row, by_col):
        r = jnp.broadcast_to(by_row[:, None, :], (rows, GRID_W, n_axis))
        c = jnp.broadcast_to(by_col[None, :, :], (rows, GRID_W, n_axis))
        return jnp.concatenate([r, c], axis=-1).reshape(seq_len, 2 * n_axis)

    cos, sin = expand(jnp.cos(ang_r), jnp.cos(ang_c)), expand(jnp.sin(ang_r), jnp.sin(ang_c))
    reps = LANES // head_dim
    return jnp.tile(jnp.concatenate([cos, cos], axis=-1), (1, reps)), jnp.tile(jnp.concatenate([-sin, sin], axis=-1), (1, reps))


def _stage_norm_in(x, gain):
    s = x.shape[0]
    tr = min(SEQ_TILE, s)

    def fn(i, o, a):
        o[0][...] = _rms_fwd(i[0][...], i[1][...]).astype(BF16)

    return _seqtiled("norm_in", fn, s // tr, [_rows(x, tr), _whole(gain)], [_out_rows(s, D_MODEL, BF16, tr)])[0]


def _stage_qkv(proj, tabs, gq_w, gk_w):
    s = proj.shape[0]
    t = _seq_tile(s)
    ca, sa, cr, sr = tabs
    ins = (_win(proj, t, C_AQ, ATTN_Q_W) + _win(proj, t, C_AK, ATTN_KV_W) + _win(proj, t, C_AV, ATTN_KV_W)
           + _win(proj, t, C_RQ, RET_W) + _win(proj, t, C_RK, RET_W)
           + [_rows(ca, t), _rows(sa, t), _rows(cr, t), _rows(sr, t), _whole(gq_w), _whole(gk_w)])

    def fn(i, o, a):
        aq, ak, av = (i[n][...].astype(F32) for n in range(3))
        rq, rk = _cat(i[3:5]), _cat(i[5:7])
        ca_, sa_, cr_, sr_ = i[7][...], i[8][...], i[9][...], i[10][...]
        qr = _rope(_headnorm_fwd(aq, i[11][...], ATTN_HEAD_DIM), ca_, sa_, ATTN_HEAD_DIM) * Q_FOLD
        kr = _rope(_headnorm_fwd(ak, i[12][...], ATTN_HEAD_DIM), ca_, sa_, ATTN_HEAD_DIM)
        qt = qr.T.astype(BF16)
        zeros = jnp.zeros((ATTN_HEAD_DIM, t), BF16)
        for h in range(ATTN_HEADS):
            g = h // ATTN_GROUP
            blk = qt[h * ATTN_HEAD_DIM:(h + 1) * ATTN_HEAD_DIM, :]
            o[0][h * LANES + g * ATTN_HEAD_DIM:h * LANES + (g + 1) * ATTN_HEAD_DIM, :] = blk
            o[0][h * LANES + (1 - g) * ATTN_HEAD_DIM:h * LANES + (2 - g) * ATTN_HEAD_DIM, :] = zeros
        o[1][...] = kr.astype(BF16)
        o[2][...] = kr.T.astype(BF16)
        o[3][...] = av.astype(BF16)
        o[4][...] = av.T.astype(BF16)
        o[5][...] = _rope(rq, cr_, sr_, RET_HEAD_DIM) * RET_SCALE
        o[6][...] = _rope(rk, cr_, sr_, RET_HEAD_DIM)

    outs = [_out_ct(s, ATTN_HEADS * LANES, BF16, t), _out_rows(s, ATTN_KV_W, BF16, t), _out_ct(s, ATTN_KV_W, BF16, t),
            _out_rows(s, ATTN_KV_W, BF16, t), _out_ct(s, ATTN_KV_W, BF16, t), _out_rows(s, RET_W, F32, t), _out_rows(s, RET_W, F32, t)]
    return _seqtiled("qkv_prep", fn, s // t, ins, outs)


def _groupnorm_gate(ry, rg, gain):
    mu = _seg_mean(ry, RET_HEAD_DIM)
    d = ry - mu
    rs = lax.rsqrt(_seg_mean(d * d, RET_HEAD_DIM) + GN_EPS)
    return d * rs, rs, _sigmoid(rg)


def _stage_mix_post(ry_f, ry_b, proj, o_ct, gain):
    s = proj.shape[0]
    t = _seq_tile(s)
    ins = [_rows(ry_f, t), _rows(ry_b, t)] + _win(proj, t, C_RG, RET_W) + [_ct(o_ct), _whole(gain)]

    def fn(i, o, a):
        ry = i[0][...] + i[1][...]
        rg = _cat(i[2:4])
        gn, _, sg = _groupnorm_gate(ry, rg, None)
        o[0][...] = (gn * i[5][...] * (rg * sg)).astype(BF16)
        o[1][...] = i[4][...].astype(F32).T.astype(BF16)

    return _seqtiled("mix_post", fn, s // t, ins, [_out_rows(s, RET_W, BF16, t), _out_rows(s, ATTN_Q_W, BF16, t)])


def _stage_merge(proj, a_out, r_out):
    s = proj.shape[0]
    tr = min(SEQ_TILE, s)
    ins = _win(proj, tr, C_GA, D_MODEL) + _win(proj, tr, C_GR, D_MODEL) + [_rows(a_out, tr), _rows(r_out, tr)]
    na = len(_win(proj, tr, C_GA, D_MODEL))

    def fn(i, o, a):
        ga, gr = _cat(i[:na]), _cat(i[na:2 * na])
        o[0][...] = (_sigmoid(ga) * i[2 * na][...] + _sigmoid(gr) * i[2 * na + 1][...]).astype(BF16)

    return _seqtiled("merge", fn, s // tr, ins, [_out_rows(s, D_MODEL, BF16, tr)])[0]


def _stage_head(zg, pe, x2, target, g_final):
    s = x2.shape[0]
    tr = min(SEQ_TILE // 2, s)
    ins = [_rows(zg, tr), _rows(pe, tr), _rows(x2, tr), _rows(target, tr), _whole(g_final)]

    def fn(i, o, a):
        gt = _sigmoid(i[0][...])
        pe_ = i[1][...]
        x3 = i[2][...] + gt * pe_
        gf = i[4][...]
        r3 = lax.rsqrt(jnp.mean(x3 * x3, axis=-1, keepdims=True) + NORM_EPS)
        x3n = x3 * r3
        e = x3n * gf - i[3][...]
        _acc_add(a[0], e * e)
        dy = e * (1.0 / D_MODEL)
        _acc_add(a[1], dy * x3n)
        dyg = dy * gf
        dx3 = r3 * (dyg - x3n * jnp.mean(dyg * x3n, axis=-1, keepdims=True))
        o[0][...] = dx3
        o[1][...] = (dx3 * pe_ * gt * (1.0 - gt)).astype(BF16)
        o[2][...] = (dx3 * gt).astype(BF16)

    outs = [_out_rows(s, D_MODEL, F32, tr), _out_rows(s, D_MODEL, BF16, tr), _out_rows(s, D_MODEL, BF16, tr)]
    return _seqtiled("head", fn, s // tr, ins, outs, acc_widths=(D_MODEL, D_MODEL))


def _stage_merge_bwd(proj, dmerged, a_out, r_out):
    s = proj.shape[0]
    tr = min(SEQ_TILE // 2, s)
    wins = _win(proj, tr, C_GA, D_MODEL)
    na = len(wins)
    ins = wins + _win(proj, tr, C_GR, D_MODEL) + [_rows(dmerged, tr), _rows(a_out, tr), _rows(r_out, tr)]

    def fn(i, o, a):
        sa, sr = _sigmoid(_cat(i[:na])), _sigmoid(_cat(i[na:2 * na]))
        dm = i[2 * na][...]
        o[0][...] = (dm * sa).astype(BF16)
        o[1][...] = (dm * sr).astype(BF16)
        o[2][...] = (dm * i[2 * na + 1][...] * sa * (1.0 - sa)).astype(BF16)
        o[3][...] = (dm * i[2 * na + 2][...] * sr * (1.0 - sr)).astype(BF16)

    return _seqtiled("merge_bwd", fn, s // tr, ins, [_out_rows(s, D_MODEL, BF16, tr)] * 4)


def _stage_mix_post_bwd(dattn, attn_rows, drz, ry_f, ry_b, proj, gain):
    s = proj.shape[0]
    t = _seq_tile(s)
    ins = ([_rows(dattn, t), _rows(attn_rows, t), _rows(drz, t), _rows(ry_f, t), _rows(ry_b, t)]
           + _win(proj, t, C_RG, RET_W) + [_whole(gain)])

    def fn(i, o, a):
        da = i[0][...]
        dat = da.T
        prod_t = (da * i[1][...].astype(F32)).T
        dat_b = dat.astype(BF16)
        zeros = jnp.zeros((ATTN_HEAD_DIM, t), BF16)
        for h in range(ATTN_HEADS):
            g = h // ATTN_GROUP
            o[0][h * LANES + g * ATTN_HEAD_DIM:h * LANES + (g + 1) * ATTN_HEAD_DIM, :] = dat_b[h * ATTN_HEAD_DIM:(h + 1) * ATTN_HEAD_DIM, :]
            o[0][h * LANES + (1 - g) * ATTN_HEAD_DIM:h * LANES + (2 - g) * ATTN_HEAD_DIM, :] = zeros
            o[1][h] = jnp.sum(prod_t[h * ATTN_HEAD_DIM:(h + 1) * ATTN_HEAD_DIM, :], axis=0, keepdims=True)
        ry = i[3][...] + i[4][...]
        rg = _cat(i[5:7])
        gain_ = i[7][...]
        gn, rs, sg = _groupnorm_gate(ry, rg, None)
        dz = i[2][...]
        silu = rg * sg
        _acc_add(a[0], dz * gn * silu)
        dgn = dz * gain_ * silu
        o[2][...] = rs * (dgn - _seg_mean(dgn, RET_HEAD_DIM) - gn * _seg_mean(dgn * gn, RET_HEAD_DIM))
        o[3][...] = (dz * gn * gain_ * (sg * (1.0 + rg * (1.0 - sg)))).astype(BF16)

    outs = [_out_ct(s, ATTN_HEADS * LANES, BF16, t),
            ((ATTN_HEADS, s // t, 1, t), F32, pl.BlockSpec((ATTN_HEADS, None, 1, t), lambda i: (0, i, 0, 0))),
            _out_rows(s, RET_W, F32, t), _out_rows(s, RET_W, BF16, t)]
    return _seqtiled("mix_post_bwd", fn, s // t, ins, outs, acc_widths=(RET_W,))


def _stage_dproj(proj, dq_ct, dk8, dv8, rgrads, drg, dga, dgr, tabs, gq_w, gk_w):
    s = proj.shape[0]
    t = _seq_tile(s)
    ca, sa, cr, sr = tabs
    kv8 = pl.BlockSpec((ATTN_HEADS, t, ATTN_KV_W), lambda i: (0, i, 0))
    ins = (_win(proj, t, C_AQ, ATTN_Q_W) + _win(proj, t, C_AK, ATTN_KV_W) + [_ct(dq_ct), (dk8, kv8), (dv8, kv8)]
           + [_rows(g, t) for g in rgrads] + [_rows(drg, t), _rows(dga, t), _rows(dgr, t)]
           + [_rows(ca, t), _rows(sa, t), _rows(cr, t), _rows(sr, t), _whole(gq_w), _whole(gk_w)])

    def fn(i, o, a):
        aq, ak = i[0][...].astype(F32), i[1][...].astype(F32)
        dq_f, dk_f, dv_f, dq_b, dk_b, dv_b = (r[...].astype(F32) for r in i[5:11])
        ca_, sa_, cr_, sr_ = i[14][...], i[15][...], i[16][...], i[17][...]
        dqn = _rope_t(i[2][...].T * ATTN_SCALE, ca_, sa_, ATTN_HEAD_DIM)
        daq, gq_rows = _headnorm_bwd(dqn, aq, i[18][...], ATTN_HEAD_DIM)
        dkn = _rope_t(jnp.sum(i[3][...].astype(F32), axis=0) * (1.0 / LOG2E), ca_, sa_, ATTN_HEAD_DIM)
        dak, gk_rows = _headnorm_bwd(dkn, ak, i[19][...], ATTN_HEAD_DIM)
        _acc_add(a[0], gq_rows)
        _acc_add(a[1], gk_rows)
        out = o[0]
        out[:, C_AQ:C_AQ + ATTN_Q_W] = daq.astype(BF16)
        out[:, C_AK:C_AK + ATTN_KV_W] = dak.astype(BF16)
        out[:, C_AV:C_AV + ATTN_KV_W] = jnp.sum(i[4][...].astype(F32), axis=0).astype(BF16)
        out[:, C_RQ:C_RQ + RET_W] = _rope_t((dq_f + dq_b) * RET_SCALE, cr_, sr_, RET_HEAD_DIM).astype(BF16)
        out[:, C_RK:C_RK + RET_W] = _rope_t(dk_f + dk_b, cr_, sr_, RET_HEAD_DIM).astype(BF16)
        out[:, C_RV:C_RV + RET_W] = (dv_f + dv_b).astype(BF16)
        out[:, C_RG:C_RG + RET_W] = i[11][...]
        out[:, C_GA:C_GA + D_MODEL] = i[12][...]
        out[:, C_GR:C_GR + D_MODEL] = i[13][...]

    return _seqtiled("dproj", fn, s // t, ins, [_out_rows(s, IN_W, BF16, t)], acc_widths=(ATTN_Q_W, ATTN_KV_W))


def _attn_fwd(q_ct, k_rows, v_ct):
    nq, _, t = q_ct.shape
    s = nq * t
    nk = nq
    assert nk % 2 == 0
    n_par = 2

    def body(q_ref, k_ref, v_ref, o_ref, lse_ref, *bufs):
        sbuf = (bufs[0:2], bufs[2:4])
        pbuf = (bufs[4:6], bufs[6:8])

        def scores(w, j, slot):
            kj = k_ref[pl.ds(pl.multiple_of(j * t, t), t), :]
            st = jnp.dot(kj, q_ref[w], preferred_element_type=F32)
            sbuf[w][slot][...] = st
            return jnp.max(st, axis=0, keepdims=True)

        def probs(w, slot, cmax, m, l):
            m_new = jnp.maximum(m, cmax)
            alpha = jnp.exp2(m - m_new)
            pt = jnp.exp2(sbuf[w][slot][...] - m_new)
            pbuf[w][slot][...] = pt.astype(BF16)
            return m_new, alpha * l + jnp.sum(pt, axis=0, keepdims=True), alpha

        def values(w, j, slot, alpha, acc):
            return alpha * acc + jnp.dot(v_ref[j], pbuf[w][slot][...], preferred_element_type=F32)

        init = []
        for w in range(n_par):
            m = jnp.full((1, t), -1e30, F32)
            l = jnp.zeros((1, t), F32)
            cmax0 = scores(w, 0, 0)
            cmax1 = scores(w, 1, 1)
            m, l, alpha0 = probs(w, 0, cmax0, m, l)
            init.append((m, l, jnp.zeros((ATTN_HEAD_DIM, t), F32), cmax1, alpha0))

        def trip(n, carry):
            c = 2 * n
            out = []
            for w in range(n_par):
                m, l, acc, cmax_b, alpha_c = carry[w]
                acc = values(w, c, 0, alpha_c, acc)
                m, l, alpha1 = probs(w, 1, cmax_b, m, l)
                cmax2 = scores(w, c + 2, 0)
                acc = values(w, c + 1, 1, alpha1, acc)
                m, l, alpha2 = probs(w, 0, cmax2, m, l)
                cmax3 = scores(w, c + 3, 1)
                out.append((m, l, acc, cmax3, alpha2))
            return tuple(out)

        res = lax.fori_loop(0, nk // 2 - 1, trip, tuple(init))
        for w in range(n_par):
            m, l, acc, cmax_b, alpha_c = res[w]
            acc = values(w, nk - 2, 0, alpha_c, acc)
            m, l, alpha1 = probs(w, 1, cmax_b, m, l)
            acc = values(w, nk - 1, 1, alpha1, acc)
            o_ref[w] = (acc / l).astype(BF16)
            lse_ref[w] = m + jnp.log2(l)

    return pl.pallas_call(
        body, name="attn_fwd", grid=(ATTN_HEADS, nq // n_par),
        in_specs=[pl.BlockSpec((n_par, LANES, t), lambda h, i: (i, h, 0)),
                  pl.BlockSpec((s, ATTN_KV_W), lambda h, i: (0, 0)),
                  pl.BlockSpec((nk, ATTN_HEAD_DIM, t), lambda h, i: (0, h // ATTN_GROUP, 0))],
        out_specs=[pl.BlockSpec((n_par, ATTN_HEAD_DIM, t), lambda h, i: (i, h, 0)),
                   pl.BlockSpec((None, n_par, 1, t), lambda h, i: (h, i, 0, 0))],
        out_shape=[jax.ShapeDtypeStruct((nq, ATTN_Q_W, t), BF16), jax.ShapeDtypeStruct((ATTN_HEADS, nq, 1, t), F32)],
        scratch_shapes=[pltpu.VMEM((t, t), F32)] * (2 * n_par) + [pltpu.VMEM((t, t), BF16)] * (2 * n_par),
        compiler_params=_cparams(("parallel", "parallel")),
    )(q_ct, k_rows, v_ct)


def _attn_bwd(q_ct, do_ct, lse, delta, k_rows, v_rows, k_ct):
    nq, _, t = q_ct.shape
    s = nq * t
    kc = 2
    tk = kc * t
    nk = nq // kc
    assert nq % 2 == 0 and nq % kc == 0

    def body(q_ref, do_ref, lse_ref, delta_ref, k_ref, v_ref, kt_ref, dq_ref, dk_ref, dv_ref, dk_acc, dv_acc,
             sb0, sb1, db0, db1, pb0, pb1, gb0, gb1):
        j = pl.program_id(1)
        sb, db, pb, gb = (sb0, sb1), (db0, db1), (pb0, pb1), (gb0, gb1)

        @pl.when(j == 0)
        def _():
            dq_ref[...] = jnp.zeros(dq_ref.shape, F32)

        kj, vj = k_ref[...], v_ref[...]
        ktj = jnp.concatenate([kt_ref[u] for u in range(kc)], axis=1)
        dk_acc[...] = jnp.zeros(dk_acc.shape, F32)
        dv_acc[...] = jnp.zeros(dv_acc.shape, F32)

        def products(i, slot):
            sb[slot][...] = jnp.dot(kj, q_ref[i], preferred_element_type=F32)
            db[slot][...] = jnp.dot(vj, do_ref[i], preferred_element_type=F32)

        def cotangents(i, slot):
            pt = jnp.exp2(sb[slot][...] - lse_ref[i])
            pb[slot][...] = pt.astype(BF16)
            gb[slot][...] = (pt * (db[slot][...] - delta_ref[i])).astype(BF16)

        def accumulate(i, slot):
            dst = gb[slot][...]
            dv_acc[...] += _nt(pb[slot][...], do_ref[i])
            dk_acc[...] += _nt(dst, q_ref[i])
            dq_ref[i] += jnp.dot(ktj, dst, preferred_element_type=F32)

        products(0, 0)
        products(1, 1)
        cotangents(0, 0)

        def trip(n, carry):
            c = 2 * n
            accumulate(c, 0)
            cotangents(c + 1, 1)
            products(c + 2, 0)
            accumulate(c + 1, 1)
            cotangents(c + 2, 0)
            products(c + 3, 1)
            return carry

        lax.fori_loop(0, nq // 2 - 1, trip, 0)
        accumulate(nq - 2, 0)
        cotangents(nq - 1, 1)
        accumulate(nq - 1, 1)
        dk_ref[...] = dk_acc[...].astype(dk_ref.dtype)
        dv_ref[...] = dv_acc[...].astype(dv_ref.dtype)

    per_head = pl.BlockSpec((nq, LANES, t), lambda h, j: (0, h, 0))
    stat = pl.BlockSpec((None, nq, 1, t), lambda h, j: (h, 0, 0, 0))
    kv_rows = pl.BlockSpec((tk, ATTN_KV_W), lambda h, j: (j, 0))
    kv_out = pl.BlockSpec((None, tk, ATTN_KV_W), lambda h, j: (h, j, 0))
    return pl.pallas_call(
        body, name="attn_bwd", grid=(ATTN_HEADS, nk),
        in_specs=[per_head, per_head, stat, stat, kv_rows, kv_rows,
                  pl.BlockSpec((kc, ATTN_HEAD_DIM, t), lambda h, j: (j, h // ATTN_GROUP, 0))],
        out_specs=[pl.BlockSpec((nq, ATTN_HEAD_DIM, t), lambda h, j: (0, h, 0)), kv_out, kv_out],
        out_shape=[jax.ShapeDtypeStruct((nq, ATTN_Q_W, t), F32), jax.ShapeDtypeStruct((ATTN_HEADS, s, ATTN_KV_W), BF16),
                   jax.ShapeDtypeStruct((ATTN_HEADS, s, ATTN_KV_W), BF16)],
        scratch_shapes=([pltpu.VMEM((tk, ATTN_KV_W), F32)] * 2 + [pltpu.VMEM((tk, t), F32)] * 4 + [pltpu.VMEM((tk, t), BF16)] * 4),
        compiler_params=_cparams(("parallel", "arbitrary")),
    )(q_ct, do_ct, lse, delta, k_rows, v_rows, k_ct)


def _log_sigmoid(x):
    t = jnp.exp(-jnp.abs(x))
    log1p_t = jnp.where(t < 1e-2, t * (1.0 - t * (0.5 - t * (1.0 / 3.0))), jnp.log(1.0 + t))
    return jnp.minimum(x, 0.0) - log1p_t


def _decay_tables(logit, backward):
    c = RET_CHUNK
    lam = _log_sigmoid(jnp.full((c, c), logit, F32))
    ii = lax.broadcasted_iota(jnp.int32, (c, c), 0).astype(F32)
    jj = lax.broadcasted_iota(jnp.int32, (c, c), 1).astype(F32)
    if not backward:
        dist, dist_t = jnp.maximum(ii - jj, 0.0), jnp.maximum(jj - ii, 0.0)
        mask, mask_t = ii >= jj, jj >= ii
        e_q, e_k = ii + 1.0, (c - 1.0) - ii
    else:
        dist, dist_t = jnp.maximum(jj - ii, 0.0), jnp.maximum(ii - jj, 0.0)
        mask, mask_t = jj > ii, ii > jj
        e_q, e_k = c - ii, ii
    return dict(
        d=jnp.where(mask, jnp.exp(lam * dist), 0.0), d_t=jnp.where(mask_t, jnp.exp(lam * dist_t), 0.0), dist=dist,
        qdec=jnp.exp(lam * e_q), kdec=jnp.exp(lam * e_k), e_q=e_q, e_k=e_k, gam=jnp.exp(lam * c))


def _nt(a, b):
    return lax.dot_general(a, b, (((1,), (1,)), ((), ())), preferred_element_type=F32)


RET_SUB = 2


def _ret_fwd(logits, q, k, proj):
    s = q.shape[0]
    c = RET_CHUNK
    sub = RET_SUB
    nb = s // (c * sub)
    block = (lambda n: n, lambda n: nb - 1 - n)
    order = (tuple(range(sub)), tuple(reversed(range(sub))))
    vwin = _win(proj, c * sub, C_RV, RET_W)
    nv = len(vwin)
    vw = RET_W // nv
    per = 2 + nv

    def body(lg_ref, *refs):
        ins, outs, states = refs[:2 * per], refs[2 * per:2 * per + 4], refs[2 * per + 4:]

        @pl.when(pl.program_id(0) == 0)
        def _():
            for st in states:
                st[...] = jnp.zeros(st.shape, F32)

        for h in range(RET_HEADS):
            for d in range(2):
                q_ref, k_ref, v_refs = ins[d * per], ins[d * per + 1], ins[d * per + 2:(d + 1) * per]
                y_ref, st_ref, state = outs[2 * d], outs[2 * d + 1], states[d]
                tb = _decay_tables(lg_ref[d, h], bool(d))
                sl = slice(h * RET_HEAD_DIM, (h + 1) * RET_HEAD_DIM)
                off = h * RET_HEAD_DIM
                sh = state[h]
                for u in order[d]:
                    rows = slice(u * c, (u + 1) * c)
                    qh, kh = q_ref[rows, sl], k_ref[rows, sl]
                    vb = v_refs[off // vw][rows, off % vw:off % vw + RET_HEAD_DIM].astype(BF16)
                    a = _nt(qh.astype(BF16), kh.astype(BF16)) * tb["d"]
                    st_ref[u, h] = sh
                    y_ref[rows, sl] = (jnp.dot(a.astype(BF16), vb, preferred_element_type=F32)
                                       + jnp.dot((qh * tb["qdec"]).astype(BF16), sh.astype(BF16), preferred_element_type=F32))
                    sh = tb["gam"] * sh + jnp.dot((kh * tb["kdec"]).T.astype(BF16), vb, preferred_element_type=F32)
                state[h] = sh

    hmat = (RET_HEADS, RET_HEAD_DIM, RET_HEAD_DIM)
    in_specs, out_specs, args = [pl.BlockSpec(memory_space=pltpu.SMEM)], [], [logits]
    for d in range(2):
        rows = pl.BlockSpec((c * sub, RET_W), lambda n, d=d: (block[d](n), 0))
        in_specs += [rows, rows] + [pl.BlockSpec(sp.block_shape, lambda n, d=d, cb=sp.index_map(0)[1]: (block[d](n), cb)) for _, sp in vwin]
        args += [q, k] + [a for a, _ in vwin]
        out_specs += [rows, pl.BlockSpec((sub,) + hmat, lambda n, d=d: (block[d](n), 0, 0, 0))]
    return pl.pallas_call(
        body, name="ret_fwd", grid=(nb,), in_specs=in_specs, out_specs=out_specs,
        out_shape=[jax.ShapeDtypeStruct((s, RET_W), F32), jax.ShapeDtypeStruct((nb * sub,) + hmat, F32)] * 2,
        scratch_shapes=[pltpu.VMEM(hmat, F32)] * 2,
        compiler_params=_cparams(("arbitrary",)),
    )(*args)


def _ret_bwd(logits, q, k, proj, dy, st_f, st_b):
    s = q.shape[0]
    c = RET_CHUNK
    sub = RET_SUB
    nb = s // (c * sub)
    block = (lambda n: nb - 1 - n, lambda n: n)
    order = (tuple(reversed(range(sub))), tuple(range(sub)))
    vwin = _win(proj, c * sub, C_RV, RET_W)
    nv = len(vwin)
    vw = RET_W // nv
    per = 4 + nv

    def body(lg_ref, *refs):
        ins, outs, scr = refs[:2 * per], refs[2 * per:2 * per + 8], refs[2 * per + 8:]
        n = pl.program_id(0)

        @pl.when(n == 0)
        def _():
            for r in scr:
                r[...] = jnp.zeros(r.shape, F32)

        for h in range(RET_HEADS):
            for d in range(2):
                q_ref, k_ref, dy_ref, st_ref = ins[d * per:d * per + 4]
                v_refs = ins[d * per + 4:(d + 1) * per]
                dq_ref, dk_ref, dv_ref = outs[4 * d:4 * d + 3]
                dstate, lacc = scr[2 * d], scr[2 * d + 1]
                tb = _decay_tables(lg_ref[d, h], bool(d))
                sl = slice(h * RET_HEAD_DIM, (h + 1) * RET_HEAD_DIM)
                off = h * RET_HEAD_DIM
                dsh = dstate[h]
                lsum = lacc[h]
                for u in order[d]:
                    rows = slice(u * c, (u + 1) * c)
                    qh, kh, dyh = q_ref[rows, sl], k_ref[rows, sl], dy_ref[rows, sl]
                    vb = v_refs[off // vw][rows, off % vw:off % vw + RET_HEAD_DIM].astype(BF16)
                    qb, kb, dyb = qh.astype(BF16), kh.astype(BF16), dyh.astype(BF16)
                    sh = st_ref[u, h]
                    shb, dshb = sh.astype(BF16), dsh.astype(BF16)
                    qk = _nt(qb, kb)
                    g = _nt(dyb, vb) * tb["d"]
                    a_t = _nt(kb, qb) * tb["d_t"]
                    g_t = _nt(vb, dyb) * tb["d_t"]
                    qd, kd = qh * tb["qdec"], kh * tb["kdec"]
                    dqd = _nt(dyb, shb)
                    dkd = _nt(vb, dshb)
                    dq_ref[rows, sl] = (jnp.dot(g.astype(BF16), kb, preferred_element_type=F32) + dqd * tb["qdec"]).astype(dq_ref.dtype)
                    dk_ref[rows, sl] = (jnp.dot(g_t.astype(BF16), qb, preferred_element_type=F32) + dkd * tb["kdec"]).astype(dk_ref.dtype)
                    dv_ref[rows, sl] = (jnp.dot(a_t.astype(BF16), dyb, preferred_element_type=F32)
                                        + jnp.dot(kd.astype(BF16), dshb, preferred_element_type=F32)).astype(dv_ref.dtype)
                    lsum = lsum + (tb["dist"] * qk * g + tb["e_q"] * qd * dqd + tb["e_k"] * kd * dkd
                                   + float(c) * tb["gam"] * dsh * sh)
                    dsh = tb["gam"] * dsh + jnp.dot(qd.T.astype(BF16), dyb, preferred_element_type=F32)
                dstate[h] = dsh
                lacc[h] = lsum

        @pl.when(n == nb - 1)
        def _():
            for d in range(2):
                for h in range(RET_HEADS):
                    outs[4 * d + 3][h] = jnp.zeros((8, LANES), F32) + jnp.sum(scr[2 * d + 1][h])

    hmat = (RET_HEADS, RET_HEAD_DIM, RET_HEAD_DIM)
    in_specs, out_specs, args = [pl.BlockSpec(memory_space=pltpu.SMEM)], [], [logits]
    for d, states in enumerate((st_f, st_b)):
        rows = pl.BlockSpec((c * sub, RET_W), lambda n, d=d: (block[d](n), 0))
        in_specs += ([rows, rows, rows, pl.BlockSpec((sub,) + hmat, lambda n, d=d: (block[d](n), 0, 0, 0))]
                     + [pl.BlockSpec(sp.block_shape, lambda n, d=d, cb=sp.index_map(0)[1]: (block[d](n), cb)) for _, sp in vwin])
        args += [q, k, dy, states] + [a for a, _ in vwin]
        out_specs += [rows, rows, rows, pl.BlockSpec((RET_HEADS, 8, LANES), lambda n: (0, 0, 0))]
    return pl.pallas_call(
        body, name="ret_bwd", grid=(nb,), in_specs=in_specs, out_specs=out_specs,
        out_shape=([jax.ShapeDtypeStruct((s, RET_W), BF16)] * 3 + [jax.ShapeDtypeStruct((RET_HEADS, 8, LANES), F32)]) * 2,
        scratch_shapes=[pltpu.VMEM(hmat, F32)] * 4,
        compiler_params=_cparams(("arbitrary",)),
    )(*args)


def _local_step(x, p, target, w, small):
    s = x.shape[0]
    tabs = _rope_tables(s, ATTN_HEAD_DIM) + _rope_tables(s, RET_HEAD_DIM)
    g_mix, g_mlp, g_ple = small["mix_norm"][None, :], small["mlp_norm"][None, :], small["ple_norm"][None, :]
    g_final, g_ret = small["final_norm"][None, :], small["ret_norm_gain"][None, :]
    gq_w = jnp.tile(small["attn_q_norm"], ATTN_HEADS)[None, :]
    gk_w = jnp.tile(small["attn_k_norm"], ATTN_KV_HEADS)[None, :]
    logits = small["ret_decay_logit"]

    hb = _stage_norm_in(x, g_mix)
    proj = _mm("in_proj", hb, w["w_in"], tm=512, tn=IN_W // 2, tk=1024, out_dtypes=(BF16,), j_outer=True)
    q_ct, k_rows, k_ct, v_rows, v_ct, rq, rk = _stage_qkv(proj, tabs, gq_w, gk_w)
    o_ct, lse = _attn_fwd(q_ct, k_rows, v_ct)
    ry_f, st_f, ry_b, st_b = _ret_fwd(logits, rq, rk, proj)
    rz, attn_rows = _stage_mix_post(ry_f, ry_b, proj, o_ct, g_ret)
    a_out = _mm("attn_o", attn_rows, w["w_attn_o"], tm=1024, tn=1024, tk=512, out_dtypes=(BF16,))
    r_out = _mm("ret_o", rz, w["w_ret_o"], tm=1024, tn=1024, tk=512, out_dtypes=(BF16,))
    merged = _stage_merge(proj, a_out, r_out)

    def epi_res_norm(acc, e, c):
        xr = e[0][...] + acc
        return xr, _rms_fwd(xr, c[0][...])

    x1, hm = _mm("out_proj", merged, w["w_out"], tm=512, tn=1024, tk=1024, out_dtypes=(F32, BF16),
                 epi=epi_res_norm, epi_ins=(x,), consts=(g_mlp,))

    def epi_relu2(acc, e, c):
        r = jnp.maximum(acc, 0.0)
        return (r * r,)

    act = _mm("mlp_up", hm, w["w_up"], tm=512, tn=2048, tk=1024, out_dtypes=(BF16,), epi=epi_relu2, j_outer=True)
    x2, hp = _mm("mlp_down", act, w["w_down"], tm=512, tn=1024, tk=D_FF, out_dtypes=(F32, BF16),
                 epi=epi_res_norm, epi_ins=(x1,), consts=(g_ple,))
    zg = _mm("ple_gate", hp, w["w_ple_gate"], tm=1024, tn=1024, tk=1024)
    pe = _mm("ple_emb", p, w["w_ple"], tm=1024, tn=1024, tk=256)
    dx3, dzg, dpe, loss_cols, g_final_p = _stage_head(zg, pe, x2, target, g_final)
    loss_sum = 0.5 / D_MODEL * jnp.sum(loss_cols)

    gw = {}
    gw["w_ple"] = _mm("g_w_ple", p, dpe, ta=True, tm=256, tn=1024, tk=2048)
    gw["w_ple_gate"] = _mm("g_w_ple_gate", hp, dzg, ta=True, tm=1024, tn=1024, tk=2048)
    def epi_norm_bwd(acc, e, c):
        dx, dg = _rms_bwd(acc, e[0][...], c[0][...])
        return e[1][...] + dx, dg

    def epi_norm_bwd_b(acc, e, c):
        tot, dg = epi_norm_bwd(acc, e, c)
        return tot, tot, dg

    dx2, dx2_b, g_ple_p = _mm("d_hp", dzg, w["w_ple_gate"], tb=True, tm=512, tn=1024, tk=1024, out_dtypes=(F32, BF16),
                              epi=epi_norm_bwd_b, epi_ins=(x2, dx3), consts=(g_ple,), n_sums=1)

    def epi_relu2_bwd(acc, e, c):
        return (acc * (2.0 * jnp.sqrt(e[0][...].astype(F32))),)

    du = _mm("d_u", dx2_b, w["w_down"], tb=True, tm=512, tn=2048, tk=1024, out_dtypes=(BF16,), epi=epi_relu2_bwd, epi_ins=(act,),
             j_outer=True)
    gw["w_down"] = _mm("g_w_down", act, dx2_b, ta=True, tm=1024, tn=1024, tk=2048)
    gw["w_up"] = _mm("g_w_up", hm, du, ta=True, tm=1024, tn=1024, tk=2048)
    dx1, dx1_b, g_mlp_p = _mm("d_hm", du, w["w_up"], tb=True, tm=512, tn=1024, tk=D_FF, out_dtypes=(F32, BF16),
                              epi=epi_norm_bwd_b, epi_ins=(x1, dx2), consts=(g_mlp,), n_sums=1)
    dmerged = _mm("d_merged", dx1_b, w["w_out"], tb=True, tm=1024, tn=1024, tk=1024)
    gw["w_out"] = _mm("g_w_out", merged, dx1_b, ta=True, tm=1024, tn=1024, tk=2048)
    dao, dro, dga, dgr = _stage_merge_bwd(proj, dmerged, a_out, r_out)
    gw["w_attn_o"] = _mm("g_w_attn_o", attn_rows, dao, ta=True, tm=512, tn=1024, tk=2048)
    gw["w_ret_o"] = _mm("g_w_ret_o", rz, dro, ta=True, tm=512, tn=1024, tk=2048)
    dattn = _mm("d_attn", dao, w["w_attn_o"], tb=True, tm=1024, tn=512, tk=1024)
    drz = _mm("d_rz", dro, w["w_ret_o"], tb=True, tm=1024, tn=512, tk=1024)
    do_ct, delta, dry, drg, g_ret_p = _stage_mix_post_bwd(dattn, attn_rows, drz, ry_f, ry_b, proj, g_ret)
    dq_f, dk_f, dv_f, dl_f, dq_b, dk_b, dv_b, dl_b = _ret_bwd(logits, rq, rk, proj, dry, st_f, st_b)
    dq_ct, dk8, dv8 = _attn_bwd(q_ct, do_ct, lse, delta, k_rows, v_rows, k_ct)
    dproj, gq_p, gk_p = _stage_dproj(proj, dq_ct, dk8, dv8, (dq_f, dk_f, dv_f, dq_b, dk_b, dv_b), drg, dga, dgr, tabs, gq_w, gk_w)
    gw["w_in"] = _mm("g_w_in", hb, dproj, ta=True, tm=1024, tn=IN_W // 2, tk=1024)
    grad_x, g_mix_p = _mm("d_h", dproj, w["w_in"], tb=True, tm=512, tn=1024, tk=IN_W, epi=epi_norm_bwd, epi_ins=(x, dx1),
                          consts=(g_mix,), n_sums=1)

    gs = {
        "mix_norm": g_mix_p[0], "mlp_norm": g_mlp_p[0], "ple_norm": g_ple_p[0], "final_norm": g_final_p[0],
        "ret_norm_gain": g_ret_p[0],
        "attn_q_norm": jnp.sum(gq_p[0].reshape(ATTN_HEADS, ATTN_HEAD_DIM), axis=0),
        "attn_k_norm": jnp.sum(gk_p[0].reshape(ATTN_KV_HEADS, ATTN_HEAD_DIM), axis=0),
        "ret_decay_logit": jnp.stack([dl_f[:, 0, 0], dl_b[:, 0, 0]]),
    }
    return loss_sum, grad_x, gw, gs


PACK_COLS = 1024
N_CHIPS = 4
HALF_ROWS = 2048


def _pack_shard(parts):
    return jnp.concatenate([parts[n].reshape(-1, PACK_COLS) for n, _ in BIG], axis=0)


def _unpack_shard(slab, shapes):
    out, r = {}, 0
    for n, _ in BIG:
        rows = math.prod(shapes[n]) // PACK_COLS
        out[n] = slab[r:r + rows].reshape(shapes[n])
        r += rows
    return out


def _shard_of(full, axis, sidx):
    size = full.shape[axis] // N_CHIPS
    return lax.slice_in_dim(full, sidx * size, (sidx + 1) * size, axis=axis)


def _position():
    x, y, c = lax.axis_index("x"), lax.axis_index("y"), lax.axis_index("c")
    return x, y, c


def _other_chips(x, y):
    return [(1 - x, y), (x, 1 - y), (1 - x, 1 - y)]


ANY = pl.BlockSpec(memory_space=pl.ANY)


def _gather_weights(slab):
    rows = slab.shape[0]
    half = rows // 2

    def body(in_ref, out_ref, send_sems, recv_sems):
        x, y, c = _position()
        chips = _other_chips(x, y)

        def piece(chip, core):
            return out_ref.at[2 * chip[0] + chip[1], pl.ds(core * half, half), :]

        def copy(k, chip, core, to, src=None):
            return pltpu.make_async_remote_copy(
                src_ref=piece(chip, core) if src is None else src, dst_ref=piece(chip, core),
                send_sem=send_sems.at[k], recv_sem=recv_sems.at[k], device_id=to, device_id_type=MESH)

        first = [copy(j, (x, y), c, (*chip, c), src=in_ref.at[pl.ds(c * half, half), :]) for j, chip in enumerate(chips)]
        for cp in first:
            cp.start()
        passed = [copy(3 + j, chip, c, (x, y, 1 - c)) for j, chip in enumerate(chips)]
        for j, chip in enumerate(chips):
            copy(j, chip, c, (x, y, c)).wait_recv()
            passed[j].start()
        for j, chip in enumerate(chips):
            copy(3 + j, chip, 1 - c, (x, y, c)).wait_recv()
        for cp in first + passed:
            cp.wait_send()

    return pl.pallas_call(
        body, name="gather_weights", in_specs=[ANY], out_specs=ANY,
        out_shape=jax.ShapeDtypeStruct((N_CHIPS,) + slab.shape, slab.dtype),
        scratch_shapes=[pltpu.SemaphoreType.DMA((6,)), pltpu.SemaphoreType.DMA((6,))],
    )(slab)


def _exchange_halves(g):
    def body(g_ref, out_ref, send_sem, recv_sem):
        x, y, c = _position()
        cp = pltpu.make_async_remote_copy(src_ref=g_ref.at[1 - c], dst_ref=out_ref, send_sem=send_sem, recv_sem=recv_sem,
                                          device_id=(x, y, 1 - c), device_id_type=MESH)
        cp.start()
        cp.wait()

    return pl.pallas_call(
        body, name="exchange_halves", in_specs=[ANY], out_specs=ANY,
        out_shape=jax.ShapeDtypeStruct(g.shape[1:], g.dtype),
        scratch_shapes=[pltpu.SemaphoreType.DMA, pltpu.SemaphoreType.DMA],
    )(g)


def _add_my_half(g, r1, c_idx):
    tr = 256
    nt = g.shape[2] // tr

    def body(c_ref, g_ref, r_ref, o_ref, ob_ref):
        tot = g_ref[...] + r_ref[...]
        o_ref[...] = tot
        ob_ref[...] = tot.astype(BF16)

    blk = (None, tr, PACK_COLS)
    spec = pl.BlockSpec(blk, lambda s, i, c_ref: (s, i, 0))
    return pl.pallas_call(
        body, name="add_my_half",
        grid_spec=pltpu.PrefetchScalarGridSpec(
            num_scalar_prefetch=1, grid=(N_CHIPS, nt),
            in_specs=[pl.BlockSpec((None,) + blk, lambda s, i, c_ref: (c_ref[0], s, i, 0)), spec],
            out_specs=[spec, spec]),
        out_shape=[jax.ShapeDtypeStruct(g.shape[1:], F32), jax.ShapeDtypeStruct(g.shape[1:], BF16)],
        compiler_params=_cparams(("parallel", "parallel")),
    )(c_idx, g, r1)


def _scatter_to_chips(part):
    def body(p_ref, out_ref, send_sems, recv_sems):
        x, y, c = _position()
        chips = _other_chips(x, y)
        sends = [pltpu.make_async_remote_copy(
            src_ref=p_ref.at[2 * chip[0] + chip[1]], dst_ref=out_ref.at[j], send_sem=send_sems.at[j], recv_sem=recv_sems.at[j],
            device_id=(*chip, c), device_id_type=MESH) for j, chip in enumerate(chips)]
        for cp in sends:
            cp.start()
        for cp in sends:
            cp.wait()

    return pl.pallas_call(
        body, name="scatter_to_chips", in_specs=[ANY], out_specs=ANY,
        out_shape=jax.ShapeDtypeStruct((N_CHIPS - 1,) + part.shape[1:], part.dtype),
        scratch_shapes=[pltpu.SemaphoreType.DMA((3,)), pltpu.SemaphoreType.DMA((3,))],
    )(part)


def _sum_chips(part, r2, chip_idx):
    tr = 256

    def body(c_ref, p_ref, r_ref, o_ref):
        o_ref[...] = ((p_ref[...] + r_ref[0]) + r_ref[1]) + r_ref[2]

    return pl.pallas_call(
        body, name="sum_chips",
        grid_spec=pltpu.PrefetchScalarGridSpec(
            num_scalar_prefetch=1, grid=(r2.shape[1] // tr,),
            in_specs=[pl.BlockSpec((None, tr, PACK_COLS), lambda i, c_ref: (c_ref[0], i, 0)),
                      pl.BlockSpec((N_CHIPS - 1, tr, PACK_COLS), lambda i, c_ref: (0, i, 0))],
            out_specs=pl.BlockSpec((tr, PACK_COLS), lambda i, c_ref: (i, 0))),
        out_shape=jax.ShapeDtypeStruct(r2.shape[1:], F32),
        compiler_params=_cparams(("parallel",)),
    )(chip_idx, part, r2)


def _join_halves(red):
    def body(r_ref, out_ref, send_sem, recv_sem):
        x, y, c = _position()
        cp = pltpu.make_async_remote_copy(src_ref=r_ref, dst_ref=out_ref, send_sem=send_sem, recv_sem=recv_sem,
                                          device_id=(x, y, 1 - c), device_id_type=MESH)
        cp.start()
        cp.wait()

    return pl.pallas_call(
        body, name="join_halves", in_specs=[ANY], out_specs=ANY,
        out_shape=jax.ShapeDtypeStruct(red.shape, red.dtype),
        scratch_shapes=[pltpu.SemaphoreType.DMA, pltpu.SemaphoreType.DMA],
    )(red)


def _adamw_math(w, g, m, v):
    m = ADAM_B1 * m + (1.0 - ADAM_B1) * g
    v = ADAM_B2 * v + (1.0 - ADAM_B2) * (g * g)
    m_hat = m / (1.0 - ADAM_B1 ** ADAM_STEP)
    v_hat = v / (1.0 - ADAM_B2 ** ADAM_STEP)
    delta = -ADAM_LR * (m_hat / (jnp.sqrt(v_hat) + ADAM_EPS) + ADAM_WD * w)
    return delta, m, v


def _adamw(name, w, g, m, v):
    tr = min(256, w.shape[0])

    def body(w_ref, g_ref, m_ref, v_ref, d_ref, nm_ref, nv_ref):
        d_ref[...], nm_ref[...], nv_ref[...] = _adamw_math(w_ref[...], g_ref[...], m_ref[...], v_ref[...])

    blk = pl.BlockSpec((tr, w.shape[1]), lambda i: (i, 0))
    return pl.pallas_call(
        body, name="adamw_" + name, grid=(w.shape[0] // tr,), in_specs=[blk] * 4, out_specs=[blk] * 3,
        out_shape=[jax.ShapeDtypeStruct(w.shape, F32)] * 3, compiler_params=_cparams(("parallel",)),
    )(w, g, m, v)


def _small_step(gpk, wpk, mpk, vpk):
    row, col, width = SMALL["ret_decay_logit"]

    def body(g_ref, w_ref, m_ref, v_ref, og_ref, od_ref, om_ref, ov_ref, gbuf, send_sems, recv_sems):
        x, y, c = _position()
        me = 4 * x + 2 * y + c
        gbuf[me] = g_ref[...]
        sends = []
        for k in range(1, 8):
            to = (x ^ (k >> 2), y ^ ((k >> 1) & 1), c ^ (k & 1))
            cp = pltpu.make_async_remote_copy(src_ref=g_ref, dst_ref=gbuf.at[me], send_sem=send_sems.at[k - 1],
                                              recv_sem=recv_sems.at[k - 1], device_id=to, device_id_type=MESH)
            cp.start()
            sends.append(cp)
        for k in range(1, 8):
            frm = me ^ k
            pltpu.make_async_remote_copy(src_ref=g_ref, dst_ref=gbuf.at[frm], send_sem=send_sems.at[k - 1],
                                         recv_sem=recv_sems.at[k - 1], device_id=(x, y, c), device_id_type=MESH).wait_recv()
        for cp in sends:
            cp.wait_send()
        tot = gbuf[0]
        for d in range(1, 8):
            tot = tot + gbuf[d]
        w = w_ref[...]
        r_i = lax.broadcasted_iota(jnp.int32, w.shape, 0)
        c_i = lax.broadcasted_iota(jnp.int32, w.shape, 1)
        is_logit = (r_i == row) & (c_i >= col) & (c_i < col + width)
        g = jnp.where(is_logit, tot * _sigmoid(-w), tot)
        og_ref[...] = g
        od_ref[...], om_ref[...], ov_ref[...] = _adamw_math(w, g, m_ref[...], v_ref[...])

    vm = pl.BlockSpec(memory_space=pltpu.VMEM)
    shp = jax.ShapeDtypeStruct(gpk.shape, F32)
    return pl.pallas_call(
        body, name="small_step", in_specs=[vm] * 4, out_specs=[vm] * 4, out_shape=[shp] * 4,
        scratch_shapes=[pltpu.VMEM((8,) + gpk.shape, F32), pltpu.SemaphoreType.DMA((7,)), pltpu.SemaphoreType.DMA((7,))],
    )(gpk, wpk, mpk, vpk)


def _pack_small(parts):
    rows = [[] for _ in range(SMALL_ROWS)]
    for n, (r, col, width) in sorted(SMALL.items(), key=lambda kv: (kv[1][0], kv[1][1])):
        rows[r].append((col, parts[n].reshape(-1).astype(F32)))
    out = []
    for r in range(SMALL_ROWS):
        segs, pos = [], 0
        for col, vec in rows[r]:
            assert col == pos
            segs.append(vec)
            pos += vec.shape[0]
        if pos < PACK_COLS:
            segs.append(jnp.zeros((PACK_COLS - pos,), F32))
        out.append(jnp.concatenate(segs))
    return jnp.stack(out)


def _unpack_small(pk, shapes):
    return {n: pk[r, col:col + width].reshape(shapes[n]) for n, (r, col, width) in SMALL.items()}


WEIGHTS = ("mix_norm", "w_in", "attn_q_norm", "attn_k_norm", "ret_decay_logit", "ret_norm_gain", "w_attn_o", "w_ret_o", "w_out",
           "mlp_norm", "w_up", "w_down", "ple_norm", "w_ple_gate", "w_ple", "final_norm")


def kernel(x, p, mix_norm, w_in, attn_q_norm, attn_k_norm, ret_decay_logit, ret_norm_gain, w_attn_o, w_ret_o, w_out, mlp_norm, w_up, w_down, ple_norm, w_ple_gate, w_ple, final_norm, loss_target, m_mix_norm, m_w_in, m_attn_q_norm, m_attn_k_norm, m_ret_decay_logit, m_ret_norm_gain, m_w_attn_o, m_w_ret_o, m_w_out, m_mlp_norm, m_w_up, m_w_down, m_ple_norm, m_w_ple_gate, m_w_ple, m_final_norm, v_mix_norm, v_w_in, v_attn_q_norm, v_attn_k_norm, v_ret_decay_logit, v_ret_norm_gain, v_w_attn_o, v_w_ret_o, v_w_out, v_mlp_norm, v_w_up, v_w_down, v_ple_norm, v_w_ple_gate, v_w_ple, v_final_norm):
    args = dict(locals())
    wts = {n: args[n] for n in WEIGHTS}
    ms = {n: args["m_" + n] for n in WEIGHTS}
    vs = {n: args["v_" + n] for n in WEIGHTS}
    shapes = {n: wts[n].shape for n in WEIGHTS}
    big_names = [n for n, _ in BIG]
    xi, yi, ci = _position()
    c_idx = ci.astype(jnp.int32).reshape(1)

    chip_idx = (2 * xi + yi).astype(jnp.int32)
    slab_b = _pack_shard({n: wts[n][0].astype(BF16) for n in big_names})
    gathered = lax.dynamic_update_slice(_gather_weights(slab_b), slab_b[None], (chip_idx, 0, 0))
    full = {}
    for n, axis in BIG:
        per_chip = [_unpack_shard(gathered[k], {m_: shapes[m_][1:] for m_ in big_names})[n] for k in range(N_CHIPS)]
        full[n] = jnp.concatenate(per_chip, axis=axis)
    small = {n: wts[n].reshape(wts[n].shape[1:] if wts[n].ndim > 1 else wts[n].shape) for n in SMALL}

    loss_part, grad_x, gw, gs = _local_step(x[0], p[0, 0], loss_target[0], full, small)
    loss = lax.psum(loss_part, ("x", "y", "c"))

    slabs = jnp.stack([_pack_shard({n: _shard_of(gw[n], axis, k) for n, axis in BIG}) for k in range(N_CHIPS)])
    halves = slabs.reshape(N_CHIPS, 2, HALF_ROWS, PACK_COLS).transpose(1, 0, 2, 3)
    chip_part, chip_part_b = _add_my_half(halves, _exchange_halves(halves), c_idx)
    mine = _sum_chips(chip_part, _scatter_to_chips(chip_part_b), chip_idx.reshape(1))
    both = jnp.stack([mine, _join_halves(mine)])
    reduced = jnp.where(ci == 0, both, both[::-1]).reshape(2 * HALF_ROWS, PACK_COLS)
    g_big = _unpack_shard(reduced, {n: shapes[n][1:] for n in big_names})
    big_out = [{}, {}, {}, {}]
    for n in big_names:
        big_out[0][n] = g_big[n][None]
        for kind, a in enumerate(_adamw(n, wts[n][0], g_big[n], ms[n][0], vs[n][0])):
            big_out[kind + 1][n] = a[None]

    sm_out = _small_step(_pack_small(gs), _pack_small({n: wts[n] for n in SMALL}), _pack_small({n: ms[n] for n in SMALL}),
                         _pack_small({n: vs[n] for n in SMALL}))
    small_out = [_unpack_small(a, {n: shapes[n] for n in SMALL}) for a in sm_out]

    outs = [loss, grad_x[None]]
    for kind in range(4):
        for n in WEIGHTS:
            outs.append(small_out[kind][n] if n in SMALL else big_out[kind][n])
    return tuple(outs)
```

```python
import functools
import math

import jax
import jax.numpy as jnp
from jax import lax
from jax.experimental import pallas as pl
from jax.experimental.pallas import tpu as pltpu

F32 = jnp.float32
BF16 = jnp.bfloat16
MESH = pl.DeviceIdType.MESH

D_MODEL = 1024
PLE_DIM = 256
GRID_W = 64
ATTN_HEAD_DIM = 64
ATTN_HEADS = 8
ATTN_KV_HEADS = 2
ATTN_GROUP = ATTN_HEADS // ATTN_KV_HEADS
RET_HEAD_DIM = 128
RET_HEADS = 4
ATTN_Q_W = 512
ATTN_KV_W = 128
RET_W = 512
IN_W = 4864
D_FF = 4096
RET_CHUNK = 128
ROPE_THETA = 10000.0
NORM_EPS = 1e-6
GN_EPS = 1e-5
ATTN_SCALE = ATTN_HEAD_DIM ** -0.5
LOG2E = math.log2(math.e)
Q_FOLD = ATTN_SCALE * LOG2E
RET_SCALE = RET_HEAD_DIM ** -0.5

C_AQ, C_AK, C_AV, C_RQ, C_RK, C_RV, C_RG, C_GA, C_GR = 0, 512, 640, 768, 1280, 1792, 2304, 2816, 3840

ADAM_LR = 0.001
ADAM_B1 = 0.9
ADAM_B2 = 0.999
ADAM_EPS = 1e-08
ADAM_WD = 0.01
ADAM_STEP = 10

LANES = 128
VMEM_LIMIT = 56 << 20
SEQ_TILE = 512

BIG = (("w_in", 1), ("w_attn_o", 1), ("w_ret_o", 1), ("w_out", 0), ("w_up", 1), ("w_down", 0), ("w_ple_gate", 0), ("w_ple", 1))
SMALL_ROWS = 8
SMALL = {"mix_norm": (0, 0, 1024), "mlp_norm": (1, 0, 1024), "ple_norm": (2, 0, 1024), "final_norm": (3, 0, 1024),
         "ret_norm_gain": (4, 0, 512), "attn_q_norm": (4, 512, 64), "attn_k_norm": (4, 576, 64), "ret_decay_logit": (4, 640, 8)}


def _seq_tile(s):
    return min(SEQ_TILE, s // 2)


def _cparams(sem=None, vmem=VMEM_LIMIT):
    return pltpu.CompilerParams(dimension_semantics=sem, vmem_limit_bytes=vmem)


def _mm(name, a, b, *, ta=False, tb=False, tm, tn, tk, out_dtypes=(F32,), epi=None, epi_ins=(), consts=(), n_sums=0, j_outer=False):
    if ta:
        kdim, m = a.shape
    else:
        m, kdim = a.shape
    n = b.shape[0] if tb else b.shape[1]
    tm, tn, tk = min(tm, m), min(tn, n), min(tk, kdim)
    assert m % tm == 0 and n % tn == 0 and kdim % tk == 0, (name, m, n, kdim, tm, tn, tk)
    nk = kdim // tk
    n_e, n_c, n_o = len(epi_ins), len(consts), len(out_dtypes)
    assert n_sums == 0 or tn == n

    def body(*refs):
        a_ref, b_ref = refs[0], refs[1]
        e_refs = refs[2:2 + n_e]
        c_refs = refs[2 + n_e:2 + n_e + n_c]
        o_refs = refs[2 + n_e + n_c:2 + n_e + n_c + n_o]
        s_refs = refs[2 + n_e + n_c + n_o:2 + n_e + n_c + n_o + n_sums]
        acc_ref = refs[2 + n_e + n_c + n_o + n_sums] if nk > 1 else None
        k = pl.program_id(2)
        if n_sums:
            @pl.when((pl.program_id(1 if j_outer else 0) == 0) & (k == 0))
            def _():
                for r in s_refs:
                    r[...] = jnp.zeros(r.shape, F32)
        av = a_ref[...].astype(BF16)
        bv = b_ref[...].astype(BF16)
        dims = (((0,) if ta else (1,), (1,) if tb else (0,)), ((), ()))
        part = lax.dot_general(av, bv, dims, preferred_element_type=F32)

        def finish(acc):
            vals = epi(acc, e_refs, c_refs) if epi is not None else (acc,)
            for o_ref, v in zip(o_refs, vals[:n_o]):
                o_ref[...] = v.astype(o_ref.dtype)
            for s_ref, v in zip(s_refs, vals[n_o:]):
                _acc_add(s_ref, v)

        if nk == 1:
            finish(part)
        else:
            @pl.when(k == 0)
            def _():
                acc_ref[...] = part

            @pl.when(k > 0)
            def _():
                acc_ref[...] += part

            @pl.when(k == nk - 1)
            def _():
                finish(acc_ref[...])

    def spec(shape, index):
        return pl.BlockSpec(shape, (lambda j, i, k: index(i, j, k)) if j_outer else index)

    a_spec = spec((tk, tm), lambda i, j, k: (k, i)) if ta else spec((tm, tk), lambda i, j, k: (i, k))
    b_spec = spec((tn, tk), lambda i, j, k: (j, k)) if tb else spec((tk, tn), lambda i, j, k: (k, j))
    o_spec = spec((tm, tn), lambda i, j, k: (i, j))
    c_specs = [spec(c.shape, lambda i, j, k, nd=c.ndim: (0,) * nd) for c in consts]
    outs = pl.pallas_call(
        body, name=name,
        grid=(n // tn, m // tm, nk) if j_outer else (m // tm, n // tn, nk),
        in_specs=[a_spec, b_spec] + [o_spec] * n_e + c_specs,
        out_specs=[o_spec] * n_o + [spec((8, n), lambda i, j, k: (0, 0))] * n_sums,
        out_shape=[jax.ShapeDtypeStruct((m, n), dt) for dt in out_dtypes] + [jax.ShapeDtypeStruct((8, n), F32)] * n_sums,
        scratch_shapes=[pltpu.VMEM((tm, tn), F32)] if nk > 1 else [],
        compiler_params=_cparams(("arbitrary",) * 3 if n_sums else ("parallel", "parallel", "arbitrary")),
    )(a, b, *epi_ins, *consts)
    return outs[0] if n_o + n_sums == 1 else outs


def _rows(arr, tr):
    return (arr, pl.BlockSpec((tr, arr.shape[1]), lambda i: (i, 0)))


def _win(arr, tr, start, width):
    bw = math.gcd(start, width) if start else width
    assert bw % LANES == 0
    return [(arr, pl.BlockSpec((tr, bw), lambda i, cb=start // bw + p: (i, cb))) for p in range(width // bw)]


def _ct(arr):
    return (arr, pl.BlockSpec((None,) + arr.shape[1:], lambda i: (i, 0, 0)))


def _whole(arr):
    return (arr, pl.BlockSpec(arr.shape, lambda i, nd=arr.ndim: (0,) * nd))


def _cat(refs):
    vals = [r[...].astype(F32) for r in refs]
    return vals[0] if len(vals) == 1 else jnp.concatenate(vals, axis=1)


def _seqtiled(name, fn, n_tiles, ins, outs, acc_widths=()):
    n_i, n_o, n_a = len(ins), len(outs), len(acc_widths)

    def body(*refs):
        i_refs, o_refs, a_refs = refs[:n_i], refs[n_i:n_i + n_o], refs[n_i + n_o:]
        if n_a:
            @pl.when(pl.program_id(0) == 0)
            def _():
                for r in a_refs:
                    r[...] = jnp.zeros(r.shape, F32)
        fn(list(i_refs), list(o_refs), list(a_refs))

    res = pl.pallas_call(
        body, name=name, grid=(n_tiles,),
        in_specs=[s for _, s in ins],
        out_specs=[s for _, _, s in outs] + [pl.BlockSpec((8, w), lambda i: (0, 0)) for w in acc_widths],
        out_shape=[jax.ShapeDtypeStruct(sh, dt) for sh, dt, _ in outs] + [jax.ShapeDtypeStruct((8, w), F32) for w in acc_widths],
        compiler_params=_cparams(("arbitrary",)),
    )(*[a for a, _ in ins])
    return res


def _acc_add(acc_ref, val):
    acc_ref[0:1, :] += jnp.sum(val, axis=0, keepdims=True)


def _out_rows(s, w, dt, tr):
    return ((s, w), dt, pl.BlockSpec((tr, w), lambda i: (i, 0)))


def _out_ct(s, w, dt, t):
    return ((s // t, w, t), dt, pl.BlockSpec((None, w, t), lambda i: (i, 0, 0)))


def _rms_fwd(x, gain):
    r = lax.rsqrt(jnp.mean(x * x, axis=-1, keepdims=True) + NORM_EPS)
    return x * r * gain


def _rms_bwd(dy, x, gain):
    r = lax.rsqrt(jnp.mean(x * x, axis=-1, keepdims=True) + NORM_EPS)
    xn = x * r
    dyg = dy * gain
    dx = r * (dyg - xn * jnp.mean(dyg * xn, axis=-1, keepdims=True))
    return dx, dy * xn


def _seg_mean(y, hd):
    w = y.shape[1]
    pieces = []
    for s in range(0, w, LANES):
        v = y[:, s:s + LANES]
        tot = jnp.sum(v, axis=1, keepdims=True)
        if hd == LANES:
            pieces.append(jnp.broadcast_to(tot, v.shape))
        else:
            low = lax.broadcasted_iota(jnp.int32, v.shape, 1) < hd
            lo = jnp.sum(jnp.where(low, v, 0.0), axis=1, keepdims=True)
            pieces.append(jnp.where(low, lo, tot - lo))
    out = pieces[0] if len(pieces) == 1 else jnp.concatenate(pieces, axis=1)
    return out * (1.0 / hd)


def _tile_lanes(t, w):
    return t if w == t.shape[1] else jnp.concatenate([t] * (w // t.shape[1]), axis=1)


def _swap_halves(x, hd):
    w = x.shape[1]
    half = hd // 2
    lane = lax.broadcasted_iota(jnp.int32, x.shape, 1)
    return jnp.where((lane % hd) < half, pltpu.roll(x, w - half, 1), pltpu.roll(x, half, 1))


def _rope(x, cos, sin_signed, hd):
    w = x.shape[1]
    return x * _tile_lanes(cos, w) + _swap_halves(x, hd) * _tile_lanes(sin_signed, w)


def _rope_t(dy, cos, sin_signed, hd):
    w = dy.shape[1]
    return dy * _tile_lanes(cos, w) + _swap_halves(dy * _tile_lanes(sin_signed, w), hd)


def _headnorm_fwd(x, gain_w, hd):
    r = lax.rsqrt(_seg_mean(x * x, hd) + NORM_EPS)
    return x * r * gain_w


def _headnorm_bwd(dy, x, gain_w, hd):
    r = lax.rsqrt(_seg_mean(x * x, hd) + NORM_EPS)
    xn = x * r
    dyg = dy * gain_w
    return r * (dyg - xn * _seg_mean(dyg * xn, hd)), dy * xn


def _sigmoid(x):
    return 1.0 / (1.0 + jnp.exp(-x))


def _rope_tables(seq_len, head_dim):
    rows = seq_len // GRID_W
    n_axis = head_dim // 4
    freqs = ROPE_THETA ** (-jnp.arange(n_axis, dtype=F32) / n_axis)
    ang_r = jnp.arange(rows, dtype=F32)[:, None] * freqs
    ang_c = jnp.arange(GRID_W, dtype=F32)[:, None] * freqs

    def expand(by_row, by_col):
        r = jnp.broadcast_to(by_row[:, None, :], (rows, GRID_W, n_axis))
        c = jnp.broadcast_to(by_col[None, :, :], (rows, GRID_W, n_axis))
        return jnp.concatenate([r, c], axis=-1).reshape(seq_len, 2 * n_axis)

    cos, sin = expand(jnp.cos(ang_r), jnp.cos(ang_c)), expand(jnp.sin(ang_r), jnp.sin(ang_c))
    reps = LANES // head_dim
    return jnp.tile(jnp.concatenate([cos, cos], axis=-1), (1, reps)), jnp.tile(jnp.concatenate([-sin, sin], axis=-1), (1, reps))


def _stage_norm_in(x, gain):
    s = x.shape[0]
    tr = min(SEQ_TILE, s)

    def fn(i, o, a):
        o[0][...] = _rms_fwd(i[0][...], i[1][...]).astype(BF16)

    return _seqtiled("norm_in", fn, s // tr, [_rows(x, tr), _whole(gain)], [_out_rows(s, D_MODEL, BF16, tr)])[0]


def _stage_qkv(proj, tabs, gq_w, gk_w):
    s = proj.shape[0]
    t = _seq_tile(s)
    ca, sa, cr, sr = tabs
    ins = (_win(proj, t, C_AQ, ATTN_Q_W) + _win(proj, t, C_AK, ATTN_KV_W) + _win(proj, t, C_AV, ATTN_KV_W)
           + _win(proj, t, C_RQ, RET_W) + _win(proj, t, C_RK, RET_W)
           + [_rows(ca, t), _rows(sa, t), _rows(cr, t), _rows(sr, t), _whole(gq_w), _whole(gk_w)])

    def fn(i, o, a):
        aq, ak, av = (i[n][...].astype(F32) for n in range(3))
        rq, rk = _cat(i[3:5]), _cat(i[5:7])
        ca_, sa_, cr_, sr_ = i[7][...], i[8][...], i[9][...], i[10][...]
        qr = _rope(_headnorm_fwd(aq, i[11][...], ATTN_HEAD_DIM), ca_, sa_, ATTN_HEAD_DIM) * Q_FOLD
        kr = _rope(_headnorm_fwd(ak, i[12][...], ATTN_HEAD_DIM), ca_, sa_, ATTN_HEAD_DIM)
        qt = qr.T.astype(BF16)
        zeros = jnp.zeros((ATTN_HEAD_DIM, t), BF16)
        for h in range(ATTN_HEADS):
            g = h // ATTN_GROUP
            blk = qt[h * ATTN_HEAD_DIM:(h + 1) * ATTN_HEAD_DIM, :]
            o[0][h * LANES + g * ATTN_HEAD_DIM:h * LANES + (g + 1) * ATTN_HEAD_DIM, :] = blk
            o[0][h * LANES + (1 - g) * ATTN_HEAD_DIM:h * LANES + (2 - g) * ATTN_HEAD_DIM, :] = zeros
        o[1][...] = kr.astype(BF16)
        o[2][...] = kr.T.astype(BF16)
        o[3][...] = av.astype(BF16)
        o[4][...] = av.T.astype(BF16)
        o[5][...] = _rope(rq, cr_, sr_, RET_HEAD_DIM) * RET_SCALE
        o[6][...] = _rope(rk, cr_, sr_, RET_HEAD_DIM)

    outs = [_out_ct(s, ATTN_HEADS * LANES, BF16, t), _out_rows(s, ATTN_KV_W, BF16, t), _out_ct(s, ATTN_KV_W, BF16, t),
            _out_rows(s, ATTN_KV_W, BF16, t), _out_ct(s, ATTN_KV_W, BF16, t), _out_rows(s, RET_W, F32, t), _out_rows(s, RET_W, F32, t)]
    return _seqtiled("qkv_prep", fn, s // t, ins, outs)


def _groupnorm_gate(ry, rg, gain):
    mu = _seg_mean(ry, RET_HEAD_DIM)
    d = ry - mu
    rs = lax.rsqrt(_seg_mean(d * d, RET_HEAD_DIM) + GN_EPS)
    return d * rs, rs, _sigmoid(rg)


def _stage_mix_post(ry_f, ry_b, proj, o_ct, gain):
    s = proj.shape[0]
    t = _seq_tile(s)
    ins = [_rows(ry_f, t), _rows(ry_b, t)] + _win(proj, t, C_RG, RET_W) + [_ct(o_ct), _whole(gain)]

    def fn(i, o, a):
        ry = i[0][...] + i[1][...]
        rg = _cat(i[2:4])
        gn, _, sg = _groupnorm_gate(ry, rg, None)
        o[0][...] = (gn * i[5][...] * (rg * sg)).astype(BF16)
        o[1][...] = i[4][...].astype(F32).T.astype(BF16)

    return _seqtiled("mix_post", fn, s // t, ins, [_out_rows(s, RET_W, BF16, t), _out_rows(s, ATTN_Q_W, BF16, t)])


def _stage_merge(proj, a_out, r_out):
    s = proj.shape[0]
    tr = min(SEQ_TILE, s)
    ins = _win(proj, tr, C_GA, D_MODEL) + _win(proj, tr, C_GR, D_MODEL) + [_rows(a_out, tr), _rows(r_out, tr)]
    na = len(_win(proj, tr, C_GA, D_MODEL))

    def fn(i, o, a):
        ga, gr = _cat(i[:na]), _cat(i[na:2 * na])
        o[0][...] = (_sigmoid(ga) * i[2 * na][...] + _sigmoid(gr) * i[2 * na + 1][...]).astype(BF16)

    return _seqtiled("merge", fn, s // tr, ins, [_out_rows(s, D_MODEL, BF16, tr)])[0]


def _stage_head(zg, pe, x2, target, g_final):
    s = x2.shape[0]
    tr = min(SEQ_TILE // 2, s)
    ins = [_rows(zg, tr), _rows(pe, tr), _rows(x2, tr), _rows(target, tr), _whole(g_final)]

    def fn(i, o, a):
        gt = _sigmoid(i[0][...])
        pe_ = i[1][...]
        x3 = i[2][...] + gt * pe_
        gf = i[4][...]
        r3 = lax.rsqrt(jnp.mean(x3 * x3, axis=-1, keepdims=True) + NORM_EPS)
        x3n = x3 * r3
        e = x3n * gf - i[3][...]
        _acc_add(a[0], e * e)
        dy = e * (1.0 / D_MODEL)
        _acc_add(a[1], dy * x3n)
        dyg = dy * gf
        dx3 = r3 * (dyg - x3n * jnp.mean(dyg * x3n, axis=-1, keepdims=True))
        o[0][...] = dx3
        o[1][...] = (dx3 * pe_ * gt * (1.0 - gt)).astype(BF16)
        o[2][...] = (dx3 * gt).astype(BF16)

    outs = [_out_rows(s, D_MODEL, F32, tr), _out_rows(s, D_MODEL, BF16, tr), _out_rows(s, D_MODEL, BF16, tr)]
    return _seqtiled("head", fn, s // tr, ins, outs, acc_widths=(D_MODEL, D_MODEL))


def _stage_merge_bwd(proj, dmerged, a_out, r_out):
    s = proj.shape[0]
    tr = min(SEQ_TILE // 2, s)
    wins = _win(proj, tr, C_GA, D_MODEL)
    na = len(wins)
    ins = wins + _win(proj, tr, C_GR, D_MODEL) + [_rows(dmerged, tr), _rows(a_out, tr), _rows(r_out, tr)]

    def fn(i, o, a):
        sa, sr = _sigmoid(_cat(i[:na])), _sigmoid(_cat(i[na:2 * na]))
        dm = i[2 * na][...]
        o[0][...] = (dm * sa).astype(BF16)
        o[1][...] = (dm * sr).astype(BF16)
        o[2][...] = (dm * i[2 * na + 1][...] * sa * (1.0 - sa)).astype(BF16)
        o[3][...] = (dm * i[2 * na + 2][...] * sr * (1.0 - sr)).astype(BF16)

    return _seqtiled("merge_bwd", fn, s // tr, ins, [_out_rows(s, D_MODEL, BF16, tr)] * 4)


def _stage_mix_post_bwd(dattn, attn_rows, drz, ry_f, ry_b, proj, gain):
    s = proj.shape[0]
    t = _seq_tile(s)
    ins = ([_rows(dattn, t), _rows(attn_rows, t), _rows(drz, t), _rows(ry_f, t), _rows(ry_b, t)]
           + _win(proj, t, C_RG, RET_W) + [_whole(gain)])

    def fn(i, o, a):
        da = i[0][...]
        dat = da.T
        prod_t = (da * i[1][...].astype(F32)).T
        dat_b = dat.astype(BF16)
        zeros = jnp.zeros((ATTN_HEAD_DIM, t), BF16)
        for h in range(ATTN_HEADS):
            g = h // ATTN_GROUP
            o[0][h * LANES + g * ATTN_HEAD_DIM:h * LANES + (g + 1) * ATTN_HEAD_DIM, :] = dat_b[h * ATTN_HEAD_DIM:(h + 1) * ATTN_HEAD_DIM, :]
            o[0][h * LANES + (1 - g) * ATTN_HEAD_DIM:h * LANES + (2 - g) * ATTN_HEAD_DIM, :] = zeros
            o[1][h] = jnp.sum(prod_t[h * ATTN_HEAD_DIM:(h + 1) * ATTN_HEAD_DIM, :], axis=0, keepdims=True)
        ry = i[3][...] + i[4][...]
        rg = _cat(i[5:7])
        gain_ = i[7][...]
        gn, rs, sg = _groupnorm_gate(ry, rg, None)
        dz = i[2][...]
        silu = rg * sg
        _acc_add(a[0], dz * gn * silu)
        dgn = dz * gain_ * silu
        o[2][...] = rs * (dgn - _seg_mean(dgn, RET_HEAD_DIM) - gn * _seg_mean(dgn * gn, RET_HEAD_DIM))
        o[3][...] = (dz * gn * gain_ * (sg * (1.0 + rg * (1.0 - sg)))).astype(BF16)

    outs = [_out_ct(s, ATTN_HEADS * LANES, BF16, t),
            ((ATTN_HEADS, s // t, 1, t), F32, pl.BlockSpec((ATTN_HEADS, None, 1, t), lambda i: (0, i, 0, 0))),
            _out_rows(s, RET_W, F32, t), _out_rows(s, RET_W, BF16, t)]
    return _seqtiled("mix_post_bwd", fn, s // t, ins, outs, acc_widths=(RET_W,))


def _stage_dproj(proj, dq_ct, dk8, dv8, rgrads, drg, dga, dgr, tabs, gq_w, gk_w):
    s = proj.shape[0]
    t = _seq_tile(s)
    ca, sa, cr, sr = tabs
    kv8 = pl.BlockSpec((ATTN_HEADS, t, ATTN_KV_W), lambda i: (0, i, 0))
    ins = (_win(proj, t, C_AQ, ATTN_Q_W) + _win(proj, t, C_AK, ATTN_KV_W) + [_ct(dq_ct), (dk8, kv8), (dv8, kv8)]
           + [_rows(g, t) for g in rgrads] + [_rows(drg, t), _rows(dga, t), _rows(dgr, t)]
           + [_rows(ca, t), _rows(sa, t), _rows(cr, t), _rows(sr, t), _whole(gq_w), _whole(gk_w)])

    def fn(i, o, a):
        aq, ak = i[0][...].astype(F32), i[1][...].astype(F32)
        dq_f, dk_f, dv_f, dq_b, dk_b, dv_b = (r[...].astype(F32) for r in i[5:11])
        ca_, sa_, cr_, sr_ = i[14][...], i[15][...], i[16][...], i[17][...]
        dqn = _rope_t(i[2][...].T * ATTN_SCALE, ca_, sa_, ATTN_HEAD_DIM)
        daq, gq_rows = _headnorm_bwd(dqn, aq, i[18][...], ATTN_HEAD_DIM)
        dkn = _rope_t(jnp.sum(i[3][...].astype(F32), axis=0) * (1.0 / LOG2E), ca_, sa_, ATTN_HEAD_DIM)
        dak, gk_rows = _headnorm_bwd(dkn, ak, i[19][...], ATTN_HEAD_DIM)
        _acc_add(a[0], gq_rows)
        _acc_add(a[1], gk_rows)
        out = o[0]
        out[:, C_AQ:C_AQ + ATTN_Q_W] = daq.astype(BF16)
        out[:, C_AK:C_AK + ATTN_KV_W] = dak.astype(BF16)
        out[:, C_AV:C_AV + ATTN_KV_W] = jnp.sum(i[4][...].astype(F32), axis=0).astype(BF16)
        out[:, C_RQ:C_RQ + RET_W] = _rope_t((dq_f + dq_b) * RET_SCALE, cr_, sr_, RET_HEAD_DIM).astype(BF16)
        out[:, C_RK:C_RK + RET_W] = _rope_t(dk_f + dk_b, cr_, sr_, RET_HEAD_DIM).astype(BF16)
        out[:, C_RV:C_RV + RET_W] = (dv_f + dv_b).astype(BF16)
        out[:, C_RG:C_RG + RET_W] = i[11][...]
        out[:, C_GA:C_GA + D_MODEL] = i[12][...]
        out[:, C_GR:C_GR + D_MODEL] = i[13][...]

    return _seqtiled("dproj", fn, s // t, ins, [_out_rows(s, IN_W, BF16, t)], acc_widths=(ATTN_Q_W, ATTN_KV_W))


def _attn_fwd(q_ct, k_rows, v_ct):
    nq, _, t = q_ct.shape
    s = nq * t
    nk = nq
    assert nk % 2 == 0
    n_par = 4 if nq % 4 == 0 else 2

    def body(q_ref, k_ref, v_ref, o_ref, lse_ref, *bufs):
        sbuf = tuple(bufs[2 * w:2 * w + 2] for w in range(n_par))
        pbuf = tuple(bufs[2 * n_par + 2 * w:2 * n_par + 2 * w + 2] for w in range(n_par))

        def scores(w, j, slot):
            kj = k_ref[pl.ds(pl.multiple_of(j * t, t), t), :]
            st = jnp.dot(kj, q_ref[w], preferred_element_type=F32)
            sbuf[w][slot][...] = st
            return jnp.max(st, axis=0, keepdims=True)

        def probs(w, slot, cmax, m, l):
            m_new = jnp.maximum(m, cmax)
            alpha = jnp.exp2(m - m_new)
            pt = jnp.exp2(sbuf[w][slot][...] - m_new)
            pbuf[w][slot][...] = pt.astype(BF16)
            return m_new, alpha * l + jnp.sum(pt, axis=0, keepdims=True), alpha

        def values(w, j, slot, alpha, acc):
            return alpha * acc + jnp.dot(v_ref[j], pbuf[w][slot][...], preferred_element_type=F32)

        init = []
        for w in range(n_par):
            m = jnp.full((1, t), -1e30, F32)
            l = jnp.zeros((1, t), F32)
            cmax0 = scores(w, 0, 0)
            cmax1 = scores(w, 1, 1)
            m, l, alpha0 = probs(w, 0, cmax0, m, l)
            init.append((m, l, jnp.zeros((ATTN_HEAD_DIM, t), F32), cmax1, alpha0))

        def trip(n, carry):
            c = 2 * n
            out = []
            for w in range(n_par):
                m, l, acc, cmax_b, alpha_c = carry[w]
                acc = values(w, c, 0, alpha_c, acc)
                m, l, alpha1 = probs(w, 1, cmax_b, m, l)
                cmax2 = scores(w, c + 2, 0)
                acc = values(w, c + 1, 1, alpha1, acc)
                m, l, alpha2 = probs(w, 0, cmax2, m, l)
                cmax3 = scores(w, c + 3, 1)
                out.append((m, l, acc, cmax3, alpha2))
            return tuple(out)

        res = lax.fori_loop(0, nk // 2 - 1, trip, tuple(init))
        for w in range(n_par):
            m, l, acc, cmax_b, alpha_c = res[w]
            acc = values(w, nk - 2, 0, alpha_c, acc)
            m, l, alpha1 = probs(w, 1, cmax_b, m, l)
            acc = values(w, nk - 1, 1, alpha1, acc)
            o_ref[w] = (acc / l).astype(BF16)
            lse_ref[w] = m + jnp.log2(l)

    return pl.pallas_call(
        body, name="attn_fwd", grid=(ATTN_HEADS, nq // n_par),
        in_specs=[pl.BlockSpec((n_par, LANES, t), lambda h, i: (i, h, 0)),
                  pl.BlockSpec((s, ATTN_KV_W), lambda h, i: (0, 0)),
                  pl.BlockSpec((nk, ATTN_HEAD_DIM, t), lambda h, i: (0, h // ATTN_GROUP, 0))],
        out_specs=[pl.BlockSpec((n_par, ATTN_HEAD_DIM, t), lambda h, i: (i, h, 0)),
                   pl.BlockSpec((None, n_par, 1, t), lambda h, i: (h, i, 0, 0))],
        out_shape=[jax.ShapeDtypeStruct((nq, ATTN_Q_W, t), BF16), jax.ShapeDtypeStruct((ATTN_HEADS, nq, 1, t), F32)],
        scratch_shapes=[pltpu.VMEM((t, t), F32)] * (2 * n_par) + [pltpu.VMEM((t, t), BF16)] * (2 * n_par),
        compiler_params=_cparams(("parallel", "parallel")),
    )(q_ct, k_rows, v_ct)


def _attn_bwd(q_ct, do_ct, lse, delta, k_rows, v_rows, k_ct):
    nq, _, t = q_ct.shape
    s = nq * t
    kc = 4 if nq % 4 == 0 else 2
    tk = kc * t
    nk = nq // kc
    assert nq % 2 == 0 and nq % kc == 0

    def body(q_ref, do_ref, lse_ref, delta_ref, k_ref, v_ref, kt_ref, dq_ref, dk_ref, dv_ref, dk_acc, dv_acc,
             sb0, sb1, db0, db1, pb0, pb1, gb0, gb1):
        j = pl.program_id(1)
        sb, db, pb, gb = (sb0, sb1), (db0, db1), (pb0, pb1), (gb0, gb1)

        @pl.when(j == 0)
        def _():
            dq_ref[...] = jnp.zeros(dq_ref.shape, F32)

        kj, vj = k_ref[...], v_ref[...]
        ktj = jnp.concatenate([kt_ref[u] for u in range(kc)], axis=1)
        dk_acc[...] = jnp.zeros(dk_acc.shape, F32)
        dv_acc[...] = jnp.zeros(dv_acc.shape, F32)

        def products(i, slot):
            sb[slot][...] = jnp.dot(kj, q_ref[i], preferred_element_type=F32)
            db[slot][...] = jnp.dot(vj, do_ref[i], preferred_element_type=F32)

        def cotangents(i, slot):
            pt = jnp.exp2(sb[slot][...] - lse_ref[i])
            pb[slot][...] = pt.astype(BF16)
            gb[slot][...] = (pt * (db[slot][...] - delta_ref[i])).astype(BF16)

        def accumulate(i, slot):
            dst = gb[slot][...]
            dv_acc[...] += _nt(pb[slot][...], do_ref[i])
            dk_acc[...] += _nt(dst, q_ref[i])
            dq_ref[i] += jnp.dot(ktj, dst, preferred_element_type=F32)

        products(0, 0)
        products(1, 1)
        cotangents(0, 0)

        def trip(n, carry):
            c = 2 * n
            accumulate(c, 0)
            cotangents(c + 1, 1)
            products(c + 2, 0)
            accumulate(c + 1, 1)
            cotangents(c + 2, 0)
            products(c + 3, 1)
            return carry

        lax.fori_loop(0, nq // 2 - 1, trip, 0)
        accumulate(nq - 2, 0)
        cotangents(nq - 1, 1)
        accumulate(nq - 1, 1)
        dk_ref[...] = dk_acc[...].astype(dk_ref.dtype)
        dv_ref[...] = dv_acc[...].astype(dv_ref.dtype)

    per_head = pl.BlockSpec((nq, LANES, t), lambda h, j: (0, h, 0))
    stat = pl.BlockSpec((None, nq, 1, t), lambda h, j: (h, 0, 0, 0))
    kv_rows = pl.BlockSpec((tk, ATTN_KV_W), lambda h, j: (j, 0))
    kv_out = pl.BlockSpec((None, tk, ATTN_KV_W), lambda h, j: (h, j, 0))
    return pl.pallas_call(
        body, name="attn_bwd", grid=(ATTN_HEADS, nk),
        in_specs=[per_head, per_head, stat, stat, kv_rows, kv_rows,
                  pl.BlockSpec((kc, ATTN_HEAD_DIM, t), lambda h, j: (j, h // ATTN_GROUP, 0))],
        out_specs=[pl.BlockSpec((nq, ATTN_HEAD_DIM, t), lambda h, j: (0, h, 0)), kv_out, kv_out],
        out_shape=[jax.ShapeDtypeStruct((nq, ATTN_Q_W, t), F32), jax.ShapeDtypeStruct((ATTN_HEADS, s, ATTN_KV_W), BF16),
                   jax.ShapeDtypeStruct((ATTN_HEADS, s, ATTN_KV_W), BF16)],
        scratch_shapes=([pltpu.VMEM((tk, ATTN_KV_W), F32)] * 2 + [pltpu.VMEM((tk, t), F32)] * 4 + [pltpu.VMEM((tk, t), BF16)] * 4),
        compiler_params=_cparams(("parallel", "arbitrary")),
    )(q_ct, do_ct, lse, delta, k_rows, v_rows, k_ct)


def _log_sigmoid(x):
    t = jnp.exp(-jnp.abs(x))
    log1p_t = jnp.where(t < 1e-2, t * (1.0 - t * (0.5 - t * (1.0 / 3.0))), jnp.log(1.0 + t))
    return jnp.minimum(x, 0.0) - log1p_t


def _decay_tables(logit, backward):
    c = RET_CHUNK
    lam = _log_sigmoid(jnp.full((c, c), logit, F32))
    ii = lax.broadcasted_iota(jnp.int32, (c, c), 0).astype(F32)
    jj = lax.broadcasted_iota(jnp.int32, (c, c), 1).astype(F32)
    if not backward:
        dist, dist_t = jnp.maximum(ii - jj, 0.0), jnp.maximum(jj - ii, 0.0)
        mask, mask_t = ii >= jj, jj >= ii
        e_q, e_k = ii + 1.0, (c - 1.0) - ii
    else:
        dist, dist_t = jnp.maximum(jj - ii, 0.0), jnp.maximum(ii - jj, 0.0)
        mask, mask_t = jj > ii, ii > jj
        e_q, e_k = c - ii, ii
    return dict(
        d=jnp.where(mask, jnp.exp(lam * dist), 0.0), d_t=jnp.where(mask_t, jnp.exp(lam * dist_t), 0.0), dist=dist,
        qdec=jnp.exp(lam * e_q), kdec=jnp.exp(lam * e_k), e_q=e_q, e_k=e_k, gam=jnp.exp(lam * c))


def _nt(a, b):
    return lax.dot_general(a, b, (((1,), (1,)), ((), ())), preferred_element_type=F32)


RET_SUB = 2


def _ret_fwd(logits, q, k, proj):
    s = q.shape[0]
    c = RET_CHUNK
    sub = RET_SUB
    nb = s // (c * sub)
    block = (lambda n: n, lambda n: nb - 1 - n)
    order = (tuple(range(sub)), tuple(reversed(range(sub))))
    vwin = _win(proj, c * sub, C_RV, RET_W)
    nv = len(vwin)
    vw = RET_W // nv
    per = 2 + nv

    def body(lg_ref, *refs):
        ins, outs, states = refs[:2 * per], refs[2 * per:2 * per + 4], refs[2 * per + 4:]

        @pl.when(pl.program_id(0) == 0)
        def _():
            for st in states:
                st[...] = jnp.zeros(st.shape, F32)

        for h in range(RET_HEADS):
            for d in range(2):
                q_ref, k_ref, v_refs = ins[d * per], ins[d * per + 1], ins[d * per + 2:(d + 1) * per]
                y_ref, st_ref, state = outs[2 * d], outs[2 * d + 1], states[d]
                tb = _decay_tables(lg_ref[d, h], bool(d))
                sl = slice(h * RET_HEAD_DIM, (h + 1) * RET_HEAD_DIM)
                off = h * RET_HEAD_DIM
                sh = state[h]
                for u in order[d]:
                    rows = slice(u * c, (u + 1) * c)
                    qh, kh = q_ref[rows, sl], k_ref[rows, sl]
                    vb = v_refs[off // vw][rows, off % vw:off % vw + RET_HEAD_DIM].astype(BF16)
                    a = _nt(qh.astype(BF16), kh.astype(BF16)) * tb["d"]
                    st_ref[u, h] = sh
                    y_ref[rows, sl] = (jnp.dot(a.astype(BF16), vb, preferred_element_type=F32)
                                       + jnp.dot((qh * tb["qdec"]).astype(BF16), sh.astype(BF16), preferred_element_type=F32))
                    sh = tb["gam"] * sh + jnp.dot((kh * tb["kdec"]).T.astype(BF16), vb, preferred_element_type=F32)
                state[h] = sh

    hmat = (RET_HEADS, RET_HEAD_DIM, RET_HEAD_DIM)
    in_specs, out_specs, args = [pl.BlockSpec(memory_space=pltpu.SMEM)], [], [logits]
    for d in range(2):
        rows = pl.BlockSpec((c * sub, RET_W), lambda n, d=d: (block[d](n), 0))
        in_specs += [rows, rows] + [pl.BlockSpec(sp.block_shape, lambda n, d=d, cb=sp.index_map(0)[1]: (block[d](n), cb)) for _, sp in vwin]
        args += [q, k] + [a for a, _ in vwin]
        out_specs += [rows, pl.BlockSpec((sub,) + hmat, lambda n, d=d: (block[d](n), 0, 0, 0))]
    return pl.pallas_call(
        body, name="ret_fwd", grid=(nb,), in_specs=in_specs, out_specs=out_specs,
        out_shape=[jax.ShapeDtypeStruct((s, RET_W), F32), jax.ShapeDtypeStruct((nb * sub,) + hmat, F32)] * 2,
        scratch_shapes=[pltpu.VMEM(hmat, F32)] * 2,
        compiler_params=_cparams(("arbitrary",)),
    )(*args)


def _ret_bwd(logits, q, k, proj, dy, st_f, st_b):
    s = q.shape[0]
    c = RET_CHUNK
    sub = RET_SUB
    nb = s // (c * sub)
    block = (lambda n: nb - 1 - n, lambda n: n)
    order = (tuple(reversed(range(sub))), tuple(range(sub)))
    vwin = _win(proj, c * sub, C_RV, RET_W)
    nv = len(vwin)
    vw = RET_W // nv
    per = 4 + nv

    def body(lg_ref, *refs):
        ins, outs, scr = refs[:2 * per], refs[2 * per:2 * per + 8], refs[2 * per + 8:]
        n = pl.program_id(0)

        @pl.when(n == 0)
        def _():
            for r in scr:
                r[...] = jnp.zeros(r.shape, F32)

        for h in range(RET_HEADS):
            for d in range(2):
                q_ref, k_ref, dy_ref, st_ref = ins[d * per:d * per + 4]
                v_refs = ins[d * per + 4:(d + 1) * per]
                dq_ref, dk_ref, dv_ref = outs[4 * d:4 * d + 3]
                dstate, lacc = scr[2 * d], scr[2 * d + 1]
                tb = _decay_tables(lg_ref[d, h], bool(d))
                sl = slice(h * RET_HEAD_DIM, (h + 1) * RET_HEAD_DIM)
                off = h * RET_HEAD_DIM
                dsh = dstate[h]
                lsum = lacc[h]
                for u in order[d]:
                    rows = slice(u * c, (u + 1) * c)
                    qh, kh, dyh = q_ref[rows, sl], k_ref[rows, sl], dy_ref[rows, sl]
                    vb = v_refs[off // vw][rows, off % vw:off % vw + RET_HEAD_DIM].astype(BF16)
                    qb, kb, dyb = qh.astype(BF16), kh.astype(BF16), dyh.astype(BF16)
                    sh = st_ref[u, h]
                    shb, dshb = sh.astype(BF16), dsh.astype(BF16)
                    qk = _nt(qb, kb)
                    g = _nt(dyb, vb) * tb["d"]
                    a_t = _nt(kb, qb) * tb["d_t"]
                    g_t = _nt(vb, dyb) * tb["d_t"]
                    qd, kd = qh * tb["qdec"], kh * tb["kdec"]
                    dqd = _nt(dyb, shb)
                    dkd = _nt(vb, dshb)
                    dq_ref[rows, sl] = (jnp.dot(g.astype(BF16), kb, preferred_element_type=F32) + dqd * tb["qdec"]).astype(dq_ref.dtype)
                    dk_ref[rows, sl] = (jnp.dot(g_t.astype(BF16), qb, preferred_element_type=F32) + dkd * tb["kdec"]).astype(dk_ref.dtype)
                    dv_ref[rows, sl] = (jnp.dot(a_t.astype(BF16), dyb, preferred_element_type=F32)
                                        + jnp.dot(kd.astype(BF16), dshb, preferred_element_type=F32)).astype(dv_ref.dtype)
                    lsum = lsum + (tb["dist"] * qk * g + tb["e_q"] * qd * dqd + tb["e_k"] * kd * dkd
                                   + float(c) * tb["gam"] * dsh * sh)
                    dsh = tb["gam"] * dsh + jnp.dot(qd.T.astype(BF16), dyb, preferred_element_type=F32)
                dstate[h] = dsh
                lacc[h] = lsum

        @pl.when(n == nb - 1)
        def _():
            for d in range(2):
                for h in range(RET_HEADS):
                    outs[4 * d + 3][h] = jnp.zeros((8, LANES), F32) + jnp.sum(scr[2 * d + 1][h])

    hmat = (RET_HEADS, RET_HEAD_DIM, RET_HEAD_DIM)
    in_specs, out_specs, args = [pl.BlockSpec(memory_space=pltpu.SMEM)], [], [logits]
    for d, states in enumerate((st_f, st_b)):
        rows = pl.BlockSpec((c * sub, RET_W), lambda n, d=d: (block[d](n), 0))
        in_specs += ([rows, rows, rows, pl.BlockSpec((sub,) + hmat, lambda n, d=d: (block[d](n), 0, 0, 0))]
                     + [pl.BlockSpec(sp.block_shape, lambda n, d=d, cb=sp.index_map(0)[1]: (block[d](n), cb)) for _, sp in vwin])
        args += [q, k, dy, states] + [a for a, _ in vwin]
        out_specs += [rows, rows, rows, pl.BlockSpec((RET_HEADS, 8, LANES), lambda n: (0, 0, 0))]
    return pl.pallas_call(
        body, name="ret_bwd", grid=(nb,), in_specs=in_specs, out_specs=out_specs,
        out_shape=([jax.ShapeDtypeStruct((s, RET_W), BF16)] * 3 + [jax.ShapeDtypeStruct((RET_HEADS, 8, LANES), F32)]) * 2,
        scratch_shapes=[pltpu.VMEM(hmat, F32)] * 4,
        compiler_params=_cparams(("arbitrary",)),
    )(*args)


def _local_step(x, p, target, w, small):
    s = x.shape[0]
    tabs = _rope_tables(s, ATTN_HEAD_DIM) + _rope_tables(s, RET_HEAD_DIM)
    g_mix, g_mlp, g_ple = small["mix_norm"][None, :], small["mlp_norm"][None, :], small["ple_norm"][None, :]
    g_final, g_ret = small["final_norm"][None, :], small["ret_norm_gain"][None, :]
    gq_w = jnp.tile(small["attn_q_norm"], ATTN_HEADS)[None, :]
    gk_w = jnp.tile(small["attn_k_norm"], ATTN_KV_HEADS)[None, :]
    logits = small["ret_decay_logit"]

    hb = _stage_norm_in(x, g_mix)
    proj = _mm("in_proj", hb, w["w_in"], tm=512, tn=IN_W // 2, tk=1024, out_dtypes=(BF16,), j_outer=True)
    q_ct, k_rows, k_ct, v_rows, v_ct, rq, rk = _stage_qkv(proj, tabs, gq_w, gk_w)
    o_ct, lse = _attn_fwd(q_ct, k_rows, v_ct)
    ry_f, st_f, ry_b, st_b = _ret_fwd(logits, rq, rk, proj)
    rz, attn_rows = _stage_mix_post(ry_f, ry_b, proj, o_ct, g_ret)
    a_out = _mm("attn_o", attn_rows, w["w_attn_o"], tm=1024, tn=1024, tk=512, out_dtypes=(BF16,))
    r_out = _mm("ret_o", rz, w["w_ret_o"], tm=1024, tn=1024, tk=512, out_dtypes=(BF16,))
    merged = _stage_merge(proj, a_out, r_out)

    def epi_res_norm(acc, e, c):
        xr = e[0][...] + acc
        return xr, _rms_fwd(xr, c[0][...])

    x1, hm = _mm("out_proj", merged, w["w_out"], tm=512, tn=1024, tk=1024, out_dtypes=(F32, BF16),
                 epi=epi_res_norm, epi_ins=(x,), consts=(g_mlp,))

    def epi_relu2(acc, e, c):
        r = jnp.maximum(acc, 0.0)
        return (r * r,)

    act = _mm("mlp_up", hm, w["w_up"], tm=512, tn=2048, tk=1024, out_dtypes=(BF16,), epi=epi_relu2, j_outer=True)
    x2, hp = _mm("mlp_down", act, w["w_down"], tm=512, tn=1024, tk=D_FF, out_dtypes=(F32, BF16),
                 epi=epi_res_norm, epi_ins=(x1,), consts=(g_ple,))
    zg = _mm("ple_gate", hp, w["w_ple_gate"], tm=1024, tn=1024, tk=1024)
    pe = _mm("ple_emb", p, w["w_ple"], tm=1024, tn=1024, tk=256)
    dx3, dzg, dpe, loss_cols, g_final_p = _stage_head(zg, pe, x2, target, g_final)
    loss_sum = 0.5 / D_MODEL * jnp.sum(loss_cols)

    gw = {}
    gw["w_ple"] = _mm("g_w_ple", p, dpe, ta=True, tm=256, tn=1024, tk=2048)
    gw["w_ple_gate"] = _mm("g_w_ple_gate", hp, dzg, ta=True, tm=1024, tn=1024, tk=2048)
    def epi_norm_bwd(acc, e, c):
        dx, dg = _rms_bwd(acc, e[0][...], c[0][...])
        return e[1][...] + dx, dg

    def epi_norm_bwd_b(acc, e, c):
        tot, dg = epi_norm_bwd(acc, e, c)
        return tot, tot, dg

    dx2, dx2_b, g_ple_p = _mm("d_hp", dzg, w["w_ple_gate"], tb=True, tm=512, tn=1024, tk=1024, out_dtypes=(F32, BF16),
                              epi=epi_norm_bwd_b, epi_ins=(x2, dx3), consts=(g_ple,), n_sums=1)

    def epi_relu2_bwd(acc, e, c):
        return (acc * (2.0 * jnp.sqrt(e[0][...].astype(F32))),)

    du = _mm("d_u", dx2_b, w["w_down"], tb=True, tm=512, tn=2048, tk=1024, out_dtypes=(BF16,), epi=epi_relu2_bwd, epi_ins=(act,),
             j_outer=True)
    gw["w_down"] = _mm("g_w_down", act, dx2_b, ta=True, tm=1024, tn=1024, tk=2048)
    gw["w_up"] = _mm("g_w_up", hm, du, ta=True, tm=1024, tn=1024, tk=2048)
    dx1, dx1_b, g_mlp_p = _mm("d_hm", du, w["w_up"], tb=True, tm=512, tn=1024, tk=D_FF, out_dtypes=(F32, BF16),
                              epi=epi_norm_bwd_b, epi_ins=(x1, dx2), consts=(g_mlp,), n_sums=1)
    dmerged = _mm("d_merged", dx1_b, w["w_out"], tb=True, tm=1024, tn=1024, tk=1024)
    gw["w_out"] = _mm("g_w_out", merged, dx1_b, ta=True, tm=1024, tn=1024, tk=2048)
    dao, dro, dga, dgr = _stage_merge_bwd(proj, dmerged, a_out, r_out)
    gw["w_attn_o"] = _mm("g_w_attn_o", attn_rows, dao, ta=True, tm=512, tn=1024, tk=2048)
    gw["w_ret_o"] = _mm("g_w_ret_o", rz, dro, ta=True, tm=512, tn=1024, tk=2048)
    dattn = _mm("d_attn", dao, w["w_attn_o"], tb=True, tm=1024, tn=512, tk=1024)
    drz = _mm("d_rz", dro, w["w_ret_o"], tb=True, tm=1024, tn=512, tk=1024)
    do_ct, delta, dry, drg, g_ret_p = _stage_mix_post_bwd(dattn, attn_rows, drz, ry_f, ry_b, proj, g_ret)
    dq_f, dk_f, dv_f, dl_f, dq_b, dk_b, dv_b, dl_b = _ret_bwd(logits, rq, rk, proj, dry, st_f, st_b)
    dq_ct, dk8, dv8 = _attn_bwd(q_ct, do_ct, lse, delta, k_rows, v_rows, k_ct)
    dproj, gq_p, gk_p = _stage_dproj(proj, dq_ct, dk8, dv8, (dq_f, dk_f, dv_f, dq_b, dk_b, dv_b), drg, dga, dgr, tabs, gq_w, gk_w)
    gw["w_in"] = _mm("g_w_in", hb, dproj, ta=True, tm=1024, tn=IN_W // 2, tk=1024)
    grad_x, g_mix_p = _mm("d_h", dproj, w["w_in"], tb=True, tm=512, tn=1024, tk=IN_W, epi=epi_norm_bwd, epi_ins=(x, dx1),
                          consts=(g_mix,), n_sums=1)

    gs = {
        "mix_norm": g_mix_p[0], "mlp_norm": g_mlp_p[0], "ple_norm": g_ple_p[0], "final_norm": g_final_p[0],
        "ret_norm_gain": g_ret_p[0],
        "attn_q_norm": jnp.sum(gq_p[0].reshape(ATTN_HEADS, ATTN_HEAD_DIM), axis=0),
        "attn_k_norm": jnp.sum(gk_p[0].reshape(ATTN_KV_HEADS, ATTN_HEAD_DIM), axis=0),
        "ret_decay_logit": jnp.stack([dl_f[:, 0, 0], dl_b[:, 0, 0]]),
    }
    return loss_sum, grad_x, gw, gs


PACK_COLS = 1024
N_CHIPS = 4
HALF_ROWS = 2048


def _pack_shard(parts):
    return jnp.concatenate([parts[n].reshape(-1, PACK_COLS) for n, _ in BIG], axis=0)


def _unpack_shard(slab, shapes):
    out, r = {}, 0
    for n, _ in BIG:
        rows = math.prod(shapes[n]) // PACK_COLS
        out[n] = slab[r:r + rows].reshape(shapes[n])
        r += rows
    return out


def _shard_of(full, axis, sidx):
    size = full.shape[axis] // N_CHIPS
    return lax.slice_in_dim(full, sidx * size, (sidx + 1) * size, axis=axis)


def _position():
    x, y, c = lax.axis_index("x"), lax.axis_index("y"), lax.axis_index("c")
    return x, y, c


def _other_chips(x, y):
    return [(1 - x, y), (x, 1 - y), (1 - x, 1 - y)]


ANY = pl.BlockSpec(memory_space=pl.ANY)


def _gather_weights(slab):
    rows = slab.shape[0]
    half = rows // 2

    def body(in_ref, out_ref, send_sems, recv_sems):
        x, y, c = _position()
        chips = _other_chips(x, y)

        def piece(chip, core):
            return out_ref.at[2 * chip[0] + chip[1], pl.ds(core * half, half), :]

        def copy(k, chip, core, to, src=None):
            return pltpu.make_async_remote_copy(
                src_ref=piece(chip, core) if src is None else src, dst_ref=piece(chip, core),
                send_sem=send_sems.at[k], recv_sem=recv_sems.at[k], device_id=to, device_id_type=MESH)

        first = [copy(j, (x, y), c, (*chip, c), src=in_ref.at[pl.ds(c * half, half), :]) for j, chip in enumerate(chips)]
        for cp in first:
            cp.start()
        passed = [copy(3 + j, chip, c, (x, y, 1 - c)) for j, chip in enumerate(chips)]
        for j, chip in enumerate(chips):
            copy(j, chip, c, (x, y, c)).wait_recv()
            passed[j].start()
        for j, chip in enumerate(chips):
            copy(3 + j, chip, 1 - c, (x, y, c)).wait_recv()
        for cp in first + passed:
            cp.wait_send()

    return pl.pallas_call(
        body, name="gather_weights", in_specs=[ANY], out_specs=ANY,
        out_shape=jax.ShapeDtypeStruct((N_CHIPS,) + slab.shape, slab.dtype),
        scratch_shapes=[pltpu.SemaphoreType.DMA((6,)), pltpu.SemaphoreType.DMA((6,))],
    )(slab)


def _exchange_halves(g):
    def body(g_ref, out_ref, send_sem, recv_sem):
        x, y, c = _position()
        cp = pltpu.make_async_remote_copy(src_ref=g_ref.at[1 - c], dst_ref=out_ref, send_sem=send_sem, recv_sem=recv_sem,
                                          device_id=(x, y, 1 - c), device_id_type=MESH)
        cp.start()
        cp.wait()

    return pl.pallas_call(
        body, name="exchange_halves", in_specs=[ANY], out_specs=ANY,
        out_shape=jax.ShapeDtypeStruct(g.shape[1:], g.dtype),
        scratch_shapes=[pltpu.SemaphoreType.DMA, pltpu.SemaphoreType.DMA],
    )(g)


def _add_my_half(g, r1, c_idx):
    tr = 256
    nt = g.shape[2] // tr

    def body(c_ref, g_ref, r_ref, o_ref, ob_ref):
        tot = g_ref[...] + r_ref[...]
        o_ref[...] = tot
        ob_ref[...] = tot.astype(BF16)

    blk = (None, tr, PACK_COLS)
    spec = pl.BlockSpec(blk, lambda s, i, c_ref: (s, i, 0))
    return pl.pallas_call(
        body, name="add_my_half",
        grid_spec=pltpu.PrefetchScalarGridSpec(
            num_scalar_prefetch=1, grid=(N_CHIPS, nt),
            in_specs=[pl.BlockSpec((None,) + blk, lambda s, i, c_ref: (c_ref[0], s, i, 0)), spec],
            out_specs=[spec, spec]),
        out_shape=[jax.ShapeDtypeStruct(g.shape[1:], F32), jax.ShapeDtypeStruct(g.shape[1:], BF16)],
        compiler_params=_cparams(("parallel", "parallel")),
    )(c_idx, g, r1)


def _scatter_to_chips(part):
    def body(p_ref, out_ref, send_sems, recv_sems):
        x, y, c = _position()
        chips = _other_chips(x, y)
        sends = [pltpu.make_async_remote_copy(
            src_ref=p_ref.at[2 * chip[0] + chip[1]], dst_ref=out_ref.at[j], send_sem=send_sems.at[j], recv_sem=recv_sems.at[j],
            device_id=(*chip, c), device_id_type=MESH) for j, chip in enumerate(chips)]
        for cp in sends:
            cp.start()
        for cp in sends:
            cp.wait()

    return pl.pallas_call(
        body, name="scatter_to_chips", in_specs=[ANY], out_specs=ANY,
        out_shape=jax.ShapeDtypeStruct((N_CHIPS - 1,) + part.shape[1:], part.dtype),
        scratch_shapes=[pltpu.SemaphoreType.DMA((3,)), pltpu.SemaphoreType.DMA((3,))],
    )(part)


def _sum_chips(part, r2, chip_idx):
    tr = 256

    def body(c_ref, p_ref, r_ref, o_ref):
        o_ref[...] = ((p_ref[...] + r_ref[0]) + r_ref[1]) + r_ref[2]

    return pl.pallas_call(
        body, name="sum_chips",
        grid_spec=pltpu.PrefetchScalarGridSpec(
            num_scalar_prefetch=1, grid=(r2.shape[1] // tr,),
            in_specs=[pl.BlockSpec((None, tr, PACK_COLS), lambda i, c_ref: (c_ref[0], i, 0)),
                      pl.BlockSpec((N_CHIPS - 1, tr, PACK_COLS), lambda i, c_ref: (0, i, 0))],
            out_specs=pl.BlockSpec((tr, PACK_COLS), lambda i, c_ref: (i, 0))),
        out_shape=jax.ShapeDtypeStruct(r2.shape[1:], F32),
        compiler_params=_cparams(("parallel",)),
    )(chip_idx, part, r2)


def _join_halves(red):
    def body(r_ref, out_ref, send_sem, recv_sem):
        x, y, c = _position()
        cp = pltpu.make_async_remote_copy(src_ref=r_ref, dst_ref=out_ref, send_sem=send_sem, recv_sem=recv_sem,
                                          device_id=(x, y, 1 - c), device_id_type=MESH)
        cp.start()
        cp.wait()

    return pl.pallas_call(
        body, name="join_halves", in_specs=[ANY], out_specs=ANY,
        out_shape=jax.ShapeDtypeStruct(red.shape, red.dtype),
        scratch_shapes=[pltpu.SemaphoreType.DMA, pltpu.SemaphoreType.DMA],
    )(red)


def _adamw_math(w, g, m, v):
    m = ADAM_B1 * m + (1.0 - ADAM_B1) * g
    v = ADAM_B2 * v + (1.0 - ADAM_B2) * (g * g)
    m_hat = m / (1.0 - ADAM_B1 ** ADAM_STEP)
    v_hat = v / (1.0 - ADAM_B2 ** ADAM_STEP)
    delta = -ADAM_LR * (m_hat / (jnp.sqrt(v_hat) + ADAM_EPS) + ADAM_WD * w)
    return delta, m, v


def _adamw(name, w, g, m, v):
    tr = min(256, w.shape[0])

    def body(w_ref, g_ref, m_ref, v_ref, d_ref, nm_ref, nv_ref):
        d_ref[...], nm_ref[...], nv_ref[...] = _adamw_math(w_ref[...], g_ref[...], m_ref[...], v_ref[...])

    blk = pl.BlockSpec((tr, w.shape[1]), lambda i: (i, 0))
    return pl.pallas_call(
        body, name="adamw_" + name, grid=(w.shape[0] // tr,), in_specs=[blk] * 4, out_specs=[blk] * 3,
        out_shape=[jax.ShapeDtypeStruct(w.shape, F32)] * 3, compiler_params=_cparams(("parallel",)),
    )(w, g, m, v)


def _small_step(gpk, wpk, mpk, vpk):
    row, col, width = SMALL["ret_decay_logit"]

    def body(g_ref, w_ref, m_ref, v_ref, og_ref, od_ref, om_ref, ov_ref, gbuf, send_sems, recv_sems):
        x, y, c = _position()
        me = 4 * x + 2 * y + c
        gbuf[me] = g_ref[...]
        sends = []
        for k in range(1, 8):
            to = (x ^ (k >> 2), y ^ ((k >> 1) & 1), c ^ (k & 1))
            cp = pltpu.make_async_remote_copy(src_ref=g_ref, dst_ref=gbuf.at[me], send_sem=send_sems.at[k - 1],
                                              recv_sem=recv_sems.at[k - 1], device_id=to, device_id_type=MESH)
            cp.start()
            sends.append(cp)
        for k in range(1, 8):
            frm = me ^ k
            pltpu.make_async_remote_copy(src_ref=g_ref, dst_ref=gbuf.at[frm], send_sem=send_sems.at[k - 1],
                                         recv_sem=recv_sems.at[k - 1], device_id=(x, y, c), device_id_type=MESH).wait_recv()
        for cp in sends:
            cp.wait_send()
        tot = gbuf[0]
        for d in range(1, 8):
            tot = tot + gbuf[d]
        w = w_ref[...]
        r_i = lax.broadcasted_iota(jnp.int32, w.shape, 0)
        c_i = lax.broadcasted_iota(jnp.int32, w.shape, 1)
        is_logit = (r_i == row) & (c_i >= col) & (c_i < col + width)
        g = jnp.where(is_logit, tot * _sigmoid(-w), tot)
        og_ref[...] = g
        od_ref[...], om_ref[...], ov_ref[...] = _adamw_math(w, g, m_ref[...], v_ref[...])

    vm = pl.BlockSpec(memory_space=pltpu.VMEM)
    shp = jax.ShapeDtypeStruct(gpk.shape, F32)
    return pl.pallas_call(
        body, name="small_step", in_specs=[vm] * 4, out_specs=[vm] * 4, out_shape=[shp] * 4,
        scratch_shapes=[pltpu.VMEM((8,) + gpk.shape, F32), pltpu.SemaphoreType.DMA((7,)), pltpu.SemaphoreType.DMA((7,))],
    )(gpk, wpk, mpk, vpk)


def _pack_small(parts):
    rows = [[] for _ in range(SMALL_ROWS)]
    for n, (r, col, width) in sorted(SMALL.items(), key=lambda kv: (kv[1][0], kv[1][1])):
        rows[r].append((col, parts[n].reshape(-1).astype(F32)))
    out = []
    for r in range(SMALL_ROWS):
        segs, pos = [], 0
        for col, vec in rows[r]:
            assert col == pos
            segs.append(vec)
            pos += vec.shape[0]
        if pos < PACK_COLS:
            segs.append(jnp.zeros((PACK_COLS - pos,), F32))
        out.append(jnp.concatenate(segs))
    return jnp.stack(out)


def _unpack_small(pk, shapes):
    return {n: pk[r, col:col + width].reshape(shapes[n]) for n, (r, col, width) in SMALL.items()}


WEIGHTS = ("mix_norm", "w_in", "attn_q_norm", "attn_k_norm", "ret_decay_logit", "ret_norm_gain", "w_attn_o", "w_ret_o", "w_out",
           "mlp_norm", "w_up", "w_down", "ple_norm", "w_ple_gate", "w_ple", "final_norm")


def kernel(x, p, mix_norm, w_in, attn_q_norm, attn_k_norm, ret_decay_logit, ret_norm_gain, w_attn_o, w_ret_o, w_out, mlp_norm, w_up, w_down, ple_norm, w_ple_gate, w_ple, final_norm, loss_target, m_mix_norm, m_w_in, m_attn_q_norm, m_attn_k_norm, m_ret_decay_logit, m_ret_norm_gain, m_w_attn_o, m_w_ret_o, m_w_out, m_mlp_norm, m_w_up, m_w_down, m_ple_norm, m_w_ple_gate, m_w_ple, m_final_norm, v_mix_norm, v_w_in, v_attn_q_norm, v_attn_k_norm, v_ret_decay_logit, v_ret_norm_gain, v_w_attn_o, v_w_ret_o, v_w_out, v_mlp_norm, v_w_up, v_w_down, v_ple_norm, v_w_ple_gate, v_w_ple, v_final_norm):
    args = dict(locals())
    wts = {n: args[n] for n in WEIGHTS}
    ms = {n: args["m_" + n] for n in WEIGHTS}
    vs = {n: args["v_" + n] for n in WEIGHTS}
    shapes = {n: wts[n].shape for n in WEIGHTS}
    big_names = [n for n, _ in BIG]
    xi, yi, ci = _position()
    c_idx = ci.astype(jnp.int32).reshape(1)

    chip_idx = (2 * xi + yi).astype(jnp.int32)
    slab_b = _pack_shard({n: wts[n][0].astype(BF16) for n in big_names})
    gathered = lax.dynamic_update_slice(_gather_weights(slab_b), slab_b[None], (chip_idx, 0, 0))
    full = {}
    for n, axis in BIG:
        per_chip = [_unpack_shard(gathered[k], {m_: shapes[m_][1:] for m_ in big_names})[n] for k in range(N_CHIPS)]
        full[n] = jnp.concatenate(per_chip, axis=axis)
    small = {n: wts[n].reshape(wts[n].shape[1:] if wts[n].ndim > 1 else wts[n].shape) for n in SMALL}

    loss_part, grad_x, gw, gs = _local_step(x[0], p[0, 0], loss_target[0], full, small)
    loss = lax.psum(loss_part, ("x", "y", "c"))

    slabs = jnp.stack([_pack_shard({n: _shard_of(gw[n], axis, k) for n, axis in BIG}) for k in range(N_CHIPS)])
    halves = slabs.reshape(N_CHIPS, 2, HALF_ROWS, PACK_COLS).transpose(1, 0, 2, 3)
    chip_part, chip_part_b = _add_my_half(halves, _exchange_halves(halves), c_idx)
    mine = _sum_chips(chip_part, _scatter_to_chips(chip_part_b), chip_idx.reshape(1))
    both = jnp.stack([mine, _join_halves(mine)])
    reduced = jnp.where(ci == 0, both, both[::-1]).reshape(2 * HALF_ROWS, PACK_COLS)
    g_big = _unpack_shard(reduced, {n: shapes[n][1:] for n in big_names})
    big_out = [{}, {}, {}, {}]
    for n in big_names:
        big_out[0][n] = g_big[n][None]
        for kind, a in enumerate(_adamw(n, wts[n][0], g_big[n], ms[n][0], vs[n][0])):
            big_out[kind + 1][n] = a[None]

    sm_out = _small_step(_pack_small(gs), _pack_small({n: wts[n] for n in SMALL}), _pack_small({n: ms[n] for n in SMALL}),
                         _pack_small({n: vs[n] for n in SMALL}))
    small_out = [_unpack_small(a, {n: shapes[n] for n in SMALL}) for a in sm_out]

    outs = [loss, grad_x[None]]
    for kind in range(4):
        for n in WEIGHTS:
            outs.append(small_out[kind][n] if n in SMALL else big_out[kind][n])
    return tuple(outs)
```

```python
import functools
import math

import jax
import jax.numpy as jnp
from jax import lax
from jax.experimental import pallas as pl
from jax.experimental.pallas import tpu as pltpu

F32 = jnp.float32
BF16 = jnp.bfloat16
MESH = pl.DeviceIdType.MESH

D_MODEL = 1024
PLE_DIM = 256
GRID_W = 64
ATTN_HEAD_DIM = 64
ATTN_HEADS = 8
ATTN_KV_HEADS = 2
ATTN_GROUP = ATTN_HEADS // ATTN_KV_HEADS
RET_HEAD_DIM = 128
RET_HEADS = 4
ATTN_Q_W = 512
ATTN_KV_W = 128
RET_W = 512
IN_W = 4864
D_FF = 4096
RET_CHUNK = 128
ROPE_THETA = 10000.0
NORM_EPS = 1e-6
GN_EPS = 1e-5
ATTN_SCALE = ATTN_HEAD_DIM ** -0.5
LOG2E = math.log2(math.e)
Q_FOLD = ATTN_SCALE * LOG2E
RET_SCALE = RET_HEAD_DIM ** -0.5

C_AQ, C_AK, C_AV, C_RQ, C_RK, C_RV, C_RG, C_GA, C_GR = 0, 512, 640, 768, 1280, 1792, 2304, 2816, 3840

ADAM_LR = 0.001
ADAM_B1 = 0.9
ADAM_B2 = 0.999
ADAM_EPS = 1e-08
ADAM_WD = 0.01
ADAM_STEP = 10

LANES = 128
VMEM_LIMIT = 56 << 20
SEQ_TILE = 512

BIG = (("w_in", 1), ("w_attn_o", 1), ("w_ret_o", 1), ("w_out", 0), ("w_up", 1), ("w_down", 0), ("w_ple_gate", 0), ("w_ple", 1))
SMALL_ROWS = 8
SMALL = {"mix_norm": (0, 0, 1024), "mlp_norm": (1, 0, 1024), "ple_norm": (2, 0, 1024), "final_norm": (3, 0, 1024),
         "ret_norm_gain": (4, 0, 512), "attn_q_norm": (4, 512, 64), "attn_k_norm": (4, 576, 64), "ret_decay_logit": (4, 640, 8)}


def _seq_tile(s):
    return min(SEQ_TILE, s // 2)


def _cparams(sem=None, vmem=VMEM_LIMIT):
    return pltpu.CompilerParams(dimension_semantics=sem, vmem_limit_bytes=vmem)


def _mm(name, a, b, *, ta=False, tb=False, tm, tn, tk, out_dtypes=(F32,), epi=None, epi_ins=(), consts=(), n_sums=0, j_outer=False):
    if ta:
        kdim, m = a.shape
    else:
        m, kdim = a.shape
    n = b.shape[0] if tb else b.shape[1]
    tm, tn, tk = min(tm, m), min(tn, n), min(tk, kdim)
    assert m % tm == 0 and n % tn == 0 and kdim % tk == 0, (name, m, n, kdim, tm, tn, tk)
    nk = kdim // tk
    n_e, n_c, n_o = len(epi_ins), len(consts), len(out_dtypes)
    assert n_sums == 0 or tn == n

    def body(*refs):
        a_ref, b_ref = refs[0], refs[1]
        e_refs = refs[2:2 + n_e]
        c_refs = refs[2 + n_e:2 + n_e + n_c]
        o_refs = refs[2 + n_e + n_c:2 + n_e + n_c + n_o]
        s_refs = refs[2 + n_e + n_c + n_o:2 + n_e + n_c + n_o + n_sums]
        acc_ref = refs[2 + n_e + n_c + n_o + n_sums] if nk > 1 else None
        k = pl.program_id(2)
        if n_sums:
            @pl.when((pl.program_id(1 if j_outer else 0) == 0) & (k == 0))
            def _():
                for r in s_refs:
                    r[...] = jnp.zeros(r.shape, F32)
        av = a_ref[...].astype(BF16)
        bv = b_ref[...].astype(BF16)
        dims = (((0,) if ta else (1,), (1,) if tb else (0,)), ((), ()))
        part = lax.dot_general(av, bv, dims, preferred_element_type=F32)

        def finish(acc):
            vals = epi(acc, e_refs, c_refs) if epi is not None else (acc,)
            for o_ref, v in zip(o_refs, vals[:n_o]):
                o_ref[...] = v.astype(o_ref.dtype)
            for s_ref, v in zip(s_refs, vals[n_o:]):
                _acc_add(s_ref, v)

        if nk == 1:
            finish(part)
        else:
            @pl.when(k == 0)
            def _():
                acc_ref[...] = part

            @pl.when(k > 0)
            def _():
                acc_ref[...] += part

            @pl.when(k == nk - 1)
            def _():
                finish(acc_ref[...])

    def spec(shape, index):
        return pl.BlockSpec(shape, (lambda j, i, k: index(i, j, k)) if j_outer else index)

    a_spec = spec((tk, tm), lambda i, j, k: (k, i)) if ta else spec((tm, tk), lambda i, j, k: (i, k))
    b_spec = spec((tn, tk), lambda i, j, k: (j, k)) if tb else spec((tk, tn), lambda i, j, k: (k, j))
    o_spec = spec((tm, tn), lambda i, j, k: (i, j))
    c_specs = [spec(c.shape, lambda i, j, k, nd=c.ndim: (0,) * nd) for c in consts]
    outs = pl.pallas_call(
        body, name=name,
        grid=(n // tn, m // tm, nk) if j_outer else (m // tm, n // tn, nk),
        in_specs=[a_spec, b_spec] + [o_spec] * n_e + c_specs,
        out_specs=[o_spec] * n_o + [spec((8, n), lambda i, j, k: (0, 0))] * n_sums,
        out_shape=[jax.ShapeDtypeStruct((m, n), dt) for dt in out_dtypes] + [jax.ShapeDtypeStruct((8, n), F32)] * n_sums,
        scratch_shapes=[pltpu.VMEM((tm, tn), F32)] if nk > 1 else [],
        compiler_params=_cparams(("arbitrary",) * 3 if n_sums else ("parallel", "parallel", "arbitrary")),
    )(a, b, *epi_ins, *consts)
    return outs[0] if n_o + n_sums == 1 else outs


def _rows(arr, tr):
    return (arr, pl.BlockSpec((tr, arr.shape[1]), lambda i: (i, 0)))


def _win(arr, tr, start, width):
    bw = math.gcd(start, width) if start else width
    assert bw % LANES == 0
    return [(arr, pl.BlockSpec((tr, bw), lambda i, cb=start // bw + p: (i, cb))) for p in range(width // bw)]


def _ct(arr):
    return (arr, pl.BlockSpec((None,) + arr.shape[1:], lambda i: (i, 0, 0)))


def _whole(arr):
    return (arr, pl.BlockSpec(arr.shape, lambda i, nd=arr.ndim: (0,) * nd))


def _cat(refs):
    vals = [r[...].astype(F32) for r in refs]
    return vals[0] if len(vals) == 1 else jnp.concatenate(vals, axis=1)


def _seqtiled(name, fn, n_tiles, ins, outs, acc_widths=()):
    n_i, n_o, n_a = len(ins), len(outs), len(acc_widths)

    def body(*refs):
        i_refs, o_refs, a_refs = refs[:n_i], refs[n_i:n_i + n_o], refs[n_i + n_o:]
        if n_a:
            @pl.when(pl.program_id(0) == 0)
            def _():
                for r in a_refs:
                    r[...] = jnp.zeros(r.shape, F32)
        fn(list(i_refs), list(o_refs), list(a_refs))

    res = pl.pallas_call(
        body, name=name, grid=(n_tiles,),
        in_specs=[s for _, s in ins],
        out_specs=[s for _, _, s in outs] + [pl.BlockSpec((8, w), lambda i: (0, 0)) for w in acc_widths],
        out_shape=[jax.ShapeDtypeStruct(sh, dt) for sh, dt, _ in outs] + [jax.ShapeDtypeStruct((8, w), F32) for w in acc_widths],
        compiler_params=_cparams(("arbitrary",)),
    )(*[a for a, _ in ins])
    return res


def _acc_add(acc_ref, val):
    acc_ref[0:1, :] += jnp.sum(val, axis=0, keepdims=True)


def _out_rows(s, w, dt, tr):
    return ((s, w), dt, pl.BlockSpec((tr, w), lambda i: (i, 0)))


def _out_ct(s, w, dt, t):
    return ((s // t, w, t), dt, pl.BlockSpec((None, w, t), lambda i: (i, 0, 0)))


def _rms_fwd(x, gain):
    r = lax.rsqrt(jnp.mean(x * x, axis=-1, keepdims=True) + NORM_EPS)
    return x * r * gain


def _rms_bwd(dy, x, gain):
    r = lax.rsqrt(jnp.mean(x * x, axis=-1, keepdims=True) + NORM_EPS)
    xn = x * r
    dyg = dy * gain
    dx = r * (dyg - xn * jnp.mean(dyg * xn, axis=-1, keepdims=True))
    return dx, dy * xn


def _seg_mean(y, hd):
    w = y.shape[1]
    pieces = []
    for s in range(0, w, LANES):
        v = y[:, s:s + LANES]
        tot = jnp.sum(v, axis=1, keepdims=True)
        if hd == LANES:
            pieces.append(jnp.broadcast_to(tot, v.shape))
        else:
            low = lax.broadcasted_iota(jnp.int32, v.shape, 1) < hd
            lo = jnp.sum(jnp.where(low, v, 0.0), axis=1, keepdims=True)
            pieces.append(jnp.where(low, lo, tot - lo))
    out = pieces[0] if len(pieces) == 1 else jnp.concatenate(pieces, axis=1)
    return out * (1.0 / hd)


def _tile_lanes(t, w):
    return t if w == t.shape[1] else jnp.concatenate([t] * (w // t.shape[1]), axis=1)


def _swap_halves(x, hd):
    w = x.shape[1]
    half = hd // 2
    lane = lax.broadcasted_iota(jnp.int32, x.shape, 1)
    return jnp.where((lane % hd) < half, pltpu.roll(x, w - half, 1), pltpu.roll(x, half, 1))


def _rope(x, cos, sin_signed, hd):
    w = x.shape[1]
    return x * _tile_lanes(cos, w) + _swap_halves(x, hd) * _tile_lanes(sin_signed, w)


def _rope_t(dy, cos, sin_signed, hd):
    w = dy.shape[1]
    return dy * _tile_lanes(cos, w) + _swap_halves(dy * _tile_lanes(sin_signed, w), hd)


def _headnorm_fwd(x, gain_w, hd):
    r = lax.rsqrt(_seg_mean(x * x, hd) + NORM_EPS)
    return x * r * gain_w


def _headnorm_bwd(dy, x, gain_w, hd):
    r = lax.rsqrt(_seg_mean(x * x, hd) + NORM_EPS)
    xn = x * r
    dyg = dy * gain_w
    return r * (dyg - xn * _seg_mean(dyg * xn, hd)), dy * xn


def _sigmoid(x):
    return 1.0 / (1.0 + jnp.exp(-x))


def _rope_tables(seq_len, head_dim):
    rows = seq_len // GRID_W
    n_axis = head_dim // 4
    freqs = ROPE_THETA ** (-jnp.arange(n_axis, dtype=F32) / n_axis)
    ang_r = jnp.arange(rows, dtype=F32)[:, None] * freqs
    ang_c = jnp.arange(GRID_W, dtype=F32)[:, None] * freqs

    def expand(by_row, by_col):
        r = jnp.broadcast_to(by_row[:, None, :], (rows, GRID_W, n_axis))
        c = jnp.broadcast_to(by_col[None, :, :], (rows, GRID_W, n_axis))
        return jnp.concatenate([r, c], axis=-1).reshape(seq_len, 2 * n_axis)

    cos, sin = expand(jnp.cos(ang_r), jnp.cos(ang_c)), expand(jnp.sin(ang_r), jnp.sin(ang_c))
    reps = LANES // head_dim
    return jnp.tile(jnp.concatenate([cos, cos], axis=-1), (1, reps)), jnp.tile(jnp.concatenate([-sin, sin], axis=-1), (1, reps))


def _stage_norm_in(x, gain):
    s = x.shape[0]
    tr = min(SEQ_TILE, s)

    def fn(i, o, a):
        o[0][...] = _rms_fwd(i[0][...], i[1][...]).astype(BF16)

    return _seqtiled("norm_in", fn, s // tr, [_rows(x, tr), _whole(gain)], [_out_rows(s, D_MODEL, BF16, tr)])[0]


def _stage_qkv(proj, tabs, gq_w, gk_w):
    s = proj.shape[0]
    t = _seq_tile(s)
    ca, sa, cr, sr = tabs
    ins = (_win(proj, t, C_AQ, ATTN_Q_W) + _win(proj, t, C_AK, ATTN_KV_W) + _win(proj, t, C_AV, ATTN_KV_W)
           + _win(proj, t, C_RQ, RET_W) + _win(proj, t, C_RK, RET_W)
           + [_rows(ca, t), _rows(sa, t), _rows(cr, t), _rows(sr, t), _whole(gq_w), _whole(gk_w)])

    def fn(i, o, a):
        aq, ak, av = (i[n][...].astype(F32) for n in range(3))
        rq, rk = _cat(i[3:5]), _cat(i[5:7])
        ca_, sa_, cr_, sr_ = i[7][...], i[8][...], i[9][...], i[10][...]
        qr = _rope(_headnorm_fwd(aq, i[11][...], ATTN_HEAD_DIM), ca_, sa_, ATTN_HEAD_DIM) * Q_FOLD
        kr = _rope(_headnorm_fwd(ak, i[12][...], ATTN_HEAD_DIM), ca_, sa_, ATTN_HEAD_DIM)
        qt = qr.T.astype(BF16)
        zeros = jnp.zeros((ATTN_HEAD_DIM, t), BF16)
        for h in range(ATTN_HEADS):
            g = h // ATTN_GROUP
            blk = qt[h * ATTN_HEAD_DIM:(h + 1) * ATTN_HEAD_DIM, :]
            o[0][h * LANES + g * ATTN_HEAD_DIM:h * LANES + (g + 1) * ATTN_HEAD_DIM, :] = blk
            o[0][h * LANES + (1 - g) * ATTN_HEAD_DIM:h * LANES + (2 - g) * ATTN_HEAD_DIM, :] = zeros
        o[1][...] = kr.astype(BF16)
        o[2][...] = kr.T.astype(BF16)
        o[3][...] = av.astype(BF16)
        o[4][...] = av.T.astype(BF16)
        o[5][...] = _rope(rq, cr_, sr_, RET_HEAD_DIM) * RET_SCALE
        o[6][...] = _rope(rk, cr_, sr_, RET_HEAD_DIM)

    outs = [_out_ct(s, ATTN_HEADS * LANES, BF16, t), _out_rows(s, ATTN_KV_W, BF16, t), _out_ct(s, ATTN_KV_W, BF16, t),
            _out_rows(s, ATTN_KV_W, BF16, t), _out_ct(s, ATTN_KV_W, BF16, t), _out_rows(s, RET_W, F32, t), _out_rows(s, RET_W, F32, t)]
    return _seqtiled("qkv_prep", fn, s // t, ins, outs)


def _groupnorm_gate(ry, rg, gain):
    mu = _seg_mean(ry, RET_HEAD_DIM)
    d = ry - mu
    rs = lax.rsqrt(_seg_mean(d * d, RET_HEAD_DIM) + GN_EPS)
    return d * rs, rs, _sigmoid(rg)


def _stage_mix_post(ry_f, ry_b, proj, o_ct, gain):
    s = proj.shape[0]
    t = _seq_tile(s)
    ins = [_rows(ry_f, t), _rows(ry_b, t)] + _win(proj, t, C_RG, RET_W) + [_ct(o_ct), _whole(gain)]

    def fn(i, o, a):
        ry = i[0][...] + i[1][...]
        rg = _cat(i[2:4])
        gn, _, sg = _groupnorm_gate(ry, rg, None)
        o[0][...] = (gn * i[5][...] * (rg * sg)).astype(BF16)
        o[1][...] = i[4][...].astype(F32).T.astype(BF16)

    return _seqtiled("mix_post", fn, s // t, ins, [_out_rows(s, RET_W, BF16, t), _out_rows(s, ATTN_Q_W, BF16, t)])


def _stage_merge(proj, a_out, r_out):
    s = proj.shape[0]
    tr = min(SEQ_TILE, s)
    ins = _win(proj, tr, C_GA, D_MODEL) + _win(proj, tr, C_GR, D_MODEL) + [_rows(a_out, tr), _rows(r_out, tr)]
    na = len(_win(proj, tr, C_GA, D_MODEL))

    def fn(i, o, a):
        ga, gr = _cat(i[:na]), _cat(i[na:2 * na])
        o[0][...] = (_sigmoid(ga) * i[2 * na][...] + _sigmoid(gr) * i[2 * na + 1][...]).astype(BF16)

    return _seqtiled("merge", fn, s // tr, ins, [_out_rows(s, D_MODEL, BF16, tr)])[0]


def _stage_head(zg, pe, x2, target, g_final):
    s = x2.shape[0]
    tr = min(SEQ_TILE // 2, s)
    ins = [_rows(zg, tr), _rows(pe, tr), _rows(x2, tr), _rows(target, tr), _whole(g_final)]

    def fn(i, o, a):
        gt = _sigmoid(i[0][...])
        pe_ = i[1][...]
        x3 = i[2][...] + gt * pe_
        gf = i[4][...]
        r3 = lax.rsqrt(jnp.mean(x3 * x3, axis=-1, keepdims=True) + NORM_EPS)
        x3n = x3 * r3
        e = x3n * gf - i[3][...]
        _acc_add(a[0], e * e)
        dy = e * (1.0 / D_MODEL)
        _acc_add(a[1], dy * x3n)
        dyg = dy * gf
        dx3 = r3 * (dyg - x3n * jnp.mean(dyg * x3n, axis=-1, keepdims=True))
        o[0][...] = dx3
        o[1][...] = (dx3 * pe_ * gt * (1.0 - gt)).astype(BF16)
        o[2][...] = (dx3 * gt).astype(BF16)

    outs = [_out_rows(s, D_MODEL, F32, tr), _out_rows(s, D_MODEL, BF16, tr), _out_rows(s, D_MODEL, BF16, tr)]
    return _seqtiled("head", fn, s // tr, ins, outs, acc_widths=(D_MODEL, D_MODEL))


def _stage_merge_bwd(proj, dmerged, a_out, r_out):
    s = proj.shape[0]
    tr = min(SEQ_TILE // 2, s)
    wins = _win(proj, tr, C_GA, D_MODEL)
    na = len(wins)
    ins = wins + _win(proj, tr, C_GR, D_MODEL) + [_rows(dmerged, tr), _rows(a_out, tr), _rows(r_out, tr)]

    def fn(i, o, a):
        sa, sr = _sigmoid(_cat(i[:na])), _sigmoid(_cat(i[na:2 * na]))
        dm = i[2 * na][...]
        o[0][...] = (dm * sa).astype(BF16)
        o[1][...] = (dm * sr).astype(BF16)
        o[2][...] = (dm * i[2 * na + 1][...] * sa * (1.0 - sa)).astype(BF16)
        o[3][...] = (dm * i[2 * na + 2][...] * sr * (1.0 - sr)).astype(BF16)

    return _seqtiled("merge_bwd", fn, s // tr, ins, [_out_rows(s, D_MODEL, BF16, tr)] * 4)


def _stage_mix_post_bwd(dattn, attn_rows, drz, ry_f, ry_b, proj, gain):
    s = proj.shape[0]
    t = _seq_tile(s)
    ins = ([_rows(dattn, t), _rows(attn_rows, t), _rows(drz, t), _rows(ry_f, t), _rows(ry_b, t)]
           + _win(proj, t, C_RG, RET_W) + [_whole(gain)])

    def fn(i, o, a):
        da = i[0][...]
        dat = da.T
        prod_t = (da * i[1][...].astype(F32)).T
        dat_b = dat.astype(BF16)
        zeros = jnp.zeros((ATTN_HEAD_DIM, t), BF16)
        for h in range(ATTN_HEADS):
            g = h // ATTN_GROUP
            o[0][h * LANES + g * ATTN_HEAD_DIM:h * LANES + (g + 1) * ATTN_HEAD_DIM, :] = dat_b[h * ATTN_HEAD_DIM:(h + 1) * ATTN_HEAD_DIM, :]
            o[0][h * LANES + (1 - g) * ATTN_HEAD_DIM:h * LANES + (2 - g) * ATTN_HEAD_DIM, :] = zeros
            o[1][h] = jnp.sum(prod_t[h * ATTN_HEAD_DIM:(h + 1) * ATTN_HEAD_DIM, :], axis=0, keepdims=True)
        ry = i[3][...] + i[4][...]
        rg = _cat(i[5:7])
        gain_ = i[7][...]
        gn, rs, sg = _groupnorm_gate(ry, rg, None)
        dz = i[2][...]
        silu = rg * sg
        _acc_add(a[0], dz * gn * silu)
        dgn = dz * gain_ * silu
        o[2][...] = rs * (dgn - _seg_mean(dgn, RET_HEAD_DIM) - gn * _seg_mean(dgn * gn, RET_HEAD_DIM))
        o[3][...] = (dz * gn * gain_ * (sg * (1.0 + rg * (1.0 - sg)))).astype(BF16)

    outs = [_out_ct(s, ATTN_HEADS * LANES, BF16, t),
            ((ATTN_HEADS, s // t, 1, t), F32, pl.BlockSpec((ATTN_HEADS, None, 1, t), lambda i: (0, i, 0, 0))),
            _out_rows(s, RET_W, F32, t), _out_rows(s, RET_W, BF16, t)]
    return _seqtiled("mix_post_bwd", fn, s // t, ins, outs, acc_widths=(RET_W,))


def _stage_dproj(proj, dq_ct, dk8, dv8, rgrads, drg, dga, dgr, tabs, gq_w, gk_w):
    s = proj.shape[0]
    t = _seq_tile(s)
    ca, sa, cr, sr = tabs
    kv8 = pl.BlockSpec((ATTN_HEADS, t, ATTN_KV_W), lambda i: (0, i, 0))
    ins = (_win(proj, t, C_AQ, ATTN_Q_W) + _win(proj, t, C_AK, ATTN_KV_W) + [_ct(dq_ct), (dk8, kv8), (dv8, kv8)]
           + [_rows(g, t) for g in rgrads] + [_rows(drg, t), _rows(dga, t), _rows(dgr, t)]
           + [_rows(ca, t), _rows(sa, t), _rows(cr, t), _rows(sr, t), _whole(gq_w), _whole(gk_w)])

    def fn(i, o, a):
        aq, ak = i[0][...].astype(F32), i[1][...].astype(F32)
        dq_f, dk_f, dv_f, dq_b, dk_b, dv_b = (r[...].astype(F32) for r in i[5:11])
        ca_, sa_, cr_, sr_ = i[14][...], i[15][...], i[16][...], i[17][...]
        dqn = _rope_t(i[2][...].T * ATTN_SCALE, ca_, sa_, ATTN_HEAD_DIM)
        daq, gq_rows = _headnorm_bwd(dqn, aq, i[18][...], ATTN_HEAD_DIM)
        dkn = _rope_t(jnp.sum(i[3][...].astype(F32), axis=0) * (1.0 / LOG2E), ca_, sa_, ATTN_HEAD_DIM)
        dak, gk_rows = _headnorm_bwd(dkn, ak, i[19][...], ATTN_HEAD_DIM)
        _acc_add(a[0], gq_rows)
        _acc_add(a[1], gk_rows)
        out = o[0]
        out[:, C_AQ:C_AQ + ATTN_Q_W] = daq.astype(BF16)
        out[:, C_AK:C_AK + ATTN_KV_W] = dak.astype(BF16)
        out[:, C_AV:C_AV + ATTN_KV_W] = jnp.sum(i[4][...].astype(F32), axis=0).astype(BF16)
        out[:, C_RQ:C_RQ + RET_W] = _rope_t((dq_f + dq_b) * RET_SCALE, cr_, sr_, RET_HEAD_DIM).astype(BF16)
        out[:, C_RK:C_RK + RET_W] = _rope_t(dk_f + dk_b, cr_, sr_, RET_HEAD_DIM).astype(BF16)
        out[:, C_RV:C_RV + RET_W] = (dv_f + dv_b).astype(BF16)
        out[:, C_RG:C_RG + RET_W] = i[11][...]
        out[:, C_GA:C_GA + D_MODEL] = i[12][...]
        out[:, C_GR:C_GR + D_MODEL] = i[13][...]

    return _seqtiled("dproj", fn, s // t, ins, [_out_rows(s, IN_W, BF16, t)], acc_widths=(ATTN_Q_W, ATTN_KV_W))


def _attn_fwd(q_ct, k_rows, v_ct):
    nq, _, t = q_ct.shape
    s = nq * t
    nk = nq
    assert nk % 2 == 0
    n_ch = 4 if nq % 4 == 0 else 2
    halves = 2 if t % (2 * LANES) == 0 else 1
    tq = t // halves
    n_par = n_ch * halves

    def body(q_ref, k_ref, v_ref, o_ref, lse_ref, *bufs):
        sbuf = tuple(bufs[2 * w:2 * w + 2] for w in range(n_par))
        pbuf = tuple(bufs[2 * n_par + 2 * w:2 * n_par + 2 * w + 2] for w in range(n_par))

        def where(w):
            return w // halves, slice((w % halves) * tq, (w % halves + 1) * tq)

        def scores(w, j, slot):
            kj = k_ref[pl.ds(pl.multiple_of(j * t, t), t), :]
            cw, lanes = where(w)
            st = jnp.dot(kj, q_ref[cw, :, lanes], preferred_element_type=F32)
            sbuf[w][slot][...] = st
            return jnp.max(st, axis=0, keepdims=True)

        def probs(w, slot, cmax, m, l):
            m_new = jnp.maximum(m, cmax)
            alpha = jnp.exp2(m - m_new)
            pt = jnp.exp2(sbuf[w][slot][...] - m_new)
            pbuf[w][slot][...] = pt.astype(BF16)
            return m_new, alpha * l + jnp.sum(pt, axis=0, keepdims=True), alpha

        def values(w, j, slot, alpha, acc):
            return alpha * acc + jnp.dot(v_ref[j], pbuf[w][slot][...], preferred_element_type=F32)

        init = []
        for w in range(n_par):
            m = jnp.full((1, tq), -1e30, F32)
            l = jnp.zeros((1, tq), F32)
            cmax0 = scores(w, 0, 0)
            cmax1 = scores(w, 1, 1)
            m, l, alpha0 = probs(w, 0, cmax0, m, l)
            init.append((m, l, jnp.zeros((ATTN_HEAD_DIM, tq), F32), cmax1, alpha0))

        def trip(n, carry):
            c = 2 * n
            out = []
            for w in range(n_par):
                m, l, acc, cmax_b, alpha_c = carry[w]
                acc = values(w, c, 0, alpha_c, acc)
                m, l, alpha1 = probs(w, 1, cmax_b, m, l)
                cmax2 = scores(w, c + 2, 0)
                acc = values(w, c + 1, 1, alpha1, acc)
                m, l, alpha2 = probs(w, 0, cmax2, m, l)
                cmax3 = scores(w, c + 3, 1)
                out.append((m, l, acc, cmax3, alpha2))
            return tuple(out)

        res = lax.fori_loop(0, nk // 2 - 1, trip, tuple(init))
        for w in range(n_par):
            m, l, acc, cmax_b, alpha_c = res[w]
            acc = values(w, nk - 2, 0, alpha_c, acc)
            m, l, alpha1 = probs(w, 1, cmax_b, m, l)
            acc = values(w, nk - 1, 1, alpha1, acc)
            cw, lanes = where(w)
            o_ref[cw, :, lanes] = (acc / l).astype(BF16)
            lse_ref[cw, :, lanes] = m + jnp.log2(l)

    return pl.pallas_call(
        body, name="attn_fwd", grid=(ATTN_HEADS, nq // n_ch),
        in_specs=[pl.BlockSpec((n_ch, LANES, t), lambda h, i: (i, h, 0)),
                  pl.BlockSpec((s, ATTN_KV_W), lambda h, i: (0, 0)),
                  pl.BlockSpec((nk, ATTN_HEAD_DIM, t), lambda h, i: (0, h // ATTN_GROUP, 0))],
        out_specs=[pl.BlockSpec((n_ch, ATTN_HEAD_DIM, t), lambda h, i: (i, h, 0)),
                   pl.BlockSpec((None, n_ch, 1, t), lambda h, i: (h, i, 0, 0))],
        out_shape=[jax.ShapeDtypeStruct((nq, ATTN_Q_W, t), BF16), jax.ShapeDtypeStruct((ATTN_HEADS, nq, 1, t), F32)],
        scratch_shapes=[pltpu.VMEM((t, tq), F32)] * (2 * n_par) + [pltpu.VMEM((t, tq), BF16)] * (2 * n_par),
        compiler_params=_cparams(("parallel", "parallel")),
    )(q_ct, k_rows, v_ct)


def _attn_bwd(q_ct, do_ct, lse, delta, k_rows, v_rows, k_ct):
    nq, _, t = q_ct.shape
    s = nq * t
    kc = 4 if nq % 4 == 0 else 2
    tk = kc * t
    nk = nq // kc
    assert nq % 2 == 0 and nq % kc == 0

    def body(q_ref, do_ref, lse_ref, delta_ref, k_ref, v_ref, kt_ref, dq_ref, dk_ref, dv_ref, dk_acc, dv_acc,
             sb0, sb1, db0, db1, pb0, pb1, gb0, gb1):
        j = pl.program_id(1)
        sb, db, pb, gb = (sb0, sb1), (db0, db1), (pb0, pb1), (gb0, gb1)

        @pl.when(j == 0)
        def _():
            dq_ref[...] = jnp.zeros(dq_ref.shape, F32)

        kj, vj = k_ref[...], v_ref[...]
        ktj = jnp.concatenate([kt_ref[u] for u in range(kc)], axis=1)
        dk_acc[...] = jnp.zeros(dk_acc.shape, F32)
        dv_acc[...] = jnp.zeros(dv_acc.shape, F32)

        def products(i, slot):
            sb[slot][...] = jnp.dot(kj, q_ref[i], preferred_element_type=F32)
            db[slot][...] = jnp.dot(vj, do_ref[i], preferred_element_type=F32)

        def cotangents(i, slot):
            pt = jnp.exp2(sb[slot][...] - lse_ref[i])
            pb[slot][...] = pt.astype(BF16)
            gb[slot][...] = (pt * (db[slot][...] - delta_ref[i])).astype(BF16)

        def accumulate(i, slot):
            dst = gb[slot][...]
            dv_acc[...] += _nt(pb[slot][...], do_ref[i])
            dk_acc[...] += _nt(dst, q_ref[i])
            dq_ref[i] += jnp.dot(ktj, dst, preferred_element_type=F32)

        products(0, 0)
        products(1, 1)
        cotangents(0, 0)

        def trip(n, carry):
            c = 2 * n
            accumulate(c, 0)
            cotangents(c + 1, 1)
            products(c + 2, 0)
            accumulate(c + 1, 1)
            cotangents(c + 2, 0)
            products(c + 3, 1)
            return carry

        lax.fori_loop(0, nq // 2 - 1, trip, 0)
        accumulate(nq - 2, 0)
        cotangents(nq - 1, 1)
        accumulate(nq - 1, 1)
        dk_ref[...] = dk_acc[...].astype(dk_ref.dtype)
        dv_ref[...] = dv_acc[...].astype(dv_ref.dtype)

    per_head = pl.BlockSpec((nq, LANES, t), lambda h, j: (0, h, 0))
    stat = pl.BlockSpec((None, nq, 1, t), lambda h, j: (h, 0, 0, 0))
    kv_rows = pl.BlockSpec((tk, ATTN_KV_W), lambda h, j: (j, 0))
    kv_out = pl.BlockSpec((None, tk, ATTN_KV_W), lambda h, j: (h, j, 0))
    return pl.pallas_call(
        body, name="attn_bwd", grid=(ATTN_HEADS, nk),
        in_specs=[per_head, per_head, stat, stat, kv_rows, kv_rows,
                  pl.BlockSpec((kc, ATTN_HEAD_DIM, t), lambda h, j: (j, h // ATTN_GROUP, 0))],
        out_specs=[pl.BlockSpec((nq, ATTN_HEAD_DIM, t), lambda h, j: (0, h, 0)), kv_out, kv_out],
        out_shape=[jax.ShapeDtypeStruct((nq, ATTN_Q_W, t), F32), jax.ShapeDtypeStruct((ATTN_HEADS, s, ATTN_KV_W), BF16),
                   jax.ShapeDtypeStruct((ATTN_HEADS, s, ATTN_KV_W), BF16)],
        scratch_shapes=([pltpu.VMEM((tk, ATTN_KV_W), F32)] * 2 + [pltpu.VMEM((tk, t), F32)] * 4 + [pltpu.VMEM((tk, t), BF16)] * 4),
        compiler_params=_cparams(("parallel", "arbitrary")),
    )(q_ct, do_ct, lse, delta, k_rows, v_rows, k_ct)


def _log_sigmoid(x):
    t = jnp.exp(-jnp.abs(x))
    log1p_t = jnp.where(t < 1e-2, t * (1.0 - t * (0.5 - t * (1.0 / 3.0))), jnp.log(1.0 + t))
    return jnp.minimum(x, 0.0) - log1p_t


def _decay_tables(logit, backward):
    c = RET_CHUNK
    lam = _log_sigmoid(jnp.full((c, c), logit, F32))
    ii = lax.broadcasted_iota(jnp.int32, (c, c), 0).astype(F32)
    jj = lax.broadcasted_iota(jnp.int32, (c, c), 1).astype(F32)
    if not backward:
        dist, dist_t = jnp.maximum(ii - jj, 0.0), jnp.maximum(jj - ii, 0.0)
        mask, mask_t = ii >= jj, jj >= ii
        e_q, e_k = ii + 1.0, (c - 1.0) - ii
    else:
        dist, dist_t = jnp.maximum(jj - ii, 0.0), jnp.maximum(ii - jj, 0.0)
        mask, mask_t = jj > ii, ii > jj
        e_q, e_k = c - ii, ii
    return dict(
        d=jnp.where(mask, jnp.exp(lam * dist), 0.0), d_t=jnp.where(mask_t, jnp.exp(lam * dist_t), 0.0), dist=dist,
        qdec=jnp.exp(lam * e_q), kdec=jnp.exp(lam * e_k), e_q=e_q, e_k=e_k, gam=jnp.exp(lam * c))


def _nt(a, b):
    return lax.dot_general(a, b, (((1,), (1,)), ((), ())), preferred_element_type=F32)


RET_SUB = 2


def _ret_fwd(logits, q, k, proj):
    s = q.shape[0]
    c = RET_CHUNK
    sub = RET_SUB
    nb = s // (c * sub)
    block = (lambda n: n, lambda n: nb - 1 - n)
    order = (tuple(range(sub)), tuple(reversed(range(sub))))
    vwin = _win(proj, c * sub, C_RV, RET_W)
    nv = len(vwin)
    vw = RET_W // nv
    per = 2 + nv

    def body(lg_ref, *refs):
        ins, outs, states = refs[:2 * per], refs[2 * per:2 * per + 4], refs[2 * per + 4:]

        @pl.when(pl.program_id(0) == 0)
        def _():
            for st in states:
                st[...] = jnp.zeros(st.shape, F32)

        for h in range(RET_HEADS):
            for d in range(2):
                q_ref, k_ref, v_refs = ins[d * per], ins[d * per + 1], ins[d * per + 2:(d + 1) * per]
                y_ref, st_ref, state = outs[2 * d], outs[2 * d + 1], states[d]
                tb = _decay_tables(lg_ref[d, h], bool(d))
                sl = slice(h * RET_HEAD_DIM, (h + 1) * RET_HEAD_DIM)
                off = h * RET_HEAD_DIM
                sh = state[h]
                for u in order[d]:
                    rows = slice(u * c, (u + 1) * c)
                    qh, kh = q_ref[rows, sl], k_ref[rows, sl]
                    vb = v_refs[off // vw][rows, off % vw:off % vw + RET_HEAD_DIM].astype(BF16)
                    a = _nt(qh.astype(BF16), kh.astype(BF16)) * tb["d"]
                    st_ref[u, h] = sh
                    y_ref[rows, sl] = (jnp.dot(a.astype(BF16), vb, preferred_element_type=F32)
                                       + jnp.dot((qh * tb["qdec"]).astype(BF16), sh.astype(BF16), preferred_element_type=F32))
                    sh = tb["gam"] * sh + jnp.dot((kh * tb["kdec"]).T.astype(BF16), vb, preferred_element_type=F32)
                state[h] = sh

    hmat = (RET_HEADS, RET_HEAD_DIM, RET_HEAD_DIM)
    in_specs, out_specs, args = [pl.BlockSpec(memory_space=pltpu.SMEM)], [], [logits]
    for d in range(2):
        rows = pl.BlockSpec((c * sub, RET_W), lambda n, d=d: (block[d](n), 0))
        in_specs += [rows, rows] + [pl.BlockSpec(sp.block_shape, lambda n, d=d, cb=sp.index_map(0)[1]: (block[d](n), cb)) for _, sp in vwin]
        args += [q, k] + [a for a, _ in vwin]
        out_specs += [rows, pl.BlockSpec((sub,) + hmat, lambda n, d=d: (block[d](n), 0, 0, 0))]
    return pl.pallas_call(
        body, name="ret_fwd", grid=(nb,), in_specs=in_specs, out_specs=out_specs,
        out_shape=[jax.ShapeDtypeStruct((s, RET_W), F32), jax.ShapeDtypeStruct((nb * sub,) + hmat, F32)] * 2,
        scratch_shapes=[pltpu.VMEM(hmat, F32)] * 2,
        compiler_params=_cparams(("arbitrary",)),
    )(*args)


def _ret_bwd(logits, q, k, proj, dy, st_f, st_b):
    s = q.shape[0]
    c = RET_CHUNK
    sub = RET_SUB
    nb = s // (c * sub)
    block = (lambda n: nb - 1 - n, lambda n: n)
    order = (tuple(reversed(range(sub))), tuple(range(sub)))
    vwin = _win(proj, c * sub, C_RV, RET_W)
    nv = len(vwin)
    vw = RET_W // nv
    per = 4 + nv

    def body(lg_ref, *refs):
        ins, outs, scr = refs[:2 * per], refs[2 * per:2 * per + 8], refs[2 * per + 8:]
        n = pl.program_id(0)

        @pl.when(n == 0)
        def _():
            for r in scr:
                r[...] = jnp.zeros(r.shape, F32)

        for h in range(RET_HEADS):
            for d in range(2):
                q_ref, k_ref, dy_ref, st_ref = ins[d * per:d * per + 4]
                v_refs = ins[d * per + 4:(d + 1) * per]
                dq_ref, dk_ref, dv_ref = outs[4 * d:4 * d + 3]
                dstate, lacc = scr[2 * d], scr[2 * d + 1]
                tb = _decay_tables(lg_ref[d, h], bool(d))
                sl = slice(h * RET_HEAD_DIM, (h + 1) * RET_HEAD_DIM)
                off = h * RET_HEAD_DIM
                dsh = dstate[h]
                lsum = lacc[h]
                for u in order[d]:
                    rows = slice(u * c, (u + 1) * c)
                    qh, kh, dyh = q_ref[rows, sl], k_ref[rows, sl], dy_ref[rows, sl]
                    vb = v_refs[off // vw][rows, off % vw:off % vw + RET_HEAD_DIM].astype(BF16)
                    qb, kb, dyb = qh.astype(BF16), kh.astype(BF16), dyh.astype(BF16)
                    sh = st_ref[u, h]
                    shb, dshb = sh.astype(BF16), dsh.astype(BF16)
                    qk = _nt(qb, kb)
                    g = _nt(dyb, vb) * tb["d"]
                    a_t = _nt(kb, qb) * tb["d_t"]
                    g_t = _nt(vb, dyb) * tb["d_t"]
                    qd, kd = qh * tb["qdec"], kh * tb["kdec"]
                    dqd = _nt(dyb, shb)
                    dkd = _nt(vb, dshb)
                    dq_ref[rows, sl] = (jnp.dot(g.astype(BF16), kb, preferred_element_type=F32) + dqd * tb["qdec"]).astype(dq_ref.dtype)
                    dk_ref[rows, sl] = (jnp.dot(g_t.astype(BF16), qb, preferred_element_type=F32) + dkd * tb["kdec"]).astype(dk_ref.dtype)
                    dv_ref[rows, sl] = (jnp.dot(a_t.astype(BF16), dyb, preferred_element_type=F32)
                                        + jnp.dot(kd.astype(BF16), dshb, preferred_element_type=F32)).astype(dv_ref.dtype)
                    lsum = lsum + (tb["dist"] * qk * g + tb["e_q"] * qd * dqd + tb["e_k"] * kd * dkd
                                   + float(c) * tb["gam"] * dsh * sh)
                    dsh = tb["gam"] * dsh + jnp.dot(qd.T.astype(BF16), dyb, preferred_element_type=F32)
                dstate[h] = dsh
                lacc[h] = lsum

        @pl.when(n == nb - 1)
        def _():
            for d in range(2):
                for h in range(RET_HEADS):
                    outs[4 * d + 3][h] = jnp.zeros((8, LANES), F32) + jnp.sum(scr[2 * d + 1][h])

    hmat = (RET_HEADS, RET_HEAD_DIM, RET_HEAD_DIM)
    in_specs, out_specs, args = [pl.BlockSpec(memory_space=pltpu.SMEM)], [], [logits]
    for d, states in enumerate((st_f, st_b)):
        rows = pl.BlockSpec((c * sub, RET_W), lambda n, d=d: (block[d](n), 0))
        in_specs += ([rows, rows, rows, pl.BlockSpec((sub,) + hmat, lambda n, d=d: (block[d](n), 0, 0, 0))]
                     + [pl.BlockSpec(sp.block_shape, lambda n, d=d, cb=sp.index_map(0)[1]: (block[d](n), cb)) for _, sp in vwin])
        args += [q, k, dy, states] + [a for a, _ in vwin]
        out_specs += [rows, rows, rows, pl.BlockSpec((RET_HEADS, 8, LANES), lambda n: (0, 0, 0))]
    return pl.pallas_call(
        body, name="ret_bwd", grid=(nb,), in_specs=in_specs, out_specs=out_specs,
        out_shape=([jax.ShapeDtypeStruct((s, RET_W), BF16)] * 3 + [jax.ShapeDtypeStruct((RET_HEADS, 8, LANES), F32)]) * 2,
        scratch_shapes=[pltpu.VMEM(hmat, F32)] * 4,
        compiler_params=_cparams(("arbitrary",)),
    )(*args)


def _local_step(x, p, target, w, small):
    s = x.shape[0]
    tabs = _rope_tables(s, ATTN_HEAD_DIM) + _rope_tables(s, RET_HEAD_DIM)
    g_mix, g_mlp, g_ple = small["mix_norm"][None, :], small["mlp_norm"][None, :], small["ple_norm"][None, :]
    g_final, g_ret = small["final_norm"][None, :], small["ret_norm_gain"][None, :]
    gq_w = jnp.tile(small["attn_q_norm"], ATTN_HEADS)[None, :]
    gk_w = jnp.tile(small["attn_k_norm"], ATTN_KV_HEADS)[None, :]
    logits = small["ret_decay_logit"]

    hb = _stage_norm_in(x, g_mix)
    proj = _mm("in_proj", hb, w["w_in"], tm=512, tn=IN_W // 2, tk=1024, out_dtypes=(BF16,), j_outer=True)
    q_ct, k_rows, k_ct, v_rows, v_ct, rq, rk = _stage_qkv(proj, tabs, gq_w, gk_w)
    o_ct, lse = _attn_fwd(q_ct, k_rows, v_ct)
    ry_f, st_f, ry_b, st_b = _ret_fwd(logits, rq, rk, proj)
    rz, attn_rows = _stage_mix_post(ry_f, ry_b, proj, o_ct, g_ret)
    a_out = _mm("attn_o", attn_rows, w["w_attn_o"], tm=1024, tn=1024, tk=512, out_dtypes=(BF16,))
    r_out = _mm("ret_o", rz, w["w_ret_o"], tm=1024, tn=1024, tk=512, out_dtypes=(BF16,))
    merged = _stage_merge(proj, a_out, r_out)

    def epi_res_norm(acc, e, c):
        xr = e[0][...] + acc
        return xr, _rms_fwd(xr, c[0][...])

    x1, hm = _mm("out_proj", merged, w["w_out"], tm=512, tn=1024, tk=1024, out_dtypes=(F32, BF16),
                 epi=epi_res_norm, epi_ins=(x,), consts=(g_mlp,))

    def epi_relu2(acc, e, c):
        r = jnp.maximum(acc, 0.0)
        return (r * r,)

    act = _mm("mlp_up", hm, w["w_up"], tm=512, tn=2048, tk=1024, out_dtypes=(BF16,), epi=epi_relu2, j_outer=True)
    x2, hp = _mm("mlp_down", act, w["w_down"], tm=512, tn=1024, tk=D_FF, out_dtypes=(F32, BF16),
                 epi=epi_res_norm, epi_ins=(x1,), consts=(g_ple,))
    zg = _mm("ple_gate", hp, w["w_ple_gate"], tm=1024, tn=1024, tk=1024)
    pe = _mm("ple_emb", p, w["w_ple"], tm=1024, tn=1024, tk=256)
    dx3, dzg, dpe, loss_cols, g_final_p = _stage_head(zg, pe, x2, target, g_final)
    loss_sum = 0.5 / D_MODEL * jnp.sum(loss_cols)

    gw = {}
    gw["w_ple"] = _mm("g_w_ple", p, dpe, ta=True, tm=256, tn=1024, tk=2048)
    gw["w_ple_gate"] = _mm("g_w_ple_gate", hp, dzg, ta=True, tm=1024, tn=1024, tk=2048)
    def epi_norm_bwd(acc, e, c):
        dx, dg = _rms_bwd(acc, e[0][...], c[0][...])
        return e[1][...] + dx, dg

    def epi_norm_bwd_b(acc, e, c):
        tot, dg = epi_norm_bwd(acc, e, c)
        return tot, tot, dg

    dx2, dx2_b, g_ple_p = _mm("d_hp", dzg, w["w_ple_gate"], tb=True, tm=512, tn=1024, tk=1024, out_dtypes=(F32, BF16),
                              epi=epi_norm_bwd_b, epi_ins=(x2, dx3), consts=(g_ple,), n_sums=1)

    def epi_relu2_bwd(acc, e, c):
        return (acc * (2.0 * jnp.sqrt(e[0][...].astype(F32))),)

    du = _mm("d_u", dx2_b, w["w_down"], tb=True, tm=512, tn=2048, tk=1024, out_dtypes=(BF16,), epi=epi_relu2_bwd, epi_ins=(act,),
             j_outer=True)
    gw["w_down"] = _mm("g_w_down", act, dx2_b, ta=True, tm=1024, tn=1024, tk=2048)
    gw["w_up"] = _mm("g_w_up", hm, du, ta=True, tm=1024, tn=1024, tk=2048)
    dx1, dx1_b, g_mlp_p = _mm("d_hm", du, w["w_up"], tb=True, tm=512, tn=1024, tk=D_FF, out_dtypes=(F32, BF16),
                              epi=epi_norm_bwd_b, epi_ins=(x1, dx2), consts=(g_mlp,), n_sums=1)
    dmerged = _mm("d_merged", dx1_b, w["w_out"], tb=True, tm=1024, tn=1024, tk=1024)
    gw["w_out"] = _mm("g_w_out", merged, dx1_b, ta=True, tm=1024, tn=1024, tk=2048)
    dao, dro, dga, dgr = _stage_merge_bwd(proj, dmerged, a_out, r_out)
    gw["w_attn_o"] = _mm("g_w_attn_o", attn_rows, dao, ta=True, tm=512, tn=1024, tk=2048)
    gw["w_ret_o"] = _mm("g_w_ret_o", rz, dro, ta=True, tm=512, tn=1024, tk=2048)
    dattn = _mm("d_attn", dao, w["w_attn_o"], tb=True, tm=1024, tn=512, tk=1024)
    drz = _mm("d_rz", dro, w["w_ret_o"], tb=True, tm=1024, tn=512, tk=1024)
    do_ct, delta, dry, drg, g_ret_p = _stage_mix_post_bwd(dattn, attn_rows, drz, ry_f, ry_b, proj, g_ret)
    dq_f, dk_f, dv_f, dl_f, dq_b, dk_b, dv_b, dl_b = _ret_bwd(logits, rq, rk, proj, dry, st_f, st_b)
    dq_ct, dk8, dv8 = _attn_bwd(q_ct, do_ct, lse, delta, k_rows, v_rows, k_ct)
    dproj, gq_p, gk_p = _stage_dproj(proj, dq_ct, dk8, dv8, (dq_f, dk_f, dv_f, dq_b, dk_b, dv_b), drg, dga, dgr, tabs, gq_w, gk_w)
    gw["w_in"] = _mm("g_w_in", hb, dproj, ta=True, tm=1024, tn=IN_W // 2, tk=1024)
    grad_x, g_mix_p = _mm("d_h", dproj, w["w_in"], tb=True, tm=512, tn=1024, tk=IN_W, epi=epi_norm_bwd, epi_ins=(x, dx1),
                          consts=(g_mix,), n_sums=1)

    gs = {
        "mix_norm": g_mix_p[0], "mlp_norm": g_mlp_p[0], "ple_norm": g_ple_p[0], "final_norm": g_final_p[0],
        "ret_norm_gain": g_ret_p[0],
        "attn_q_norm": jnp.sum(gq_p[0].reshape(ATTN_HEADS, ATTN_HEAD_DIM), axis=0),
        "attn_k_norm": jnp.sum(gk_p[0].reshape(ATTN_KV_HEADS, ATTN_HEAD_DIM), axis=0),
        "ret_decay_logit": jnp.stack([dl_f[:, 0, 0], dl_b[:, 0, 0]]),
    }
    return loss_sum, grad_x, gw, gs


PACK_COLS = 1024
N_CHIPS = 4
HALF_ROWS = 2048


def _pack_shard(parts):
    return jnp.concatenate([parts[n].reshape(-1, PACK_COLS) for n, _ in BIG], axis=0)


def _unpack_shard(slab, shapes):
    out, r = {}, 0
    for n, _ in BIG:
        rows = math.prod(shapes[n]) // PACK_COLS
        out[n] = slab[r:r + rows].reshape(shapes[n])
        r += rows
    return out


def _shard_of(full, axis, sidx):
    size = full.shape[axis] // N_CHIPS
    return lax.slice_in_dim(full, sidx * size, (sidx + 1) * size, axis=axis)


def _position():
    x, y, c = lax.axis_index("x"), lax.axis_index("y"), lax.axis_index("c")
    return x, y, c


def _other_chips(x, y):
    return [(1 - x, y), (x, 1 - y), (1 - x, 1 - y)]


ANY = pl.BlockSpec(memory_space=pl.ANY)


def _gather_weights(slab):
    rows = slab.shape[0]
    half = rows // 2

    def body(in_ref, out_ref, send_sems, recv_sems):
        x, y, c = _position()
        chips = _other_chips(x, y)

        def piece(chip, core):
            return out_ref.at[2 * chip[0] + chip[1], pl.ds(core * half, half), :]

        def copy(k, chip, core, to, src=None):
            return pltpu.make_async_remote_copy(
                src_ref=piece(chip, core) if src is None else src, dst_ref=piece(chip, core),
                send_sem=send_sems.at[k], recv_sem=recv_sems.at[k], device_id=to, device_id_type=MESH)

        first = [copy(j, (x, y), c, (*chip, c), src=in_ref.at[pl.ds(c * half, half), :]) for j, chip in enumerate(chips)]
        for cp in first:
            cp.start()
        passed = [copy(3 + j, chip, c, (x, y, 1 - c)) for j, chip in enumerate(chips)]
        for j, chip in enumerate(chips):
            copy(j, chip, c, (x, y, c)).wait_recv()
            passed[j].start()
        for j, chip in enumerate(chips):
            copy(3 + j, chip, 1 - c, (x, y, c)).wait_recv()
        for cp in first + passed:
            cp.wait_send()

    return pl.pallas_call(
        body, name="gather_weights", in_specs=[ANY], out_specs=ANY,
        out_shape=jax.ShapeDtypeStruct((N_CHIPS,) + slab.shape, slab.dtype),
        scratch_shapes=[pltpu.SemaphoreType.DMA((6,)), pltpu.SemaphoreType.DMA((6,))],
    )(slab)


def _exchange_halves(g):
    def body(g_ref, out_ref, send_sem, recv_sem):
        x, y, c = _position()
        cp = pltpu.make_async_remote_copy(src_ref=g_ref.at[1 - c], dst_ref=out_ref, send_sem=send_sem, recv_sem=recv_sem,
                                          device_id=(x, y, 1 - c), device_id_type=MESH)
        cp.start()
        cp.wait()

    return pl.pallas_call(
        body, name="exchange_halves", in_specs=[ANY], out_specs=ANY,
        out_shape=jax.ShapeDtypeStruct(g.shape[1:], g.dtype),
        scratch_shapes=[pltpu.SemaphoreType.DMA, pltpu.SemaphoreType.DMA],
    )(g)


def _add_my_half(g, r1, c_idx):
    tr = 256
    nt = g.shape[2] // tr

    def body(c_ref, g_ref, r_ref, o_ref, ob_ref):
        tot = g_ref[...] + r_ref[...]
        o_ref[...] = tot
        ob_ref[...] = tot.astype(BF16)

    blk = (None, tr, PACK_COLS)
    spec = pl.BlockSpec(blk, lambda s, i, c_ref: (s, i, 0))
    return pl.pallas_call(
        body, name="add_my_half",
        grid_spec=pltpu.PrefetchScalarGridSpec(
            num_scalar_prefetch=1, grid=(N_CHIPS, nt),
            in_specs=[pl.BlockSpec((None,) + blk, lambda s, i, c_ref: (c_ref[0], s, i, 0)), spec],
            out_specs=[spec, spec]),
        out_shape=[jax.ShapeDtypeStruct(g.shape[1:], F32), jax.ShapeDtypeStruct(g.shape[1:], BF16)],
        compiler_params=_cparams(("parallel", "parallel")),
    )(c_idx, g, r1)


def _scatter_to_chips(part):
    def body(p_ref, out_ref, send_sems, recv_sems):
        x, y, c = _position()
        chips = _other_chips(x, y)
        sends = [pltpu.make_async_remote_copy(
            src_ref=p_ref.at[2 * chip[0] + chip[1]], dst_ref=out_ref.at[j], send_sem=send_sems.at[j], recv_sem=recv_sems.at[j],
            device_id=(*chip, c), device_id_type=MESH) for j, chip in enumerate(chips)]
        for cp in sends:
            cp.start()
        for cp in sends:
            cp.wait()

    return pl.pallas_call(
        body, name="scatter_to_chips", in_specs=[ANY], out_specs=ANY,
        out_shape=jax.ShapeDtypeStruct((N_CHIPS - 1,) + part.shape[1:], part.dtype),
        scratch_shapes=[pltpu.SemaphoreType.DMA((3,)), pltpu.SemaphoreType.DMA((3,))],
    )(part)


def _sum_chips(part, r2, chip_idx):
    tr = 256

    def body(c_ref, p_ref, r_ref, o_ref):
        o_ref[...] = ((p_ref[...] + r_ref[0]) + r_ref[1]) + r_ref[2]

    return pl.pallas_call(
        body, name="sum_chips",
        grid_spec=pltpu.PrefetchScalarGridSpec(
            num_scalar_prefetch=1, grid=(r2.shape[1] // tr,),
            in_specs=[pl.BlockSpec((None, tr, PACK_COLS), lambda i, c_ref: (c_ref[0], i, 0)),
                      pl.BlockSpec((N_CHIPS - 1, tr, PACK_COLS), lambda i, c_ref: (0, i, 0))],
            out_specs=pl.BlockSpec((tr, PACK_COLS), lambda i, c_ref: (i, 0))),
        out_shape=jax.ShapeDtypeStruct(r2.shape[1:], F32),
        compiler_params=_cparams(("parallel",)),
    )(chip_idx, part, r2)


def _join_halves(red):
    def body(r_ref, out_ref, send_sem, recv_sem):
        x, y, c = _position()
        cp = pltpu.make_async_remote_copy(src_ref=r_ref, dst_ref=out_ref, send_sem=send_sem, recv_sem=recv_sem,
                                          device_id=(x, y, 1 - c), device_id_type=MESH)
        cp.start()
        cp.wait()

    return pl.pallas_call(
        body, name="join_halves", in_specs=[ANY], out_specs=ANY,
        out_shape=jax.ShapeDtypeStruct(red.shape, red.dtype),
        scratch_shapes=[pltpu.SemaphoreType.DMA, pltpu.SemaphoreType.DMA],
    )(red)


def _adamw_math(w, g, m, v):
    m = ADAM_B1 * m + (1.0 - ADAM_B1) * g
    v = ADAM_B2 * v + (1.0 - ADAM_B2) * (g * g)
    m_hat = m / (1.0 - ADAM_B1 ** ADAM_STEP)
    v_hat = v / (1.0 - ADAM_B2 ** ADAM_STEP)
    delta = -ADAM_LR * (m_hat / (jnp.sqrt(v_hat) + ADAM_EPS) + ADAM_WD * w)
    return delta, m, v


def _adamw(name, w, g, m, v):
    tr = min(256, w.shape[0])

    def body(w_ref, g_ref, m_ref, v_ref, d_ref, nm_ref, nv_ref):
        d_ref[...], nm_ref[...], nv_ref[...] = _adamw_math(w_ref[...], g_ref[...], m_ref[...], v_ref[...])

    blk = pl.BlockSpec((tr, w.shape[1]), lambda i: (i, 0))
    return pl.pallas_call(
        body, name="adamw_" + name, grid=(w.shape[0] // tr,), in_specs=[blk] * 4, out_specs=[blk] * 3,
        out_shape=[jax.ShapeDtypeStruct(w.shape, F32)] * 3, compiler_params=_cparams(("parallel",)),
    )(w, g, m, v)


def _small_step(gpk, wpk, mpk, vpk):
    row, col, width = SMALL["ret_decay_logit"]

    def body(g_ref, w_ref, m_ref, v_ref, og_ref, od_ref, om_ref, ov_ref, gbuf, send_sems, recv_sems):
        x, y, c = _position()
        me = 4 * x + 2 * y + c
        gbuf[me] = g_ref[...]
        sends = []
        for k in range(1, 8):
            to = (x ^ (k >> 2), y ^ ((k >> 1) & 1), c ^ (k & 1))
            cp = pltpu.make_async_remote_copy(src_ref=g_ref, dst_ref=gbuf.at[me], send_sem=send_sems.at[k - 1],
                                              recv_sem=recv_sems.at[k - 1], device_id=to, device_id_type=MESH)
            cp.start()
            sends.append(cp)
        for k in range(1, 8):
            frm = me ^ k
            pltpu.make_async_remote_copy(src_ref=g_ref, dst_ref=gbuf.at[frm], send_sem=send_sems.at[k - 1],
                                         recv_sem=recv_sems.at[k - 1], device_id=(x, y, c), device_id_type=MESH).wait_recv()
        for cp in sends:
            cp.wait_send()
        tot = gbuf[0]
        for d in range(1, 8):
            tot = tot + gbuf[d]
        w = w_ref[...]
        r_i = lax.broadcasted_iota(jnp.int32, w.shape, 0)
        c_i = lax.broadcasted_iota(jnp.int32, w.shape, 1)
        is_logit = (r_i == row) & (c_i >= col) & (c_i < col + width)
        g = jnp.where(is_logit, tot * _sigmoid(-w), tot)
        og_ref[...] = g
        od_ref[...], om_ref[...], ov_ref[...] = _adamw_math(w, g, m_ref[...], v_ref[...])

    vm = pl.BlockSpec(memory_space=pltpu.VMEM)
    shp = jax.ShapeDtypeStruct(gpk.shape, F32)
    return pl.pallas_call(
        body, name="small_step", in_specs=[vm] * 4, out_specs=[vm] * 4, out_shape=[shp] * 4,
        scratch_shapes=[pltpu.VMEM((8,) + gpk.shape, F32), pltpu.SemaphoreType.DMA((7,)), pltpu.SemaphoreType.DMA((7,))],
    )(gpk, wpk, mpk, vpk)


def _pack_small(parts):
    rows = [[] for _ in range(SMALL_ROWS)]
    for n, (r, col, width) in sorted(SMALL.items(), key=lambda kv: (kv[1][0], kv[1][1])):
        rows[r].append((col, parts[n].reshape(-1).astype(F32)))
    out = []
    for r in range(SMALL_ROWS):
        segs, pos = [], 0
        for col, vec in rows[r]:
            assert col == pos
            segs.append(vec)
            pos += vec.shape[0]
        if pos < PACK_COLS:
            segs.append(jnp.zeros((PACK_COLS - pos,), F32))
        out.append(jnp.concatenate(segs))
    return jnp.stack(out)


def _unpack_small(pk, shapes):
    return {n: pk[r, col:col + width].reshape(shapes[n]) for n, (r, col, width) in SMALL.items()}


WEIGHTS = ("mix_norm", "w_in", "attn_q_norm", "attn_k_norm", "ret_decay_logit", "ret_norm_gain", "w_attn_o", "w_ret_o", "w_out",
           "mlp_norm", "w_up", "w_down", "ple_norm", "w_ple_gate", "w_ple", "final_norm")


def kernel(x, p, mix_norm, w_in, attn_q_norm, attn_k_norm, ret_decay_logit, ret_norm_gain, w_attn_o, w_ret_o, w_out, mlp_norm, w_up, w_down, ple_norm, w_ple_gate, w_ple, final_norm, loss_target, m_mix_norm, m_w_in, m_attn_q_norm, m_attn_k_norm, m_ret_decay_logit, m_ret_norm_gain, m_w_attn_o, m_w_ret_o, m_w_out, m_mlp_norm, m_w_up, m_w_down, m_ple_norm, m_w_ple_gate, m_w_ple, m_final_norm, v_mix_norm, v_w_in, v_attn_q_norm, v_attn_k_norm, v_ret_decay_logit, v_ret_norm_gain, v_w_attn_o, v_w_ret_o, v_w_out, v_mlp_norm, v_w_up, v_w_down, v_ple_norm, v_w_ple_gate, v_w_ple, v_final_norm):
    args = dict(locals())
    wts = {n: args[n] for n in WEIGHTS}
    ms = {n: args["m_" + n] for n in WEIGHTS}
    vs = {n: args["v_" + n] for n in WEIGHTS}
    shapes = {n: wts[n].shape for n in WEIGHTS}
    big_names = [n for n, _ in BIG]
    xi, yi, ci = _position()
    c_idx = ci.astype(jnp.int32).reshape(1)

    chip_idx = (2 * xi + yi).astype(jnp.int32)
    slab_b = _pack_shard({n: wts[n][0].astype(BF16) for n in big_names})
    gathered = lax.dynamic_update_slice(_gather_weights(slab_b), slab_b[None], (chip_idx, 0, 0))
    full = {}
    for n, axis in BIG:
        per_chip = [_unpack_shard(gathered[k], {m_: shapes[m_][1:] for m_ in big_names})[n] for k in range(N_CHIPS)]
        full[n] = jnp.concatenate(per_chip, axis=axis)
    small = {n: wts[n].reshape(wts[n].shape[1:] if wts[n].ndim > 1 else wts[n].shape) for n in SMALL}

    loss_part, grad_x, gw, gs = _local_step(x[0], p[0, 0], loss_target[0], full, small)
    loss = lax.psum(loss_part, ("x", "y", "c"))

    slabs = jnp.stack([_pack_shard({n: _shard_of(gw[n], axis, k) for n, axis in BIG}) for k in range(N_CHIPS)])
    halves = slabs.reshape(N_CHIPS, 2, HALF_ROWS, PACK_COLS).transpose(1, 0, 2, 3)
    chip_part, chip_part_b = _add_my_half(halves, _exchange_halves(halves), c_idx)
    mine = _sum_chips(chip_part, _scatter_to_chips(chip_part_b), chip_idx.reshape(1))
    both = jnp.stack([mine, _join_halves(mine)])
    reduced = jnp.where(ci == 0, both, both[::-1]).reshape(2 * HALF_ROWS, PACK_COLS)
    g_big = _unpack_shard(reduced, {n: shapes[n][1:] for n in big_names})
    big_out = [{}, {}, {}, {}]
    for n in big_names:
        big_out[0][n] = g_big[n][None]
        for kind, a in enumerate(_adamw(n, wts[n][0], g_big[n], ms[n][0], vs[n][0])):
            big_out[kind + 1][n] = a[None]

    sm_out = _small_step(_pack_small(gs), _pack_small({n: wts[n] for n in SMALL}), _pack_small({n: ms[n] for n in SMALL}),
                         _pack_small({n: vs[n] for n in SMALL}))
    small_out = [_unpack_small(a, {n: shapes[n] for n in SMALL}) for a in sm_out]

    outs = [loss, grad_x[None]]
    for kind in range(4):
        for n in WEIGHTS:
            outs.append(small_out[kind][n] if n in SMALL else big_out[kind][n])
    return tuple(outs)
```

```python
import functools
import math

import jax
import jax.numpy as jnp
from jax import lax
from jax.experimental import pallas as pl
from jax.experimental.pallas import tpu as pltpu

F32 = jnp.float32
BF16 = jnp.bfloat16
MESH = pl.DeviceIdType.MESH

D_MODEL = 1024
PLE_DIM = 256
GRID_W = 64
ATTN_HEAD_DIM = 64
ATTN_HEADS = 8
ATTN_KV_HEADS = 2
ATTN_GROUP = ATTN_HEADS // ATTN_KV_HEADS
RET_HEAD_DIM = 128
RET_HEADS = 4
ATTN_Q_W = 512
ATTN_KV_W = 128
RET_W = 512
IN_W = 4864
D_FF = 4096
RET_CHUNK = 128
ROPE_THETA = 10000.0
NORM_EPS = 1e-6
GN_EPS = 1e-5
ATTN_SCALE = ATTN_HEAD_DIM ** -0.5
LOG2E = math.log2(math.e)
Q_FOLD = ATTN_SCALE * LOG2E
RET_SCALE = RET_HEAD_DIM ** -0.5

C_AQ, C_AK, C_AV, C_RQ, C_RK, C_RV, C_RG, C_GA, C_GR = 0, 512, 640, 768, 1280, 1792, 2304, 2816, 3840

ADAM_LR = 0.001
ADAM_B1 = 0.9
ADAM_B2 = 0.999
ADAM_EPS = 1e-08
ADAM_WD = 0.01
ADAM_STEP = 10

LANES = 128
VMEM_LIMIT = 56 << 20
SEQ_TILE = 512

BIG = (("w_in", 1), ("w_attn_o", 1), ("w_ret_o", 1), ("w_out", 0), ("w_up", 1), ("w_down", 0), ("w_ple_gate", 0), ("w_ple", 1))
SMALL_ROWS = 8
SMALL = {"mix_norm": (0, 0, 1024), "mlp_norm": (1, 0, 1024), "ple_norm": (2, 0, 1024), "final_norm": (3, 0, 1024),
         "ret_norm_gain": (4, 0, 512), "attn_q_norm": (4, 512, 64), "attn_k_norm": (4, 576, 64), "ret_decay_logit": (4, 640, 8)}


def _seq_tile(s):
    return min(SEQ_TILE, s // 2)


def _cparams(sem=None, vmem=VMEM_LIMIT):
    return pltpu.CompilerParams(dimension_semantics=sem, vmem_limit_bytes=vmem)


def _mm(name, a, b, *, ta=False, tb=False, tm, tn, tk, out_dtypes=(F32,), epi=None, epi_ins=(), consts=(), n_sums=0, j_outer=False):
    if ta:
        kdim, m = a.shape
    else:
        m, kdim = a.shape
    n = b.shape[0] if tb else b.shape[1]
    tm, tn, tk = min(tm, m), min(tn, n), min(tk, kdim)
    assert m % tm == 0 and n % tn == 0 and kdim % tk == 0, (name, m, n, kdim, tm, tn, tk)
    nk = kdim // tk
    n_e, n_c, n_o = len(epi_ins), len(consts), len(out_dtypes)
    assert n_sums == 0 or tn == n

    def body(*refs):
        a_ref, b_ref = refs[0], refs[1]
        e_refs = refs[2:2 + n_e]
        c_refs = refs[2 + n_e:2 + n_e + n_c]
        o_refs = refs[2 + n_e + n_c:2 + n_e + n_c + n_o]
        s_refs = refs[2 + n_e + n_c + n_o:2 + n_e + n_c + n_o + n_sums]
        acc_ref = refs[2 + n_e + n_c + n_o + n_sums] if nk > 1 else None
        k = pl.program_id(2)
        if n_sums:
            @pl.when((pl.program_id(1 if j_outer else 0) == 0) & (k == 0))
            def _():
                for r in s_refs:
                    r[...] = jnp.zeros(r.shape, F32)
        av = a_ref[...].astype(BF16)
        bv = b_ref[...].astype(BF16)
        dims = (((0,) if ta else (1,), (1,) if tb else (0,)), ((), ()))
        part = lax.dot_general(av, bv, dims, preferred_element_type=F32)

        def finish(acc):
            vals = epi(acc, e_refs, c_refs) if epi is not None else (acc,)
            for o_ref, v in zip(o_refs, vals[:n_o]):
                o_ref[...] = v.astype(o_ref.dtype)
            for s_ref, v in zip(s_refs, vals[n_o:]):
                _acc_add(s_ref, v)

        if nk == 1:
            finish(part)
        else:
            @pl.when(k == 0)
            def _():
                acc_ref[...] = part

            @pl.when(k > 0)
            def _():
                acc_ref[...] += part

            @pl.when(k == nk - 1)
            def _():
                finish(acc_ref[...])

    def spec(shape, index):
        return pl.BlockSpec(shape, (lambda j, i, k: index(i, j, k)) if j_outer else index)

    a_spec = spec((tk, tm), lambda i, j, k: (k, i)) if ta else spec((tm, tk), lambda i, j, k: (i, k))
    b_spec = spec((tn, tk), lambda i, j, k: (j, k)) if tb else spec((tk, tn), lambda i, j, k: (k, j))
    o_spec = spec((tm, tn), lambda i, j, k: (i, j))
    c_specs = [spec(c.shape, lambda i, j, k, nd=c.ndim: (0,) * nd) for c in consts]
    outs = pl.pallas_call(
        body, name=name,
        grid=(n // tn, m // tm, nk) if j_outer else (m // tm, n // tn, nk),
        in_specs=[a_spec, b_spec] + [o_spec] * n_e + c_specs,
        out_specs=[o_spec] * n_o + [spec((8, n), lambda i, j, k: (0, 0))] * n_sums,
        out_shape=[jax.ShapeDtypeStruct((m, n), dt) for dt in out_dtypes] + [jax.ShapeDtypeStruct((8, n), F32)] * n_sums,
        scratch_shapes=[pltpu.VMEM((tm, tn), F32)] if nk > 1 else [],
        compiler_params=_cparams(("arbitrary",) * 3 if n_sums else ("parallel", "parallel", "arbitrary")),
    )(a, b, *epi_ins, *consts)
    return outs[0] if n_o + n_sums == 1 else outs


def _rows(arr, tr):
    return (arr, pl.BlockSpec((tr, arr.shape[1]), lambda i: (i, 0)))


def _win(arr, tr, start, width):
    bw = math.gcd(start, width) if start else width
    assert bw % LANES == 0
    return [(arr, pl.BlockSpec((tr, bw), lambda i, cb=start // bw + p: (i, cb))) for p in range(width // bw)]


def _ct(arr):
    return (arr, pl.BlockSpec((None,) + arr.shape[1:], lambda i: (i, 0, 0)))


def _whole(arr):
    return (arr, pl.BlockSpec(arr.shape, lambda i, nd=arr.ndim: (0,) * nd))


def _cat(refs):
    vals = [r[...].astype(F32) for r in refs]
    return vals[0] if len(vals) == 1 else jnp.concatenate(vals, axis=1)


def _seqtiled(name, fn, n_tiles, ins, outs, acc_widths=()):
    n_i, n_o, n_a = len(ins), len(outs), len(acc_widths)

    def body(*refs):
        i_refs, o_refs, a_refs = refs[:n_i], refs[n_i:n_i + n_o], refs[n_i + n_o:]
        if n_a:
            @pl.when(pl.program_id(0) == 0)
            def _():
                for r in a_refs:
                    r[...] = jnp.zeros(r.shape, F32)
        fn(list(i_refs), list(o_refs), list(a_refs))

    res = pl.pallas_call(
        body, name=name, grid=(n_tiles,),
        in_specs=[s for _, s in ins],
        out_specs=[s for _, _, s in outs] + [pl.BlockSpec((8, w), lambda i: (0, 0)) for w in acc_widths],
        out_shape=[jax.ShapeDtypeStruct(sh, dt) for sh, dt, _ in outs] + [jax.ShapeDtypeStruct((8, w), F32) for w in acc_widths],
        compiler_params=_cparams(("arbitrary",)),
    )(*[a for a, _ in ins])
    return res


def _acc_add(acc_ref, val):
    acc_ref[0:1, :] += jnp.sum(val, axis=0, keepdims=True)


def _out_rows(s, w, dt, tr):
    return ((s, w), dt, pl.BlockSpec((tr, w), lambda i: (i, 0)))


def _out_ct(s, w, dt, t):
    return ((s // t, w, t), dt, pl.BlockSpec((None, w, t), lambda i: (i, 0, 0)))


def _rms_fwd(x, gain):
    r = lax.rsqrt(jnp.mean(x * x, axis=-1, keepdims=True) + NORM_EPS)
    return x * r * gain


def _rms_bwd(dy, x, gain):
    r = lax.rsqrt(jnp.mean(x * x, axis=-1, keepdims=True) + NORM_EPS)
    xn = x * r
    dyg = dy * gain
    dx = r * (dyg - xn * jnp.mean(dyg * xn, axis=-1, keepdims=True))
    return dx, dy * xn


def _seg_mean(y, hd):
    w = y.shape[1]
    pieces = []
    for s in range(0, w, LANES):
        v = y[:, s:s + LANES]
        tot = jnp.sum(v, axis=1, keepdims=True)
        if hd == LANES:
            pieces.append(jnp.broadcast_to(tot, v.shape))
        else:
            low = lax.broadcasted_iota(jnp.int32, v.shape, 1) < hd
            lo = jnp.sum(jnp.where(low, v, 0.0), axis=1, keepdims=True)
            pieces.append(jnp.where(low, lo, tot - lo))
    out = pieces[0] if len(pieces) == 1 else jnp.concatenate(pieces, axis=1)
    return out * (1.0 / hd)


def _tile_lanes(t, w):
    return t if w == t.shape[1] else jnp.concatenate([t] * (w // t.shape[1]), axis=1)


def _swap_halves(x, hd):
    w = x.shape[1]
    half = hd // 2
    lane = lax.broadcasted_iota(jnp.int32, x.shape, 1)
    return jnp.where((lane % hd) < half, pltpu.roll(x, w - half, 1), pltpu.roll(x, half, 1))


def _rope(x, cos, sin_signed, hd):
    w = x.shape[1]
    return x * _tile_lanes(cos, w) + _swap_halves(x, hd) * _tile_lanes(sin_signed, w)


def _rope_t(dy, cos, sin_signed, hd):
    w = dy.shape[1]
    return dy * _tile_lanes(cos, w) + _swap_halves(dy * _tile_lanes(sin_signed, w), hd)


def _headnorm_fwd(x, gain_w, hd):
    r = lax.rsqrt(_seg_mean(x * x, hd) + NORM_EPS)
    return x * r * gain_w


def _headnorm_bwd(dy, x, gain_w, hd):
    r = lax.rsqrt(_seg_mean(x * x, hd) + NORM_EPS)
    xn = x * r
    dyg = dy * gain_w
    return r * (dyg - xn * _seg_mean(dyg * xn, hd)), dy * xn


def _sigmoid(x):
    return 1.0 / (1.0 + jnp.exp(-x))


def _rope_tables(seq_len, head_dim):
    rows = seq_len // GRID_W
    n_axis = head_dim // 4
    freqs = ROPE_THETA ** (-jnp.arange(n_axis, dtype=F32) / n_axis)
    ang_r = jnp.arange(rows, dtype=F32)[:, None] * freqs
    ang_c = jnp.arange(GRID_W, dtype=F32)[:, None] * freqs

    def expand(by_row, by_col):
        r = jnp.broadcast_to(by_row[:, None, :], (rows, GRID_W, n_axis))
        c = jnp.broadcast_to(by_col[None, :, :], (rows, GRID_W, n_axis))
        return jnp.concatenate([r, c], axis=-1).reshape(seq_len, 2 * n_axis)

    cos, sin = expand(jnp.cos(ang_r), jnp.cos(ang_c)), expand(jnp.sin(ang_r), jnp.sin(ang_c))
    reps = LANES // head_dim
    return jnp.tile(jnp.concatenate([cos, cos], axis=-1), (1, reps)), jnp.tile(jnp.concatenate([-sin, sin], axis=-1), (1, reps))


def _stage_norm_in(x, gain):
    s = x.shape[0]
    tr = min(SEQ_TILE, s)

    def fn(i, o, a):
        o[0][...] = _rms_fwd(i[0][...], i[1][...]).astype(BF16)

    return _seqtiled("norm_in", fn, s // tr, [_rows(x, tr), _whole(gain)], [_out_rows(s, D_MODEL, BF16, tr)])[0]


def _stage_qkv(proj, tabs, gq_w, gk_w):
    s = proj.shape[0]
    t = _seq_tile(s)
    ca, sa, cr, sr = tabs
    ins = (_win(proj, t, C_AQ, ATTN_Q_W) + _win(proj, t, C_AK, ATTN_KV_W) + _win(proj, t, C_AV, ATTN_KV_W)
           + _win(proj, t, C_RQ, RET_W) + _win(proj, t, C_RK, RET_W)
           + [_rows(ca, t), _rows(sa, t), _rows(cr, t), _rows(sr, t), _whole(gq_w), _whole(gk_w)])

    def fn(i, o, a):
        aq, ak, av = (i[n][...].astype(F32) for n in range(3))
        rq, rk = _cat(i[3:5]), _cat(i[5:7])
        ca_, sa_, cr_, sr_ = i[7][...], i[8][...], i[9][...], i[10][...]
        qr = _rope(_headnorm_fwd(aq, i[11][...], ATTN_HEAD_DIM), ca_, sa_, ATTN_HEAD_DIM) * Q_FOLD
        kr = _rope(_headnorm_fwd(ak, i[12][...], ATTN_HEAD_DIM), ca_, sa_, ATTN_HEAD_DIM)
        qt = qr.T.astype(BF16)
        zeros = jnp.zeros((ATTN_HEAD_DIM, t), BF16)
        for h in range(ATTN_HEADS):
            g = h // ATTN_GROUP
            blk = qt[h * ATTN_HEAD_DIM:(h + 1) * ATTN_HEAD_DIM, :]
            o[0][h * LANES + g * ATTN_HEAD_DIM:h * LANES + (g + 1) * ATTN_HEAD_DIM, :] = blk
            o[0][h * LANES + (1 - g) * ATTN_HEAD_DIM:h * LANES + (2 - g) * ATTN_HEAD_DIM, :] = zeros
        o[1][...] = kr.astype(BF16)
        o[2][...] = kr.T.astype(BF16)
        o[3][...] = av.astype(BF16)
        o[4][...] = av.T.astype(BF16)
        o[5][...] = _rope(rq, cr_, sr_, RET_HEAD_DIM) * RET_SCALE
        o[6][...] = _rope(rk, cr_, sr_, RET_HEAD_DIM)

    outs = [_out_ct(s, ATTN_HEADS * LANES, BF16, t), _out_rows(s, ATTN_KV_W, BF16, t), _out_ct(s, ATTN_KV_W, BF16, t),
            _out_rows(s, ATTN_KV_W, BF16, t), _out_ct(s, ATTN_KV_W, BF16, t), _out_rows(s, RET_W, F32, t), _out_rows(s, RET_W, F32, t)]
    return _seqtiled("qkv_prep", fn, s // t, ins, outs)


def _groupnorm_gate(ry, rg, gain):
    mu = _seg_mean(ry, RET_HEAD_DIM)
    d = ry - mu
    rs = lax.rsqrt(_seg_mean(d * d, RET_HEAD_DIM) + GN_EPS)
    return d * rs, rs, _sigmoid(rg)


def _stage_mix_post(ry_f, ry_b, proj, o_ct, gain):
    s = proj.shape[0]
    t = _seq_tile(s)
    ins = [_rows(ry_f, t), _rows(ry_b, t)] + _win(proj, t, C_RG, RET_W) + [_ct(o_ct), _whole(gain)]

    def fn(i, o, a):
        ry = i[0][...] + i[1][...]
        rg = _cat(i[2:4])
        gn, _, sg = _groupnorm_gate(ry, rg, None)
        o[0][...] = (gn * i[5][...] * (rg * sg)).astype(BF16)
        o[1][...] = i[4][...].astype(F32).T.astype(BF16)

    return _seqtiled("mix_post", fn, s // t, ins, [_out_rows(s, RET_W, BF16, t), _out_rows(s, ATTN_Q_W, BF16, t)])


def _stage_merge(proj, a_out, r_out):
    s = proj.shape[0]
    tr = min(SEQ_TILE, s)
    ins = _win(proj, tr, C_GA, D_MODEL) + _win(proj, tr, C_GR, D_MODEL) + [_rows(a_out, tr), _rows(r_out, tr)]
    na = len(_win(proj, tr, C_GA, D_MODEL))

    def fn(i, o, a):
        ga, gr = _cat(i[:na]), _cat(i[na:2 * na])
        o[0][...] = (_sigmoid(ga) * i[2 * na][...] + _sigmoid(gr) * i[2 * na + 1][...]).astype(BF16)

    return _seqtiled("merge", fn, s // tr, ins, [_out_rows(s, D_MODEL, BF16, tr)])[0]


def _stage_head(zg, pe, x2, target, g_final):
    s = x2.shape[0]
    tr = min(SEQ_TILE // 2, s)
    ins = [_rows(zg, tr), _rows(pe, tr), _rows(x2, tr), _rows(target, tr), _whole(g_final)]

    def fn(i, o, a):
        gt = _sigmoid(i[0][...])
        pe_ = i[1][...]
        x3 = i[2][...] + gt * pe_
        gf = i[4][...]
        r3 = lax.rsqrt(jnp.mean(x3 * x3, axis=-1, keepdims=True) + NORM_EPS)
        x3n = x3 * r3
        e = x3n * gf - i[3][...]
        _acc_add(a[0], e * e)
        dy = e * (1.0 / D_MODEL)
        _acc_add(a[1], dy * x3n)
        dyg = dy * gf
        dx3 = r3 * (dyg - x3n * jnp.mean(dyg * x3n, axis=-1, keepdims=True))
        o[0][...] = dx3
        o[1][...] = (dx3 * pe_ * gt * (1.0 - gt)).astype(BF16)
        o[2][...] = (dx3 * gt).astype(BF16)

    outs = [_out_rows(s, D_MODEL, F32, tr), _out_rows(s, D_MODEL, BF16, tr), _out_rows(s, D_MODEL, BF16, tr)]
    return _seqtiled("head", fn, s // tr, ins, outs, acc_widths=(D_MODEL, D_MODEL))


def _stage_merge_bwd(proj, dmerged, a_out, r_out):
    s = proj.shape[0]
    tr = min(SEQ_TILE // 2, s)
    wins = _win(proj, tr, C_GA, D_MODEL)
    na = len(wins)
    ins = wins + _win(proj, tr, C_GR, D_MODEL) + [_rows(dmerged, tr), _rows(a_out, tr), _rows(r_out, tr)]

    def fn(i, o, a):
        sa, sr = _sigmoid(_cat(i[:na])), _sigmoid(_cat(i[na:2 * na]))
        dm = i[2 * na][...]
        o[0][...] = (dm * sa).astype(BF16)
        o[1][...] = (dm * sr).astype(BF16)
        o[2][...] = (dm * i[2 * na + 1][...] * sa * (1.0 - sa)).astype(BF16)
        o[3][...] = (dm * i[2 * na + 2][...] * sr * (1.0 - sr)).astype(BF16)

    return _seqtiled("merge_bwd", fn, s // tr, ins, [_out_rows(s, D_MODEL, BF16, tr)] * 4)


def _stage_mix_post_bwd(dattn, attn_rows, drz, ry_f, ry_b, proj, gain):
    s = proj.shape[0]
    t = _seq_tile(s)
    ins = ([_rows(dattn, t), _rows(attn_rows, t), _rows(drz, t), _rows(ry_f, t), _rows(ry_b, t)]
           + _win(proj, t, C_RG, RET_W) + [_whole(gain)])

    def fn(i, o, a):
        da = i[0][...]
        dat = da.T
        prod_t = (da * i[1][...].astype(F32)).T
        dat_b = dat.astype(BF16)
        zeros = jnp.zeros((ATTN_HEAD_DIM, t), BF16)
        for h in range(ATTN_HEADS):
            g = h // ATTN_GROUP
            o[0][h * LANES + g * ATTN_HEAD_DIM:h * LANES + (g + 1) * ATTN_HEAD_DIM, :] = dat_b[h * ATTN_HEAD_DIM:(h + 1) * ATTN_HEAD_DIM, :]
            o[0][h * LANES + (1 - g) * ATTN_HEAD_DIM:h * LANES + (2 - g) * ATTN_HEAD_DIM, :] = zeros
            o[1][h] = jnp.sum(prod_t[h * ATTN_HEAD_DIM:(h + 1) * ATTN_HEAD_DIM, :], axis=0, keepdims=True)
        ry = i[3][...] + i[4][...]
        rg = _cat(i[5:7])
        gain_ = i[7][...]
        gn, rs, sg = _groupnorm_gate(ry, rg, None)
        dz = i[2][...]
        silu = rg * sg
        _acc_add(a[0], dz * gn * silu)
        dgn = dz * gain_ * silu
        o[2][...] = rs * (dgn - _seg_mean(dgn, RET_HEAD_DIM) - gn * _seg_mean(dgn * gn, RET_HEAD_DIM))
        o[3][...] = (dz * gn * gain_ * (sg * (1.0 + rg * (1.0 - sg)))).astype(BF16)

    outs = [_out_ct(s, ATTN_HEADS * LANES, BF16, t),
            ((ATTN_HEADS, s // t, 1, t), F32, pl.BlockSpec((ATTN_HEADS, None, 1, t), lambda i: (0, i, 0, 0))),
            _out_rows(s, RET_W, F32, t), _out_rows(s, RET_W, BF16, t)]
    return _seqtiled("mix_post_bwd", fn, s // t, ins, outs, acc_widths=(RET_W,))


def _stage_dproj(proj, dq_ct, dk8, dv8, rgrads, drg, dga, dgr, tabs, gq_w, gk_w):
    s = proj.shape[0]
    t = _seq_tile(s)
    ca, sa, cr, sr = tabs
    kv8 = pl.BlockSpec((ATTN_HEADS, t, ATTN_KV_W), lambda i: (0, i, 0))
    ins = (_win(proj, t, C_AQ, ATTN_Q_W) + _win(proj, t, C_AK, ATTN_KV_W) + [_ct(dq_ct), (dk8, kv8), (dv8, kv8)]
           + [_rows(g, t) for g in rgrads] + [_rows(drg, t), _rows(dga, t), _rows(dgr, t)]
           + [_rows(ca, t), _rows(sa, t), _rows(cr, t), _rows(sr, t), _whole(gq_w), _whole(gk_w)])

    def fn(i, o, a):
        aq, ak = i[0][...].astype(F32), i[1][...].astype(F32)
        dq_f, dk_f, dv_f, dq_b, dk_b, dv_b = (r[...].astype(F32) for r in i[5:11])
        ca_, sa_, cr_, sr_ = i[14][...], i[15][...], i[16][...], i[17][...]
        dqn = _rope_t(i[2][...].T * ATTN_SCALE, ca_, sa_, ATTN_HEAD_DIM)
        daq, gq_rows = _headnorm_bwd(dqn, aq, i[18][...], ATTN_HEAD_DIM)
        dkn = _rope_t(jnp.sum(i[3][...].astype(F32), axis=0) * (1.0 / LOG2E), ca_, sa_, ATTN_HEAD_DIM)
        dak, gk_rows = _headnorm_bwd(dkn, ak, i[19][...], ATTN_HEAD_DIM)
        _acc_add(a[0], gq_rows)
        _acc_add(a[1], gk_rows)
        out = o[0]
        out[:, C_AQ:C_AQ + ATTN_Q_W] = daq.astype(BF16)
        out[:, C_AK:C_AK + ATTN_KV_W] = dak.astype(BF16)
        out[:, C_AV:C_AV + ATTN_KV_W] = jnp.sum(i[4][...].astype(F32), axis=0).astype(BF16)
        out[:, C_RQ:C_RQ + RET_W] = _rope_t((dq_f + dq_b) * RET_SCALE, cr_, sr_, RET_HEAD_DIM).astype(BF16)
        out[:, C_RK:C_RK + RET_W] = _rope_t(dk_f + dk_b, cr_, sr_, RET_HEAD_DIM).astype(BF16)
        out[:, C_RV:C_RV + RET_W] = (dv_f + dv_b).astype(BF16)
        out[:, C_RG:C_RG + RET_W] = i[11][...]
        out[:, C_GA:C_GA + D_MODEL] = i[12][...]
        out[:, C_GR:C_GR + D_MODEL] = i[13][...]

    return _seqtiled("dproj", fn, s // t, ins, [_out_rows(s, IN_W, BF16, t)], acc_widths=(ATTN_Q_W, ATTN_KV_W))


def _attn_fwd(q_ct, k_rows, v_ct):
    nq, _, t = q_ct.shape
    s = nq * t
    nk = nq
    assert nk % 2 == 0
    n_ch = next(n for n in (8, 4, 2) if nq % n == 0)
    halves = 2 if t % (2 * LANES) == 0 else 1
    tq = t // halves
    n_par = n_ch * halves

    def body(q_ref, k_ref, v_ref, o_ref, lse_ref, *bufs):
        sbuf = tuple(bufs[2 * w:2 * w + 2] for w in range(n_par))
        pbuf = tuple(bufs[2 * n_par + 2 * w:2 * n_par + 2 * w + 2] for w in range(n_par))

        def where(w):
            return w // halves, slice((w % halves) * tq, (w % halves + 1) * tq)

        def scores(w, j, slot):
            kj = k_ref[pl.ds(pl.multiple_of(j * t, t), t), :]
            cw, lanes = where(w)
            st = jnp.dot(kj, q_ref[cw, :, lanes], preferred_element_type=F32)
            sbuf[w][slot][...] = st
            return jnp.max(st, axis=0, keepdims=True)

        def probs(w, slot, cmax, m, l):
            m_new = jnp.maximum(m, cmax)
            alpha = jnp.exp2(m - m_new)
            pt = jnp.exp2(sbuf[w][slot][...] - m_new)
            pbuf[w][slot][...] = pt.astype(BF16)
            return m_new, alpha * l + jnp.sum(pt, axis=0, keepdims=True), alpha

        def values(w, j, slot, alpha, acc):
            return alpha * acc + jnp.dot(v_ref[j], pbuf[w][slot][...], preferred_element_type=F32)

        init = []
        for w in range(n_par):
            m = jnp.full((1, tq), -1e30, F32)
            l = jnp.zeros((1, tq), F32)
            cmax0 = scores(w, 0, 0)
            cmax1 = scores(w, 1, 1)
            m, l, alpha0 = probs(w, 0, cmax0, m, l)
            init.append((m, l, jnp.zeros((ATTN_HEAD_DIM, tq), F32), cmax1, alpha0))

        def trip(n, carry):
            c = 2 * n
            out = []
            for w in range(n_par):
                m, l, acc, cmax_b, alpha_c = carry[w]
                acc = values(w, c, 0, alpha_c, acc)
                m, l, alpha1 = probs(w, 1, cmax_b, m, l)
                cmax2 = scores(w, c + 2, 0)
                acc = values(w, c + 1, 1, alpha1, acc)
                m, l, alpha2 = probs(w, 0, cmax2, m, l)
                cmax3 = scores(w, c + 3, 1)
                out.append((m, l, acc, cmax3, alpha2))
            return tuple(out)

        res = lax.fori_loop(0, nk // 2 - 1, trip, tuple(init))
        for w in range(n_par):
            m, l, acc, cmax_b, alpha_c = res[w]
            acc = values(w, nk - 2, 0, alpha_c, acc)
            m, l, alpha1 = probs(w, 1, cmax_b, m, l)
            acc = values(w, nk - 1, 1, alpha1, acc)
            cw, lanes = where(w)
            o_ref[cw, :, lanes] = (acc / l).astype(BF16)
            lse_ref[cw, :, lanes] = m + jnp.log2(l)

    return pl.pallas_call(
        body, name="attn_fwd", grid=(ATTN_HEADS, nq // n_ch),
        in_specs=[pl.BlockSpec((n_ch, LANES, t), lambda h, i: (i, h, 0)),
                  pl.BlockSpec((s, ATTN_KV_W), lambda h, i: (0, 0)),
                  pl.BlockSpec((nk, ATTN_HEAD_DIM, t), lambda h, i: (0, h // ATTN_GROUP, 0))],
        out_specs=[pl.BlockSpec((n_ch, ATTN_HEAD_DIM, t), lambda h, i: (i, h, 0)),
                   pl.BlockSpec((None, n_ch, 1, t), lambda h, i: (h, i, 0, 0))],
        out_shape=[jax.ShapeDtypeStruct((nq, ATTN_Q_W, t), BF16), jax.ShapeDtypeStruct((ATTN_HEADS, nq, 1, t), F32)],
        scratch_shapes=[pltpu.VMEM((t, tq), F32)] * (2 * n_par) + [pltpu.VMEM((t, tq), BF16)] * (2 * n_par),
        compiler_params=_cparams(("parallel", "parallel")),
    )(q_ct, k_rows, v_ct)


def _attn_bwd(q_ct, do_ct, lse, delta, k_rows, v_rows, k_ct):
    nq, _, t = q_ct.shape
    s = nq * t
    kc = 4 if nq % 4 == 0 else 2
    tk = kc * t
    nk = nq // kc
    assert nq % 2 == 0 and nq % kc == 0

    def body(q_ref, do_ref, lse_ref, delta_ref, k_ref, v_ref, kt_ref, dq_ref, dk_ref, dv_ref, dk_acc, dv_acc,
             sb0, sb1, db0, db1, pb0, pb1, gb0, gb1):
        j = pl.program_id(1)
        sb, db, pb, gb = (sb0, sb1), (db0, db1), (pb0, pb1), (gb0, gb1)

        @pl.when(j == 0)
        def _():
            dq_ref[...] = jnp.zeros(dq_ref.shape, F32)

        kj, vj = k_ref[...], v_ref[...]
        ktj = jnp.concatenate([kt_ref[u] for u in range(kc)], axis=1)
        dk_acc[...] = jnp.zeros(dk_acc.shape, F32)
        dv_acc[...] = jnp.zeros(dv_acc.shape, F32)

        def products(i, slot):
            sb[slot][...] = jnp.dot(kj, q_ref[i], preferred_element_type=F32)
            db[slot][...] = jnp.dot(vj, do_ref[i], preferred_element_type=F32)

        def cotangents(i, slot):
            pt = jnp.exp2(sb[slot][...] - lse_ref[i])
            pb[slot][...] = pt.astype(BF16)
            gb[slot][...] = (pt * (db[slot][...] - delta_ref[i])).astype(BF16)

        def accumulate(i, slot):
            dst = gb[slot][...]
            dv_acc[...] += _nt(pb[slot][...], do_ref[i])
            dk_acc[...] += _nt(dst, q_ref[i])
            dq_ref[i] += jnp.dot(ktj, dst, preferred_element_type=F32)

        products(0, 0)
        products(1, 1)
        cotangents(0, 0)

        def trip(n, carry):
            c = 2 * n
            accumulate(c, 0)
            cotangents(c + 1, 1)
            products(c + 2, 0)
            accumulate(c + 1, 1)
            cotangents(c + 2, 0)
            products(c + 3, 1)
            return carry

        lax.fori_loop(0, nq // 2 - 1, trip, 0)
        accumulate(nq - 2, 0)
        cotangents(nq - 1, 1)
        accumulate(nq - 1, 1)
        dk_ref[...] = dk_acc[...].astype(dk_ref.dtype)
        dv_ref[...] = dv_acc[...].astype(dv_ref.dtype)

    per_head = pl.BlockSpec((nq, LANES, t), lambda h, j: (0, h, 0))
    stat = pl.BlockSpec((None, nq, 1, t), lambda h, j: (h, 0, 0, 0))
    kv_rows = pl.BlockSpec((tk, ATTN_KV_W), lambda h, j: (j, 0))
    kv_out = pl.BlockSpec((None, tk, ATTN_KV_W), lambda h, j: (h, j, 0))
    return pl.pallas_call(
        body, name="attn_bwd", grid=(ATTN_HEADS, nk),
        in_specs=[per_head, per_head, stat, stat, kv_rows, kv_rows,
                  pl.BlockSpec((kc, ATTN_HEAD_DIM, t), lambda h, j: (j, h // ATTN_GROUP, 0))],
        out_specs=[pl.BlockSpec((nq, ATTN_HEAD_DIM, t), lambda h, j: (0, h, 0)), kv_out, kv_out],
        out_shape=[jax.ShapeDtypeStruct((nq, ATTN_Q_W, t), F32), jax.ShapeDtypeStruct((ATTN_HEADS, s, ATTN_KV_W), BF16),
                   jax.ShapeDtypeStruct((ATTN_HEADS, s, ATTN_KV_W), BF16)],
        scratch_shapes=([pltpu.VMEM((tk, ATTN_KV_W), F32)] * 2 + [pltpu.VMEM((tk, t), F32)] * 4 + [pltpu.VMEM((tk, t), BF16)] * 4),
        compiler_params=_cparams(("parallel", "arbitrary")),
    )(q_ct, do_ct, lse, delta, k_rows, v_rows, k_ct)


def _log_sigmoid(x):
    t = jnp.exp(-jnp.abs(x))
    log1p_t = jnp.where(t < 1e-2, t * (1.0 - t * (0.5 - t * (1.0 / 3.0))), jnp.log(1.0 + t))
    return jnp.minimum(x, 0.0) - log1p_t


def _decay_tables(logit, backward):
    c = RET_CHUNK
    lam = _log_sigmoid(jnp.full((c, c), logit, F32))
    ii = lax.broadcasted_iota(jnp.int32, (c, c), 0).astype(F32)
    jj = lax.broadcasted_iota(jnp.int32, (c, c), 1).astype(F32)
    if not backward:
        dist, dist_t = jnp.maximum(ii - jj, 0.0), jnp.maximum(jj - ii, 0.0)
        mask, mask_t = ii >= jj, jj >= ii
        e_q, e_k = ii + 1.0, (c - 1.0) - ii
    else:
        dist, dist_t = jnp.maximum(jj - ii, 0.0), jnp.maximum(ii - jj, 0.0)
        mask, mask_t = jj > ii, ii > jj
        e_q, e_k = c - ii, ii
    return dict(
        d=jnp.where(mask, jnp.exp(lam * dist), 0.0), d_t=jnp.where(mask_t, jnp.exp(lam * dist_t), 0.0), dist=dist,
        qdec=jnp.exp(lam * e_q), kdec=jnp.exp(lam * e_k), e_q=e_q, e_k=e_k, gam=jnp.exp(lam * c))


def _nt(a, b):
    return lax.dot_general(a, b, (((1,), (1,)), ((), ())), preferred_element_type=F32)


def _ret_sub(n_chunks):
    return 4 if n_chunks % 4 == 0 else 2


def _ret_fwd(logits, q, k, proj):
    s = q.shape[0]
    c = RET_CHUNK
    sub = _ret_sub(s // c)
    nb = s // (c * sub)
    block = (lambda n: n, lambda n: nb - 1 - n)
    order = (tuple(range(sub)), tuple(reversed(range(sub))))
    vwin = _win(proj, c * sub, C_RV, RET_W)
    nv = len(vwin)
    vw = RET_W // nv
    per = 2 + nv

    def body(lg_ref, *refs):
        ins, outs, states = refs[:2 * per], refs[2 * per:2 * per + 4], refs[2 * per + 4:]

        @pl.when(pl.program_id(0) == 0)
        def _():
            for st in states:
                st[...] = jnp.zeros(st.shape, F32)

        for h in range(RET_HEADS):
            for d in range(2):
                q_ref, k_ref, v_refs = ins[d * per], ins[d * per + 1], ins[d * per + 2:(d + 1) * per]
                y_ref, st_ref, state = outs[2 * d], outs[2 * d + 1], states[d]
                tb = _decay_tables(lg_ref[d, h], bool(d))
                sl = slice(h * RET_HEAD_DIM, (h + 1) * RET_HEAD_DIM)
                off = h * RET_HEAD_DIM
                sh = state[h]
                for u in order[d]:
                    rows = slice(u * c, (u + 1) * c)
                    qh, kh = q_ref[rows, sl], k_ref[rows, sl]
                    vb = v_refs[off // vw][rows, off % vw:off % vw + RET_HEAD_DIM].astype(BF16)
                    a = _nt(qh.astype(BF16), kh.astype(BF16)) * tb["d"]
                    st_ref[u, h] = sh
                    y_ref[rows, sl] = (jnp.dot(a.astype(BF16), vb, preferred_element_type=F32)
                                       + jnp.dot((qh * tb["qdec"]).astype(BF16), sh.astype(BF16), preferred_element_type=F32))
                    sh = tb["gam"] * sh + jnp.dot((kh * tb["kdec"]).T.astype(BF16), vb, preferred_element_type=F32)
                state[h] = sh

    hmat = (RET_HEADS, RET_HEAD_DIM, RET_HEAD_DIM)
    in_specs, out_specs, args = [pl.BlockSpec(memory_space=pltpu.SMEM)], [], [logits]
    for d in range(2):
        rows = pl.BlockSpec((c * sub, RET_W), lambda n, d=d: (block[d](n), 0))
        in_specs += [rows, rows] + [pl.BlockSpec(sp.block_shape, lambda n, d=d, cb=sp.index_map(0)[1]: (block[d](n), cb)) for _, sp in vwin]
        args += [q, k] + [a for a, _ in vwin]
        out_specs += [rows, pl.BlockSpec((sub,) + hmat, lambda n, d=d: (block[d](n), 0, 0, 0))]
    return pl.pallas_call(
        body, name="ret_fwd", grid=(nb,), in_specs=in_specs, out_specs=out_specs,
        out_shape=[jax.ShapeDtypeStruct((s, RET_W), F32), jax.ShapeDtypeStruct((nb * sub,) + hmat, F32)] * 2,
        scratch_shapes=[pltpu.VMEM(hmat, F32)] * 2,
        compiler_params=_cparams(("arbitrary",)),
    )(*args)


def _ret_bwd(logits, q, k, proj, dy, st_f, st_b):
    s = q.shape[0]
    c = RET_CHUNK
    sub = _ret_sub(s // c)
    nb = s // (c * sub)
    block = (lambda n: nb - 1 - n, lambda n: n)
    order = (tuple(reversed(range(sub))), tuple(range(sub)))
    vwin = _win(proj, c * sub, C_RV, RET_W)
    nv = len(vwin)
    vw = RET_W // nv
    per = 4 + nv

    def body(lg_ref, *refs):
        ins, outs, scr = refs[:2 * per], refs[2 * per:2 * per + 8], refs[2 * per + 8:]
        n = pl.program_id(0)

        @pl.when(n == 0)
        def _():
            for r in scr:
                r[...] = jnp.zeros(r.shape, F32)

        for h in range(RET_HEADS):
            for d in range(2):
                q_ref, k_ref, dy_ref, st_ref = ins[d * per:d * per + 4]
                v_refs = ins[d * per + 4:(d + 1) * per]
                dq_ref, dk_ref, dv_ref = outs[4 * d:4 * d + 3]
                dstate, lacc = scr[2 * d], scr[2 * d + 1]
                tb = _decay_tables(lg_ref[d, h], bool(d))
                sl = slice(h * RET_HEAD_DIM, (h + 1) * RET_HEAD_DIM)
                off = h * RET_HEAD_DIM
                dsh = dstate[h]
                lsum = lacc[h]
                for u in order[d]:
                    rows = slice(u * c, (u + 1) * c)
                    qh, kh, dyh = q_ref[rows, sl], k_ref[rows, sl], dy_ref[rows, sl]
                    vb = v_refs[off // vw][rows, off % vw:off % vw + RET_HEAD_DIM].astype(BF16)
                    qb, kb, dyb = qh.astype(BF16), kh.astype(BF16), dyh.astype(BF16)
                    sh = st_ref[u, h]
                    shb, dshb = sh.astype(BF16), dsh.astype(BF16)
                    qk = _nt(qb, kb)
                    g = _nt(dyb, vb) * tb["d"]
                    a_t = _nt(kb, qb) * tb["d_t"]
                    g_t = _nt(vb, dyb) * tb["d_t"]
                    qd, kd = qh * tb["qdec"], kh * tb["kdec"]
                    dqd = _nt(dyb, shb)
                    dkd = _nt(vb, dshb)
                    dq_ref[rows, sl] = (jnp.dot(g.astype(BF16), kb, preferred_element_type=F32) + dqd * tb["qdec"]).astype(dq_ref.dtype)
                    dk_ref[rows, sl] = (jnp.dot(g_t.astype(BF16), qb, preferred_element_type=F32) + dkd * tb["kdec"]).astype(dk_ref.dtype)
                    dv_ref[rows, sl] = (jnp.dot(a_t.astype(BF16), dyb, preferred_element_type=F32)
                                        + jnp.dot(kd.astype(BF16), dshb, preferred_element_type=F32)).astype(dv_ref.dtype)
                    lsum = lsum + (tb["dist"] * qk * g + tb["e_q"] * qd * dqd + tb["e_k"] * kd * dkd
                                   + float(c) * tb["gam"] * dsh * sh)
                    dsh = tb["gam"] * dsh + jnp.dot(qd.T.astype(BF16), dyb, preferred_element_type=F32)
                dstate[h] = dsh
                lacc[h] = lsum

        @pl.when(n == nb - 1)
        def _():
            for d in range(2):
                for h in range(RET_HEADS):
                    outs[4 * d + 3][h] = jnp.zeros((8, LANES), F32) + jnp.sum(scr[2 * d + 1][h])

    hmat = (RET_HEADS, RET_HEAD_DIM, RET_HEAD_DIM)
    in_specs, out_specs, args = [pl.BlockSpec(memory_space=pltpu.SMEM)], [], [logits]
    for d, states in enumerate((st_f, st_b)):
        rows = pl.BlockSpec((c * sub, RET_W), lambda n, d=d: (block[d](n), 0))
        in_specs += ([rows, rows, rows, pl.BlockSpec((sub,) + hmat, lambda n, d=d: (block[d](n), 0, 0, 0))]
                     + [pl.BlockSpec(sp.block_shape, lambda n, d=d, cb=sp.index_map(0)[1]: (block[d](n), cb)) for _, sp in vwin])
        args += [q, k, dy, states] + [a for a, _ in vwin]
        out_specs += [rows, rows, rows, pl.BlockSpec((RET_HEADS, 8, LANES), lambda n: (0, 0, 0))]
    return pl.pallas_call(
        body, name="ret_bwd", grid=(nb,), in_specs=in_specs, out_specs=out_specs,
        out_shape=([jax.ShapeDtypeStruct((s, RET_W), BF16)] * 3 + [jax.ShapeDtypeStruct((RET_HEADS, 8, LANES), F32)]) * 2,
        scratch_shapes=[pltpu.VMEM(hmat, F32)] * 4,
        compiler_params=_cparams(("arbitrary",)),
    )(*args)


def _local_step(x, p, target, w, small):
    s = x.shape[0]
    tabs = _rope_tables(s, ATTN_HEAD_DIM) + _rope_tables(s, RET_HEAD_DIM)
    g_mix, g_mlp, g_ple = small["mix_norm"][None, :], small["mlp_norm"][None, :], small["ple_norm"][None, :]
    g_final, g_ret = small["final_norm"][None, :], small["ret_norm_gain"][None, :]
    gq_w = jnp.tile(small["attn_q_norm"], ATTN_HEADS)[None, :]
    gk_w = jnp.tile(small["attn_k_norm"], ATTN_KV_HEADS)[None, :]
    logits = small["ret_decay_logit"]

    hb = _stage_norm_in(x, g_mix)
    proj = _mm("in_proj", hb, w["w_in"], tm=512, tn=IN_W // 2, tk=1024, out_dtypes=(BF16,), j_outer=True)
    q_ct, k_rows, k_ct, v_rows, v_ct, rq, rk = _stage_qkv(proj, tabs, gq_w, gk_w)
    o_ct, lse = _attn_fwd(q_ct, k_rows, v_ct)
    ry_f, st_f, ry_b, st_b = _ret_fwd(logits, rq, rk, proj)
    rz, attn_rows = _stage_mix_post(ry_f, ry_b, proj, o_ct, g_ret)
    a_out = _mm("attn_o", attn_rows, w["w_attn_o"], tm=1024, tn=1024, tk=512, out_dtypes=(BF16,))
    r_out = _mm("ret_o", rz, w["w_ret_o"], tm=1024, tn=1024, tk=512, out_dtypes=(BF16,))
    merged = _stage_merge(proj, a_out, r_out)

    def epi_res_norm(acc, e, c):
        xr = e[0][...] + acc
        return xr, _rms_fwd(xr, c[0][...])

    x1, hm = _mm("out_proj", merged, w["w_out"], tm=512, tn=1024, tk=1024, out_dtypes=(F32, BF16),
                 epi=epi_res_norm, epi_ins=(x,), consts=(g_mlp,))

    def epi_relu2(acc, e, c):
        r = jnp.maximum(acc, 0.0)
        return (r * r,)

    act = _mm("mlp_up", hm, w["w_up"], tm=512, tn=2048, tk=1024, out_dtypes=(BF16,), epi=epi_relu2, j_outer=True)
    x2, hp = _mm("mlp_down", act, w["w_down"], tm=512, tn=1024, tk=D_FF, out_dtypes=(F32, BF16),
                 epi=epi_res_norm, epi_ins=(x1,), consts=(g_ple,))
    zg = _mm("ple_gate", hp, w["w_ple_gate"], tm=1024, tn=1024, tk=1024)
    pe = _mm("ple_emb", p, w["w_ple"], tm=1024, tn=1024, tk=256)
    dx3, dzg, dpe, loss_cols, g_final_p = _stage_head(zg, pe, x2, target, g_final)
    loss_sum = 0.5 / D_MODEL * jnp.sum(loss_cols)

    gw = {}
    gw["w_ple"] = _mm("g_w_ple", p, dpe, ta=True, tm=256, tn=1024, tk=2048)
    gw["w_ple_gate"] = _mm("g_w_ple_gate", hp, dzg, ta=True, tm=1024, tn=1024, tk=2048)
    def epi_norm_bwd(acc, e, c):
        dx, dg = _rms_bwd(acc, e[0][...], c[0][...])
        return e[1][...] + dx, dg

    def epi_norm_bwd_b(acc, e, c):
        tot, dg = epi_norm_bwd(acc, e, c)
        return tot, tot, dg

    dx2, dx2_b, g_ple_p = _mm("d_hp", dzg, w["w_ple_gate"], tb=True, tm=512, tn=1024, tk=1024, out_dtypes=(F32, BF16),
                              epi=epi_norm_bwd_b, epi_ins=(x2, dx3), consts=(g_ple,), n_sums=1)

    def epi_relu2_bwd(acc, e, c):
        return (acc * (2.0 * jnp.sqrt(e[0][...].astype(F32))),)

    du = _mm("d_u", dx2_b, w["w_down"], tb=True, tm=512, tn=2048, tk=1024, out_dtypes=(BF16,), epi=epi_relu2_bwd, epi_ins=(act,),
             j_outer=True)
    gw["w_down"] = _mm("g_w_down", act, dx2_b, ta=True, tm=1024, tn=1024, tk=2048)
    gw["w_up"] = _mm("g_w_up", hm, du, ta=True, tm=1024, tn=1024, tk=2048)
    dx1, dx1_b, g_mlp_p = _mm("d_hm", du, w["w_up"], tb=True, tm=512, tn=1024, tk=D_FF, out_dtypes=(F32, BF16),
                              epi=epi_norm_bwd_b, epi_ins=(x1, dx2), consts=(g_mlp,), n_sums=1)
    dmerged = _mm("d_merged", dx1_b, w["w_out"], tb=True, tm=1024, tn=1024, tk=1024)
    gw["w_out"] = _mm("g_w_out", merged, dx1_b, ta=True, tm=1024, tn=1024, tk=2048)
    dao, dro, dga, dgr = _stage_merge_bwd(proj, dmerged, a_out, r_out)
    gw["w_attn_o"] = _mm("g_w_attn_o", attn_rows, dao, ta=True, tm=512, tn=1024, tk=2048)
    gw["w_ret_o"] = _mm("g_w_ret_o", rz, dro, ta=True, tm=512, tn=1024, tk=2048)
    dattn = _mm("d_attn", dao, w["w_attn_o"], tb=True, tm=1024, tn=512, tk=1024)
    drz = _mm("d_rz", dro, w["w_ret_o"], tb=True, tm=1024, tn=512, tk=1024)
    do_ct, delta, dry, drg, g_ret_p = _stage_mix_post_bwd(dattn, attn_rows, drz, ry_f, ry_b, proj, g_ret)
    dq_f, dk_f, dv_f, dl_f, dq_b, dk_b, dv_b, dl_b = _ret_bwd(logits, rq, rk, proj, dry, st_f, st_b)
    dq_ct, dk8, dv8 = _attn_bwd(q_ct, do_ct, lse, delta, k_rows, v_rows, k_ct)
    dproj, gq_p, gk_p = _stage_dproj(proj, dq_ct, dk8, dv8, (dq_f, dk_f, dv_f, dq_b, dk_b, dv_b), drg, dga, dgr, tabs, gq_w, gk_w)
    gw["w_in"] = _mm("g_w_in", hb, dproj, ta=True, tm=1024, tn=IN_W // 2, tk=1024)
    grad_x, g_mix_p = _mm("d_h", dproj, w["w_in"], tb=True, tm=512, tn=1024, tk=IN_W, epi=epi_norm_bwd, epi_ins=(x, dx1),
                          consts=(g_mix,), n_sums=1)

    gs = {
        "mix_norm": g_mix_p[0], "mlp_norm": g_mlp_p[0], "ple_norm": g_ple_p[0], "final_norm": g_final_p[0],
        "ret_norm_gain": g_ret_p[0],
        "attn_q_norm": jnp.sum(gq_p[0].reshape(ATTN_HEADS, ATTN_HEAD_DIM), axis=0),
        "attn_k_norm": jnp.sum(gk_p[0].reshape(ATTN_KV_HEADS, ATTN_HEAD_DIM), axis=0),
        "ret_decay_logit": jnp.stack([dl_f[:, 0, 0], dl_b[:, 0, 0]]),
    }
    return loss_sum, grad_x, gw, gs


PACK_COLS = 1024
N_CHIPS = 4
HALF_ROWS = 2048


def _pack_shard(parts):
    return jnp.concatenate([parts[n].reshape(-1, PACK_COLS) for n, _ in BIG], axis=0)


def _unpack_shard(slab, shapes):
    out, r = {}, 0
    for n, _ in BIG:
        rows = math.prod(shapes[n]) // PACK_COLS
        out[n] = slab[r:r + rows].reshape(shapes[n])
        r += rows
    return out


def _shard_of(full, axis, sidx):
    size = full.shape[axis] // N_CHIPS
    return lax.slice_in_dim(full, sidx * size, (sidx + 1) * size, axis=axis)


def _position():
    x, y, c = lax.axis_index("x"), lax.axis_index("y"), lax.axis_index("c")
    return x, y, c


def _other_chips(x, y):
    return [(1 - x, y), (x, 1 - y), (1 - x, 1 - y)]


ANY = pl.BlockSpec(memory_space=pl.ANY)


def _gather_weights(slab):
    rows = slab.shape[0]
    half = rows // 2

    def body(in_ref, out_ref, send_sems, recv_sems):
        x, y, c = _position()
        chips = _other_chips(x, y)

        def piece(chip, core):
            return out_ref.at[2 * chip[0] + chip[1], pl.ds(core * half, half), :]

        def copy(k, chip, core, to, src=None):
            return pltpu.make_async_remote_copy(
                src_ref=piece(chip, core) if src is None else src, dst_ref=piece(chip, core),
                send_sem=send_sems.at[k], recv_sem=recv_sems.at[k], device_id=to, device_id_type=MESH)

        first = [copy(j, (x, y), c, (*chip, c), src=in_ref.at[pl.ds(c * half, half), :]) for j, chip in enumerate(chips)]
        for cp in first:
            cp.start()
        passed = [copy(3 + j, chip, c, (x, y, 1 - c)) for j, chip in enumerate(chips)]
        for j, chip in enumerate(chips):
            copy(j, chip, c, (x, y, c)).wait_recv()
            passed[j].start()
        for j, chip in enumerate(chips):
            copy(3 + j, chip, 1 - c, (x, y, c)).wait_recv()
        for cp in first + passed:
            cp.wait_send()

    return pl.pallas_call(
        body, name="gather_weights", in_specs=[ANY], out_specs=ANY,
        out_shape=jax.ShapeDtypeStruct((N_CHIPS,) + slab.shape, slab.dtype),
        scratch_shapes=[pltpu.SemaphoreType.DMA((6,)), pltpu.SemaphoreType.DMA((6,))],
    )(slab)


def _exchange_halves(g):
    def body(g_ref, out_ref, send_sem, recv_sem):
        x, y, c = _position()
        cp = pltpu.make_async_remote_copy(src_ref=g_ref.at[1 - c], dst_ref=out_ref, send_sem=send_sem, recv_sem=recv_sem,
                                          device_id=(x, y, 1 - c), device_id_type=MESH)
        cp.start()
        cp.wait()

    return pl.pallas_call(
        body, name="exchange_halves", in_specs=[ANY], out_specs=ANY,
        out_shape=jax.ShapeDtypeStruct(g.shape[1:], g.dtype),
        scratch_shapes=[pltpu.SemaphoreType.DMA, pltpu.SemaphoreType.DMA],
    )(g)


def _add_my_half(g, r1, c_idx):
    tr = 256
    nt = g.shape[2] // tr

    def body(c_ref, g_ref, r_ref, o_ref, ob_ref):
        tot = g_ref[...] + r_ref[...]
        o_ref[...] = tot
        ob_ref[...] = tot.astype(BF16)

    blk = (None, tr, PACK_COLS)
    spec = pl.BlockSpec(blk, lambda s, i, c_ref: (s, i, 0))
    return pl.pallas_call(
        body, name="add_my_half",
        grid_spec=pltpu.PrefetchScalarGridSpec(
            num_scalar_prefetch=1, grid=(N_CHIPS, nt),
            in_specs=[pl.BlockSpec((None,) + blk, lambda s, i, c_ref: (c_ref[0], s, i, 0)), spec],
            out_specs=[spec, spec]),
        out_shape=[jax.ShapeDtypeStruct(g.shape[1:], F32), jax.ShapeDtypeStruct(g.shape[1:], BF16)],
        compiler_params=_cparams(("parallel", "parallel")),
    )(c_idx, g, r1)


def _scatter_to_chips(part):
    def body(p_ref, out_ref, send_sems, recv_sems):
        x, y, c = _position()
        chips = _other_chips(x, y)
        sends = [pltpu.make_async_remote_copy(
            src_ref=p_ref.at[2 * chip[0] + chip[1]], dst_ref=out_ref.at[j], send_sem=send_sems.at[j], recv_sem=recv_sems.at[j],
            device_id=(*chip, c), device_id_type=MESH) for j, chip in enumerate(chips)]
        for cp in sends:
            cp.start()
        for cp in sends:
            cp.wait()

    return pl.pallas_call(
        body, name="scatter_to_chips", in_specs=[ANY], out_specs=ANY,
        out_shape=jax.ShapeDtypeStruct((N_CHIPS - 1,) + part.shape[1:], part.dtype),
        scratch_shapes=[pltpu.SemaphoreType.DMA((3,)), pltpu.SemaphoreType.DMA((3,))],
    )(part)


def _sum_chips(part, r2, chip_idx):
    tr = 256

    def body(c_ref, p_ref, r_ref, o_ref):
        o_ref[...] = ((p_ref[...] + r_ref[0]) + r_ref[1]) + r_ref[2]

    return pl.pallas_call(
        body, name="sum_chips",
        grid_spec=pltpu.PrefetchScalarGridSpec(
            num_scalar_prefetch=1, grid=(r2.shape[1] // tr,),
            in_specs=[pl.BlockSpec((None, tr, PACK_COLS), lambda i, c_ref: (c_ref[0], i, 0)),
                      pl.BlockSpec((N_CHIPS - 1, tr, PACK_COLS), lambda i, c_ref: (0, i, 0))],
            out_specs=pl.BlockSpec((tr, PACK_COLS), lambda i, c_ref: (i, 0))),
        out_shape=jax.ShapeDtypeStruct(r2.shape[1:], F32),
        compiler_params=_cparams(("parallel",)),
    )(chip_idx, part, r2)


def _join_halves(red):
    def body(r_ref, out_ref, send_sem, recv_sem):
        x, y, c = _position()
        cp = pltpu.make_async_remote_copy(src_ref=r_ref, dst_ref=out_ref, send_sem=send_sem, recv_sem=recv_sem,
                                          device_id=(x, y, 1 - c), device_id_type=MESH)
        cp.start()
        cp.wait()

    return pl.pallas_call(
        body, name="join_halves", in_specs=[ANY], out_specs=ANY,
        out_shape=jax.ShapeDtypeStruct(red.shape, red.dtype),
        scratch_shapes=[pltpu.SemaphoreType.DMA, pltpu.SemaphoreType.DMA],
    )(red)


def _adamw_math(w, g, m, v):
    m = ADAM_B1 * m + (1.0 - ADAM_B1) * g
    v = ADAM_B2 * v + (1.0 - ADAM_B2) * (g * g)
    m_hat = m / (1.0 - ADAM_B1 ** ADAM_STEP)
    v_hat = v / (1.0 - ADAM_B2 ** ADAM_STEP)
    delta = -ADAM_LR * (m_hat / (jnp.sqrt(v_hat) + ADAM_EPS) + ADAM_WD * w)
    return delta, m, v


def _adamw(name, w, g, m, v):
    tr = min(256, w.shape[0])

    def body(w_ref, g_ref, m_ref, v_ref, d_ref, nm_ref, nv_ref):
        d_ref[...], nm_ref[...], nv_ref[...] = _adamw_math(w_ref[...], g_ref[...], m_ref[...], v_ref[...])

    blk = pl.BlockSpec((tr, w.shape[1]), lambda i: (i, 0))
    return pl.pallas_call(
        body, name="adamw_" + name, grid=(w.shape[0] // tr,), in_specs=[blk] * 4, out_specs=[blk] * 3,
        out_shape=[jax.ShapeDtypeStruct(w.shape, F32)] * 3, compiler_params=_cparams(("parallel",)),
    )(w, g, m, v)


def _small_step(gpk, wpk, mpk, vpk):
    row, col, width = SMALL["ret_decay_logit"]

    def body(g_ref, w_ref, m_ref, v_ref, og_ref, od_ref, om_ref, ov_ref, gbuf, send_sems, recv_sems):
        x, y, c = _position()
        me = 4 * x + 2 * y + c
        gbuf[me] = g_ref[...]
        sends = []
        for k in range(1, 8):
            to = (x ^ (k >> 2), y ^ ((k >> 1) & 1), c ^ (k & 1))
            cp = pltpu.make_async_remote_copy(src_ref=g_ref, dst_ref=gbuf.at[me], send_sem=send_sems.at[k - 1],
                                              recv_sem=recv_sems.at[k - 1], device_id=to, device_id_type=MESH)
            cp.start()
            sends.append(cp)
        for k in range(1, 8):
            frm = me ^ k
            pltpu.make_async_remote_copy(src_ref=g_ref, dst_ref=gbuf.at[frm], send_sem=send_sems.at[k - 1],
                                         recv_sem=recv_sems.at[k - 1], device_id=(x, y, c), device_id_type=MESH).wait_recv()
        for cp in sends:
            cp.wait_send()
        tot = gbuf[0]
        for d in range(1, 8):
            tot = tot + gbuf[d]
        w = w_ref[...]
        r_i = lax.broadcasted_iota(jnp.int32, w.shape, 0)
        c_i = lax.broadcasted_iota(jnp.int32, w.shape, 1)
        is_logit = (r_i == row) & (c_i >= col) & (c_i < col + width)
        g = jnp.where(is_logit, tot * _sigmoid(-w), tot)
        og_ref[...] = g
        od_ref[...], om_ref[...], ov_ref[...] = _adamw_math(w, g, m_ref[...], v_ref[...])

    vm = pl.BlockSpec(memory_space=pltpu.VMEM)
    shp = jax.ShapeDtypeStruct(gpk.shape, F32)
    return pl.pallas_call(
        body, name="small_step", in_specs=[vm] * 4, out_specs=[vm] * 4, out_shape=[shp] * 4,
        scratch_shapes=[pltpu.VMEM((8,) + gpk.shape, F32), pltpu.SemaphoreType.DMA((7,)), pltpu.SemaphoreType.DMA((7,))],
    )(gpk, wpk, mpk, vpk)


def _pack_small(parts):
    rows = [[] for _ in range(SMALL_ROWS)]
    for n, (r, col, width) in sorted(SMALL.items(), key=lambda kv: (kv[1][0], kv[1][1])):
        rows[r].append((col, parts[n].reshape(-1).astype(F32)))
    out = []
    for r in range(SMALL_ROWS):
        segs, pos = [], 0
        for col, vec in rows[r]:
            assert col == pos
            segs.append(vec)
            pos += vec.shape[0]
        if pos < PACK_COLS:
            segs.append(jnp.zeros((PACK_COLS - pos,), F32))
        out.append(jnp.concatenate(segs))
    return jnp.stack(out)


def _unpack_small(pk, shapes):
    return {n: pk[r, col:col + width].reshape(shapes[n]) for n, (r, col, width) in SMALL.items()}


WEIGHTS = ("mix_norm", "w_in", "attn_q_norm", "attn_k_norm", "ret_decay_logit", "ret_norm_gain", "w_attn_o", "w_ret_o", "w_out",
           "mlp_norm", "w_up", "w_down", "ple_norm", "w_ple_gate", "w_ple", "final_norm")


def kernel(x, p, mix_norm, w_in, attn_q_norm, attn_k_norm, ret_decay_logit, ret_norm_gain, w_attn_o, w_ret_o, w_out, mlp_norm, w_up, w_down, ple_norm, w_ple_gate, w_ple, final_norm, loss_target, m_mix_norm, m_w_in, m_attn_q_norm, m_attn_k_norm, m_ret_decay_logit, m_ret_norm_gain, m_w_attn_o, m_w_ret_o, m_w_out, m_mlp_norm, m_w_up, m_w_down, m_ple_norm, m_w_ple_gate, m_w_ple, m_final_norm, v_mix_norm, v_w_in, v_attn_q_norm, v_attn_k_norm, v_ret_decay_logit, v_ret_norm_gain, v_w_attn_o, v_w_ret_o, v_w_out, v_mlp_norm, v_w_up, v_w_down, v_ple_norm, v_w_ple_gate, v_w_ple, v_final_norm):
    args = dict(locals())
    wts = {n: args[n] for n in WEIGHTS}
    ms = {n: args["m_" + n] for n in WEIGHTS}
    vs = {n: args["v_" + n] for n in WEIGHTS}
    shapes = {n: wts[n].shape for n in WEIGHTS}
    big_names = [n for n, _ in BIG]
    xi, yi, ci = _position()
    c_idx = ci.astype(jnp.int32).reshape(1)

    chip_idx = (2 * xi + yi).astype(jnp.int32)
    slab_b = _pack_shard({n: wts[n][0].astype(BF16) for n in big_names})
    gathered = lax.dynamic_update_slice(_gather_weights(slab_b), slab_b[None], (chip_idx, 0, 0))
    full = {}
    for n, axis in BIG:
        per_chip = [_unpack_shard(gathered[k], {m_: shapes[m_][1:] for m_ in big_names})[n] for k in range(N_CHIPS)]
        full[n] = jnp.concatenate(per_chip, axis=axis)
    small = {n: wts[n].reshape(wts[n].shape[1:] if wts[n].ndim > 1 else wts[n].shape) for n in SMALL}

    loss_part, grad_x, gw, gs = _local_step(x[0], p[0, 0], loss_target[0], full, small)
    loss = lax.psum(loss_part, ("x", "y", "c"))

    slabs = jnp.stack([_pack_shard({n: _shard_of(gw[n], axis, k) for n, axis in BIG}) for k in range(N_CHIPS)])
    halves = slabs.reshape(N_CHIPS, 2, HALF_ROWS, PACK_COLS).transpose(1, 0, 2, 3)
    chip_part, chip_part_b = _add_my_half(halves, _exchange_halves(halves), c_idx)
    mine = _sum_chips(chip_part, _scatter_to_chips(chip_part_b), chip_idx.reshape(1))
    both = jnp.stack([mine, _join_halves(mine)])
    reduced = jnp.where(ci == 0, both, both[::-1]).reshape(2 * HALF_ROWS, PACK_COLS)
    g_big = _unpack_shard(reduced, {n: shapes[n][1:] for n in big_names})
    big_out = [{}, {}, {}, {}]
    for n in big_names:
        big_out[0][n] = g_big[n][None]
        for kind, a in enumerate(_adamw(n, wts[n][0], g_big[n], ms[n][0], vs[n][0])):
            big_out[kind + 1][n] = a[None]

    sm_out = _small_step(_pack_small(gs), _pack_small({n: wts[n] for n in SMALL}), _pack_small({n: ms[n] for n in SMALL}),
                         _pack_small({n: vs[n] for n in SMALL}))
    small_out = [_unpack_small(a, {n: shapes[n] for n in SMALL}) for a in sm_out]

    outs = [loss, grad_x[None]]
    for kind in range(4):
        for n in WEIGHTS:
            outs.append(small_out[kind][n] if n in SMALL else big_out[kind][n])
    return tuple(outs)
```

```python
import math

import jax
import jax.numpy as jnp
from jax import lax
from jax.experimental import pallas as pl
from jax.experimental.pallas import tpu as pltpu

F32 = jnp.float32
BF16 = jnp.bfloat16
MESH = pl.DeviceIdType.MESH

D_MODEL = 1024
GRID_W = 64
ATTN_HEAD_DIM = 64
ATTN_HEADS = 8
ATTN_KV_HEADS = 2
ATTN_GROUP = ATTN_HEADS // ATTN_KV_HEADS
RET_HEAD_DIM = 128
RET_HEADS = 4
ATTN_Q_W = 512
ATTN_KV_W = 128
RET_W = 512
IN_W = 4864
D_FF = 4096
RET_CHUNK = 128
ROPE_THETA = 10000.0
NORM_EPS = 1e-6
GN_EPS = 1e-5
ATTN_SCALE = ATTN_HEAD_DIM ** -0.5
LOG2E = math.log2(math.e)
Q_FOLD = ATTN_SCALE * LOG2E
RET_SCALE = RET_HEAD_DIM ** -0.5

C_AQ, C_AK, C_AV, C_RQ, C_RK, C_RV, C_RG, C_GA, C_GR = 0, 512, 640, 768, 1280, 1792, 2304, 2816, 3840

ADAM_LR = 0.001
ADAM_B1 = 0.9
ADAM_B2 = 0.999
ADAM_EPS = 1e-08
ADAM_WD = 0.01
ADAM_STEP = 10

LANES = 128
VMEM_LIMIT = 56 << 20
SEQ_TILE = 512
EPI_PIECE = 256

BIG = (("w_in", 1), ("w_attn_o", 1), ("w_ret_o", 1), ("w_out", 0), ("w_up", 1), ("w_down", 0), ("w_ple_gate", 0), ("w_ple", 1))
SMALL_ROWS = 8
SMALL = {"mix_norm": (0, 0, 1024), "mlp_norm": (1, 0, 1024), "ple_norm": (2, 0, 1024), "final_norm": (3, 0, 1024),
         "ret_norm_gain": (4, 0, 512), "attn_q_norm": (4, 512, 64), "attn_k_norm": (4, 576, 64), "ret_decay_logit": (4, 640, 8)}


def _seq_tile(s):
    return min(SEQ_TILE, s // 2)


def _cparams(sem=None, vmem=VMEM_LIMIT):
    return pltpu.CompilerParams(dimension_semantics=sem, vmem_limit_bytes=vmem)


def _mm(name, a, b, *, ta=False, tb=False, tm, tn, tk, out_dtypes=(F32,), epi=None, epi_ins=(), consts=(), n_sums=0, j_outer=False):
    if ta:
        kdim, m = a.shape
    else:
        m, kdim = a.shape
    n = b.shape[0] if tb else b.shape[1]
    tm, tn, tk = min(tm, m), min(tn, n), min(tk, kdim)
    assert m % tm == 0 and n % tn == 0 and kdim % tk == 0, (name, m, n, kdim, tm, tn, tk)
    nk = kdim // tk
    e_arrs, e_cols = [], []
    for item in epi_ins:
        if isinstance(item, tuple):
            arr, start = item
            assert tn == n and start % EPI_PIECE == 0 and n % EPI_PIECE == 0
            for piece in range(n // EPI_PIECE):
                e_arrs.append(arr)
                e_cols.append(start // EPI_PIECE + piece)
        else:
            e_arrs.append(item)
            e_cols.append(None)
    n_e, n_c, n_o = len(e_arrs), len(consts), len(out_dtypes)
    assert n_sums == 0 or tn == n

    def body(*refs):
        a_ref, b_ref = refs[0], refs[1]
        e_refs = refs[2:2 + n_e]
        c_refs = refs[2 + n_e:2 + n_e + n_c]
        o_refs = refs[2 + n_e + n_c:2 + n_e + n_c + n_o]
        s_refs = refs[2 + n_e + n_c + n_o:2 + n_e + n_c + n_o + n_sums]
        acc_ref = refs[2 + n_e + n_c + n_o + n_sums] if nk > 1 else None
        k = pl.program_id(2)
        if n_sums:
            @pl.when((pl.program_id(1 if j_outer else 0) == 0) & (k == 0))
            def _():
                for r in s_refs:
                    r[...] = jnp.zeros(r.shape, F32)
        av = a_ref[...].astype(BF16)
        bv = b_ref[...].astype(BF16)
        dims = (((0,) if ta else (1,), (1,) if tb else (0,)), ((), ()))
        part = lax.dot_general(av, bv, dims, preferred_element_type=F32)

        def finish(acc):
            vals = epi(acc, e_refs, c_refs) if epi is not None else (acc,)
            for o_ref, v in zip(o_refs, vals[:n_o]):
                o_ref[...] = v.astype(o_ref.dtype)
            for s_ref, v in zip(s_refs, vals[n_o:]):
                _acc_add(s_ref, v)

        if nk == 1:
            finish(part)
        else:
            @pl.when(k == 0)
            def _():
                acc_ref[...] = part

            @pl.when(k > 0)
            def _():
                acc_ref[...] += part

            @pl.when(k == nk - 1)
            def _():
                finish(acc_ref[...])

    def spec(shape, index):
        return pl.BlockSpec(shape, (lambda j, i, k: index(i, j, k)) if j_outer else index)

    a_spec = spec((tk, tm), lambda i, j, k: (k, i)) if ta else spec((tm, tk), lambda i, j, k: (i, k))
    b_spec = spec((tn, tk), lambda i, j, k: (j, k)) if tb else spec((tk, tn), lambda i, j, k: (k, j))
    o_spec = spec((tm, tn), lambda i, j, k: (i, j))
    c_specs = [spec(c.shape, lambda i, j, k, nd=c.ndim: (0,) * nd) for c in consts]
    outs = pl.pallas_call(
        body, name=name,
        grid=(n // tn, m // tm, nk) if j_outer else (m // tm, n // tn, nk),
        in_specs=([a_spec, b_spec]
                  + [o_spec if cb is None else spec((tm, EPI_PIECE), lambda i, j, k, cb=cb: (i, cb)) for cb in e_cols] + c_specs),
        out_specs=[o_spec] * n_o + [spec((8, n), lambda i, j, k: (0, 0))] * n_sums,
        out_shape=[jax.ShapeDtypeStruct((m, n), dt) for dt in out_dtypes] + [jax.ShapeDtypeStruct((8, n), F32)] * n_sums,
        scratch_shapes=[pltpu.VMEM((tm, tn), F32)] if nk > 1 else [],
        compiler_params=_cparams(("arbitrary",) * 3 if n_sums else ("parallel", "parallel", "arbitrary")),
    )(a, b, *e_arrs, *consts)
    return outs[0] if n_o + n_sums == 1 else outs


def _rows(arr, tr):
    return (arr, pl.BlockSpec((tr, arr.shape[1]), lambda i: (i, 0)))


def _win(arr, tr, start, width):
    bw = math.gcd(start, width) if start else width
    assert bw % LANES == 0
    return [(arr, pl.BlockSpec((tr, bw), lambda i, cb=start // bw + p: (i, cb))) for p in range(width // bw)]


def _ct(arr):
    return (arr, pl.BlockSpec((None,) + arr.shape[1:], lambda i: (i, 0, 0)))


def _whole(arr):
    return (arr, pl.BlockSpec(arr.shape, lambda i, nd=arr.ndim: (0,) * nd))


def _cat(refs):
    vals = [r[...].astype(F32) for r in refs]
    return vals[0] if len(vals) == 1 else jnp.concatenate(vals, axis=1)


def _seqtiled(name, fn, n_tiles, ins, outs, acc_widths=()):
    n_i, n_o, n_a = len(ins), len(outs), len(acc_widths)

    def body(*refs):
        i_refs, o_refs, a_refs = refs[:n_i], refs[n_i:n_i + n_o], refs[n_i + n_o:]
        if n_a:
            @pl.when(pl.program_id(0) == 0)
            def _():
                for r in a_refs:
                    r[...] = jnp.zeros(r.shape, F32)
        fn(list(i_refs), list(o_refs), list(a_refs))

    res = pl.pallas_call(
        body, name=name, grid=(n_tiles,),
        in_specs=[s for _, s in ins],
        out_specs=[s for _, _, s in outs] + [pl.BlockSpec((8, w), lambda i: (0, 0)) for w in acc_widths],
        out_shape=[jax.ShapeDtypeStruct(sh, dt) for sh, dt, _ in outs] + [jax.ShapeDtypeStruct((8, w), F32) for w in acc_widths],
        compiler_params=_cparams(("arbitrary",)),
    )(*[a for a, _ in ins])
    return res


def _acc_add(acc_ref, val):
    acc_ref[0:1, :] += jnp.sum(val, axis=0, keepdims=True)


def _out_rows(s, w, dt, tr):
    return ((s, w), dt, pl.BlockSpec((tr, w), lambda i: (i, 0)))


def _out_ct(s, w, dt, t):
    return ((s // t, w, t), dt, pl.BlockSpec((None, w, t), lambda i: (i, 0, 0)))


def _rms_fwd(x, gain):
    r = lax.rsqrt(jnp.mean(x * x, axis=-1, keepdims=True) + NORM_EPS)
    return x * r * gain


def _rms_bwd(dy, x, gain):
    r = lax.rsqrt(jnp.mean(x * x, axis=-1, keepdims=True) + NORM_EPS)
    xn = x * r
    dyg = dy * gain
    dx = r * (dyg - xn * jnp.mean(dyg * xn, axis=-1, keepdims=True))
    return dx, dy * xn


def _seg_mean(y, hd):
    w = y.shape[1]
    pieces = []
    for s in range(0, w, LANES):
        v = y[:, s:s + LANES]
        tot = jnp.sum(v, axis=1, keepdims=True)
        if hd == LANES:
            pieces.append(jnp.broadcast_to(tot, v.shape))
        else:
            low = lax.broadcasted_iota(jnp.int32, v.shape, 1) < hd
            lo = jnp.sum(jnp.where(low, v, 0.0), axis=1, keepdims=True)
            pieces.append(jnp.where(low, lo, tot - lo))
    out = pieces[0] if len(pieces) == 1 else jnp.concatenate(pieces, axis=1)
    return out * (1.0 / hd)


def _tile_lanes(t, w):
    return t if w == t.shape[1] else jnp.concatenate([t] * (w // t.shape[1]), axis=1)


def _swap_halves(x, hd):
    w = x.shape[1]
    half = hd // 2
    lane = lax.broadcasted_iota(jnp.int32, x.shape, 1)
    return jnp.where((lane % hd) < half, pltpu.roll(x, w - half, 1), pltpu.roll(x, half, 1))


def _rope(x, cos, sin_signed, hd):
    w = x.shape[1]
    return x * _tile_lanes(cos, w) + _swap_halves(x, hd) * _tile_lanes(sin_signed, w)


def _rope_t(dy, cos, sin_signed, hd):
    w = dy.shape[1]
    return dy * _tile_lanes(cos, w) + _swap_halves(dy * _tile_lanes(sin_signed, w), hd)


def _headnorm_fwd(x, gain_w, hd):
    r = lax.rsqrt(_seg_mean(x * x, hd) + NORM_EPS)
    return x * r * gain_w


def _headnorm_bwd(dy, x, gain_w, hd):
    r = lax.rsqrt(_seg_mean(x * x, hd) + NORM_EPS)
    xn = x * r
    dyg = dy * gain_w
    return r * (dyg - xn * _seg_mean(dyg * xn, hd)), dy * xn


def _sigmoid(x):
    return 1.0 / (1.0 + jnp.exp(-x))


def _rope_tables(seq_len, head_dim):
    rows = seq_len // GRID_W
    n_axis = head_dim // 4
    freqs = ROPE_THETA ** (-jnp.arange(n_axis, dtype=F32) / n_axis)
    ang_r = jnp.arange(rows, dtype=F32)[:, None] * freqs
    ang_c = jnp.arange(GRID_W, dtype=F32)[:, None] * freqs

    def expand(by_row, by_col):
        r = jnp.broadcast_to(by_row[:, None, :], (rows, GRID_W, n_axis))
        c = jnp.broadcast_to(by_col[None, :, :], (rows, GRID_W, n_axis))
        return jnp.concatenate([r, c], axis=-1).reshape(seq_len, 2 * n_axis)

    cos, sin = expand(jnp.cos(ang_r), jnp.cos(ang_c)), expand(jnp.sin(ang_r), jnp.sin(ang_c))
    reps = LANES // head_dim
    return jnp.tile(jnp.concatenate([cos, cos], axis=-1), (1, reps)), jnp.tile(jnp.concatenate([-sin, sin], axis=-1), (1, reps))


def _stage_norm_in(x, gain):
    s = x.shape[0]
    tr = min(SEQ_TILE, s)

    def fn(i, o, a):
        o[0][...] = _rms_fwd(i[0][...], i[1][...]).astype(BF16)

    return _seqtiled("norm_in", fn, s // tr, [_rows(x, tr), _whole(gain)], [_out_rows(s, D_MODEL, BF16, tr)])[0]


def _stage_qkv(proj, tabs, gq_w, gk_w):
    s = proj.shape[0]
    t = _seq_tile(s)
    ca, sa, cr, sr = tabs
    ins = (_win(proj, t, C_AQ, ATTN_Q_W) + _win(proj, t, C_AK, ATTN_KV_W) + _win(proj, t, C_AV, ATTN_KV_W)
           + _win(proj, t, C_RQ, RET_W) + _win(proj, t, C_RK, RET_W)
           + [_rows(ca, t), _rows(sa, t), _rows(cr, t), _rows(sr, t), _whole(gq_w), _whole(gk_w)])

    def fn(i, o, a):
        aq, ak, av = (i[n][...].astype(F32) for n in range(3))
        rq, rk = _cat(i[3:5]), _cat(i[5:7])
        ca_, sa_, cr_, sr_ = i[7][...], i[8][...], i[9][...], i[10][...]
        qr = _rope(_headnorm_fwd(aq, i[11][...], ATTN_HEAD_DIM), ca_, sa_, ATTN_HEAD_DIM) * Q_FOLD
        kr = _rope(_headnorm_fwd(ak, i[12][...], ATTN_HEAD_DIM), ca_, sa_, ATTN_HEAD_DIM)
        qt = qr.T.astype(BF16)
        zeros = jnp.zeros((ATTN_HEAD_DIM, t), BF16)
        for h in range(ATTN_HEADS):
            g = h // ATTN_GROUP
            blk = qt[h * ATTN_HEAD_DIM:(h + 1) * ATTN_HEAD_DIM, :]
            o[0][h * LANES + g * ATTN_HEAD_DIM:h * LANES + (g + 1) * ATTN_HEAD_DIM, :] = blk
            o[0][h * LANES + (1 - g) * ATTN_HEAD_DIM:h * LANES + (2 - g) * ATTN_HEAD_DIM, :] = zeros
        o[1][...] = kr.astype(BF16)
        o[2][...] = kr.T.astype(BF16)
        o[3][...] = av.astype(BF16)
        o[4][...] = av.T.astype(BF16)
        o[5][...] = _rope(rq, cr_, sr_, RET_HEAD_DIM) * RET_SCALE
        o[6][...] = _rope(rk, cr_, sr_, RET_HEAD_DIM)

    outs = [_out_ct(s, ATTN_HEADS * LANES, BF16, t), _out_rows(s, ATTN_KV_W, BF16, t), _out_ct(s, ATTN_KV_W, BF16, t),
            _out_rows(s, ATTN_KV_W, BF16, t), _out_ct(s, ATTN_KV_W, BF16, t), _out_rows(s, RET_W, F32, t), _out_rows(s, RET_W, F32, t)]
    return _seqtiled("qkv_prep", fn, s // t, ins, outs)


def _groupnorm_gate(ry, rg, gain):
    mu = _seg_mean(ry, RET_HEAD_DIM)
    d = ry - mu
    rs = lax.rsqrt(_seg_mean(d * d, RET_HEAD_DIM) + GN_EPS)
    return d * rs, rs, _sigmoid(rg)


def _stage_mix_post(ry_f, ry_b, proj, o_ct, gain):
    s = proj.shape[0]
    t = _seq_tile(s)
    ins = [_rows(ry_f, t), _rows(ry_b, t)] + _win(proj, t, C_RG, RET_W) + [_ct(o_ct), _whole(gain)]

    def fn(i, o, a):
        ry = i[0][...] + i[1][...]
        rg = _cat(i[2:4])
        gn, _, sg = _groupnorm_gate(ry, rg, None)
        o[0][...] = (gn * i[5][...] * (rg * sg)).astype(BF16)
        o[1][...] = i[4][...].astype(F32).T.astype(BF16)

    return _seqtiled("mix_post", fn, s // t, ins, [_out_rows(s, RET_W, BF16, t), _out_rows(s, ATTN_Q_W, BF16, t)])


def _stage_mix_post_bwd(dattn, attn_rows, drz, ry_f, ry_b, proj, gain):
    s = proj.shape[0]
    t = _seq_tile(s)
    ins = ([_rows(dattn, t), _rows(attn_rows, t), _rows(drz, t), _rows(ry_f, t), _rows(ry_b, t)]
           + _win(proj, t, C_RG, RET_W) + [_whole(gain)])

    def fn(i, o, a):
        da = i[0][...]
        dat = da.T
        prod_t = (da * i[1][...].astype(F32)).T
        dat_b = dat.astype(BF16)
        zeros = jnp.zeros((ATTN_HEAD_DIM, t), BF16)
        for h in range(ATTN_HEADS):
            g = h // ATTN_GROUP
            o[0][h * LANES + g * ATTN_HEAD_DIM:h * LANES + (g + 1) * ATTN_HEAD_DIM, :] = dat_b[h * ATTN_HEAD_DIM:(h + 1) * ATTN_HEAD_DIM, :]
            o[0][h * LANES + (1 - g) * ATTN_HEAD_DIM:h * LANES + (2 - g) * ATTN_HEAD_DIM, :] = zeros
            o[1][h] = jnp.sum(prod_t[h * ATTN_HEAD_DIM:(h + 1) * ATTN_HEAD_DIM, :], axis=0, keepdims=True)
        ry = i[3][...] + i[4][...]
        rg = _cat(i[5:7])
        gain_ = i[7][...]
        gn, rs, sg = _groupnorm_gate(ry, rg, None)
        dz = i[2][...]
        silu = rg * sg
        _acc_add(a[0], dz * gn * silu)
        dgn = dz * gain_ * silu
        o[2][...] = rs * (dgn - _seg_mean(dgn, RET_HEAD_DIM) - gn * _seg_mean(dgn * gn, RET_HEAD_DIM))
        o[3][...] = (dz * gn * gain_ * (sg * (1.0 + rg * (1.0 - sg)))).astype(BF16)

    outs = [_out_ct(s, ATTN_HEADS * LANES, BF16, t),
            ((ATTN_HEADS, s // t, 1, t), F32, pl.BlockSpec((ATTN_HEADS, None, 1, t), lambda i: (0, i, 0, 0))),
            _out_rows(s, RET_W, F32, t), _out_rows(s, RET_W, BF16, t)]
    return _seqtiled("mix_post_bwd", fn, s // t, ins, outs, acc_widths=(RET_W,))


def _stage_dproj(proj, dq_ct, dk8, dv8, rgrads, drg, dga, dgr, tabs, gq_w, gk_w):
    s = proj.shape[0]
    t = _seq_tile(s)
    ca, sa, cr, sr = tabs
    kv8 = pl.BlockSpec((ATTN_HEADS, t, ATTN_KV_W), lambda i: (0, i, 0))
    ins = (_win(proj, t, C_AQ, ATTN_Q_W) + _win(proj, t, C_AK, ATTN_KV_W) + [_ct(dq_ct), (dk8, kv8), (dv8, kv8)]
           + [_rows(g, t) for g in rgrads] + [_rows(drg, t), _rows(dga, t), _rows(dgr, t)]
           + [_rows(ca, t), _rows(sa, t), _rows(cr, t), _rows(sr, t), _whole(gq_w), _whole(gk_w)])

    def fn(i, o, a):
        aq, ak = i[0][...].astype(F32), i[1][...].astype(F32)
        dq_f, dk_f, dv_f, dq_b, dk_b, dv_b = (r[...].astype(F32) for r in i[5:11])
        ca_, sa_, cr_, sr_ = i[14][...], i[15][...], i[16][...], i[17][...]
        dqn = _rope_t(i[2][...].T * ATTN_SCALE, ca_, sa_, ATTN_HEAD_DIM)
        daq, gq_rows = _headnorm_bwd(dqn, aq, i[18][...], ATTN_HEAD_DIM)
        dkn = _rope_t(jnp.sum(i[3][...].astype(F32), axis=0) * (1.0 / LOG2E), ca_, sa_, ATTN_HEAD_DIM)
        dak, gk_rows = _headnorm_bwd(dkn, ak, i[19][...], ATTN_HEAD_DIM)
        _acc_add(a[0], gq_rows)
        _acc_add(a[1], gk_rows)
        out = o[0]
        out[:, C_AQ:C_AQ + ATTN_Q_W] = daq.astype(BF16)
        out[:, C_AK:C_AK + ATTN_KV_W] = dak.astype(BF16)
        out[:, C_AV:C_AV + ATTN_KV_W] = jnp.sum(i[4][...].astype(F32), axis=0).astype(BF16)
        out[:, C_RQ:C_RQ + RET_W] = _rope_t((dq_f + dq_b) * RET_SCALE, cr_, sr_, RET_HEAD_DIM).astype(BF16)
        out[:, C_RK:C_RK + RET_W] = _rope_t(dk_f + dk_b, cr_, sr_, RET_HEAD_DIM).astype(BF16)
        out[:, C_RV:C_RV + RET_W] = (dv_f + dv_b).astype(BF16)
        out[:, C_RG:C_RG + RET_W] = i[11][...]
        out[:, C_GA:C_GA + D_MODEL] = i[12][...]
        out[:, C_GR:C_GR + D_MODEL] = i[13][...]

    return _seqtiled("dproj", fn, s // t, ins, [_out_rows(s, IN_W, BF16, t)], acc_widths=(ATTN_Q_W, ATTN_KV_W))


def _attn_fwd(q_ct, k_rows, v_ct):
    nq, _, t = q_ct.shape
    s = nq * t
    nk = nq
    assert nk % 2 == 0
    n_ch = next(n for n in (8, 4, 2) if nq % n == 0)
    halves = 2 if t % (2 * LANES) == 0 else 1
    tq = t // halves
    n_par = n_ch * halves

    def body(q_ref, k_ref, v_ref, o_ref, lse_ref, *bufs):
        sbuf = tuple(bufs[2 * w:2 * w + 2] for w in range(n_par))
        pbuf = tuple(bufs[2 * n_par + 2 * w:2 * n_par + 2 * w + 2] for w in range(n_par))

        def where(w):
            return w // halves, slice((w % halves) * tq, (w % halves + 1) * tq)

        def scores(w, j, slot):
            kj = k_ref[pl.ds(pl.multiple_of(j * t, t), t), :]
            cw, lanes = where(w)
            st = jnp.dot(kj, q_ref[cw, :, lanes], preferred_element_type=F32)
            sbuf[w][slot][...] = st
            return jnp.max(st, axis=0, keepdims=True)

        def probs(w, slot, cmax, m, l):
            m_new = jnp.maximum(m, cmax)
            alpha = jnp.exp2(m - m_new)
            pt = jnp.exp2(sbuf[w][slot][...] - m_new)
            pbuf[w][slot][...] = pt.astype(BF16)
            return m_new, alpha * l + jnp.sum(pt, axis=0, keepdims=True), alpha

        def values(w, j, slot, alpha, acc):
            return alpha * acc + jnp.dot(v_ref[j], pbuf[w][slot][...], preferred_element_type=F32)

        init = []
        for w in range(n_par):
            m = jnp.full((1, tq), -1e30, F32)
            l = jnp.zeros((1, tq), F32)
            cmax0 = scores(w, 0, 0)
            cmax1 = scores(w, 1, 1)
            m, l, alpha0 = probs(w, 0, cmax0, m, l)
            init.append((m, l, jnp.zeros((ATTN_HEAD_DIM, tq), F32), cmax1, alpha0))

        def trip(n, carry):
            c = 2 * n
            out = []
            for w in range(n_par):
                m, l, acc, cmax_b, alpha_c = carry[w]
                acc = values(w, c, 0, alpha_c, acc)
                m, l, alpha1 = probs(w, 1, cmax_b, m, l)
                cmax2 = scores(w, c + 2, 0)
                acc = values(w, c + 1, 1, alpha1, acc)
                m, l, alpha2 = probs(w, 0, cmax2, m, l)
                cmax3 = scores(w, c + 3, 1)
                out.append((m, l, acc, cmax3, alpha2))
            return tuple(out)

        res = lax.fori_loop(0, nk // 2 - 1, trip, tuple(init))
        for w in range(n_par):
            m, l, acc, cmax_b, alpha_c = res[w]
            acc = values(w, nk - 2, 0, alpha_c, acc)
            m, l, alpha1 = probs(w, 1, cmax_b, m, l)
            acc = values(w, nk - 1, 1, alpha1, acc)
            cw, lanes = where(w)
            o_ref[cw, :, lanes] = (acc / l).astype(BF16)
            lse_ref[cw, :, lanes] = m + jnp.log2(l)

    return pl.pallas_call(
        body, name="attn_fwd", grid=(ATTN_HEADS, nq // n_ch),
        in_specs=[pl.BlockSpec((n_ch, LANES, t), lambda h, i: (i, h, 0)),
                  pl.BlockSpec((s, ATTN_KV_W), lambda h, i: (0, 0)),
                  pl.BlockSpec((nk, ATTN_HEAD_DIM, t), lambda h, i: (0, h // ATTN_GROUP, 0))],
        out_specs=[pl.BlockSpec((n_ch, ATTN_HEAD_DIM, t), lambda h, i: (i, h, 0)),
                   pl.BlockSpec((None, n_ch, 1, t), lambda h, i: (h, i, 0, 0))],
        out_shape=[jax.ShapeDtypeStruct((nq, ATTN_Q_W, t), BF16), jax.ShapeDtypeStruct((ATTN_HEADS, nq, 1, t), F32)],
        scratch_shapes=[pltpu.VMEM((t, tq), F32)] * (2 * n_par) + [pltpu.VMEM((t, tq), BF16)] * (2 * n_par),
        compiler_params=_cparams(("parallel", "parallel")),
    )(q_ct, k_rows, v_ct)


def _attn_bwd(q_ct, do_ct, lse, delta, k_rows, v_rows, k_ct):
    nq, _, t = q_ct.shape
    s = nq * t
    kc = 4 if nq % 4 == 0 else 2
    tk = kc * t
    nk = nq // kc
    assert nq % 2 == 0 and nq % kc == 0

    def body(q_ref, do_ref, lse_ref, delta_ref, k_ref, v_ref, kt_ref, dq_ref, dk_ref, dv_ref, dk_acc, dv_acc,
             sb0, sb1, db0, db1, pb0, pb1, gb0, gb1):
        j = pl.program_id(1)
        sb, db, pb, gb = (sb0, sb1), (db0, db1), (pb0, pb1), (gb0, gb1)

        @pl.when(j == 0)
        def _():
            dq_ref[...] = jnp.zeros(dq_ref.shape, F32)

        kj, vj = k_ref[...], v_ref[...]
        ktj = jnp.concatenate([kt_ref[u] for u in range(kc)], axis=1)
        dk_acc[...] = jnp.zeros(dk_acc.shape, F32)
        dv_acc[...] = jnp.zeros(dv_acc.shape, F32)

        def products(i, slot):
            sb[slot][...] = jnp.dot(kj, q_ref[i], preferred_element_type=F32)
            db[slot][...] = jnp.dot(vj, do_ref[i], preferred_element_type=F32)

        def cotangents(i, slot):
            pt = jnp.exp2(sb[slot][...] - lse_ref[i])
            pb[slot][...] = pt.astype(BF16)
            gb[slot][...] = (pt * (db[slot][...] - delta_ref[i])).astype(BF16)

        def accumulate(i, slot):
            dst = gb[slot][...]
            dv_acc[...] += _nt(pb[slot][...], do_ref[i])
            dk_acc[...] += _nt(dst, q_ref[i])
            dq_ref[i] += jnp.dot(ktj, dst, preferred_element_type=F32)

        products(0, 0)
        products(1, 1)
        cotangents(0, 0)

        def trip(n, carry):
            c = 2 * n
            accumulate(c, 0)
            cotangents(c + 1, 1)
            products(c + 2, 0)
            accumulate(c + 1, 1)
            cotangents(c + 2, 0)
            products(c + 3, 1)
            return carry

        lax.fori_loop(0, nq // 2 - 1, trip, 0)
        accumulate(nq - 2, 0)
        cotangents(nq - 1, 1)
        accumulate(nq - 1, 1)
        dk_ref[...] = dk_acc[...].astype(dk_ref.dtype)
        dv_ref[...] = dv_acc[...].astype(dv_ref.dtype)

    per_head = pl.BlockSpec((nq, LANES, t), lambda h, j: (0, h, 0))
    stat = pl.BlockSpec((None, nq, 1, t), lambda h, j: (h, 0, 0, 0))
    kv_rows = pl.BlockSpec((tk, ATTN_KV_W), lambda h, j: (j, 0))
    kv_out = pl.BlockSpec((None, tk, ATTN_KV_W), lambda h, j: (h, j, 0))
    return pl.pallas_call(
        body, name="attn_bwd", grid=(ATTN_HEADS, nk),
        in_specs=[per_head, per_head, stat, stat, kv_rows, kv_rows,
                  pl.BlockSpec((kc, ATTN_HEAD_DIM, t), lambda h, j: (j, h // ATTN_GROUP, 0))],
        out_specs=[pl.BlockSpec((nq, ATTN_HEAD_DIM, t), lambda h, j: (0, h, 0)), kv_out, kv_out],
        out_shape=[jax.ShapeDtypeStruct((nq, ATTN_Q_W, t), F32), jax.ShapeDtypeStruct((ATTN_HEADS, s, ATTN_KV_W), BF16),
                   jax.ShapeDtypeStruct((ATTN_HEADS, s, ATTN_KV_W), BF16)],
        scratch_shapes=([pltpu.VMEM((tk, ATTN_KV_W), F32)] * 2 + [pltpu.VMEM((tk, t), F32)] * 4 + [pltpu.VMEM((tk, t), BF16)] * 4),
        compiler_params=_cparams(("parallel", "arbitrary")),
    )(q_ct, do_ct, lse, delta, k_rows, v_rows, k_ct)


def _log_sigmoid(x):
    t = jnp.exp(-jnp.abs(x))
    log1p_t = jnp.where(t < 1e-2, t * (1.0 - t * (0.5 - t * (1.0 / 3.0))), jnp.log(1.0 + t))
    return jnp.minimum(x, 0.0) - log1p_t


def _decay_tables(logit, backward):
    c = RET_CHUNK
    lam = _log_sigmoid(jnp.full((c, c), logit, F32))
    ii = lax.broadcasted_iota(jnp.int32, (c, c), 0).astype(F32)
    jj = lax.broadcasted_iota(jnp.int32, (c, c), 1).astype(F32)
    if not backward:
        dist, dist_t = jnp.maximum(ii - jj, 0.0), jnp.maximum(jj - ii, 0.0)
        mask, mask_t = ii >= jj, jj >= ii
        e_q, e_k = ii + 1.0, (c - 1.0) - ii
    else:
        dist, dist_t = jnp.maximum(jj - ii, 0.0), jnp.maximum(ii - jj, 0.0)
        mask, mask_t = jj > ii, ii > jj
        e_q, e_k = c - ii, ii
    return dict(
        d=jnp.where(mask, jnp.exp(lam * dist), 0.0), d_t=jnp.where(mask_t, jnp.exp(lam * dist_t), 0.0), dist=dist,
        qdec=jnp.exp(lam * e_q), kdec=jnp.exp(lam * e_k), e_q=e_q, e_k=e_k, gam=jnp.exp(lam * c))


def _nt(a, b):
    return lax.dot_general(a, b, (((1,), (1,)), ((), ())), preferred_element_type=F32)


def _ret_sub(n_chunks):
    return 4 if n_chunks % 4 == 0 else 2


def _ret_fwd(logits, q, k, proj):
    s = q.shape[0]
    c = RET_CHUNK
    sub = _ret_sub(s // c)
    nb = s // (c * sub)
    block = (lambda n: n, lambda n: nb - 1 - n)
    order = (tuple(range(sub)), tuple(reversed(range(sub))))
    vwin = _win(proj, c * sub, C_RV, RET_W)
    nv = len(vwin)
    vw = RET_W // nv
    per = 2 + nv

    def body(lg_ref, *refs):
        ins, outs, states = refs[:2 * per], refs[2 * per:2 * per + 4], refs[2 * per + 4:]

        @pl.when(pl.program_id(0) == 0)
        def _():
            for st in states:
                st[...] = jnp.zeros(st.shape, F32)

        for h in range(RET_HEADS):
            for d in range(2):
                q_ref, k_ref, v_refs = ins[d * per], ins[d * per + 1], ins[d * per + 2:(d + 1) * per]
                y_ref, st_ref, state = outs[2 * d], outs[2 * d + 1], states[d]
                tb = _decay_tables(lg_ref[d, h], bool(d))
                sl = slice(h * RET_HEAD_DIM, (h + 1) * RET_HEAD_DIM)
                off = h * RET_HEAD_DIM
                sh = state[h]
                for u in order[d]:
                    rows = slice(u * c, (u + 1) * c)
                    qh, kh = q_ref[rows, sl], k_ref[rows, sl]
                    vb = v_refs[off // vw][rows, off % vw:off % vw + RET_HEAD_DIM].astype(BF16)
                    a = _nt(qh.astype(BF16), kh.astype(BF16)) * tb["d"]
                    st_ref[u, h] = sh
                    y_ref[rows, sl] = (jnp.dot(a.astype(BF16), vb, preferred_element_type=F32)
                                       + jnp.dot((qh * tb["qdec"]).astype(BF16), sh.astype(BF16), preferred_element_type=F32))
                    sh = tb["gam"] * sh + jnp.dot((kh * tb["kdec"]).T.astype(BF16), vb, preferred_element_type=F32)
                state[h] = sh

    hmat = (RET_HEADS, RET_HEAD_DIM, RET_HEAD_DIM)
    in_specs, out_specs, args = [pl.BlockSpec(memory_space=pltpu.SMEM)], [], [logits]
    for d in range(2):
        rows = pl.BlockSpec((c * sub, RET_W), lambda n, d=d: (block[d](n), 0))
        in_specs += [rows, rows] + [pl.BlockSpec(sp.block_shape, lambda n, d=d, cb=sp.index_map(0)[1]: (block[d](n), cb)) for _, sp in vwin]
        args += [q, k] + [a for a, _ in vwin]
        out_specs += [rows, pl.BlockSpec((sub,) + hmat, lambda n, d=d: (block[d](n), 0, 0, 0))]
    return pl.pallas_call(
        body, name="ret_fwd", grid=(nb,), in_specs=in_specs, out_specs=out_specs,
        out_shape=[jax.ShapeDtypeStruct((s, RET_W), F32), jax.ShapeDtypeStruct((nb * sub,) + hmat, F32)] * 2,
        scratch_shapes=[pltpu.VMEM(hmat, F32)] * 2,
        compiler_params=_cparams(("arbitrary",)),
    )(*args)


def _ret_bwd(logits, q, k, proj, dy, st_f, st_b):
    s = q.shape[0]
    c = RET_CHUNK
    sub = _ret_sub(s // c)
    nb = s // (c * sub)
    block = (lambda n: nb - 1 - n, lambda n: n)
    order = (tuple(reversed(range(sub))), tuple(range(sub)))
    vwin = _win(proj, c * sub, C_RV, RET_W)
    nv = len(vwin)
    vw = RET_W // nv
    per = 4 + nv

    def body(lg_ref, *refs):
        ins, outs, scr = refs[:2 * per], refs[2 * per:2 * per + 8], refs[2 * per + 8:]
        n = pl.program_id(0)

        @pl.when(n == 0)
        def _():
            for r in scr:
                r[...] = jnp.zeros(r.shape, F32)

        for h in range(RET_HEADS):
            for d in range(2):
                q_ref, k_ref, dy_ref, st_ref = ins[d * per:d * per + 4]
                v_refs = ins[d * per + 4:(d + 1) * per]
                dq_ref, dk_ref, dv_ref = outs[4 * d:4 * d + 3]
                dstate, lacc = scr[2 * d], scr[2 * d + 1]
                tb = _decay_tables(lg_ref[d, h], bool(d))
                sl = slice(h * RET_HEAD_DIM, (h + 1) * RET_HEAD_DIM)
                off = h * RET_HEAD_DIM
                dsh = dstate[h]
                lsum = lacc[h]
                for u in order[d]:
                    rows = slice(u * c, (u + 1) * c)
                    qh, kh, dyh = q_ref[rows, sl], k_ref[rows, sl], dy_ref[rows, sl]
                    vb = v_refs[off // vw][rows, off % vw:off % vw + RET_HEAD_DIM].astype(BF16)
                    qb, kb, dyb = qh.astype(BF16), kh.astype(BF16), dyh.astype(BF16)
                    sh = st_ref[u, h]
                    shb, dshb = sh.astype(BF16), dsh.astype(BF16)
                    qk = _nt(qb, kb)
                    g = _nt(dyb, vb) * tb["d"]
                    a_t = _nt(kb, qb) * tb["d_t"]
                    g_t = _nt(vb, dyb) * tb["d_t"]
                    qd, kd = qh * tb["qdec"], kh * tb["kdec"]
                    dqd = _nt(dyb, shb)
                    dkd = _nt(vb, dshb)
                    dq_ref[rows, sl] = (jnp.dot(g.astype(BF16), kb, preferred_element_type=F32) + dqd * tb["qdec"]).astype(dq_ref.dtype)
                    dk_ref[rows, sl] = (jnp.dot(g_t.astype(BF16), qb, preferred_element_type=F32) + dkd * tb["kdec"]).astype(dk_ref.dtype)
                    dv_ref[rows, sl] = (jnp.dot(a_t.astype(BF16), dyb, preferred_element_type=F32)
                                        + jnp.dot(kd.astype(BF16), dshb, preferred_element_type=F32)).astype(dv_ref.dtype)
                    lsum = lsum + (tb["dist"] * qk * g + tb["e_q"] * qd * dqd + tb["e_k"] * kd * dkd
                                   + float(c) * tb["gam"] * dsh * sh)
                    dsh = tb["gam"] * dsh + jnp.dot(qd.T.astype(BF16), dyb, preferred_element_type=F32)
                dstate[h] = dsh
                lacc[h] = lsum

        @pl.when(n == nb - 1)
        def _():
            for d in range(2):
                for h in range(RET_HEADS):
                    outs[4 * d + 3][h] = jnp.zeros((8, LANES), F32) + jnp.sum(scr[2 * d + 1][h])

    hmat = (RET_HEADS, RET_HEAD_DIM, RET_HEAD_DIM)
    in_specs, out_specs, args = [pl.BlockSpec(memory_space=pltpu.SMEM)], [], [logits]
    for d, states in enumerate((st_f, st_b)):
        rows = pl.BlockSpec((c * sub, RET_W), lambda n, d=d: (block[d](n), 0))
        in_specs += ([rows, rows, rows, pl.BlockSpec((sub,) + hmat, lambda n, d=d: (block[d](n), 0, 0, 0))]
                     + [pl.BlockSpec(sp.block_shape, lambda n, d=d, cb=sp.index_map(0)[1]: (block[d](n), cb)) for _, sp in vwin])
        args += [q, k, dy, states] + [a for a, _ in vwin]
        out_specs += [rows, rows, rows, pl.BlockSpec((RET_HEADS, 8, LANES), lambda n: (0, 0, 0))]
    return pl.pallas_call(
        body, name="ret_bwd", grid=(nb,), in_specs=in_specs, out_specs=out_specs,
        out_shape=([jax.ShapeDtypeStruct((s, RET_W), BF16)] * 3 + [jax.ShapeDtypeStruct((RET_HEADS, 8, LANES), F32)]) * 2,
        scratch_shapes=[pltpu.VMEM(hmat, F32)] * 4,
        compiler_params=_cparams(("arbitrary",)),
    )(*args)


def _local_step(x, p, target, w, small):
    s = x.shape[0]
    tabs = _rope_tables(s, ATTN_HEAD_DIM) + _rope_tables(s, RET_HEAD_DIM)
    g_mix, g_mlp, g_ple = small["mix_norm"][None, :], small["mlp_norm"][None, :], small["ple_norm"][None, :]
    g_final, g_ret = small["final_norm"][None, :], small["ret_norm_gain"][None, :]
    gq_w = jnp.tile(small["attn_q_norm"], ATTN_HEADS)[None, :]
    gk_w = jnp.tile(small["attn_k_norm"], ATTN_KV_HEADS)[None, :]
    logits = small["ret_decay_logit"]

    hb = _stage_norm_in(x, g_mix)
    proj = _mm("in_proj", hb, w["w_in"], tm=512, tn=IN_W // 2, tk=1024, out_dtypes=(BF16,), j_outer=True)
    q_ct, k_rows, k_ct, v_rows, v_ct, rq, rk = _stage_qkv(proj, tabs, gq_w, gk_w)
    o_ct, lse = _attn_fwd(q_ct, k_rows, v_ct)
    ry_f, st_f, ry_b, st_b = _ret_fwd(logits, rq, rk, proj)
    rz, attn_rows = _stage_mix_post(ry_f, ry_b, proj, o_ct, g_ret)
    a_out = _mm("attn_o", attn_rows, w["w_attn_o"], tm=1024, tn=1024, tk=512, out_dtypes=(BF16,))
    n_gate = D_MODEL // EPI_PIECE

    def epi_merge(acc, e, c):
        ga, gr = _cat(e[1:1 + n_gate]), _cat(e[1 + n_gate:1 + 2 * n_gate])
        return acc, _sigmoid(ga) * e[0][...] + _sigmoid(gr) * acc

    r_out, merged = _mm("ret_o", rz, w["w_ret_o"], tm=512, tn=1024, tk=512, out_dtypes=(BF16, BF16), epi=epi_merge,
                        epi_ins=(a_out, (proj, C_GA), (proj, C_GR)))

    def epi_res_norm(acc, e, c):
        xr = e[0][...] + acc
        return xr, _rms_fwd(xr, c[0][...])

    x1, hm = _mm("out_proj", merged, w["w_out"], tm=512, tn=1024, tk=1024, out_dtypes=(F32, BF16),
                 epi=epi_res_norm, epi_ins=(x,), consts=(g_mlp,))

    def epi_relu2(acc, e, c):
        r = jnp.maximum(acc, 0.0)
        return (r * r,)

    act = _mm("mlp_up", hm, w["w_up"], tm=512, tn=2048, tk=1024, out_dtypes=(BF16,), epi=epi_relu2, j_outer=True)
    x2, hp = _mm("mlp_down", act, w["w_down"], tm=512, tn=1024, tk=D_FF, out_dtypes=(F32, BF16),
                 epi=epi_res_norm, epi_ins=(x1,), consts=(g_ple,))
    pe = _mm("ple_emb", p, w["w_ple"], tm=1024, tn=1024, tk=256)

    def epi_head(acc, e, c):
        gt = _sigmoid(acc)
        pe_, gf = e[0][...], c[0][...]
        x3 = e[1][...] + gt * pe_
        r3 = lax.rsqrt(jnp.mean(x3 * x3, axis=-1, keepdims=True) + NORM_EPS)
        x3n = x3 * r3
        err = x3n * gf - e[2][...]
        dy = err * (1.0 / D_MODEL)
        dyg = dy * gf
        dx3 = r3 * (dyg - x3n * jnp.mean(dyg * x3n, axis=-1, keepdims=True))
        return dx3, dx3 * pe_ * gt * (1.0 - gt), dx3 * gt, err * err, dy * x3n

    dx3, dzg, dpe, loss_cols, g_final_p = _mm("ple_gate", hp, w["w_ple_gate"], tm=512, tn=1024, tk=1024, out_dtypes=(F32, BF16, BF16),
                                              epi=epi_head, epi_ins=(pe, x2, target), consts=(g_final,), n_sums=2)
    loss_sum = 0.5 / D_MODEL * jnp.sum(loss_cols)

    gw = {}
    gw["w_ple"] = _mm("g_w_ple", p, dpe, ta=True, tm=256, tn=1024, tk=2048)
    gw["w_ple_gate"] = _mm("g_w_ple_gate", hp, dzg, ta=True, tm=1024, tn=1024, tk=2048)
    def epi_norm_bwd(acc, e, c):
        dx, dg = _rms_bwd(acc, e[0][...], c[0][...])
        return e[1][...] + dx, dg

    def epi_norm_bwd_b(acc, e, c):
        tot, dg = epi_norm_bwd(acc, e, c)
        return tot, tot, dg

    dx2, dx2_b, g_ple_p = _mm("d_hp", dzg, w["w_ple_gate"], tb=True, tm=512, tn=1024, tk=1024, out_dtypes=(F32, BF16),
                              epi=epi_norm_bwd_b, epi_ins=(x2, dx3), consts=(g_ple,), n_sums=1)

    def epi_relu2_bwd(acc, e, c):
        return (acc * (2.0 * jnp.sqrt(e[0][...].astype(F32))),)

    du = _mm("d_u", dx2_b, w["w_down"], tb=True, tm=512, tn=2048, tk=1024, out_dtypes=(BF16,), epi=epi_relu2_bwd, epi_ins=(act,),
             j_outer=True)
    gw["w_down"] = _mm("g_w_down", act, dx2_b, ta=True, tm=1024, tn=1024, tk=2048)
    gw["w_up"] = _mm("g_w_up", hm, du, ta=True, tm=1024, tn=1024, tk=2048)
    dx1, dx1_b, g_mlp_p = _mm("d_hm", du, w["w_up"], tb=True, tm=512, tn=1024, tk=D_FF, out_dtypes=(F32, BF16),
                              epi=epi_norm_bwd_b, epi_ins=(x1, dx2), consts=(g_mlp,), n_sums=1)
    def epi_merge_bwd(acc, e, c):
        sa, sr = _sigmoid(_cat(e[2:2 + n_gate])), _sigmoid(_cat(e[2 + n_gate:2 + 2 * n_gate]))
        return acc * sa, acc * sr, acc * e[0][...] * sa * (1.0 - sa), acc * e[1][...] * sr * (1.0 - sr)

    dao, dro, dga, dgr = _mm("d_merged", dx1_b, w["w_out"], tb=True, tm=512, tn=1024, tk=1024, out_dtypes=(BF16,) * 4,
                             epi=epi_merge_bwd, epi_ins=(a_out, r_out, (proj, C_GA), (proj, C_GR)))
    gw["w_out"] = _mm("g_w_out", merged, dx1_b, ta=True, tm=1024, tn=1024, tk=2048)
    gw["w_attn_o"] = _mm("g_w_attn_o", attn_rows, dao, ta=True, tm=512, tn=1024, tk=2048)
    gw["w_ret_o"] = _mm("g_w_ret_o", rz, dro, ta=True, tm=512, tn=1024, tk=2048)
    dattn = _mm("d_attn", dao, w["w_attn_o"], tb=True, tm=1024, tn=512, tk=1024)
    drz = _mm("d_rz", dro, w["w_ret_o"], tb=True, tm=1024, tn=512, tk=1024)
    do_ct, delta, dry, drg, g_ret_p = _stage_mix_post_bwd(dattn, attn_rows, drz, ry_f, ry_b, proj, g_ret)
    dq_f, dk_f, dv_f, dl_f, dq_b, dk_b, dv_b, dl_b = _ret_bwd(logits, rq, rk, proj, dry, st_f, st_b)
    dq_ct, dk8, dv8 = _attn_bwd(q_ct, do_ct, lse, delta, k_rows, v_rows, k_ct)
    dproj, gq_p, gk_p = _stage_dproj(proj, dq_ct, dk8, dv8, (dq_f, dk_f, dv_f, dq_b, dk_b, dv_b), drg, dga, dgr, tabs, gq_w, gk_w)
    gw["w_in"] = _mm("g_w_in", hb, dproj, ta=True, tm=1024, tn=IN_W // 2, tk=1024)
    grad_x, g_mix_p = _mm("d_h", dproj, w["w_in"], tb=True, tm=512, tn=1024, tk=IN_W, epi=epi_norm_bwd, epi_ins=(x, dx1),
                          consts=(g_mix,), n_sums=1)

    gs = {
        "mix_norm": g_mix_p[0], "mlp_norm": g_mlp_p[0], "ple_norm": g_ple_p[0], "final_norm": g_final_p[0],
        "ret_norm_gain": g_ret_p[0],
        "attn_q_norm": jnp.sum(gq_p[0].reshape(ATTN_HEADS, ATTN_HEAD_DIM), axis=0),
        "attn_k_norm": jnp.sum(gk_p[0].reshape(ATTN_KV_HEADS, ATTN_HEAD_DIM), axis=0),
        "ret_decay_logit": jnp.stack([dl_f[:, 0, 0], dl_b[:, 0, 0]]),
    }
    return loss_sum, grad_x, gw, gs


PACK_COLS = 1024
N_CHIPS = 4
HALF_ROWS = 2048


def _pack_shard(parts):
    return jnp.concatenate([parts[n].reshape(-1, PACK_COLS) for n, _ in BIG], axis=0)


def _unpack_shard(slab, shapes):
    out, r = {}, 0
    for n, _ in BIG:
        rows = math.prod(shapes[n]) // PACK_COLS
        out[n] = slab[r:r + rows].reshape(shapes[n])
        r += rows
    return out


def _shard_of(full, axis, sidx):
    size = full.shape[axis] // N_CHIPS
    return lax.slice_in_dim(full, sidx * size, (sidx + 1) * size, axis=axis)


def _position():
    x, y, c = lax.axis_index("x"), lax.axis_index("y"), lax.axis_index("c")
    return x, y, c


def _other_chips(x, y):
    return [(1 - x, y), (x, 1 - y), (1 - x, 1 - y)]


ANY = pl.BlockSpec(memory_space=pl.ANY)


def _gather_weights(slab):
    rows = slab.shape[0]
    half = rows // 2

    def body(in_ref, out_ref, send_sems, recv_sems):
        x, y, c = _position()
        chips = _other_chips(x, y)

        def piece(chip, core):
            return out_ref.at[2 * chip[0] + chip[1], pl.ds(core * half, half), :]

        def copy(k, chip, core, to, src=None):
            return pltpu.make_async_remote_copy(
                src_ref=piece(chip, core) if src is None else src, dst_ref=piece(chip, core),
                send_sem=send_sems.at[k], recv_sem=recv_sems.at[k], device_id=to, device_id_type=MESH)

        first = [copy(j, (x, y), c, (*chip, c), src=in_ref.at[pl.ds(c * half, half), :]) for j, chip in enumerate(chips)]
        for cp in first:
            cp.start()
        passed = [copy(3 + j, chip, c, (x, y, 1 - c)) for j, chip in enumerate(chips)]
        for j, chip in enumerate(chips):
            copy(j, chip, c, (x, y, c)).wait_recv()
            passed[j].start()
        for j, chip in enumerate(chips):
            copy(3 + j, chip, 1 - c, (x, y, c)).wait_recv()
        for cp in first + passed:
            cp.wait_send()

    return pl.pallas_call(
        body, name="gather_weights", in_specs=[ANY], out_specs=ANY,
        out_shape=jax.ShapeDtypeStruct((N_CHIPS,) + slab.shape, slab.dtype),
        scratch_shapes=[pltpu.SemaphoreType.DMA((6,)), pltpu.SemaphoreType.DMA((6,))],
    )(slab)


def _exchange_halves(g):
    def body(g_ref, out_ref, send_sem, recv_sem):
        x, y, c = _position()
        cp = pltpu.make_async_remote_copy(src_ref=g_ref.at[1 - c], dst_ref=out_ref, send_sem=send_sem, recv_sem=recv_sem,
                                          device_id=(x, y, 1 - c), device_id_type=MESH)
        cp.start()
        cp.wait()

    return pl.pallas_call(
        body, name="exchange_halves", in_specs=[ANY], out_specs=ANY,
        out_shape=jax.ShapeDtypeStruct(g.shape[1:], g.dtype),
        scratch_shapes=[pltpu.SemaphoreType.DMA, pltpu.SemaphoreType.DMA],
    )(g)


def _add_my_half(g, r1, c_idx):
    tr = 256
    nt = g.shape[2] // tr

    def body(c_ref, g_ref, r_ref, o_ref, ob_ref):
        tot = g_ref[...] + r_ref[...]
        o_ref[...] = tot
        ob_ref[...] = tot.astype(BF16)

    blk = (None, tr, PACK_COLS)
    spec = pl.BlockSpec(blk, lambda s, i, c_ref: (s, i, 0))
    return pl.pallas_call(
        body, name="add_my_half",
        grid_spec=pltpu.PrefetchScalarGridSpec(
            num_scalar_prefetch=1, grid=(N_CHIPS, nt),
            in_specs=[pl.BlockSpec((None,) + blk, lambda s, i, c_ref: (c_ref[0], s, i, 0)), spec],
            out_specs=[spec, spec]),
        out_shape=[jax.ShapeDtypeStruct(g.shape[1:], F32), jax.ShapeDtypeStruct(g.shape[1:], BF16)],
        compiler_params=_cparams(("parallel", "parallel")),
    )(c_idx, g, r1)


def _scatter_to_chips(part):
    def body(p_ref, out_ref, send_sems, recv_sems):
        x, y, c = _position()
        chips = _other_chips(x, y)
        sends = [pltpu.make_async_remote_copy(
            src_ref=p_ref.at[2 * chip[0] + chip[1]], dst_ref=out_ref.at[j], send_sem=send_sems.at[j], recv_sem=recv_sems.at[j],
            device_id=(*chip, c), device_id_type=MESH) for j, chip in enumerate(chips)]
        for cp in sends:
            cp.start()
        for cp in sends:
            cp.wait()

    return pl.pallas_call(
        body, name="scatter_to_chips", in_specs=[ANY], out_specs=ANY,
        out_shape=jax.ShapeDtypeStruct((N_CHIPS - 1,) + part.shape[1:], part.dtype),
        scratch_shapes=[pltpu.SemaphoreType.DMA((3,)), pltpu.SemaphoreType.DMA((3,))],
    )(part)


def _sum_chips(part, r2, chip_idx):
    tr = 256

    def body(c_ref, p_ref, r_ref, o_ref):
        o_ref[...] = ((p_ref[...] + r_ref[0]) + r_ref[1]) + r_ref[2]

    return pl.pallas_call(
        body, name="sum_chips",
        grid_spec=pltpu.PrefetchScalarGridSpec(
            num_scalar_prefetch=1, grid=(r2.shape[1] // tr,),
            in_specs=[pl.BlockSpec((None, tr, PACK_COLS), lambda i, c_ref: (c_ref[0], i, 0)),
                      pl.BlockSpec((N_CHIPS - 1, tr, PACK_COLS), lambda i, c_ref: (0, i, 0))],
            out_specs=pl.BlockSpec((tr, PACK_COLS), lambda i, c_ref: (i, 0))),
        out_shape=jax.ShapeDtypeStruct(r2.shape[1:], F32),
        compiler_params=_cparams(("parallel",)),
    )(chip_idx, part, r2)


def _join_halves(red):
    def body(r_ref, out_ref, send_sem, recv_sem):
        x, y, c = _position()
        cp = pltpu.make_async_remote_copy(src_ref=r_ref, dst_ref=out_ref, send_sem=send_sem, recv_sem=recv_sem,
                                          device_id=(x, y, 1 - c), device_id_type=MESH)
        cp.start()
        cp.wait()

    return pl.pallas_call(
        body, name="join_halves", in_specs=[ANY], out_specs=ANY,
        out_shape=jax.ShapeDtypeStruct(red.shape, red.dtype),
        scratch_shapes=[pltpu.SemaphoreType.DMA, pltpu.SemaphoreType.DMA],
    )(red)


def _adamw_math(w, g, m, v):
    m = ADAM_B1 * m + (1.0 - ADAM_B1) * g
    v = ADAM_B2 * v + (1.0 - ADAM_B2) * (g * g)
    m_hat = m / (1.0 - ADAM_B1 ** ADAM_STEP)
    v_hat = v / (1.0 - ADAM_B2 ** ADAM_STEP)
    delta = -ADAM_LR * (m_hat / (jnp.sqrt(v_hat) + ADAM_EPS) + ADAM_WD * w)
    return delta, m, v


def _adamw(name, w, g, m, v):
    tr = min(256, w.shape[0])

    def body(w_ref, g_ref, m_ref, v_ref, d_ref, nm_ref, nv_ref):
        d_ref[...], nm_ref[...], nv_ref[...] = _adamw_math(w_ref[...], g_ref[...], m_ref[...], v_ref[...])

    blk = pl.BlockSpec((tr, w.shape[1]), lambda i: (i, 0))
    return pl.pallas_call(
        body, name="adamw_" + name, grid=(w.shape[0] // tr,), in_specs=[blk] * 4, out_specs=[blk] * 3,
        out_shape=[jax.ShapeDtypeStruct(w.shape, F32)] * 3, compiler_params=_cparams(("parallel",)),
    )(w, g, m, v)


def _small_step(gpk, wpk, mpk, vpk):
    row, col, width = SMALL["ret_decay_logit"]

    def body(g_ref, w_ref, m_ref, v_ref, og_ref, od_ref, om_ref, ov_ref, gbuf, send_sems, recv_sems):
        x, y, c = _position()
        me = 4 * x + 2 * y + c
        gbuf[me] = g_ref[...]
        sends = []
        for k in range(1, 8):
            to = (x ^ (k >> 2), y ^ ((k >> 1) & 1), c ^ (k & 1))
            cp = pltpu.make_async_remote_copy(src_ref=g_ref, dst_ref=gbuf.at[me], send_sem=send_sems.at[k - 1],
                                              recv_sem=recv_sems.at[k - 1], device_id=to, device_id_type=MESH)
            cp.start()
            sends.append(cp)
        for k in range(1, 8):
            frm = me ^ k
            pltpu.make_async_remote_copy(src_ref=g_ref, dst_ref=gbuf.at[frm], send_sem=send_sems.at[k - 1],
                                         recv_sem=recv_sems.at[k - 1], device_id=(x, y, c), device_id_type=MESH).wait_recv()
        for cp in sends:
            cp.wait_send()
        tot = gbuf[0]
        for d in range(1, 8):
            tot = tot + gbuf[d]
        w = w_ref[...]
        r_i = lax.broadcasted_iota(jnp.int32, w.shape, 0)
        c_i = lax.broadcasted_iota(jnp.int32, w.shape, 1)
        is_logit = (r_i == row) & (c_i >= col) & (c_i < col + width)
        g = jnp.where(is_logit, tot * _sigmoid(-w), tot)
        og_ref[...] = g
        od_ref[...], om_ref[...], ov_ref[...] = _adamw_math(w, g, m_ref[...], v_ref[...])

    vm = pl.BlockSpec(memory_space=pltpu.VMEM)
    shp = jax.ShapeDtypeStruct(gpk.shape, F32)
    return pl.pallas_call(
        body, name="small_step", in_specs=[vm] * 4, out_specs=[vm] * 4, out_shape=[shp] * 4,
        scratch_shapes=[pltpu.VMEM((8,) + gpk.shape, F32), pltpu.SemaphoreType.DMA((7,)), pltpu.SemaphoreType.DMA((7,))],
    )(gpk, wpk, mpk, vpk)


def _pack_small(parts):
    rows = [[] for _ in range(SMALL_ROWS)]
    for n, (r, col, width) in sorted(SMALL.items(), key=lambda kv: (kv[1][0], kv[1][1])):
        rows[r].append((col, parts[n].reshape(-1).astype(F32)))
    out = []
    for r in range(SMALL_ROWS):
        segs, pos = [], 0
        for col, vec in rows[r]:
            assert col == pos
            segs.append(vec)
            pos += vec.shape[0]
        if pos < PACK_COLS:
            segs.append(jnp.zeros((PACK_COLS - pos,), F32))
        out.append(jnp.concatenate(segs))
    return jnp.stack(out)


def _unpack_small(pk, shapes):
    return {n: pk[r, col:col + width].reshape(shapes[n]) for n, (r, col, width) in SMALL.items()}


WEIGHTS = ("mix_norm", "w_in", "attn_q_norm", "attn_k_norm", "ret_decay_logit", "ret_norm_gain", "w_attn_o", "w_ret_o", "w_out",
           "mlp_norm", "w_up", "w_down", "ple_norm", "w_ple_gate", "w_ple", "final_norm")


def kernel(x, p, mix_norm, w_in, attn_q_norm, attn_k_norm, ret_decay_logit, ret_norm_gain, w_attn_o, w_ret_o, w_out, mlp_norm, w_up, w_down, ple_norm, w_ple_gate, w_ple, final_norm, loss_target, m_mix_norm, m_w_in, m_attn_q_norm, m_attn_k_norm, m_ret_decay_logit, m_ret_norm_gain, m_w_attn_o, m_w_ret_o, m_w_out, m_mlp_norm, m_w_up, m_w_down, m_ple_norm, m_w_ple_gate, m_w_ple, m_final_norm, v_mix_norm, v_w_in, v_attn_q_norm, v_attn_k_norm, v_ret_decay_logit, v_ret_norm_gain, v_w_attn_o, v_w_ret_o, v_w_out, v_mlp_norm, v_w_up, v_w_down, v_ple_norm, v_w_ple_gate, v_w_ple, v_final_norm):
    args = dict(locals())
    wts = {n: args[n] for n in WEIGHTS}
    ms = {n: args["m_" + n] for n in WEIGHTS}
    vs = {n: args["v_" + n] for n in WEIGHTS}
    shapes = {n: wts[n].shape for n in WEIGHTS}
    big_names = [n for n, _ in BIG]
    xi, yi, ci = _position()
    c_idx = ci.astype(jnp.int32).reshape(1)

    chip_idx = (2 * xi + yi).astype(jnp.int32)
    slab_b = _pack_shard({n: wts[n][0].astype(BF16) for n in big_names})
    gathered = lax.dynamic_update_slice(_gather_weights(slab_b), slab_b[None], (chip_idx, 0, 0))
    full = {}
    for n, axis in BIG:
        per_chip = [_unpack_shard(gathered[k], {m_: shapes[m_][1:] for m_ in big_names})[n] for k in range(N_CHIPS)]
        full[n] = jnp.concatenate(per_chip, axis=axis)
    small = {n: wts[n].reshape(wts[n].shape[1:] if wts[n].ndim > 1 else wts[n].shape) for n in SMALL}

    loss_part, grad_x, gw, gs = _local_step(x[0], p[0, 0], loss_target[0], full, small)
    loss = lax.psum(loss_part, ("x", "y", "c"))

    slabs = jnp.stack([_pack_shard({n: _shard_of(gw[n], axis, k) for n, axis in BIG}) for k in range(N_CHIPS)])
    halves = slabs.reshape(N_CHIPS, 2, HALF_ROWS, PACK_COLS).transpose(1, 0, 2, 3)
    chip_part, chip_part_b = _add_my_half(halves, _exchange_halves(halves), c_idx)
    mine = _sum_chips(chip_part, _scatter_to_chips(chip_part_b), chip_idx.reshape(1))
    both = jnp.stack([mine, _join_halves(mine)])
    reduced = jnp.where(ci == 0, both, both[::-1]).reshape(2 * HALF_ROWS, PACK_COLS)
    g_big = _unpack_shard(reduced, {n: shapes[n][1:] for n in big_names})
    big_out = [{}, {}, {}, {}]
    for n in big_names:
        big_out[0][n] = g_big[n][None]
        for kind, a in enumerate(_adamw(n, wts[n][0], g_big[n], ms[n][0], vs[n][0])):
            big_out[kind + 1][n] = a[None]

    sm_out = _small_step(_pack_small(gs), _pack_small({n: wts[n] for n in SMALL}), _pack_small({n: ms[n] for n in SMALL}),
                         _pack_small({n: vs[n] for n in SMALL}))
    small_out = [_unpack_small(a, {n: shapes[n] for n in SMALL}) for a in sm_out]

    outs = [loss, grad_x[None]]
    for kind in range(4):
        for n in WEIGHTS:
            outs.append(small_out[kind][n] if n in SMALL else big_out[kind][n])
    return tuple(outs)
```

```python
import math

import jax
import jax.numpy as jnp
from jax import lax
from jax.experimental import pallas as pl
from jax.experimental.pallas import tpu as pltpu

F32 = jnp.float32
BF16 = jnp.bfloat16
MESH = pl.DeviceIdType.MESH

D_MODEL = 1024
GRID_W = 64
ATTN_HEAD_DIM = 64
ATTN_HEADS = 8
ATTN_KV_HEADS = 2
ATTN_GROUP = ATTN_HEADS // ATTN_KV_HEADS
RET_HEAD_DIM = 128
RET_HEADS = 4
ATTN_Q_W = 512
ATTN_KV_W = 128
RET_W = 512
IN_W = 4864
D_FF = 4096
RET_CHUNK = 128
ROPE_THETA = 10000.0
NORM_EPS = 1e-6
GN_EPS = 1e-5
ATTN_SCALE = ATTN_HEAD_DIM ** -0.5
LOG2E = math.log2(math.e)
Q_FOLD = ATTN_SCALE * LOG2E
RET_SCALE = RET_HEAD_DIM ** -0.5

C_AQ, C_AK, C_AV, C_RQ, C_RK, C_RV, C_RG, C_GA, C_GR = 0, 512, 640, 768, 1280, 1792, 2304, 2816, 3840

ADAM_LR = 0.001
ADAM_B1 = 0.9
ADAM_B2 = 0.999
ADAM_EPS = 1e-08
ADAM_WD = 0.01
ADAM_STEP = 10

LANES = 128
VMEM_LIMIT = 56 << 20
SEQ_TILE = 512
EPI_PIECE = 256

BIG = (("w_in", 1), ("w_attn_o", 1), ("w_ret_o", 1), ("w_out", 0), ("w_up", 1), ("w_down", 0), ("w_ple_gate", 0), ("w_ple", 1))
SMALL_ROWS = 8
SMALL = {"mix_norm": (0, 0, 1024), "mlp_norm": (1, 0, 1024), "ple_norm": (2, 0, 1024), "final_norm": (3, 0, 1024),
         "ret_norm_gain": (4, 0, 512), "attn_q_norm": (4, 512, 64), "attn_k_norm": (4, 576, 64), "ret_decay_logit": (4, 640, 8)}


def _seq_tile(s):
    return min(SEQ_TILE, s // 2)


def _cparams(sem=None, vmem=VMEM_LIMIT):
    return pltpu.CompilerParams(dimension_semantics=sem, vmem_limit_bytes=vmem)


def _mm(name, a, b, *, ta=False, tb=False, tm, tn, tk, out_dtypes=(F32,), epi=None, epi_ins=(), consts=(), n_sums=0, j_outer=False):
    if ta:
        kdim, m = a.shape
    else:
        m, kdim = a.shape
    n = b.shape[0] if tb else b.shape[1]
    tm, tn, tk = min(tm, m), min(tn, n), min(tk, kdim)
    assert m % tm == 0 and n % tn == 0 and kdim % tk == 0, (name, m, n, kdim, tm, tn, tk)
    nk = kdim // tk
    e_arrs, e_cols = [], []
    for item in epi_ins:
        if isinstance(item, tuple):
            arr, start = item
            assert tn == n and start % EPI_PIECE == 0 and n % EPI_PIECE == 0
            for piece in range(n // EPI_PIECE):
                e_arrs.append(arr)
                e_cols.append(start // EPI_PIECE + piece)
        else:
            e_arrs.append(item)
            e_cols.append(None)
    n_e, n_c, n_o = len(e_arrs), len(consts), len(out_dtypes)
    assert n_sums == 0 or tn == n

    def body(*refs):
        a_ref, b_ref = refs[0], refs[1]
        e_refs = refs[2:2 + n_e]
        c_refs = refs[2 + n_e:2 + n_e + n_c]
        o_refs = refs[2 + n_e + n_c:2 + n_e + n_c + n_o]
        s_refs = refs[2 + n_e + n_c + n_o:2 + n_e + n_c + n_o + n_sums]
        acc_ref = refs[2 + n_e + n_c + n_o + n_sums] if nk > 1 else None
        k = pl.program_id(2)
        if n_sums:
            @pl.when((pl.program_id(1 if j_outer else 0) == 0) & (k == 0))
            def _():
                for r in s_refs:
                    r[...] = jnp.zeros(r.shape, F32)
        av = a_ref[...].astype(BF16)
        bv = b_ref[...].astype(BF16)
        dims = (((0,) if ta else (1,), (1,) if tb else (0,)), ((), ()))
        part = lax.dot_general(av, bv, dims, preferred_element_type=F32)

        def finish(acc):
            vals = epi(acc, e_refs, c_refs) if epi is not None else (acc,)
            for o_ref, v in zip(o_refs, vals[:n_o]):
                o_ref[...] = v.astype(o_ref.dtype)
            for s_ref, v in zip(s_refs, vals[n_o:]):
                _acc_add(s_ref, v)

        if nk == 1:
            finish(part)
        else:
            @pl.when(k == 0)
            def _():
                acc_ref[...] = part

            @pl.when(k > 0)
            def _():
                acc_ref[...] += part

            @pl.when(k == nk - 1)
            def _():
                finish(acc_ref[...])

    def spec(shape, index):
        return pl.BlockSpec(shape, (lambda j, i, k: index(i, j, k)) if j_outer else index)

    a_spec = spec((tk, tm), lambda i, j, k: (k, i)) if ta else spec((tm, tk), lambda i, j, k: (i, k))
    b_spec = spec((tn, tk), lambda i, j, k: (j, k)) if tb else spec((tk, tn), lambda i, j, k: (k, j))
    o_spec = spec((tm, tn), lambda i, j, k: (i, j))
    c_specs = [spec(c.shape, lambda i, j, k, nd=c.ndim: (0,) * nd) for c in consts]
    outs = pl.pallas_call(
        body, name=name,
        grid=(n // tn, m // tm, nk) if j_outer else (m // tm, n // tn, nk),
        in_specs=([a_spec, b_spec]
                  + [o_spec if cb is None else spec((tm, EPI_PIECE), lambda i, j, k, cb=cb: (i, cb)) for cb in e_cols] + c_specs),
        out_specs=[o_spec] * n_o + [spec((8, n), lambda i, j, k: (0, 0))] * n_sums,
        out_shape=[jax.ShapeDtypeStruct((m, n), dt) for dt in out_dtypes] + [jax.ShapeDtypeStruct((8, n), F32)] * n_sums,
        scratch_shapes=[pltpu.VMEM((tm, tn), F32)] if nk > 1 else [],
        compiler_params=_cparams(("arbitrary",) * 3 if n_sums else ("parallel", "parallel", "arbitrary")),
    )(a, b, *e_arrs, *consts)
    return outs[0] if n_o + n_sums == 1 else outs


def _rows(arr, tr):
    return (arr, pl.BlockSpec((tr, arr.shape[1]), lambda i: (i, 0)))


def _win(arr, tr, start, width):
    bw = math.gcd(start, width) if start else width
    assert bw % LANES == 0
    return [(arr, pl.BlockSpec((tr, bw), lambda i, cb=start // bw + p: (i, cb))) for p in range(width // bw)]


def _ct(arr):
    return (arr, pl.BlockSpec((None,) + arr.shape[1:], lambda i: (i, 0, 0)))


def _whole(arr):
    return (arr, pl.BlockSpec(arr.shape, lambda i, nd=arr.ndim: (0,) * nd))


def _cat(refs):
    vals = [r[...].astype(F32) for r in refs]
    return vals[0] if len(vals) == 1 else jnp.concatenate(vals, axis=1)


def _seqtiled(name, fn, n_tiles, ins, outs, acc_widths=()):
    n_i, n_o, n_a = len(ins), len(outs), len(acc_widths)

    def body(*refs):
        i_refs, o_refs, a_refs = refs[:n_i], refs[n_i:n_i + n_o], refs[n_i + n_o:]
        if n_a:
            @pl.when(pl.program_id(0) == 0)
            def _():
                for r in a_refs:
                    r[...] = jnp.zeros(r.shape, F32)
        fn(list(i_refs), list(o_refs), list(a_refs))

    res = pl.pallas_call(
        body, name=name, grid=(n_tiles,),
        in_specs=[s for _, s in ins],
        out_specs=[s for _, _, s in outs] + [pl.BlockSpec((8, w), lambda i: (0, 0)) for w in acc_widths],
        out_shape=[jax.ShapeDtypeStruct(sh, dt) for sh, dt, _ in outs] + [jax.ShapeDtypeStruct((8, w), F32) for w in acc_widths],
        compiler_params=_cparams(("arbitrary",)),
    )(*[a for a, _ in ins])
    return res


def _acc_add(acc_ref, val):
    acc_ref[0:1, :] += jnp.sum(val, axis=0, keepdims=True)


def _out_rows(s, w, dt, tr):
    return ((s, w), dt, pl.BlockSpec((tr, w), lambda i: (i, 0)))


def _out_ct(s, w, dt, t):
    return ((s // t, w, t), dt, pl.BlockSpec((None, w, t), lambda i: (i, 0, 0)))


def _rms_fwd(x, gain):
    r = lax.rsqrt(jnp.mean(x * x, axis=-1, keepdims=True) + NORM_EPS)
    return x * r * gain


def _rms_bwd(dy, x, gain):
    r = lax.rsqrt(jnp.mean(x * x, axis=-1, keepdims=True) + NORM_EPS)
    xn = x * r
    dyg = dy * gain
    dx = r * (dyg - xn * jnp.mean(dyg * xn, axis=-1, keepdims=True))
    return dx, dy * xn


def _seg_mean(y, hd):
    w = y.shape[1]
    pieces = []
    for s in range(0, w, LANES):
        v = y[:, s:s + LANES]
        tot = jnp.sum(v, axis=1, keepdims=True)
        if hd == LANES:
            pieces.append(jnp.broadcast_to(tot, v.shape))
        else:
            low = lax.broadcasted_iota(jnp.int32, v.shape, 1) < hd
            lo = jnp.sum(jnp.where(low, v, 0.0), axis=1, keepdims=True)
            pieces.append(jnp.where(low, lo, tot - lo))
    out = pieces[0] if len(pieces) == 1 else jnp.concatenate(pieces, axis=1)
    return out * (1.0 / hd)


def _tile_lanes(t, w):
    return t if w == t.shape[1] else jnp.concatenate([t] * (w // t.shape[1]), axis=1)


def _swap_halves(x, hd):
    w = x.shape[1]
    half = hd // 2
    lane = lax.broadcasted_iota(jnp.int32, x.shape, 1)
    return jnp.where((lane % hd) < half, pltpu.roll(x, w - half, 1), pltpu.roll(x, half, 1))


def _rope(x, cos, sin_signed, hd):
    w = x.shape[1]
    return x * _tile_lanes(cos, w) + _swap_halves(x, hd) * _tile_lanes(sin_signed, w)


def _rope_t(dy, cos, sin_signed, hd):
    w = dy.shape[1]
    return dy * _tile_lanes(cos, w) + _swap_halves(dy * _tile_lanes(sin_signed, w), hd)


def _headnorm_fwd(x, gain_w, hd):
    r = lax.rsqrt(_seg_mean(x * x, hd) + NORM_EPS)
    return x * r * gain_w


def _headnorm_bwd(dy, x, gain_w, hd):
    r = lax.rsqrt(_seg_mean(x * x, hd) + NORM_EPS)
    xn = x * r
    dyg = dy * gain_w
    return r * (dyg - xn * _seg_mean(dyg * xn, hd)), dy * xn


def _sigmoid(x):
    return 1.0 / (1.0 + jnp.exp(-x))


def _rope_tables(seq_len, head_dim):
    rows = seq_len // GRID_W
    n_axis = head_dim // 4
    freqs = ROPE_THETA ** (-jnp.arange(n_axis, dtype=F32) / n_axis)
    ang_r = jnp.arange(rows, dtype=F32)[:, None] * freqs
    ang_c = jnp.arange(GRID_W, dtype=F32)[:, None] * freqs

    def expand(by_row, by_col):
        r = jnp.broadcast_to(by_row[:, None, :], (rows, GRID_W, n_axis))
        c = jnp.broadcast_to(by_col[None, :, :], (rows, GRID_W, n_axis))
        return jnp.concatenate([r, c], axis=-1).reshape(seq_len, 2 * n_axis)

    cos, sin = expand(jnp.cos(ang_r), jnp.cos(ang_c)), expand(jnp.sin(ang_r), jnp.sin(ang_c))
    reps = LANES // head_dim
    return jnp.tile(jnp.concatenate([cos, cos], axis=-1), (1, reps)), jnp.tile(jnp.concatenate([-sin, sin], axis=-1), (1, reps))


def _stage_norm_in(x, gain):
    s = x.shape[0]
    tr = min(SEQ_TILE, s)

    def fn(i, o, a):
        o[0][...] = _rms_fwd(i[0][...], i[1][...]).astype(BF16)

    return _seqtiled("norm_in", fn, s // tr, [_rows(x, tr), _whole(gain)], [_out_rows(s, D_MODEL, BF16, tr)])[0]


def _stage_qkv(proj, tabs, gq_w, gk_w):
    s = proj.shape[0]
    t = _seq_tile(s)
    ca, sa, cr, sr = tabs
    ins = (_win(proj, t, C_AQ, ATTN_Q_W) + _win(proj, t, C_AK, ATTN_KV_W) + _win(proj, t, C_AV, ATTN_KV_W)
           + _win(proj, t, C_RQ, RET_W) + _win(proj, t, C_RK, RET_W)
           + [_rows(ca, t), _rows(sa, t), _rows(cr, t), _rows(sr, t), _whole(gq_w), _whole(gk_w)])

    def fn(i, o, a):
        aq, ak, av = (i[n][...].astype(F32) for n in range(3))
        rq, rk = _cat(i[3:5]), _cat(i[5:7])
        ca_, sa_, cr_, sr_ = i[7][...], i[8][...], i[9][...], i[10][...]
        qr = _rope(_headnorm_fwd(aq, i[11][...], ATTN_HEAD_DIM), ca_, sa_, ATTN_HEAD_DIM) * Q_FOLD
        kr = _rope(_headnorm_fwd(ak, i[12][...], ATTN_HEAD_DIM), ca_, sa_, ATTN_HEAD_DIM)
        qt = qr.T.astype(BF16)
        zeros = jnp.zeros((ATTN_HEAD_DIM, t), BF16)
        for h in range(ATTN_HEADS):
            g = h // ATTN_GROUP
            blk = qt[h * ATTN_HEAD_DIM:(h + 1) * ATTN_HEAD_DIM, :]
            o[0][h * LANES + g * ATTN_HEAD_DIM:h * LANES + (g + 1) * ATTN_HEAD_DIM, :] = blk
            o[0][h * LANES + (1 - g) * ATTN_HEAD_DIM:h * LANES + (2 - g) * ATTN_HEAD_DIM, :] = zeros
        o[1][...] = kr.astype(BF16)
        o[2][...] = kr.T.astype(BF16)
        o[3][...] = av.astype(BF16)
        o[4][...] = av.T.astype(BF16)
        o[5][...] = _rope(rq, cr_, sr_, RET_HEAD_DIM) * RET_SCALE
        o[6][...] = _rope(rk, cr_, sr_, RET_HEAD_DIM)

    outs = [_out_ct(s, ATTN_HEADS * LANES, BF16, t), _out_rows(s, ATTN_KV_W, BF16, t), _out_ct(s, ATTN_KV_W, BF16, t),
            _out_rows(s, ATTN_KV_W, BF16, t), _out_ct(s, ATTN_KV_W, BF16, t), _out_rows(s, RET_W, F32, t), _out_rows(s, RET_W, F32, t)]
    return _seqtiled("qkv_prep", fn, s // t, ins, outs)


def _groupnorm_gate(ry, rg, gain):
    mu = _seg_mean(ry, RET_HEAD_DIM)
    d = ry - mu
    rs = lax.rsqrt(_seg_mean(d * d, RET_HEAD_DIM) + GN_EPS)
    return d * rs, rs, _sigmoid(rg)


def _stage_mix_post(ry_f, ry_b, proj, o_ct, gain):
    s = proj.shape[0]
    t = _seq_tile(s)
    ins = [_rows(ry_f, t), _rows(ry_b, t)] + _win(proj, t, C_RG, RET_W) + [_ct(o_ct), _whole(gain)]

    def fn(i, o, a):
        ry = i[0][...] + i[1][...]
        rg = _cat(i[2:4])
        gn, _, sg = _groupnorm_gate(ry, rg, None)
        o[0][...] = (gn * i[5][...] * (rg * sg)).astype(BF16)
        o[1][...] = i[4][...].astype(F32).T.astype(BF16)

    return _seqtiled("mix_post", fn, s // t, ins, [_out_rows(s, RET_W, BF16, t), _out_rows(s, ATTN_Q_W, BF16, t)])


def _stage_mix_post_bwd(dattn, attn_rows, drz, ry_f, ry_b, proj, gain):
    s = proj.shape[0]
    t = _seq_tile(s)
    ins = ([_rows(dattn, t), _rows(attn_rows, t), _rows(drz, t), _rows(ry_f, t), _rows(ry_b, t)]
           + _win(proj, t, C_RG, RET_W) + [_whole(gain)])

    def fn(i, o, a):
        da = i[0][...]
        dat = da.T
        prod_t = (da * i[1][...].astype(F32)).T
        dat_b = dat.astype(BF16)
        zeros = jnp.zeros((ATTN_HEAD_DIM, t), BF16)
        for h in range(ATTN_HEADS):
            g = h // ATTN_GROUP
            o[0][h * LANES + g * ATTN_HEAD_DIM:h * LANES + (g + 1) * ATTN_HEAD_DIM, :] = dat_b[h * ATTN_HEAD_DIM:(h + 1) * ATTN_HEAD_DIM, :]
            o[0][h * LANES + (1 - g) * ATTN_HEAD_DIM:h * LANES + (2 - g) * ATTN_HEAD_DIM, :] = zeros
            o[1][h] = jnp.sum(prod_t[h * ATTN_HEAD_DIM:(h + 1) * ATTN_HEAD_DIM, :], axis=0, keepdims=True)
        ry = i[3][...] + i[4][...]
        rg = _cat(i[5:7])
        gain_ = i[7][...]
        gn, rs, sg = _groupnorm_gate(ry, rg, None)
        dz = i[2][...]
        silu = rg * sg
        _acc_add(a[0], dz * gn * silu)
        dgn = dz * gain_ * silu
        o[2][...] = rs * (dgn - _seg_mean(dgn, RET_HEAD_DIM) - gn * _seg_mean(dgn * gn, RET_HEAD_DIM))
        o[3][...] = (dz * gn * gain_ * (sg * (1.0 + rg * (1.0 - sg)))).astype(BF16)

    outs = [_out_ct(s, ATTN_HEADS * LANES, BF16, t),
            ((ATTN_HEADS, s // t, 1, t), F32, pl.BlockSpec((ATTN_HEADS, None, 1, t), lambda i: (0, i, 0, 0))),
            _out_rows(s, RET_W, F32, t), _out_rows(s, RET_W, BF16, t)]
    return _seqtiled("mix_post_bwd", fn, s // t, ins, outs, acc_widths=(RET_W,))


def _stage_dproj(proj, dq_ct, dk8, dv8, rgrads, drg, dga, dgr, tabs, gq_w, gk_w):
    s = proj.shape[0]
    t = _seq_tile(s)
    ca, sa, cr, sr = tabs
    kv8 = pl.BlockSpec((ATTN_HEADS, t, ATTN_KV_W), lambda i: (0, i, 0))
    ins = (_win(proj, t, C_AQ, ATTN_Q_W) + _win(proj, t, C_AK, ATTN_KV_W) + [_ct(dq_ct), (dk8, kv8), (dv8, kv8)]
           + [_rows(g, t) for g in rgrads] + [_rows(drg, t), _rows(dga, t), _rows(dgr, t)]
           + [_rows(ca, t), _rows(sa, t), _rows(cr, t), _rows(sr, t), _whole(gq_w), _whole(gk_w)])

    def fn(i, o, a):
        aq, ak = i[0][...].astype(F32), i[1][...].astype(F32)
        dq_f, dk_f, dv_f, dq_b, dk_b, dv_b = (r[...].astype(F32) for r in i[5:11])
        ca_, sa_, cr_, sr_ = i[14][...], i[15][...], i[16][...], i[17][...]
        dqn = _rope_t(i[2][...].T * ATTN_SCALE, ca_, sa_, ATTN_HEAD_DIM)
        daq, gq_rows = _headnorm_bwd(dqn, aq, i[18][...], ATTN_HEAD_DIM)
        dkn = _rope_t(jnp.sum(i[3][...].astype(F32), axis=0) * (1.0 / LOG2E), ca_, sa_, ATTN_HEAD_DIM)
        dak, gk_rows = _headnorm_bwd(dkn, ak, i[19][...], ATTN_HEAD_DIM)
        _acc_add(a[0], gq_rows)
        _acc_add(a[1], gk_rows)
        out = o[0]
        out[:, C_AQ:C_AQ + ATTN_Q_W] = daq.astype(BF16)
        out[:, C_AK:C_AK + ATTN_KV_W] = dak.astype(BF16)
        out[:, C_AV:C_AV + ATTN_KV_W] = jnp.sum(i[4][...].astype(F32), axis=0).astype(BF16)
        out[:, C_RQ:C_RQ + RET_W] = _rope_t((dq_f + dq_b) * RET_SCALE, cr_, sr_, RET_HEAD_DIM).astype(BF16)
        out[:, C_RK:C_RK + RET_W] = _rope_t(dk_f + dk_b, cr_, sr_, RET_HEAD_DIM).astype(BF16)
        out[:, C_RV:C_RV + RET_W] = (dv_f + dv_b).astype(BF16)
        out[:, C_RG:C_RG + RET_W] = i[11][...]
        out[:, C_GA:C_GA + D_MODEL] = i[12][...]
        out[:, C_GR:C_GR + D_MODEL] = i[13][...]

    return _seqtiled("dproj", fn, s // t, ins, [_out_rows(s, IN_W, BF16, t)], acc_widths=(ATTN_Q_W, ATTN_KV_W))


def _attn_fwd(q_ct, k_rows, v_ct):
    nq, _, t = q_ct.shape
    s = nq * t
    nk = nq
    assert nk % 2 == 0
    n_ch = next(n for n in (8, 4, 2) if nq % n == 0)
    halves = 2 if t % (2 * LANES) == 0 else 1
    tq = t // halves
    n_par = n_ch * halves

    def body(q_ref, k_ref, v_ref, o_ref, lse_ref, *bufs):
        sbuf = tuple(bufs[2 * w:2 * w + 2] for w in range(n_par))
        pbuf = tuple(bufs[2 * n_par + 2 * w:2 * n_par + 2 * w + 2] for w in range(n_par))

        def where(w):
            return w // halves, slice((w % halves) * tq, (w % halves + 1) * tq)

        def scores(w, j, slot):
            kj = k_ref[pl.ds(pl.multiple_of(j * t, t), t), :]
            cw, lanes = where(w)
            st = jnp.dot(kj, q_ref[cw, :, lanes], preferred_element_type=F32)
            sbuf[w][slot][...] = st
            return jnp.max(st, axis=0, keepdims=True)

        def probs(w, slot, cmax, m, l):
            m_new = jnp.maximum(m, cmax)
            alpha = jnp.exp2(m - m_new)
            pt = jnp.exp2(sbuf[w][slot][...] - m_new)
            pbuf[w][slot][...] = pt.astype(BF16)
            return m_new, alpha * l + jnp.sum(pt, axis=0, keepdims=True), alpha

        def values(w, j, slot, alpha, acc):
            return alpha * acc + jnp.dot(v_ref[j], pbuf[w][slot][...], preferred_element_type=F32)

        init = []
        for w in range(n_par):
            m = jnp.full((1, tq), -1e30, F32)
            l = jnp.zeros((1, tq), F32)
            cmax0 = scores(w, 0, 0)
            cmax1 = scores(w, 1, 1)
            m, l, alpha0 = probs(w, 0, cmax0, m, l)
            init.append((m, l, jnp.zeros((ATTN_HEAD_DIM, tq), F32), cmax1, alpha0))

        def trip(n, carry):
            c = 2 * n
            out = []
            for w in range(n_par):
                m, l, acc, cmax_b, alpha_c = carry[w]
                acc = values(w, c, 0, alpha_c, acc)
                m, l, alpha1 = probs(w, 1, cmax_b, m, l)
                cmax2 = scores(w, c + 2, 0)
                acc = values(w, c + 1, 1, alpha1, acc)
                m, l, alpha2 = probs(w, 0, cmax2, m, l)
                cmax3 = scores(w, c + 3, 1)
                out.append((m, l, acc, cmax3, alpha2))
            return tuple(out)

        res = lax.fori_loop(0, nk // 2 - 1, trip, tuple(init))
        for w in range(n_par):
            m, l, acc, cmax_b, alpha_c = res[w]
            acc = values(w, nk - 2, 0, alpha_c, acc)
            m, l, alpha1 = probs(w, 1, cmax_b, m, l)
            acc = values(w, nk - 1, 1, alpha1, acc)
            cw, lanes = where(w)
            o_ref[cw, :, lanes] = (acc / l).astype(BF16)
            lse_ref[cw, :, lanes] = m + jnp.log2(l)

    return pl.pallas_call(
        body, name="attn_fwd", grid=(ATTN_HEADS, nq // n_ch),
        in_specs=[pl.BlockSpec((n_ch, LANES, t), lambda h, i: (i, h, 0)),
                  pl.BlockSpec((s, ATTN_KV_W), lambda h, i: (0, 0)),
                  pl.BlockSpec((nk, ATTN_HEAD_DIM, t), lambda h, i: (0, h // ATTN_GROUP, 0))],
        out_specs=[pl.BlockSpec((n_ch, ATTN_HEAD_DIM, t), lambda h, i: (i, h, 0)),
                   pl.BlockSpec((None, n_ch, 1, t), lambda h, i: (h, i, 0, 0))],
        out_shape=[jax.ShapeDtypeStruct((nq, ATTN_Q_W, t), BF16), jax.ShapeDtypeStruct((ATTN_HEADS, nq, 1, t), F32)],
        scratch_shapes=[pltpu.VMEM((t, tq), F32)] * (2 * n_par) + [pltpu.VMEM((t, tq), BF16)] * (2 * n_par),
        compiler_params=_cparams(("parallel", "parallel")),
    )(q_ct, k_rows, v_ct)


def _attn_bwd(q_ct, do_ct, lse, delta, k_rows, v_rows, k_ct):
    nq, _, t = q_ct.shape
    s = nq * t
    kc = 4 if nq % 4 == 0 else 2
    tk = kc * t
    nk = nq // kc
    assert nq % 2 == 0 and nq % kc == 0

    def body(q_ref, do_ref, lse_ref, delta_ref, k_ref, v_ref, kt_ref, dq_ref, dk_ref, dv_ref, dk_acc, dv_acc,
             sb0, sb1, db0, db1, pb0, pb1, gb0, gb1):
        j = pl.program_id(1)
        sb, db, pb, gb = (sb0, sb1), (db0, db1), (pb0, pb1), (gb0, gb1)

        @pl.when(j == 0)
        def _():
            dq_ref[...] = jnp.zeros(dq_ref.shape, F32)

        kj, vj = k_ref[...], v_ref[...]
        ktj = jnp.concatenate([kt_ref[u] for u in range(kc)], axis=1)
        dk_acc[...] = jnp.zeros(dk_acc.shape, F32)
        dv_acc[...] = jnp.zeros(dv_acc.shape, F32)

        def products(i, slot):
            sb[slot][...] = jnp.dot(kj, q_ref[i], preferred_element_type=F32)
            db[slot][...] = jnp.dot(vj, do_ref[i], preferred_element_type=F32)

        def cotangents(i, slot):
            pt = jnp.exp2(sb[slot][...] - lse_ref[i])
            pb[slot][...] = pt.astype(BF16)
            gb[slot][...] = (pt * (db[slot][...] - delta_ref[i])).astype(BF16)

        def accumulate(i, slot):
            dst = gb[slot][...]
            dv_acc[...] += _nt(pb[slot][...], do_ref[i])
            dk_acc[...] += _nt(dst, q_ref[i])
            dq_ref[i] += jnp.dot(ktj, dst, preferred_element_type=F32)

        products(0, 0)
        products(1, 1)
        cotangents(0, 0)

        def trip(n, carry):
            c = 2 * n
            accumulate(c, 0)
            cotangents(c + 1, 1)
            products(c + 2, 0)
            accumulate(c + 1, 1)
            cotangents(c + 2, 0)
            products(c + 3, 1)
            return carry

        lax.fori_loop(0, nq // 2 - 1, trip, 0)
        accumulate(nq - 2, 0)
        cotangents(nq - 1, 1)
        accumulate(nq - 1, 1)
        dk_ref[...] = dk_acc[...].astype(dk_ref.dtype)
        dv_ref[...] = dv_acc[...].astype(dv_ref.dtype)

    per_head = pl.BlockSpec((nq, LANES, t), lambda h, j: (0, h, 0))
    stat = pl.BlockSpec((None, nq, 1, t), lambda h, j: (h, 0, 0, 0))
    kv_rows = pl.BlockSpec((tk, ATTN_KV_W), lambda h, j: (j, 0))
    kv_out = pl.BlockSpec((None, tk, ATTN_KV_W), lambda h, j: (h, j, 0))
    return pl.pallas_call(
        body, name="attn_bwd", grid=(ATTN_HEADS, nk),
        in_specs=[per_head, per_head, stat, stat, kv_rows, kv_rows,
                  pl.BlockSpec((kc, ATTN_HEAD_DIM, t), lambda h, j: (j, h // ATTN_GROUP, 0))],
        out_specs=[pl.BlockSpec((nq, ATTN_HEAD_DIM, t), lambda h, j: (0, h, 0)), kv_out, kv_out],
        out_shape=[jax.ShapeDtypeStruct((nq, ATTN_Q_W, t), F32), jax.ShapeDtypeStruct((ATTN_HEADS, s, ATTN_KV_W), BF16),
                   jax.ShapeDtypeStruct((ATTN_HEADS, s, ATTN_KV_W), BF16)],
        scratch_shapes=([pltpu.VMEM((tk, ATTN_KV_W), F32)] * 2 + [pltpu.VMEM((tk, t), F32)] * 4 + [pltpu.VMEM((tk, t), BF16)] * 4),
        compiler_params=_cparams(("parallel", "arbitrary")),
    )(q_ct, do_ct, lse, delta, k_rows, v_rows, k_ct)


def _log_sigmoid(x):
    t = jnp.exp(-jnp.abs(x))
    log1p_t = jnp.where(t < 1e-2, t * (1.0 - t * (0.5 - t * (1.0 / 3.0))), jnp.log(1.0 + t))
    return jnp.minimum(x, 0.0) - log1p_t


def _decay_tables(logit, backward):
    c = RET_CHUNK
    lam = _log_sigmoid(jnp.full((c, c), logit, F32))
    ii = lax.broadcasted_iota(jnp.int32, (c, c), 0).astype(F32)
    jj = lax.broadcasted_iota(jnp.int32, (c, c), 1).astype(F32)
    if not backward:
        dist, dist_t = jnp.maximum(ii - jj, 0.0), jnp.maximum(jj - ii, 0.0)
        mask, mask_t = ii >= jj, jj >= ii
        e_q, e_k = ii + 1.0, (c - 1.0) - ii
    else:
        dist, dist_t = jnp.maximum(jj - ii, 0.0), jnp.maximum(ii - jj, 0.0)
        mask, mask_t = jj > ii, ii > jj
        e_q, e_k = c - ii, ii
    return dict(
        d=jnp.where(mask, jnp.exp(lam * dist), 0.0), d_t=jnp.where(mask_t, jnp.exp(lam * dist_t), 0.0), dist=dist,
        qdec=jnp.exp(lam * e_q), kdec=jnp.exp(lam * e_k), e_q=e_q, e_k=e_k, gam=jnp.exp(lam * c))


def _nt(a, b):
    return lax.dot_general(a, b, (((1,), (1,)), ((), ())), preferred_element_type=F32)


def _ret_sub(n_chunks):
    return 4 if n_chunks % 4 == 0 else 2


def _ret_fwd(logits, q, k, proj):
    s = q.shape[0]
    c = RET_CHUNK
    sub = _ret_sub(s // c)
    nb = s // (c * sub)
    block = (lambda n: n, lambda n: nb - 1 - n)
    order = (tuple(range(sub)), tuple(reversed(range(sub))))
    vwin = _win(proj, c * sub, C_RV, RET_W)
    nv = len(vwin)
    vw = RET_W // nv
    per = 2 + nv

    def body(lg_ref, *refs):
        ins, outs, states = refs[:2 * per], refs[2 * per:2 * per + 4], refs[2 * per + 4:]

        @pl.when(pl.program_id(0) == 0)
        def _():
            for st in states:
                st[...] = jnp.zeros(st.shape, F32)

        for h in range(RET_HEADS):
            for d in range(2):
                q_ref, k_ref, v_refs = ins[d * per], ins[d * per + 1], ins[d * per + 2:(d + 1) * per]
                y_ref, st_ref, state = outs[2 * d], outs[2 * d + 1], states[d]
                tb = _decay_tables(lg_ref[d, h], bool(d))
                sl = slice(h * RET_HEAD_DIM, (h + 1) * RET_HEAD_DIM)
                off = h * RET_HEAD_DIM
                sh = state[h]
                for u in order[d]:
                    rows = slice(u * c, (u + 1) * c)
                    qh, kh = q_ref[rows, sl], k_ref[rows, sl]
                    vb = v_refs[off // vw][rows, off % vw:off % vw + RET_HEAD_DIM].astype(BF16)
                    a = _nt(qh.astype(BF16), kh.astype(BF16)) * tb["d"]
                    st_ref[u, h] = sh
                    y_ref[rows, sl] = (jnp.dot(a.astype(BF16), vb, preferred_element_type=F32)
                                       + jnp.dot((qh * tb["qdec"]).astype(BF16), sh.astype(BF16), preferred_element_type=F32))
                    sh = tb["gam"] * sh + jnp.dot((kh * tb["kdec"]).T.astype(BF16), vb, preferred_element_type=F32)
                state[h] = sh

    hmat = (RET_HEADS, RET_HEAD_DIM, RET_HEAD_DIM)
    in_specs, out_specs, args = [pl.BlockSpec(memory_space=pltpu.SMEM)], [], [logits]
    for d in range(2):
        rows = pl.BlockSpec((c * sub, RET_W), lambda n, d=d: (block[d](n), 0))
        in_specs += [rows, rows] + [pl.BlockSpec(sp.block_shape, lambda n, d=d, cb=sp.index_map(0)[1]: (block[d](n), cb)) for _, sp in vwin]
        args += [q, k] + [a for a, _ in vwin]
        out_specs += [rows, pl.BlockSpec((sub,) + hmat, lambda n, d=d: (block[d](n), 0, 0, 0))]
    return pl.pallas_call(
        body, name="ret_fwd", grid=(nb,), in_specs=in_specs, out_specs=out_specs,
        out_shape=[jax.ShapeDtypeStruct((s, RET_W), F32), jax.ShapeDtypeStruct((nb * sub,) + hmat, F32)] * 2,
        scratch_shapes=[pltpu.VMEM(hmat, F32)] * 2,
        compiler_params=_cparams(("arbitrary",)),
    )(*args)


def _ret_bwd(logits, q, k, proj, dy, st_f, st_b):
    s = q.shape[0]
    c = RET_CHUNK
    sub = _ret_sub(s // c)
    nb = s // (c * sub)
    block = (lambda n: nb - 1 - n, lambda n: n)
    order = (tuple(reversed(range(sub))), tuple(range(sub)))
    vwin = _win(proj, c * sub, C_RV, RET_W)
    nv = len(vwin)
    vw = RET_W // nv
    per = 4 + nv

    def body(lg_ref, *refs):
        ins, outs, scr = refs[:2 * per], refs[2 * per:2 * per + 8], refs[2 * per + 8:]
        n = pl.program_id(0)

        @pl.when(n == 0)
        def _():
            for r in scr:
                r[...] = jnp.zeros(r.shape, F32)

        for h in range(RET_HEADS):
            for d in range(2):
                q_ref, k_ref, dy_ref, st_ref = ins[d * per:d * per + 4]
                v_refs = ins[d * per + 4:(d + 1) * per]
                dq_ref, dk_ref, dv_ref = outs[4 * d:4 * d + 3]
                dstate, lacc = scr[2 * d], scr[2 * d + 1]
                tb = _decay_tables(lg_ref[d, h], bool(d))
                sl = slice(h * RET_HEAD_DIM, (h + 1) * RET_HEAD_DIM)
                off = h * RET_HEAD_DIM
                dsh = dstate[h]
                lsum = lacc[h]
                for u in order[d]:
                    rows = slice(u * c, (u + 1) * c)
                    qh, kh, dyh = q_ref[rows, sl], k_ref[rows, sl], dy_ref[rows, sl]
                    vb = v_refs[off // vw][rows, off % vw:off % vw + RET_HEAD_DIM].astype(BF16)
                    qb, kb, dyb = qh.astype(BF16), kh.astype(BF16), dyh.astype(BF16)
                    sh = st_ref[u, h]
                    shb, dshb = sh.astype(BF16), dsh.astype(BF16)
                    qk = _nt(qb, kb)
                    g = _nt(dyb, vb) * tb["d"]
                    a_t = _nt(kb, qb) * tb["d_t"]
                    g_t = _nt(vb, dyb) * tb["d_t"]
                    qd, kd = qh * tb["qdec"], kh * tb["kdec"]
                    dqd = _nt(dyb, shb)
                    dkd = _nt(vb, dshb)
                    dq_ref[rows, sl] = (jnp.dot(g.astype(BF16), kb, preferred_element_type=F32) + dqd * tb["qdec"]).astype(dq_ref.dtype)
                    dk_ref[rows, sl] = (jnp.dot(g_t.astype(BF16), qb, preferred_element_type=F32) + dkd * tb["kdec"]).astype(dk_ref.dtype)
                    dv_ref[rows, sl] = (jnp.dot(a_t.astype(BF16), dyb, preferred_element_type=F32)
                                        + jnp.dot(kd.astype(BF16), dshb, preferred_element_type=F32)).astype(dv_ref.dtype)
                    lsum = lsum + (tb["dist"] * qk * g + tb["e_q"] * qd * dqd + tb["e_k"] * kd * dkd
                                   + float(c) * tb["gam"] * dsh * sh)
                    dsh = tb["gam"] * dsh + jnp.dot(qd.T.astype(BF16), dyb, preferred_element_type=F32)
                dstate[h] = dsh
                lacc[h] = lsum

        @pl.when(n == nb - 1)
        def _():
            for d in range(2):
                for h in range(RET_HEADS):
                    outs[4 * d + 3][h] = jnp.zeros((8, LANES), F32) + jnp.sum(scr[2 * d + 1][h])

    hmat = (RET_HEADS, RET_HEAD_DIM, RET_HEAD_DIM)
    in_specs, out_specs, args = [pl.BlockSpec(memory_space=pltpu.SMEM)], [], [logits]
    for d, states in enumerate((st_f, st_b)):
        rows = pl.BlockSpec((c * sub, RET_W), lambda n, d=d: (block[d](n), 0))
        in_specs += ([rows, rows, rows, pl.BlockSpec((sub,) + hmat, lambda n, d=d: (block[d](n), 0, 0, 0))]
                     + [pl.BlockSpec(sp.block_shape, lambda n, d=d, cb=sp.index_map(0)[1]: (block[d](n), cb)) for _, sp in vwin])
        args += [q, k, dy, states] + [a for a, _ in vwin]
        out_specs += [rows, rows, rows, pl.BlockSpec((RET_HEADS, 8, LANES), lambda n: (0, 0, 0))]
    return pl.pallas_call(
        body, name="ret_bwd", grid=(nb,), in_specs=in_specs, out_specs=out_specs,
        out_shape=([jax.ShapeDtypeStruct((s, RET_W), BF16)] * 3 + [jax.ShapeDtypeStruct((RET_HEADS, 8, LANES), F32)]) * 2,
        scratch_shapes=[pltpu.VMEM(hmat, F32)] * 4,
        compiler_params=_cparams(("arbitrary",)),
    )(*args)


def _local_step(x, p, target, w, small):
    s = x.shape[0]
    tabs = _rope_tables(s, ATTN_HEAD_DIM) + _rope_tables(s, RET_HEAD_DIM)
    g_mix, g_mlp, g_ple = small["mix_norm"][None, :], small["mlp_norm"][None, :], small["ple_norm"][None, :]
    g_final, g_ret = small["final_norm"][None, :], small["ret_norm_gain"][None, :]
    gq_w = jnp.tile(small["attn_q_norm"], ATTN_HEADS)[None, :]
    gk_w = jnp.tile(small["attn_k_norm"], ATTN_KV_HEADS)[None, :]
    logits = small["ret_decay_logit"]

    hb = _stage_norm_in(x, g_mix)
    proj = _mm("in_proj", hb, w["w_in"], tm=512, tn=IN_W // 2, tk=1024, out_dtypes=(BF16,), j_outer=True)
    q_ct, k_rows, k_ct, v_rows, v_ct, rq, rk = _stage_qkv(proj, tabs, gq_w, gk_w)
    o_ct, lse = _attn_fwd(q_ct, k_rows, v_ct)
    ry_f, st_f, ry_b, st_b = _ret_fwd(logits, rq, rk, proj)
    rz, attn_rows = _stage_mix_post(ry_f, ry_b, proj, o_ct, g_ret)
    a_out = _mm("attn_o", attn_rows, w["w_attn_o"], tm=1024, tn=1024, tk=512, out_dtypes=(BF16,))
    n_gate = D_MODEL // EPI_PIECE

    def epi_merge(acc, e, c):
        ga, gr = _cat(e[1:1 + n_gate]), _cat(e[1 + n_gate:1 + 2 * n_gate])
        return acc, _sigmoid(ga) * e[0][...] + _sigmoid(gr) * acc

    r_out, merged = _mm("ret_o", rz, w["w_ret_o"], tm=512, tn=1024, tk=512, out_dtypes=(BF16, BF16), epi=epi_merge,
                        epi_ins=(a_out, (proj, C_GA), (proj, C_GR)))

    def epi_res_norm(acc, e, c):
        xr = e[0][...] + acc
        return xr, _rms_fwd(xr, c[0][...])

    x1, hm = _mm("out_proj", merged, w["w_out"], tm=512, tn=1024, tk=1024, out_dtypes=(F32, BF16),
                 epi=epi_res_norm, epi_ins=(x,), consts=(g_mlp,))

    def epi_relu2(acc, e, c):
        r = jnp.maximum(acc, 0.0)
        return (r * r,)

    act = _mm("mlp_up", hm, w["w_up"], tm=512, tn=2048, tk=1024, out_dtypes=(BF16,), epi=epi_relu2, j_outer=True)
    x2, hp = _mm("mlp_down", act, w["w_down"], tm=512, tn=1024, tk=D_FF, out_dtypes=(F32, BF16),
                 epi=epi_res_norm, epi_ins=(x1,), consts=(g_ple,))
    pe = _mm("ple_emb", p, w["w_ple"], tm=1024, tn=1024, tk=256)

    def epi_head(acc, e, c):
        gt = _sigmoid(acc)
        pe_, gf = e[0][...], c[0][...]
        x3 = e[1][...] + gt * pe_
        r3 = lax.rsqrt(jnp.mean(x3 * x3, axis=-1, keepdims=True) + NORM_EPS)
        x3n = x3 * r3
        err = x3n * gf - e[2][...]
        dy = err * (1.0 / D_MODEL)
        dyg = dy * gf
        dx3 = r3 * (dyg - x3n * jnp.mean(dyg * x3n, axis=-1, keepdims=True))
        return dx3, dx3 * pe_ * gt * (1.0 - gt), dx3 * gt, err * err, dy * x3n

    dx3, dzg, dpe, loss_cols, g_final_p = _mm("ple_gate", hp, w["w_ple_gate"], tm=512, tn=1024, tk=1024, out_dtypes=(F32, BF16, BF16),
                                              epi=epi_head, epi_ins=(pe, x2, target), consts=(g_final,), n_sums=2)
    loss_sum = 0.5 / D_MODEL * jnp.sum(loss_cols)

    gw = {}
    gw["w_ple"] = _mm("g_w_ple", p, dpe, ta=True, tm=256, tn=1024, tk=2048)
    gw["w_ple_gate"] = _mm("g_w_ple_gate", hp, dzg, ta=True, tm=1024, tn=1024, tk=2048)
    def epi_norm_bwd(acc, e, c):
        dx, dg = _rms_bwd(acc, e[0][...], c[0][...])
        return e[1][...] + dx, dg

    def epi_norm_bwd_b(acc, e, c):
        tot, dg = epi_norm_bwd(acc, e, c)
        return tot, tot, dg

    dx2, dx2_b, g_ple_p = _mm("d_hp", dzg, w["w_ple_gate"], tb=True, tm=512, tn=1024, tk=1024, out_dtypes=(F32, BF16),
                              epi=epi_norm_bwd_b, epi_ins=(x2, dx3), consts=(g_ple,), n_sums=1)

    def epi_relu2_bwd(acc, e, c):
        return (acc * (2.0 * jnp.sqrt(e[0][...].astype(F32))),)

    du = _mm("d_u", dx2_b, w["w_down"], tb=True, tm=512, tn=2048, tk=1024, out_dtypes=(BF16,), epi=epi_relu2_bwd, epi_ins=(act,),
             j_outer=True)
    gw["w_down"] = _mm("g_w_down", act, dx2_b, ta=True, tm=1024, tn=1024, tk=2048)
    gw["w_up"] = _mm("g_w_up", hm, du, ta=True, tm=1024, tn=1024, tk=2048)
    dx1, dx1_b, g_mlp_p = _mm("d_hm", du, w["w_up"], tb=True, tm=512, tn=1024, tk=D_FF, out_dtypes=(F32, BF16),
                              epi=epi_norm_bwd_b, epi_ins=(x1, dx2), consts=(g_mlp,), n_sums=1)
    def epi_merge_bwd(acc, e, c):
        sa, sr = _sigmoid(_cat(e[2:2 + n_gate])), _sigmoid(_cat(e[2 + n_gate:2 + 2 * n_gate]))
        return acc * sa, acc * sr, acc * e[0][...] * sa * (1.0 - sa), acc * e[1][...] * sr * (1.0 - sr)

    dao, dro, dga, dgr = _mm("d_merged", dx1_b, w["w_out"], tb=True, tm=512, tn=1024, tk=1024, out_dtypes=(BF16,) * 4,
                             epi=epi_merge_bwd, epi_ins=(a_out, r_out, (proj, C_GA), (proj, C_GR)))
    gw["w_out"] = _mm("g_w_out", merged, dx1_b, ta=True, tm=1024, tn=1024, tk=2048)
    gw["w_attn_o"] = _mm("g_w_attn_o", attn_rows, dao, ta=True, tm=512, tn=1024, tk=2048)
    gw["w_ret_o"] = _mm("g_w_ret_o", rz, dro, ta=True, tm=512, tn=1024, tk=2048)
    dattn = _mm("d_attn", dao, w["w_attn_o"], tb=True, tm=1024, tn=512, tk=1024)
    drz = _mm("d_rz", dro, w["w_ret_o"], tb=True, tm=1024, tn=512, tk=1024)
    do_ct, delta, dry, drg, g_ret_p = _stage_mix_post_bwd(dattn, attn_rows, drz, ry_f, ry_b, proj, g_ret)
    dq_f, dk_f, dv_f, dl_f, dq_b, dk_b, dv_b, dl_b = _ret_bwd(logits, rq, rk, proj, dry, st_f, st_b)
    dq_ct, dk8, dv8 = _attn_bwd(q_ct, do_ct, lse, delta, k_rows, v_rows, k_ct)
    dproj, gq_p, gk_p = _stage_dproj(proj, dq_ct, dk8, dv8, (dq_f, dk_f, dv_f, dq_b, dk_b, dv_b), drg, dga, dgr, tabs, gq_w, gk_w)
    gw["w_in"] = _mm("g_w_in", hb, dproj, ta=True, tm=1024, tn=IN_W // 2, tk=1024)
    grad_x, g_mix_p = _mm("d_h", dproj, w["w_in"], tb=True, tm=512, tn=1024, tk=IN_W, epi=epi_norm_bwd, epi_ins=(x, dx1),
                          consts=(g_mix,), n_sums=1)

    gs = {
        "mix_norm": g_mix_p[0], "mlp_norm": g_mlp_p[0], "ple_norm": g_ple_p[0], "final_norm": g_final_p[0],
        "ret_norm_gain": g_ret_p[0],
        "attn_q_norm": jnp.sum(gq_p[0].reshape(ATTN_HEADS, ATTN_HEAD_DIM), axis=0),
        "attn_k_norm": jnp.sum(gk_p[0].reshape(ATTN_KV_HEADS, ATTN_HEAD_DIM), axis=0),
        "ret_decay_logit": jnp.stack([dl_f[:, 0, 0], dl_b[:, 0, 0]]),
    }
    return loss_sum, grad_x, gw, gs


PACK_COLS = 1024
N_CHIPS = 4
HALF_ROWS = 2048


def _pack_shard(parts):
    return jnp.concatenate([parts[n].reshape(-1, PACK_COLS) for n, _ in BIG], axis=0)


def _unpack_shard(slab, shapes):
    out, r = {}, 0
    for n, _ in BIG:
        rows = math.prod(shapes[n]) // PACK_COLS
        out[n] = slab[r:r + rows].reshape(shapes[n])
        r += rows
    return out


def _shard_of(full, axis, sidx):
    size = full.shape[axis] // N_CHIPS
    return lax.slice_in_dim(full, sidx * size, (sidx + 1) * size, axis=axis)


def _position():
    x, y, c = lax.axis_index("x"), lax.axis_index("y"), lax.axis_index("c")
    return x, y, c


def _other_chips(x, y):
    return [(1 - x, y), (x, 1 - y), (1 - x, 1 - y)]


ANY = pl.BlockSpec(memory_space=pl.ANY)


def _gather_weights(slab):
    rows = slab.shape[0]
    half = rows // 2

    def body(in_ref, out_ref, send_sems, recv_sems):
        x, y, c = _position()
        chips = _other_chips(x, y)

        def piece(chip, core):
            return out_ref.at[2 * chip[0] + chip[1], pl.ds(core * half, half), :]

        def copy(k, chip, core, to, src=None):
            return pltpu.make_async_remote_copy(
                src_ref=piece(chip, core) if src is None else src, dst_ref=piece(chip, core),
                send_sem=send_sems.at[k], recv_sem=recv_sems.at[k], device_id=to, device_id_type=MESH)

        first = [copy(j, (x, y), c, (*chip, c), src=in_ref.at[pl.ds(c * half, half), :]) for j, chip in enumerate(chips)]
        for cp in first:
            cp.start()
        passed = [copy(3 + j, chip, c, (x, y, 1 - c)) for j, chip in enumerate(chips)]
        for j, chip in enumerate(chips):
            copy(j, chip, c, (x, y, c)).wait_recv()
            passed[j].start()
        for j, chip in enumerate(chips):
            copy(3 + j, chip, 1 - c, (x, y, c)).wait_recv()
        for cp in first + passed:
            cp.wait_send()

    return pl.pallas_call(
        body, name="gather_weights", in_specs=[ANY], out_specs=ANY,
        out_shape=jax.ShapeDtypeStruct((N_CHIPS,) + slab.shape, slab.dtype),
        scratch_shapes=[pltpu.SemaphoreType.DMA((6,)), pltpu.SemaphoreType.DMA((6,))],
    )(slab)


def _exchange_halves(g):
    def body(g_ref, out_ref, send_sem, recv_sem):
        x, y, c = _position()
        cp = pltpu.make_async_remote_copy(src_ref=g_ref.at[pl.ds(0, N_CHIPS), 1 - c], dst_ref=out_ref, send_sem=send_sem,
                                          recv_sem=recv_sem, device_id=(x, y, 1 - c), device_id_type=MESH)
        cp.start()
        cp.wait()

    return pl.pallas_call(
        body, name="exchange_halves", in_specs=[ANY], out_specs=ANY,
        out_shape=jax.ShapeDtypeStruct((N_CHIPS,) + g.shape[2:], g.dtype),
        scratch_shapes=[pltpu.SemaphoreType.DMA, pltpu.SemaphoreType.DMA],
    )(g)


def _add_my_half(g, r1, c_idx):
    tr = 256
    nt = g.shape[2] // tr
    out = (N_CHIPS,) + g.shape[2:]

    def body(c_ref, g_ref, r_ref, o_ref, ob_ref):
        tot = g_ref[...] + r_ref[...]
        o_ref[...] = tot
        ob_ref[...] = tot.astype(BF16)

    blk = (None, tr, PACK_COLS)
    spec = pl.BlockSpec(blk, lambda s, i, c_ref: (s, i, 0))
    return pl.pallas_call(
        body, name="add_my_half",
        grid_spec=pltpu.PrefetchScalarGridSpec(
            num_scalar_prefetch=1, grid=(N_CHIPS, nt),
            in_specs=[pl.BlockSpec((None,) + blk, lambda s, i, c_ref: (s, c_ref[0], i, 0)), spec],
            out_specs=[spec, spec]),
        out_shape=[jax.ShapeDtypeStruct(out, F32), jax.ShapeDtypeStruct(out, BF16)],
        compiler_params=_cparams(("parallel", "parallel")),
    )(c_idx, g, r1)


def _scatter_to_chips(part):
    def body(p_ref, out_ref, send_sems, recv_sems):
        x, y, c = _position()
        chips = _other_chips(x, y)
        sends = [pltpu.make_async_remote_copy(
            src_ref=p_ref.at[2 * chip[0] + chip[1]], dst_ref=out_ref.at[j], send_sem=send_sems.at[j], recv_sem=recv_sems.at[j],
            device_id=(*chip, c), device_id_type=MESH) for j, chip in enumerate(chips)]
        for cp in sends:
            cp.start()
        for cp in sends:
            cp.wait()

    return pl.pallas_call(
        body, name="scatter_to_chips", in_specs=[ANY], out_specs=ANY,
        out_shape=jax.ShapeDtypeStruct((N_CHIPS - 1,) + part.shape[1:], part.dtype),
        scratch_shapes=[pltpu.SemaphoreType.DMA((3,)), pltpu.SemaphoreType.DMA((3,))],
    )(part)


def _sum_chips(part, r2, chip_idx):
    tr = 256

    def body(c_ref, p_ref, r_ref, o_ref):
        o_ref[...] = ((p_ref[...] + r_ref[0]) + r_ref[1]) + r_ref[2]

    return pl.pallas_call(
        body, name="sum_chips",
        grid_spec=pltpu.PrefetchScalarGridSpec(
            num_scalar_prefetch=1, grid=(r2.shape[1] // tr,),
            in_specs=[pl.BlockSpec((None, tr, PACK_COLS), lambda i, c_ref: (c_ref[0], i, 0)),
                      pl.BlockSpec((N_CHIPS - 1, tr, PACK_COLS), lambda i, c_ref: (0, i, 0))],
            out_specs=pl.BlockSpec((tr, PACK_COLS), lambda i, c_ref: (i, 0))),
        out_shape=jax.ShapeDtypeStruct(r2.shape[1:], F32),
        compiler_params=_cparams(("parallel",)),
    )(chip_idx, part, r2)


def _join_halves(red):
    def body(r_ref, out_ref, send_sem, recv_sem):
        x, y, c = _position()
        cp = pltpu.make_async_remote_copy(src_ref=r_ref, dst_ref=out_ref, send_sem=send_sem, recv_sem=recv_sem,
                                          device_id=(x, y, 1 - c), device_id_type=MESH)
        cp.start()
        cp.wait()

    return pl.pallas_call(
        body, name="join_halves", in_specs=[ANY], out_specs=ANY,
        out_shape=jax.ShapeDtypeStruct(red.shape, red.dtype),
        scratch_shapes=[pltpu.SemaphoreType.DMA, pltpu.SemaphoreType.DMA],
    )(red)


def _adamw_math(w, g, m, v):
    m = ADAM_B1 * m + (1.0 - ADAM_B1) * g
    v = ADAM_B2 * v + (1.0 - ADAM_B2) * (g * g)
    m_hat = m / (1.0 - ADAM_B1 ** ADAM_STEP)
    v_hat = v / (1.0 - ADAM_B2 ** ADAM_STEP)
    delta = -ADAM_LR * (m_hat / (jnp.sqrt(v_hat) + ADAM_EPS) + ADAM_WD * w)
    return delta, m, v


def _adamw(name, w, g, m, v):
    tr = min(256, w.shape[0])

    def body(w_ref, g_ref, m_ref, v_ref, d_ref, nm_ref, nv_ref):
        d_ref[...], nm_ref[...], nv_ref[...] = _adamw_math(w_ref[...], g_ref[...], m_ref[...], v_ref[...])

    blk = pl.BlockSpec((tr, w.shape[1]), lambda i: (i, 0))
    return pl.pallas_call(
        body, name="adamw_" + name, grid=(w.shape[0] // tr,), in_specs=[blk] * 4, out_specs=[blk] * 3,
        out_shape=[jax.ShapeDtypeStruct(w.shape, F32)] * 3, compiler_params=_cparams(("parallel",)),
    )(w, g, m, v)


def _small_step(gpk, wpk, mpk, vpk):
    row, col, width = SMALL["ret_decay_logit"]

    def body(g_ref, w_ref, m_ref, v_ref, og_ref, od_ref, om_ref, ov_ref, gbuf, send_sems, recv_sems):
        x, y, c = _position()
        me = 4 * x + 2 * y + c
        gbuf[me] = g_ref[...]
        sends = []
        for k in range(1, 8):
            to = (x ^ (k >> 2), y ^ ((k >> 1) & 1), c ^ (k & 1))
            cp = pltpu.make_async_remote_copy(src_ref=g_ref, dst_ref=gbuf.at[me], send_sem=send_sems.at[k - 1],
                                              recv_sem=recv_sems.at[k - 1], device_id=to, device_id_type=MESH)
            cp.start()
            sends.append(cp)
        for k in range(1, 8):
            frm = me ^ k
            pltpu.make_async_remote_copy(src_ref=g_ref, dst_ref=gbuf.at[frm], send_sem=send_sems.at[k - 1],
                                         recv_sem=recv_sems.at[k - 1], device_id=(x, y, c), device_id_type=MESH).wait_recv()
        for cp in sends:
            cp.wait_send()
        tot = gbuf[0]
        for d in range(1, 8):
            tot = tot + gbuf[d]
        w = w_ref[...]
        r_i = lax.broadcasted_iota(jnp.int32, w.shape, 0)
        c_i = lax.broadcasted_iota(jnp.int32, w.shape, 1)
        is_logit = (r_i == row) & (c_i >= col) & (c_i < col + width)
        g = jnp.where(is_logit, tot * _sigmoid(-w), tot)
        og_ref[...] = g
        od_ref[...], om_ref[...], ov_ref[...] = _adamw_math(w, g, m_ref[...], v_ref[...])

    vm = pl.BlockSpec(memory_space=pltpu.VMEM)
    shp = jax.ShapeDtypeStruct(gpk.shape, F32)
    return pl.pallas_call(
        body, name="small_step", in_specs=[vm] * 4, out_specs=[vm] * 4, out_shape=[shp] * 4,
        scratch_shapes=[pltpu.VMEM((8,) + gpk.shape, F32), pltpu.SemaphoreType.DMA((7,)), pltpu.SemaphoreType.DMA((7,))],
    )(gpk, wpk, mpk, vpk)


def _pack_small(parts):
    rows = [[] for _ in range(SMALL_ROWS)]
    for n, (r, col, width) in sorted(SMALL.items(), key=lambda kv: (kv[1][0], kv[1][1])):
        rows[r].append((col, parts[n].reshape(-1).astype(F32)))
    out = []
    for r in range(SMALL_ROWS):
        segs, pos = [], 0
        for col, vec in rows[r]:
            assert col == pos
            segs.append(vec)
            pos += vec.shape[0]
        if pos < PACK_COLS:
            segs.append(jnp.zeros((PACK_COLS - pos,), F32))
        out.append(jnp.concatenate(segs))
    return jnp.stack(out)


def _unpack_small(pk, shapes):
    return {n: pk[r, col:col + width].reshape(shapes[n]) for n, (r, col, width) in SMALL.items()}


WEIGHTS = ("mix_norm", "w_in", "attn_q_norm", "attn_k_norm", "ret_decay_logit", "ret_norm_gain", "w_attn_o", "w_ret_o", "w_out",
           "mlp_norm", "w_up", "w_down", "ple_norm", "w_ple_gate", "w_ple", "final_norm")


def kernel(x, p, mix_norm, w_in, attn_q_norm, attn_k_norm, ret_decay_logit, ret_norm_gain, w_attn_o, w_ret_o, w_out, mlp_norm, w_up, w_down, ple_norm, w_ple_gate, w_ple, final_norm, loss_target, m_mix_norm, m_w_in, m_attn_q_norm, m_attn_k_norm, m_ret_decay_logit, m_ret_norm_gain, m_w_attn_o, m_w_ret_o, m_w_out, m_mlp_norm, m_w_up, m_w_down, m_ple_norm, m_w_ple_gate, m_w_ple, m_final_norm, v_mix_norm, v_w_in, v_attn_q_norm, v_attn_k_norm, v_ret_decay_logit, v_ret_norm_gain, v_w_attn_o, v_w_ret_o, v_w_out, v_mlp_norm, v_w_up, v_w_down, v_ple_norm, v_w_ple_gate, v_w_ple, v_final_norm):
    args = dict(locals())
    wts = {n: args[n] for n in WEIGHTS}
    ms = {n: args["m_" + n] for n in WEIGHTS}
    vs = {n: args["v_" + n] for n in WEIGHTS}
    shapes = {n: wts[n].shape for n in WEIGHTS}
    big_names = [n for n, _ in BIG]
    xi, yi, ci = _position()
    c_idx = ci.astype(jnp.int32).reshape(1)

    chip_idx = (2 * xi + yi).astype(jnp.int32)
    slab_b = _pack_shard({n: wts[n][0].astype(BF16) for n in big_names})
    gathered = lax.dynamic_update_slice(_gather_weights(slab_b), slab_b[None], (chip_idx, 0, 0))
    full, r0 = {}, 0
    for n, axis in BIG:
        shard = shapes[n][1:]
        rows = math.prod(shard) // PACK_COLS
        if axis == 0 and shard[1] == PACK_COLS:
            full[n] = gathered[:, r0:r0 + rows].reshape(N_CHIPS * shard[0], shard[1])
        else:
            full[n] = jnp.concatenate([gathered[k, r0:r0 + rows].reshape(shard) for k in range(N_CHIPS)], axis=axis)
        r0 += rows
    small = {n: wts[n].reshape(wts[n].shape[1:] if wts[n].ndim > 1 else wts[n].shape) for n in SMALL}

    loss_part, grad_x, gw, gs = _local_step(x[0], p[0, 0], loss_target[0], full, small)
    loss = lax.psum(loss_part, ("x", "y", "c"))

    slabs = jnp.stack([_pack_shard({n: _shard_of(gw[n], axis, k) for n, axis in BIG}) for k in range(N_CHIPS)])
    halves = slabs.reshape(N_CHIPS, 2, HALF_ROWS, PACK_COLS)
    chip_part, chip_part_b = _add_my_half(halves, _exchange_halves(halves), c_idx)
    mine = _sum_chips(chip_part, _scatter_to_chips(chip_part_b), chip_idx.reshape(1))
    both = jnp.stack([mine, _join_halves(mine)])
    reduced = jnp.where(ci == 0, both, both[::-1]).reshape(2 * HALF_ROWS, PACK_COLS)
    g_big = _unpack_shard(reduced, {n: shapes[n][1:] for n in big_names})
    big_out = [{}, {}, {}, {}]
    for n in big_names:
        big_out[0][n] = g_big[n][None]
        for kind, a in enumerate(_adamw(n, wts[n][0], g_big[n], ms[n][0], vs[n][0])):
            big_out[kind + 1][n] = a[None]

    sm_out = _small_step(_pack_small(gs), _pack_small({n: wts[n] for n in SMALL}), _pack_small({n: ms[n] for n in SMALL}),
                         _pack_small({n: vs[n] for n in SMALL}))
    small_out = [_unpack_small(a, {n: shapes[n] for n in SMALL}) for a in sm_out]

    outs = [loss, grad_x[None]]
    for kind in range(4):
        for n in WEIGHTS:
            outs.append(small_out[kind][n] if n in SMALL else big_out[kind][n])
    return tuple(outs)
```

```python
import math

import jax
import jax.numpy as jnp
from jax import lax
from jax.experimental import pallas as pl
from jax.experimental.pallas import tpu as pltpu

F32 = jnp.float32
BF16 = jnp.bfloat16
MESH = pl.DeviceIdType.MESH

D_MODEL = 1024
GRID_W = 64
ATTN_HEAD_DIM = 64
ATTN_HEADS = 8
ATTN_KV_HEADS = 2
ATTN_GROUP = ATTN_HEADS // ATTN_KV_HEADS
RET_HEAD_DIM = 128
RET_HEADS = 4
ATTN_Q_W = 512
ATTN_KV_W = 128
RET_W = 512
IN_W = 4864
D_FF = 4096
RET_CHUNK = 256
ROPE_THETA = 10000.0
NORM_EPS = 1e-6
GN_EPS = 1e-5
ATTN_SCALE = ATTN_HEAD_DIM ** -0.5
LOG2E = math.log2(math.e)
Q_FOLD = ATTN_SCALE * LOG2E
RET_SCALE = RET_HEAD_DIM ** -0.5

C_AQ, C_AK, C_AV, C_RQ, C_RK, C_RV, C_RG, C_GA, C_GR = 0, 512, 640, 768, 1280, 1792, 2304, 2816, 3840

ADAM_LR = 0.001
ADAM_B1 = 0.9
ADAM_B2 = 0.999
ADAM_EPS = 1e-08
ADAM_WD = 0.01
ADAM_STEP = 10

LANES = 128
VMEM_LIMIT = 56 << 20
SEQ_TILE = 512
EPI_PIECE = 256

BIG = (("w_in", 1), ("w_attn_o", 1), ("w_ret_o", 1), ("w_out", 0), ("w_up", 1), ("w_down", 0), ("w_ple_gate", 0), ("w_ple", 1))
SMALL_ROWS = 8
SMALL = {"mix_norm": (0, 0, 1024), "mlp_norm": (1, 0, 1024), "ple_norm": (2, 0, 1024), "final_norm": (3, 0, 1024),
         "ret_norm_gain": (4, 0, 512), "attn_q_norm": (4, 512, 64), "attn_k_norm": (4, 576, 64), "ret_decay_logit": (4, 640, 8)}


def _seq_tile(s):
    return min(SEQ_TILE, s // 2)


def _cparams(sem=None, vmem=VMEM_LIMIT):
    return pltpu.CompilerParams(dimension_semantics=sem, vmem_limit_bytes=vmem)


def _mm(name, a, b, *, ta=False, tb=False, tm, tn, tk, out_dtypes=(F32,), epi=None, epi_ins=(), consts=(), n_sums=0, j_outer=False):
    if ta:
        kdim, m = a.shape
    else:
        m, kdim = a.shape
    n = b.shape[0] if tb else b.shape[1]
    tm, tn, tk = min(tm, m), min(tn, n), min(tk, kdim)
    assert m % tm == 0 and n % tn == 0 and kdim % tk == 0, (name, m, n, kdim, tm, tn, tk)
    nk = kdim // tk
    e_arrs, e_cols = [], []
    for item in epi_ins:
        if isinstance(item, tuple):
            arr, start = item
            assert tn == n and start % EPI_PIECE == 0 and n % EPI_PIECE == 0
            for piece in range(n // EPI_PIECE):
                e_arrs.append(arr)
                e_cols.append(start // EPI_PIECE + piece)
        else:
            e_arrs.append(item)
            e_cols.append(None)
    n_e, n_c, n_o = len(e_arrs), len(consts), len(out_dtypes)
    assert n_sums == 0 or tn == n

    def body(*refs):
        a_ref, b_ref = refs[0], refs[1]
        e_refs = refs[2:2 + n_e]
        c_refs = refs[2 + n_e:2 + n_e + n_c]
        o_refs = refs[2 + n_e + n_c:2 + n_e + n_c + n_o]
        s_refs = refs[2 + n_e + n_c + n_o:2 + n_e + n_c + n_o + n_sums]
        acc_ref = refs[2 + n_e + n_c + n_o + n_sums] if nk > 1 else None
        k = pl.program_id(2)
        if n_sums:
            @pl.when((pl.program_id(1 if j_outer else 0) == 0) & (k == 0))
            def _():
                for r in s_refs:
                    r[...] = jnp.zeros(r.shape, F32)
        av = a_ref[...].astype(BF16)
        bv = b_ref[...].astype(BF16)
        dims = (((0,) if ta else (1,), (1,) if tb else (0,)), ((), ()))
        part = lax.dot_general(av, bv, dims, preferred_element_type=F32)

        def finish(acc):
            vals = epi(acc, e_refs, c_refs) if epi is not None else (acc,)
            for o_ref, v in zip(o_refs, vals[:n_o]):
                o_ref[...] = v.astype(o_ref.dtype)
            for s_ref, v in zip(s_refs, vals[n_o:]):
                _acc_add(s_ref, v)

        if nk == 1:
            finish(part)
        else:
            @pl.when(k == 0)
            def _():
                acc_ref[...] = part

            @pl.when(k > 0)
            def _():
                acc_ref[...] += part

            @pl.when(k == nk - 1)
            def _():
                finish(acc_ref[...])

    def spec(shape, index):
        return pl.BlockSpec(shape, (lambda j, i, k: index(i, j, k)) if j_outer else index)

    a_spec = spec((tk, tm), lambda i, j, k: (k, i)) if ta else spec((tm, tk), lambda i, j, k: (i, k))
    b_spec = spec((tn, tk), lambda i, j, k: (j, k)) if tb else spec((tk, tn), lambda i, j, k: (k, j))
    o_spec = spec((tm, tn), lambda i, j, k: (i, j))
    c_specs = [spec(c.shape, lambda i, j, k, nd=c.ndim: (0,) * nd) for c in consts]
    outs = pl.pallas_call(
        body, name=name,
        grid=(n // tn, m // tm, nk) if j_outer else (m // tm, n // tn, nk),
        in_specs=([a_spec, b_spec]
                  + [o_spec if cb is None else spec((tm, EPI_PIECE), lambda i, j, k, cb=cb: (i, cb)) for cb in e_cols] + c_specs),
        out_specs=[o_spec] * n_o + [spec((8, n), lambda i, j, k: (0, 0))] * n_sums,
        out_shape=[jax.ShapeDtypeStruct((m, n), dt) for dt in out_dtypes] + [jax.ShapeDtypeStruct((8, n), F32)] * n_sums,
        scratch_shapes=[pltpu.VMEM((tm, tn), F32)] if nk > 1 else [],
        compiler_params=_cparams(("arbitrary",) * 3 if n_sums else ("parallel", "parallel", "arbitrary")),
    )(a, b, *e_arrs, *consts)
    return outs[0] if n_o + n_sums == 1 else outs


def _rows(arr, tr):
    return (arr, pl.BlockSpec((tr, arr.shape[1]), lambda i: (i, 0)))


def _win(arr, tr, start, width):
    bw = math.gcd(start, width) if start else width
    assert bw % LANES == 0
    return [(arr, pl.BlockSpec((tr, bw), lambda i, cb=start // bw + p: (i, cb))) for p in range(width // bw)]


def _ct(arr):
    return (arr, pl.BlockSpec((None,) + arr.shape[1:], lambda i: (i, 0, 0)))


def _whole(arr):
    return (arr, pl.BlockSpec(arr.shape, lambda i, nd=arr.ndim: (0,) * nd))


def _cat(refs):
    vals = [r[...].astype(F32) for r in refs]
    return vals[0] if len(vals) == 1 else jnp.concatenate(vals, axis=1)


def _seqtiled(name, fn, n_tiles, ins, outs, acc_widths=()):
    n_i, n_o, n_a = len(ins), len(outs), len(acc_widths)

    def body(*refs):
        i_refs, o_refs, a_refs = refs[:n_i], refs[n_i:n_i + n_o], refs[n_i + n_o:]
        if n_a:
            @pl.when(pl.program_id(0) == 0)
            def _():
                for r in a_refs:
                    r[...] = jnp.zeros(r.shape, F32)
        fn(list(i_refs), list(o_refs), list(a_refs))

    res = pl.pallas_call(
        body, name=name, grid=(n_tiles,),
        in_specs=[s for _, s in ins],
        out_specs=[s for _, _, s in outs] + [pl.BlockSpec((8, w), lambda i: (0, 0)) for w in acc_widths],
        out_shape=[jax.ShapeDtypeStruct(sh, dt) for sh, dt, _ in outs] + [jax.ShapeDtypeStruct((8, w), F32) for w in acc_widths],
        compiler_params=_cparams(("arbitrary",)),
    )(*[a for a, _ in ins])
    return res


def _acc_add(acc_ref, val):
    acc_ref[0:1, :] += jnp.sum(val, axis=0, keepdims=True)


def _out_rows(s, w, dt, tr):
    return ((s, w), dt, pl.BlockSpec((tr, w), lambda i: (i, 0)))


def _out_ct(s, w, dt, t):
    return ((s // t, w, t), dt, pl.BlockSpec((None, w, t), lambda i: (i, 0, 0)))


def _rms_fwd(x, gain):
    r = lax.rsqrt(jnp.mean(x * x, axis=-1, keepdims=True) + NORM_EPS)
    return x * r * gain


def _rms_bwd(dy, x, gain):
    r = lax.rsqrt(jnp.mean(x * x, axis=-1, keepdims=True) + NORM_EPS)
    xn = x * r
    dyg = dy * gain
    dx = r * (dyg - xn * jnp.mean(dyg * xn, axis=-1, keepdims=True))
    return dx, dy * xn


def _seg_mean(y, hd):
    w = y.shape[1]
    pieces = []
    for s in range(0, w, LANES):
        v = y[:, s:s + LANES]
        tot = jnp.sum(v, axis=1, keepdims=True)
        if hd == LANES:
            pieces.append(jnp.broadcast_to(tot, v.shape))
        else:
            low = lax.broadcasted_iota(jnp.int32, v.shape, 1) < hd
            lo = jnp.sum(jnp.where(low, v, 0.0), axis=1, keepdims=True)
            pieces.append(jnp.where(low, lo, tot - lo))
    out = pieces[0] if len(pieces) == 1 else jnp.concatenate(pieces, axis=1)
    return out * (1.0 / hd)


def _tile_lanes(t, w):
    return t if w == t.shape[1] else jnp.concatenate([t] * (w // t.shape[1]), axis=1)


def _swap_halves(x, hd):
    w = x.shape[1]
    half = hd // 2
    lane = lax.broadcasted_iota(jnp.int32, x.shape, 1)
    return jnp.where((lane % hd) < half, pltpu.roll(x, w - half, 1), pltpu.roll(x, half, 1))


def _rope(x, cos, sin_signed, hd):
    w = x.shape[1]
    return x * _tile_lanes(cos, w) + _swap_halves(x, hd) * _tile_lanes(sin_signed, w)


def _rope_t(dy, cos, sin_signed, hd):
    w = dy.shape[1]
    return dy * _tile_lanes(cos, w) + _swap_halves(dy * _tile_lanes(sin_signed, w), hd)


def _headnorm_fwd(x, gain_w, hd):
    r = lax.rsqrt(_seg_mean(x * x, hd) + NORM_EPS)
    return x * r * gain_w


def _headnorm_bwd(dy, x, gain_w, hd):
    r = lax.rsqrt(_seg_mean(x * x, hd) + NORM_EPS)
    xn = x * r
    dyg = dy * gain_w
    return r * (dyg - xn * _seg_mean(dyg * xn, hd)), dy * xn


def _sigmoid(x):
    return 1.0 / (1.0 + jnp.exp(-x))


def _rope_tables(seq_len, head_dim):
    rows = seq_len // GRID_W
    n_axis = head_dim // 4
    freqs = ROPE_THETA ** (-jnp.arange(n_axis, dtype=F32) / n_axis)
    ang_r = jnp.arange(rows, dtype=F32)[:, None] * freqs
    ang_c = jnp.arange(GRID_W, dtype=F32)[:, None] * freqs

    def expand(by_row, by_col):
        r = jnp.broadcast_to(by_row[:, None, :], (rows, GRID_W, n_axis))
        c = jnp.broadcast_to(by_col[None, :, :], (rows, GRID_W, n_axis))
        return jnp.concatenate([r, c], axis=-1).reshape(seq_len, 2 * n_axis)

    cos, sin = expand(jnp.cos(ang_r), jnp.cos(ang_c)), expand(jnp.sin(ang_r), jnp.sin(ang_c))
    reps = LANES // head_dim
    return jnp.tile(jnp.concatenate([cos, cos], axis=-1), (1, reps)), jnp.tile(jnp.concatenate([-sin, sin], axis=-1), (1, reps))


def _stage_norm_in(x, gain):
    s = x.shape[0]
    tr = min(SEQ_TILE, s)

    def fn(i, o, a):
        o[0][...] = _rms_fwd(i[0][...], i[1][...]).astype(BF16)

    return _seqtiled("norm_in", fn, s // tr, [_rows(x, tr), _whole(gain)], [_out_rows(s, D_MODEL, BF16, tr)])[0]


def _stage_qkv(proj, tabs, gq_w, gk_w):
    s = proj.shape[0]
    t = _seq_tile(s)
    ca, sa, cr, sr = tabs
    ins = (_win(proj, t, C_AQ, ATTN_Q_W) + _win(proj, t, C_AK, ATTN_KV_W) + _win(proj, t, C_AV, ATTN_KV_W)
           + _win(proj, t, C_RQ, RET_W) + _win(proj, t, C_RK, RET_W)
           + [_rows(ca, t), _rows(sa, t), _rows(cr, t), _rows(sr, t), _whole(gq_w), _whole(gk_w)])

    def fn(i, o, a):
        aq, ak, av = (i[n][...].astype(F32) for n in range(3))
        rq, rk = _cat(i[3:5]), _cat(i[5:7])
        ca_, sa_, cr_, sr_ = i[7][...], i[8][...], i[9][...], i[10][...]
        qr = _rope(_headnorm_fwd(aq, i[11][...], ATTN_HEAD_DIM), ca_, sa_, ATTN_HEAD_DIM) * Q_FOLD
        kr = _rope(_headnorm_fwd(ak, i[12][...], ATTN_HEAD_DIM), ca_, sa_, ATTN_HEAD_DIM)
        qt = qr.T.astype(BF16)
        zeros = jnp.zeros((ATTN_HEAD_DIM, t), BF16)
        for h in range(ATTN_HEADS):
            g = h // ATTN_GROUP
            blk = qt[h * ATTN_HEAD_DIM:(h + 1) * ATTN_HEAD_DIM, :]
            o[0][h * LANES + g * ATTN_HEAD_DIM:h * LANES + (g + 1) * ATTN_HEAD_DIM, :] = blk
            o[0][h * LANES + (1 - g) * ATTN_HEAD_DIM:h * LANES + (2 - g) * ATTN_HEAD_DIM, :] = zeros
        o[1][...] = kr.astype(BF16)
        o[2][...] = kr.T.astype(BF16)
        o[3][...] = av.astype(BF16)
        o[4][...] = av.T.astype(BF16)
        o[5][...] = _rope(rq, cr_, sr_, RET_HEAD_DIM) * RET_SCALE
        o[6][...] = _rope(rk, cr_, sr_, RET_HEAD_DIM)

    outs = [_out_ct(s, ATTN_HEADS * LANES, BF16, t), _out_rows(s, ATTN_KV_W, BF16, t), _out_ct(s, ATTN_KV_W, BF16, t),
            _out_rows(s, ATTN_KV_W, BF16, t), _out_ct(s, ATTN_KV_W, BF16, t), _out_rows(s, RET_W, F32, t), _out_rows(s, RET_W, F32, t)]
    return _seqtiled("qkv_prep", fn, s // t, ins, outs)


def _groupnorm_gate(ry, rg, gain):
    mu = _seg_mean(ry, RET_HEAD_DIM)
    d = ry - mu
    rs = lax.rsqrt(_seg_mean(d * d, RET_HEAD_DIM) + GN_EPS)
    return d * rs, rs, _sigmoid(rg)


def _stage_mix_post(ry_f, ry_b, proj, o_ct, gain):
    s = proj.shape[0]
    t = _seq_tile(s)
    ins = [_rows(ry_f, t), _rows(ry_b, t)] + _win(proj, t, C_RG, RET_W) + [_ct(o_ct), _whole(gain)]

    def fn(i, o, a):
        ry = i[0][...] + i[1][...]
        rg = _cat(i[2:4])
        gn, _, sg = _groupnorm_gate(ry, rg, None)
        o[0][...] = (gn * i[5][...] * (rg * sg)).astype(BF16)
        o[1][...] = i[4][...].astype(F32).T.astype(BF16)

    return _seqtiled("mix_post", fn, s // t, ins, [_out_rows(s, RET_W, BF16, t), _out_rows(s, ATTN_Q_W, BF16, t)])


def _stage_mix_post_bwd(dattn, attn_rows, drz, ry_f, ry_b, proj, gain):
    s = proj.shape[0]
    t = _seq_tile(s)
    ins = ([_rows(dattn, t), _rows(attn_rows, t), _rows(drz, t), _rows(ry_f, t), _rows(ry_b, t)]
           + _win(proj, t, C_RG, RET_W) + [_whole(gain)])

    def fn(i, o, a):
        da = i[0][...]
        dat = da.T
        prod_t = (da * i[1][...].astype(F32)).T
        dat_b = dat.astype(BF16)
        zeros = jnp.zeros((ATTN_HEAD_DIM, t), BF16)
        for h in range(ATTN_HEADS):
            g = h // ATTN_GROUP
            o[0][h * LANES + g * ATTN_HEAD_DIM:h * LANES + (g + 1) * ATTN_HEAD_DIM, :] = dat_b[h * ATTN_HEAD_DIM:(h + 1) * ATTN_HEAD_DIM, :]
            o[0][h * LANES + (1 - g) * ATTN_HEAD_DIM:h * LANES + (2 - g) * ATTN_HEAD_DIM, :] = zeros
            o[1][h] = jnp.sum(prod_t[h * ATTN_HEAD_DIM:(h + 1) * ATTN_HEAD_DIM, :], axis=0, keepdims=True)
        ry = i[3][...] + i[4][...]
        rg = _cat(i[5:7])
        gain_ = i[7][...]
        gn, rs, sg = _groupnorm_gate(ry, rg, None)
        dz = i[2][...]
        silu = rg * sg
        _acc_add(a[0], dz * gn * silu)
        dgn = dz * gain_ * silu
        o[2][...] = rs * (dgn - _seg_mean(dgn, RET_HEAD_DIM) - gn * _seg_mean(dgn * gn, RET_HEAD_DIM))
        o[3][...] = (dz * gn * gain_ * (sg * (1.0 + rg * (1.0 - sg)))).astype(BF16)

    outs = [_out_ct(s, ATTN_HEADS * LANES, BF16, t),
            ((ATTN_HEADS, s // t, 1, t), F32, pl.BlockSpec((ATTN_HEADS, None, 1, t), lambda i: (0, i, 0, 0))),
            _out_rows(s, RET_W, F32, t), _out_rows(s, RET_W, BF16, t)]
    return _seqtiled("mix_post_bwd", fn, s // t, ins, outs, acc_widths=(RET_W,))


def _stage_dproj(proj, dq_ct, dk8, dv8, rgrads, drg, dga, dgr, tabs, gq_w, gk_w):
    s = proj.shape[0]
    t = _seq_tile(s)
    ca, sa, cr, sr = tabs
    kv8 = pl.BlockSpec((ATTN_HEADS, t, ATTN_KV_W), lambda i: (0, i, 0))
    ins = (_win(proj, t, C_AQ, ATTN_Q_W) + _win(proj, t, C_AK, ATTN_KV_W) + [_ct(dq_ct), (dk8, kv8), (dv8, kv8)]
           + [_rows(g, t) for g in rgrads] + [_rows(drg, t), _rows(dga, t), _rows(dgr, t)]
           + [_rows(ca, t), _rows(sa, t), _rows(cr, t), _rows(sr, t), _whole(gq_w), _whole(gk_w)])

    def fn(i, o, a):
        aq, ak = i[0][...].astype(F32), i[1][...].astype(F32)
        dq_f, dk_f, dv_f, dq_b, dk_b, dv_b = (r[...].astype(F32) for r in i[5:11])
        ca_, sa_, cr_, sr_ = i[14][...], i[15][...], i[16][...], i[17][...]
        dqn = _rope_t(i[2][...].T * ATTN_SCALE, ca_, sa_, ATTN_HEAD_DIM)
        daq, gq_rows = _headnorm_bwd(dqn, aq, i[18][...], ATTN_HEAD_DIM)
        dkn = _rope_t(jnp.sum(i[3][...].astype(F32), axis=0) * (1.0 / LOG2E), ca_, sa_, ATTN_HEAD_DIM)
        dak, gk_rows = _headnorm_bwd(dkn, ak, i[19][...], ATTN_HEAD_DIM)
        _acc_add(a[0], gq_rows)
        _acc_add(a[1], gk_rows)
        out = o[0]
        out[:, C_AQ:C_AQ + ATTN_Q_W] = daq.astype(BF16)
        out[:, C_AK:C_AK + ATTN_KV_W] = dak.astype(BF16)
        out[:, C_AV:C_AV + ATTN_KV_W] = jnp.sum(i[4][...].astype(F32), axis=0).astype(BF16)
        out[:, C_RQ:C_RQ + RET_W] = _rope_t((dq_f + dq_b) * RET_SCALE, cr_, sr_, RET_HEAD_DIM).astype(BF16)
        out[:, C_RK:C_RK + RET_W] = _rope_t(dk_f + dk_b, cr_, sr_, RET_HEAD_DIM).astype(BF16)
        out[:, C_RV:C_RV + RET_W] = (dv_f + dv_b).astype(BF16)
        out[:, C_RG:C_RG + RET_W] = i[11][...]
        out[:, C_GA:C_GA + D_MODEL] = i[12][...]
        out[:, C_GR:C_GR + D_MODEL] = i[13][...]

    return _seqtiled("dproj", fn, s // t, ins, [_out_rows(s, IN_W, BF16, t)], acc_widths=(ATTN_Q_W, ATTN_KV_W))


def _attn_fwd(q_ct, k_rows, v_ct):
    nq, _, t = q_ct.shape
    s = nq * t
    nk = nq
    assert nk % 2 == 0
    n_ch = next(n for n in (8, 4, 2) if nq % n == 0)
    halves = 2 if t % (2 * LANES) == 0 else 1
    tq = t // halves
    n_par = n_ch * halves

    def body(q_ref, k_ref, v_ref, o_ref, lse_ref, *bufs):
        sbuf = tuple(bufs[2 * w:2 * w + 2] for w in range(n_par))
        pbuf = tuple(bufs[2 * n_par + 2 * w:2 * n_par + 2 * w + 2] for w in range(n_par))

        def where(w):
            return w // halves, slice((w % halves) * tq, (w % halves + 1) * tq)

        def scores(w, j, slot):
            kj = k_ref[pl.ds(pl.multiple_of(j * t, t), t), :]
            cw, lanes = where(w)
            st = jnp.dot(kj, q_ref[cw, :, lanes], preferred_element_type=F32)
            sbuf[w][slot][...] = st
            return jnp.max(st, axis=0, keepdims=True)

        def probs(w, slot, cmax, m, l):
            m_new = jnp.maximum(m, cmax)
            alpha = jnp.exp2(m - m_new)
            pt = jnp.exp2(sbuf[w][slot][...] - m_new)
            pbuf[w][slot][...] = pt.astype(BF16)
            return m_new, alpha * l + jnp.sum(pt, axis=0, keepdims=True), alpha

        def values(w, j, slot, alpha, acc):
            return alpha * acc + jnp.dot(v_ref[j], pbuf[w][slot][...], preferred_element_type=F32)

        init = []
        for w in range(n_par):
            m = jnp.full((1, tq), -1e30, F32)
            l = jnp.zeros((1, tq), F32)
            cmax0 = scores(w, 0, 0)
            cmax1 = scores(w, 1, 1)
            m, l, alpha0 = probs(w, 0, cmax0, m, l)
            init.append((m, l, jnp.zeros((ATTN_HEAD_DIM, tq), F32), cmax1, alpha0))

        def trip(n, carry):
            c = 2 * n
            out = []
            for w in range(n_par):
                m, l, acc, cmax_b, alpha_c = carry[w]
                acc = values(w, c, 0, alpha_c, acc)
                m, l, alpha1 = probs(w, 1, cmax_b, m, l)
                cmax2 = scores(w, c + 2, 0)
                acc = values(w, c + 1, 1, alpha1, acc)
                m, l, alpha2 = probs(w, 0, cmax2, m, l)
                cmax3 = scores(w, c + 3, 1)
                out.append((m, l, acc, cmax3, alpha2))
            return tuple(out)

        res = lax.fori_loop(0, nk // 2 - 1, trip, tuple(init))
        for w in range(n_par):
            m, l, acc, cmax_b, alpha_c = res[w]
            acc = values(w, nk - 2, 0, alpha_c, acc)
            m, l, alpha1 = probs(w, 1, cmax_b, m, l)
            acc = values(w, nk - 1, 1, alpha1, acc)
            cw, lanes = where(w)
            o_ref[cw, :, lanes] = (acc / l).astype(BF16)
            lse_ref[cw, :, lanes] = m + jnp.log2(l)

    return pl.pallas_call(
        body, name="attn_fwd", grid=(ATTN_HEADS, nq // n_ch),
        in_specs=[pl.BlockSpec((n_ch, LANES, t), lambda h, i: (i, h, 0)),
                  pl.BlockSpec((s, ATTN_KV_W), lambda h, i: (0, 0)),
                  pl.BlockSpec((nk, ATTN_HEAD_DIM, t), lambda h, i: (0, h // ATTN_GROUP, 0))],
        out_specs=[pl.BlockSpec((n_ch, ATTN_HEAD_DIM, t), lambda h, i: (i, h, 0)),
                   pl.BlockSpec((None, n_ch, 1, t), lambda h, i: (h, i, 0, 0))],
        out_shape=[jax.ShapeDtypeStruct((nq, ATTN_Q_W, t), BF16), jax.ShapeDtypeStruct((ATTN_HEADS, nq, 1, t), F32)],
        scratch_shapes=[pltpu.VMEM((t, tq), F32)] * (2 * n_par) + [pltpu.VMEM((t, tq), BF16)] * (2 * n_par),
        compiler_params=_cparams(("parallel", "parallel")),
    )(q_ct, k_rows, v_ct)


def _attn_bwd(q_ct, do_ct, lse, delta, k_rows, v_rows, k_ct):
    nq, _, t = q_ct.shape
    s = nq * t
    kc = 4 if nq % 4 == 0 else 2
    tk = kc * t
    nk = nq // kc
    assert nq % 2 == 0 and nq % kc == 0

    def body(q_ref, do_ref, lse_ref, delta_ref, k_ref, v_ref, kt_ref, dq_ref, dk_ref, dv_ref, dk_acc, dv_acc,
             sb0, sb1, db0, db1, pb0, pb1, gb0, gb1):
        j = pl.program_id(1)
        sb, db, pb, gb = (sb0, sb1), (db0, db1), (pb0, pb1), (gb0, gb1)

        @pl.when(j == 0)
        def _():
            dq_ref[...] = jnp.zeros(dq_ref.shape, F32)

        kj, vj = k_ref[...], v_ref[...]
        ktj = jnp.concatenate([kt_ref[u] for u in range(kc)], axis=1)
        dk_acc[...] = jnp.zeros(dk_acc.shape, F32)
        dv_acc[...] = jnp.zeros(dv_acc.shape, F32)

        def products(i, slot):
            sb[slot][...] = jnp.dot(kj, q_ref[i], preferred_element_type=F32)
            db[slot][...] = jnp.dot(vj, do_ref[i], preferred_element_type=F32)

        def cotangents(i, slot):
            pt = jnp.exp2(sb[slot][...] - lse_ref[i])
            pb[slot][...] = pt.astype(BF16)
            gb[slot][...] = (pt * (db[slot][...] - delta_ref[i])).astype(BF16)

        def accumulate(i, slot):
            dst = gb[slot][...]
            dv_acc[...] += _nt(pb[slot][...], do_ref[i])
            dk_acc[...] += _nt(dst, q_ref[i])
            dq_ref[i] += jnp.dot(ktj, dst, preferred_element_type=F32)

        products(0, 0)
        products(1, 1)
        cotangents(0, 0)

        def trip(n, carry):
            c = 2 * n
            accumulate(c, 0)
            cotangents(c + 1, 1)
            products(c + 2, 0)
            accumulate(c + 1, 1)
            cotangents(c + 2, 0)
            products(c + 3, 1)
            return carry

        lax.fori_loop(0, nq // 2 - 1, trip, 0)
        accumulate(nq - 2, 0)
        cotangents(nq - 1, 1)
        accumulate(nq - 1, 1)
        dk_ref[...] = dk_acc[...].astype(dk_ref.dtype)
        dv_ref[...] = dv_acc[...].astype(dv_ref.dtype)

    per_head = pl.BlockSpec((nq, LANES, t), lambda h, j: (0, h, 0))
    stat = pl.BlockSpec((None, nq, 1, t), lambda h, j: (h, 0, 0, 0))
    kv_rows = pl.BlockSpec((tk, ATTN_KV_W), lambda h, j: (j, 0))
    kv_out = pl.BlockSpec((None, tk, ATTN_KV_W), lambda h, j: (h, j, 0))
    return pl.pallas_call(
        body, name="attn_bwd", grid=(ATTN_HEADS, nk),
        in_specs=[per_head, per_head, stat, stat, kv_rows, kv_rows,
                  pl.BlockSpec((kc, ATTN_HEAD_DIM, t), lambda h, j: (j, h // ATTN_GROUP, 0))],
        out_specs=[pl.BlockSpec((nq, ATTN_HEAD_DIM, t), lambda h, j: (0, h, 0)), kv_out, kv_out],
        out_shape=[jax.ShapeDtypeStruct((nq, ATTN_Q_W, t), F32), jax.ShapeDtypeStruct((ATTN_HEADS, s, ATTN_KV_W), BF16),
                   jax.ShapeDtypeStruct((ATTN_HEADS, s, ATTN_KV_W), BF16)],
        scratch_shapes=([pltpu.VMEM((tk, ATTN_KV_W), F32)] * 2 + [pltpu.VMEM((tk, t), F32)] * 4 + [pltpu.VMEM((tk, t), BF16)] * 4),
        compiler_params=_cparams(("parallel", "arbitrary")),
    )(q_ct, do_ct, lse, delta, k_rows, v_rows, k_ct)


def _log_sigmoid(x):
    t = jnp.exp(-jnp.abs(x))
    log1p_t = jnp.where(t < 1e-2, t * (1.0 - t * (0.5 - t * (1.0 / 3.0))), jnp.log(1.0 + t))
    return jnp.minimum(x, 0.0) - log1p_t


def _decay_tables(logit, backward):
    c, hd = RET_CHUNK, RET_HEAD_DIM

    def lam(shape):
        return _log_sigmoid(jnp.full(shape, logit, F32))

    ii = lax.broadcasted_iota(jnp.int32, (c, c), 0).astype(F32)
    jj = lax.broadcasted_iota(jnp.int32, (c, c), 1).astype(F32)
    pos = lax.broadcasted_iota(jnp.int32, (c, hd), 0).astype(F32)
    if not backward:
        dist, dist_t = jnp.maximum(ii - jj, 0.0), jnp.maximum(jj - ii, 0.0)
        mask, mask_t = ii >= jj, jj >= ii
        e_q, e_k = pos + 1.0, (c - 1.0) - pos
    else:
        dist, dist_t = jnp.maximum(jj - ii, 0.0), jnp.maximum(ii - jj, 0.0)
        mask, mask_t = jj > ii, ii > jj
        e_q, e_k = c - pos, pos
    lam_cc, lam_row = lam((c, c)), lam((c, hd))
    return dict(
        d=jnp.where(mask, jnp.exp(lam_cc * dist), 0.0), d_t=jnp.where(mask_t, jnp.exp(lam_cc * dist_t), 0.0), dist=dist,
        qdec=jnp.exp(lam_row * e_q), kdec=jnp.exp(lam_row * e_k), e_q=e_q, e_k=e_k, gam=jnp.exp(lam((hd, hd)) * c))


def _nt(a, b):
    return lax.dot_general(a, b, (((1,), (1,)), ((), ())), preferred_element_type=F32)


def _ret_sub(n_chunks):
    return max(1, min(n_chunks, 512 // RET_CHUNK))


def _ret_fwd(logits, q, k, proj):
    s = q.shape[0]
    c = RET_CHUNK
    sub = _ret_sub(s // c)
    nb = s // (c * sub)
    block = (lambda n: n, lambda n: nb - 1 - n)
    order = (tuple(range(sub)), tuple(reversed(range(sub))))
    vwin = _win(proj, c * sub, C_RV, RET_W)
    nv = len(vwin)
    vw = RET_W // nv
    per = 2 + nv

    def body(lg_ref, *refs):
        ins, outs, states = refs[:2 * per], refs[2 * per:2 * per + 4], refs[2 * per + 4:]

        @pl.when(pl.program_id(0) == 0)
        def _():
            for st in states:
                st[...] = jnp.zeros(st.shape, F32)

        for h in range(RET_HEADS):
            for d in range(2):
                q_ref, k_ref, v_refs = ins[d * per], ins[d * per + 1], ins[d * per + 2:(d + 1) * per]
                y_ref, st_ref, state = outs[2 * d], outs[2 * d + 1], states[d]
                tb = _decay_tables(lg_ref[d, h], bool(d))
                sl = slice(h * RET_HEAD_DIM, (h + 1) * RET_HEAD_DIM)
                off = h * RET_HEAD_DIM
                sh = state[h]
                for u in order[d]:
                    rows = slice(u * c, (u + 1) * c)
                    qh, kh = q_ref[rows, sl], k_ref[rows, sl]
                    vb = v_refs[off // vw][rows, off % vw:off % vw + RET_HEAD_DIM].astype(BF16)
                    a = _nt(qh.astype(BF16), kh.astype(BF16)) * tb["d"]
                    st_ref[u, h] = sh
                    y_ref[rows, sl] = (jnp.dot(a.astype(BF16), vb, preferred_element_type=F32)
                                       + jnp.dot((qh * tb["qdec"]).astype(BF16), sh.astype(BF16), preferred_element_type=F32))
                    sh = tb["gam"] * sh + jnp.dot((kh * tb["kdec"]).T.astype(BF16), vb, preferred_element_type=F32)
                state[h] = sh

    hmat = (RET_HEADS, RET_HEAD_DIM, RET_HEAD_DIM)
    in_specs, out_specs, args = [pl.BlockSpec(memory_space=pltpu.SMEM)], [], [logits]
    for d in range(2):
        rows = pl.BlockSpec((c * sub, RET_W), lambda n, d=d: (block[d](n), 0))
        in_specs += [rows, rows] + [pl.BlockSpec(sp.block_shape, lambda n, d=d, cb=sp.index_map(0)[1]: (block[d](n), cb)) for _, sp in vwin]
        args += [q, k] + [a for a, _ in vwin]
        out_specs += [rows, pl.BlockSpec((sub,) + hmat, lambda n, d=d: (block[d](n), 0, 0, 0))]
    return pl.pallas_call(
        body, name="ret_fwd", grid=(nb,), in_specs=in_specs, out_specs=out_specs,
        out_shape=[jax.ShapeDtypeStruct((s, RET_W), F32), jax.ShapeDtypeStruct((nb * sub,) + hmat, F32)] * 2,
        scratch_shapes=[pltpu.VMEM(hmat, F32)] * 2,
        compiler_params=_cparams(("arbitrary",)),
    )(*args)


def _ret_bwd(logits, q, k, proj, dy, st_f, st_b):
    s = q.shape[0]
    c = RET_CHUNK
    sub = _ret_sub(s // c)
    nb = s // (c * sub)
    block = (lambda n: nb - 1 - n, lambda n: n)
    order = (tuple(reversed(range(sub))), tuple(range(sub)))
    vwin = _win(proj, c * sub, C_RV, RET_W)
    nv = len(vwin)
    vw = RET_W // nv
    per = 4 + nv

    def body(lg_ref, *refs):
        ins, outs, scr = refs[:2 * per], refs[2 * per:2 * per + 8], refs[2 * per + 8:]
        n = pl.program_id(0)

        @pl.when(n == 0)
        def _():
            for r in scr:
                r[...] = jnp.zeros(r.shape, F32)

        for h in range(RET_HEADS):
            for d in range(2):
                q_ref, k_ref, dy_ref, st_ref = ins[d * per:d * per + 4]
                v_refs = ins[d * per + 4:(d + 1) * per]
                dq_ref, dk_ref, dv_ref = outs[4 * d:4 * d + 3]
                dstate, lacc = scr[2 * d], scr[2 * d + 1]
                tb = _decay_tables(lg_ref[d, h], bool(d))
                sl = slice(h * RET_HEAD_DIM, (h + 1) * RET_HEAD_DIM)
                off = h * RET_HEAD_DIM
                dsh = dstate[h]
                lsum = lacc[h, 0:1, :]
                for u in order[d]:
                    rows = slice(u * c, (u + 1) * c)
                    qh, kh, dyh = q_ref[rows, sl], k_ref[rows, sl], dy_ref[rows, sl]
                    vb = v_refs[off // vw][rows, off % vw:off % vw + RET_HEAD_DIM].astype(BF16)
                    qb, kb, dyb = qh.astype(BF16), kh.astype(BF16), dyh.astype(BF16)
                    sh = st_ref[u, h]
                    shb, dshb = sh.astype(BF16), dsh.astype(BF16)
                    qk = _nt(qb, kb)
                    g = _nt(dyb, vb) * tb["d"]
                    a_t = _nt(kb, qb) * tb["d_t"]
                    g_t = _nt(vb, dyb) * tb["d_t"]
                    qd, kd = qh * tb["qdec"], kh * tb["kdec"]
                    dqd = _nt(dyb, shb)
                    dkd = _nt(vb, dshb)
                    dq_ref[rows, sl] = (jnp.dot(g.astype(BF16), kb, preferred_element_type=F32) + dqd * tb["qdec"]).astype(dq_ref.dtype)
                    dk_ref[rows, sl] = (jnp.dot(g_t.astype(BF16), qb, preferred_element_type=F32) + dkd * tb["kdec"]).astype(dk_ref.dtype)
                    dv_ref[rows, sl] = (jnp.dot(a_t.astype(BF16), dyb, preferred_element_type=F32)
                                        + jnp.dot(kd.astype(BF16), dshb, preferred_element_type=F32)).astype(dv_ref.dtype)
                    intra = jnp.sum(tb["dist"] * qk * g, axis=0, keepdims=True)
                    lsum = (lsum + sum(intra[:, o:o + LANES] for o in range(0, c, LANES))
                            + jnp.sum(tb["e_q"] * qd * dqd + tb["e_k"] * kd * dkd, axis=0, keepdims=True)
                            + jnp.sum(float(c) * tb["gam"] * dsh * sh, axis=0, keepdims=True))
                    dsh = tb["gam"] * dsh + jnp.dot(qd.T.astype(BF16), dyb, preferred_element_type=F32)
                dstate[h] = dsh
                lacc[h, 0:1, :] = lsum

        @pl.when(n == nb - 1)
        def _():
            for d in range(2):
                for h in range(RET_HEADS):
                    outs[4 * d + 3][h] = jnp.zeros((8, LANES), F32) + jnp.sum(scr[2 * d + 1][h])

    hmat = (RET_HEADS, RET_HEAD_DIM, RET_HEAD_DIM)
    in_specs, out_specs, args = [pl.BlockSpec(memory_space=pltpu.SMEM)], [], [logits]
    for d, states in enumerate((st_f, st_b)):
        rows = pl.BlockSpec((c * sub, RET_W), lambda n, d=d: (block[d](n), 0))
        in_specs += ([rows, rows, rows, pl.BlockSpec((sub,) + hmat, lambda n, d=d: (block[d](n), 0, 0, 0))]
                     + [pl.BlockSpec(sp.block_shape, lambda n, d=d, cb=sp.index_map(0)[1]: (block[d](n), cb)) for _, sp in vwin])
        args += [q, k, dy, states] + [a for a, _ in vwin]
        out_specs += [rows, rows, rows, pl.BlockSpec((RET_HEADS, 8, LANES), lambda n: (0, 0, 0))]
    return pl.pallas_call(
        body, name="ret_bwd", grid=(nb,), in_specs=in_specs, out_specs=out_specs,
        out_shape=([jax.ShapeDtypeStruct((s, RET_W), BF16)] * 3 + [jax.ShapeDtypeStruct((RET_HEADS, 8, LANES), F32)]) * 2,
        scratch_shapes=[pltpu.VMEM(hmat, F32), pltpu.VMEM((RET_HEADS, 8, LANES), F32)] * 2,
        compiler_params=_cparams(("arbitrary",)),
    )(*args)


def _local_step(x, p, target, w, small):
    s = x.shape[0]
    tabs = _rope_tables(s, ATTN_HEAD_DIM) + _rope_tables(s, RET_HEAD_DIM)
    g_mix, g_mlp, g_ple = small["mix_norm"][None, :], small["mlp_norm"][None, :], small["ple_norm"][None, :]
    g_final, g_ret = small["final_norm"][None, :], small["ret_norm_gain"][None, :]
    gq_w = jnp.tile(small["attn_q_norm"], ATTN_HEADS)[None, :]
    gk_w = jnp.tile(small["attn_k_norm"], ATTN_KV_HEADS)[None, :]
    logits = small["ret_decay_logit"]

    hb = _stage_norm_in(x, g_mix)
    proj = _mm("in_proj", hb, w["w_in"], tm=512, tn=IN_W // 2, tk=1024, out_dtypes=(BF16,), j_outer=True)
    q_ct, k_rows, k_ct, v_rows, v_ct, rq, rk = _stage_qkv(proj, tabs, gq_w, gk_w)
    o_ct, lse = _attn_fwd(q_ct, k_rows, v_ct)
    ry_f, st_f, ry_b, st_b = _ret_fwd(logits, rq, rk, proj)
    rz, attn_rows = _stage_mix_post(ry_f, ry_b, proj, o_ct, g_ret)
    a_out = _mm("attn_o", attn_rows, w["w_attn_o"], tm=1024, tn=1024, tk=512, out_dtypes=(BF16,))
    n_gate = D_MODEL // EPI_PIECE

    def epi_merge(acc, e, c):
        ga, gr = _cat(e[1:1 + n_gate]), _cat(e[1 + n_gate:1 + 2 * n_gate])
        return acc, _sigmoid(ga) * e[0][...] + _sigmoid(gr) * acc

    r_out, merged = _mm("ret_o", rz, w["w_ret_o"], tm=512, tn=1024, tk=512, out_dtypes=(BF16, BF16), epi=epi_merge,
                        epi_ins=(a_out, (proj, C_GA), (proj, C_GR)))

    def epi_res_norm(acc, e, c):
        xr = e[0][...] + acc
        return xr, _rms_fwd(xr, c[0][...])

    x1, hm = _mm("out_proj", merged, w["w_out"], tm=512, tn=1024, tk=1024, out_dtypes=(F32, BF16),
                 epi=epi_res_norm, epi_ins=(x,), consts=(g_mlp,))

    def epi_relu2(acc, e, c):
        r = jnp.maximum(acc, 0.0)
        return (r * r,)

    act = _mm("mlp_up", hm, w["w_up"], tm=512, tn=2048, tk=1024, out_dtypes=(BF16,), epi=epi_relu2, j_outer=True)
    x2, hp = _mm("mlp_down", act, w["w_down"], tm=512, tn=1024, tk=D_FF, out_dtypes=(F32, BF16),
                 epi=epi_res_norm, epi_ins=(x1,), consts=(g_ple,))
    pe = _mm("ple_emb", p, w["w_ple"], tm=1024, tn=1024, tk=256)

    def epi_head(acc, e, c):
        gt = _sigmoid(acc)
        pe_, gf = e[0][...], c[0][...]
        x3 = e[1][...] + gt * pe_
        r3 = lax.rsqrt(jnp.mean(x3 * x3, axis=-1, keepdims=True) + NORM_EPS)
        x3n = x3 * r3
        err = x3n * gf - e[2][...]
        dy = err * (1.0 / D_MODEL)
        dyg = dy * gf
        dx3 = r3 * (dyg - x3n * jnp.mean(dyg * x3n, axis=-1, keepdims=True))
        return dx3, dx3 * pe_ * gt * (1.0 - gt), dx3 * gt, err * err, dy * x3n

    dx3, dzg, dpe, loss_cols, g_final_p = _mm("ple_gate", hp, w["w_ple_gate"], tm=512, tn=1024, tk=1024, out_dtypes=(F32, BF16, BF16),
                                              epi=epi_head, epi_ins=(pe, x2, target), consts=(g_final,), n_sums=2)
    loss_sum = 0.5 / D_MODEL * jnp.sum(loss_cols)

    gw = {}
    gw["w_ple"] = _mm("g_w_ple", p, dpe, ta=True, tm=256, tn=1024, tk=2048)
    gw["w_ple_gate"] = _mm("g_w_ple_gate", hp, dzg, ta=True, tm=1024, tn=1024, tk=2048)
    def epi_norm_bwd(acc, e, c):
        dx, dg = _rms_bwd(acc, e[0][...], c[0][...])
        return e[1][...] + dx, dg

    def epi_norm_bwd_b(acc, e, c):
        tot, dg = epi_norm_bwd(acc, e, c)
        return tot, tot, dg

    dx2, dx2_b, g_ple_p = _mm("d_hp", dzg, w["w_ple_gate"], tb=True, tm=512, tn=1024, tk=1024, out_dtypes=(F32, BF16),
                              epi=epi_norm_bwd_b, epi_ins=(x2, dx3), consts=(g_ple,), n_sums=1)

    def epi_relu2_bwd(acc, e, c):
        return (acc * (2.0 * jnp.sqrt(e[0][...].astype(F32))),)

    du = _mm("d_u", dx2_b, w["w_down"], tb=True, tm=512, tn=2048, tk=1024, out_dtypes=(BF16,), epi=epi_relu2_bwd, epi_ins=(act,),
             j_outer=True)
    gw["w_down"] = _mm("g_w_down", act, dx2_b, ta=True, tm=1024, tn=1024, tk=2048)
    gw["w_up"] = _mm("g_w_up", hm, du, ta=True, tm=1024, tn=1024, tk=2048)
    dx1, dx1_b, g_mlp_p = _mm("d_hm", du, w["w_up"], tb=True, tm=512, tn=1024, tk=D_FF, out_dtypes=(F32, BF16),
                              epi=epi_norm_bwd_b, epi_ins=(x1, dx2), consts=(g_mlp,), n_sums=1)
    def epi_merge_bwd(acc, e, c):
        sa, sr = _sigmoid(_cat(e[2:2 + n_gate])), _sigmoid(_cat(e[2 + n_gate:2 + 2 * n_gate]))
        return acc * sa, acc * sr, acc * e[0][...] * sa * (1.0 - sa), acc * e[1][...] * sr * (1.0 - sr)

    dao, dro, dga, dgr = _mm("d_merged", dx1_b, w["w_out"], tb=True, tm=512, tn=1024, tk=1024, out_dtypes=(BF16,) * 4,
                             epi=epi_merge_bwd, epi_ins=(a_out, r_out, (proj, C_GA), (proj, C_GR)))
    gw["w_out"] = _mm("g_w_out", merged, dx1_b, ta=True, tm=1024, tn=1024, tk=2048)
    gw["w_attn_o"] = _mm("g_w_attn_o", attn_rows, dao, ta=True, tm=512, tn=1024, tk=2048)
    gw["w_ret_o"] = _mm("g_w_ret_o", rz, dro, ta=True, tm=512, tn=1024, tk=2048)
    dattn = _mm("d_attn", dao, w["w_attn_o"], tb=True, tm=1024, tn=512, tk=1024)
    drz = _mm("d_rz", dro, w["w_ret_o"], tb=True, tm=1024, tn=512, tk=1024)
    do_ct, delta, dry, drg, g_ret_p = _stage_mix_post_bwd(dattn, attn_rows, drz, ry_f, ry_b, proj, g_ret)
    dq_f, dk_f, dv_f, dl_f, dq_b, dk_b, dv_b, dl_b = _ret_bwd(logits, rq, rk, proj, dry, st_f, st_b)
    dq_ct, dk8, dv8 = _attn_bwd(q_ct, do_ct, lse, delta, k_rows, v_rows, k_ct)
    dproj, gq_p, gk_p = _stage_dproj(proj, dq_ct, dk8, dv8, (dq_f, dk_f, dv_f, dq_b, dk_b, dv_b), drg, dga, dgr, tabs, gq_w, gk_w)
    gw["w_in"] = _mm("g_w_in", hb, dproj, ta=True, tm=1024, tn=IN_W // 2, tk=1024)
    grad_x, g_mix_p = _mm("d_h", dproj, w["w_in"], tb=True, tm=512, tn=1024, tk=IN_W, epi=epi_norm_bwd, epi_ins=(x, dx1),
                          consts=(g_mix,), n_sums=1)

    gs = {
        "mix_norm": g_mix_p[0], "mlp_norm": g_mlp_p[0], "ple_norm": g_ple_p[0], "final_norm": g_final_p[0],
        "ret_norm_gain": g_ret_p[0],
        "attn_q_norm": jnp.sum(gq_p[0].reshape(ATTN_HEADS, ATTN_HEAD_DIM), axis=0),
        "attn_k_norm": jnp.sum(gk_p[0].reshape(ATTN_KV_HEADS, ATTN_HEAD_DIM), axis=0),
        "ret_decay_logit": jnp.stack([dl_f[:, 0, 0], dl_b[:, 0, 0]]),
    }
    return loss_sum, grad_x, gw, gs


PACK_COLS = 1024
N_CHIPS = 4
HALF_ROWS = 2048


def _pack_shard(parts):
    return jnp.concatenate([parts[n].reshape(-1, PACK_COLS) for n, _ in BIG], axis=0)


def _unpack_shard(slab, shapes):
    out, r = {}, 0
    for n, _ in BIG:
        rows = math.prod(shapes[n]) // PACK_COLS
        out[n] = slab[r:r + rows].reshape(shapes[n])
        r += rows
    return out


def _shard_of(full, axis, sidx):
    size = full.shape[axis] // N_CHIPS
    return lax.slice_in_dim(full, sidx * size, (sidx + 1) * size, axis=axis)


def _position():
    x, y, c = lax.axis_index("x"), lax.axis_index("y"), lax.axis_index("c")
    return x, y, c


def _other_chips(x, y):
    return [(1 - x, y), (x, 1 - y), (1 - x, 1 - y)]


ANY = pl.BlockSpec(memory_space=pl.ANY)


def _gather_weights(slab):
    rows = slab.shape[0]
    half = rows // 2

    def body(in_ref, out_ref, send_sems, recv_sems):
        x, y, c = _position()
        chips = _other_chips(x, y)

        def piece(chip, core):
            return out_ref.at[2 * chip[0] + chip[1], pl.ds(core * half, half), :]

        def copy(k, chip, core, to, src=None):
            return pltpu.make_async_remote_copy(
                src_ref=piece(chip, core) if src is None else src, dst_ref=piece(chip, core),
                send_sem=send_sems.at[k], recv_sem=recv_sems.at[k], device_id=to, device_id_type=MESH)

        first = [copy(j, (x, y), c, (*chip, c), src=in_ref.at[pl.ds(c * half, half), :]) for j, chip in enumerate(chips)]
        for cp in first:
            cp.start()
        passed = [copy(3 + j, chip, c, (x, y, 1 - c)) for j, chip in enumerate(chips)]
        for j, chip in enumerate(chips):
            copy(j, chip, c, (x, y, c)).wait_recv()
            passed[j].start()
        for j, chip in enumerate(chips):
            copy(3 + j, chip, 1 - c, (x, y, c)).wait_recv()
        for cp in first + passed:
            cp.wait_send()

    return pl.pallas_call(
        body, name="gather_weights", in_specs=[ANY], out_specs=ANY,
        out_shape=jax.ShapeDtypeStruct((N_CHIPS,) + slab.shape, slab.dtype),
        scratch_shapes=[pltpu.SemaphoreType.DMA((6,)), pltpu.SemaphoreType.DMA((6,))],
    )(slab)


def _exchange_halves(g):
    def body(g_ref, out_ref, send_sem, recv_sem):
        x, y, c = _position()
        cp = pltpu.make_async_remote_copy(src_ref=g_ref.at[pl.ds(0, N_CHIPS), 1 - c], dst_ref=out_ref, send_sem=send_sem,
                                          recv_sem=recv_sem, device_id=(x, y, 1 - c), device_id_type=MESH)
        cp.start()
        cp.wait()

    return pl.pallas_call(
        body, name="exchange_halves", in_specs=[ANY], out_specs=ANY,
        out_shape=jax.ShapeDtypeStruct((N_CHIPS,) + g.shape[2:], g.dtype),
        scratch_shapes=[pltpu.SemaphoreType.DMA, pltpu.SemaphoreType.DMA],
    )(g)


def _add_my_half(g, r1, c_idx):
    tr = 256
    nt = g.shape[2] // tr
    out = (N_CHIPS,) + g.shape[2:]

    def body(c_ref, g_ref, r_ref, o_ref, ob_ref):
        tot = g_ref[...] + r_ref[...]
        o_ref[...] = tot
        ob_ref[...] = tot.astype(BF16)

    blk = (None, tr, PACK_COLS)
    spec = pl.BlockSpec(blk, lambda s, i, c_ref: (s, i, 0))
    return pl.pallas_call(
        body, name="add_my_half",
        grid_spec=pltpu.PrefetchScalarGridSpec(
            num_scalar_prefetch=1, grid=(N_CHIPS, nt),
            in_specs=[pl.BlockSpec((None,) + blk, lambda s, i, c_ref: (s, c_ref[0], i, 0)), spec],
            out_specs=[spec, spec]),
        out_shape=[jax.ShapeDtypeStruct(out, F32), jax.ShapeDtypeStruct(out, BF16)],
        compiler_params=_cparams(("parallel", "parallel")),
    )(c_idx, g, r1)


def _scatter_to_chips(part):
    def body(p_ref, out_ref, send_sems, recv_sems):
        x, y, c = _position()
        chips = _other_chips(x, y)
        sends = [pltpu.make_async_remote_copy(
            src_ref=p_ref.at[2 * chip[0] + chip[1]], dst_ref=out_ref.at[j], send_sem=send_sems.at[j], recv_sem=recv_sems.at[j],
            device_id=(*chip, c), device_id_type=MESH) for j, chip in enumerate(chips)]
        for cp in sends:
            cp.start()
        for cp in sends:
            cp.wait()

    return pl.pallas_call(
        body, name="scatter_to_chips", in_specs=[ANY], out_specs=ANY,
        out_shape=jax.ShapeDtypeStruct((N_CHIPS - 1,) + part.shape[1:], part.dtype),
        scratch_shapes=[pltpu.SemaphoreType.DMA((3,)), pltpu.SemaphoreType.DMA((3,))],
    )(part)


def _sum_chips(part, r2, chip_idx):
    tr = 256

    def body(c_ref, p_ref, r_ref, o_ref):
        o_ref[...] = ((p_ref[...] + r_ref[0]) + r_ref[1]) + r_ref[2]

    return pl.pallas_call(
        body, name="sum_chips",
        grid_spec=pltpu.PrefetchScalarGridSpec(
            num_scalar_prefetch=1, grid=(r2.shape[1] // tr,),
            in_specs=[pl.BlockSpec((None, tr, PACK_COLS), lambda i, c_ref: (c_ref[0], i, 0)),
                      pl.BlockSpec((N_CHIPS - 1, tr, PACK_COLS), lambda i, c_ref: (0, i, 0))],
            out_specs=pl.BlockSpec((tr, PACK_COLS), lambda i, c_ref: (i, 0))),
        out_shape=jax.ShapeDtypeStruct(r2.shape[1:], F32),
        compiler_params=_cparams(("parallel",)),
    )(chip_idx, part, r2)


def _join_halves(red):
    def body(r_ref, out_ref, send_sem, recv_sem):
        x, y, c = _position()
        cp = pltpu.make_async_remote_copy(src_ref=r_ref, dst_ref=out_ref, send_sem=send_sem, recv_sem=recv_sem,
                                          device_id=(x, y, 1 - c), device_id_type=MESH)
        cp.start()
        cp.wait()

    return pl.pallas_call(
        body, name="join_halves", in_specs=[ANY], out_specs=ANY,
        out_shape=jax.ShapeDtypeStruct(red.shape, red.dtype),
        scratch_shapes=[pltpu.SemaphoreType.DMA, pltpu.SemaphoreType.DMA],
    )(red)


def _adamw_math(w, g, m, v):
    m = ADAM_B1 * m + (1.0 - ADAM_B1) * g
    v = ADAM_B2 * v + (1.0 - ADAM_B2) * (g * g)
    m_hat = m / (1.0 - ADAM_B1 ** ADAM_STEP)
    v_hat = v / (1.0 - ADAM_B2 ** ADAM_STEP)
    delta = -ADAM_LR * (m_hat / (jnp.sqrt(v_hat) + ADAM_EPS) + ADAM_WD * w)
    return delta, m, v


def _adamw(name, w, g, m, v):
    tr = min(256, w.shape[0])

    def body(w_ref, g_ref, m_ref, v_ref, d_ref, nm_ref, nv_ref):
        d_ref[...], nm_ref[...], nv_ref[...] = _adamw_math(w_ref[...], g_ref[...], m_ref[...], v_ref[...])

    blk = pl.BlockSpec((tr, w.shape[1]), lambda i: (i, 0))
    return pl.pallas_call(
        body, name="adamw_" + name, grid=(w.shape[0] // tr,), in_specs=[blk] * 4, out_specs=[blk] * 3,
        out_shape=[jax.ShapeDtypeStruct(w.shape, F32)] * 3, compiler_params=_cparams(("parallel",)),
    )(w, g, m, v)


def _small_step(gpk, wpk, mpk, vpk):
    row, col, width = SMALL["ret_decay_logit"]

    def body(g_ref, w_ref, m_ref, v_ref, og_ref, od_ref, om_ref, ov_ref, gbuf, send_sems, recv_sems):
        x, y, c = _position()
        me = 4 * x + 2 * y + c
        gbuf[me] = g_ref[...]
        sends = []
        for k in range(1, 8):
            to = (x ^ (k >> 2), y ^ ((k >> 1) & 1), c ^ (k & 1))
            cp = pltpu.make_async_remote_copy(src_ref=g_ref, dst_ref=gbuf.at[me], send_sem=send_sems.at[k - 1],
                                              recv_sem=recv_sems.at[k - 1], device_id=to, device_id_type=MESH)
            cp.start()
            sends.append(cp)
        for k in range(1, 8):
            frm = me ^ k
            pltpu.make_async_remote_copy(src_ref=g_ref, dst_ref=gbuf.at[frm], send_sem=send_sems.at[k - 1],
                                         recv_sem=recv_sems.at[k - 1], device_id=(x, y, c), device_id_type=MESH).wait_recv()
        for cp in sends:
            cp.wait_send()
        tot = gbuf[0]
        for d in range(1, 8):
            tot = tot + gbuf[d]
        w = w_ref[...]
        r_i = lax.broadcasted_iota(jnp.int32, w.shape, 0)
        c_i = lax.broadcasted_iota(jnp.int32, w.shape, 1)
        is_logit = (r_i == row) & (c_i >= col) & (c_i < col + width)
        g = jnp.where(is_logit, tot * _sigmoid(-w), tot)
        og_ref[...] = g
        od_ref[...], om_ref[...], ov_ref[...] = _adamw_math(w, g, m_ref[...], v_ref[...])

    vm = pl.BlockSpec(memory_space=pltpu.VMEM)
    shp = jax.ShapeDtypeStruct(gpk.shape, F32)
    return pl.pallas_call(
        body, name="small_step", in_specs=[vm] * 4, out_specs=[vm] * 4, out_shape=[shp] * 4,
        scratch_shapes=[pltpu.VMEM((8,) + gpk.shape, F32), pltpu.SemaphoreType.DMA((7,)), pltpu.SemaphoreType.DMA((7,))],
    )(gpk, wpk, mpk, vpk)


def _pack_small(parts):
    rows = [[] for _ in range(SMALL_ROWS)]
    for n, (r, col, width) in sorted(SMALL.items(), key=lambda kv: (kv[1][0], kv[1][1])):
        rows[r].append((col, parts[n].reshape(-1).astype(F32)))
    out = []
    for r in range(SMALL_ROWS):
        segs, pos = [], 0
        for col, vec in rows[r]:
            assert col == pos
            segs.append(vec)
            pos += vec.shape[0]
        if pos < PACK_COLS:
            segs.append(jnp.zeros((PACK_COLS - pos,), F32))
        out.append(jnp.concatenate(segs))
    return jnp.stack(out)


def _unpack_small(pk, shapes):
    return {n: pk[r, col:col + width].reshape(shapes[n]) for n, (r, col, width) in SMALL.items()}


WEIGHTS = ("mix_norm", "w_in", "attn_q_norm", "attn_k_norm", "ret_decay_logit", "ret_norm_gain", "w_attn_o", "w_ret_o", "w_out",
           "mlp_norm", "w_up", "w_down", "ple_norm", "w_ple_gate", "w_ple", "final_norm")


def kernel(x, p, mix_norm, w_in, attn_q_norm, attn_k_norm, ret_decay_logit, ret_norm_gain, w_attn_o, w_ret_o, w_out, mlp_norm, w_up, w_down, ple_norm, w_ple_gate, w_ple, final_norm, loss_target, m_mix_norm, m_w_in, m_attn_q_norm, m_attn_k_norm, m_ret_decay_logit, m_ret_norm_gain, m_w_attn_o, m_w_ret_o, m_w_out, m_mlp_norm, m_w_up, m_w_down, m_ple_norm, m_w_ple_gate, m_w_ple, m_final_norm, v_mix_norm, v_w_in, v_attn_q_norm, v_attn_k_norm, v_ret_decay_logit, v_ret_norm_gain, v_w_attn_o, v_w_ret_o, v_w_out, v_mlp_norm, v_w_up, v_w_down, v_ple_norm, v_w_ple_gate, v_w_ple, v_final_norm):
    args = dict(locals())
    wts = {n: args[n] for n in WEIGHTS}
    ms = {n: args["m_" + n] for n in WEIGHTS}
    vs = {n: args["v_" + n] for n in WEIGHTS}
    shapes = {n: wts[n].shape for n in WEIGHTS}
    big_names = [n for n, _ in BIG]
    xi, yi, ci = _position()
    c_idx = ci.astype(jnp.int32).reshape(1)

    chip_idx = (2 * xi + yi).astype(jnp.int32)
    slab_b = _pack_shard({n: wts[n][0].astype(BF16) for n in big_names})
    gathered = lax.dynamic_update_slice(_gather_weights(slab_b), slab_b[None], (chip_idx, 0, 0))
    full, r0 = {}, 0
    for n, axis in BIG:
        shard = shapes[n][1:]
        rows = math.prod(shard) // PACK_COLS
        if axis == 0 and shard[1] == PACK_COLS:
            full[n] = gathered[:, r0:r0 + rows].reshape(N_CHIPS * shard[0], shard[1])
        else:
            full[n] = jnp.concatenate([gathered[k, r0:r0 + rows].reshape(shard) for k in range(N_CHIPS)], axis=axis)
        r0 += rows
    small = {n: wts[n].reshape(wts[n].shape[1:] if wts[n].ndim > 1 else wts[n].shape) for n in SMALL}

    loss_part, grad_x, gw, gs = _local_step(x[0], p[0, 0], loss_target[0], full, small)
    loss = lax.psum(loss_part, ("x", "y", "c"))

    slabs = jnp.stack([_pack_shard({n: _shard_of(gw[n], axis, k) for n, axis in BIG}) for k in range(N_CHIPS)])
    halves = slabs.reshape(N_CHIPS, 2, HALF_ROWS, PACK_COLS)
    chip_part, chip_part_b = _add_my_half(halves, _exchange_halves(halves), c_idx)
    mine = _sum_chips(chip_part, _scatter_to_chips(chip_part_b), chip_idx.reshape(1))
    both = jnp.stack([mine, _join_halves(mine)])
    reduced = jnp.where(ci == 0, both, both[::-1]).reshape(2 * HALF_ROWS, PACK_COLS)
    g_big = _unpack_shard(reduced, {n: shapes[n][1:] for n in big_names})
    big_out = [{}, {}, {}, {}]
    for n in big_names:
        big_out[0][n] = g_big[n][None]
        for kind, a in enumerate(_adamw(n, wts[n][0], g_big[n], ms[n][0], vs[n][0])):
            big_out[kind + 1][n] = a[None]

    sm_out = _small_step(_pack_small(gs), _pack_small({n: wts[n] for n in SMALL}), _pack_small({n: ms[n] for n in SMALL}),
                         _pack_small({n: vs[n] for n in SMALL}))
    small_out = [_unpack_small(a, {n: shapes[n] for n in SMALL}) for a in sm_out]

    outs = [loss, grad_x[None]]
    for kind in range(4):
        for n in WEIGHTS:
            outs.append(small_out[kind][n] if n in SMALL else big_out[kind][n])
    return tuple(outs)
```

```python
import math

import jax
import jax.numpy as jnp
from jax import lax
from jax.experimental import pallas as pl
from jax.experimental.pallas import tpu as pltpu

F32 = jnp.float32
BF16 = jnp.bfloat16
MESH = pl.DeviceIdType.MESH

D_MODEL = 1024
GRID_W = 64
ATTN_HEAD_DIM = 64
ATTN_HEADS = 8
ATTN_KV_HEADS = 2
ATTN_GROUP = ATTN_HEADS // ATTN_KV_HEADS
RET_HEAD_DIM = 128
RET_HEADS = 4
ATTN_Q_W = 512
ATTN_KV_W = 128
RET_W = 512
IN_W = 4864
D_FF = 4096
RET_CHUNK = 256
ROPE_THETA = 10000.0
NORM_EPS = 1e-6
GN_EPS = 1e-5
ATTN_SCALE = ATTN_HEAD_DIM ** -0.5
LOG2E = math.log2(math.e)
Q_FOLD = ATTN_SCALE * LOG2E
RET_SCALE = RET_HEAD_DIM ** -0.5

C_AQ, C_AK, C_AV, C_RQ, C_RK, C_RV, C_RG, C_GA, C_GR = 0, 512, 640, 768, 1280, 1792, 2304, 2816, 3840

ADAM_LR = 0.001
ADAM_B1 = 0.9
ADAM_B2 = 0.999
ADAM_EPS = 1e-08
ADAM_WD = 0.01
ADAM_STEP = 10

LANES = 128
VMEM_LIMIT = 56 << 20
SEQ_TILE = 512
EPI_PIECE = 256

BIG = (("w_in", 1), ("w_attn_o", 1), ("w_ret_o", 1), ("w_out", 0), ("w_up", 1), ("w_down", 0), ("w_ple_gate", 0), ("w_ple", 1))
SMALL_ROWS = 8
SMALL = {"mix_norm": (0, 0, 1024), "mlp_norm": (1, 0, 1024), "ple_norm": (2, 0, 1024), "final_norm": (3, 0, 1024),
         "ret_norm_gain": (4, 0, 512), "attn_q_norm": (4, 512, 64), "attn_k_norm": (4, 576, 64), "ret_decay_logit": (4, 640, 8)}


def _seq_tile(s):
    return min(SEQ_TILE, s // 2)


def _cparams(sem=None, vmem=VMEM_LIMIT):
    return pltpu.CompilerParams(dimension_semantics=sem, vmem_limit_bytes=vmem)


def _mm(name, a, b, *, ta=False, tb=False, tm, tn, tk, out_dtypes=(F32,), epi=None, epi_ins=(), consts=(), n_sums=0, j_outer=False):
    if ta:
        kdim, m = a.shape
    else:
        m, kdim = a.shape
    n = b.shape[0] if tb else b.shape[1]
    tm, tn, tk = min(tm, m), min(tn, n), min(tk, kdim)
    assert m % tm == 0 and n % tn == 0 and kdim % tk == 0, (name, m, n, kdim, tm, tn, tk)
    nk = kdim // tk
    e_arrs, e_cols = [], []
    for item in epi_ins:
        if isinstance(item, tuple):
            arr, start = item
            assert tn == n and start % EPI_PIECE == 0 and n % EPI_PIECE == 0
            for piece in range(n // EPI_PIECE):
                e_arrs.append(arr)
                e_cols.append(start // EPI_PIECE + piece)
        else:
            e_arrs.append(item)
            e_cols.append(None)
    n_e, n_c, n_o = len(e_arrs), len(consts), len(out_dtypes)
    assert n_sums == 0 or tn == n

    def body(*refs):
        a_ref, b_ref = refs[0], refs[1]
        e_refs = refs[2:2 + n_e]
        c_refs = refs[2 + n_e:2 + n_e + n_c]
        o_refs = refs[2 + n_e + n_c:2 + n_e + n_c + n_o]
        s_refs = refs[2 + n_e + n_c + n_o:2 + n_e + n_c + n_o + n_sums]
        acc_ref = refs[2 + n_e + n_c + n_o + n_sums] if nk > 1 else None
        k = pl.program_id(2)
        if n_sums:
            @pl.when((pl.program_id(1 if j_outer else 0) == 0) & (k == 0))
            def _():
                for r in s_refs:
                    r[...] = jnp.zeros(r.shape, F32)
        av = a_ref[...].astype(BF16)
        bv = b_ref[...].astype(BF16)
        dims = (((0,) if ta else (1,), (1,) if tb else (0,)), ((), ()))
        part = lax.dot_general(av, bv, dims, preferred_element_type=F32)

        def finish(acc):
            vals = epi(acc, e_refs, c_refs) if epi is not None else (acc,)
            for o_ref, v in zip(o_refs, vals[:n_o]):
                o_ref[...] = v.astype(o_ref.dtype)
            for s_ref, v in zip(s_refs, vals[n_o:]):
                _acc_add(s_ref, v)

        if nk == 1:
            finish(part)
        else:
            @pl.when(k == 0)
            def _():
                acc_ref[...] = part

            @pl.when(k > 0)
            def _():
                acc_ref[...] += part

            @pl.when(k == nk - 1)
            def _():
                finish(acc_ref[...])

    def spec(shape, index):
        return pl.BlockSpec(shape, (lambda j, i, k: index(i, j, k)) if j_outer else index)

    a_spec = spec((tk, tm), lambda i, j, k: (k, i)) if ta else spec((tm, tk), lambda i, j, k: (i, k))
    b_spec = spec((tn, tk), lambda i, j, k: (j, k)) if tb else spec((tk, tn), lambda i, j, k: (k, j))
    o_spec = spec((tm, tn), lambda i, j, k: (i, j))
    c_specs = [spec(c.shape, lambda i, j, k, nd=c.ndim: (0,) * nd) for c in consts]
    outs = pl.pallas_call(
        body, name=name,
        grid=(n // tn, m // tm, nk) if j_outer else (m // tm, n // tn, nk),
        in_specs=([a_spec, b_spec]
                  + [o_spec if cb is None else spec((tm, EPI_PIECE), lambda i, j, k, cb=cb: (i, cb)) for cb in e_cols] + c_specs),
        out_specs=[o_spec] * n_o + [spec((8, n), lambda i, j, k: (0, 0))] * n_sums,
        out_shape=[jax.ShapeDtypeStruct((m, n), dt) for dt in out_dtypes] + [jax.ShapeDtypeStruct((8, n), F32)] * n_sums,
        scratch_shapes=[pltpu.VMEM((tm, tn), F32)] if nk > 1 else [],
        compiler_params=_cparams(("arbitrary",) * 3 if n_sums else ("parallel", "parallel", "arbitrary")),
    )(a, b, *e_arrs, *consts)
    return outs[0] if n_o + n_sums == 1 else outs


def _rows(arr, tr):
    return (arr, pl.BlockSpec((tr, arr.shape[1]), lambda i: (i, 0)))


def _win(arr, tr, start, width):
    bw = math.gcd(start, width) if start else width
    assert bw % LANES == 0
    return [(arr, pl.BlockSpec((tr, bw), lambda i, cb=start // bw + p: (i, cb))) for p in range(width // bw)]


def _ct(arr):
    return (arr, pl.BlockSpec((None,) + arr.shape[1:], lambda i: (i, 0, 0)))


def _whole(arr):
    return (arr, pl.BlockSpec(arr.shape, lambda i, nd=arr.ndim: (0,) * nd))


def _cat(refs):
    vals = [r[...].astype(F32) for r in refs]
    return vals[0] if len(vals) == 1 else jnp.concatenate(vals, axis=1)


def _seqtiled(name, fn, n_tiles, ins, outs, acc_widths=()):
    n_i, n_o, n_a = len(ins), len(outs), len(acc_widths)

    def body(*refs):
        i_refs, o_refs, a_refs = refs[:n_i], refs[n_i:n_i + n_o], refs[n_i + n_o:]
        if n_a:
            @pl.when(pl.program_id(0) == 0)
            def _():
                for r in a_refs:
                    r[...] = jnp.zeros(r.shape, F32)
        fn(list(i_refs), list(o_refs), list(a_refs))

    res = pl.pallas_call(
        body, name=name, grid=(n_tiles,),
        in_specs=[s for _, s in ins],
        out_specs=[s for _, _, s in outs] + [pl.BlockSpec((8, w), lambda i: (0, 0)) for w in acc_widths],
        out_shape=[jax.ShapeDtypeStruct(sh, dt) for sh, dt, _ in outs] + [jax.ShapeDtypeStruct((8, w), F32) for w in acc_widths],
        compiler_params=_cparams(("arbitrary",)),
    )(*[a for a, _ in ins])
    return res


def _acc_add(acc_ref, val):
    acc_ref[0:1, :] += jnp.sum(val, axis=0, keepdims=True)


def _out_rows(s, w, dt, tr):
    return ((s, w), dt, pl.BlockSpec((tr, w), lambda i: (i, 0)))


def _out_ct(s, w, dt, t):
    return ((s // t, w, t), dt, pl.BlockSpec((None, w, t), lambda i: (i, 0, 0)))


def _rms_fwd(x, gain):
    r = lax.rsqrt(jnp.mean(x * x, axis=-1, keepdims=True) + NORM_EPS)
    return x * r * gain


def _rms_bwd(dy, x, gain):
    r = lax.rsqrt(jnp.mean(x * x, axis=-1, keepdims=True) + NORM_EPS)
    xn = x * r
    dyg = dy * gain
    dx = r * (dyg - xn * jnp.mean(dyg * xn, axis=-1, keepdims=True))
    return dx, dy * xn


def _seg_mean(y, hd):
    w = y.shape[1]
    pieces = []
    for s in range(0, w, LANES):
        v = y[:, s:s + LANES]
        tot = jnp.sum(v, axis=1, keepdims=True)
        if hd == LANES:
            pieces.append(jnp.broadcast_to(tot, v.shape))
        else:
            low = lax.broadcasted_iota(jnp.int32, v.shape, 1) < hd
            lo = jnp.sum(jnp.where(low, v, 0.0), axis=1, keepdims=True)
            pieces.append(jnp.where(low, lo, tot - lo))
    out = pieces[0] if len(pieces) == 1 else jnp.concatenate(pieces, axis=1)
    return out * (1.0 / hd)


def _tile_lanes(t, w):
    return t if w == t.shape[1] else jnp.concatenate([t] * (w // t.shape[1]), axis=1)


def _swap_halves(x, hd):
    w = x.shape[1]
    half = hd // 2
    lane = lax.broadcasted_iota(jnp.int32, x.shape, 1)
    return jnp.where((lane % hd) < half, pltpu.roll(x, w - half, 1), pltpu.roll(x, half, 1))


def _rope(x, cos, sin_signed, hd):
    w = x.shape[1]
    return x * _tile_lanes(cos, w) + _swap_halves(x, hd) * _tile_lanes(sin_signed, w)


def _rope_t(dy, cos, sin_signed, hd):
    w = dy.shape[1]
    return dy * _tile_lanes(cos, w) + _swap_halves(dy * _tile_lanes(sin_signed, w), hd)


def _headnorm_fwd(x, gain_w, hd):
    r = lax.rsqrt(_seg_mean(x * x, hd) + NORM_EPS)
    return x * r * gain_w


def _headnorm_bwd(dy, x, gain_w, hd):
    r = lax.rsqrt(_seg_mean(x * x, hd) + NORM_EPS)
    xn = x * r
    dyg = dy * gain_w
    return r * (dyg - xn * _seg_mean(dyg * xn, hd)), dy * xn


def _sigmoid(x):
    return 1.0 / (1.0 + jnp.exp(-x))


def _rope_tables(seq_len, head_dim):
    rows = seq_len // GRID_W
    n_axis = head_dim // 4
    freqs = ROPE_THETA ** (-jnp.arange(n_axis, dtype=F32) / n_axis)
    ang_r = jnp.arange(rows, dtype=F32)[:, None] * freqs
    ang_c = jnp.arange(GRID_W, dtype=F32)[:, None] * freqs

    def expand(by_row, by_col):
        r = jnp.broadcast_to(by_row[:, None, :], (rows, GRID_W, n_axis))
        c = jnp.broadcast_to(by_col[None, :, :], (rows, GRID_W, n_axis))
        return jnp.concatenate([r, c], axis=-1).reshape(seq_len, 2 * n_axis)

    cos, sin = expand(jnp.cos(ang_r), jnp.cos(ang_c)), expand(jnp.sin(ang_r), jnp.sin(ang_c))
    reps = LANES // head_dim
    return jnp.tile(jnp.concatenate([cos, cos], axis=-1), (1, reps)), jnp.tile(jnp.concatenate([-sin, sin], axis=-1), (1, reps))


def _stage_norm_in(x, gain):
    s = x.shape[0]
    tr = min(SEQ_TILE, s)

    def fn(i, o, a):
        o[0][...] = _rms_fwd(i[0][...], i[1][...]).astype(BF16)

    return _seqtiled("norm_in", fn, s // tr, [_rows(x, tr), _whole(gain)], [_out_rows(s, D_MODEL, BF16, tr)])[0]


def _stage_qkv(proj, tabs, gq_w, gk_w):
    s = proj.shape[0]
    t = _seq_tile(s)
    ca, sa, cr, sr = tabs
    ins = (_win(proj, t, C_AQ, ATTN_Q_W) + _win(proj, t, C_AK, ATTN_KV_W) + _win(proj, t, C_AV, ATTN_KV_W)
           + _win(proj, t, C_RQ, RET_W) + _win(proj, t, C_RK, RET_W)
           + [_rows(ca, t), _rows(sa, t), _rows(cr, t), _rows(sr, t), _whole(gq_w), _whole(gk_w)])

    def fn(i, o, a):
        aq, ak, av = (i[n][...].astype(F32) for n in range(3))
        rq, rk = _cat(i[3:5]), _cat(i[5:7])
        ca_, sa_, cr_, sr_ = i[7][...], i[8][...], i[9][...], i[10][...]
        qr = _rope(_headnorm_fwd(aq, i[11][...], ATTN_HEAD_DIM), ca_, sa_, ATTN_HEAD_DIM) * Q_FOLD
        kr = _rope(_headnorm_fwd(ak, i[12][...], ATTN_HEAD_DIM), ca_, sa_, ATTN_HEAD_DIM)
        qt = qr.T.astype(BF16)
        zeros = jnp.zeros((ATTN_HEAD_DIM, t), BF16)
        for h in range(ATTN_HEADS):
            g = h // ATTN_GROUP
            blk = qt[h * ATTN_HEAD_DIM:(h + 1) * ATTN_HEAD_DIM, :]
            o[0][h * LANES + g * ATTN_HEAD_DIM:h * LANES + (g + 1) * ATTN_HEAD_DIM, :] = blk
            o[0][h * LANES + (1 - g) * ATTN_HEAD_DIM:h * LANES + (2 - g) * ATTN_HEAD_DIM, :] = zeros
        o[1][...] = kr.astype(BF16)
        o[2][...] = kr.T.astype(BF16)
        o[3][...] = av.astype(BF16)
        o[4][...] = av.T.astype(BF16)
        o[5][...] = _rope(rq, cr_, sr_, RET_HEAD_DIM) * RET_SCALE
        o[6][...] = _rope(rk, cr_, sr_, RET_HEAD_DIM)

    outs = [_out_ct(s, ATTN_HEADS * LANES, BF16, t), _out_rows(s, ATTN_KV_W, BF16, t), _out_ct(s, ATTN_KV_W, BF16, t),
            _out_rows(s, ATTN_KV_W, BF16, t), _out_ct(s, ATTN_KV_W, BF16, t), _out_rows(s, RET_W, F32, t), _out_rows(s, RET_W, F32, t)]
    return _seqtiled("qkv_prep", fn, s // t, ins, outs)


def _groupnorm_gate(ry, rg, gain):
    mu = _seg_mean(ry, RET_HEAD_DIM)
    d = ry - mu
    rs = lax.rsqrt(_seg_mean(d * d, RET_HEAD_DIM) + GN_EPS)
    return d * rs, rs, _sigmoid(rg)


def _stage_mix_post(ry_f, ry_b, proj, o_ct, gain):
    s = proj.shape[0]
    t = _seq_tile(s)
    ins = [_rows(ry_f, t), _rows(ry_b, t)] + _win(proj, t, C_RG, RET_W) + [_ct(o_ct), _whole(gain)]

    def fn(i, o, a):
        ry = i[0][...] + i[1][...]
        rg = _cat(i[2:4])
        gn, _, sg = _groupnorm_gate(ry, rg, None)
        o[0][...] = (gn * i[5][...] * (rg * sg)).astype(BF16)
        o[1][...] = i[4][...].astype(F32).T.astype(BF16)

    return _seqtiled("mix_post", fn, s // t, ins, [_out_rows(s, RET_W, BF16, t), _out_rows(s, ATTN_Q_W, BF16, t)])


def _stage_mix_post_bwd(dattn, attn_rows, drz, ry_f, ry_b, proj, gain):
    s = proj.shape[0]
    t = _seq_tile(s)
    ins = ([_rows(dattn, t), _rows(attn_rows, t), _rows(drz, t), _rows(ry_f, t), _rows(ry_b, t)]
           + _win(proj, t, C_RG, RET_W) + [_whole(gain)])

    def fn(i, o, a):
        da = i[0][...]
        dat = da.T
        prod_t = (da * i[1][...].astype(F32)).T
        dat_b = dat.astype(BF16)
        zeros = jnp.zeros((ATTN_HEAD_DIM, t), BF16)
        for h in range(ATTN_HEADS):
            g = h // ATTN_GROUP
            o[0][h * LANES + g * ATTN_HEAD_DIM:h * LANES + (g + 1) * ATTN_HEAD_DIM, :] = dat_b[h * ATTN_HEAD_DIM:(h + 1) * ATTN_HEAD_DIM, :]
            o[0][h * LANES + (1 - g) * ATTN_HEAD_DIM:h * LANES + (2 - g) * ATTN_HEAD_DIM, :] = zeros
            o[1][h] = jnp.sum(prod_t[h * ATTN_HEAD_DIM:(h + 1) * ATTN_HEAD_DIM, :], axis=0, keepdims=True)
        ry = i[3][...] + i[4][...]
        rg = _cat(i[5:7])
        gain_ = i[7][...]
        gn, rs, sg = _groupnorm_gate(ry, rg, None)
        dz = i[2][...]
        silu = rg * sg
        _acc_add(a[0], dz * gn * silu)
        dgn = dz * gain_ * silu
        o[2][...] = rs * (dgn - _seg_mean(dgn, RET_HEAD_DIM) - gn * _seg_mean(dgn * gn, RET_HEAD_DIM))
        o[3][...] = (dz * gn * gain_ * (sg * (1.0 + rg * (1.0 - sg)))).astype(BF16)

    outs = [_out_ct(s, ATTN_HEADS * LANES, BF16, t),
            ((ATTN_HEADS, s // t, 1, t), F32, pl.BlockSpec((ATTN_HEADS, None, 1, t), lambda i: (0, i, 0, 0))),
            _out_rows(s, RET_W, F32, t), _out_rows(s, RET_W, BF16, t)]
    return _seqtiled("mix_post_bwd", fn, s // t, ins, outs, acc_widths=(RET_W,))


def _stage_dproj(proj, dq_ct, dk8, dv8, rgrads, drg, dga, dgr, tabs, gq_w, gk_w):
    s = proj.shape[0]
    t = _seq_tile(s)
    ca, sa, cr, sr = tabs
    kv8 = pl.BlockSpec((ATTN_HEADS, t, ATTN_KV_W), lambda i: (0, i, 0))
    ins = (_win(proj, t, C_AQ, ATTN_Q_W) + _win(proj, t, C_AK, ATTN_KV_W) + [_ct(dq_ct), (dk8, kv8), (dv8, kv8)]
           + [_rows(g, t) for g in rgrads] + [_rows(drg, t), _rows(dga, t), _rows(dgr, t)]
           + [_rows(ca, t), _rows(sa, t), _rows(cr, t), _rows(sr, t), _whole(gq_w), _whole(gk_w)])

    def fn(i, o, a):
        aq, ak = i[0][...].astype(F32), i[1][...].astype(F32)
        dq_f, dk_f, dv_f, dq_b, dk_b, dv_b = (r[...].astype(F32) for r in i[5:11])
        ca_, sa_, cr_, sr_ = i[14][...], i[15][...], i[16][...], i[17][...]
        dqn = _rope_t(i[2][...].T * ATTN_SCALE, ca_, sa_, ATTN_HEAD_DIM)
        daq, gq_rows = _headnorm_bwd(dqn, aq, i[18][...], ATTN_HEAD_DIM)
        dkn = _rope_t(jnp.sum(i[3][...].astype(F32), axis=0) * (1.0 / LOG2E), ca_, sa_, ATTN_HEAD_DIM)
        dak, gk_rows = _headnorm_bwd(dkn, ak, i[19][...], ATTN_HEAD_DIM)
        _acc_add(a[0], gq_rows)
        _acc_add(a[1], gk_rows)
        out = o[0]
        out[:, C_AQ:C_AQ + ATTN_Q_W] = daq.astype(BF16)
        out[:, C_AK:C_AK + ATTN_KV_W] = dak.astype(BF16)
        out[:, C_AV:C_AV + ATTN_KV_W] = jnp.sum(i[4][...].astype(F32), axis=0).astype(BF16)
        out[:, C_RQ:C_RQ + RET_W] = _rope_t((dq_f + dq_b) * RET_SCALE, cr_, sr_, RET_HEAD_DIM).astype(BF16)
        out[:, C_RK:C_RK + RET_W] = _rope_t(dk_f + dk_b, cr_, sr_, RET_HEAD_DIM).astype(BF16)
        out[:, C_RV:C_RV + RET_W] = (dv_f + dv_b).astype(BF16)
        out[:, C_RG:C_RG + RET_W] = i[11][...]
        out[:, C_GA:C_GA + D_MODEL] = i[12][...]
        out[:, C_GR:C_GR + D_MODEL] = i[13][...]

    return _seqtiled("dproj", fn, s // t, ins, [_out_rows(s, IN_W, BF16, t)], acc_widths=(ATTN_Q_W, ATTN_KV_W))


def _attn_fwd(q_ct, k_rows, v_ct):
    nq, _, t = q_ct.shape
    s = nq * t
    nk = nq
    assert nk % 2 == 0
    n_ch = next(n for n in (8, 4, 2) if nq % n == 0)
    halves = 2 if t % (2 * LANES) == 0 else 1
    tq = t // halves
    n_par = n_ch * halves

    def body(q_ref, k_ref, v_ref, o_ref, lse_ref, *bufs):
        sbuf = tuple(bufs[2 * w:2 * w + 2] for w in range(n_par))
        pbuf = tuple(bufs[2 * n_par + 2 * w:2 * n_par + 2 * w + 2] for w in range(n_par))

        def where(w):
            return w // halves, slice((w % halves) * tq, (w % halves + 1) * tq)

        def scores(w, j, slot):
            kj = k_ref[pl.ds(pl.multiple_of(j * t, t), t), :]
            cw, lanes = where(w)
            st = jnp.dot(kj, q_ref[cw, :, lanes], preferred_element_type=F32)
            sbuf[w][slot][...] = st
            return jnp.max(st, axis=0, keepdims=True)

        def probs(w, slot, cmax, m, l):
            m_new = jnp.maximum(m, cmax)
            alpha = jnp.exp2(m - m_new)
            pt = jnp.exp2(sbuf[w][slot][...] - m_new)
            pbuf[w][slot][...] = pt.astype(BF16)
            return m_new, alpha * l + jnp.sum(pt, axis=0, keepdims=True), alpha

        def values(w, j, slot, alpha, acc):
            return alpha * acc + jnp.dot(v_ref[j], pbuf[w][slot][...], preferred_element_type=F32)

        init = []
        for w in range(n_par):
            m = jnp.full((1, tq), -1e30, F32)
            l = jnp.zeros((1, tq), F32)
            cmax0 = scores(w, 0, 0)
            cmax1 = scores(w, 1, 1)
            m, l, alpha0 = probs(w, 0, cmax0, m, l)
            init.append((m, l, jnp.zeros((ATTN_HEAD_DIM, tq), F32), cmax1, alpha0))

        def trip(n, carry):
            c = 2 * n
            out = []
            for w in range(n_par):
                m, l, acc, cmax_b, alpha_c = carry[w]
                acc = values(w, c, 0, alpha_c, acc)
                m, l, alpha1 = probs(w, 1, cmax_b, m, l)
                cmax2 = scores(w, c + 2, 0)
                acc = values(w, c + 1, 1, alpha1, acc)
                m, l, alpha2 = probs(w, 0, cmax2, m, l)
                cmax3 = scores(w, c + 3, 1)
                out.append((m, l, acc, cmax3, alpha2))
            return tuple(out)

        res = lax.fori_loop(0, nk // 2 - 1, trip, tuple(init))
        for w in range(n_par):
            m, l, acc, cmax_b, alpha_c = res[w]
            acc = values(w, nk - 2, 0, alpha_c, acc)
            m, l, alpha1 = probs(w, 1, cmax_b, m, l)
            acc = values(w, nk - 1, 1, alpha1, acc)
            cw, lanes = where(w)
            o_ref[cw, :, lanes] = (acc / l).astype(BF16)
            lse_ref[cw, :, lanes] = m + jnp.log2(l)

    return pl.pallas_call(
        body, name="attn_fwd", grid=(ATTN_HEADS, nq // n_ch),
        in_specs=[pl.BlockSpec((n_ch, LANES, t), lambda h, i: (i, h, 0)),
                  pl.BlockSpec((s, ATTN_KV_W), lambda h, i: (0, 0)),
                  pl.BlockSpec((nk, ATTN_HEAD_DIM, t), lambda h, i: (0, h // ATTN_GROUP, 0))],
        out_specs=[pl.BlockSpec((n_ch, ATTN_HEAD_DIM, t), lambda h, i: (i, h, 0)),
                   pl.BlockSpec((None, n_ch, 1, t), lambda h, i: (h, i, 0, 0))],
        out_shape=[jax.ShapeDtypeStruct((nq, ATTN_Q_W, t), BF16), jax.ShapeDtypeStruct((ATTN_HEADS, nq, 1, t), F32)],
        scratch_shapes=[pltpu.VMEM((t, tq), F32)] * (2 * n_par) + [pltpu.VMEM((t, tq), BF16)] * (2 * n_par),
        compiler_params=_cparams(("parallel", "parallel")),
    )(q_ct, k_rows, v_ct)


def _attn_bwd(q_ct, do_ct, lse, delta, k_rows, v_rows, k_ct):
    nq, _, t = q_ct.shape
    s = nq * t
    kc = 4 if nq % 4 == 0 else 2
    tk = kc * t
    nk = nq // kc
    assert nq % 2 == 0 and nq % kc == 0

    def body(q_ref, do_ref, lse_ref, delta_ref, k_ref, v_ref, kt_ref, dq_ref, dk_ref, dv_ref, dk_acc, dv_acc,
             sb0, sb1, db0, db1, pb0, pb1, gb0, gb1):
        j = pl.program_id(1)
        sb, db, pb, gb = (sb0, sb1), (db0, db1), (pb0, pb1), (gb0, gb1)

        @pl.when(j == 0)
        def _():
            dq_ref[...] = jnp.zeros(dq_ref.shape, F32)

        kj, vj = k_ref[...], v_ref[...]
        ktj = jnp.concatenate([kt_ref[u] for u in range(kc)], axis=1)
        dk_acc[...] = jnp.zeros(dk_acc.shape, F32)
        dv_acc[...] = jnp.zeros(dv_acc.shape, F32)

        def products(i, slot):
            sb[slot][...] = jnp.dot(kj, q_ref[i], preferred_element_type=F32)
            db[slot][...] = jnp.dot(vj, do_ref[i], preferred_element_type=F32)

        def cotangents(i, slot):
            pt = jnp.exp2(sb[slot][...] - lse_ref[i])
            pb[slot][...] = pt.astype(BF16)
            gb[slot][...] = (pt * (db[slot][...] - delta_ref[i])).astype(BF16)

        def accumulate(i, slot):
            dst = gb[slot][...]
            dv_acc[...] += _nt(pb[slot][...], do_ref[i])
            dk_acc[...] += _nt(dst, q_ref[i])
            dq_ref[i] += jnp.dot(ktj, dst, preferred_element_type=F32)

        products(0, 0)
        products(1, 1)
        cotangents(0, 0)

        def trip(n, carry):
            c = 2 * n
            accumulate(c, 0)
            cotangents(c + 1, 1)
            products(c + 2, 0)
            accumulate(c + 1, 1)
            cotangents(c + 2, 0)
            products(c + 3, 1)
            return carry

        lax.fori_loop(0, nq // 2 - 1, trip, 0)
        accumulate(nq - 2, 0)
        cotangents(nq - 1, 1)
        accumulate(nq - 1, 1)
        dk_ref[...] = dk_acc[...].astype(dk_ref.dtype)
        dv_ref[...] = dv_acc[...].astype(dv_ref.dtype)

    per_head = pl.BlockSpec((nq, LANES, t), lambda h, j: (0, h, 0))
    stat = pl.BlockSpec((None, nq, 1, t), lambda h, j: (h, 0, 0, 0))
    kv_rows = pl.BlockSpec((tk, ATTN_KV_W), lambda h, j: (j, 0))
    kv_out = pl.BlockSpec((None, tk, ATTN_KV_W), lambda h, j: (h, j, 0))
    return pl.pallas_call(
        body, name="attn_bwd", grid=(ATTN_HEADS, nk),
        in_specs=[per_head, per_head, stat, stat, kv_rows, kv_rows,
                  pl.BlockSpec((kc, ATTN_HEAD_DIM, t), lambda h, j: (j, h // ATTN_GROUP, 0))],
        out_specs=[pl.BlockSpec((nq, ATTN_HEAD_DIM, t), lambda h, j: (0, h, 0)), kv_out, kv_out],
        out_shape=[jax.ShapeDtypeStruct((nq, ATTN_Q_W, t), F32), jax.ShapeDtypeStruct((ATTN_HEADS, s, ATTN_KV_W), BF16),
                   jax.ShapeDtypeStruct((ATTN_HEADS, s, ATTN_KV_W), BF16)],
        scratch_shapes=([pltpu.VMEM((tk, ATTN_KV_W), F32)] * 2 + [pltpu.VMEM((tk, t), F32)] * 4 + [pltpu.VMEM((tk, t), BF16)] * 4),
        compiler_params=_cparams(("parallel", "arbitrary")),
    )(q_ct, do_ct, lse, delta, k_rows, v_rows, k_ct)


def _log_sigmoid(x):
    t = jnp.exp(-jnp.abs(x))
    log1p_t = jnp.where(t < 1e-2, t * (1.0 - t * (0.5 - t * (1.0 / 3.0))), jnp.log(1.0 + t))
    return jnp.minimum(x, 0.0) - log1p_t


def _decay_tables(logit, backward):
    c, hd = RET_CHUNK, RET_HEAD_DIM

    def lam(shape):
        return _log_sigmoid(jnp.full(shape, logit, F32))

    ii = lax.broadcasted_iota(jnp.int32, (c, c), 0).astype(F32)
    jj = lax.broadcasted_iota(jnp.int32, (c, c), 1).astype(F32)
    pos = lax.broadcasted_iota(jnp.int32, (c, hd), 0).astype(F32)
    if not backward:
        dist, dist_t = jnp.maximum(ii - jj, 0.0), jnp.maximum(jj - ii, 0.0)
        mask, mask_t = ii >= jj, jj >= ii
        e_q, e_k = pos + 1.0, (c - 1.0) - pos
    else:
        dist, dist_t = jnp.maximum(jj - ii, 0.0), jnp.maximum(ii - jj, 0.0)
        mask, mask_t = jj > ii, ii > jj
        e_q, e_k = c - pos, pos
    lam_cc, lam_row = lam((c, c)), lam((c, hd))
    return dict(
        d=jnp.where(mask, jnp.exp(lam_cc * dist), 0.0), d_t=jnp.where(mask_t, jnp.exp(lam_cc * dist_t), 0.0), dist=dist,
        qdec=jnp.exp(lam_row * e_q), kdec=jnp.exp(lam_row * e_k), e_q=e_q, e_k=e_k, gam=jnp.exp(lam((hd, hd)) * c))


def _nt(a, b):
    return lax.dot_general(a, b, (((1,), (1,)), ((), ())), preferred_element_type=F32)


def _ret_sub(n_chunks):
    return max(1, min(n_chunks, 512 // RET_CHUNK))


def _ret_fwd(logits, q, k, proj):
    s = q.shape[0]
    c = RET_CHUNK
    sub = _ret_sub(s // c)
    nb = s // (c * sub)
    block = (lambda n: n, lambda n: nb - 1 - n)
    order = (tuple(range(sub)), tuple(reversed(range(sub))))
    vwin = _win(proj, c * sub, C_RV, RET_W)
    nv = len(vwin)
    vw = RET_W // nv
    per = 2 + nv

    def body(lg_ref, *refs):
        ins, outs, states = refs[:2 * per], refs[2 * per:2 * per + 4], refs[2 * per + 4:]

        @pl.when(pl.program_id(0) == 0)
        def _():
            for st in states:
                st[...] = jnp.zeros(st.shape, F32)

        for h in range(RET_HEADS):
            for d in range(2):
                q_ref, k_ref, v_refs = ins[d * per], ins[d * per + 1], ins[d * per + 2:(d + 1) * per]
                y_ref, st_ref, state = outs[2 * d], outs[2 * d + 1], states[d]
                tb = _decay_tables(lg_ref[d, h], bool(d))
                sl = slice(h * RET_HEAD_DIM, (h + 1) * RET_HEAD_DIM)
                off = h * RET_HEAD_DIM
                sh = state[h]
                for u in order[d]:
                    rows = slice(u * c, (u + 1) * c)
                    qh, kh = q_ref[rows, sl], k_ref[rows, sl]
                    vb = v_refs[off // vw][rows, off % vw:off % vw + RET_HEAD_DIM].astype(BF16)
                    a = _nt(qh.astype(BF16), kh.astype(BF16)) * tb["d"]
                    st_ref[u, h] = sh
                    y_ref[rows, sl] = (jnp.dot(a.astype(BF16), vb, preferred_element_type=F32)
                                       + jnp.dot((qh * tb["qdec"]).astype(BF16), sh.astype(BF16), preferred_element_type=F32))
                    sh = tb["gam"] * sh + jnp.dot((kh * tb["kdec"]).T.astype(BF16), vb, preferred_element_type=F32)
                state[h] = sh

    hmat = (RET_HEADS, RET_HEAD_DIM, RET_HEAD_DIM)
    in_specs, out_specs, args = [pl.BlockSpec(memory_space=pltpu.SMEM)], [], [logits]
    for d in range(2):
        rows = pl.BlockSpec((c * sub, RET_W), lambda n, d=d: (block[d](n), 0))
        in_specs += [rows, rows] + [pl.BlockSpec(sp.block_shape, lambda n, d=d, cb=sp.index_map(0)[1]: (block[d](n), cb)) for _, sp in vwin]
        args += [q, k] + [a for a, _ in vwin]
        out_specs += [rows, pl.BlockSpec((sub,) + hmat, lambda n, d=d: (block[d](n), 0, 0, 0))]
    return pl.pallas_call(
        body, name="ret_fwd", grid=(nb,), in_specs=in_specs, out_specs=out_specs,
        out_shape=[jax.ShapeDtypeStruct((s, RET_W), F32), jax.ShapeDtypeStruct((nb * sub,) + hmat, F32)] * 2,
        scratch_shapes=[pltpu.VMEM(hmat, F32)] * 2,
        compiler_params=_cparams(("arbitrary",)),
    )(*args)


def _ret_bwd(logits, q, k, proj, dy, st_f, st_b):
    s = q.shape[0]
    c = RET_CHUNK
    sub = _ret_sub(s // c)
    nb = s // (c * sub)
    block = (lambda n: nb - 1 - n, lambda n: n)
    order = (tuple(reversed(range(sub))), tuple(range(sub)))
    vwin = _win(proj, c * sub, C_RV, RET_W)
    nv = len(vwin)
    vw = RET_W // nv
    per = 4 + nv

    def body(lg_ref, *refs):
        ins, outs, scr = refs[:2 * per], refs[2 * per:2 * per + 8], refs[2 * per + 8:]
        n = pl.program_id(0)

        @pl.when(n == 0)
        def _():
            for r in scr:
                r[...] = jnp.zeros(r.shape, F32)

        for h in range(RET_HEADS):
            for d in range(2):
                q_ref, k_ref, dy_ref, st_ref = ins[d * per:d * per + 4]
                v_refs = ins[d * per + 4:(d + 1) * per]
                dq_ref, dk_ref, dv_ref = outs[4 * d:4 * d + 3]
                dstate, lacc = scr[2 * d], scr[2 * d + 1]
                tb = _decay_tables(lg_ref[d, h], bool(d))
                sl = slice(h * RET_HEAD_DIM, (h + 1) * RET_HEAD_DIM)
                off = h * RET_HEAD_DIM
                dsh = dstate[h]
                lsum = lacc[h, 0:1, :]
                for u in order[d]:
                    rows = slice(u * c, (u + 1) * c)
                    qh, kh, dyh = q_ref[rows, sl], k_ref[rows, sl], dy_ref[rows, sl]
                    vb = v_refs[off // vw][rows, off % vw:off % vw + RET_HEAD_DIM].astype(BF16)
                    qb, kb, dyb = qh.astype(BF16), kh.astype(BF16), dyh.astype(BF16)
                    sh = st_ref[u, h]
                    shb, dshb = sh.astype(BF16), dsh.astype(BF16)
                    qk = _nt(qb, kb)
                    g = _nt(dyb, vb) * tb["d"]
                    a_t = _nt(kb, qb) * tb["d_t"]
                    g_t = _nt(vb, dyb) * tb["d_t"]
                    qd, kd = qh * tb["qdec"], kh * tb["kdec"]
                    dqd = _nt(dyb, shb)
                    dkd = _nt(vb, dshb)
                    dq_ref[rows, sl] = (jnp.dot(g.astype(BF16), kb, preferred_element_type=F32) + dqd * tb["qdec"]).astype(dq_ref.dtype)
                    dk_ref[rows, sl] = (jnp.dot(g_t.astype(BF16), qb, preferred_element_type=F32) + dkd * tb["kdec"]).astype(dk_ref.dtype)
                    dv_ref[rows, sl] = (jnp.dot(a_t.astype(BF16), dyb, preferred_element_type=F32)
                                        + jnp.dot(kd.astype(BF16), dshb, preferred_element_type=F32)).astype(dv_ref.dtype)
                    intra = jnp.sum(tb["dist"] * qk * g, axis=0, keepdims=True)
                    lsum = (lsum + sum(intra[:, o:o + LANES] for o in range(0, c, LANES))
                            + jnp.sum(tb["e_q"] * qd * dqd + tb["e_k"] * kd * dkd, axis=0, keepdims=True)
                            + jnp.sum(float(c) * tb["gam"] * dsh * sh, axis=0, keepdims=True))
                    dsh = tb["gam"] * dsh + jnp.dot(qd.T.astype(BF16), dyb, preferred_element_type=F32)
                dstate[h] = dsh
                lacc[h, 0:1, :] = lsum

        @pl.when(n == nb - 1)
        def _():
            for d in range(2):
                for h in range(RET_HEADS):
                    outs[4 * d + 3][h] = jnp.zeros((8, LANES), F32) + jnp.sum(scr[2 * d + 1][h])

    hmat = (RET_HEADS, RET_HEAD_DIM, RET_HEAD_DIM)
    in_specs, out_specs, args = [pl.BlockSpec(memory_space=pltpu.SMEM)], [], [logits]
    for d, states in enumerate((st_f, st_b)):
        rows = pl.BlockSpec((c * sub, RET_W), lambda n, d=d: (block[d](n), 0))
        in_specs += ([rows, rows, rows, pl.BlockSpec((sub,) + hmat, lambda n, d=d: (block[d](n), 0, 0, 0))]
                     + [pl.BlockSpec(sp.block_shape, lambda n, d=d, cb=sp.index_map(0)[1]: (block[d](n), cb)) for _, sp in vwin])
        args += [q, k, dy, states] + [a for a, _ in vwin]
        out_specs += [rows, rows, rows, pl.BlockSpec((RET_HEADS, 8, LANES), lambda n: (0, 0, 0))]
    return pl.pallas_call(
        body, name="ret_bwd", grid=(nb,), in_specs=in_specs, out_specs=out_specs,
        out_shape=([jax.ShapeDtypeStruct((s, RET_W), BF16)] * 3 + [jax.ShapeDtypeStruct((RET_HEADS, 8, LANES), F32)]) * 2,
        scratch_shapes=[pltpu.VMEM(hmat, F32), pltpu.VMEM((RET_HEADS, 8, LANES), F32)] * 2,
        compiler_params=_cparams(("arbitrary",)),
    )(*args)


def _local_step(x, p, target, w, small):
    s = x.shape[0]
    tabs = _rope_tables(s, ATTN_HEAD_DIM) + _rope_tables(s, RET_HEAD_DIM)
    g_mix, g_mlp, g_ple = small["mix_norm"][None, :], small["mlp_norm"][None, :], small["ple_norm"][None, :]
    g_final, g_ret = small["final_norm"][None, :], small["ret_norm_gain"][None, :]
    gq_w = jnp.tile(small["attn_q_norm"], ATTN_HEADS)[None, :]
    gk_w = jnp.tile(small["attn_k_norm"], ATTN_KV_HEADS)[None, :]
    logits = small["ret_decay_logit"]

    hb = _stage_norm_in(x, g_mix)
    proj = _mm("in_proj", hb, w["w_in"], tm=512, tn=IN_W // 2, tk=1024, out_dtypes=(BF16,), j_outer=True)
    q_ct, k_rows, k_ct, v_rows, v_ct, rq, rk = _stage_qkv(proj, tabs, gq_w, gk_w)
    o_ct, lse = _attn_fwd(q_ct, k_rows, v_ct)
    ry_f, st_f, ry_b, st_b = _ret_fwd(logits, rq, rk, proj)
    rz, attn_rows = _stage_mix_post(ry_f, ry_b, proj, o_ct, g_ret)
    a_out = _mm("attn_o", attn_rows, w["w_attn_o"], tm=1024, tn=1024, tk=512, out_dtypes=(BF16,))
    n_gate = D_MODEL // EPI_PIECE

    def epi_merge(acc, e, c):
        ga, gr = _cat(e[1:1 + n_gate]), _cat(e[1 + n_gate:1 + 2 * n_gate])
        return acc, _sigmoid(ga) * e[0][...] + _sigmoid(gr) * acc

    r_out, merged = _mm("ret_o", rz, w["w_ret_o"], tm=512, tn=1024, tk=512, out_dtypes=(BF16, BF16), epi=epi_merge,
                        epi_ins=(a_out, (proj, C_GA), (proj, C_GR)))

    def epi_res_norm(acc, e, c):
        xr = e[0][...] + acc
        return xr, _rms_fwd(xr, c[0][...])

    x1, hm = _mm("out_proj", merged, w["w_out"], tm=512, tn=1024, tk=1024, out_dtypes=(F32, BF16),
                 epi=epi_res_norm, epi_ins=(x,), consts=(g_mlp,))

    def epi_relu2(acc, e, c):
        r = jnp.maximum(acc, 0.0)
        return (r * r,)

    act = _mm("mlp_up", hm, w["w_up"], tm=512, tn=2048, tk=1024, out_dtypes=(BF16,), epi=epi_relu2, j_outer=True)
    x2, hp = _mm("mlp_down", act, w["w_down"], tm=512, tn=1024, tk=D_FF, out_dtypes=(F32, BF16),
                 epi=epi_res_norm, epi_ins=(x1,), consts=(g_ple,))
    pe = _mm("ple_emb", p, w["w_ple"], tm=1024, tn=1024, tk=256)

    def epi_head(acc, e, c):
        gt = _sigmoid(acc)
        pe_, gf = e[0][...], c[0][...]
        x3 = e[1][...] + gt * pe_
        r3 = lax.rsqrt(jnp.mean(x3 * x3, axis=-1, keepdims=True) + NORM_EPS)
        x3n = x3 * r3
        err = x3n * gf - e[2][...]
        dy = err * (1.0 / D_MODEL)
        dyg = dy * gf
        dx3 = r3 * (dyg - x3n * jnp.mean(dyg * x3n, axis=-1, keepdims=True))
        return dx3, dx3 * pe_ * gt * (1.0 - gt), dx3 * gt, err * err, dy * x3n

    dx3, dzg, dpe, loss_cols, g_final_p = _mm("ple_gate", hp, w["w_ple_gate"], tm=512, tn=1024, tk=1024, out_dtypes=(F32, BF16, BF16),
                                              epi=epi_head, epi_ins=(pe, x2, target), consts=(g_final,), n_sums=2)
    loss_sum = 0.5 / D_MODEL * jnp.sum(loss_cols)

    gw = {}
    gw["w_ple"] = _mm("g_w_ple", p, dpe, ta=True, tm=256, tn=1024, tk=2048)
    gw["w_ple_gate"] = _mm("g_w_ple_gate", hp, dzg, ta=True, tm=1024, tn=1024, tk=2048)
    def epi_norm_bwd(acc, e, c):
        dx, dg = _rms_bwd(acc, e[0][...], c[0][...])
        return e[1][...] + dx, dg

    def epi_norm_bwd_b(acc, e, c):
        tot, dg = epi_norm_bwd(acc, e, c)
        return tot, tot, dg

    dx2, dx2_b, g_ple_p = _mm("d_hp", dzg, w["w_ple_gate"], tb=True, tm=512, tn=1024, tk=1024, out_dtypes=(F32, BF16),
                              epi=epi_norm_bwd_b, epi_ins=(x2, dx3), consts=(g_ple,), n_sums=1)

    def epi_relu2_bwd(acc, e, c):
        return (acc * (2.0 * jnp.sqrt(e[0][...]).astype(F32)),)

    du = _mm("d_u", dx2_b, w["w_down"], tb=True, tm=512, tn=2048, tk=1024, out_dtypes=(BF16,), epi=epi_relu2_bwd, epi_ins=(act,),
             j_outer=True)
    gw["w_down"] = _mm("g_w_down", act, dx2_b, ta=True, tm=1024, tn=1024, tk=4096)
    gw["w_up"] = _mm("g_w_up", hm, du, ta=True, tm=1024, tn=1024, tk=4096)
    dx1, dx1_b, g_mlp_p = _mm("d_hm", du, w["w_up"], tb=True, tm=512, tn=1024, tk=D_FF, out_dtypes=(F32, BF16),
                              epi=epi_norm_bwd_b, epi_ins=(x1, dx2), consts=(g_mlp,), n_sums=1)
    def epi_merge_bwd(acc, e, c):
        sa, sr = _sigmoid(_cat(e[2:2 + n_gate])), _sigmoid(_cat(e[2 + n_gate:2 + 2 * n_gate]))
        return acc * sa, acc * sr, acc * e[0][...] * sa * (1.0 - sa), acc * e[1][...] * sr * (1.0 - sr)

    dao, dro, dga, dgr = _mm("d_merged", dx1_b, w["w_out"], tb=True, tm=512, tn=1024, tk=1024, out_dtypes=(BF16,) * 4,
                             epi=epi_merge_bwd, epi_ins=(a_out, r_out, (proj, C_GA), (proj, C_GR)))
    gw["w_out"] = _mm("g_w_out", merged, dx1_b, ta=True, tm=1024, tn=1024, tk=2048)
    gw["w_attn_o"] = _mm("g_w_attn_o", attn_rows, dao, ta=True, tm=512, tn=1024, tk=2048)
    gw["w_ret_o"] = _mm("g_w_ret_o", rz, dro, ta=True, tm=512, tn=1024, tk=2048)
    dattn = _mm("d_attn", dao, w["w_attn_o"], tb=True, tm=1024, tn=512, tk=1024)
    drz = _mm("d_rz", dro, w["w_ret_o"], tb=True, tm=1024, tn=512, tk=1024)
    do_ct, delta, dry, drg, g_ret_p = _stage_mix_post_bwd(dattn, attn_rows, drz, ry_f, ry_b, proj, g_ret)
    dq_f, dk_f, dv_f, dl_f, dq_b, dk_b, dv_b, dl_b = _ret_bwd(logits, rq, rk, proj, dry, st_f, st_b)
    dq_ct, dk8, dv8 = _attn_bwd(q_ct, do_ct, lse, delta, k_rows, v_rows, k_ct)
    dproj, gq_p, gk_p = _stage_dproj(proj, dq_ct, dk8, dv8, (dq_f, dk_f, dv_f, dq_b, dk_b, dv_b), drg, dga, dgr, tabs, gq_w, gk_w)
    gw["w_in"] = _mm("g_w_in", hb, dproj, ta=True, tm=512, tn=IN_W // 2, tk=2048)
    grad_x, g_mix_p = _mm("d_h", dproj, w["w_in"], tb=True, tm=512, tn=1024, tk=IN_W, epi=epi_norm_bwd, epi_ins=(x, dx1),
                          consts=(g_mix,), n_sums=1)

    gs = {
        "mix_norm": g_mix_p[0], "mlp_norm": g_mlp_p[0], "ple_norm": g_ple_p[0], "final_norm": g_final_p[0],
        "ret_norm_gain": g_ret_p[0],
        "attn_q_norm": jnp.sum(gq_p[0].reshape(ATTN_HEADS, ATTN_HEAD_DIM), axis=0),
        "attn_k_norm": jnp.sum(gk_p[0].reshape(ATTN_KV_HEADS, ATTN_HEAD_DIM), axis=0),
        "ret_decay_logit": jnp.stack([dl_f[:, 0, 0], dl_b[:, 0, 0]]),
    }
    return loss_sum, grad_x, gw, gs


PACK_COLS = 1024
N_CHIPS = 4
HALF_ROWS = 2048


def _pack_shard(parts):
    return jnp.concatenate([parts[n].reshape(-1, PACK_COLS) for n, _ in BIG], axis=0)


def _unpack_shard(slab, shapes):
    out, r = {}, 0
    for n, _ in BIG:
        rows = math.prod(shapes[n]) // PACK_COLS
        out[n] = slab[r:r + rows].reshape(shapes[n])
        r += rows
    return out


def _shard_of(full, axis, sidx):
    size = full.shape[axis] // N_CHIPS
    return lax.slice_in_dim(full, sidx * size, (sidx + 1) * size, axis=axis)


def _position():
    x, y, c = lax.axis_index("x"), lax.axis_index("y"), lax.axis_index("c")
    return x, y, c


def _other_chips(x, y):
    return [(1 - x, y), (x, 1 - y), (1 - x, 1 - y)]


ANY = pl.BlockSpec(memory_space=pl.ANY)


def _gather_weights(slab):
    rows = slab.shape[0]
    half = rows // 2

    def body(in_ref, out_ref, send_sems, recv_sems):
        x, y, c = _position()
        chips = _other_chips(x, y)

        def piece(chip, core):
            return out_ref.at[2 * chip[0] + chip[1], pl.ds(core * half, half), :]

        def copy(k, chip, core, to, src=None):
            return pltpu.make_async_remote_copy(
                src_ref=piece(chip, core) if src is None else src, dst_ref=piece(chip, core),
                send_sem=send_sems.at[k], recv_sem=recv_sems.at[k], device_id=to, device_id_type=MESH)

        first = [copy(j, (x, y), c, (*chip, c), src=in_ref.at[pl.ds(c * half, half), :]) for j, chip in enumerate(chips)]
        for cp in first:
            cp.start()
        passed = [copy(3 + j, chip, c, (x, y, 1 - c)) for j, chip in enumerate(chips)]
        for j, chip in enumerate(chips):
            copy(j, chip, c, (x, y, c)).wait_recv()
            passed[j].start()
        for j, chip in enumerate(chips):
            copy(3 + j, chip, 1 - c, (x, y, c)).wait_recv()
        for cp in first + passed:
            cp.wait_send()

    return pl.pallas_call(
        body, name="gather_weights", in_specs=[ANY], out_specs=ANY,
        out_shape=jax.ShapeDtypeStruct((N_CHIPS,) + slab.shape, slab.dtype),
        scratch_shapes=[pltpu.SemaphoreType.DMA((6,)), pltpu.SemaphoreType.DMA((6,))],
    )(slab)


def _exchange_halves(g):
    def body(g_ref, out_ref, send_sem, recv_sem):
        x, y, c = _position()
        cp = pltpu.make_async_remote_copy(src_ref=g_ref.at[pl.ds(0, N_CHIPS), 1 - c], dst_ref=out_ref, send_sem=send_sem,
                                          recv_sem=recv_sem, device_id=(x, y, 1 - c), device_id_type=MESH)
        cp.start()
        cp.wait()

    return pl.pallas_call(
        body, name="exchange_halves", in_specs=[ANY], out_specs=ANY,
        out_shape=jax.ShapeDtypeStruct((N_CHIPS,) + g.shape[2:], g.dtype),
        scratch_shapes=[pltpu.SemaphoreType.DMA, pltpu.SemaphoreType.DMA],
    )(g)


def _add_my_half(g, r1, c_idx):
    tr = 256
    nt = g.shape[2] // tr
    out = (N_CHIPS,) + g.shape[2:]

    def body(c_ref, g_ref, r_ref, o_ref, ob_ref):
        tot = g_ref[...] + r_ref[...]
        o_ref[...] = tot
        ob_ref[...] = tot.astype(BF16)

    blk = (None, tr, PACK_COLS)
    spec = pl.BlockSpec(blk, lambda s, i, c_ref: (s, i, 0))
    return pl.pallas_call(
        body, name="add_my_half",
        grid_spec=pltpu.PrefetchScalarGridSpec(
            num_scalar_prefetch=1, grid=(N_CHIPS, nt),
            in_specs=[pl.BlockSpec((None,) + blk, lambda s, i, c_ref: (s, c_ref[0], i, 0)), spec],
            out_specs=[spec, spec]),
        out_shape=[jax.ShapeDtypeStruct(out, F32), jax.ShapeDtypeStruct(out, BF16)],
        compiler_params=_cparams(("parallel", "parallel")),
    )(c_idx, g, r1)


def _scatter_to_chips(part):
    def body(p_ref, out_ref, send_sems, recv_sems):
        x, y, c = _position()
        chips = _other_chips(x, y)
        sends = [pltpu.make_async_remote_copy(
            src_ref=p_ref.at[2 * chip[0] + chip[1]], dst_ref=out_ref.at[j], send_sem=send_sems.at[j], recv_sem=recv_sems.at[j],
            device_id=(*chip, c), device_id_type=MESH) for j, chip in enumerate(chips)]
        for cp in sends:
            cp.start()
        for cp in sends:
            cp.wait()

    return pl.pallas_call(
        body, name="scatter_to_chips", in_specs=[ANY], out_specs=ANY,
        out_shape=jax.ShapeDtypeStruct((N_CHIPS - 1,) + part.shape[1:], part.dtype),
        scratch_shapes=[pltpu.SemaphoreType.DMA((3,)), pltpu.SemaphoreType.DMA((3,))],
    )(part)


def _sum_chips(part, r2, chip_idx):
    tr = 256

    def body(c_ref, p_ref, r_ref, o_ref):
        o_ref[...] = ((p_ref[...] + r_ref[0]) + r_ref[1]) + r_ref[2]

    return pl.pallas_call(
        body, name="sum_chips",
        grid_spec=pltpu.PrefetchScalarGridSpec(
            num_scalar_prefetch=1, grid=(r2.shape[1] // tr,),
            in_specs=[pl.BlockSpec((None, tr, PACK_COLS), lambda i, c_ref: (c_ref[0], i, 0)),
                      pl.BlockSpec((N_CHIPS - 1, tr, PACK_COLS), lambda i, c_ref: (0, i, 0))],
            out_specs=pl.BlockSpec((tr, PACK_COLS), lambda i, c_ref: (i, 0))),
        out_shape=jax.ShapeDtypeStruct(r2.shape[1:], F32),
        compiler_params=_cparams(("parallel",)),
    )(chip_idx, part, r2)


def _join_halves(red):
    def body(r_ref, out_ref, send_sem, recv_sem):
        x, y, c = _position()
        cp = pltpu.make_async_remote_copy(src_ref=r_ref, dst_ref=out_ref, send_sem=send_sem, recv_sem=recv_sem,
                                          device_id=(x, y, 1 - c), device_id_type=MESH)
        cp.start()
        cp.wait()

    return pl.pallas_call(
        body, name="join_halves", in_specs=[ANY], out_specs=ANY,
        out_shape=jax.ShapeDtypeStruct(red.shape, red.dtype),
        scratch_shapes=[pltpu.SemaphoreType.DMA, pltpu.SemaphoreType.DMA],
    )(red)


def _adamw_math(w, g, m, v):
    m = ADAM_B1 * m + (1.0 - ADAM_B1) * g
    v = ADAM_B2 * v + (1.0 - ADAM_B2) * (g * g)
    m_hat = m / (1.0 - ADAM_B1 ** ADAM_STEP)
    v_hat = v / (1.0 - ADAM_B2 ** ADAM_STEP)
    delta = -ADAM_LR * (m_hat / (jnp.sqrt(v_hat) + ADAM_EPS) + ADAM_WD * w)
    return delta, m, v


def _adamw(name, w, g, m, v):
    tr = min(256, w.shape[0])

    def body(w_ref, g_ref, m_ref, v_ref, d_ref, nm_ref, nv_ref):
        d_ref[...], nm_ref[...], nv_ref[...] = _adamw_math(w_ref[...], g_ref[...], m_ref[...], v_ref[...])

    blk = pl.BlockSpec((tr, w.shape[1]), lambda i: (i, 0))
    return pl.pallas_call(
        body, name="adamw_" + name, grid=(w.shape[0] // tr,), in_specs=[blk] * 4, out_specs=[blk] * 3,
        out_shape=[jax.ShapeDtypeStruct(w.shape, F32)] * 3, compiler_params=_cparams(("parallel",)),
    )(w, g, m, v)


def _small_step(gpk, wpk, mpk, vpk):
    row, col, width = SMALL["ret_decay_logit"]

    def body(g_ref, w_ref, m_ref, v_ref, og_ref, od_ref, om_ref, ov_ref, gbuf, send_sems, recv_sems):
        x, y, c = _position()
        me = 4 * x + 2 * y + c
        gbuf[me] = g_ref[...]
        sends = []
        for k in range(1, 8):
            to = (x ^ (k >> 2), y ^ ((k >> 1) & 1), c ^ (k & 1))
            cp = pltpu.make_async_remote_copy(src_ref=g_ref, dst_ref=gbuf.at[me], send_sem=send_sems.at[k - 1],
                                              recv_sem=recv_sems.at[k - 1], device_id=to, device_id_type=MESH)
            cp.start()
            sends.append(cp)
        for k in range(1, 8):
            frm = me ^ k
            pltpu.make_async_remote_copy(src_ref=g_ref, dst_ref=gbuf.at[frm], send_sem=send_sems.at[k - 1],
                                         recv_sem=recv_sems.at[k - 1], device_id=(x, y, c), device_id_type=MESH).wait_recv()
        for cp in sends:
            cp.wait_send()
        tot = gbuf[0]
        for d in range(1, 8):
            tot = tot + gbuf[d]
        w = w_ref[...]
        r_i = lax.broadcasted_iota(jnp.int32, w.shape, 0)
        c_i = lax.broadcasted_iota(jnp.int32, w.shape, 1)
        is_logit = (r_i == row) & (c_i >= col) & (c_i < col + width)
        g = jnp.where(is_logit, tot * _sigmoid(-w), tot)
        og_ref[...] = g
        od_ref[...], om_ref[...], ov_ref[...] = _adamw_math(w, g, m_ref[...], v_ref[...])

    vm = pl.BlockSpec(memory_space=pltpu.VMEM)
    shp = jax.ShapeDtypeStruct(gpk.shape, F32)
    return pl.pallas_call(
        body, name="small_step", in_specs=[vm] * 4, out_specs=[vm] * 4, out_shape=[shp] * 4,
        scratch_shapes=[pltpu.VMEM((8,) + gpk.shape, F32), pltpu.SemaphoreType.DMA((7,)), pltpu.SemaphoreType.DMA((7,))],
    )(gpk, wpk, mpk, vpk)


def _pack_small(parts):
    rows = [[] for _ in range(SMALL_ROWS)]
    for n, (r, col, width) in sorted(SMALL.items(), key=lambda kv: (kv[1][0], kv[1][1])):
        rows[r].append((col, parts[n].reshape(-1).astype(F32)))
    out = []
    for r in range(SMALL_ROWS):
        segs, pos = [], 0
        for col, vec in rows[r]:
            assert col == pos
            segs.append(vec)
            pos += vec.shape[0]
        if pos < PACK_COLS:
            segs.append(jnp.zeros((PACK_COLS - pos,), F32))
        out.append(jnp.concatenate(segs))
    return jnp.stack(out)


def _unpack_small(pk, shapes):
    return {n: pk[r, col:col + width].reshape(shapes[n]) for n, (r, col, width) in SMALL.items()}


WEIGHTS = ("mix_norm", "w_in", "attn_q_norm", "attn_k_norm", "ret_decay_logit", "ret_norm_gain", "w_attn_o", "w_ret_o", "w_out",
           "mlp_norm", "w_up", "w_down", "ple_norm", "w_ple_gate", "w_ple", "final_norm")


def kernel(x, p, mix_norm, w_in, attn_q_norm, attn_k_norm, ret_decay_logit, ret_norm_gain, w_attn_o, w_ret_o, w_out, mlp_norm, w_up, w_down, ple_norm, w_ple_gate, w_ple, final_norm, loss_target, m_mix_norm, m_w_in, m_attn_q_norm, m_attn_k_norm, m_ret_decay_logit, m_ret_norm_gain, m_w_attn_o, m_w_ret_o, m_w_out, m_mlp_norm, m_w_up, m_w_down, m_ple_norm, m_w_ple_gate, m_w_ple, m_final_norm, v_mix_norm, v_w_in, v_attn_q_norm, v_attn_k_norm, v_ret_decay_logit, v_ret_norm_gain, v_w_attn_o, v_w_ret_o, v_w_out, v_mlp_norm, v_w_up, v_w_down, v_ple_norm, v_w_ple_gate, v_w_ple, v_final_norm):
    args = dict(locals())
    wts = {n: args[n] for n in WEIGHTS}
    ms = {n: args["m_" + n] for n in WEIGHTS}
    vs = {n: args["v_" + n] for n in WEIGHTS}
    shapes = {n: wts[n].shape for n in WEIGHTS}
    big_names = [n for n, _ in BIG]
    xi, yi, ci = _position()
    c_idx = ci.astype(jnp.int32).reshape(1)

    chip_idx = (2 * xi + yi).astype(jnp.int32)
    slab_b = _pack_shard({n: wts[n][0].astype(BF16) for n in big_names})
    gathered = lax.dynamic_update_slice(_gather_weights(slab_b), slab_b[None], (chip_idx, 0, 0))
    full, r0 = {}, 0
    for n, axis in BIG:
        shard = shapes[n][1:]
        rows = math.prod(shard) // PACK_COLS
        if axis == 0 and shard[1] == PACK_COLS:
            full[n] = gathered[:, r0:r0 + rows].reshape(N_CHIPS * shard[0], shard[1])
        else:
            full[n] = jnp.concatenate([gathered[k, r0:r0 + rows].reshape(shard) for k in range(N_CHIPS)], axis=axis)
        r0 += rows
    small = {n: wts[n].reshape(wts[n].shape[1:] if wts[n].ndim > 1 else wts[n].shape) for n in SMALL}

    loss_part, grad_x, gw, gs = _local_step(x[0], p[0, 0], loss_target[0], full, small)
    loss = lax.psum(loss_part, ("x", "y", "c"))

    slabs = jnp.stack([_pack_shard({n: _shard_of(gw[n], axis, k) for n, axis in BIG}) for k in range(N_CHIPS)])
    halves = slabs.reshape(N_CHIPS, 2, HALF_ROWS, PACK_COLS)
    chip_part, chip_part_b = _add_my_half(halves, _exchange_halves(halves), c_idx)
    mine = _sum_chips(chip_part, _scatter_to_chips(chip_part_b), chip_idx.reshape(1))
    both = jnp.stack([mine, _join_halves(mine)])
    reduced = jnp.where(ci == 0, both, both[::-1]).reshape(2 * HALF_ROWS, PACK_COLS)
    g_big = _unpack_shard(reduced, {n: shapes[n][1:] for n in big_names})
    big_out = [{}, {}, {}, {}]
    for n in big_names:
        big_out[0][n] = g_big[n][None]
        for kind, a in enumerate(_adamw(n, wts[n][0], g_big[n], ms[n][0], vs[n][0])):
            big_out[kind + 1][n] = a[None]

    sm_out = _small_step(_pack_small(gs), _pack_small({n: wts[n] for n in SMALL}), _pack_small({n: ms[n] for n in SMALL}),
                         _pack_small({n: vs[n] for n in SMALL}))
    small_out = [_unpack_small(a, {n: shapes[n] for n in SMALL}) for a in sm_out]

    outs = [loss, grad_x[None]]
    for kind in range(4):
        for n in WEIGHTS:
            outs.append(small_out[kind][n] if n in SMALL else big_out[kind][n])
    return tuple(outs)
```

```python
import math

import jax
import jax.numpy as jnp
from jax import lax
from jax.experimental import pallas as pl
from jax.experimental.pallas import tpu as pltpu

F32 = jnp.float32
BF16 = jnp.bfloat16
MESH = pl.DeviceIdType.MESH

D_MODEL = 1024
GRID_W = 64
ATTN_HEAD_DIM = 64
ATTN_HEADS = 8
ATTN_KV_HEADS = 2
ATTN_GROUP = ATTN_HEADS // ATTN_KV_HEADS
RET_HEAD_DIM = 128
RET_HEADS = 4
ATTN_Q_W = 512
ATTN_KV_W = 128
RET_W = 512
IN_W = 4864
D_FF = 4096
RET_CHUNK = 256
ROPE_THETA = 10000.0
NORM_EPS = 1e-6
GN_EPS = 1e-5
ATTN_SCALE = ATTN_HEAD_DIM ** -0.5
LOG2E = math.log2(math.e)
Q_FOLD = ATTN_SCALE * LOG2E
RET_SCALE = RET_HEAD_DIM ** -0.5

C_AQ, C_AK, C_AV, C_RQ, C_RK, C_RV, C_RG, C_GA, C_GR = 0, 512, 640, 768, 1280, 1792, 2304, 2816, 3840

ADAM_LR = 0.001
ADAM_B1 = 0.9
ADAM_B2 = 0.999
ADAM_EPS = 1e-08
ADAM_WD = 0.01
ADAM_STEP = 10

LANES = 128
VMEM_LIMIT = 56 << 20
SEQ_TILE = 512
EPI_PIECE = 256

BIG = (("w_in", 1), ("w_attn_o", 1), ("w_ret_o", 1), ("w_out", 0), ("w_up", 1), ("w_down", 0), ("w_ple_gate", 0), ("w_ple", 1))
SMALL_ROWS = 8
SMALL = {"mix_norm": (0, 0, 1024), "mlp_norm": (1, 0, 1024), "ple_norm": (2, 0, 1024), "final_norm": (3, 0, 1024),
         "ret_norm_gain": (4, 0, 512), "attn_q_norm": (4, 512, 64), "attn_k_norm": (4, 576, 64), "ret_decay_logit": (4, 640, 8)}


def _seq_tile(s):
    return min(SEQ_TILE, s // 2)


def _cparams(sem=None, vmem=VMEM_LIMIT):
    return pltpu.CompilerParams(dimension_semantics=sem, vmem_limit_bytes=vmem)


def _mm(name, a, b, *, ta=False, tb=False, tm, tn, tk, out_dtypes=(F32,), epi=None, epi_ins=(), consts=(), n_sums=0, j_outer=False):
    if ta:
        kdim, m = a.shape
    else:
        m, kdim = a.shape
    n = b.shape[0] if tb else b.shape[1]
    tm, tn, tk = min(tm, m), min(tn, n), min(tk, kdim)
    assert m % tm == 0 and n % tn == 0 and kdim % tk == 0, (name, m, n, kdim, tm, tn, tk)
    nk = kdim // tk
    e_arrs, e_cols = [], []
    for item in epi_ins:
        if isinstance(item, tuple):
            arr, start = item
            assert tn == n and start % EPI_PIECE == 0 and n % EPI_PIECE == 0
            for piece in range(n // EPI_PIECE):
                e_arrs.append(arr)
                e_cols.append(start // EPI_PIECE + piece)
        else:
            e_arrs.append(item)
            e_cols.append(None)
    n_e, n_c, n_o = len(e_arrs), len(consts), len(out_dtypes)
    assert n_sums == 0 or tn == n

    def body(*refs):
        a_ref, b_ref = refs[0], refs[1]
        e_refs = refs[2:2 + n_e]
        c_refs = refs[2 + n_e:2 + n_e + n_c]
        o_refs = refs[2 + n_e + n_c:2 + n_e + n_c + n_o]
        s_refs = refs[2 + n_e + n_c + n_o:2 + n_e + n_c + n_o + n_sums]
        acc_ref = refs[2 + n_e + n_c + n_o + n_sums] if nk > 1 else None
        k = pl.program_id(2)
        if n_sums:
            @pl.when((pl.program_id(1 if j_outer else 0) == 0) & (k == 0))
            def _():
                for r in s_refs:
                    r[...] = jnp.zeros(r.shape, F32)
        av = a_ref[...].astype(BF16)
        bv = b_ref[...].astype(BF16)
        dims = (((0,) if ta else (1,), (1,) if tb else (0,)), ((), ()))
        part = lax.dot_general(av, bv, dims, preferred_element_type=F32)

        def finish(acc):
            vals = epi(acc, e_refs, c_refs) if epi is not None else (acc,)
            for o_ref, v in zip(o_refs, vals[:n_o]):
                o_ref[...] = v.astype(o_ref.dtype)
            for s_ref, v in zip(s_refs, vals[n_o:]):
                _acc_add(s_ref, v)

        if nk == 1:
            finish(part)
        else:
            @pl.when(k == 0)
            def _():
                acc_ref[...] = part

            @pl.when(k > 0)
            def _():
                acc_ref[...] += part

            @pl.when(k == nk - 1)
            def _():
                finish(acc_ref[...])

    def spec(shape, index):
        return pl.BlockSpec(shape, (lambda j, i, k: index(i, j, k)) if j_outer else index)

    a_spec = spec((tk, tm), lambda i, j, k: (k, i)) if ta else spec((tm, tk), lambda i, j, k: (i, k))
    b_spec = spec((tn, tk), lambda i, j, k: (j, k)) if tb else spec((tk, tn), lambda i, j, k: (k, j))
    o_spec = spec((tm, tn), lambda i, j, k: (i, j))
    c_specs = [spec(c.shape, lambda i, j, k, nd=c.ndim: (0,) * nd) for c in consts]
    outs = pl.pallas_call(
        body, name=name,
        grid=(n // tn, m // tm, nk) if j_outer else (m // tm, n // tn, nk),
        in_specs=([a_spec, b_spec]
                  + [o_spec if cb is None else spec((tm, EPI_PIECE), lambda i, j, k, cb=cb: (i, cb)) for cb in e_cols] + c_specs),
        out_specs=[o_spec] * n_o + [spec((8, n), lambda i, j, k: (0, 0))] * n_sums,
        out_shape=[jax.ShapeDtypeStruct((m, n), dt) for dt in out_dtypes] + [jax.ShapeDtypeStruct((8, n), F32)] * n_sums,
        scratch_shapes=[pltpu.VMEM((tm, tn), F32)] if nk > 1 else [],
        compiler_params=_cparams(("arbitrary",) * 3 if n_sums else ("parallel", "parallel", "arbitrary")),
    )(a, b, *e_arrs, *consts)
    return outs[0] if n_o + n_sums == 1 else outs


def _rows(arr, tr):
    return (arr, pl.BlockSpec((tr, arr.shape[1]), lambda i: (i, 0)))


def _win(arr, tr, start, width):
    bw = math.gcd(start, width) if start else width
    assert bw % LANES == 0
    return [(arr, pl.BlockSpec((tr, bw), lambda i, cb=start // bw + p: (i, cb))) for p in range(width // bw)]


def _ct(arr):
    return (arr, pl.BlockSpec((None,) + arr.shape[1:], lambda i: (i, 0, 0)))


def _whole(arr):
    return (arr, pl.BlockSpec(arr.shape, lambda i, nd=arr.ndim: (0,) * nd))


def _cat(refs):
    vals = [r[...].astype(F32) for r in refs]
    return vals[0] if len(vals) == 1 else jnp.concatenate(vals, axis=1)


def _seqtiled(name, fn, n_tiles, ins, outs, acc_widths=()):
    n_i, n_o, n_a = len(ins), len(outs), len(acc_widths)

    def body(*refs):
        i_refs, o_refs, a_refs = refs[:n_i], refs[n_i:n_i + n_o], refs[n_i + n_o:]
        if n_a:
            @pl.when(pl.program_id(0) == 0)
            def _():
                for r in a_refs:
                    r[...] = jnp.zeros(r.shape, F32)
        fn(list(i_refs), list(o_refs), list(a_refs))

    res = pl.pallas_call(
        body, name=name, grid=(n_tiles,),
        in_specs=[s for _, s in ins],
        out_specs=[s for _, _, s in outs] + [pl.BlockSpec((8, w), lambda i: (0, 0)) for w in acc_widths],
        out_shape=[jax.ShapeDtypeStruct(sh, dt) for sh, dt, _ in outs] + [jax.ShapeDtypeStruct((8, w), F32) for w in acc_widths],
        compiler_params=_cparams(("arbitrary",)),
    )(*[a for a, _ in ins])
    return res


def _acc_add(acc_ref, val):
    acc_ref[0:1, :] += jnp.sum(val, axis=0, keepdims=True)


def _out_rows(s, w, dt, tr):
    return ((s, w), dt, pl.BlockSpec((tr, w), lambda i: (i, 0)))


def _out_ct(s, w, dt, t):
    return ((s // t, w, t), dt, pl.BlockSpec((None, w, t), lambda i: (i, 0, 0)))


def _rms_fwd(x, gain):
    r = lax.rsqrt(jnp.mean(x * x, axis=-1, keepdims=True) + NORM_EPS)
    return x * r * gain


def _rms_bwd(dy, x, gain):
    r = lax.rsqrt(jnp.mean(x * x, axis=-1, keepdims=True) + NORM_EPS)
    xn = x * r
    dyg = dy * gain
    dx = r * (dyg - xn * jnp.mean(dyg * xn, axis=-1, keepdims=True))
    return dx, dy * xn


def _seg_mean(y, hd):
    w = y.shape[1]
    pieces = []
    for s in range(0, w, LANES):
        v = y[:, s:s + LANES]
        tot = jnp.sum(v, axis=1, keepdims=True)
        if hd == LANES:
            pieces.append(jnp.broadcast_to(tot, v.shape))
        else:
            low = lax.broadcasted_iota(jnp.int32, v.shape, 1) < hd
            lo = jnp.sum(jnp.where(low, v, 0.0), axis=1, keepdims=True)
            pieces.append(jnp.where(low, lo, tot - lo))
    out = pieces[0] if len(pieces) == 1 else jnp.concatenate(pieces, axis=1)
    return out * (1.0 / hd)


def _tile_lanes(t, w):
    return t if w == t.shape[1] else jnp.concatenate([t] * (w // t.shape[1]), axis=1)


def _swap_halves(x, hd):
    w = x.shape[1]
    half = hd // 2
    lane = lax.broadcasted_iota(jnp.int32, x.shape, 1)
    return jnp.where((lane % hd) < half, pltpu.roll(x, w - half, 1), pltpu.roll(x, half, 1))


def _rope(x, cos, sin_signed, hd):
    w = x.shape[1]
    return x * _tile_lanes(cos, w) + _swap_halves(x, hd) * _tile_lanes(sin_signed, w)


def _rope_t(dy, cos, sin_signed, hd):
    w = dy.shape[1]
    return dy * _tile_lanes(cos, w) + _swap_halves(dy * _tile_lanes(sin_signed, w), hd)


def _headnorm_fwd(x, gain_w, hd):
    r = lax.rsqrt(_seg_mean(x * x, hd) + NORM_EPS)
    return x * r * gain_w


def _headnorm_bwd(dy, x, gain_w, hd):
    r = lax.rsqrt(_seg_mean(x * x, hd) + NORM_EPS)
    xn = x * r
    dyg = dy * gain_w
    return r * (dyg - xn * _seg_mean(dyg * xn, hd)), dy * xn


def _sigmoid(x):
    return 1.0 / (1.0 + jnp.exp(-x))


def _rope_tables(seq_len, head_dim):
    rows = seq_len // GRID_W
    n_axis = head_dim // 4
    freqs = ROPE_THETA ** (-jnp.arange(n_axis, dtype=F32) / n_axis)
    ang_r = jnp.arange(rows, dtype=F32)[:, None] * freqs
    ang_c = jnp.arange(GRID_W, dtype=F32)[:, None] * freqs

    def expand(by_row, by_col):
        r = jnp.broadcast_to(by_row[:, None, :], (rows, GRID_W, n_axis))
        c = jnp.broadcast_to(by_col[None, :, :], (rows, GRID_W, n_axis))
        return jnp.concatenate([r, c], axis=-1).reshape(seq_len, 2 * n_axis)

    cos, sin = expand(jnp.cos(ang_r), jnp.cos(ang_c)), expand(jnp.sin(ang_r), jnp.sin(ang_c))
    reps = LANES // head_dim
    return jnp.tile(jnp.concatenate([cos, cos], axis=-1), (1, reps)), jnp.tile(jnp.concatenate([-sin, sin], axis=-1), (1, reps))


def _stage_norm_in(x, gain):
    s = x.shape[0]
    tr = min(SEQ_TILE, s)

    def fn(i, o, a):
        o[0][...] = _rms_fwd(i[0][...], i[1][...]).astype(BF16)

    return _seqtiled("norm_in", fn, s // tr, [_rows(x, tr), _whole(gain)], [_out_rows(s, D_MODEL, BF16, tr)])[0]


def _stage_qkv(proj, tabs, gq_w, gk_w):
    s = proj.shape[0]
    t = _seq_tile(s)
    ca, sa, cr, sr = tabs
    ins = (_win(proj, t, C_AQ, ATTN_Q_W) + _win(proj, t, C_AK, ATTN_KV_W) + _win(proj, t, C_AV, ATTN_KV_W)
           + _win(proj, t, C_RQ, RET_W) + _win(proj, t, C_RK, RET_W)
           + [_rows(ca, t), _rows(sa, t), _rows(cr, t), _rows(sr, t), _whole(gq_w), _whole(gk_w)])

    def fn(i, o, a):
        aq, ak, av = (i[n][...].astype(F32) for n in range(3))
        rq, rk = _cat(i[3:5]), _cat(i[5:7])
        ca_, sa_, cr_, sr_ = i[7][...], i[8][...], i[9][...], i[10][...]
        qr = _rope(_headnorm_fwd(aq, i[11][...], ATTN_HEAD_DIM), ca_, sa_, ATTN_HEAD_DIM) * Q_FOLD
        kr = _rope(_headnorm_fwd(ak, i[12][...], ATTN_HEAD_DIM), ca_, sa_, ATTN_HEAD_DIM)
        qt = qr.T.astype(BF16)
        zeros = jnp.zeros((ATTN_HEAD_DIM, t), BF16)
        for h in range(ATTN_HEADS):
            g = h // ATTN_GROUP
            blk = qt[h * ATTN_HEAD_DIM:(h + 1) * ATTN_HEAD_DIM, :]
            o[0][h * LANES + g * ATTN_HEAD_DIM:h * LANES + (g + 1) * ATTN_HEAD_DIM, :] = blk
            o[0][h * LANES + (1 - g) * ATTN_HEAD_DIM:h * LANES + (2 - g) * ATTN_HEAD_DIM, :] = zeros
        o[1][...] = kr.astype(BF16)
        o[2][...] = kr.T.astype(BF16)
        o[3][...] = av.astype(BF16)
        o[4][...] = av.T.astype(BF16)
        o[5][...] = _rope(rq, cr_, sr_, RET_HEAD_DIM) * RET_SCALE
        o[6][...] = _rope(rk, cr_, sr_, RET_HEAD_DIM)

    outs = [_out_ct(s, ATTN_HEADS * LANES, BF16, t), _out_rows(s, ATTN_KV_W, BF16, t), _out_ct(s, ATTN_KV_W, BF16, t),
            _out_rows(s, ATTN_KV_W, BF16, t), _out_ct(s, ATTN_KV_W, BF16, t), _out_rows(s, RET_W, F32, t), _out_rows(s, RET_W, F32, t)]
    return _seqtiled("qkv_prep", fn, s // t, ins, outs)


def _groupnorm_gate(ry, rg, gain):
    mu = _seg_mean(ry, RET_HEAD_DIM)
    d = ry - mu
    rs = lax.rsqrt(_seg_mean(d * d, RET_HEAD_DIM) + GN_EPS)
    return d * rs, rs, _sigmoid(rg)


def _stage_mix_post(ry_f, ry_b, proj, o_ct, gain):
    s = proj.shape[0]
    t = _seq_tile(s)
    ins = [_rows(ry_f, t), _rows(ry_b, t)] + _win(proj, t, C_RG, RET_W) + [_ct(o_ct), _whole(gain)]

    def fn(i, o, a):
        ry = i[0][...] + i[1][...]
        rg = _cat(i[2:4])
        gn, _, sg = _groupnorm_gate(ry, rg, None)
        o[0][...] = (gn * i[5][...] * (rg * sg)).astype(BF16)
        o[1][...] = i[4][...].astype(F32).T.astype(BF16)

    return _seqtiled("mix_post", fn, s // t, ins, [_out_rows(s, RET_W, BF16, t), _out_rows(s, ATTN_Q_W, BF16, t)])


def _stage_mix_post_bwd(dattn, attn_rows, drz, ry_f, ry_b, proj, gain):
    s = proj.shape[0]
    t = _seq_tile(s)
    ins = ([_rows(dattn, t), _rows(attn_rows, t), _rows(drz, t), _rows(ry_f, t), _rows(ry_b, t)]
           + _win(proj, t, C_RG, RET_W) + [_whole(gain)])

    def fn(i, o, a):
        da = i[0][...]
        dat = da.T
        prod_t = (da * i[1][...].astype(F32)).T
        dat_b = dat.astype(BF16)
        zeros = jnp.zeros((ATTN_HEAD_DIM, t), BF16)
        for h in range(ATTN_HEADS):
            g = h // ATTN_GROUP
            o[0][h * LANES + g * ATTN_HEAD_DIM:h * LANES + (g + 1) * ATTN_HEAD_DIM, :] = dat_b[h * ATTN_HEAD_DIM:(h + 1) * ATTN_HEAD_DIM, :]
            o[0][h * LANES + (1 - g) * ATTN_HEAD_DIM:h * LANES + (2 - g) * ATTN_HEAD_DIM, :] = zeros
            o[1][h] = jnp.sum(prod_t[h * ATTN_HEAD_DIM:(h + 1) * ATTN_HEAD_DIM, :], axis=0, keepdims=True)
        ry = i[3][...] + i[4][...]
        rg = _cat(i[5:7])
        gain_ = i[7][...]
        gn, rs, sg = _groupnorm_gate(ry, rg, None)
        dz = i[2][...]
        silu = rg * sg
        _acc_add(a[0], dz * gn * silu)
        dgn = dz * gain_ * silu
        o[2][...] = rs * (dgn - _seg_mean(dgn, RET_HEAD_DIM) - gn * _seg_mean(dgn * gn, RET_HEAD_DIM))
        o[3][...] = (dz * gn * gain_ * (sg * (1.0 + rg * (1.0 - sg)))).astype(BF16)

    outs = [_out_ct(s, ATTN_HEADS * LANES, BF16, t),
            ((ATTN_HEADS, s // t, 1, t), F32, pl.BlockSpec((ATTN_HEADS, None, 1, t), lambda i: (0, i, 0, 0))),
            _out_rows(s, RET_W, F32, t), _out_rows(s, RET_W, BF16, t)]
    return _seqtiled("mix_post_bwd", fn, s // t, ins, outs, acc_widths=(RET_W,))


def _stage_dproj(proj, dq_ct, dk8, dv8, rgrads, drg, dga, dgr, tabs, gq_w, gk_w):
    s = proj.shape[0]
    t = _seq_tile(s)
    ca, sa, cr, sr = tabs
    kv8 = pl.BlockSpec((ATTN_HEADS, t, ATTN_KV_W), lambda i: (0, i, 0))
    ins = (_win(proj, t, C_AQ, ATTN_Q_W) + _win(proj, t, C_AK, ATTN_KV_W) + [_ct(dq_ct), (dk8, kv8), (dv8, kv8)]
           + [_rows(g, t) for g in rgrads] + [_rows(drg, t), _rows(dga, t), _rows(dgr, t)]
           + [_rows(ca, t), _rows(sa, t), _rows(cr, t), _rows(sr, t), _whole(gq_w), _whole(gk_w)])

    def fn(i, o, a):
        aq, ak = i[0][...].astype(F32), i[1][...].astype(F32)
        dq_f, dk_f, dv_f, dq_b, dk_b, dv_b = (r[...].astype(F32) for r in i[5:11])
        ca_, sa_, cr_, sr_ = i[14][...], i[15][...], i[16][...], i[17][...]
        dqn = _rope_t(i[2][...].T * ATTN_SCALE, ca_, sa_, ATTN_HEAD_DIM)
        daq, gq_rows = _headnorm_bwd(dqn, aq, i[18][...], ATTN_HEAD_DIM)
        dkn = _rope_t(jnp.sum(i[3][...].astype(F32), axis=0) * (1.0 / LOG2E), ca_, sa_, ATTN_HEAD_DIM)
        dak, gk_rows = _headnorm_bwd(dkn, ak, i[19][...], ATTN_HEAD_DIM)
        _acc_add(a[0], gq_rows)
        _acc_add(a[1], gk_rows)
        out = o[0]
        out[:, C_AQ:C_AQ + ATTN_Q_W] = daq.astype(BF16)
        out[:, C_AK:C_AK + ATTN_KV_W] = dak.astype(BF16)
        out[:, C_AV:C_AV + ATTN_KV_W] = jnp.sum(i[4][...].astype(F32), axis=0).astype(BF16)
        out[:, C_RQ:C_RQ + RET_W] = _rope_t((dq_f + dq_b) * RET_SCALE, cr_, sr_, RET_HEAD_DIM).astype(BF16)
        out[:, C_RK:C_RK + RET_W] = _rope_t(dk_f + dk_b, cr_, sr_, RET_HEAD_DIM).astype(BF16)
        out[:, C_RV:C_RV + RET_W] = (dv_f + dv_b).astype(BF16)
        out[:, C_RG:C_RG + RET_W] = i[11][...]
        out[:, C_GA:C_GA + D_MODEL] = i[12][...]
        out[:, C_GR:C_GR + D_MODEL] = i[13][...]

    return _seqtiled("dproj", fn, s // t, ins, [_out_rows(s, IN_W, BF16, t)], acc_widths=(ATTN_Q_W, ATTN_KV_W))


def _attn_fwd(q_ct, k_rows, v_ct):
    nq, _, t = q_ct.shape
    s = nq * t
    nk = nq
    assert nk % 2 == 0
    n_ch = next(n for n in (8, 4, 2) if nq % n == 0)
    halves = 2 if t % (2 * LANES) == 0 else 1
    tq = t // halves
    n_par = n_ch * halves

    def body(q_ref, k_ref, v_ref, o_ref, lse_ref, *bufs):
        sbuf = tuple(bufs[2 * w:2 * w + 2] for w in range(n_par))
        pbuf = tuple(bufs[2 * n_par + 2 * w:2 * n_par + 2 * w + 2] for w in range(n_par))

        def where(w):
            return w // halves, slice((w % halves) * tq, (w % halves + 1) * tq)

        def scores(w, j, slot):
            kj = k_ref[pl.ds(pl.multiple_of(j * t, t), t), :]
            cw, lanes = where(w)
            st = jnp.dot(kj, q_ref[cw, :, lanes], preferred_element_type=F32)
            sbuf[w][slot][...] = st
            return jnp.max(st, axis=0, keepdims=True)

        def probs(w, slot, cmax, m, l):
            m_new = jnp.maximum(m, cmax)
            alpha = jnp.exp2(m - m_new)
            pt = jnp.exp2(sbuf[w][slot][...] - m_new)
            pbuf[w][slot][...] = pt.astype(BF16)
            return m_new, alpha * l + jnp.sum(pt, axis=0, keepdims=True), alpha

        def values(w, j, slot, alpha, acc):
            return alpha * acc + jnp.dot(v_ref[j], pbuf[w][slot][...], preferred_element_type=F32)

        init = []
        for w in range(n_par):
            m = jnp.full((1, tq), -1e30, F32)
            l = jnp.zeros((1, tq), F32)
            cmax0 = scores(w, 0, 0)
            cmax1 = scores(w, 1, 1)
            m, l, alpha0 = probs(w, 0, cmax0, m, l)
            init.append((m, l, jnp.zeros((ATTN_HEAD_DIM, tq), F32), cmax1, alpha0))

        def trip(n, carry):
            c = 2 * n
            out = []
            for w in range(n_par):
                m, l, acc, cmax_b, alpha_c = carry[w]
                acc = values(w, c, 0, alpha_c, acc)
                m, l, alpha1 = probs(w, 1, cmax_b, m, l)
                cmax2 = scores(w, c + 2, 0)
                acc = values(w, c + 1, 1, alpha1, acc)
                m, l, alpha2 = probs(w, 0, cmax2, m, l)
                cmax3 = scores(w, c + 3, 1)
                out.append((m, l, acc, cmax3, alpha2))
            return tuple(out)

        res = lax.fori_loop(0, nk // 2 - 1, trip, tuple(init))
        for w in range(n_par):
            m, l, acc, cmax_b, alpha_c = res[w]
            acc = values(w, nk - 2, 0, alpha_c, acc)
            m, l, alpha1 = probs(w, 1, cmax_b, m, l)
            acc = values(w, nk - 1, 1, alpha1, acc)
            cw, lanes = where(w)
            o_ref[cw, :, lanes] = (acc / l).astype(BF16)
            lse_ref[cw, :, lanes] = m + jnp.log2(l)

    return pl.pallas_call(
        body, name="attn_fwd", grid=(ATTN_HEADS, nq // n_ch),
        in_specs=[pl.BlockSpec((n_ch, LANES, t), lambda h, i: (i, h, 0)),
                  pl.BlockSpec((s, ATTN_KV_W), lambda h, i: (0, 0)),
                  pl.BlockSpec((nk, ATTN_HEAD_DIM, t), lambda h, i: (0, h // ATTN_GROUP, 0))],
        out_specs=[pl.BlockSpec((n_ch, ATTN_HEAD_DIM, t), lambda h, i: (i, h, 0)),
                   pl.BlockSpec((None, n_ch, 1, t), lambda h, i: (h, i, 0, 0))],
        out_shape=[jax.ShapeDtypeStruct((nq, ATTN_Q_W, t), BF16), jax.ShapeDtypeStruct((ATTN_HEADS, nq, 1, t), F32)],
        scratch_shapes=[pltpu.VMEM((t, tq), F32)] * (2 * n_par) + [pltpu.VMEM((t, tq), BF16)] * (2 * n_par),
        compiler_params=_cparams(("parallel", "parallel")),
    )(q_ct, k_rows, v_ct)


def _attn_bwd(q_ct, do_ct, lse, delta, k_rows, v_rows, k_ct):
    nq, _, t = q_ct.shape
    s = nq * t
    kc = 4 if nq % 4 == 0 else 2
    tk = kc * t
    nk = nq // kc
    assert nq % 2 == 0 and nq % kc == 0

    def body(q_ref, do_ref, lse_ref, delta_ref, k_ref, v_ref, kt_ref, dq_ref, dk_ref, dv_ref, dk_acc, dv_acc,
             sb0, sb1, db0, db1, pb0, pb1, gb0, gb1):
        j = pl.program_id(1)
        sb, db, pb, gb = (sb0, sb1), (db0, db1), (pb0, pb1), (gb0, gb1)

        @pl.when(j == 0)
        def _():
            dq_ref[...] = jnp.zeros(dq_ref.shape, F32)

        kj, vj = k_ref[...], v_ref[...]
        ktj = jnp.concatenate([kt_ref[u] for u in range(kc)], axis=1)
        dk_acc[...] = jnp.zeros(dk_acc.shape, F32)
        dv_acc[...] = jnp.zeros(dv_acc.shape, F32)

        def products(i, slot):
            sb[slot][...] = jnp.dot(kj, q_ref[i], preferred_element_type=F32)
            db[slot][...] = jnp.dot(vj, do_ref[i], preferred_element_type=F32)

        def cotangents(i, slot):
            pt = jnp.exp2(sb[slot][...] - lse_ref[i])
            pb[slot][...] = pt.astype(BF16)
            gb[slot][...] = (pt * (db[slot][...] - delta_ref[i])).astype(BF16)

        def accumulate(i, slot):
            dst = gb[slot][...]
            dv_acc[...] += _nt(pb[slot][...], do_ref[i])
            dk_acc[...] += _nt(dst, q_ref[i])
            dq_ref[i] += jnp.dot(ktj, dst, preferred_element_type=F32)

        products(0, 0)
        products(1, 1)
        cotangents(0, 0)

        def trip(n, carry):
            c = 2 * n
            accumulate(c, 0)
            cotangents(c + 1, 1)
            products(c + 2, 0)
            accumulate(c + 1, 1)
            cotangents(c + 2, 0)
            products(c + 3, 1)
            return carry

        lax.fori_loop(0, nq // 2 - 1, trip, 0)
        accumulate(nq - 2, 0)
        cotangents(nq - 1, 1)
        accumulate(nq - 1, 1)
        dk_ref[...] = dk_acc[...].astype(dk_ref.dtype)
        dv_ref[...] = dv_acc[...].astype(dv_ref.dtype)

    per_head = pl.BlockSpec((nq, LANES, t), lambda h, j: (0, h, 0))
    stat = pl.BlockSpec((None, nq, 1, t), lambda h, j: (h, 0, 0, 0))
    kv_rows = pl.BlockSpec((tk, ATTN_KV_W), lambda h, j: (j, 0))
    kv_out = pl.BlockSpec((None, tk, ATTN_KV_W), lambda h, j: (h, j, 0))
    return pl.pallas_call(
        body, name="attn_bwd", grid=(ATTN_HEADS, nk),
        in_specs=[per_head, per_head, stat, stat, kv_rows, kv_rows,
                  pl.BlockSpec((kc, ATTN_HEAD_DIM, t), lambda h, j: (j, h // ATTN_GROUP, 0))],
        out_specs=[pl.BlockSpec((nq, ATTN_HEAD_DIM, t), lambda h, j: (0, h, 0)), kv_out, kv_out],
        out_shape=[jax.ShapeDtypeStruct((nq, ATTN_Q_W, t), F32), jax.ShapeDtypeStruct((ATTN_HEADS, s, ATTN_KV_W), BF16),
                   jax.ShapeDtypeStruct((ATTN_HEADS, s, ATTN_KV_W), BF16)],
        scratch_shapes=([pltpu.VMEM((tk, ATTN_KV_W), F32)] * 2 + [pltpu.VMEM((tk, t), F32)] * 4 + [pltpu.VMEM((tk, t), BF16)] * 4),
        compiler_params=_cparams(("parallel", "arbitrary")),
    )(q_ct, do_ct, lse, delta, k_rows, v_rows, k_ct)


def _log_sigmoid(x):
    t = jnp.exp(-jnp.abs(x))
    log1p_t = jnp.where(t < 1e-2, t * (1.0 - t * (0.5 - t * (1.0 / 3.0))), jnp.log(1.0 + t))
    return jnp.minimum(x, 0.0) - log1p_t


def _decay_tables(logit, backward):
    c, hd = RET_CHUNK, RET_HEAD_DIM

    def lam(shape):
        return _log_sigmoid(jnp.full(shape, logit, F32))

    ii = lax.broadcasted_iota(jnp.int32, (c, c), 0).astype(F32)
    jj = lax.broadcasted_iota(jnp.int32, (c, c), 1).astype(F32)
    pos = lax.broadcasted_iota(jnp.int32, (c, hd), 0).astype(F32)
    if not backward:
        dist, dist_t = jnp.maximum(ii - jj, 0.0), jnp.maximum(jj - ii, 0.0)
        mask, mask_t = ii >= jj, jj >= ii
        e_q, e_k = pos + 1.0, (c - 1.0) - pos
    else:
        dist, dist_t = jnp.maximum(jj - ii, 0.0), jnp.maximum(ii - jj, 0.0)
        mask, mask_t = jj > ii, ii > jj
        e_q, e_k = c - pos, pos
    lam_cc, lam_row = lam((c, c)), lam((c, hd))
    return dict(
        d=jnp.where(mask, jnp.exp(lam_cc * dist), 0.0), d_t=jnp.where(mask_t, jnp.exp(lam_cc * dist_t), 0.0), dist=dist,
        qdec=jnp.exp(lam_row * e_q), kdec=jnp.exp(lam_row * e_k), e_q=e_q, e_k=e_k, gam=jnp.exp(lam((hd, hd)) * c))


def _nt(a, b):
    return lax.dot_general(a, b, (((1,), (1,)), ((), ())), preferred_element_type=F32)


def _ret_sub(n_chunks):
    return max(1, min(n_chunks, 512 // RET_CHUNK))


def _ret_fwd(logits, q, k, proj):
    s = q.shape[0]
    c = RET_CHUNK
    sub = _ret_sub(s // c)
    nb = s // (c * sub)
    block = (lambda n: n, lambda n: nb - 1 - n)
    order = (tuple(range(sub)), tuple(reversed(range(sub))))
    vwin = _win(proj, c * sub, C_RV, RET_W)
    nv = len(vwin)
    vw = RET_W // nv
    per = 2 + nv

    def body(lg_ref, *refs):
        ins, outs, states = refs[:2 * per], refs[2 * per:2 * per + 4], refs[2 * per + 4:]

        @pl.when(pl.program_id(0) == 0)
        def _():
            for st in states:
                st[...] = jnp.zeros(st.shape, F32)

        for h in range(RET_HEADS):
            for d in range(2):
                q_ref, k_ref, v_refs = ins[d * per], ins[d * per + 1], ins[d * per + 2:(d + 1) * per]
                y_ref, st_ref, state = outs[2 * d], outs[2 * d + 1], states[d]
                tb = _decay_tables(lg_ref[d, h], bool(d))
                sl = slice(h * RET_HEAD_DIM, (h + 1) * RET_HEAD_DIM)
                off = h * RET_HEAD_DIM
                sh = state[h]
                for u in order[d]:
                    rows = slice(u * c, (u + 1) * c)
                    qh, kh = q_ref[rows, sl], k_ref[rows, sl]
                    vb = v_refs[off // vw][rows, off % vw:off % vw + RET_HEAD_DIM].astype(BF16)
                    a = _nt(qh.astype(BF16), kh.astype(BF16)) * tb["d"]
                    st_ref[u, h] = sh
                    y_ref[rows, sl] = (jnp.dot(a.astype(BF16), vb, preferred_element_type=F32)
                                       + jnp.dot((qh * tb["qdec"]).astype(BF16), sh.astype(BF16), preferred_element_type=F32))
                    sh = tb["gam"] * sh + jnp.dot((kh * tb["kdec"]).T.astype(BF16), vb, preferred_element_type=F32)
                state[h] = sh

    hmat = (RET_HEADS, RET_HEAD_DIM, RET_HEAD_DIM)
    in_specs, out_specs, args = [pl.BlockSpec(memory_space=pltpu.SMEM)], [], [logits]
    for d in range(2):
        rows = pl.BlockSpec((c * sub, RET_W), lambda n, d=d: (block[d](n), 0))
        in_specs += [rows, rows] + [pl.BlockSpec(sp.block_shape, lambda n, d=d, cb=sp.index_map(0)[1]: (block[d](n), cb)) for _, sp in vwin]
        args += [q, k] + [a for a, _ in vwin]
        out_specs += [rows, pl.BlockSpec((sub,) + hmat, lambda n, d=d: (block[d](n), 0, 0, 0))]
    return pl.pallas_call(
        body, name="ret_fwd", grid=(nb,), in_specs=in_specs, out_specs=out_specs,
        out_shape=[jax.ShapeDtypeStruct((s, RET_W), F32), jax.ShapeDtypeStruct((nb * sub,) + hmat, F32)] * 2,
        scratch_shapes=[pltpu.VMEM(hmat, F32)] * 2,
        compiler_params=_cparams(("arbitrary",)),
    )(*args)


def _ret_bwd(logits, q, k, proj, dy, st_f, st_b):
    s = q.shape[0]
    c = RET_CHUNK
    sub = _ret_sub(s // c)
    nb = s // (c * sub)
    block = (lambda n: nb - 1 - n, lambda n: n)
    order = (tuple(reversed(range(sub))), tuple(range(sub)))
    vwin = _win(proj, c * sub, C_RV, RET_W)
    nv = len(vwin)
    vw = RET_W // nv
    per = 4 + nv

    def body(lg_ref, *refs):
        ins, outs, scr = refs[:2 * per], refs[2 * per:2 * per + 8], refs[2 * per + 8:]
        n = pl.program_id(0)

        @pl.when(n == 0)
        def _():
            for r in scr:
                r[...] = jnp.zeros(r.shape, F32)

        for h in range(RET_HEADS):
            for d in range(2):
                q_ref, k_ref, dy_ref, st_ref = ins[d * per:d * per + 4]
                v_refs = ins[d * per + 4:(d + 1) * per]
                dq_ref, dk_ref, dv_ref = outs[4 * d:4 * d + 3]
                dstate, lacc = scr[2 * d], scr[2 * d + 1]
                tb = _decay_tables(lg_ref[d, h], bool(d))
                sl = slice(h * RET_HEAD_DIM, (h + 1) * RET_HEAD_DIM)
                off = h * RET_HEAD_DIM
                dsh = dstate[h]
                lsum = lacc[h, 0:1, :]
                for u in order[d]:
                    rows = slice(u * c, (u + 1) * c)
                    qh, kh, dyh = q_ref[rows, sl], k_ref[rows, sl], dy_ref[rows, sl]
                    vb = v_refs[off // vw][rows, off % vw:off % vw + RET_HEAD_DIM].astype(BF16)
                    qb, kb, dyb = qh.astype(BF16), kh.astype(BF16), dyh.astype(BF16)
                    sh = st_ref[u, h]
                    shb, dshb = sh.astype(BF16), dsh.astype(BF16)
                    qk = _nt(qb, kb)
                    g = _nt(dyb, vb) * tb["d"]
                    a_t = _nt(kb, qb) * tb["d_t"]
                    g_t = _nt(vb, dyb) * tb["d_t"]
                    qd, kd = qh * tb["qdec"], kh * tb["kdec"]
                    dqd = _nt(dyb, shb)
                    dkd = _nt(vb, dshb)
                    dq_ref[rows, sl] = (jnp.dot(g.astype(BF16), kb, preferred_element_type=F32) + dqd * tb["qdec"]).astype(dq_ref.dtype)
                    dk_ref[rows, sl] = (jnp.dot(g_t.astype(BF16), qb, preferred_element_type=F32) + dkd * tb["kdec"]).astype(dk_ref.dtype)
                    dv_ref[rows, sl] = (jnp.dot(a_t.astype(BF16), dyb, preferred_element_type=F32)
                                        + jnp.dot(kd.astype(BF16), dshb, preferred_element_type=F32)).astype(dv_ref.dtype)
                    intra = jnp.sum(tb["dist"] * qk * g, axis=0, keepdims=True)
                    lsum = (lsum + sum(intra[:, o:o + LANES] for o in range(0, c, LANES))
                            + jnp.sum(tb["e_q"] * qd * dqd + tb["e_k"] * kd * dkd, axis=0, keepdims=True)
                            + jnp.sum(float(c) * tb["gam"] * dsh * sh, axis=0, keepdims=True))
                    dsh = tb["gam"] * dsh + jnp.dot(qd.T.astype(BF16), dyb, preferred_element_type=F32)
                dstate[h] = dsh
                lacc[h, 0:1, :] = lsum

        @pl.when(n == nb - 1)
        def _():
            for d in range(2):
                for h in range(RET_HEADS):
                    outs[4 * d + 3][h] = jnp.zeros((8, LANES), F32) + jnp.sum(scr[2 * d + 1][h])

    hmat = (RET_HEADS, RET_HEAD_DIM, RET_HEAD_DIM)
    in_specs, out_specs, args = [pl.BlockSpec(memory_space=pltpu.SMEM)], [], [logits]
    for d, states in enumerate((st_f, st_b)):
        rows = pl.BlockSpec((c * sub, RET_W), lambda n, d=d: (block[d](n), 0))
        in_specs += ([rows, rows, rows, pl.BlockSpec((sub,) + hmat, lambda n, d=d: (block[d](n), 0, 0, 0))]
                     + [pl.BlockSpec(sp.block_shape, lambda n, d=d, cb=sp.index_map(0)[1]: (block[d](n), cb)) for _, sp in vwin])
        args += [q, k, dy, states] + [a for a, _ in vwin]
        out_specs += [rows, rows, rows, pl.BlockSpec((RET_HEADS, 8, LANES), lambda n: (0, 0, 0))]
    return pl.pallas_call(
        body, name="ret_bwd", grid=(nb,), in_specs=in_specs, out_specs=out_specs,
        out_shape=([jax.ShapeDtypeStruct((s, RET_W), BF16)] * 3 + [jax.ShapeDtypeStruct((RET_HEADS, 8, LANES), F32)]) * 2,
        scratch_shapes=[pltpu.VMEM(hmat, F32), pltpu.VMEM((RET_HEADS, 8, LANES), F32)] * 2,
        compiler_params=_cparams(("arbitrary",)),
    )(*args)


def _local_step(x, p, target, w, small):
    s = x.shape[0]
    tabs = _rope_tables(s, ATTN_HEAD_DIM) + _rope_tables(s, RET_HEAD_DIM)
    g_mix, g_mlp, g_ple = small["mix_norm"][None, :], small["mlp_norm"][None, :], small["ple_norm"][None, :]
    g_final, g_ret = small["final_norm"][None, :], small["ret_norm_gain"][None, :]
    gq_w = jnp.tile(small["attn_q_norm"], ATTN_HEADS)[None, :]
    gk_w = jnp.tile(small["attn_k_norm"], ATTN_KV_HEADS)[None, :]
    logits = small["ret_decay_logit"]

    hb = _stage_norm_in(x, g_mix)
    proj = _mm("in_proj", hb, w["w_in"], tm=1024, tn=IN_W // 2, tk=1024, out_dtypes=(BF16,), j_outer=True)
    q_ct, k_rows, k_ct, v_rows, v_ct, rq, rk = _stage_qkv(proj, tabs, gq_w, gk_w)
    o_ct, lse = _attn_fwd(q_ct, k_rows, v_ct)
    ry_f, st_f, ry_b, st_b = _ret_fwd(logits, rq, rk, proj)
    rz, attn_rows = _stage_mix_post(ry_f, ry_b, proj, o_ct, g_ret)
    a_out = _mm("attn_o", attn_rows, w["w_attn_o"], tm=1024, tn=1024, tk=512, out_dtypes=(BF16,))
    n_gate = D_MODEL // EPI_PIECE

    def epi_merge(acc, e, c):
        ga, gr = _cat(e[1:1 + n_gate]), _cat(e[1 + n_gate:1 + 2 * n_gate])
        return acc, _sigmoid(ga) * e[0][...] + _sigmoid(gr) * acc

    r_out, merged = _mm("ret_o", rz, w["w_ret_o"], tm=1024, tn=1024, tk=512, out_dtypes=(BF16, BF16), epi=epi_merge,
                        epi_ins=(a_out, (proj, C_GA), (proj, C_GR)))

    def epi_res_norm(acc, e, c):
        xr = e[0][...] + acc
        return xr, _rms_fwd(xr, c[0][...])

    x1, hm = _mm("out_proj", merged, w["w_out"], tm=1024, tn=1024, tk=1024, out_dtypes=(F32, BF16),
                 epi=epi_res_norm, epi_ins=(x,), consts=(g_mlp,))

    def epi_relu2(acc, e, c):
        r = jnp.maximum(acc, 0.0)
        return (r * r,)

    act = _mm("mlp_up", hm, w["w_up"], tm=1024, tn=2048, tk=1024, out_dtypes=(BF16,), epi=epi_relu2, j_outer=True)
    x2, hp = _mm("mlp_down", act, w["w_down"], tm=512, tn=1024, tk=D_FF, out_dtypes=(F32, BF16),
                 epi=epi_res_norm, epi_ins=(x1,), consts=(g_ple,))
    pe = _mm("ple_emb", p, w["w_ple"], tm=1024, tn=1024, tk=256)

    def epi_head(acc, e, c):
        gt = _sigmoid(acc)
        pe_, gf = e[0][...], c[0][...]
        x3 = e[1][...] + gt * pe_
        r3 = lax.rsqrt(jnp.mean(x3 * x3, axis=-1, keepdims=True) + NORM_EPS)
        x3n = x3 * r3
        err = x3n * gf - e[2][...]
        dy = err * (1.0 / D_MODEL)
        dyg = dy * gf
        dx3 = r3 * (dyg - x3n * jnp.mean(dyg * x3n, axis=-1, keepdims=True))
        return dx3, dx3 * pe_ * gt * (1.0 - gt), dx3 * gt, err * err, dy * x3n

    dx3, dzg, dpe, loss_cols, g_final_p = _mm("ple_gate", hp, w["w_ple_gate"], tm=1024, tn=1024, tk=1024, out_dtypes=(F32, BF16, BF16),
                                              epi=epi_head, epi_ins=(pe, x2, target), consts=(g_final,), n_sums=2)
    loss_sum = 0.5 / D_MODEL * jnp.sum(loss_cols)

    gw = {}
    gw["w_ple"] = _mm("g_w_ple", p, dpe, ta=True, tm=256, tn=1024, tk=2048)
    gw["w_ple_gate"] = _mm("g_w_ple_gate", hp, dzg, ta=True, tm=1024, tn=1024, tk=2048)
    def epi_norm_bwd(acc, e, c):
        dx, dg = _rms_bwd(acc, e[0][...], c[0][...])
        return e[1][...] + dx, dg

    def epi_norm_bwd_b(acc, e, c):
        tot, dg = epi_norm_bwd(acc, e, c)
        return tot, tot, dg

    dx2, dx2_b, g_ple_p = _mm("d_hp", dzg, w["w_ple_gate"], tb=True, tm=1024, tn=1024, tk=1024, out_dtypes=(F32, BF16),
                              epi=epi_norm_bwd_b, epi_ins=(x2, dx3), consts=(g_ple,), n_sums=1)

    def epi_relu2_bwd(acc, e, c):
        return (acc * (2.0 * jnp.sqrt(e[0][...]).astype(F32)),)

    du = _mm("d_u", dx2_b, w["w_down"], tb=True, tm=1024, tn=2048, tk=1024, out_dtypes=(BF16,), epi=epi_relu2_bwd, epi_ins=(act,),
             j_outer=True)
    gw["w_down"] = _mm("g_w_down", act, dx2_b, ta=True, tm=1024, tn=1024, tk=4096)
    gw["w_up"] = _mm("g_w_up", hm, du, ta=True, tm=1024, tn=1024, tk=4096)
    dx1, dx1_b, g_mlp_p = _mm("d_hm", du, w["w_up"], tb=True, tm=512, tn=1024, tk=D_FF, out_dtypes=(F32, BF16),
                              epi=epi_norm_bwd_b, epi_ins=(x1, dx2), consts=(g_mlp,), n_sums=1)
    def epi_merge_bwd(acc, e, c):
        sa, sr = _sigmoid(_cat(e[2:2 + n_gate])), _sigmoid(_cat(e[2 + n_gate:2 + 2 * n_gate]))
        return acc * sa, acc * sr, acc * e[0][...] * sa * (1.0 - sa), acc * e[1][...] * sr * (1.0 - sr)

    dao, dro, dga, dgr = _mm("d_merged", dx1_b, w["w_out"], tb=True, tm=1024, tn=1024, tk=1024, out_dtypes=(BF16,) * 4,
                             epi=epi_merge_bwd, epi_ins=(a_out, r_out, (proj, C_GA), (proj, C_GR)))
    gw["w_out"] = _mm("g_w_out", merged, dx1_b, ta=True, tm=1024, tn=1024, tk=2048)
    gw["w_attn_o"] = _mm("g_w_attn_o", attn_rows, dao, ta=True, tm=512, tn=1024, tk=2048)
    gw["w_ret_o"] = _mm("g_w_ret_o", rz, dro, ta=True, tm=512, tn=1024, tk=2048)
    dattn = _mm("d_attn", dao, w["w_attn_o"], tb=True, tm=1024, tn=512, tk=1024)
    drz = _mm("d_rz", dro, w["w_ret_o"], tb=True, tm=1024, tn=512, tk=1024)
    do_ct, delta, dry, drg, g_ret_p = _stage_mix_post_bwd(dattn, attn_rows, drz, ry_f, ry_b, proj, g_ret)
    dq_f, dk_f, dv_f, dl_f, dq_b, dk_b, dv_b, dl_b = _ret_bwd(logits, rq, rk, proj, dry, st_f, st_b)
    dq_ct, dk8, dv8 = _attn_bwd(q_ct, do_ct, lse, delta, k_rows, v_rows, k_ct)
    dproj, gq_p, gk_p = _stage_dproj(proj, dq_ct, dk8, dv8, (dq_f, dk_f, dv_f, dq_b, dk_b, dv_b), drg, dga, dgr, tabs, gq_w, gk_w)
    gw["w_in"] = _mm("g_w_in", hb, dproj, ta=True, tm=512, tn=IN_W // 2, tk=2048)
    grad_x, g_mix_p = _mm("d_h", dproj, w["w_in"], tb=True, tm=512, tn=1024, tk=IN_W, epi=epi_norm_bwd, epi_ins=(x, dx1),
                          consts=(g_mix,), n_sums=1)

    gs = {
        "mix_norm": g_mix_p[0], "mlp_norm": g_mlp_p[0], "ple_norm": g_ple_p[0], "final_norm": g_final_p[0],
        "ret_norm_gain": g_ret_p[0],
        "attn_q_norm": jnp.sum(gq_p[0].reshape(ATTN_HEADS, ATTN_HEAD_DIM), axis=0),
        "attn_k_norm": jnp.sum(gk_p[0].reshape(ATTN_KV_HEADS, ATTN_HEAD_DIM), axis=0),
        "ret_decay_logit": jnp.stack([dl_f[:, 0, 0], dl_b[:, 0, 0]]),
    }
    return loss_sum, grad_x, gw, gs


PACK_COLS = 1024
N_CHIPS = 4
HALF_ROWS = 2048


def _pack_shard(parts):
    return jnp.concatenate([parts[n].reshape(-1, PACK_COLS) for n, _ in BIG], axis=0)


def _unpack_shard(slab, shapes):
    out, r = {}, 0
    for n, _ in BIG:
        rows = math.prod(shapes[n]) // PACK_COLS
        out[n] = slab[r:r + rows].reshape(shapes[n])
        r += rows
    return out


def _shard_of(full, axis, sidx):
    size = full.shape[axis] // N_CHIPS
    return lax.slice_in_dim(full, sidx * size, (sidx + 1) * size, axis=axis)


def _position():
    x, y, c = lax.axis_index("x"), lax.axis_index("y"), lax.axis_index("c")
    return x, y, c


def _other_chips(x, y):
    return [(1 - x, y), (x, 1 - y), (1 - x, 1 - y)]


ANY = pl.BlockSpec(memory_space=pl.ANY)


def _gather_weights(slab):
    rows = slab.shape[0]
    half = rows // 2

    def body(in_ref, out_ref, send_sems, recv_sems):
        x, y, c = _position()
        chips = _other_chips(x, y)

        def piece(chip, core):
            return out_ref.at[2 * chip[0] + chip[1], pl.ds(core * half, half), :]

        def copy(k, chip, core, to, src=None):
            return pltpu.make_async_remote_copy(
                src_ref=piece(chip, core) if src is None else src, dst_ref=piece(chip, core),
                send_sem=send_sems.at[k], recv_sem=recv_sems.at[k], device_id=to, device_id_type=MESH)

        first = [copy(j, (x, y), c, (*chip, c), src=in_ref.at[pl.ds(c * half, half), :]) for j, chip in enumerate(chips)]
        for cp in first:
            cp.start()
        passed = [copy(3 + j, chip, c, (x, y, 1 - c)) for j, chip in enumerate(chips)]
        for j, chip in enumerate(chips):
            copy(j, chip, c, (x, y, c)).wait_recv()
            passed[j].start()
        for j, chip in enumerate(chips):
            copy(3 + j, chip, 1 - c, (x, y, c)).wait_recv()
        for cp in first + passed:
            cp.wait_send()

    return pl.pallas_call(
        body, name="gather_weights", in_specs=[ANY], out_specs=ANY,
        out_shape=jax.ShapeDtypeStruct((N_CHIPS,) + slab.shape, slab.dtype),
        scratch_shapes=[pltpu.SemaphoreType.DMA((6,)), pltpu.SemaphoreType.DMA((6,))],
    )(slab)


def _exchange_halves(g):
    def body(g_ref, out_ref, send_sem, recv_sem):
        x, y, c = _position()
        cp = pltpu.make_async_remote_copy(src_ref=g_ref.at[pl.ds(0, N_CHIPS), 1 - c], dst_ref=out_ref, send_sem=send_sem,
                                          recv_sem=recv_sem, device_id=(x, y, 1 - c), device_id_type=MESH)
        cp.start()
        cp.wait()

    return pl.pallas_call(
        body, name="exchange_halves", in_specs=[ANY], out_specs=ANY,
        out_shape=jax.ShapeDtypeStruct((N_CHIPS,) + g.shape[2:], g.dtype),
        scratch_shapes=[pltpu.SemaphoreType.DMA, pltpu.SemaphoreType.DMA],
    )(g)


def _add_my_half(g, r1, c_idx):
    tr = 256
    nt = g.shape[2] // tr
    out = (N_CHIPS,) + g.shape[2:]

    def body(c_ref, g_ref, r_ref, o_ref, ob_ref):
        tot = g_ref[...] + r_ref[...]
        o_ref[...] = tot
        ob_ref[...] = tot.astype(BF16)

    blk = (None, tr, PACK_COLS)
    spec = pl.BlockSpec(blk, lambda s, i, c_ref: (s, i, 0))
    return pl.pallas_call(
        body, name="add_my_half",
        grid_spec=pltpu.PrefetchScalarGridSpec(
            num_scalar_prefetch=1, grid=(N_CHIPS, nt),
            in_specs=[pl.BlockSpec((None,) + blk, lambda s, i, c_ref: (s, c_ref[0], i, 0)), spec],
            out_specs=[spec, spec]),
        out_shape=[jax.ShapeDtypeStruct(out, F32), jax.ShapeDtypeStruct(out, BF16)],
        compiler_params=_cparams(("parallel", "parallel")),
    )(c_idx, g, r1)


def _scatter_to_chips(part):
    def body(p_ref, out_ref, send_sems, recv_sems):
        x, y, c = _position()
        chips = _other_chips(x, y)
        sends = [pltpu.make_async_remote_copy(
            src_ref=p_ref.at[2 * chip[0] + chip[1]], dst_ref=out_ref.at[j], send_sem=send_sems.at[j], recv_sem=recv_sems.at[j],
            device_id=(*chip, c), device_id_type=MESH) for j, chip in enumerate(chips)]
        for cp in sends:
            cp.start()
        for cp in sends:
            cp.wait()

    return pl.pallas_call(
        body, name="scatter_to_chips", in_specs=[ANY], out_specs=ANY,
        out_shape=jax.ShapeDtypeStruct((N_CHIPS - 1,) + part.shape[1:], part.dtype),
        scratch_shapes=[pltpu.SemaphoreType.DMA((3,)), pltpu.SemaphoreType.DMA((3,))],
    )(part)


def _sum_chips(part, r2, chip_idx):
    tr = 256

    def body(c_ref, p_ref, r_ref, o_ref):
        o_ref[...] = ((p_ref[...] + r_ref[0]) + r_ref[1]) + r_ref[2]

    return pl.pallas_call(
        body, name="sum_chips",
        grid_spec=pltpu.PrefetchScalarGridSpec(
            num_scalar_prefetch=1, grid=(r2.shape[1] // tr,),
            in_specs=[pl.BlockSpec((None, tr, PACK_COLS), lambda i, c_ref: (c_ref[0], i, 0)),
                      pl.BlockSpec((N_CHIPS - 1, tr, PACK_COLS), lambda i, c_ref: (0, i, 0))],
            out_specs=pl.BlockSpec((tr, PACK_COLS), lambda i, c_ref: (i, 0))),
        out_shape=jax.ShapeDtypeStruct(r2.shape[1:], F32),
        compiler_params=_cparams(("parallel",)),
    )(chip_idx, part, r2)


def _join_halves(red):
    def body(r_ref, out_ref, send_sem, recv_sem):
        x, y, c = _position()
        cp = pltpu.make_async_remote_copy(src_ref=r_ref, dst_ref=out_ref, send_sem=send_sem, recv_sem=recv_sem,
                                          device_id=(x, y, 1 - c), device_id_type=MESH)
        cp.start()
        cp.wait()

    return pl.pallas_call(
        body, name="join_halves", in_specs=[ANY], out_specs=ANY,
        out_shape=jax.ShapeDtypeStruct(red.shape, red.dtype),
        scratch_shapes=[pltpu.SemaphoreType.DMA, pltpu.SemaphoreType.DMA],
    )(red)


def _adamw_math(w, g, m, v):
    m = ADAM_B1 * m + (1.0 - ADAM_B1) * g
    v = ADAM_B2 * v + (1.0 - ADAM_B2) * (g * g)
    m_hat = m / (1.0 - ADAM_B1 ** ADAM_STEP)
    v_hat = v / (1.0 - ADAM_B2 ** ADAM_STEP)
    delta = -ADAM_LR * (m_hat / (jnp.sqrt(v_hat) + ADAM_EPS) + ADAM_WD * w)
    return delta, m, v


def _adamw(name, w, g, m, v):
    tr = min(256, w.shape[0])

    def body(w_ref, g_ref, m_ref, v_ref, d_ref, nm_ref, nv_ref):
        d_ref[...], nm_ref[...], nv_ref[...] = _adamw_math(w_ref[...], g_ref[...], m_ref[...], v_ref[...])

    blk = pl.BlockSpec((tr, w.shape[1]), lambda i: (i, 0))
    return pl.pallas_call(
        body, name="adamw_" + name, grid=(w.shape[0] // tr,), in_specs=[blk] * 4, out_specs=[blk] * 3,
        out_shape=[jax.ShapeDtypeStruct(w.shape, F32)] * 3, compiler_params=_cparams(("parallel",)),
    )(w, g, m, v)


def _small_step(gpk, wpk, mpk, vpk):
    row, col, width = SMALL["ret_decay_logit"]

    def body(g_ref, w_ref, m_ref, v_ref, og_ref, od_ref, om_ref, ov_ref, gbuf, send_sems, recv_sems):
        x, y, c = _position()
        me = 4 * x + 2 * y + c
        gbuf[me] = g_ref[...]
        sends = []
        for k in range(1, 8):
            to = (x ^ (k >> 2), y ^ ((k >> 1) & 1), c ^ (k & 1))
            cp = pltpu.make_async_remote_copy(src_ref=g_ref, dst_ref=gbuf.at[me], send_sem=send_sems.at[k - 1],
                                              recv_sem=recv_sems.at[k - 1], device_id=to, device_id_type=MESH)
            cp.start()
            sends.append(cp)
        for k in range(1, 8):
            frm = me ^ k
            pltpu.make_async_remote_copy(src_ref=g_ref, dst_ref=gbuf.at[frm], send_sem=send_sems.at[k - 1],
                                         recv_sem=recv_sems.at[k - 1], device_id=(x, y, c), device_id_type=MESH).wait_recv()
        for cp in sends:
            cp.wait_send()
        tot = gbuf[0]
        for d in range(1, 8):
            tot = tot + gbuf[d]
        w = w_ref[...]
        r_i = lax.broadcasted_iota(jnp.int32, w.shape, 0)
        c_i = lax.broadcasted_iota(jnp.int32, w.shape, 1)
        is_logit = (r_i == row) & (c_i >= col) & (c_i < col + width)
        g = jnp.where(is_logit, tot * _sigmoid(-w), tot)
        og_ref[...] = g
        od_ref[...], om_ref[...], ov_ref[...] = _adamw_math(w, g, m_ref[...], v_ref[...])

    vm = pl.BlockSpec(memory_space=pltpu.VMEM)
    shp = jax.ShapeDtypeStruct(gpk.shape, F32)
    return pl.pallas_call(
        body, name="small_step", in_specs=[vm] * 4, out_specs=[vm] * 4, out_shape=[shp] * 4,
        scratch_shapes=[pltpu.VMEM((8,) + gpk.shape, F32), pltpu.SemaphoreType.DMA((7,)), pltpu.SemaphoreType.DMA((7,))],
    )(gpk, wpk, mpk, vpk)


def _pack_small(parts):
    rows = [[] for _ in range(SMALL_ROWS)]
    for n, (r, col, width) in sorted(SMALL.items(), key=lambda kv: (kv[1][0], kv[1][1])):
        rows[r].append((col, parts[n].reshape(-1).astype(F32)))
    out = []
    for r in range(SMALL_ROWS):
        segs, pos = [], 0
        for col, vec in rows[r]:
            assert col == pos
            segs.append(vec)
            pos += vec.shape[0]
        if pos < PACK_COLS:
            segs.append(jnp.zeros((PACK_COLS - pos,), F32))
        out.append(jnp.concatenate(segs))
    return jnp.stack(out)


def _unpack_small(pk, shapes):
    return {n: pk[r, col:col + width].reshape(shapes[n]) for n, (r, col, width) in SMALL.items()}


WEIGHTS = ("mix_norm", "w_in", "attn_q_norm", "attn_k_norm", "ret_decay_logit", "ret_norm_gain", "w_attn_o", "w_ret_o", "w_out",
           "mlp_norm", "w_up", "w_down", "ple_norm", "w_ple_gate", "w_ple", "final_norm")


def kernel(x, p, mix_norm, w_in, attn_q_norm, attn_k_norm, ret_decay_logit, ret_norm_gain, w_attn_o, w_ret_o, w_out, mlp_norm, w_up, w_down, ple_norm, w_ple_gate, w_ple, final_norm, loss_target, m_mix_norm, m_w_in, m_attn_q_norm, m_attn_k_norm, m_ret_decay_logit, m_ret_norm_gain, m_w_attn_o, m_w_ret_o, m_w_out, m_mlp_norm, m_w_up, m_w_down, m_ple_norm, m_w_ple_gate, m_w_ple, m_final_norm, v_mix_norm, v_w_in, v_attn_q_norm, v_attn_k_norm, v_ret_decay_logit, v_ret_norm_gain, v_w_attn_o, v_w_ret_o, v_w_out, v_mlp_norm, v_w_up, v_w_down, v_ple_norm, v_w_ple_gate, v_w_ple, v_final_norm):
    args = dict(locals())
    wts = {n: args[n] for n in WEIGHTS}
    ms = {n: args["m_" + n] for n in WEIGHTS}
    vs = {n: args["v_" + n] for n in WEIGHTS}
    shapes = {n: wts[n].shape for n in WEIGHTS}
    big_names = [n for n, _ in BIG]
    xi, yi, ci = _position()
    c_idx = ci.astype(jnp.int32).reshape(1)

    chip_idx = (2 * xi + yi).astype(jnp.int32)
    slab_b = _pack_shard({n: wts[n][0].astype(BF16) for n in big_names})
    gathered = lax.dynamic_update_slice(_gather_weights(slab_b), slab_b[None], (chip_idx, 0, 0))
    full, r0 = {}, 0
    for n, axis in BIG:
        shard = shapes[n][1:]
        rows = math.prod(shard) // PACK_COLS
        if axis == 0 and shard[1] == PACK_COLS:
            full[n] = gathered[:, r0:r0 + rows].reshape(N_CHIPS * shard[0], shard[1])
        else:
            full[n] = jnp.concatenate([gathered[k, r0:r0 + rows].reshape(shard) for k in range(N_CHIPS)], axis=axis)
        r0 += rows
    small = {n: wts[n].reshape(wts[n].shape[1:] if wts[n].ndim > 1 else wts[n].shape) for n in SMALL}

    loss_part, grad_x, gw, gs = _local_step(x[0], p[0, 0], loss_target[0], full, small)
    loss = lax.psum(loss_part, ("x", "y", "c"))

    slabs = jnp.stack([_pack_shard({n: _shard_of(gw[n], axis, k) for n, axis in BIG}) for k in range(N_CHIPS)])
    halves = slabs.reshape(N_CHIPS, 2, HALF_ROWS, PACK_COLS)
    chip_part, chip_part_b = _add_my_half(halves, _exchange_halves(halves), c_idx)
    mine = _sum_chips(chip_part, _scatter_to_chips(chip_part_b), chip_idx.reshape(1))
    both = jnp.stack([mine, _join_halves(mine)])
    reduced = jnp.where(ci == 0, both, both[::-1]).reshape(2 * HALF_ROWS, PACK_COLS)
    g_big = _unpack_shard(reduced, {n: shapes[n][1:] for n in big_names})
    big_out = [{}, {}, {}, {}]
    for n in big_names:
        big_out[0][n] = g_big[n][None]
        for kind, a in enumerate(_adamw(n, wts[n][0], g_big[n], ms[n][0], vs[n][0])):
            big_out[kind + 1][n] = a[None]

    sm_out = _small_step(_pack_small(gs), _pack_small({n: wts[n] for n in SMALL}), _pack_small({n: ms[n] for n in SMALL}),
                         _pack_small({n: vs[n] for n in SMALL}))
    small_out = [_unpack_small(a, {n: shapes[n] for n in SMALL}) for a in sm_out]

    outs = [loss, grad_x[None]]
    for kind in range(4):
        for n in WEIGHTS:
            outs.append(small_out[kind][n] if n in SMALL else big_out[kind][n])
    return tuple(outs)
```

```python
import math

import jax
import jax.numpy as jnp
from jax import lax
from jax.experimental import pallas as pl
from jax.experimental.pallas import tpu as pltpu

F32 = jnp.float32
BF16 = jnp.bfloat16
MESH = pl.DeviceIdType.MESH

D_MODEL = 1024
GRID_W = 64
ATTN_HEAD_DIM = 64
ATTN_HEADS = 8
ATTN_KV_HEADS = 2
ATTN_GROUP = ATTN_HEADS // ATTN_KV_HEADS
RET_HEAD_DIM = 128
RET_HEADS = 4
ATTN_Q_W = 512
ATTN_KV_W = 128
RET_W = 512
IN_W = 4864
D_FF = 4096
RET_CHUNK = 256
ROPE_THETA = 10000.0
NORM_EPS = 1e-6
GN_EPS = 1e-5
ATTN_SCALE = ATTN_HEAD_DIM ** -0.5
LOG2E = math.log2(math.e)
Q_FOLD = ATTN_SCALE * LOG2E
RET_SCALE = RET_HEAD_DIM ** -0.5

C_AQ, C_AK, C_AV, C_RQ, C_RK, C_RV, C_RG, C_GA, C_GR = 0, 512, 640, 768, 1280, 1792, 2304, 2816, 3840

ADAM_LR = 0.001
ADAM_B1 = 0.9
ADAM_B2 = 0.999
ADAM_EPS = 1e-08
ADAM_WD = 0.01
ADAM_STEP = 10

LANES = 128
VMEM_LIMIT = 56 << 20
SEQ_TILE = 512
EPI_PIECE = 256

BIG = (("w_in", 1), ("w_attn_o", 1), ("w_ret_o", 1), ("w_out", 0), ("w_up", 1), ("w_down", 0), ("w_ple_gate", 0), ("w_ple", 1))
SMALL_ROWS = 8
SMALL = {"mix_norm": (0, 0, 1024), "mlp_norm": (1, 0, 1024), "ple_norm": (2, 0, 1024), "final_norm": (3, 0, 1024),
         "ret_norm_gain": (4, 0, 512), "attn_q_norm": (4, 512, 64), "attn_k_norm": (4, 576, 64), "ret_decay_logit": (4, 640, 8)}


def _seq_tile(s):
    return min(SEQ_TILE, s // 2)


def _cparams(sem=None, vmem=VMEM_LIMIT):
    return pltpu.CompilerParams(dimension_semantics=sem, vmem_limit_bytes=vmem)


def _mm(name, a, b, *, ta=False, tb=False, tm, tn, tk, out_dtypes=(F32,), epi=None, epi_ins=(), consts=(), n_sums=0, j_outer=False):
    if ta:
        kdim, m = a.shape
    else:
        m, kdim = a.shape
    n = b.shape[0] if tb else b.shape[1]
    tm, tn, tk = min(tm, m), min(tn, n), min(tk, kdim)
    assert m % tm == 0 and n % tn == 0 and kdim % tk == 0, (name, m, n, kdim, tm, tn, tk)
    nk = kdim // tk
    e_arrs, e_cols = [], []
    for item in epi_ins:
        if isinstance(item, tuple):
            arr, start = item
            assert tn == n and start % EPI_PIECE == 0 and n % EPI_PIECE == 0
            for piece in range(n // EPI_PIECE):
                e_arrs.append(arr)
                e_cols.append(start // EPI_PIECE + piece)
        else:
            e_arrs.append(item)
            e_cols.append(None)
    n_e, n_c, n_o = len(e_arrs), len(consts), len(out_dtypes)
    assert n_sums == 0 or tn == n

    def body(*refs):
        a_ref, b_ref = refs[0], refs[1]
        e_refs = refs[2:2 + n_e]
        c_refs = refs[2 + n_e:2 + n_e + n_c]
        o_refs = refs[2 + n_e + n_c:2 + n_e + n_c + n_o]
        s_refs = refs[2 + n_e + n_c + n_o:2 + n_e + n_c + n_o + n_sums]
        acc_ref = refs[2 + n_e + n_c + n_o + n_sums] if nk > 1 else None
        k = pl.program_id(2)
        if n_sums:
            @pl.when((pl.program_id(1 if j_outer else 0) == 0) & (k == 0))
            def _():
                for r in s_refs:
                    r[...] = jnp.zeros(r.shape, F32)
        av = a_ref[...].astype(BF16)
        bv = b_ref[...].astype(BF16)
        dims = (((0,) if ta else (1,), (1,) if tb else (0,)), ((), ()))
        part = lax.dot_general(av, bv, dims, preferred_element_type=F32)

        def finish(acc):
            vals = epi(acc, e_refs, c_refs) if epi is not None else (acc,)
            for o_ref, v in zip(o_refs, vals[:n_o]):
                o_ref[...] = v.astype(o_ref.dtype)
            for s_ref, v in zip(s_refs, vals[n_o:]):
                _acc_add(s_ref, v)

        if nk == 1:
            finish(part)
        else:
            @pl.when(k == 0)
            def _():
                acc_ref[...] = part

            @pl.when(k > 0)
            def _():
                acc_ref[...] += part

            @pl.when(k == nk - 1)
            def _():
                finish(acc_ref[...])

    def spec(shape, index):
        return pl.BlockSpec(shape, (lambda j, i, k: index(i, j, k)) if j_outer else index)

    a_spec = spec((tk, tm), lambda i, j, k: (k, i)) if ta else spec((tm, tk), lambda i, j, k: (i, k))
    b_spec = spec((tn, tk), lambda i, j, k: (j, k)) if tb else spec((tk, tn), lambda i, j, k: (k, j))
    o_spec = spec((tm, tn), lambda i, j, k: (i, j))
    c_specs = [spec(c.shape, lambda i, j, k, nd=c.ndim: (0,) * nd) for c in consts]
    outs = pl.pallas_call(
        body, name=name,
        grid=(n // tn, m // tm, nk) if j_outer else (m // tm, n // tn, nk),
        in_specs=([a_spec, b_spec]
                  + [o_spec if cb is None else spec((tm, EPI_PIECE), lambda i, j, k, cb=cb: (i, cb)) for cb in e_cols] + c_specs),
        out_specs=[o_spec] * n_o + [spec((8, n), lambda i, j, k: (0, 0))] * n_sums,
        out_shape=[jax.ShapeDtypeStruct((m, n), dt) for dt in out_dtypes] + [jax.ShapeDtypeStruct((8, n), F32)] * n_sums,
        scratch_shapes=[pltpu.VMEM((tm, tn), F32)] if nk > 1 else [],
        compiler_params=_cparams(("arbitrary",) * 3 if n_sums else ("parallel", "parallel", "arbitrary")),
    )(a, b, *e_arrs, *consts)
    return outs[0] if n_o + n_sums == 1 else outs


def _rows(arr, tr):
    return (arr, pl.BlockSpec((tr, arr.shape[1]), lambda i: (i, 0)))


def _win(arr, tr, start, width):
    bw = math.gcd(start, width) if start else width
    assert bw % LANES == 0
    return [(arr, pl.BlockSpec((tr, bw), lambda i, cb=start // bw + p: (i, cb))) for p in range(width // bw)]


def _ct(arr):
    return (arr, pl.BlockSpec((None,) + arr.shape[1:], lambda i: (i, 0, 0)))


def _whole(arr):
    return (arr, pl.BlockSpec(arr.shape, lambda i, nd=arr.ndim: (0,) * nd))


def _cat(refs):
    vals = [r[...].astype(F32) for r in refs]
    return vals[0] if len(vals) == 1 else jnp.concatenate(vals, axis=1)


def _seqtiled(name, fn, n_tiles, ins, outs, acc_widths=()):
    n_i, n_o, n_a = len(ins), len(outs), len(acc_widths)

    def body(*refs):
        i_refs, o_refs, a_refs = refs[:n_i], refs[n_i:n_i + n_o], refs[n_i + n_o:]
        if n_a:
            @pl.when(pl.program_id(0) == 0)
            def _():
                for r in a_refs:
                    r[...] = jnp.zeros(r.shape, F32)
        fn(list(i_refs), list(o_refs), list(a_refs))

    res = pl.pallas_call(
        body, name=name, grid=(n_tiles,),
        in_specs=[s for _, s in ins],
        out_specs=[s for _, _, s in outs] + [pl.BlockSpec((8, w), lambda i: (0, 0)) for w in acc_widths],
        out_shape=[jax.ShapeDtypeStruct(sh, dt) for sh, dt, _ in outs] + [jax.ShapeDtypeStruct((8, w), F32) for w in acc_widths],
        compiler_params=_cparams(("arbitrary",)),
    )(*[a for a, _ in ins])
    return res


def _acc_add(acc_ref, val):
    acc_ref[0:1, :] += jnp.sum(val, axis=0, keepdims=True)


def _out_rows(s, w, dt, tr):
    return ((s, w), dt, pl.BlockSpec((tr, w), lambda i: (i, 0)))


def _out_ct(s, w, dt, t):
    return ((s // t, w, t), dt, pl.BlockSpec((None, w, t), lambda i: (i, 0, 0)))


def _rms_fwd(x, gain):
    r = lax.rsqrt(jnp.mean(x * x, axis=-1, keepdims=True) + NORM_EPS)
    return x * r * gain


def _rms_bwd(dy, x, gain):
    r = lax.rsqrt(jnp.mean(x * x, axis=-1, keepdims=True) + NORM_EPS)
    xn = x * r
    dyg = dy * gain
    dx = r * (dyg - xn * jnp.mean(dyg * xn, axis=-1, keepdims=True))
    return dx, dy * xn


def _seg_mean(y, hd):
    w = y.shape[1]
    pieces = []
    for s in range(0, w, LANES):
        v = y[:, s:s + LANES]
        tot = jnp.sum(v, axis=1, keepdims=True)
        if hd == LANES:
            pieces.append(jnp.broadcast_to(tot, v.shape))
        else:
            low = lax.broadcasted_iota(jnp.int32, v.shape, 1) < hd
            lo = jnp.sum(jnp.where(low, v, 0.0), axis=1, keepdims=True)
            pieces.append(jnp.where(low, lo, tot - lo))
    out = pieces[0] if len(pieces) == 1 else jnp.concatenate(pieces, axis=1)
    return out * (1.0 / hd)


def _tile_lanes(t, w):
    return t if w == t.shape[1] else jnp.concatenate([t] * (w // t.shape[1]), axis=1)


def _swap_halves(x, hd):
    w = x.shape[1]
    half = hd // 2
    lane = lax.broadcasted_iota(jnp.int32, x.shape, 1)
    return jnp.where((lane % hd) < half, pltpu.roll(x, w - half, 1), pltpu.roll(x, half, 1))


def _rope(x, cos, sin_signed, hd):
    w = x.shape[1]
    return x * _tile_lanes(cos, w) + _swap_halves(x, hd) * _tile_lanes(sin_signed, w)


def _rope_t(dy, cos, sin_signed, hd):
    w = dy.shape[1]
    return dy * _tile_lanes(cos, w) + _swap_halves(dy * _tile_lanes(sin_signed, w), hd)


def _headnorm_fwd(x, gain_w, hd):
    r = lax.rsqrt(_seg_mean(x * x, hd) + NORM_EPS)
    return x * r * gain_w


def _headnorm_bwd(dy, x, gain_w, hd):
    r = lax.rsqrt(_seg_mean(x * x, hd) + NORM_EPS)
    xn = x * r
    dyg = dy * gain_w
    return r * (dyg - xn * _seg_mean(dyg * xn, hd)), dy * xn


def _sigmoid(x):
    return 0.5 * jnp.tanh(0.5 * x) + 0.5


def _rope_tables(seq_len, head_dim):
    rows = seq_len // GRID_W
    n_axis = head_dim // 4
    freqs = ROPE_THETA ** (-jnp.arange(n_axis, dtype=F32) / n_axis)
    ang_r = jnp.arange(rows, dtype=F32)[:, None] * freqs
    ang_c = jnp.arange(GRID_W, dtype=F32)[:, None] * freqs

    def expand(by_row, by_col):
        r = jnp.broadcast_to(by_row[:, None, :], (rows, GRID_W, n_axis))
        c = jnp.broadcast_to(by_col[None, :, :], (rows, GRID_W, n_axis))
        return jnp.concatenate([r, c], axis=-1).reshape(seq_len, 2 * n_axis)

    cos, sin = expand(jnp.cos(ang_r), jnp.cos(ang_c)), expand(jnp.sin(ang_r), jnp.sin(ang_c))
    reps = LANES // head_dim
    return jnp.tile(jnp.concatenate([cos, cos], axis=-1), (1, reps)), jnp.tile(jnp.concatenate([-sin, sin], axis=-1), (1, reps))


def _stage_norm_in(x, gain):
    s = x.shape[0]
    tr = min(SEQ_TILE, s)

    def fn(i, o, a):
        o[0][...] = _rms_fwd(i[0][...], i[1][...]).astype(BF16)

    return _seqtiled("norm_in", fn, s // tr, [_rows(x, tr), _whole(gain)], [_out_rows(s, D_MODEL, BF16, tr)])[0]


def _stage_qkv(proj, tabs, gq_w, gk_w):
    s = proj.shape[0]
    t = _seq_tile(s)
    ca, sa, cr, sr = tabs
    ins = (_win(proj, t, C_AQ, ATTN_Q_W) + _win(proj, t, C_AK, ATTN_KV_W) + _win(proj, t, C_AV, ATTN_KV_W)
           + _win(proj, t, C_RQ, RET_W) + _win(proj, t, C_RK, RET_W)
           + [_rows(ca, t), _rows(sa, t), _rows(cr, t), _rows(sr, t), _whole(gq_w), _whole(gk_w)])

    def fn(i, o, a):
        aq, ak, av = (i[n][...].astype(F32) for n in range(3))
        rq, rk = _cat(i[3:5]), _cat(i[5:7])
        ca_, sa_, cr_, sr_ = i[7][...], i[8][...], i[9][...], i[10][...]
        qr = _rope(_headnorm_fwd(aq, i[11][...], ATTN_HEAD_DIM), ca_, sa_, ATTN_HEAD_DIM) * Q_FOLD
        kr = _rope(_headnorm_fwd(ak, i[12][...], ATTN_HEAD_DIM), ca_, sa_, ATTN_HEAD_DIM)
        qt = qr.T.astype(BF16)
        zeros = jnp.zeros((ATTN_HEAD_DIM, t), BF16)
        for h in range(ATTN_HEADS):
            g = h // ATTN_GROUP
            blk = qt[h * ATTN_HEAD_DIM:(h + 1) * ATTN_HEAD_DIM, :]
            o[0][h * LANES + g * ATTN_HEAD_DIM:h * LANES + (g + 1) * ATTN_HEAD_DIM, :] = blk
            o[0][h * LANES + (1 - g) * ATTN_HEAD_DIM:h * LANES + (2 - g) * ATTN_HEAD_DIM, :] = zeros
        o[1][...] = kr.astype(BF16)
        o[2][...] = kr.T.astype(BF16)
        o[3][...] = av.astype(BF16)
        o[4][...] = av.T.astype(BF16)
        o[5][...] = _rope(rq, cr_, sr_, RET_HEAD_DIM) * RET_SCALE
        o[6][...] = _rope(rk, cr_, sr_, RET_HEAD_DIM)

    outs = [_out_ct(s, ATTN_HEADS * LANES, BF16, t), _out_rows(s, ATTN_KV_W, BF16, t), _out_ct(s, ATTN_KV_W, BF16, t),
            _out_rows(s, ATTN_KV_W, BF16, t), _out_ct(s, ATTN_KV_W, BF16, t), _out_rows(s, RET_W, F32, t), _out_rows(s, RET_W, F32, t)]
    return _seqtiled("qkv_prep", fn, s // t, ins, outs)


def _groupnorm_gate(ry, rg, gain):
    mu = _seg_mean(ry, RET_HEAD_DIM)
    d = ry - mu
    rs = lax.rsqrt(_seg_mean(d * d, RET_HEAD_DIM) + GN_EPS)
    return d * rs, rs, _sigmoid(rg)


def _stage_mix_post(ry_f, ry_b, proj, o_ct, gain):
    s = proj.shape[0]
    t = _seq_tile(s)
    ins = [_rows(ry_f, t), _rows(ry_b, t)] + _win(proj, t, C_RG, RET_W) + [_ct(o_ct), _whole(gain)]

    def fn(i, o, a):
        ry = i[0][...] + i[1][...]
        rg = _cat(i[2:4])
        gn, _, sg = _groupnorm_gate(ry, rg, None)
        o[0][...] = (gn * i[5][...] * (rg * sg)).astype(BF16)
        o[1][...] = i[4][...].astype(F32).T.astype(BF16)

    return _seqtiled("mix_post", fn, s // t, ins, [_out_rows(s, RET_W, BF16, t), _out_rows(s, ATTN_Q_W, BF16, t)])


def _stage_mix_post_bwd(dattn, attn_rows, drz, ry_f, ry_b, proj, gain):
    s = proj.shape[0]
    t = _seq_tile(s)
    ins = ([_rows(dattn, t), _rows(attn_rows, t), _rows(drz, t), _rows(ry_f, t), _rows(ry_b, t)]
           + _win(proj, t, C_RG, RET_W) + [_whole(gain)])

    def fn(i, o, a):
        da = i[0][...]
        dat = da.T
        prod_t = (da * i[1][...].astype(F32)).T
        dat_b = dat.astype(BF16)
        zeros = jnp.zeros((ATTN_HEAD_DIM, t), BF16)
        for h in range(ATTN_HEADS):
            g = h // ATTN_GROUP
            o[0][h * LANES + g * ATTN_HEAD_DIM:h * LANES + (g + 1) * ATTN_HEAD_DIM, :] = dat_b[h * ATTN_HEAD_DIM:(h + 1) * ATTN_HEAD_DIM, :]
            o[0][h * LANES + (1 - g) * ATTN_HEAD_DIM:h * LANES + (2 - g) * ATTN_HEAD_DIM, :] = zeros
            o[1][h] = jnp.sum(prod_t[h * ATTN_HEAD_DIM:(h + 1) * ATTN_HEAD_DIM, :], axis=0, keepdims=True)
        ry = i[3][...] + i[4][...]
        rg = _cat(i[5:7])
        gain_ = i[7][...]
        gn, rs, sg = _groupnorm_gate(ry, rg, None)
        dz = i[2][...]
        silu = rg * sg
        _acc_add(a[0], dz * gn * silu)
        dgn = dz * gain_ * silu
        o[2][...] = rs * (dgn - _seg_mean(dgn, RET_HEAD_DIM) - gn * _seg_mean(dgn * gn, RET_HEAD_DIM))
        o[3][...] = (dz * gn * gain_ * (sg * (1.0 + rg * (1.0 - sg)))).astype(BF16)

    outs = [_out_ct(s, ATTN_HEADS * LANES, BF16, t),
            ((ATTN_HEADS, s // t, 1, t), F32, pl.BlockSpec((ATTN_HEADS, None, 1, t), lambda i: (0, i, 0, 0))),
            _out_rows(s, RET_W, F32, t), _out_rows(s, RET_W, BF16, t)]
    return _seqtiled("mix_post_bwd", fn, s // t, ins, outs, acc_widths=(RET_W,))


def _stage_dproj(proj, dq_ct, dk8, dv8, rgrads, drg, dga, dgr, tabs, gq_w, gk_w):
    s = proj.shape[0]
    t = _seq_tile(s)
    ca, sa, cr, sr = tabs
    kv8 = pl.BlockSpec((ATTN_HEADS, t, ATTN_KV_W), lambda i: (0, i, 0))
    ins = (_win(proj, t, C_AQ, ATTN_Q_W) + _win(proj, t, C_AK, ATTN_KV_W) + [_ct(dq_ct), (dk8, kv8), (dv8, kv8)]
           + [_rows(g, t) for g in rgrads] + [_rows(drg, t), _rows(dga, t), _rows(dgr, t)]
           + [_rows(ca, t), _rows(sa, t), _rows(cr, t), _rows(sr, t), _whole(gq_w), _whole(gk_w)])

    def fn(i, o, a):
        aq, ak = i[0][...].astype(F32), i[1][...].astype(F32)
        dq_f, dk_f, dv_f, dq_b, dk_b, dv_b = (r[...].astype(F32) for r in i[5:11])
        ca_, sa_, cr_, sr_ = i[14][...], i[15][...], i[16][...], i[17][...]
        dqn = _rope_t(i[2][...].T * ATTN_SCALE, ca_, sa_, ATTN_HEAD_DIM)
        daq, gq_rows = _headnorm_bwd(dqn, aq, i[18][...], ATTN_HEAD_DIM)
        dkn = _rope_t(jnp.sum(i[3][...].astype(F32), axis=0) * (1.0 / LOG2E), ca_, sa_, ATTN_HEAD_DIM)
        dak, gk_rows = _headnorm_bwd(dkn, ak, i[19][...], ATTN_HEAD_DIM)
        _acc_add(a[0], gq_rows)
        _acc_add(a[1], gk_rows)
        out = o[0]
        out[:, C_AQ:C_AQ + ATTN_Q_W] = daq.astype(BF16)
        out[:, C_AK:C_AK + ATTN_KV_W] = dak.astype(BF16)
        out[:, C_AV:C_AV + ATTN_KV_W] = jnp.sum(i[4][...].astype(F32), axis=0).astype(BF16)
        out[:, C_RQ:C_RQ + RET_W] = _rope_t((dq_f + dq_b) * RET_SCALE, cr_, sr_, RET_HEAD_DIM).astype(BF16)
        out[:, C_RK:C_RK + RET_W] = _rope_t(dk_f + dk_b, cr_, sr_, RET_HEAD_DIM).astype(BF16)
        out[:, C_RV:C_RV + RET_W] = (dv_f + dv_b).astype(BF16)
        out[:, C_RG:C_RG + RET_W] = i[11][...]
        out[:, C_GA:C_GA + D_MODEL] = i[12][...]
        out[:, C_GR:C_GR + D_MODEL] = i[13][...]

    return _seqtiled("dproj", fn, s // t, ins, [_out_rows(s, IN_W, BF16, t)], acc_widths=(ATTN_Q_W, ATTN_KV_W))


def _attn_fwd(q_ct, k_rows, v_ct):
    nq, _, t = q_ct.shape
    s = nq * t
    nk = nq
    assert nk % 2 == 0
    n_ch = next(n for n in (8, 4, 2) if nq % n == 0)
    halves = 2 if t % (2 * LANES) == 0 else 1
    tq = t // halves
    n_par = n_ch * halves

    def body(q_ref, k_ref, v_ref, o_ref, lse_ref, *bufs):
        sbuf = tuple(bufs[2 * w:2 * w + 2] for w in range(n_par))
        pbuf = tuple(bufs[2 * n_par + 2 * w:2 * n_par + 2 * w + 2] for w in range(n_par))

        def where(w):
            return w // halves, slice((w % halves) * tq, (w % halves + 1) * tq)

        def scores(w, j, slot):
            kj = k_ref[pl.ds(pl.multiple_of(j * t, t), t), :]
            cw, lanes = where(w)
            st = jnp.dot(kj, q_ref[cw, :, lanes], preferred_element_type=F32)
            sbuf[w][slot][...] = st
            return jnp.max(st, axis=0, keepdims=True)

        def probs(w, slot, cmax, m, l):
            m_new = jnp.maximum(m, cmax)
            alpha = jnp.exp2(m - m_new)
            pt = jnp.exp2(sbuf[w][slot][...] - m_new)
            pbuf[w][slot][...] = pt.astype(BF16)
            return m_new, alpha * l + jnp.sum(pt, axis=0, keepdims=True), alpha

        def values(w, j, slot, alpha, acc):
            return alpha * acc + jnp.dot(v_ref[j], pbuf[w][slot][...], preferred_element_type=F32)

        init = []
        for w in range(n_par):
            m = jnp.full((1, tq), -1e30, F32)
            l = jnp.zeros((1, tq), F32)
            cmax0 = scores(w, 0, 0)
            cmax1 = scores(w, 1, 1)
            m, l, alpha0 = probs(w, 0, cmax0, m, l)
            init.append((m, l, jnp.zeros((ATTN_HEAD_DIM, tq), F32), cmax1, alpha0))

        def trip(n, carry):
            c = 2 * n
            out = []
            for w in range(n_par):
                m, l, acc, cmax_b, alpha_c = carry[w]
                acc = values(w, c, 0, alpha_c, acc)
                m, l, alpha1 = probs(w, 1, cmax_b, m, l)
                cmax2 = scores(w, c + 2, 0)
                acc = values(w, c + 1, 1, alpha1, acc)
                m, l, alpha2 = probs(w, 0, cmax2, m, l)
                cmax3 = scores(w, c + 3, 1)
                out.append((m, l, acc, cmax3, alpha2))
            return tuple(out)

        res = lax.fori_loop(0, nk // 2 - 1, trip, tuple(init))
        for w in range(n_par):
            m, l, acc, cmax_b, alpha_c = res[w]
            acc = values(w, nk - 2, 0, alpha_c, acc)
            m, l, alpha1 = probs(w, 1, cmax_b, m, l)
            acc = values(w, nk - 1, 1, alpha1, acc)
            cw, lanes = where(w)
            o_ref[cw, :, lanes] = (acc / l).astype(BF16)
            lse_ref[cw, :, lanes] = m + jnp.log2(l)

    return pl.pallas_call(
        body, name="attn_fwd", grid=(ATTN_HEADS, nq // n_ch),
        in_specs=[pl.BlockSpec((n_ch, LANES, t), lambda h, i: (i, h, 0)),
                  pl.BlockSpec((s, ATTN_KV_W), lambda h, i: (0, 0)),
                  pl.BlockSpec((nk, ATTN_HEAD_DIM, t), lambda h, i: (0, h // ATTN_GROUP, 0))],
        out_specs=[pl.BlockSpec((n_ch, ATTN_HEAD_DIM, t), lambda h, i: (i, h, 0)),
                   pl.BlockSpec((None, n_ch, 1, t), lambda h, i: (h, i, 0, 0))],
        out_shape=[jax.ShapeDtypeStruct((nq, ATTN_Q_W, t), BF16), jax.ShapeDtypeStruct((ATTN_HEADS, nq, 1, t), F32)],
        scratch_shapes=[pltpu.VMEM((t, tq), F32)] * (2 * n_par) + [pltpu.VMEM((t, tq), BF16)] * (2 * n_par),
        compiler_params=_cparams(("parallel", "parallel")),
    )(q_ct, k_rows, v_ct)


def _attn_bwd(q_ct, do_ct, lse, delta, k_rows, v_rows, k_ct):
    nq, _, t = q_ct.shape
    s = nq * t
    kc = 4 if nq % 4 == 0 else 2
    tk = kc * t
    nk = nq // kc
    assert nq % 2 == 0 and nq % kc == 0

    def body(q_ref, do_ref, lse_ref, delta_ref, k_ref, v_ref, kt_ref, dq_ref, dk_ref, dv_ref, dk_acc, dv_acc,
             sb0, sb1, db0, db1, pb0, pb1, gb0, gb1):
        j = pl.program_id(1)
        sb, db, pb, gb = (sb0, sb1), (db0, db1), (pb0, pb1), (gb0, gb1)

        @pl.when(j == 0)
        def _():
            dq_ref[...] = jnp.zeros(dq_ref.shape, F32)

        kj, vj = k_ref[...], v_ref[...]
        ktj = jnp.concatenate([kt_ref[u] for u in range(kc)], axis=1)
        dk_acc[...] = jnp.zeros(dk_acc.shape, F32)
        dv_acc[...] = jnp.zeros(dv_acc.shape, F32)

        def products(i, slot):
            sb[slot][...] = jnp.dot(kj, q_ref[i], preferred_element_type=F32)
            db[slot][...] = jnp.dot(vj, do_ref[i], preferred_element_type=F32)

        def cotangents(i, slot):
            pt = jnp.exp2(sb[slot][...] - lse_ref[i])
            pb[slot][...] = pt.astype(BF16)
            gb[slot][...] = (pt * (db[slot][...] - delta_ref[i])).astype(BF16)

        def accumulate(i, slot):
            dst = gb[slot][...]
            dv_acc[...] += _nt(pb[slot][...], do_ref[i])
            dk_acc[...] += _nt(dst, q_ref[i])
            dq_ref[i] += jnp.dot(ktj, dst, preferred_element_type=F32)

        products(0, 0)
        products(1, 1)
        cotangents(0, 0)

        def trip(n, carry):
            c = 2 * n
            accumulate(c, 0)
            cotangents(c + 1, 1)
            products(c + 2, 0)
            accumulate(c + 1, 1)
            cotangents(c + 2, 0)
            products(c + 3, 1)
            return carry

        lax.fori_loop(0, nq // 2 - 1, trip, 0)
        accumulate(nq - 2, 0)
        cotangents(nq - 1, 1)
        accumulate(nq - 1, 1)
        dk_ref[...] = dk_acc[...].astype(dk_ref.dtype)
        dv_ref[...] = dv_acc[...].astype(dv_ref.dtype)

    per_head = pl.BlockSpec((nq, LANES, t), lambda h, j: (0, h, 0))
    stat = pl.BlockSpec((None, nq, 1, t), lambda h, j: (h, 0, 0, 0))
    kv_rows = pl.BlockSpec((tk, ATTN_KV_W), lambda h, j: (j, 0))
    kv_out = pl.BlockSpec((None, tk, ATTN_KV_W), lambda h, j: (h, j, 0))
    return pl.pallas_call(
        body, name="attn_bwd", grid=(ATTN_HEADS, nk),
        in_specs=[per_head, per_head, stat, stat, kv_rows, kv_rows,
                  pl.BlockSpec((kc, ATTN_HEAD_DIM, t), lambda h, j: (j, h // ATTN_GROUP, 0))],
        out_specs=[pl.BlockSpec((nq, ATTN_HEAD_DIM, t), lambda h, j: (0, h, 0)), kv_out, kv_out],
        out_shape=[jax.ShapeDtypeStruct((nq, ATTN_Q_W, t), F32), jax.ShapeDtypeStruct((ATTN_HEADS, s, ATTN_KV_W), BF16),
                   jax.ShapeDtypeStruct((ATTN_HEADS, s, ATTN_KV_W), BF16)],
        scratch_shapes=([pltpu.VMEM((tk, ATTN_KV_W), F32)] * 2 + [pltpu.VMEM((tk, t), F32)] * 4 + [pltpu.VMEM((tk, t), BF16)] * 4),
        compiler_params=_cparams(("parallel", "arbitrary")),
    )(q_ct, do_ct, lse, delta, k_rows, v_rows, k_ct)


def _log_sigmoid(x):
    t = jnp.exp(-jnp.abs(x))
    log1p_t = jnp.where(t < 1e-2, t * (1.0 - t * (0.5 - t * (1.0 / 3.0))), jnp.log(1.0 + t))
    return jnp.minimum(x, 0.0) - log1p_t


def _decay_tables(logit, backward):
    c, hd = RET_CHUNK, RET_HEAD_DIM

    def lam(shape):
        return _log_sigmoid(jnp.full(shape, logit, F32))

    ii = lax.broadcasted_iota(jnp.int32, (c, c), 0).astype(F32)
    jj = lax.broadcasted_iota(jnp.int32, (c, c), 1).astype(F32)
    pos = lax.broadcasted_iota(jnp.int32, (c, hd), 0).astype(F32)
    if not backward:
        dist, dist_t = jnp.maximum(ii - jj, 0.0), jnp.maximum(jj - ii, 0.0)
        mask, mask_t = ii >= jj, jj >= ii
        e_q, e_k = pos + 1.0, (c - 1.0) - pos
    else:
        dist, dist_t = jnp.maximum(jj - ii, 0.0), jnp.maximum(ii - jj, 0.0)
        mask, mask_t = jj > ii, ii > jj
        e_q, e_k = c - pos, pos
    lam_cc, lam_row = lam((c, c)), lam((c, hd))
    return dict(
        d=jnp.where(mask, jnp.exp(lam_cc * dist), 0.0), d_t=jnp.where(mask_t, jnp.exp(lam_cc * dist_t), 0.0), dist=dist,
        qdec=jnp.exp(lam_row * e_q), kdec=jnp.exp(lam_row * e_k), e_q=e_q, e_k=e_k, gam=jnp.exp(lam((hd, hd)) * c))


def _nt(a, b):
    return lax.dot_general(a, b, (((1,), (1,)), ((), ())), preferred_element_type=F32)


def _ret_sub(n_chunks):
    return max(1, min(n_chunks, 512 // RET_CHUNK))


def _ret_fwd(logits, q, k, proj):
    s = q.shape[0]
    c = RET_CHUNK
    sub = _ret_sub(s // c)
    nb = s // (c * sub)
    block = (lambda n: n, lambda n: nb - 1 - n)
    order = (tuple(range(sub)), tuple(reversed(range(sub))))
    vwin = _win(proj, c * sub, C_RV, RET_W)
    nv = len(vwin)
    vw = RET_W // nv
    per = 2 + nv

    def body(lg_ref, *refs):
        ins, outs, states = refs[:2 * per], refs[2 * per:2 * per + 4], refs[2 * per + 4:]

        @pl.when(pl.program_id(0) == 0)
        def _():
            for st in states:
                st[...] = jnp.zeros(st.shape, F32)

        for h in range(RET_HEADS):
            for d in range(2):
                q_ref, k_ref, v_refs = ins[d * per], ins[d * per + 1], ins[d * per + 2:(d + 1) * per]
                y_ref, st_ref, state = outs[2 * d], outs[2 * d + 1], states[d]
                tb = _decay_tables(lg_ref[d, h], bool(d))
                sl = slice(h * RET_HEAD_DIM, (h + 1) * RET_HEAD_DIM)
                off = h * RET_HEAD_DIM
                sh = state[h]
                for u in order[d]:
                    rows = slice(u * c, (u + 1) * c)
                    qh, kh = q_ref[rows, sl], k_ref[rows, sl]
                    vb = v_refs[off // vw][rows, off % vw:off % vw + RET_HEAD_DIM].astype(BF16)
                    a = _nt(qh.astype(BF16), kh.astype(BF16)) * tb["d"]
                    st_ref[u, h] = sh
                    y_ref[rows, sl] = (jnp.dot(a.astype(BF16), vb, preferred_element_type=F32)
                                       + jnp.dot((qh * tb["qdec"]).astype(BF16), sh.astype(BF16), preferred_element_type=F32))
                    sh = tb["gam"] * sh + jnp.dot((kh * tb["kdec"]).T.astype(BF16), vb, preferred_element_type=F32)
                state[h] = sh

    hmat = (RET_HEADS, RET_HEAD_DIM, RET_HEAD_DIM)
    in_specs, out_specs, args = [pl.BlockSpec(memory_space=pltpu.SMEM)], [], [logits]
    for d in range(2):
        rows = pl.BlockSpec((c * sub, RET_W), lambda n, d=d: (block[d](n), 0))
        in_specs += [rows, rows] + [pl.BlockSpec(sp.block_shape, lambda n, d=d, cb=sp.index_map(0)[1]: (block[d](n), cb)) for _, sp in vwin]
        args += [q, k] + [a for a, _ in vwin]
        out_specs += [rows, pl.BlockSpec((sub,) + hmat, lambda n, d=d: (block[d](n), 0, 0, 0))]
    return pl.pallas_call(
        body, name="ret_fwd", grid=(nb,), in_specs=in_specs, out_specs=out_specs,
        out_shape=[jax.ShapeDtypeStruct((s, RET_W), F32), jax.ShapeDtypeStruct((nb * sub,) + hmat, F32)] * 2,
        scratch_shapes=[pltpu.VMEM(hmat, F32)] * 2,
        compiler_params=_cparams(("arbitrary",)),
    )(*args)


def _ret_bwd(logits, q, k, proj, dy, st_f, st_b):
    s = q.shape[0]
    c = RET_CHUNK
    sub = _ret_sub(s // c)
    nb = s // (c * sub)
    block = (lambda n: nb - 1 - n, lambda n: n)
    order = (tuple(reversed(range(sub))), tuple(range(sub)))
    vwin = _win(proj, c * sub, C_RV, RET_W)
    nv = len(vwin)
    vw = RET_W // nv
    per = 4 + nv

    def body(lg_ref, *refs):
        ins, outs, scr = refs[:2 * per], refs[2 * per:2 * per + 8], refs[2 * per + 8:]
        n = pl.program_id(0)

        @pl.when(n == 0)
        def _():
            for r in scr:
                r[...] = jnp.zeros(r.shape, F32)

        for h in range(RET_HEADS):
            for d in range(2):
                q_ref, k_ref, dy_ref, st_ref = ins[d * per:d * per + 4]
                v_refs = ins[d * per + 4:(d + 1) * per]
                dq_ref, dk_ref, dv_ref = outs[4 * d:4 * d + 3]
                dstate, lacc = scr[2 * d], scr[2 * d + 1]
                tb = _decay_tables(lg_ref[d, h], bool(d))
                sl = slice(h * RET_HEAD_DIM, (h + 1) * RET_HEAD_DIM)
                off = h * RET_HEAD_DIM
                dsh = dstate[h]
                lsum = lacc[h, 0:1, :]
                for u in order[d]:
                    rows = slice(u * c, (u + 1) * c)
                    qh, kh, dyh = q_ref[rows, sl], k_ref[rows, sl], dy_ref[rows, sl]
                    vb = v_refs[off // vw][rows, off % vw:off % vw + RET_HEAD_DIM].astype(BF16)
                    qb, kb, dyb = qh.astype(BF16), kh.astype(BF16), dyh.astype(BF16)
                    sh = st_ref[u, h]
                    shb, dshb = sh.astype(BF16), dsh.astype(BF16)
                    qk = _nt(qb, kb)
                    g = _nt(dyb, vb) * tb["d"]
                    a_t = _nt(kb, qb) * tb["d_t"]
                    g_t = _nt(vb, dyb) * tb["d_t"]
                    qd, kd = qh * tb["qdec"], kh * tb["kdec"]
                    dqd = _nt(dyb, shb)
                    dkd = _nt(vb, dshb)
                    dq_ref[rows, sl] = (jnp.dot(g.astype(BF16), kb, preferred_element_type=F32) + dqd * tb["qdec"]).astype(dq_ref.dtype)
                    dk_ref[rows, sl] = (jnp.dot(g_t.astype(BF16), qb, preferred_element_type=F32) + dkd * tb["kdec"]).astype(dk_ref.dtype)
                    dv_ref[rows, sl] = (jnp.dot(a_t.astype(BF16), dyb, preferred_element_type=F32)
                                        + jnp.dot(kd.astype(BF16), dshb, preferred_element_type=F32)).astype(dv_ref.dtype)
                    intra = jnp.sum(tb["dist"] * qk * g, axis=0, keepdims=True)
                    lsum = (lsum + sum(intra[:, o:o + LANES] for o in range(0, c, LANES))
                            + jnp.sum(tb["e_q"] * qd * dqd + tb["e_k"] * kd * dkd, axis=0, keepdims=True)
                            + jnp.sum(float(c) * tb["gam"] * dsh * sh, axis=0, keepdims=True))
                    dsh = tb["gam"] * dsh + jnp.dot(qd.T.astype(BF16), dyb, preferred_element_type=F32)
                dstate[h] = dsh
                lacc[h, 0:1, :] = lsum

        @pl.when(n == nb - 1)
        def _():
            for d in range(2):
                for h in range(RET_HEADS):
                    outs[4 * d + 3][h] = jnp.zeros((8, LANES), F32) + jnp.sum(scr[2 * d + 1][h])

    hmat = (RET_HEADS, RET_HEAD_DIM, RET_HEAD_DIM)
    in_specs, out_specs, args = [pl.BlockSpec(memory_space=pltpu.SMEM)], [], [logits]
    for d, states in enumerate((st_f, st_b)):
        rows = pl.BlockSpec((c * sub, RET_W), lambda n, d=d: (block[d](n), 0))
        in_specs += ([rows, rows, rows, pl.BlockSpec((sub,) + hmat, lambda n, d=d: (block[d](n), 0, 0, 0))]
                     + [pl.BlockSpec(sp.block_shape, lambda n, d=d, cb=sp.index_map(0)[1]: (block[d](n), cb)) for _, sp in vwin])
        args += [q, k, dy, states] + [a for a, _ in vwin]
        out_specs += [rows, rows, rows, pl.BlockSpec((RET_HEADS, 8, LANES), lambda n: (0, 0, 0))]
    return pl.pallas_call(
        body, name="ret_bwd", grid=(nb,), in_specs=in_specs, out_specs=out_specs,
        out_shape=([jax.ShapeDtypeStruct((s, RET_W), BF16)] * 3 + [jax.ShapeDtypeStruct((RET_HEADS, 8, LANES), F32)]) * 2,
        scratch_shapes=[pltpu.VMEM(hmat, F32), pltpu.VMEM((RET_HEADS, 8, LANES), F32)] * 2,
        compiler_params=_cparams(("arbitrary",)),
    )(*args)


def _local_step(x, p, target, w, small):
    s = x.shape[0]
    tabs = _rope_tables(s, ATTN_HEAD_DIM) + _rope_tables(s, RET_HEAD_DIM)
    g_mix, g_mlp, g_ple = small["mix_norm"][None, :], small["mlp_norm"][None, :], small["ple_norm"][None, :]
    g_final, g_ret = small["final_norm"][None, :], small["ret_norm_gain"][None, :]
    gq_w = jnp.tile(small["attn_q_norm"], ATTN_HEADS)[None, :]
    gk_w = jnp.tile(small["attn_k_norm"], ATTN_KV_HEADS)[None, :]
    logits = small["ret_decay_logit"]

    hb = _stage_norm_in(x, g_mix)
    proj = _mm("in_proj", hb, w["w_in"], tm=1024, tn=IN_W // 2, tk=1024, out_dtypes=(BF16,), j_outer=True)
    q_ct, k_rows, k_ct, v_rows, v_ct, rq, rk = _stage_qkv(proj, tabs, gq_w, gk_w)
    o_ct, lse = _attn_fwd(q_ct, k_rows, v_ct)
    ry_f, st_f, ry_b, st_b = _ret_fwd(logits, rq, rk, proj)
    rz, attn_rows = _stage_mix_post(ry_f, ry_b, proj, o_ct, g_ret)
    a_out = _mm("attn_o", attn_rows, w["w_attn_o"], tm=1024, tn=1024, tk=512, out_dtypes=(BF16,))
    n_gate = D_MODEL // EPI_PIECE

    def epi_merge(acc, e, c):
        ga, gr = _cat(e[1:1 + n_gate]), _cat(e[1 + n_gate:1 + 2 * n_gate])
        return acc, _sigmoid(ga) * e[0][...] + _sigmoid(gr) * acc

    r_out, merged = _mm("ret_o", rz, w["w_ret_o"], tm=1024, tn=1024, tk=512, out_dtypes=(BF16, BF16), epi=epi_merge,
                        epi_ins=(a_out, (proj, C_GA), (proj, C_GR)))

    def epi_res_norm(acc, e, c):
        xr = e[0][...] + acc
        return xr, _rms_fwd(xr, c[0][...])

    x1, hm = _mm("out_proj", merged, w["w_out"], tm=1024, tn=1024, tk=1024, out_dtypes=(F32, BF16),
                 epi=epi_res_norm, epi_ins=(x,), consts=(g_mlp,))

    def epi_relu2(acc, e, c):
        r = jnp.maximum(acc, 0.0)
        return (r * r,)

    act = _mm("mlp_up", hm, w["w_up"], tm=1024, tn=2048, tk=1024, out_dtypes=(BF16,), epi=epi_relu2, j_outer=True)
    x2, hp = _mm("mlp_down", act, w["w_down"], tm=512, tn=1024, tk=D_FF, out_dtypes=(F32, BF16),
                 epi=epi_res_norm, epi_ins=(x1,), consts=(g_ple,))
    pe = _mm("ple_emb", p, w["w_ple"], tm=1024, tn=1024, tk=256)

    def epi_head(acc, e, c):
        gt = _sigmoid(acc)
        pe_, gf = e[0][...], c[0][...]
        x3 = e[1][...] + gt * pe_
        r3 = lax.rsqrt(jnp.mean(x3 * x3, axis=-1, keepdims=True) + NORM_EPS)
        x3n = x3 * r3
        err = x3n * gf - e[2][...]
        dy = err * (1.0 / D_MODEL)
        dyg = dy * gf
        dx3 = r3 * (dyg - x3n * jnp.mean(dyg * x3n, axis=-1, keepdims=True))
        return dx3, dx3 * pe_ * gt * (1.0 - gt), dx3 * gt, err * err, dy * x3n

    dx3, dzg, dpe, loss_cols, g_final_p = _mm("ple_gate", hp, w["w_ple_gate"], tm=1024, tn=1024, tk=1024, out_dtypes=(F32, BF16, BF16),
                                              epi=epi_head, epi_ins=(pe, x2, target), consts=(g_final,), n_sums=2)
    loss_sum = 0.5 / D_MODEL * jnp.sum(loss_cols)

    gw = {}
    gw["w_ple"] = _mm("g_w_ple", p, dpe, ta=True, tm=256, tn=1024, tk=2048)
    gw["w_ple_gate"] = _mm("g_w_ple_gate", hp, dzg, ta=True, tm=1024, tn=1024, tk=4096)
    def epi_norm_bwd(acc, e, c):
        dx, dg = _rms_bwd(acc, e[0][...], c[0][...])
        return e[1][...] + dx, dg

    def epi_norm_bwd_b(acc, e, c):
        tot, dg = epi_norm_bwd(acc, e, c)
        return tot, tot, dg

    dx2, dx2_b, g_ple_p = _mm("d_hp", dzg, w["w_ple_gate"], tb=True, tm=1024, tn=1024, tk=1024, out_dtypes=(F32, BF16),
                              epi=epi_norm_bwd_b, epi_ins=(x2, dx3), consts=(g_ple,), n_sums=1)

    def epi_relu2_bwd(acc, e, c):
        return (acc * (2.0 * jnp.sqrt(e[0][...]).astype(F32)),)

    du = _mm("d_u", dx2_b, w["w_down"], tb=True, tm=1024, tn=2048, tk=1024, out_dtypes=(BF16,), epi=epi_relu2_bwd, epi_ins=(act,),
             j_outer=True)
    gw["w_down"] = _mm("g_w_down", act, dx2_b, ta=True, tm=1024, tn=1024, tk=4096)
    gw["w_up"] = _mm("g_w_up", hm, du, ta=True, tm=1024, tn=1024, tk=4096)
    dx1, dx1_b, g_mlp_p = _mm("d_hm", du, w["w_up"], tb=True, tm=512, tn=1024, tk=D_FF, out_dtypes=(F32, BF16),
                              epi=epi_norm_bwd_b, epi_ins=(x1, dx2), consts=(g_mlp,), n_sums=1)
    def epi_merge_bwd(acc, e, c):
        sa, sr = _sigmoid(_cat(e[2:2 + n_gate])), _sigmoid(_cat(e[2 + n_gate:2 + 2 * n_gate]))
        return acc * sa, acc * sr, acc * e[0][...] * sa * (1.0 - sa), acc * e[1][...] * sr * (1.0 - sr)

    dao, dro, dga, dgr = _mm("d_merged", dx1_b, w["w_out"], tb=True, tm=1024, tn=1024, tk=1024, out_dtypes=(BF16,) * 4,
                             epi=epi_merge_bwd, epi_ins=(a_out, r_out, (proj, C_GA), (proj, C_GR)))
    gw["w_out"] = _mm("g_w_out", merged, dx1_b, ta=True, tm=1024, tn=1024, tk=4096)
    gw["w_attn_o"] = _mm("g_w_attn_o", attn_rows, dao, ta=True, tm=512, tn=1024, tk=2048)
    gw["w_ret_o"] = _mm("g_w_ret_o", rz, dro, ta=True, tm=512, tn=1024, tk=2048)
    dattn = _mm("d_attn", dao, w["w_attn_o"], tb=True, tm=1024, tn=512, tk=1024)
    drz = _mm("d_rz", dro, w["w_ret_o"], tb=True, tm=1024, tn=512, tk=1024)
    do_ct, delta, dry, drg, g_ret_p = _stage_mix_post_bwd(dattn, attn_rows, drz, ry_f, ry_b, proj, g_ret)
    dq_f, dk_f, dv_f, dl_f, dq_b, dk_b, dv_b, dl_b = _ret_bwd(logits, rq, rk, proj, dry, st_f, st_b)
    dq_ct, dk8, dv8 = _attn_bwd(q_ct, do_ct, lse, delta, k_rows, v_rows, k_ct)
    dproj, gq_p, gk_p = _stage_dproj(proj, dq_ct, dk8, dv8, (dq_f, dk_f, dv_f, dq_b, dk_b, dv_b), drg, dga, dgr, tabs, gq_w, gk_w)
    gw["w_in"] = _mm("g_w_in", hb, dproj, ta=True, tm=512, tn=IN_W // 2, tk=2048)
    grad_x, g_mix_p = _mm("d_h", dproj, w["w_in"], tb=True, tm=512, tn=1024, tk=IN_W, epi=epi_norm_bwd, epi_ins=(x, dx1),
                          consts=(g_mix,), n_sums=1)

    gs = {
        "mix_norm": g_mix_p[0], "mlp_norm": g_mlp_p[0], "ple_norm": g_ple_p[0], "final_norm": g_final_p[0],
        "ret_norm_gain": g_ret_p[0],
        "attn_q_norm": jnp.sum(gq_p[0].reshape(ATTN_HEADS, ATTN_HEAD_DIM), axis=0),
        "attn_k_norm": jnp.sum(gk_p[0].reshape(ATTN_KV_HEADS, ATTN_HEAD_DIM), axis=0),
        "ret_decay_logit": jnp.stack([dl_f[:, 0, 0], dl_b[:, 0, 0]]),
    }
    return loss_sum, grad_x, gw, gs


PACK_COLS = 1024
N_CHIPS = 4
HALF_ROWS = 2048


def _pack_shard(parts):
    return jnp.concatenate([parts[n].reshape(-1, PACK_COLS) for n, _ in BIG], axis=0)


def _unpack_shard(slab, shapes):
    out, r = {}, 0
    for n, _ in BIG:
        rows = math.prod(shapes[n]) // PACK_COLS
        out[n] = slab[r:r + rows].reshape(shapes[n])
        r += rows
    return out


def _shard_of(full, axis, sidx):
    size = full.shape[axis] // N_CHIPS
    return lax.slice_in_dim(full, sidx * size, (sidx + 1) * size, axis=axis)


def _position():
    x, y, c = lax.axis_index("x"), lax.axis_index("y"), lax.axis_index("c")
    return x, y, c


def _other_chips(x, y):
    return [(1 - x, y), (x, 1 - y), (1 - x, 1 - y)]


ANY = pl.BlockSpec(memory_space=pl.ANY)


def _gather_weights(slab):
    rows = slab.shape[0]
    half = rows // 2

    def body(in_ref, out_ref, send_sems, recv_sems):
        x, y, c = _position()
        chips = _other_chips(x, y)

        def piece(chip, core):
            return out_ref.at[2 * chip[0] + chip[1], pl.ds(core * half, half), :]

        def copy(k, chip, core, to, src=None):
            return pltpu.make_async_remote_copy(
                src_ref=piece(chip, core) if src is None else src, dst_ref=piece(chip, core),
                send_sem=send_sems.at[k], recv_sem=recv_sems.at[k], device_id=to, device_id_type=MESH)

        first = [copy(j, (x, y), c, (*chip, c), src=in_ref.at[pl.ds(c * half, half), :]) for j, chip in enumerate(chips)]
        for cp in first:
            cp.start()
        passed = [copy(3 + j, chip, c, (x, y, 1 - c)) for j, chip in enumerate(chips)]
        for j, chip in enumerate(chips):
            copy(j, chip, c, (x, y, c)).wait_recv()
            passed[j].start()
        for j, chip in enumerate(chips):
            copy(3 + j, chip, 1 - c, (x, y, c)).wait_recv()
        for cp in first + passed:
            cp.wait_send()

    return pl.pallas_call(
        body, name="gather_weights", in_specs=[ANY], out_specs=ANY,
        out_shape=jax.ShapeDtypeStruct((N_CHIPS,) + slab.shape, slab.dtype),
        scratch_shapes=[pltpu.SemaphoreType.DMA((6,)), pltpu.SemaphoreType.DMA((6,))],
    )(slab)


def _exchange_halves(g):
    def body(g_ref, out_ref, send_sem, recv_sem):
        x, y, c = _position()
        cp = pltpu.make_async_remote_copy(src_ref=g_ref.at[pl.ds(0, N_CHIPS), 1 - c], dst_ref=out_ref, send_sem=send_sem,
                                          recv_sem=recv_sem, device_id=(x, y, 1 - c), device_id_type=MESH)
        cp.start()
        cp.wait()

    return pl.pallas_call(
        body, name="exchange_halves", in_specs=[ANY], out_specs=ANY,
        out_shape=jax.ShapeDtypeStruct((N_CHIPS,) + g.shape[2:], g.dtype),
        scratch_shapes=[pltpu.SemaphoreType.DMA, pltpu.SemaphoreType.DMA],
    )(g)


def _add_my_half(g, r1, c_idx):
    tr = 256
    nt = g.shape[2] // tr
    out = (N_CHIPS,) + g.shape[2:]

    def body(c_ref, g_ref, r_ref, o_ref, ob_ref):
        tot = g_ref[...] + r_ref[...]
        o_ref[...] = tot
        ob_ref[...] = tot.astype(BF16)

    blk = (None, tr, PACK_COLS)
    spec = pl.BlockSpec(blk, lambda s, i, c_ref: (s, i, 0))
    return pl.pallas_call(
        body, name="add_my_half",
        grid_spec=pltpu.PrefetchScalarGridSpec(
            num_scalar_prefetch=1, grid=(N_CHIPS, nt),
            in_specs=[pl.BlockSpec((None,) + blk, lambda s, i, c_ref: (s, c_ref[0], i, 0)), spec],
            out_specs=[spec, spec]),
        out_shape=[jax.ShapeDtypeStruct(out, F32), jax.ShapeDtypeStruct(out, BF16)],
        compiler_params=_cparams(("parallel", "parallel")),
    )(c_idx, g, r1)


def _scatter_to_chips(part):
    def body(p_ref, out_ref, send_sems, recv_sems):
        x, y, c = _position()
        chips = _other_chips(x, y)
        sends = [pltpu.make_async_remote_copy(
            src_ref=p_ref.at[2 * chip[0] + chip[1]], dst_ref=out_ref.at[j], send_sem=send_sems.at[j], recv_sem=recv_sems.at[j],
            device_id=(*chip, c), device_id_type=MESH) for j, chip in enumerate(chips)]
        for cp in sends:
            cp.start()
        for cp in sends:
            cp.wait()

    return pl.pallas_call(
        body, name="scatter_to_chips", in_specs=[ANY], out_specs=ANY,
        out_shape=jax.ShapeDtypeStruct((N_CHIPS - 1,) + part.shape[1:], part.dtype),
        scratch_shapes=[pltpu.SemaphoreType.DMA((3,)), pltpu.SemaphoreType.DMA((3,))],
    )(part)


def _sum_chips(part, r2, chip_idx):
    tr = 256

    def body(c_ref, p_ref, r_ref, o_ref):
        o_ref[...] = ((p_ref[...] + r_ref[0]) + r_ref[1]) + r_ref[2]

    return pl.pallas_call(
        body, name="sum_chips",
        grid_spec=pltpu.PrefetchScalarGridSpec(
            num_scalar_prefetch=1, grid=(r2.shape[1] // tr,),
            in_specs=[pl.BlockSpec((None, tr, PACK_COLS), lambda i, c_ref: (c_ref[0], i, 0)),
                      pl.BlockSpec((N_CHIPS - 1, tr, PACK_COLS), lambda i, c_ref: (0, i, 0))],
            out_specs=pl.BlockSpec((tr, PACK_COLS), lambda i, c_ref: (i, 0))),
        out_shape=jax.ShapeDtypeStruct(r2.shape[1:], F32),
        compiler_params=_cparams(("parallel",)),
    )(chip_idx, part, r2)


def _join_halves(red):
    def body(r_ref, out_ref, send_sem, recv_sem):
        x, y, c = _position()
        cp = pltpu.make_async_remote_copy(src_ref=r_ref, dst_ref=out_ref, send_sem=send_sem, recv_sem=recv_sem,
                                          device_id=(x, y, 1 - c), device_id_type=MESH)
        cp.start()
        cp.wait()

    return pl.pallas_call(
        body, name="join_halves", in_specs=[ANY], out_specs=ANY,
        out_shape=jax.ShapeDtypeStruct(red.shape, red.dtype),
        scratch_shapes=[pltpu.SemaphoreType.DMA, pltpu.SemaphoreType.DMA],
    )(red)


def _adamw_math(w, g, m, v):
    m = ADAM_B1 * m + (1.0 - ADAM_B1) * g
    v = ADAM_B2 * v + (1.0 - ADAM_B2) * (g * g)
    m_hat = m / (1.0 - ADAM_B1 ** ADAM_STEP)
    v_hat = v / (1.0 - ADAM_B2 ** ADAM_STEP)
    delta = -ADAM_LR * (m_hat / (jnp.sqrt(v_hat) + ADAM_EPS) + ADAM_WD * w)
    return delta, m, v


def _adamw(name, w, g, m, v):
    tr = min(256, w.shape[0])

    def body(w_ref, g_ref, m_ref, v_ref, d_ref, nm_ref, nv_ref):
        d_ref[...], nm_ref[...], nv_ref[...] = _adamw_math(w_ref[...], g_ref[...], m_ref[...], v_ref[...])

    blk = pl.BlockSpec((tr, w.shape[1]), lambda i: (i, 0))
    return pl.pallas_call(
        body, name="adamw_" + name, grid=(w.shape[0] // tr,), in_specs=[blk] * 4, out_specs=[blk] * 3,
        out_shape=[jax.ShapeDtypeStruct(w.shape, F32)] * 3, compiler_params=_cparams(("parallel",)),
    )(w, g, m, v)


def _small_step(gpk, wpk, mpk, vpk):
    row, col, width = SMALL["ret_decay_logit"]

    def body(g_ref, w_ref, m_ref, v_ref, og_ref, od_ref, om_ref, ov_ref, gbuf, send_sems, recv_sems):
        x, y, c = _position()
        me = 4 * x + 2 * y + c
        gbuf[me] = g_ref[...]
        sends = []
        for k in range(1, 8):
            to = (x ^ (k >> 2), y ^ ((k >> 1) & 1), c ^ (k & 1))
            cp = pltpu.make_async_remote_copy(src_ref=g_ref, dst_ref=gbuf.at[me], send_sem=send_sems.at[k - 1],
                                              recv_sem=recv_sems.at[k - 1], device_id=to, device_id_type=MESH)
            cp.start()
            sends.append(cp)
        for k in range(1, 8):
            frm = me ^ k
            pltpu.make_async_remote_copy(src_ref=g_ref, dst_ref=gbuf.at[frm], send_sem=send_sems.at[k - 1],
                                         recv_sem=recv_sems.at[k - 1], device_id=(x, y, c), device_id_type=MESH).wait_recv()
        for cp in sends:
            cp.wait_send()
        tot = gbuf[0]
        for d in range(1, 8):
            tot = tot + gbuf[d]
        w = w_ref[...]
        r_i = lax.broadcasted_iota(jnp.int32, w.shape, 0)
        c_i = lax.broadcasted_iota(jnp.int32, w.shape, 1)
        is_logit = (r_i == row) & (c_i >= col) & (c_i < col + width)
        g = jnp.where(is_logit, tot * _sigmoid(-w), tot)
        og_ref[...] = g
        od_ref[...], om_ref[...], ov_ref[...] = _adamw_math(w, g, m_ref[...], v_ref[...])

    vm = pl.BlockSpec(memory_space=pltpu.VMEM)
    shp = jax.ShapeDtypeStruct(gpk.shape, F32)
    return pl.pallas_call(
        body, name="small_step", in_specs=[vm] * 4, out_specs=[vm] * 4, out_shape=[shp] * 4,
        scratch_shapes=[pltpu.VMEM((8,) + gpk.shape, F32), pltpu.SemaphoreType.DMA((7,)), pltpu.SemaphoreType.DMA((7,))],
    )(gpk, wpk, mpk, vpk)


def _pack_small(parts):
    rows = [[] for _ in range(SMALL_ROWS)]
    for n, (r, col, width) in sorted(SMALL.items(), key=lambda kv: (kv[1][0], kv[1][1])):
        rows[r].append((col, parts[n].reshape(-1).astype(F32)))
    out = []
    for r in range(SMALL_ROWS):
        segs, pos = [], 0
        for col, vec in rows[r]:
            assert col == pos
            segs.append(vec)
            pos += vec.shape[0]
        if pos < PACK_COLS:
            segs.append(jnp.zeros((PACK_COLS - pos,), F32))
        out.append(jnp.concatenate(segs))
    return jnp.stack(out)


def _unpack_small(pk, shapes):
    return {n: pk[r, col:col + width].reshape(shapes[n]) for n, (r, col, width) in SMALL.items()}


WEIGHTS = ("mix_norm", "w_in", "attn_q_norm", "attn_k_norm", "ret_decay_logit", "ret_norm_gain", "w_attn_o", "w_ret_o", "w_out",
           "mlp_norm", "w_up", "w_down", "ple_norm", "w_ple_gate", "w_ple", "final_norm")


def kernel(x, p, mix_norm, w_in, attn_q_norm, attn_k_norm, ret_decay_logit, ret_norm_gain, w_attn_o, w_ret_o, w_out, mlp_norm, w_up, w_down, ple_norm, w_ple_gate, w_ple, final_norm, loss_target, m_mix_norm, m_w_in, m_attn_q_norm, m_attn_k_norm, m_ret_decay_logit, m_ret_norm_gain, m_w_attn_o, m_w_ret_o, m_w_out, m_mlp_norm, m_w_up, m_w_down, m_ple_norm, m_w_ple_gate, m_w_ple, m_final_norm, v_mix_norm, v_w_in, v_attn_q_norm, v_attn_k_norm, v_ret_decay_logit, v_ret_norm_gain, v_w_attn_o, v_w_ret_o, v_w_out, v_mlp_norm, v_w_up, v_w_down, v_ple_norm, v_w_ple_gate, v_w_ple, v_final_norm):
    args = dict(locals())
    wts = {n: args[n] for n in WEIGHTS}
    ms = {n: args["m_" + n] for n in WEIGHTS}
    vs = {n: args["v_" + n] for n in WEIGHTS}
    shapes = {n: wts[n].shape for n in WEIGHTS}
    big_names = [n for n, _ in BIG]
    xi, yi, ci = _position()
    c_idx = ci.astype(jnp.int32).reshape(1)

    chip_idx = (2 * xi + yi).astype(jnp.int32)
    slab_b = _pack_shard({n: wts[n][0].astype(BF16) for n in big_names})
    gathered = lax.dynamic_update_slice(_gather_weights(slab_b), slab_b[None], (chip_idx, 0, 0))
    full, r0 = {}, 0
    for n, axis in BIG:
        shard = shapes[n][1:]
        rows = math.prod(shard) // PACK_COLS
        if axis == 0 and shard[1] == PACK_COLS:
            full[n] = gathered[:, r0:r0 + rows].reshape(N_CHIPS * shard[0], shard[1])
        else:
            full[n] = jnp.concatenate([gathered[k, r0:r0 + rows].reshape(shard) for k in range(N_CHIPS)], axis=axis)
        r0 += rows
    small = {n: wts[n].reshape(wts[n].shape[1:] if wts[n].ndim > 1 else wts[n].shape) for n in SMALL}

    loss_part, grad_x, gw, gs = _local_step(x[0], p[0, 0], loss_target[0], full, small)
    loss = lax.psum(loss_part, ("x", "y", "c"))

    slabs = jnp.stack([_pack_shard({n: _shard_of(gw[n], axis, k) for n, axis in BIG}) for k in range(N_CHIPS)])
    halves = slabs.reshape(N_CHIPS, 2, HALF_ROWS, PACK_COLS)
    chip_part, chip_part_b = _add_my_half(halves, _exchange_halves(halves), c_idx)
    mine = _sum_chips(chip_part, _scatter_to_chips(chip_part_b), chip_idx.reshape(1))
    both = jnp.stack([mine, _join_halves(mine)])
    reduced = jnp.where(ci == 0, both, both[::-1]).reshape(2 * HALF_ROWS, PACK_COLS)
    g_big = _unpack_shard(reduced, {n: shapes[n][1:] for n in big_names})
    big_out = [{}, {}, {}, {}]
    for n in big_names:
        big_out[0][n] = g_big[n][None]
        for kind, a in enumerate(_adamw(n, wts[n][0], g_big[n], ms[n][0], vs[n][0])):
            big_out[kind + 1][n] = a[None]

    sm_out = _small_step(_pack_small(gs), _pack_small({n: wts[n] for n in SMALL}), _pack_small({n: ms[n] for n in SMALL}),
                         _pack_small({n: vs[n] for n in SMALL}))
    small_out = [_unpack_small(a, {n: shapes[n] for n in SMALL}) for a in sm_out]

    outs = [loss, grad_x[None]]
    for kind in range(4):
        for n in WEIGHTS:
            outs.append(small_out[kind][n] if n in SMALL else big_out[kind][n])
    return tuple(outs)
```

```python
import math

import jax
import jax.numpy as jnp
from jax import lax
from jax.experimental import pallas as pl
from jax.experimental.pallas import tpu as pltpu

F32 = jnp.float32
BF16 = jnp.bfloat16
MESH = pl.DeviceIdType.MESH

D_MODEL = 1024
GRID_W = 64
ATTN_HEAD_DIM = 64
ATTN_HEADS = 8
ATTN_KV_HEADS = 2
ATTN_GROUP = ATTN_HEADS // ATTN_KV_HEADS
RET_HEAD_DIM = 128
RET_HEADS = 4
ATTN_Q_W = 512
ATTN_KV_W = 128
RET_W = 512
IN_W = 4864
D_FF = 4096
RET_CHUNK = 256
ROPE_THETA = 10000.0
NORM_EPS = 1e-6
GN_EPS = 1e-5
ATTN_SCALE = ATTN_HEAD_DIM ** -0.5
LOG2E = math.log2(math.e)
Q_FOLD = ATTN_SCALE * LOG2E
RET_SCALE = RET_HEAD_DIM ** -0.5

C_AQ, C_AK, C_AV, C_RQ, C_RK, C_RV, C_RG, C_GA, C_GR = 0, 512, 640, 768, 1280, 1792, 2304, 2816, 3840

ADAM_LR = 0.001
ADAM_B1 = 0.9
ADAM_B2 = 0.999
ADAM_EPS = 1e-08
ADAM_WD = 0.01
ADAM_STEP = 10

LANES = 128
VMEM_LIMIT = 56 << 20
SEQ_TILE = 512
EPI_PIECE = 256

BIG = (("w_in", 1), ("w_attn_o", 1), ("w_ret_o", 1), ("w_out", 0), ("w_up", 1), ("w_down", 0), ("w_ple_gate", 0), ("w_ple", 1))
SMALL_ROWS = 8
SMALL = {"mix_norm": (0, 0, 1024), "mlp_norm": (1, 0, 1024), "ple_norm": (2, 0, 1024), "final_norm": (3, 0, 1024),
         "ret_norm_gain": (4, 0, 512), "attn_q_norm": (4, 512, 64), "attn_k_norm": (4, 576, 64), "ret_decay_logit": (4, 640, 8)}


def _seq_tile(s):
    return min(SEQ_TILE, s // 2)


def _cparams(sem=None, vmem=VMEM_LIMIT):
    return pltpu.CompilerParams(dimension_semantics=sem, vmem_limit_bytes=vmem)


def _mm(name, a, b, *, ta=False, tb=False, tm, tn, tk, out_dtypes=(F32,), epi=None, epi_ins=(), consts=(), n_sums=0, j_outer=False):
    if ta:
        kdim, m = a.shape
    else:
        m, kdim = a.shape
    n = b.shape[0] if tb else b.shape[1]
    tm, tn, tk = min(tm, m), min(tn, n), min(tk, kdim)
    assert m % tm == 0 and n % tn == 0 and kdim % tk == 0, (name, m, n, kdim, tm, tn, tk)
    nk = kdim // tk
    e_arrs, e_cols = [], []
    for item in epi_ins:
        if isinstance(item, tuple):
            arr, start = item
            assert tn == n and start % EPI_PIECE == 0 and n % EPI_PIECE == 0
            for piece in range(n // EPI_PIECE):
                e_arrs.append(arr)
                e_cols.append(start // EPI_PIECE + piece)
        else:
            e_arrs.append(item)
            e_cols.append(None)
    n_e, n_c, n_o = len(e_arrs), len(consts), len(out_dtypes)
    assert n_sums == 0 or tn == n

    def body(*refs):
        a_ref, b_ref = refs[0], refs[1]
        e_refs = refs[2:2 + n_e]
        c_refs = refs[2 + n_e:2 + n_e + n_c]
        o_refs = refs[2 + n_e + n_c:2 + n_e + n_c + n_o]
        s_refs = refs[2 + n_e + n_c + n_o:2 + n_e + n_c + n_o + n_sums]
        acc_ref = refs[2 + n_e + n_c + n_o + n_sums] if nk > 1 else None
        k = pl.program_id(2)
        if n_sums:
            @pl.when((pl.program_id(1 if j_outer else 0) == 0) & (k == 0))
            def _():
                for r in s_refs:
                    r[...] = jnp.zeros(r.shape, F32)
        av = a_ref[...].astype(BF16)
        bv = b_ref[...].astype(BF16)
        dims = (((0,) if ta else (1,), (1,) if tb else (0,)), ((), ()))
        part = lax.dot_general(av, bv, dims, preferred_element_type=F32)

        def finish(acc):
            vals = epi(acc, e_refs, c_refs) if epi is not None else (acc,)
            for o_ref, v in zip(o_refs, vals[:n_o]):
                o_ref[...] = v.astype(o_ref.dtype)
            for s_ref, v in zip(s_refs, vals[n_o:]):
                _acc_add(s_ref, v)

        if nk == 1:
            finish(part)
        else:
            @pl.when(k == 0)
            def _():
                acc_ref[...] = part

            @pl.when(k > 0)
            def _():
                acc_ref[...] += part

            @pl.when(k == nk - 1)
            def _():
                finish(acc_ref[...])

    def spec(shape, index):
        return pl.BlockSpec(shape, (lambda j, i, k: index(i, j, k)) if j_outer else index)

    a_spec = spec((tk, tm), lambda i, j, k: (k, i)) if ta else spec((tm, tk), lambda i, j, k: (i, k))
    b_spec = spec((tn, tk), lambda i, j, k: (j, k)) if tb else spec((tk, tn), lambda i, j, k: (k, j))
    o_spec = spec((tm, tn), lambda i, j, k: (i, j))
    c_specs = [spec(c.shape, lambda i, j, k, nd=c.ndim: (0,) * nd) for c in consts]
    outs = pl.pallas_call(
        body, name=name,
        grid=(n // tn, m // tm, nk) if j_outer else (m // tm, n // tn, nk),
        in_specs=([a_spec, b_spec]
                  + [o_spec if cb is None else spec((tm, EPI_PIECE), lambda i, j, k, cb=cb: (i, cb)) for cb in e_cols] + c_specs),
        out_specs=[o_spec] * n_o + [spec((8, n), lambda i, j, k: (0, 0))] * n_sums,
        out_shape=[jax.ShapeDtypeStruct((m, n), dt) for dt in out_dtypes] + [jax.ShapeDtypeStruct((8, n), F32)] * n_sums,
        scratch_shapes=[pltpu.VMEM((tm, tn), F32)] if nk > 1 else [],
        compiler_params=_cparams(("arbitrary",) * 3 if n_sums else ("parallel", "parallel", "arbitrary")),
    )(a, b, *e_arrs, *consts)
    return outs[0] if n_o + n_sums == 1 else outs


def _rows(arr, tr):
    return (arr, pl.BlockSpec((tr, arr.shape[1]), lambda i: (i, 0)))


def _win(arr, tr, start, width):
    bw = math.gcd(start, width) if start else width
    assert bw % LANES == 0
    return [(arr, pl.BlockSpec((tr, bw), lambda i, cb=start // bw + p: (i, cb))) for p in range(width // bw)]


def _ct(arr):
    return (arr, pl.BlockSpec((None,) + arr.shape[1:], lambda i: (i, 0, 0)))


def _whole(arr):
    return (arr, pl.BlockSpec(arr.shape, lambda i, nd=arr.ndim: (0,) * nd))


def _cat(refs):
    vals = [r[...].astype(F32) for r in refs]
    return vals[0] if len(vals) == 1 else jnp.concatenate(vals, axis=1)


def _seqtiled(name, fn, n_tiles, ins, outs, acc_widths=()):
    n_i, n_o, n_a = len(ins), len(outs), len(acc_widths)

    def body(*refs):
        i_refs, o_refs, a_refs = refs[:n_i], refs[n_i:n_i + n_o], refs[n_i + n_o:]
        if n_a:
            @pl.when(pl.program_id(0) == 0)
            def _():
                for r in a_refs:
                    r[...] = jnp.zeros(r.shape, F32)
        fn(list(i_refs), list(o_refs), list(a_refs))

    res = pl.pallas_call(
        body, name=name, grid=(n_tiles,),
        in_specs=[s for _, s in ins],
        out_specs=[s for _, _, s in outs] + [pl.BlockSpec((8, w), lambda i: (0, 0)) for w in acc_widths],
        out_shape=[jax.ShapeDtypeStruct(sh, dt) for sh, dt, _ in outs] + [jax.ShapeDtypeStruct((8, w), F32) for w in acc_widths],
        compiler_params=_cparams(("arbitrary",)),
    )(*[a for a, _ in ins])
    return res


def _acc_add(acc_ref, val):
    acc_ref[0:1, :] += jnp.sum(val, axis=0, keepdims=True)


def _out_rows(s, w, dt, tr):
    return ((s, w), dt, pl.BlockSpec((tr, w), lambda i: (i, 0)))


def _out_ct(s, w, dt, t):
    return ((s // t, w, t), dt, pl.BlockSpec((None, w, t), lambda i: (i, 0, 0)))


def _rms_fwd(x, gain):
    r = lax.rsqrt(jnp.mean(x * x, axis=-1, keepdims=True) + NORM_EPS)
    return x * r * gain


def _rms_bwd(dy, x, gain):
    r = lax.rsqrt(jnp.mean(x * x, axis=-1, keepdims=True) + NORM_EPS)
    xn = x * r
    dyg = dy * gain
    dx = r * (dyg - xn * jnp.mean(dyg * xn, axis=-1, keepdims=True))
    return dx, dy * xn


def _seg_mean(y, hd):
    w = y.shape[1]
    pieces = []
    for s in range(0, w, LANES):
        v = y[:, s:s + LANES]
        tot = jnp.sum(v, axis=1, keepdims=True)
        if hd == LANES:
            pieces.append(jnp.broadcast_to(tot, v.shape))
        else:
            low = lax.broadcasted_iota(jnp.int32, v.shape, 1) < hd
            lo = jnp.sum(jnp.where(low, v, 0.0), axis=1, keepdims=True)
            pieces.append(jnp.where(low, lo, tot - lo))
    out = pieces[0] if len(pieces) == 1 else jnp.concatenate(pieces, axis=1)
    return out * (1.0 / hd)


def _tile_lanes(t, w):
    return t if w == t.shape[1] else jnp.concatenate([t] * (w // t.shape[1]), axis=1)


def _swap_halves(x, hd):
    w = x.shape[1]
    half = hd // 2
    lane = lax.broadcasted_iota(jnp.int32, x.shape, 1)
    return jnp.where((lane % hd) < half, pltpu.roll(x, w - half, 1), pltpu.roll(x, half, 1))


def _rope(x, cos, sin_signed, hd):
    w = x.shape[1]
    return x * _tile_lanes(cos, w) + _swap_halves(x, hd) * _tile_lanes(sin_signed, w)


def _rope_t(dy, cos, sin_signed, hd):
    w = dy.shape[1]
    return dy * _tile_lanes(cos, w) + _swap_halves(dy * _tile_lanes(sin_signed, w), hd)


def _headnorm_fwd(x, gain_w, hd):
    r = lax.rsqrt(_seg_mean(x * x, hd) + NORM_EPS)
    return x * r * gain_w


def _headnorm_bwd(dy, x, gain_w, hd):
    r = lax.rsqrt(_seg_mean(x * x, hd) + NORM_EPS)
    xn = x * r
    dyg = dy * gain_w
    return r * (dyg - xn * _seg_mean(dyg * xn, hd)), dy * xn


def _sigmoid(x):
    return 1.0 / (1.0 + jnp.exp(-x))


def _rope_tables(seq_len, head_dim):
    rows = seq_len // GRID_W
    n_axis = head_dim // 4
    freqs = ROPE_THETA ** (-jnp.arange(n_axis, dtype=F32) / n_axis)
    ang_r = jnp.arange(rows, dtype=F32)[:, None] * freqs
    ang_c = jnp.arange(GRID_W, dtype=F32)[:, None] * freqs

    def expand(by_row, by_col):
        r = jnp.broadcast_to(by_row[:, None, :], (rows, GRID_W, n_axis))
        c = jnp.broadcast_to(by_col[None, :, :], (rows, GRID_W, n_axis))
        return jnp.concatenate([r, c], axis=-1).reshape(seq_len, 2 * n_axis)

    cos, sin = expand(jnp.cos(ang_r), jnp.cos(ang_c)), expand(jnp.sin(ang_r), jnp.sin(ang_c))
    reps = LANES // head_dim
    return jnp.tile(jnp.concatenate([cos, cos], axis=-1), (1, reps)), jnp.tile(jnp.concatenate([-sin, sin], axis=-1), (1, reps))


def _stage_qkv(proj, tabs, gq_w, gk_w):
    s = proj.shape[0]
    t = _seq_tile(s)
    ca, sa, cr, sr = tabs
    ins = (_win(proj, t, C_AQ, ATTN_Q_W) + _win(proj, t, C_AK, ATTN_KV_W) + _win(proj, t, C_AV, ATTN_KV_W)
           + _win(proj, t, C_RQ, RET_W) + _win(proj, t, C_RK, RET_W)
           + [_rows(ca, t), _rows(sa, t), _rows(cr, t), _rows(sr, t), _whole(gq_w), _whole(gk_w)])

    def fn(i, o, a):
        aq, ak, av = (i[n][...].astype(F32) for n in range(3))
        rq, rk = _cat(i[3:5]), _cat(i[5:7])
        ca_, sa_, cr_, sr_ = i[7][...], i[8][...], i[9][...], i[10][...]
        qr = _rope(_headnorm_fwd(aq, i[11][...], ATTN_HEAD_DIM), ca_, sa_, ATTN_HEAD_DIM) * Q_FOLD
        kr = _rope(_headnorm_fwd(ak, i[12][...], ATTN_HEAD_DIM), ca_, sa_, ATTN_HEAD_DIM)
        qt = qr.T.astype(BF16)
        zeros = jnp.zeros((ATTN_HEAD_DIM, t), BF16)
        for h in range(ATTN_HEADS):
            g = h // ATTN_GROUP
            blk = qt[h * ATTN_HEAD_DIM:(h + 1) * ATTN_HEAD_DIM, :]
            o[0][h * LANES + g * ATTN_HEAD_DIM:h * LANES + (g + 1) * ATTN_HEAD_DIM, :] = blk
            o[0][h * LANES + (1 - g) * ATTN_HEAD_DIM:h * LANES + (2 - g) * ATTN_HEAD_DIM, :] = zeros
        o[1][...] = kr.astype(BF16)
        o[2][...] = kr.T.astype(BF16)
        o[3][...] = av.astype(BF16)
        o[4][...] = av.T.astype(BF16)
        o[5][...] = _rope(rq, cr_, sr_, RET_HEAD_DIM) * RET_SCALE
        o[6][...] = _rope(rk, cr_, sr_, RET_HEAD_DIM)

    outs = [_out_ct(s, ATTN_HEADS * LANES, BF16, t), _out_rows(s, ATTN_KV_W, BF16, t), _out_ct(s, ATTN_KV_W, BF16, t),
            _out_rows(s, ATTN_KV_W, BF16, t), _out_ct(s, ATTN_KV_W, BF16, t), _out_rows(s, RET_W, F32, t), _out_rows(s, RET_W, F32, t)]
    return _seqtiled("qkv_prep", fn, s // t, ins, outs)


def _groupnorm_gate(ry, rg, gain):
    mu = _seg_mean(ry, RET_HEAD_DIM)
    d = ry - mu
    rs = lax.rsqrt(_seg_mean(d * d, RET_HEAD_DIM) + GN_EPS)
    return d * rs, rs, _sigmoid(rg)


def _stage_mix_post(ry_f, ry_b, proj, o_ct, gain):
    s = proj.shape[0]
    t = _seq_tile(s)
    ins = [_rows(ry_f, t), _rows(ry_b, t)] + _win(proj, t, C_RG, RET_W) + [_ct(o_ct), _whole(gain)]

    def fn(i, o, a):
        ry = i[0][...] + i[1][...]
        rg = _cat(i[2:4])
        gn, _, sg = _groupnorm_gate(ry, rg, None)
        o[0][...] = (gn * i[5][...] * (rg * sg)).astype(BF16)
        o[1][...] = i[4][...].astype(F32).T.astype(BF16)

    return _seqtiled("mix_post", fn, s // t, ins, [_out_rows(s, RET_W, BF16, t), _out_rows(s, ATTN_Q_W, BF16, t)])


def _stage_mix_post_bwd(dattn, attn_rows, drz, ry_f, ry_b, proj, gain):
    s = proj.shape[0]
    t = _seq_tile(s)
    ins = ([_rows(dattn, t), _rows(attn_rows, t), _rows(drz, t), _rows(ry_f, t), _rows(ry_b, t)]
           + _win(proj, t, C_RG, RET_W) + [_whole(gain)])

    def fn(i, o, a):
        da = i[0][...]
        dat = da.T
        prod_t = (da * i[1][...].astype(F32)).T
        dat_b = dat.astype(BF16)
        zeros = jnp.zeros((ATTN_HEAD_DIM, t), BF16)
        for h in range(ATTN_HEADS):
            g = h // ATTN_GROUP
            o[0][h * LANES + g * ATTN_HEAD_DIM:h * LANES + (g + 1) * ATTN_HEAD_DIM, :] = dat_b[h * ATTN_HEAD_DIM:(h + 1) * ATTN_HEAD_DIM, :]
            o[0][h * LANES + (1 - g) * ATTN_HEAD_DIM:h * LANES + (2 - g) * ATTN_HEAD_DIM, :] = zeros
            o[1][h] = jnp.sum(prod_t[h * ATTN_HEAD_DIM:(h + 1) * ATTN_HEAD_DIM, :], axis=0, keepdims=True)
        ry = i[3][...] + i[4][...]
        rg = _cat(i[5:7])
        gain_ = i[7][...]
        gn, rs, sg = _groupnorm_gate(ry, rg, None)
        dz = i[2][...]
        silu = rg * sg
        _acc_add(a[0], dz * gn * silu)
        dgn = dz * gain_ * silu
        o[2][...] = rs * (dgn - _seg_mean(dgn, RET_HEAD_DIM) - gn * _seg_mean(dgn * gn, RET_HEAD_DIM))
        o[3][...] = (dz * gn * gain_ * (sg * (1.0 + rg * (1.0 - sg)))).astype(BF16)

    outs = [_out_ct(s, ATTN_HEADS * LANES, BF16, t),
            ((ATTN_HEADS, s // t, 1, t), F32, pl.BlockSpec((ATTN_HEADS, None, 1, t), lambda i: (0, i, 0, 0))),
            _out_rows(s, RET_W, F32, t), _out_rows(s, RET_W, BF16, t)]
    return _seqtiled("mix_post_bwd", fn, s // t, ins, outs, acc_widths=(RET_W,))


def _stage_dproj(proj, dq_ct, dk8, dv8, rgrads, drg, dga, dgr, tabs, gq_w, gk_w):
    s = proj.shape[0]
    t = _seq_tile(s)
    ca, sa, cr, sr = tabs
    kv8 = pl.BlockSpec((ATTN_HEADS, t, ATTN_KV_W), lambda i: (0, i, 0))
    ins = (_win(proj, t, C_AQ, ATTN_Q_W) + _win(proj, t, C_AK, ATTN_KV_W) + [_ct(dq_ct), (dk8, kv8), (dv8, kv8)]
           + [_rows(g, t) for g in rgrads] + [_rows(drg, t), _rows(dga, t), _rows(dgr, t)]
           + [_rows(ca, t), _rows(sa, t), _rows(cr, t), _rows(sr, t), _whole(gq_w), _whole(gk_w)])

    def fn(i, o, a):
        aq, ak = i[0][...].astype(F32), i[1][...].astype(F32)
        dq_f, dk_f, dv_f, dq_b, dk_b, dv_b = (r[...].astype(F32) for r in i[5:11])
        ca_, sa_, cr_, sr_ = i[14][...], i[15][...], i[16][...], i[17][...]
        dqn = _rope_t(i[2][...].T * ATTN_SCALE, ca_, sa_, ATTN_HEAD_DIM)
        daq, gq_rows = _headnorm_bwd(dqn, aq, i[18][...], ATTN_HEAD_DIM)
        dkn = _rope_t(jnp.sum(i[3][...].astype(F32), axis=0) * (1.0 / LOG2E), ca_, sa_, ATTN_HEAD_DIM)
        dak, gk_rows = _headnorm_bwd(dkn, ak, i[19][...], ATTN_HEAD_DIM)
        _acc_add(a[0], gq_rows)
        _acc_add(a[1], gk_rows)
        out = o[0]
        out[:, C_AQ:C_AQ + ATTN_Q_W] = daq.astype(BF16)
        out[:, C_AK:C_AK + ATTN_KV_W] = dak.astype(BF16)
        out[:, C_AV:C_AV + ATTN_KV_W] = jnp.sum(i[4][...].astype(F32), axis=0).astype(BF16)
        out[:, C_RQ:C_RQ + RET_W] = _rope_t((dq_f + dq_b) * RET_SCALE, cr_, sr_, RET_HEAD_DIM).astype(BF16)
        out[:, C_RK:C_RK + RET_W] = _rope_t(dk_f + dk_b, cr_, sr_, RET_HEAD_DIM).astype(BF16)
        out[:, C_RV:C_RV + RET_W] = (dv_f + dv_b).astype(BF16)
        out[:, C_RG:C_RG + RET_W] = i[11][...]
        out[:, C_GA:C_GA + D_MODEL] = i[12][...]
        out[:, C_GR:C_GR + D_MODEL] = i[13][...]

    return _seqtiled("dproj", fn, s // t, ins, [_out_rows(s, IN_W, BF16, t)], acc_widths=(ATTN_Q_W, ATTN_KV_W))


def _attn_fwd(q_ct, k_rows, v_ct):
    nq, _, t = q_ct.shape
    s = nq * t
    nk = nq
    assert nk % 2 == 0
    n_ch = next(n for n in (8, 4, 2) if nq % n == 0)
    halves = 2 if t % (2 * LANES) == 0 else 1
    tq = t // halves
    n_par = n_ch * halves

    def body(q_ref, k_ref, v_ref, o_ref, lse_ref, *bufs):
        sbuf = tuple(bufs[2 * w:2 * w + 2] for w in range(n_par))
        pbuf = tuple(bufs[2 * n_par + 2 * w:2 * n_par + 2 * w + 2] for w in range(n_par))

        def where(w):
            return w // halves, slice((w % halves) * tq, (w % halves + 1) * tq)

        def scores(w, j, slot):
            kj = k_ref[pl.ds(pl.multiple_of(j * t, t), t), :]
            cw, lanes = where(w)
            st = jnp.dot(kj, q_ref[cw, :, lanes], preferred_element_type=F32)
            sbuf[w][slot][...] = st
            return jnp.max(st, axis=0, keepdims=True)

        def probs(w, slot, cmax, m, l):
            m_new = jnp.maximum(m, cmax)
            alpha = jnp.exp2(m - m_new)
            pt = jnp.exp2(sbuf[w][slot][...] - m_new)
            pbuf[w][slot][...] = pt.astype(BF16)
            return m_new, alpha * l + jnp.sum(pt, axis=0, keepdims=True), alpha

        def values(w, j, slot, alpha, acc):
            return alpha * acc + jnp.dot(v_ref[j], pbuf[w][slot][...], preferred_element_type=F32)

        init = []
        for w in range(n_par):
            m = jnp.full((1, tq), -1e30, F32)
            l = jnp.zeros((1, tq), F32)
            cmax0 = scores(w, 0, 0)
            cmax1 = scores(w, 1, 1)
            m, l, alpha0 = probs(w, 0, cmax0, m, l)
            init.append((m, l, jnp.zeros((ATTN_HEAD_DIM, tq), F32), cmax1, alpha0))

        def trip(n, carry):
            c = 2 * n
            out = []
            for w in range(n_par):
                m, l, acc, cmax_b, alpha_c = carry[w]
                acc = values(w, c, 0, alpha_c, acc)
                m, l, alpha1 = probs(w, 1, cmax_b, m, l)
                cmax2 = scores(w, c + 2, 0)
                acc = values(w, c + 1, 1, alpha1, acc)
                m, l, alpha2 = probs(w, 0, cmax2, m, l)
                cmax3 = scores(w, c + 3, 1)
                out.append((m, l, acc, cmax3, alpha2))
            return tuple(out)

        res = lax.fori_loop(0, nk // 2 - 1, trip, tuple(init))
        for w in range(n_par):
            m, l, acc, cmax_b, alpha_c = res[w]
            acc = values(w, nk - 2, 0, alpha_c, acc)
            m, l, alpha1 = probs(w, 1, cmax_b, m, l)
            acc = values(w, nk - 1, 1, alpha1, acc)
            cw, lanes = where(w)
            o_ref[cw, :, lanes] = (acc / l).astype(BF16)
            lse_ref[cw, :, lanes] = m + jnp.log2(l)

    return pl.pallas_call(
        body, name="attn_fwd", grid=(ATTN_HEADS, nq // n_ch),
        in_specs=[pl.BlockSpec((n_ch, LANES, t), lambda h, i: (i, h, 0)),
                  pl.BlockSpec((s, ATTN_KV_W), lambda h, i: (0, 0)),
                  pl.BlockSpec((nk, ATTN_HEAD_DIM, t), lambda h, i: (0, h // ATTN_GROUP, 0))],
        out_specs=[pl.BlockSpec((n_ch, ATTN_HEAD_DIM, t), lambda h, i: (i, h, 0)),
                   pl.BlockSpec((None, n_ch, 1, t), lambda h, i: (h, i, 0, 0))],
        out_shape=[jax.ShapeDtypeStruct((nq, ATTN_Q_W, t), BF16), jax.ShapeDtypeStruct((ATTN_HEADS, nq, 1, t), F32)],
        scratch_shapes=[pltpu.VMEM((t, tq), F32)] * (2 * n_par) + [pltpu.VMEM((t, tq), BF16)] * (2 * n_par),
        compiler_params=_cparams(("parallel", "parallel")),
    )(q_ct, k_rows, v_ct)


def _attn_bwd(q_ct, do_ct, lse, delta, k_rows, v_rows, k_ct):
    nq, _, t = q_ct.shape
    s = nq * t
    kc = 4 if nq % 4 == 0 else 2
    tk = kc * t
    nk = nq // kc
    assert nq % 2 == 0 and nq % kc == 0

    def body(q_ref, do_ref, lse_ref, delta_ref, k_ref, v_ref, kt_ref, dq_ref, dk_ref, dv_ref, dk_acc, dv_acc,
             sb0, sb1, db0, db1, pb0, pb1, gb0, gb1):
        j = pl.program_id(1)
        sb, db, pb, gb = (sb0, sb1), (db0, db1), (pb0, pb1), (gb0, gb1)

        @pl.when(j == 0)
        def _():
            dq_ref[...] = jnp.zeros(dq_ref.shape, F32)

        kj, vj = k_ref[...], v_ref[...]
        ktj = jnp.concatenate([kt_ref[u] for u in range(kc)], axis=1)
        dk_acc[...] = jnp.zeros(dk_acc.shape, F32)
        dv_acc[...] = jnp.zeros(dv_acc.shape, F32)

        def products(i, slot):
            sb[slot][...] = jnp.dot(kj, q_ref[i], preferred_element_type=F32)
            db[slot][...] = jnp.dot(vj, do_ref[i], preferred_element_type=F32)

        def cotangents(i, slot):
            pt = jnp.exp2(sb[slot][...] - lse_ref[i])
            pb[slot][...] = pt.astype(BF16)
            gb[slot][...] = (pt * (db[slot][...] - delta_ref[i])).astype(BF16)

        def accumulate(i, slot):
            dst = gb[slot][...]
            dv_acc[...] += _nt(pb[slot][...], do_ref[i])
            dk_acc[...] += _nt(dst, q_ref[i])
            dq_ref[i] += jnp.dot(ktj, dst, preferred_element_type=F32)

        products(0, 0)
        products(1, 1)
        cotangents(0, 0)

        def trip(n, carry):
            c = 2 * n
            accumulate(c, 0)
            cotangents(c + 1, 1)
            products(c + 2, 0)
            accumulate(c + 1, 1)
            cotangents(c + 2, 0)
            products(c + 3, 1)
            return carry

        lax.fori_loop(0, nq // 2 - 1, trip, 0)
        accumulate(nq - 2, 0)
        cotangents(nq - 1, 1)
        accumulate(nq - 1, 1)
        dk_ref[...] = dk_acc[...].astype(dk_ref.dtype)
        dv_ref[...] = dv_acc[...].astype(dv_ref.dtype)

    per_head = pl.BlockSpec((nq, LANES, t), lambda h, j: (0, h, 0))
    stat = pl.BlockSpec((None, nq, 1, t), lambda h, j: (h, 0, 0, 0))
    kv_rows = pl.BlockSpec((tk, ATTN_KV_W), lambda h, j: (j, 0))
    kv_out = pl.BlockSpec((None, tk, ATTN_KV_W), lambda h, j: (h, j, 0))
    return pl.pallas_call(
        body, name="attn_bwd", grid=(ATTN_HEADS, nk),
        in_specs=[per_head, per_head, stat, stat, kv_rows, kv_rows,
                  pl.BlockSpec((kc, ATTN_HEAD_DIM, t), lambda h, j: (j, h // ATTN_GROUP, 0))],
        out_specs=[pl.BlockSpec((nq, ATTN_HEAD_DIM, t), lambda h, j: (0, h, 0)), kv_out, kv_out],
        out_shape=[jax.ShapeDtypeStruct((nq, ATTN_Q_W, t), F32), jax.ShapeDtypeStruct((ATTN_HEADS, s, ATTN_KV_W), BF16),
                   jax.ShapeDtypeStruct((ATTN_HEADS, s, ATTN_KV_W), BF16)],
        scratch_shapes=([pltpu.VMEM((tk, ATTN_KV_W), F32)] * 2 + [pltpu.VMEM((tk, t), F32)] * 4 + [pltpu.VMEM((tk, t), BF16)] * 4),
        compiler_params=_cparams(("parallel", "arbitrary")),
    )(q_ct, do_ct, lse, delta, k_rows, v_rows, k_ct)


def _log_sigmoid(x):
    t = jnp.exp(-jnp.abs(x))
    log1p_t = jnp.where(t < 1e-2, t * (1.0 - t * (0.5 - t * (1.0 / 3.0))), jnp.log(1.0 + t))
    return jnp.minimum(x, 0.0) - log1p_t


def _decay_tables(logit, backward):
    c, hd = RET_CHUNK, RET_HEAD_DIM

    def lam(shape):
        return _log_sigmoid(jnp.full(shape, logit, F32))

    ii = lax.broadcasted_iota(jnp.int32, (c, c), 0).astype(F32)
    jj = lax.broadcasted_iota(jnp.int32, (c, c), 1).astype(F32)
    pos = lax.broadcasted_iota(jnp.int32, (c, hd), 0).astype(F32)
    if not backward:
        dist, dist_t = jnp.maximum(ii - jj, 0.0), jnp.maximum(jj - ii, 0.0)
        mask, mask_t = ii >= jj, jj >= ii
        e_q, e_k = pos + 1.0, (c - 1.0) - pos
    else:
        dist, dist_t = jnp.maximum(jj - ii, 0.0), jnp.maximum(ii - jj, 0.0)
        mask, mask_t = jj > ii, ii > jj
        e_q, e_k = c - pos, pos
    lam_cc, lam_row = lam((c, c)), lam((c, hd))
    return dict(
        d=jnp.where(mask, jnp.exp(lam_cc * dist), 0.0), d_t=jnp.where(mask_t, jnp.exp(lam_cc * dist_t), 0.0), dist=dist,
        qdec=jnp.exp(lam_row * e_q), kdec=jnp.exp(lam_row * e_k), e_q=e_q, e_k=e_k, gam=jnp.exp(lam((hd, hd)) * c))


def _nt(a, b):
    return lax.dot_general(a, b, (((1,), (1,)), ((), ())), preferred_element_type=F32)


def _ret_sub(n_chunks):
    return max(1, min(n_chunks, 512 // RET_CHUNK))


def _ret_fwd(logits, q, k, proj):
    s = q.shape[0]
    c = RET_CHUNK
    sub = _ret_sub(s // c)
    nb = s // (c * sub)
    block = (lambda n: n, lambda n: nb - 1 - n)
    order = (tuple(range(sub)), tuple(reversed(range(sub))))
    vwin = _win(proj, c * sub, C_RV, RET_W)
    nv = len(vwin)
    vw = RET_W // nv
    per = 2 + nv

    def body(lg_ref, *refs):
        ins, outs, states = refs[:2 * per], refs[2 * per:2 * per + 4], refs[2 * per + 4:]

        @pl.when(pl.program_id(0) == 0)
        def _():
            for st in states:
                st[...] = jnp.zeros(st.shape, F32)

        for h in range(RET_HEADS):
            for d in range(2):
                q_ref, k_ref, v_refs = ins[d * per], ins[d * per + 1], ins[d * per + 2:(d + 1) * per]
                y_ref, st_ref, state = outs[2 * d], outs[2 * d + 1], states[d]
                tb = _decay_tables(lg_ref[d, h], bool(d))
                sl = slice(h * RET_HEAD_DIM, (h + 1) * RET_HEAD_DIM)
                off = h * RET_HEAD_DIM
                sh = state[h]
                for u in order[d]:
                    rows = slice(u * c, (u + 1) * c)
                    qh, kh = q_ref[rows, sl], k_ref[rows, sl]
                    vb = v_refs[off // vw][rows, off % vw:off % vw + RET_HEAD_DIM].astype(BF16)
                    a = _nt(qh.astype(BF16), kh.astype(BF16)) * tb["d"]
                    st_ref[u, h] = sh
                    y_ref[rows, sl] = (jnp.dot(a.astype(BF16), vb, preferred_element_type=F32)
                                       + jnp.dot((qh * tb["qdec"]).astype(BF16), sh.astype(BF16), preferred_element_type=F32))
                    sh = tb["gam"] * sh + jnp.dot((kh * tb["kdec"]).T.astype(BF16), vb, preferred_element_type=F32)
                state[h] = sh

    hmat = (RET_HEADS, RET_HEAD_DIM, RET_HEAD_DIM)
    in_specs, out_specs, args = [pl.BlockSpec(memory_space=pltpu.SMEM)], [], [logits]
    for d in range(2):
        rows = pl.BlockSpec((c * sub, RET_W), lambda n, d=d: (block[d](n), 0))
        in_specs += [rows, rows] + [pl.BlockSpec(sp.block_shape, lambda n, d=d, cb=sp.index_map(0)[1]: (block[d](n), cb)) for _, sp in vwin]
        args += [q, k] + [a for a, _ in vwin]
        out_specs += [rows, pl.BlockSpec((sub,) + hmat, lambda n, d=d: (block[d](n), 0, 0, 0))]
    return pl.pallas_call(
        body, name="ret_fwd", grid=(nb,), in_specs=in_specs, out_specs=out_specs,
        out_shape=[jax.ShapeDtypeStruct((s, RET_W), F32), jax.ShapeDtypeStruct((nb * sub,) + hmat, F32)] * 2,
        scratch_shapes=[pltpu.VMEM(hmat, F32)] * 2,
        compiler_params=_cparams(("arbitrary",)),
    )(*args)


def _ret_bwd(logits, q, k, proj, dy, st_f, st_b):
    s = q.shape[0]
    c = RET_CHUNK
    sub = _ret_sub(s // c)
    nb = s // (c * sub)
    block = (lambda n: nb - 1 - n, lambda n: n)
    order = (tuple(reversed(range(sub))), tuple(range(sub)))
    vwin = _win(proj, c * sub, C_RV, RET_W)
    nv = len(vwin)
    vw = RET_W // nv
    per = 4 + nv

    def body(lg_ref, *refs):
        ins, outs, scr = refs[:2 * per], refs[2 * per:2 * per + 8], refs[2 * per + 8:]
        n = pl.program_id(0)

        @pl.when(n == 0)
        def _():
            for r in scr:
                r[...] = jnp.zeros(r.shape, F32)

        for h in range(RET_HEADS):
            for d in range(2):
                q_ref, k_ref, dy_ref, st_ref = ins[d * per:d * per + 4]
                v_refs = ins[d * per + 4:(d + 1) * per]
                dq_ref, dk_ref, dv_ref = outs[4 * d:4 * d + 3]
                dstate, lacc = scr[2 * d], scr[2 * d + 1]
                tb = _decay_tables(lg_ref[d, h], bool(d))
                sl = slice(h * RET_HEAD_DIM, (h + 1) * RET_HEAD_DIM)
                off = h * RET_HEAD_DIM
                dsh = dstate[h]
                lsum = lacc[h, 0:1, :]
                for u in order[d]:
                    rows = slice(u * c, (u + 1) * c)
                    qh, kh, dyh = q_ref[rows, sl], k_ref[rows, sl], dy_ref[rows, sl]
                    vb = v_refs[off // vw][rows, off % vw:off % vw + RET_HEAD_DIM].astype(BF16)
                    qb, kb, dyb = qh.astype(BF16), kh.astype(BF16), dyh.astype(BF16)
                    sh = st_ref[u, h]
                    shb, dshb = sh.astype(BF16), dsh.astype(BF16)
                    qk = _nt(qb, kb)
                    g = _nt(dyb, vb) * tb["d"]
                    a_t = _nt(kb, qb) * tb["d_t"]
                    g_t = _nt(vb, dyb) * tb["d_t"]
                    qd, kd = qh * tb["qdec"], kh * tb["kdec"]
                    dqd = _nt(dyb, shb)
                    dkd = _nt(vb, dshb)
                    dq_ref[rows, sl] = (jnp.dot(g.astype(BF16), kb, preferred_element_type=F32) + dqd * tb["qdec"]).astype(dq_ref.dtype)
                    dk_ref[rows, sl] = (jnp.dot(g_t.astype(BF16), qb, preferred_element_type=F32) + dkd * tb["kdec"]).astype(dk_ref.dtype)
                    dv_ref[rows, sl] = (jnp.dot(a_t.astype(BF16), dyb, preferred_element_type=F32)
                                        + jnp.dot(kd.astype(BF16), dshb, preferred_element_type=F32)).astype(dv_ref.dtype)
                    intra = jnp.sum(tb["dist"] * qk * g, axis=0, keepdims=True)
                    lsum = (lsum + sum(intra[:, o:o + LANES] for o in range(0, c, LANES))
                            + jnp.sum(tb["e_q"] * qd * dqd + tb["e_k"] * kd * dkd, axis=0, keepdims=True)
                            + jnp.sum(float(c) * tb["gam"] * dsh * sh, axis=0, keepdims=True))
                    dsh = tb["gam"] * dsh + jnp.dot(qd.T.astype(BF16), dyb, preferred_element_type=F32)
                dstate[h] = dsh
                lacc[h, 0:1, :] = lsum

        @pl.when(n == nb - 1)
        def _():
            for d in range(2):
                for h in range(RET_HEADS):
                    outs[4 * d + 3][h] = jnp.zeros((8, LANES), F32) + jnp.sum(scr[2 * d + 1][h])

    hmat = (RET_HEADS, RET_HEAD_DIM, RET_HEAD_DIM)
    in_specs, out_specs, args = [pl.BlockSpec(memory_space=pltpu.SMEM)], [], [logits]
    for d, states in enumerate((st_f, st_b)):
        rows = pl.BlockSpec((c * sub, RET_W), lambda n, d=d: (block[d](n), 0))
        in_specs += ([rows, rows, rows, pl.BlockSpec((sub,) + hmat, lambda n, d=d: (block[d](n), 0, 0, 0))]
                     + [pl.BlockSpec(sp.block_shape, lambda n, d=d, cb=sp.index_map(0)[1]: (block[d](n), cb)) for _, sp in vwin])
        args += [q, k, dy, states] + [a for a, _ in vwin]
        out_specs += [rows, rows, rows, pl.BlockSpec((RET_HEADS, 8, LANES), lambda n: (0, 0, 0))]
    return pl.pallas_call(
        body, name="ret_bwd", grid=(nb,), in_specs=in_specs, out_specs=out_specs,
        out_shape=([jax.ShapeDtypeStruct((s, RET_W), BF16)] * 3 + [jax.ShapeDtypeStruct((RET_HEADS, 8, LANES), F32)]) * 2,
        scratch_shapes=[pltpu.VMEM(hmat, F32), pltpu.VMEM((RET_HEADS, 8, LANES), F32)] * 2,
        compiler_params=_cparams(("arbitrary",)),
    )(*args)


def _local_step(x, p, target, w, small, hb):
    s = x.shape[0]
    tabs = _rope_tables(s, ATTN_HEAD_DIM) + _rope_tables(s, RET_HEAD_DIM)
    g_mix, g_mlp, g_ple = small["mix_norm"][None, :], small["mlp_norm"][None, :], small["ple_norm"][None, :]
    g_final, g_ret = small["final_norm"][None, :], small["ret_norm_gain"][None, :]
    gq_w = jnp.tile(small["attn_q_norm"], ATTN_HEADS)[None, :]
    gk_w = jnp.tile(small["attn_k_norm"], ATTN_KV_HEADS)[None, :]
    logits = small["ret_decay_logit"]

    proj = _mm("in_proj", hb, w["w_in"], tm=1024, tn=IN_W // 2, tk=1024, out_dtypes=(BF16,), j_outer=True)
    q_ct, k_rows, k_ct, v_rows, v_ct, rq, rk = _stage_qkv(proj, tabs, gq_w, gk_w)
    o_ct, lse = _attn_fwd(q_ct, k_rows, v_ct)
    ry_f, st_f, ry_b, st_b = _ret_fwd(logits, rq, rk, proj)
    rz, attn_rows = _stage_mix_post(ry_f, ry_b, proj, o_ct, g_ret)
    a_out = _mm("attn_o", attn_rows, w["w_attn_o"], tm=1024, tn=1024, tk=512, out_dtypes=(BF16,))
    n_gate = D_MODEL // EPI_PIECE

    def epi_merge(acc, e, c):
        ga, gr = _cat(e[1:1 + n_gate]), _cat(e[1 + n_gate:1 + 2 * n_gate])
        return acc, _sigmoid(ga) * e[0][...] + _sigmoid(gr) * acc

    r_out, merged = _mm("ret_o", rz, w["w_ret_o"], tm=1024, tn=1024, tk=512, out_dtypes=(BF16, BF16), epi=epi_merge,
                        epi_ins=(a_out, (proj, C_GA), (proj, C_GR)))

    def epi_res_norm(acc, e, c):
        xr = e[0][...] + acc
        return xr, _rms_fwd(xr, c[0][...])

    x1, hm = _mm("out_proj", merged, w["w_out"], tm=1024, tn=1024, tk=1024, out_dtypes=(F32, BF16),
                 epi=epi_res_norm, epi_ins=(x,), consts=(g_mlp,))

    def epi_relu2(acc, e, c):
        r = jnp.maximum(acc, 0.0)
        return (r * r,)

    act = _mm("mlp_up", hm, w["w_up"], tm=1024, tn=2048, tk=1024, out_dtypes=(BF16,), epi=epi_relu2, j_outer=True)
    x2, hp = _mm("mlp_down", act, w["w_down"], tm=512, tn=1024, tk=D_FF, out_dtypes=(F32, BF16),
                 epi=epi_res_norm, epi_ins=(x1,), consts=(g_ple,))
    pe = _mm("ple_emb", p, w["w_ple"], tm=1024, tn=1024, tk=256)

    def epi_head(acc, e, c):
        gt = _sigmoid(acc)
        pe_, gf = e[0][...], c[0][...]
        x3 = e[1][...] + gt * pe_
        r3 = lax.rsqrt(jnp.mean(x3 * x3, axis=-1, keepdims=True) + NORM_EPS)
        x3n = x3 * r3
        err = x3n * gf - e[2][...]
        dy = err * (1.0 / D_MODEL)
        dyg = dy * gf
        dx3 = r3 * (dyg - x3n * jnp.mean(dyg * x3n, axis=-1, keepdims=True))
        return dx3, dx3 * pe_ * gt * (1.0 - gt), dx3 * gt, err * err, dy * x3n

    dx3, dzg, dpe, loss_cols, g_final_p = _mm("ple_gate", hp, w["w_ple_gate"], tm=1024, tn=1024, tk=1024, out_dtypes=(F32, BF16, BF16),
                                              epi=epi_head, epi_ins=(pe, x2, target), consts=(g_final,), n_sums=2)
    loss_sum = 0.5 / D_MODEL * jnp.sum(loss_cols)

    gw = {}
    gw["w_ple"] = _mm("g_w_ple", p, dpe, ta=True, tm=256, tn=1024, tk=2048)
    gw["w_ple_gate"] = _mm("g_w_ple_gate", hp, dzg, ta=True, tm=1024, tn=1024, tk=2048)
    def epi_norm_bwd(acc, e, c):
        dx, dg = _rms_bwd(acc, e[0][...], c[0][...])
        return e[1][...] + dx, dg

    def epi_norm_bwd_b(acc, e, c):
        tot, dg = epi_norm_bwd(acc, e, c)
        return tot, tot, dg

    dx2, dx2_b, g_ple_p = _mm("d_hp", dzg, w["w_ple_gate"], tb=True, tm=1024, tn=1024, tk=1024, out_dtypes=(F32, BF16),
                              epi=epi_norm_bwd_b, epi_ins=(x2, dx3), consts=(g_ple,), n_sums=1)

    def epi_relu2_bwd(acc, e, c):
        return (acc * (2.0 * jnp.sqrt(e[0][...]).astype(F32)),)

    du = _mm("d_u", dx2_b, w["w_down"], tb=True, tm=1024, tn=2048, tk=1024, out_dtypes=(BF16,), epi=epi_relu2_bwd, epi_ins=(act,),
             j_outer=True)
    gw["w_down"] = _mm("g_w_down", act, dx2_b, ta=True, tm=1024, tn=1024, tk=4096)
    gw["w_up"] = _mm("g_w_up", hm, du, ta=True, tm=1024, tn=1024, tk=4096)
    dx1, dx1_b, g_mlp_p = _mm("d_hm", du, w["w_up"], tb=True, tm=512, tn=1024, tk=D_FF, out_dtypes=(F32, BF16),
                              epi=epi_norm_bwd_b, epi_ins=(x1, dx2), consts=(g_mlp,), n_sums=1)
    def epi_merge_bwd(acc, e, c):
        sa, sr = _sigmoid(_cat(e[2:2 + n_gate])), _sigmoid(_cat(e[2 + n_gate:2 + 2 * n_gate]))
        return acc * sa, acc * sr, acc * e[0][...] * sa * (1.0 - sa), acc * e[1][...] * sr * (1.0 - sr)

    dao, dro, dga, dgr = _mm("d_merged", dx1_b, w["w_out"], tb=True, tm=1024, tn=1024, tk=1024, out_dtypes=(BF16,) * 4,
                             epi=epi_merge_bwd, epi_ins=(a_out, r_out, (proj, C_GA), (proj, C_GR)))
    gw["w_out"] = _mm("g_w_out", merged, dx1_b, ta=True, tm=1024, tn=1024, tk=2048)
    gw["w_attn_o"] = _mm("g_w_attn_o", attn_rows, dao, ta=True, tm=512, tn=1024, tk=2048)
    gw["w_ret_o"] = _mm("g_w_ret_o", rz, dro, ta=True, tm=512, tn=1024, tk=2048)
    dattn = _mm("d_attn", dao, w["w_attn_o"], tb=True, tm=1024, tn=512, tk=1024)
    drz = _mm("d_rz", dro, w["w_ret_o"], tb=True, tm=1024, tn=512, tk=1024)
    do_ct, delta, dry, drg, g_ret_p = _stage_mix_post_bwd(dattn, attn_rows, drz, ry_f, ry_b, proj, g_ret)
    dq_f, dk_f, dv_f, dl_f, dq_b, dk_b, dv_b, dl_b = _ret_bwd(logits, rq, rk, proj, dry, st_f, st_b)
    dq_ct, dk8, dv8 = _attn_bwd(q_ct, do_ct, lse, delta, k_rows, v_rows, k_ct)
    dproj, gq_p, gk_p = _stage_dproj(proj, dq_ct, dk8, dv8, (dq_f, dk_f, dv_f, dq_b, dk_b, dv_b), drg, dga, dgr, tabs, gq_w, gk_w)
    gw["w_in"] = _mm("g_w_in", hb, dproj, ta=True, tm=512, tn=IN_W // 2, tk=2048)
    grad_x, g_mix_p = _mm("d_h", dproj, w["w_in"], tb=True, tm=512, tn=1024, tk=IN_W, epi=epi_norm_bwd, epi_ins=(x, dx1),
                          consts=(g_mix,), n_sums=1)

    gs = {
        "mix_norm": g_mix_p[0], "mlp_norm": g_mlp_p[0], "ple_norm": g_ple_p[0], "final_norm": g_final_p[0],
        "ret_norm_gain": g_ret_p[0],
        "attn_q_norm": jnp.sum(gq_p[0].reshape(ATTN_HEADS, ATTN_HEAD_DIM), axis=0),
        "attn_k_norm": jnp.sum(gk_p[0].reshape(ATTN_KV_HEADS, ATTN_HEAD_DIM), axis=0),
        "ret_decay_logit": jnp.stack([dl_f[:, 0, 0], dl_b[:, 0, 0]]),
    }
    return loss_sum, grad_x, gw, gs


PACK_COLS = 1024
N_CHIPS = 4
HALF_ROWS = 2048


def _pack_shard(parts):
    return jnp.concatenate([parts[n].reshape(-1, PACK_COLS) for n, _ in BIG], axis=0)


def _unpack_shard(slab, shapes):
    out, r = {}, 0
    for n, _ in BIG:
        rows = math.prod(shapes[n]) // PACK_COLS
        out[n] = slab[r:r + rows].reshape(shapes[n])
        r += rows
    return out


def _shard_of(full, axis, sidx):
    size = full.shape[axis] // N_CHIPS
    return lax.slice_in_dim(full, sidx * size, (sidx + 1) * size, axis=axis)


def _position():
    x, y, c = lax.axis_index("x"), lax.axis_index("y"), lax.axis_index("c")
    return x, y, c


def _other_chips(x, y):
    return [(1 - x, y), (x, 1 - y), (1 - x, 1 - y)]


ANY = pl.BlockSpec(memory_space=pl.ANY)


def _gather_weights(slab, xs, gain):
    rows = slab.shape[0]
    half = rows // 2
    s = xs.shape[0]
    tr = min(SEQ_TILE, s)

    def body(in_ref, x_ref, g_ref, out_ref, hb_ref, send_sems, recv_sems, xbuf, hbuf):
        x, y, c = _position()
        chips = _other_chips(x, y)

        def piece(chip, core):
            return out_ref.at[2 * chip[0] + chip[1], pl.ds(core * half, half), :]

        def copy(k, chip, core, to, src=None):
            return pltpu.make_async_remote_copy(
                src_ref=piece(chip, core) if src is None else src, dst_ref=piece(chip, core),
                send_sem=send_sems.at[k], recv_sem=recv_sems.at[k], device_id=to, device_id_type=MESH)

        first = [copy(j, (x, y), c, (*chip, c), src=in_ref.at[pl.ds(c * half, half), :]) for j, chip in enumerate(chips)]
        for cp in first:
            cp.start()

        def norm_tile(i, carry):
            r = pl.ds(pl.multiple_of(i * tr, tr), tr)
            pltpu.sync_copy(x_ref.at[r], xbuf)
            hbuf[...] = _rms_fwd(xbuf[...], g_ref[...]).astype(BF16)
            pltpu.sync_copy(hbuf, hb_ref.at[r])
            return carry

        lax.fori_loop(0, s // tr, norm_tile, 0)
        passed = [copy(3 + j, chip, c, (x, y, 1 - c)) for j, chip in enumerate(chips)]
        for j, chip in enumerate(chips):
            copy(j, chip, c, (x, y, c)).wait_recv()
            passed[j].start()
        for j, chip in enumerate(chips):
            copy(3 + j, chip, 1 - c, (x, y, c)).wait_recv()
        for cp in first + passed:
            cp.wait_send()

    return pl.pallas_call(
        body, name="gather_weights", in_specs=[ANY, ANY, pl.BlockSpec(memory_space=pltpu.VMEM)], out_specs=[ANY, ANY],
        out_shape=[jax.ShapeDtypeStruct((N_CHIPS,) + slab.shape, slab.dtype), jax.ShapeDtypeStruct(xs.shape, BF16)],
        scratch_shapes=[pltpu.SemaphoreType.DMA((6,)), pltpu.SemaphoreType.DMA((6,)),
                        pltpu.VMEM((tr, xs.shape[1]), F32), pltpu.VMEM((tr, xs.shape[1]), BF16)],
    )(slab, xs, gain)


def _exchange_halves(g):
    def body(g_ref, out_ref, send_sem, recv_sem):
        x, y, c = _position()
        cp = pltpu.make_async_remote_copy(src_ref=g_ref.at[pl.ds(0, N_CHIPS), 1 - c], dst_ref=out_ref, send_sem=send_sem,
                                          recv_sem=recv_sem, device_id=(x, y, 1 - c), device_id_type=MESH)
        cp.start()
        cp.wait()

    return pl.pallas_call(
        body, name="exchange_halves", in_specs=[ANY], out_specs=ANY,
        out_shape=jax.ShapeDtypeStruct((N_CHIPS,) + g.shape[2:], g.dtype),
        scratch_shapes=[pltpu.SemaphoreType.DMA, pltpu.SemaphoreType.DMA],
    )(g)


def _add_my_half(g, r1, c_idx):
    tr = 256
    nt = g.shape[2] // tr
    out = (N_CHIPS,) + g.shape[2:]

    def body(c_ref, g_ref, r_ref, o_ref, ob_ref):
        tot = g_ref[...] + r_ref[...]
        o_ref[...] = tot
        ob_ref[...] = tot.astype(BF16)

    blk = (None, tr, PACK_COLS)
    spec = pl.BlockSpec(blk, lambda s, i, c_ref: (s, i, 0))
    return pl.pallas_call(
        body, name="add_my_half",
        grid_spec=pltpu.PrefetchScalarGridSpec(
            num_scalar_prefetch=1, grid=(N_CHIPS, nt),
            in_specs=[pl.BlockSpec((None,) + blk, lambda s, i, c_ref: (s, c_ref[0], i, 0)), spec],
            out_specs=[spec, spec]),
        out_shape=[jax.ShapeDtypeStruct(out, F32), jax.ShapeDtypeStruct(out, BF16)],
        compiler_params=_cparams(("parallel", "parallel")),
    )(c_idx, g, r1)


def _scatter_to_chips(part):
    def body(p_ref, out_ref, send_sems, recv_sems):
        x, y, c = _position()
        chips = _other_chips(x, y)
        sends = [pltpu.make_async_remote_copy(
            src_ref=p_ref.at[2 * chip[0] + chip[1]], dst_ref=out_ref.at[j], send_sem=send_sems.at[j], recv_sem=recv_sems.at[j],
            device_id=(*chip, c), device_id_type=MESH) for j, chip in enumerate(chips)]
        for cp in sends:
            cp.start()
        for cp in sends:
            cp.wait()

    return pl.pallas_call(
        body, name="scatter_to_chips", in_specs=[ANY], out_specs=ANY,
        out_shape=jax.ShapeDtypeStruct((N_CHIPS - 1,) + part.shape[1:], part.dtype),
        scratch_shapes=[pltpu.SemaphoreType.DMA((3,)), pltpu.SemaphoreType.DMA((3,))],
    )(part)


def _sum_chips(part, r2, chip_idx):
    tr = 256

    def body(c_ref, p_ref, r_ref, o_ref):
        o_ref[...] = ((p_ref[...] + r_ref[0]) + r_ref[1]) + r_ref[2]

    return pl.pallas_call(
        body, name="sum_chips",
        grid_spec=pltpu.PrefetchScalarGridSpec(
            num_scalar_prefetch=1, grid=(r2.shape[1] // tr,),
            in_specs=[pl.BlockSpec((None, tr, PACK_COLS), lambda i, c_ref: (c_ref[0], i, 0)),
                      pl.BlockSpec((N_CHIPS - 1, tr, PACK_COLS), lambda i, c_ref: (0, i, 0))],
            out_specs=pl.BlockSpec((tr, PACK_COLS), lambda i, c_ref: (i, 0))),
        out_shape=jax.ShapeDtypeStruct(r2.shape[1:], F32),
        compiler_params=_cparams(("parallel",)),
    )(chip_idx, part, r2)


def _join_halves(red):
    def body(r_ref, out_ref, send_sem, recv_sem):
        x, y, c = _position()
        cp = pltpu.make_async_remote_copy(src_ref=r_ref, dst_ref=out_ref, send_sem=send_sem, recv_sem=recv_sem,
                                          device_id=(x, y, 1 - c), device_id_type=MESH)
        cp.start()
        cp.wait()

    return pl.pallas_call(
        body, name="join_halves", in_specs=[ANY], out_specs=ANY,
        out_shape=jax.ShapeDtypeStruct(red.shape, red.dtype),
        scratch_shapes=[pltpu.SemaphoreType.DMA, pltpu.SemaphoreType.DMA],
    )(red)


def _adamw_math(w, g, m, v):
    m = ADAM_B1 * m + (1.0 - ADAM_B1) * g
    v = ADAM_B2 * v + (1.0 - ADAM_B2) * (g * g)
    m_hat = m / (1.0 - ADAM_B1 ** ADAM_STEP)
    v_hat = v / (1.0 - ADAM_B2 ** ADAM_STEP)
    delta = -ADAM_LR * (m_hat / (jnp.sqrt(v_hat) + ADAM_EPS) + ADAM_WD * w)
    return delta, m, v


def _adamw(name, w, g, m, v):
    tr = min(256, w.shape[0])

    def body(w_ref, g_ref, m_ref, v_ref, d_ref, nm_ref, nv_ref):
        d_ref[...], nm_ref[...], nv_ref[...] = _adamw_math(w_ref[...], g_ref[...], m_ref[...], v_ref[...])

    blk = pl.BlockSpec((tr, w.shape[1]), lambda i: (i, 0))
    return pl.pallas_call(
        body, name="adamw_" + name, grid=(w.shape[0] // tr,), in_specs=[blk] * 4, out_specs=[blk] * 3,
        out_shape=[jax.ShapeDtypeStruct(w.shape, F32)] * 3, compiler_params=_cparams(("parallel",)),
    )(w, g, m, v)


def _small_step(gpk, wpk, mpk, vpk):
    row, col, width = SMALL["ret_decay_logit"]

    def body(g_ref, w_ref, m_ref, v_ref, og_ref, od_ref, om_ref, ov_ref, gbuf, send_sems, recv_sems):
        x, y, c = _position()
        me = 4 * x + 2 * y + c
        gbuf[me] = g_ref[...]
        sends = []
        for k in range(1, 8):
            to = (x ^ (k >> 2), y ^ ((k >> 1) & 1), c ^ (k & 1))
            cp = pltpu.make_async_remote_copy(src_ref=g_ref, dst_ref=gbuf.at[me], send_sem=send_sems.at[k - 1],
                                              recv_sem=recv_sems.at[k - 1], device_id=to, device_id_type=MESH)
            cp.start()
            sends.append(cp)
        for k in range(1, 8):
            frm = me ^ k
            pltpu.make_async_remote_copy(src_ref=g_ref, dst_ref=gbuf.at[frm], send_sem=send_sems.at[k - 1],
                                         recv_sem=recv_sems.at[k - 1], device_id=(x, y, c), device_id_type=MESH).wait_recv()
        for cp in sends:
            cp.wait_send()
        tot = gbuf[0]
        for d in range(1, 8):
            tot = tot + gbuf[d]
        w = w_ref[...]
        r_i = lax.broadcasted_iota(jnp.int32, w.shape, 0)
        c_i = lax.broadcasted_iota(jnp.int32, w.shape, 1)
        is_logit = (r_i == row) & (c_i >= col) & (c_i < col + width)
        g = jnp.where(is_logit, tot * _sigmoid(-w), tot)
        og_ref[...] = g
        od_ref[...], om_ref[...], ov_ref[...] = _adamw_math(w, g, m_ref[...], v_ref[...])

    vm = pl.BlockSpec(memory_space=pltpu.VMEM)
    shp = jax.ShapeDtypeStruct(gpk.shape, F32)
    return pl.pallas_call(
        body, name="small_step", in_specs=[vm] * 4, out_specs=[vm] * 4, out_shape=[shp] * 4,
        scratch_shapes=[pltpu.VMEM((8,) + gpk.shape, F32), pltpu.SemaphoreType.DMA((7,)), pltpu.SemaphoreType.DMA((7,))],
    )(gpk, wpk, mpk, vpk)


def _pack_small(parts):
    rows = [[] for _ in range(SMALL_ROWS)]
    for n, (r, col, width) in sorted(SMALL.items(), key=lambda kv: (kv[1][0], kv[1][1])):
        rows[r].append((col, parts[n].reshape(-1).astype(F32)))
    out = []
    for r in range(SMALL_ROWS):
        segs, pos = [], 0
        for col, vec in rows[r]:
            assert col == pos
            segs.append(vec)
            pos += vec.shape[0]
        if pos < PACK_COLS:
            segs.append(jnp.zeros((PACK_COLS - pos,), F32))
        out.append(jnp.concatenate(segs))
    return jnp.stack(out)


def _unpack_small(pk, shapes):
    return {n: pk[r, col:col + width].reshape(shapes[n]) for n, (r, col, width) in SMALL.items()}


WEIGHTS = ("mix_norm", "w_in", "attn_q_norm", "attn_k_norm", "ret_decay_logit", "ret_norm_gain", "w_attn_o", "w_ret_o", "w_out",
           "mlp_norm", "w_up", "w_down", "ple_norm", "w_ple_gate", "w_ple", "final_norm")


def kernel(x, p, mix_norm, w_in, attn_q_norm, attn_k_norm, ret_decay_logit, ret_norm_gain, w_attn_o, w_ret_o, w_out, mlp_norm, w_up, w_down, ple_norm, w_ple_gate, w_ple, final_norm, loss_target, m_mix_norm, m_w_in, m_attn_q_norm, m_attn_k_norm, m_ret_decay_logit, m_ret_norm_gain, m_w_attn_o, m_w_ret_o, m_w_out, m_mlp_norm, m_w_up, m_w_down, m_ple_norm, m_w_ple_gate, m_w_ple, m_final_norm, v_mix_norm, v_w_in, v_attn_q_norm, v_attn_k_norm, v_ret_decay_logit, v_ret_norm_gain, v_w_attn_o, v_w_ret_o, v_w_out, v_mlp_norm, v_w_up, v_w_down, v_ple_norm, v_w_ple_gate, v_w_ple, v_final_norm):
    args = dict(locals())
    wts = {n: args[n] for n in WEIGHTS}
    ms = {n: args["m_" + n] for n in WEIGHTS}
    vs = {n: args["v_" + n] for n in WEIGHTS}
    shapes = {n: wts[n].shape for n in WEIGHTS}
    big_names = [n for n, _ in BIG]
    xi, yi, ci = _position()
    c_idx = ci.astype(jnp.int32).reshape(1)

    chip_idx = (2 * xi + yi).astype(jnp.int32)
    slab_b = _pack_shard({n: wts[n][0].astype(BF16) for n in big_names})
    others, hb = _gather_weights(slab_b, x[0], wts["mix_norm"].reshape(1, D_MODEL))
    gathered = lax.dynamic_update_slice(others, slab_b[None], (chip_idx, 0, 0))
    full, r0 = {}, 0
    for n, axis in BIG:
        shard = shapes[n][1:]
        rows = math.prod(shard) // PACK_COLS
        if axis == 0 and shard[1] == PACK_COLS:
            full[n] = gathered[:, r0:r0 + rows].reshape(N_CHIPS * shard[0], shard[1])
        else:
            full[n] = jnp.concatenate([gathered[k, r0:r0 + rows].reshape(shard) for k in range(N_CHIPS)], axis=axis)
        r0 += rows
    small = {n: wts[n].reshape(wts[n].shape[1:] if wts[n].ndim > 1 else wts[n].shape) for n in SMALL}

    loss_part, grad_x, gw, gs = _local_step(x[0], p[0, 0], loss_target[0], full, small, hb)
    loss = lax.psum(loss_part, ("x", "y", "c"))

    slabs = jnp.stack([_pack_shard({n: _shard_of(gw[n], axis, k) for n, axis in BIG}) for k in range(N_CHIPS)])
    halves = slabs.reshape(N_CHIPS, 2, HALF_ROWS, PACK_COLS)
    chip_part, chip_part_b = _add_my_half(halves, _exchange_halves(halves), c_idx)
    mine = _sum_chips(chip_part, _scatter_to_chips(chip_part_b), chip_idx.reshape(1))
    both = jnp.stack([mine, _join_halves(mine)])
    reduced = jnp.where(ci == 0, both, both[::-1]).reshape(2 * HALF_ROWS, PACK_COLS)
    g_big = _unpack_shard(reduced, {n: shapes[n][1:] for n in big_names})
    big_out = [{}, {}, {}, {}]
    for n in big_names:
        big_out[0][n] = g_big[n][None]
        for kind, a in enumerate(_adamw(n, wts[n][0], g_big[n], ms[n][0], vs[n][0])):
            big_out[kind + 1][n] = a[None]

    sm_out = _small_step(_pack_small(gs), _pack_small({n: wts[n] for n in SMALL}), _pack_small({n: ms[n] for n in SMALL}),
                         _pack_small({n: vs[n] for n in SMALL}))
    small_out = [_unpack_small(a, {n: shapes[n] for n in SMALL}) for a in sm_out]

    outs = [loss, grad_x[None]]
    for kind in range(4):
        for n in WEIGHTS:
            outs.append(small_out[kind][n] if n in SMALL else big_out[kind][n])
    return tuple(outs)
```

```python
import math

import jax
import jax.numpy as jnp
from jax import lax
from jax.experimental import pallas as pl
from jax.experimental.pallas import tpu as pltpu

F32 = jnp.float32
BF16 = jnp.bfloat16
MESH = pl.DeviceIdType.MESH

D_MODEL = 1024
GRID_W = 64
ATTN_HEAD_DIM = 64
ATTN_HEADS = 8
ATTN_KV_HEADS = 2
ATTN_GROUP = ATTN_HEADS // ATTN_KV_HEADS
RET_HEAD_DIM = 128
RET_HEADS = 4
ATTN_Q_W = 512
ATTN_KV_W = 128
RET_W = 512
IN_W = 4864
D_FF = 4096
RET_CHUNK = 256
ROPE_THETA = 10000.0
NORM_EPS = 1e-6
GN_EPS = 1e-5
ATTN_SCALE = ATTN_HEAD_DIM ** -0.5
LOG2E = math.log2(math.e)
Q_FOLD = ATTN_SCALE * LOG2E
RET_SCALE = RET_HEAD_DIM ** -0.5

C_AQ, C_AK, C_AV, C_RQ, C_RK, C_RV, C_RG, C_GA, C_GR = 0, 512, 640, 768, 1280, 1792, 2304, 2816, 3840

ADAM_LR = 0.001
ADAM_B1 = 0.9
ADAM_B2 = 0.999
ADAM_EPS = 1e-08
ADAM_WD = 0.01
ADAM_STEP = 10

LANES = 128
VMEM_LIMIT = 56 << 20
SEQ_TILE = 512
EPI_PIECE = 256

BIG = (("w_in", 1), ("w_attn_o", 1), ("w_ret_o", 1), ("w_out", 0), ("w_up", 1), ("w_down", 0), ("w_ple_gate", 0), ("w_ple", 1))
SMALL_ROWS = 8
SMALL = {"mix_norm": (0, 0, 1024), "mlp_norm": (1, 0, 1024), "ple_norm": (2, 0, 1024), "final_norm": (3, 0, 1024),
         "ret_norm_gain": (4, 0, 512), "attn_q_norm": (4, 512, 64), "attn_k_norm": (4, 576, 64), "ret_decay_logit": (4, 640, 8)}


def _seq_tile(s):
    return min(SEQ_TILE, s // 2)


def _cparams(sem=None, vmem=VMEM_LIMIT):
    return pltpu.CompilerParams(dimension_semantics=sem, vmem_limit_bytes=vmem)


def _mm(name, a, b, *, ta=False, tb=False, tm, tn, tk, out_dtypes=(F32,), epi=None, epi_ins=(), consts=(), n_sums=0, j_outer=False):
    if ta:
        kdim, m = a.shape
    else:
        m, kdim = a.shape
    n = b.shape[0] if tb else b.shape[1]
    tm, tn, tk = min(tm, m), min(tn, n), min(tk, kdim)
    assert m % tm == 0 and n % tn == 0 and kdim % tk == 0, (name, m, n, kdim, tm, tn, tk)
    nk = kdim // tk
    e_arrs, e_cols = [], []
    for item in epi_ins:
        if isinstance(item, tuple):
            arr, start = item
            assert tn == n and start % EPI_PIECE == 0 and n % EPI_PIECE == 0
            for piece in range(n // EPI_PIECE):
                e_arrs.append(arr)
                e_cols.append(start // EPI_PIECE + piece)
        else:
            e_arrs.append(item)
            e_cols.append(None)
    n_e, n_c, n_o = len(e_arrs), len(consts), len(out_dtypes)
    assert n_sums == 0 or tn == n

    def body(*refs):
        a_ref, b_ref = refs[0], refs[1]
        e_refs = refs[2:2 + n_e]
        c_refs = refs[2 + n_e:2 + n_e + n_c]
        o_refs = refs[2 + n_e + n_c:2 + n_e + n_c + n_o]
        s_refs = refs[2 + n_e + n_c + n_o:2 + n_e + n_c + n_o + n_sums]
        acc_ref = refs[2 + n_e + n_c + n_o + n_sums] if nk > 1 else None
        k = pl.program_id(2)
        if n_sums:
            @pl.when((pl.program_id(1 if j_outer else 0) == 0) & (k == 0))
            def _():
                for r in s_refs:
                    r[...] = jnp.zeros(r.shape, F32)
        av = a_ref[...].astype(BF16)
        bv = b_ref[...].astype(BF16)
        dims = (((0,) if ta else (1,), (1,) if tb else (0,)), ((), ()))
        part = lax.dot_general(av, bv, dims, preferred_element_type=F32)

        def finish(acc):
            vals = epi(acc, e_refs, c_refs) if epi is not None else (acc,)
            for o_ref, v in zip(o_refs, vals[:n_o]):
                o_ref[...] = v.astype(o_ref.dtype)
            for s_ref, v in zip(s_refs, vals[n_o:]):
                _acc_add(s_ref, v)

        if nk == 1:
            finish(part)
        else:
            @pl.when(k == 0)
            def _():
                acc_ref[...] = part

            @pl.when(k > 0)
            def _():
                acc_ref[...] += part

            @pl.when(k == nk - 1)
            def _():
                finish(acc_ref[...])

    def spec(shape, index):
        return pl.BlockSpec(shape, (lambda j, i, k: index(i, j, k)) if j_outer else index)

    a_spec = spec((tk, tm), lambda i, j, k: (k, i)) if ta else spec((tm, tk), lambda i, j, k: (i, k))
    b_spec = spec((tn, tk), lambda i, j, k: (j, k)) if tb else spec((tk, tn), lambda i, j, k: (k, j))
    o_spec = spec((tm, tn), lambda i, j, k: (i, j))
    c_specs = [spec(c.shape, lambda i, j, k, nd=c.ndim: (0,) * nd) for c in consts]
    outs = pl.pallas_call(
        body, name=name,
        grid=(n // tn, m // tm, nk) if j_outer else (m // tm, n // tn, nk),
        in_specs=([a_spec, b_spec]
                  + [o_spec if cb is None else spec((tm, EPI_PIECE), lambda i, j, k, cb=cb: (i, cb)) for cb in e_cols] + c_specs),
        out_specs=[o_spec] * n_o + [spec((8, n), lambda i, j, k: (0, 0))] * n_sums,
        out_shape=[jax.ShapeDtypeStruct((m, n), dt) for dt in out_dtypes] + [jax.ShapeDtypeStruct((8, n), F32)] * n_sums,
        scratch_shapes=[pltpu.VMEM((tm, tn), F32)] if nk > 1 else [],
        compiler_params=_cparams(("arbitrary",) * 3 if n_sums else ("parallel", "parallel", "arbitrary")),
    )(a, b, *e_arrs, *consts)
    return outs[0] if n_o + n_sums == 1 else outs


def _rows(arr, tr):
    return (arr, pl.BlockSpec((tr, arr.shape[1]), lambda i: (i, 0)))


def _win(arr, tr, start, width):
    bw = math.gcd(start, width) if start else width
    assert bw % LANES == 0
    return [(arr, pl.BlockSpec((tr, bw), lambda i, cb=start // bw + p: (i, cb))) for p in range(width // bw)]


def _ct(arr):
    return (arr, pl.BlockSpec((None,) + arr.shape[1:], lambda i: (i, 0, 0)))


def _whole(arr):
    return (arr, pl.BlockSpec(arr.shape, lambda i, nd=arr.ndim: (0,) * nd))


def _cat(refs):
    vals = [r[...].astype(F32) for r in refs]
    return vals[0] if len(vals) == 1 else jnp.concatenate(vals, axis=1)


def _seqtiled(name, fn, n_tiles, ins, outs, acc_widths=()):
    n_i, n_o, n_a = len(ins), len(outs), len(acc_widths)

    def body(*refs):
        i_refs, o_refs, a_refs = refs[:n_i], refs[n_i:n_i + n_o], refs[n_i + n_o:]
        if n_a:
            @pl.when(pl.program_id(0) == 0)
            def _():
                for r in a_refs:
                    r[...] = jnp.zeros(r.shape, F32)
        fn(list(i_refs), list(o_refs), list(a_refs))

    res = pl.pallas_call(
        body, name=name, grid=(n_tiles,),
        in_specs=[s for _, s in ins],
        out_specs=[s for _, _, s in outs] + [pl.BlockSpec((8, w), lambda i: (0, 0)) for w in acc_widths],
        out_shape=[jax.ShapeDtypeStruct(sh, dt) for sh, dt, _ in outs] + [jax.ShapeDtypeStruct((8, w), F32) for w in acc_widths],
        compiler_params=_cparams(("arbitrary",)),
    )(*[a for a, _ in ins])
    return res


def _acc_add(acc_ref, val):
    acc_ref[0:1, :] += jnp.sum(val, axis=0, keepdims=True)


def _out_rows(s, w, dt, tr):
    return ((s, w), dt, pl.BlockSpec((tr, w), lambda i: (i, 0)))


def _out_ct(s, w, dt, t):
    return ((s // t, w, t), dt, pl.BlockSpec((None, w, t), lambda i: (i, 0, 0)))


def _rms_fwd(x, gain):
    r = lax.rsqrt(jnp.mean(x * x, axis=-1, keepdims=True) + NORM_EPS)
    return x * r * gain


def _rms_bwd(dy, x, gain):
    r = lax.rsqrt(jnp.mean(x * x, axis=-1, keepdims=True) + NORM_EPS)
    xn = x * r
    dyg = dy * gain
    dx = r * (dyg - xn * jnp.mean(dyg * xn, axis=-1, keepdims=True))
    return dx, dy * xn


def _seg_mean(y, hd):
    w = y.shape[1]
    pieces = []
    for s in range(0, w, LANES):
        v = y[:, s:s + LANES]
        tot = jnp.sum(v, axis=1, keepdims=True)
        if hd == LANES:
            pieces.append(jnp.broadcast_to(tot, v.shape))
        else:
            low = lax.broadcasted_iota(jnp.int32, v.shape, 1) < hd
            lo = jnp.sum(jnp.where(low, v, 0.0), axis=1, keepdims=True)
            pieces.append(jnp.where(low, lo, tot - lo))
    out = pieces[0] if len(pieces) == 1 else jnp.concatenate(pieces, axis=1)
    return out * (1.0 / hd)


def _tile_lanes(t, w):
    return t if w == t.shape[1] else jnp.concatenate([t] * (w // t.shape[1]), axis=1)


def _swap_halves(x, hd):
    w = x.shape[1]
    half = hd // 2
    lane = lax.broadcasted_iota(jnp.int32, x.shape, 1)
    return jnp.where((lane % hd) < half, pltpu.roll(x, w - half, 1), pltpu.roll(x, half, 1))


def _rope(x, cos, sin_signed, hd):
    w = x.shape[1]
    return x * _tile_lanes(cos, w) + _swap_halves(x, hd) * _tile_lanes(sin_signed, w)


def _rope_t(dy, cos, sin_signed, hd):
    w = dy.shape[1]
    return dy * _tile_lanes(cos, w) + _swap_halves(dy * _tile_lanes(sin_signed, w), hd)


def _headnorm_fwd(x, gain_w, hd):
    r = lax.rsqrt(_seg_mean(x * x, hd) + NORM_EPS)
    return x * r * gain_w


def _headnorm_bwd(dy, x, gain_w, hd):
    r = lax.rsqrt(_seg_mean(x * x, hd) + NORM_EPS)
    xn = x * r
    dyg = dy * gain_w
    return r * (dyg - xn * _seg_mean(dyg * xn, hd)), dy * xn


def _sigmoid(x):
    return 1.0 / (1.0 + jnp.exp(-x))


def _rope_tables(seq_len, head_dim):
    rows = seq_len // GRID_W
    n_axis = head_dim // 4
    freqs = ROPE_THETA ** (-jnp.arange(n_axis, dtype=F32) / n_axis)
    ang_r = jnp.arange(rows, dtype=F32)[:, None] * freqs
    ang_c = jnp.arange(GRID_W, dtype=F32)[:, None] * freqs

    def expand(by_row, by_col):
        r = jnp.broadcast_to(by_row[:, None, :], (rows, GRID_W, n_axis))
        c = jnp.broadcast_to(by_col[None, :, :], (rows, GRID_W, n_axis))
        return jnp.concatenate([r, c], axis=-1).reshape(seq_len, 2 * n_axis)

    cos, sin = expand(jnp.cos(ang_r), jnp.cos(ang_c)), expand(jnp.sin(ang_r), jnp.sin(ang_c))
    reps = LANES // head_dim
    return jnp.tile(jnp.concatenate([cos, cos], axis=-1), (1, reps)), jnp.tile(jnp.concatenate([-sin, sin], axis=-1), (1, reps))


def _stage_qkv(proj, tabs, gq_w, gk_w):
    s = proj.shape[0]
    t = _seq_tile(s)
    ca, sa, cr, sr = tabs
    ins = (_win(proj, t, C_AQ, ATTN_Q_W) + _win(proj, t, C_AK, ATTN_KV_W) + _win(proj, t, C_AV, ATTN_KV_W)
           + _win(proj, t, C_RQ, RET_W) + _win(proj, t, C_RK, RET_W)
           + [_rows(ca, t), _rows(sa, t), _rows(cr, t), _rows(sr, t), _whole(gq_w), _whole(gk_w)])

    def fn(i, o, a):
        aq, ak, av = (i[n][...].astype(F32) for n in range(3))
        rq, rk = _cat(i[3:5]), _cat(i[5:7])
        ca_, sa_, cr_, sr_ = i[7][...], i[8][...], i[9][...], i[10][...]
        qr = _rope(_headnorm_fwd(aq, i[11][...], ATTN_HEAD_DIM), ca_, sa_, ATTN_HEAD_DIM) * Q_FOLD
        kr = _rope(_headnorm_fwd(ak, i[12][...], ATTN_HEAD_DIM), ca_, sa_, ATTN_HEAD_DIM)
        qt = qr.T.astype(BF16)
        zeros = jnp.zeros((ATTN_HEAD_DIM, t), BF16)
        for h in range(ATTN_HEADS):
            g = h // ATTN_GROUP
            blk = qt[h * ATTN_HEAD_DIM:(h + 1) * ATTN_HEAD_DIM, :]
            o[0][h * LANES + g * ATTN_HEAD_DIM:h * LANES + (g + 1) * ATTN_HEAD_DIM, :] = blk
            o[0][h * LANES + (1 - g) * ATTN_HEAD_DIM:h * LANES + (2 - g) * ATTN_HEAD_DIM, :] = zeros
        o[1][...] = kr.astype(BF16)
        o[2][...] = kr.T.astype(BF16)
        o[3][...] = av.astype(BF16)
        o[4][...] = av.T.astype(BF16)
        o[5][...] = _rope(rq, cr_, sr_, RET_HEAD_DIM) * RET_SCALE
        o[6][...] = _rope(rk, cr_, sr_, RET_HEAD_DIM)

    outs = [_out_ct(s, ATTN_HEADS * LANES, BF16, t), _out_rows(s, ATTN_KV_W, BF16, t), _out_ct(s, ATTN_KV_W, BF16, t),
            _out_rows(s, ATTN_KV_W, BF16, t), _out_ct(s, ATTN_KV_W, BF16, t), _out_rows(s, RET_W, F32, t), _out_rows(s, RET_W, F32, t)]
    return _seqtiled("qkv_prep", fn, s // t, ins, outs)


def _groupnorm_gate(ry, rg, gain):
    mu = _seg_mean(ry, RET_HEAD_DIM)
    d = ry - mu
    rs = lax.rsqrt(_seg_mean(d * d, RET_HEAD_DIM) + GN_EPS)
    return d * rs, rs, _sigmoid(rg)


def _stage_mix_post(ry_f, ry_b, proj, o_ct, gain):
    s = proj.shape[0]
    t = _seq_tile(s)
    ins = [_rows(ry_f, t), _rows(ry_b, t)] + _win(proj, t, C_RG, RET_W) + [_ct(o_ct), _whole(gain)]

    def fn(i, o, a):
        ry = i[0][...] + i[1][...]
        rg = _cat(i[2:4])
        gn, _, sg = _groupnorm_gate(ry, rg, None)
        o[0][...] = (gn * i[5][...] * (rg * sg)).astype(BF16)
        o[1][...] = i[4][...].astype(F32).T.astype(BF16)

    return _seqtiled("mix_post", fn, s // t, ins, [_out_rows(s, RET_W, BF16, t), _out_rows(s, ATTN_Q_W, BF16, t)])


def _stage_mix_post_bwd(dattn, attn_rows, drz, ry_f, ry_b, proj, gain):
    s = proj.shape[0]
    t = _seq_tile(s)
    ins = ([_rows(dattn, t), _rows(attn_rows, t), _rows(drz, t), _rows(ry_f, t), _rows(ry_b, t)]
           + _win(proj, t, C_RG, RET_W) + [_whole(gain)])

    def fn(i, o, a):
        da = i[0][...]
        dat = da.T
        prod_t = (da * i[1][...].astype(F32)).T
        dat_b = dat.astype(BF16)
        zeros = jnp.zeros((ATTN_HEAD_DIM, t), BF16)
        for h in range(ATTN_HEADS):
            g = h // ATTN_GROUP
            o[0][h * LANES + g * ATTN_HEAD_DIM:h * LANES + (g + 1) * ATTN_HEAD_DIM, :] = dat_b[h * ATTN_HEAD_DIM:(h + 1) * ATTN_HEAD_DIM, :]
            o[0][h * LANES + (1 - g) * ATTN_HEAD_DIM:h * LANES + (2 - g) * ATTN_HEAD_DIM, :] = zeros
            o[1][h] = jnp.sum(prod_t[h * ATTN_HEAD_DIM:(h + 1) * ATTN_HEAD_DIM, :], axis=0, keepdims=True)
        ry = i[3][...] + i[4][...]
        rg = _cat(i[5:7])
        gain_ = i[7][...]
        gn, rs, sg = _groupnorm_gate(ry, rg, None)
        dz = i[2][...]
        silu = rg * sg
        _acc_add(a[0], dz * gn * silu)
        dgn = dz * gain_ * silu
        o[2][...] = rs * (dgn - _seg_mean(dgn, RET_HEAD_DIM) - gn * _seg_mean(dgn * gn, RET_HEAD_DIM))
        o[3][...] = (dz * gn * gain_ * (sg * (1.0 + rg * (1.0 - sg)))).astype(BF16)

    outs = [_out_ct(s, ATTN_HEADS * LANES, BF16, t),
            ((ATTN_HEADS, s // t, 1, t), F32, pl.BlockSpec((ATTN_HEADS, None, 1, t), lambda i: (0, i, 0, 0))),
            _out_rows(s, RET_W, F32, t), _out_rows(s, RET_W, BF16, t)]
    return _seqtiled("mix_post_bwd", fn, s // t, ins, outs, acc_widths=(RET_W,))


def _stage_dproj(proj, dq_ct, dk8, dv8, rgrads, drg, dga, dgr, tabs, gq_w, gk_w):
    s = proj.shape[0]
    t = _seq_tile(s)
    ca, sa, cr, sr = tabs
    kv8 = pl.BlockSpec((ATTN_HEADS, t, ATTN_KV_W), lambda i: (0, i, 0))
    ins = (_win(proj, t, C_AQ, ATTN_Q_W) + _win(proj, t, C_AK, ATTN_KV_W) + [_ct(dq_ct), (dk8, kv8), (dv8, kv8)]
           + [_rows(g, t) for g in rgrads] + [_rows(drg, t), _rows(dga, t), _rows(dgr, t)]
           + [_rows(ca, t), _rows(sa, t), _rows(cr, t), _rows(sr, t), _whole(gq_w), _whole(gk_w)])

    def fn(i, o, a):
        aq, ak = i[0][...].astype(F32), i[1][...].astype(F32)
        dq_f, dk_f, dv_f, dq_b, dk_b, dv_b = (r[...].astype(F32) for r in i[5:11])
        ca_, sa_, cr_, sr_ = i[14][...], i[15][...], i[16][...], i[17][...]
        dqn = _rope_t(i[2][...].T * ATTN_SCALE, ca_, sa_, ATTN_HEAD_DIM)
        daq, gq_rows = _headnorm_bwd(dqn, aq, i[18][...], ATTN_HEAD_DIM)
        dkn = _rope_t(jnp.sum(i[3][...].astype(F32), axis=0) * (1.0 / LOG2E), ca_, sa_, ATTN_HEAD_DIM)
        dak, gk_rows = _headnorm_bwd(dkn, ak, i[19][...], ATTN_HEAD_DIM)
        _acc_add(a[0], gq_rows)
        _acc_add(a[1], gk_rows)
        out = o[0]
        out[:, C_AQ:C_AQ + ATTN_Q_W] = daq.astype(BF16)
        out[:, C_AK:C_AK + ATTN_KV_W] = dak.astype(BF16)
        out[:, C_AV:C_AV + ATTN_KV_W] = jnp.sum(i[4][...].astype(F32), axis=0).astype(BF16)
        out[:, C_RQ:C_RQ + RET_W] = _rope_t((dq_f + dq_b) * RET_SCALE, cr_, sr_, RET_HEAD_DIM).astype(BF16)
        out[:, C_RK:C_RK + RET_W] = _rope_t(dk_f + dk_b, cr_, sr_, RET_HEAD_DIM).astype(BF16)
        out[:, C_RV:C_RV + RET_W] = (dv_f + dv_b).astype(BF16)
        out[:, C_RG:C_RG + RET_W] = i[11][...]
        out[:, C_GA:C_GA + D_MODEL] = i[12][...]
        out[:, C_GR:C_GR + D_MODEL] = i[13][...]

    return _seqtiled("dproj", fn, s // t, ins, [_out_rows(s, IN_W, BF16, t)], acc_widths=(ATTN_Q_W, ATTN_KV_W))


def _attn_fwd(q_ct, k_rows, v_ct):
    nq, _, t = q_ct.shape
    s = nq * t
    nk = nq
    assert nk % 2 == 0
    n_ch = next(n for n in (8, 4, 2) if nq % n == 0)
    halves = 2 if t % (2 * LANES) == 0 else 1
    tq = t // halves
    n_par = n_ch * halves

    def body(q_ref, k_ref, v_ref, o_ref, lse_ref, *bufs):
        sbuf = tuple(bufs[2 * w:2 * w + 2] for w in range(n_par))
        pbuf = tuple(bufs[2 * n_par + 2 * w:2 * n_par + 2 * w + 2] for w in range(n_par))

        def where(w):
            return w // halves, slice((w % halves) * tq, (w % halves + 1) * tq)

        def scores(w, j, slot):
            kj = k_ref[pl.ds(pl.multiple_of(j * t, t), t), :]
            cw, lanes = where(w)
            st = jnp.dot(kj, q_ref[cw, :, lanes], preferred_element_type=F32)
            sbuf[w][slot][...] = st
            return jnp.max(st, axis=0, keepdims=True)

        def probs(w, slot, cmax, m, l):
            m_new = jnp.maximum(m, cmax)
            alpha = jnp.exp2(m - m_new)
            pt = jnp.exp2(sbuf[w][slot][...] - m_new)
            pbuf[w][slot][...] = pt.astype(BF16)
            return m_new, alpha * l + jnp.sum(pt, axis=0, keepdims=True), alpha

        def values(w, j, slot, alpha, acc):
            return alpha * acc + jnp.dot(v_ref[j], pbuf[w][slot][...], preferred_element_type=F32)

        init = []
        for w in range(n_par):
            m = jnp.full((1, tq), -1e30, F32)
            l = jnp.zeros((1, tq), F32)
            cmax0 = scores(w, 0, 0)
            cmax1 = scores(w, 1, 1)
            m, l, alpha0 = probs(w, 0, cmax0, m, l)
            init.append((m, l, jnp.zeros((ATTN_HEAD_DIM, tq), F32), cmax1, alpha0))

        def trip(n, carry):
            c = 2 * n
            out = []
            for w in range(n_par):
                m, l, acc, cmax_b, alpha_c = carry[w]
                acc = values(w, c, 0, alpha_c, acc)
                m, l, alpha1 = probs(w, 1, cmax_b, m, l)
                cmax2 = scores(w, c + 2, 0)
                acc = values(w, c + 1, 1, alpha1, acc)
                m, l, alpha2 = probs(w, 0, cmax2, m, l)
                cmax3 = scores(w, c + 3, 1)
                out.append((m, l, acc, cmax3, alpha2))
            return tuple(out)

        res = lax.fori_loop(0, nk // 2 - 1, trip, tuple(init))
        for w in range(n_par):
            m, l, acc, cmax_b, alpha_c = res[w]
            acc = values(w, nk - 2, 0, alpha_c, acc)
            m, l, alpha1 = probs(w, 1, cmax_b, m, l)
            acc = values(w, nk - 1, 1, alpha1, acc)
            cw, lanes = where(w)
            o_ref[cw, :, lanes] = (acc / l).astype(BF16)
            lse_ref[cw, :, lanes] = m + jnp.log2(l)

    return pl.pallas_call(
        body, name="attn_fwd", grid=(ATTN_HEADS, nq // n_ch),
        in_specs=[pl.BlockSpec((n_ch, LANES, t), lambda h, i: (i, h, 0)),
                  pl.BlockSpec((s, ATTN_KV_W), lambda h, i: (0, 0)),
                  pl.BlockSpec((nk, ATTN_HEAD_DIM, t), lambda h, i: (0, h // ATTN_GROUP, 0))],
        out_specs=[pl.BlockSpec((n_ch, ATTN_HEAD_DIM, t), lambda h, i: (i, h, 0)),
                   pl.BlockSpec((None, n_ch, 1, t), lambda h, i: (h, i, 0, 0))],
        out_shape=[jax.ShapeDtypeStruct((nq, ATTN_Q_W, t), BF16), jax.ShapeDtypeStruct((ATTN_HEADS, nq, 1, t), F32)],
        scratch_shapes=[pltpu.VMEM((t, tq), F32)] * (2 * n_par) + [pltpu.VMEM((t, tq), BF16)] * (2 * n_par),
        compiler_params=_cparams(("parallel", "parallel")),
    )(q_ct, k_rows, v_ct)


def _attn_bwd(q_ct, do_ct, lse, delta, k_rows, v_rows, k_ct):
    nq, _, t = q_ct.shape
    s = nq * t
    kc = 4 if nq % 4 == 0 else 2
    tk = kc * t
    nk = nq // kc
    assert nq % 2 == 0 and nq % kc == 0

    def body(q_ref, do_ref, lse_ref, delta_ref, k_ref, v_ref, kt_ref, dq_ref, dk_ref, dv_ref, dk_acc, dv_acc,
             sb0, sb1, db0, db1, pb0, pb1, gb0, gb1):
        j = pl.program_id(1)
        sb, db, pb, gb = (sb0, sb1), (db0, db1), (pb0, pb1), (gb0, gb1)

        @pl.when(j == 0)
        def _():
            dq_ref[...] = jnp.zeros(dq_ref.shape, F32)

        kj, vj = k_ref[...], v_ref[...]
        ktj = jnp.concatenate([kt_ref[u] for u in range(kc)], axis=1)
        dk_acc[...] = jnp.zeros(dk_acc.shape, F32)
        dv_acc[...] = jnp.zeros(dv_acc.shape, F32)

        def products(i, slot):
            sb[slot][...] = jnp.dot(kj, q_ref[i], preferred_element_type=F32)
            db[slot][...] = jnp.dot(vj, do_ref[i], preferred_element_type=F32)

        def cotangents(i, slot):
            pt = jnp.exp2(sb[slot][...] - lse_ref[i])
            pb[slot][...] = pt.astype(BF16)
            gb[slot][...] = (pt * (db[slot][...] - delta_ref[i])).astype(BF16)

        def accumulate(i, slot):
            dst = gb[slot][...]
            dv_acc[...] += _nt(pb[slot][...], do_ref[i])
            dk_acc[...] += _nt(dst, q_ref[i])
            dq_ref[i] += jnp.dot(ktj, dst, preferred_element_type=F32)

        products(0, 0)
        products(1, 1)
        cotangents(0, 0)

        def trip(n, carry):
            c = 2 * n
            accumulate(c, 0)
            cotangents(c + 1, 1)
            products(c + 2, 0)
            accumulate(c + 1, 1)
            cotangents(c + 2, 0)
            products(c + 3, 1)
            return carry

        lax.fori_loop(0, nq // 2 - 1, trip, 0)
        accumulate(nq - 2, 0)
        cotangents(nq - 1, 1)
        accumulate(nq - 1, 1)
        dk_ref[...] = dk_acc[...].astype(dk_ref.dtype)
        dv_ref[...] = dv_acc[...].astype(dv_ref.dtype)

    per_head = pl.BlockSpec((nq, LANES, t), lambda h, j: (0, h, 0))
    stat = pl.BlockSpec((None, nq, 1, t), lambda h, j: (h, 0, 0, 0))
    kv_rows = pl.BlockSpec((tk, ATTN_KV_W), lambda h, j: (j, 0))
    kv_out = pl.BlockSpec((None, tk, ATTN_KV_W), lambda h, j: (h, j, 0))
    return pl.pallas_call(
        body, name="attn_bwd", grid=(ATTN_HEADS, nk),
        in_specs=[per_head, per_head, stat, stat, kv_rows, kv_rows,
                  pl.BlockSpec((kc, ATTN_HEAD_DIM, t), lambda h, j: (j, h // ATTN_GROUP, 0))],
        out_specs=[pl.BlockSpec((nq, ATTN_HEAD_DIM, t), lambda h, j: (0, h, 0)), kv_out, kv_out],
        out_shape=[jax.ShapeDtypeStruct((nq, ATTN_Q_W, t), F32), jax.ShapeDtypeStruct((ATTN_HEADS, s, ATTN_KV_W), BF16),
                   jax.ShapeDtypeStruct((ATTN_HEADS, s, ATTN_KV_W), BF16)],
        scratch_shapes=([pltpu.VMEM((tk, ATTN_KV_W), F32)] * 2 + [pltpu.VMEM((tk, t), F32)] * 4 + [pltpu.VMEM((tk, t), BF16)] * 4),
        compiler_params=_cparams(("parallel", "arbitrary")),
    )(q_ct, do_ct, lse, delta, k_rows, v_rows, k_ct)


def _log_sigmoid(x):
    t = jnp.exp(-jnp.abs(x))
    log1p_t = jnp.where(t < 1e-2, t * (1.0 - t * (0.5 - t * (1.0 / 3.0))), jnp.log(1.0 + t))
    return jnp.minimum(x, 0.0) - log1p_t


def _decay_tables(logit, backward):
    c, hd = RET_CHUNK, RET_HEAD_DIM

    def lam(shape):
        return _log_sigmoid(jnp.full(shape, logit, F32))

    ii = lax.broadcasted_iota(jnp.int32, (c, c), 0).astype(F32)
    jj = lax.broadcasted_iota(jnp.int32, (c, c), 1).astype(F32)
    pos = lax.broadcasted_iota(jnp.int32, (c, hd), 0).astype(F32)
    if not backward:
        dist, dist_t = jnp.maximum(ii - jj, 0.0), jnp.maximum(jj - ii, 0.0)
        mask, mask_t = ii >= jj, jj >= ii
        e_q, e_k = pos + 1.0, (c - 1.0) - pos
    else:
        dist, dist_t = jnp.maximum(jj - ii, 0.0), jnp.maximum(ii - jj, 0.0)
        mask, mask_t = jj > ii, ii > jj
        e_q, e_k = c - pos, pos
    lam_cc, lam_row = lam((c, c)), lam((c, hd))
    return dict(
        d=jnp.where(mask, jnp.exp(lam_cc * dist), 0.0), d_t=jnp.where(mask_t, jnp.exp(lam_cc * dist_t), 0.0), dist=dist,
        qdec=jnp.exp(lam_row * e_q), kdec=jnp.exp(lam_row * e_k), e_q=e_q, e_k=e_k, gam=jnp.exp(lam((hd, hd)) * c))


def _nt(a, b):
    return lax.dot_general(a, b, (((1,), (1,)), ((), ())), preferred_element_type=F32)


def _ret_sub(n_chunks):
    return max(1, min(n_chunks, 512 // RET_CHUNK))


def _ret_fwd(logits, q, k, proj):
    s = q.shape[0]
    c = RET_CHUNK
    sub = _ret_sub(s // c)
    nb = s // (c * sub)
    block = (lambda n: n, lambda n: nb - 1 - n)
    order = (tuple(range(sub)), tuple(reversed(range(sub))))
    vwin = _win(proj, c * sub, C_RV, RET_W)
    nv = len(vwin)
    vw = RET_W // nv
    per = 2 + nv

    def body(lg_ref, *refs):
        ins, outs, states = refs[:2 * per], refs[2 * per:2 * per + 4], refs[2 * per + 4:]

        @pl.when(pl.program_id(0) == 0)
        def _():
            for st in states:
                st[...] = jnp.zeros(st.shape, F32)

        for h in range(RET_HEADS):
            for d in range(2):
                q_ref, k_ref, v_refs = ins[d * per], ins[d * per + 1], ins[d * per + 2:(d + 1) * per]
                y_ref, st_ref, state = outs[2 * d], outs[2 * d + 1], states[d]
                tb = _decay_tables(lg_ref[d, h], bool(d))
                sl = slice(h * RET_HEAD_DIM, (h + 1) * RET_HEAD_DIM)
                off = h * RET_HEAD_DIM
                sh = state[h]
                for u in order[d]:
                    rows = slice(u * c, (u + 1) * c)
                    qh, kh = q_ref[rows, sl], k_ref[rows, sl]
                    vb = v_refs[off // vw][rows, off % vw:off % vw + RET_HEAD_DIM].astype(BF16)
                    a = _nt(qh.astype(BF16), kh.astype(BF16)) * tb["d"]
                    st_ref[u, h] = sh
                    y_ref[rows, sl] = (jnp.dot(a.astype(BF16), vb, preferred_element_type=F32)
                                       + jnp.dot((qh * tb["qdec"]).astype(BF16), sh.astype(BF16), preferred_element_type=F32))
                    sh = tb["gam"] * sh + jnp.dot((kh * tb["kdec"]).T.astype(BF16), vb, preferred_element_type=F32)
                state[h] = sh

    hmat = (RET_HEADS, RET_HEAD_DIM, RET_HEAD_DIM)
    in_specs, out_specs, args = [pl.BlockSpec(memory_space=pltpu.SMEM)], [], [logits]
    for d in range(2):
        rows = pl.BlockSpec((c * sub, RET_W), lambda n, d=d: (block[d](n), 0))
        in_specs += [rows, rows] + [pl.BlockSpec(sp.block_shape, lambda n, d=d, cb=sp.index_map(0)[1]: (block[d](n), cb)) for _, sp in vwin]
        args += [q, k] + [a for a, _ in vwin]
        out_specs += [rows, pl.BlockSpec((sub,) + hmat, lambda n, d=d: (block[d](n), 0, 0, 0))]
    return pl.pallas_call(
        body, name="ret_fwd", grid=(nb,), in_specs=in_specs, out_specs=out_specs,
        out_shape=[jax.ShapeDtypeStruct((s, RET_W), F32), jax.ShapeDtypeStruct((nb * sub,) + hmat, F32)] * 2,
        scratch_shapes=[pltpu.VMEM(hmat, F32)] * 2,
        compiler_params=_cparams(("arbitrary",)),
    )(*args)


def _ret_bwd(logits, q, k, proj, dy, st_f, st_b):
    s = q.shape[0]
    c = RET_CHUNK
    sub = _ret_sub(s // c)
    nb = s // (c * sub)
    block = (lambda n: nb - 1 - n, lambda n: n)
    order = (tuple(reversed(range(sub))), tuple(range(sub)))
    vwin = _win(proj, c * sub, C_RV, RET_W)
    nv = len(vwin)
    vw = RET_W // nv
    per = 4 + nv

    def body(lg_ref, *refs):
        ins, outs, scr = refs[:2 * per], refs[2 * per:2 * per + 8], refs[2 * per + 8:]
        n = pl.program_id(0)

        @pl.when(n == 0)
        def _():
            for r in scr:
                r[...] = jnp.zeros(r.shape, F32)

        for h in range(RET_HEADS):
            for d in range(2):
                q_ref, k_ref, dy_ref, st_ref = ins[d * per:d * per + 4]
                v_refs = ins[d * per + 4:(d + 1) * per]
                dq_ref, dk_ref, dv_ref = outs[4 * d:4 * d + 3]
                dstate, lacc = scr[2 * d], scr[2 * d + 1]
                tb = _decay_tables(lg_ref[d, h], bool(d))
                sl = slice(h * RET_HEAD_DIM, (h + 1) * RET_HEAD_DIM)
                off = h * RET_HEAD_DIM
                dsh = dstate[h]
                lsum = lacc[h, 0:1, :]
                for u in order[d]:
                    rows = slice(u * c, (u + 1) * c)
                    qh, kh, dyh = q_ref[rows, sl], k_ref[rows, sl], dy_ref[rows, sl]
                    vb = v_refs[off // vw][rows, off % vw:off % vw + RET_HEAD_DIM].astype(BF16)
                    qb, kb, dyb = qh.astype(BF16), kh.astype(BF16), dyh.astype(BF16)
                    sh = st_ref[u, h]
                    shb, dshb = sh.astype(BF16), dsh.astype(BF16)
                    qk = _nt(qb, kb)
                    g = _nt(dyb, vb) * tb["d"]
                    a_t = _nt(kb, qb) * tb["d_t"]
                    g_t = _nt(vb, dyb) * tb["d_t"]
                    qd, kd = qh * tb["qdec"], kh * tb["kdec"]
                    dqd = _nt(dyb, shb)
                    dkd = _nt(vb, dshb)
                    dq_ref[rows, sl] = (jnp.dot(g.astype(BF16), kb, preferred_element_type=F32) + dqd * tb["qdec"]).astype(dq_ref.dtype)
                    dk_ref[rows, sl] = (jnp.dot(g_t.astype(BF16), qb, preferred_element_type=F32) + dkd * tb["kdec"]).astype(dk_ref.dtype)
                    dv_ref[rows, sl] = (jnp.dot(a_t.astype(BF16), dyb, preferred_element_type=F32)
                                        + jnp.dot(kd.astype(BF16), dshb, preferred_element_type=F32)).astype(dv_ref.dtype)
                    intra = jnp.sum(tb["dist"] * qk * g, axis=0, keepdims=True)
                    lsum = (lsum + sum(intra[:, o:o + LANES] for o in range(0, c, LANES))
                            + jnp.sum(tb["e_q"] * qd * dqd + tb["e_k"] * kd * dkd, axis=0, keepdims=True)
                            + jnp.sum(float(c) * tb["gam"] * dsh * sh, axis=0, keepdims=True))
                    dsh = tb["gam"] * dsh + jnp.dot(qd.T.astype(BF16), dyb, preferred_element_type=F32)
                dstate[h] = dsh
                lacc[h, 0:1, :] = lsum

        @pl.when(n == nb - 1)
        def _():
            for d in range(2):
                for h in range(RET_HEADS):
                    outs[4 * d + 3][h] = jnp.zeros((8, LANES), F32) + jnp.sum(scr[2 * d + 1][h])

    hmat = (RET_HEADS, RET_HEAD_DIM, RET_HEAD_DIM)
    in_specs, out_specs, args = [pl.BlockSpec(memory_space=pltpu.SMEM)], [], [logits]
    for d, states in enumerate((st_f, st_b)):
        rows = pl.BlockSpec((c * sub, RET_W), lambda n, d=d: (block[d](n), 0))
        in_specs += ([rows, rows, rows, pl.BlockSpec((sub,) + hmat, lambda n, d=d: (block[d](n), 0, 0, 0))]
                     + [pl.BlockSpec(sp.block_shape, lambda n, d=d, cb=sp.index_map(0)[1]: (block[d](n), cb)) for _, sp in vwin])
        args += [q, k, dy, states] + [a for a, _ in vwin]
        out_specs += [rows, rows, rows, pl.BlockSpec((RET_HEADS, 8, LANES), lambda n: (0, 0, 0))]
    return pl.pallas_call(
        body, name="ret_bwd", grid=(nb,), in_specs=in_specs, out_specs=out_specs,
        out_shape=([jax.ShapeDtypeStruct((s, RET_W), BF16)] * 3 + [jax.ShapeDtypeStruct((RET_HEADS, 8, LANES), F32)]) * 2,
        scratch_shapes=[pltpu.VMEM(hmat, F32), pltpu.VMEM((RET_HEADS, 8, LANES), F32)] * 2,
        compiler_params=_cparams(("arbitrary",)),
    )(*args)


def _local_step(x, p, target, w, small, hb):
    s = x.shape[0]
    tabs = _rope_tables(s, ATTN_HEAD_DIM) + _rope_tables(s, RET_HEAD_DIM)
    g_mix, g_mlp, g_ple = small["mix_norm"][None, :], small["mlp_norm"][None, :], small["ple_norm"][None, :]
    g_final, g_ret = small["final_norm"][None, :], small["ret_norm_gain"][None, :]
    gq_w = jnp.tile(small["attn_q_norm"], ATTN_HEADS)[None, :]
    gk_w = jnp.tile(small["attn_k_norm"], ATTN_KV_HEADS)[None, :]
    logits = small["ret_decay_logit"]

    proj = _mm("in_proj", hb, w["w_in"], tm=1024, tn=IN_W // 2, tk=1024, out_dtypes=(BF16,), j_outer=True)
    q_ct, k_rows, k_ct, v_rows, v_ct, rq, rk = _stage_qkv(proj, tabs, gq_w, gk_w)
    o_ct, lse = _attn_fwd(q_ct, k_rows, v_ct)
    ry_f, st_f, ry_b, st_b = _ret_fwd(logits, rq, rk, proj)
    rz, attn_rows = _stage_mix_post(ry_f, ry_b, proj, o_ct, g_ret)
    a_out = _mm("attn_o", attn_rows, w["w_attn_o"], tm=1024, tn=1024, tk=512, out_dtypes=(BF16,))
    n_gate = D_MODEL // EPI_PIECE

    def epi_merge(acc, e, c):
        ga, gr = _cat(e[1:1 + n_gate]), _cat(e[1 + n_gate:1 + 2 * n_gate])
        return acc, _sigmoid(ga) * e[0][...] + _sigmoid(gr) * acc

    r_out, merged = _mm("ret_o", rz, w["w_ret_o"], tm=1024, tn=1024, tk=512, out_dtypes=(BF16, BF16), epi=epi_merge,
                        epi_ins=(a_out, (proj, C_GA), (proj, C_GR)))

    def epi_res_norm(acc, e, c):
        xr = e[0][...] + acc
        return xr, _rms_fwd(xr, c[0][...])

    x1, hm = _mm("out_proj", merged, w["w_out"], tm=1024, tn=1024, tk=1024, out_dtypes=(F32, BF16),
                 epi=epi_res_norm, epi_ins=(x,), consts=(g_mlp,))

    def epi_relu2(acc, e, c):
        r = jnp.maximum(acc, 0.0)
        return (r * r,)

    act = _mm("mlp_up", hm, w["w_up"], tm=1024, tn=2048, tk=1024, out_dtypes=(BF16,), epi=epi_relu2, j_outer=True)
    x2, hp = _mm("mlp_down", act, w["w_down"], tm=512, tn=1024, tk=D_FF, out_dtypes=(F32, BF16),
                 epi=epi_res_norm, epi_ins=(x1,), consts=(g_ple,))
    pe = _mm("ple_emb", p, w["w_ple"], tm=1024, tn=1024, tk=256)

    def epi_head(acc, e, c):
        gt = _sigmoid(acc)
        pe_, gf = e[0][...], c[0][...]
        x3 = e[1][...] + gt * pe_
        r3 = lax.rsqrt(jnp.mean(x3 * x3, axis=-1, keepdims=True) + NORM_EPS)
        x3n = x3 * r3
        err = x3n * gf - e[2][...]
        dy = err * (1.0 / D_MODEL)
        dyg = dy * gf
        dx3 = r3 * (dyg - x3n * jnp.mean(dyg * x3n, axis=-1, keepdims=True))
        return dx3, dx3 * pe_ * gt * (1.0 - gt), dx3 * gt, err * err, dy * x3n

    dx3, dzg, dpe, loss_cols, g_final_p = _mm("ple_gate", hp, w["w_ple_gate"], tm=1024, tn=1024, tk=1024, out_dtypes=(F32, BF16, BF16),
                                              epi=epi_head, epi_ins=(pe, x2, target), consts=(g_final,), n_sums=2)
    loss_sum = 0.5 / D_MODEL * jnp.sum(loss_cols)

    gw = {}
    gw["w_ple"] = _mm("g_w_ple", p, dpe, ta=True, tm=256, tn=1024, tk=2048)
    gw["w_ple_gate"] = _mm("g_w_ple_gate", hp, dzg, ta=True, tm=1024, tn=1024, tk=2048)
    def epi_norm_bwd(acc, e, c):
        dx, dg = _rms_bwd(acc, e[0][...], c[0][...])
        return e[1][...] + dx, dg

    def epi_norm_bwd_b(acc, e, c):
        tot, dg = epi_norm_bwd(acc, e, c)
        return tot, tot, dg

    dx2, dx2_b, g_ple_p = _mm("d_hp", dzg, w["w_ple_gate"], tb=True, tm=1024, tn=1024, tk=1024, out_dtypes=(F32, BF16),
                              epi=epi_norm_bwd_b, epi_ins=(x2, dx3), consts=(g_ple,), n_sums=1)

    def epi_relu2_bwd(acc, e, c):
        return (acc * (2.0 * jnp.sqrt(e[0][...]).astype(F32)),)

    du = _mm("d_u", dx2_b, w["w_down"], tb=True, tm=1024, tn=2048, tk=1024, out_dtypes=(BF16,), epi=epi_relu2_bwd, epi_ins=(act,),
             j_outer=True)
    gw["w_down"] = _mm("g_w_down", act, dx2_b, ta=True, tm=1024, tn=1024, tk=4096)
    gw["w_up"] = _mm("g_w_up", hm, du, ta=True, tm=1024, tn=1024, tk=4096)
    dx1, dx1_b, g_mlp_p = _mm("d_hm", du, w["w_up"], tb=True, tm=512, tn=1024, tk=D_FF, out_dtypes=(F32, BF16),
                              epi=epi_norm_bwd_b, epi_ins=(x1, dx2), consts=(g_mlp,), n_sums=1)
    def epi_merge_bwd(acc, e, c):
        sa, sr = _sigmoid(_cat(e[2:2 + n_gate])), _sigmoid(_cat(e[2 + n_gate:2 + 2 * n_gate]))
        return acc * sa, acc * sr, acc * e[0][...] * sa * (1.0 - sa), acc * e[1][...] * sr * (1.0 - sr)

    dao, dro, dga, dgr = _mm("d_merged", dx1_b, w["w_out"], tb=True, tm=1024, tn=1024, tk=1024, out_dtypes=(BF16,) * 4,
                             epi=epi_merge_bwd, epi_ins=(a_out, r_out, (proj, C_GA), (proj, C_GR)))
    gw["w_out"] = _mm("g_w_out", merged, dx1_b, ta=True, tm=1024, tn=1024, tk=2048)
    gw["w_attn_o"] = _mm("g_w_attn_o", attn_rows, dao, ta=True, tm=512, tn=1024, tk=2048)
    gw["w_ret_o"] = _mm("g_w_ret_o", rz, dro, ta=True, tm=512, tn=1024, tk=2048)
    dattn = _mm("d_attn", dao, w["w_attn_o"], tb=True, tm=1024, tn=512, tk=1024)
    drz = _mm("d_rz", dro, w["w_ret_o"], tb=True, tm=1024, tn=512, tk=1024)
    do_ct, delta, dry, drg, g_ret_p = _stage_mix_post_bwd(dattn, attn_rows, drz, ry_f, ry_b, proj, g_ret)
    dq_f, dk_f, dv_f, dl_f, dq_b, dk_b, dv_b, dl_b = _ret_bwd(logits, rq, rk, proj, dry, st_f, st_b)
    dq_ct, dk8, dv8 = _attn_bwd(q_ct, do_ct, lse, delta, k_rows, v_rows, k_ct)
    dproj, gq_p, gk_p = _stage_dproj(proj, dq_ct, dk8, dv8, (dq_f, dk_f, dv_f, dq_b, dk_b, dv_b), drg, dga, dgr, tabs, gq_w, gk_w)
    gw["w_in"] = _mm("g_w_in", hb, dproj, ta=True, tm=512, tn=IN_W // 2, tk=2048)
    grad_x, g_mix_p = _mm("d_h", dproj, w["w_in"], tb=True, tm=512, tn=1024, tk=IN_W, epi=epi_norm_bwd, epi_ins=(x, dx1),
                          consts=(g_mix,), n_sums=1)

    gs = {
        "mix_norm": g_mix_p[0], "mlp_norm": g_mlp_p[0], "ple_norm": g_ple_p[0], "final_norm": g_final_p[0],
        "ret_norm_gain": g_ret_p[0],
        "attn_q_norm": jnp.sum(gq_p[0].reshape(ATTN_HEADS, ATTN_HEAD_DIM), axis=0),
        "attn_k_norm": jnp.sum(gk_p[0].reshape(ATTN_KV_HEADS, ATTN_HEAD_DIM), axis=0),
        "ret_decay_logit": jnp.stack([dl_f[:, 0, 0], dl_b[:, 0, 0]]),
    }
    return loss_sum, grad_x, gw, gs


PACK_COLS = 1024
N_CHIPS = 4
HALF_ROWS = 2048


def _pack_shard(parts):
    return jnp.concatenate([parts[n].reshape(-1, PACK_COLS) for n, _ in BIG], axis=0)


def _unpack_shard(slab, shapes):
    out, r = {}, 0
    for n, _ in BIG:
        rows = math.prod(shapes[n]) // PACK_COLS
        out[n] = slab[r:r + rows].reshape(shapes[n])
        r += rows
    return out


def _shard_of(full, axis, sidx):
    size = full.shape[axis] // N_CHIPS
    return lax.slice_in_dim(full, sidx * size, (sidx + 1) * size, axis=axis)


def _position():
    x, y, c = lax.axis_index("x"), lax.axis_index("y"), lax.axis_index("c")
    return x, y, c


def _other_chips(x, y):
    return [(1 - x, y), (x, 1 - y), (1 - x, 1 - y)]


ANY = pl.BlockSpec(memory_space=pl.ANY)


def _gather_weights(slab, xs, gain):
    rows = slab.shape[0]
    half = rows // 2
    s = xs.shape[0]
    tr = min(SEQ_TILE, s)

    def body(in_ref, x_ref, g_ref, out_ref, hb_ref, send_sems, recv_sems, xbuf, hbuf):
        x, y, c = _position()
        chips = _other_chips(x, y)

        def piece(chip, core):
            return out_ref.at[2 * chip[0] + chip[1], pl.ds(core * half, half), :]

        def copy(k, chip, core, to, src=None):
            return pltpu.make_async_remote_copy(
                src_ref=piece(chip, core) if src is None else src, dst_ref=piece(chip, core),
                send_sem=send_sems.at[k], recv_sem=recv_sems.at[k], device_id=to, device_id_type=MESH)

        first = [copy(j, (x, y), c, (*chip, c), src=in_ref.at[pl.ds(c * half, half), :]) for j, chip in enumerate(chips)]
        for cp in first:
            cp.start()

        def norm_tile(i, carry):
            r = pl.ds(pl.multiple_of(i * tr, tr), tr)
            pltpu.sync_copy(x_ref.at[r], xbuf)
            hbuf[...] = _rms_fwd(xbuf[...], g_ref[...]).astype(BF16)
            pltpu.sync_copy(hbuf, hb_ref.at[r])
            return carry

        lax.fori_loop(0, s // tr, norm_tile, 0)
        passed = [copy(3 + j, chip, c, (x, y, 1 - c)) for j, chip in enumerate(chips)]
        for j, chip in enumerate(chips):
            copy(j, chip, c, (x, y, c)).wait_recv()
            passed[j].start()
        for j, chip in enumerate(chips):
            copy(3 + j, chip, 1 - c, (x, y, c)).wait_recv()
        for cp in first + passed:
            cp.wait_send()

    return pl.pallas_call(
        body, name="gather_weights", in_specs=[ANY, ANY, pl.BlockSpec(memory_space=pltpu.VMEM)], out_specs=[ANY, ANY],
        out_shape=[jax.ShapeDtypeStruct((N_CHIPS,) + slab.shape, slab.dtype), jax.ShapeDtypeStruct(xs.shape, BF16)],
        scratch_shapes=[pltpu.SemaphoreType.DMA((6,)), pltpu.SemaphoreType.DMA((6,)),
                        pltpu.VMEM((tr, xs.shape[1]), F32), pltpu.VMEM((tr, xs.shape[1]), BF16)],
    )(slab, xs, gain)


def _exchange_halves(g):
    def body(g_ref, out_ref, send_sem, recv_sem):
        x, y, c = _position()
        cp = pltpu.make_async_remote_copy(src_ref=g_ref.at[pl.ds(0, N_CHIPS), 1 - c], dst_ref=out_ref, send_sem=send_sem,
                                          recv_sem=recv_sem, device_id=(x, y, 1 - c), device_id_type=MESH)
        cp.start()
        cp.wait()

    return pl.pallas_call(
        body, name="exchange_halves", in_specs=[ANY], out_specs=ANY,
        out_shape=jax.ShapeDtypeStruct((N_CHIPS,) + g.shape[2:], g.dtype),
        scratch_shapes=[pltpu.SemaphoreType.DMA, pltpu.SemaphoreType.DMA],
    )(g)


def _add_my_half(g, r1, c_idx):
    tr = 256
    nt = g.shape[2] // tr
    out = (N_CHIPS,) + g.shape[2:]

    def body(c_ref, g_ref, r_ref, o_ref, ob_ref):
        tot = g_ref[...] + r_ref[...]
        o_ref[...] = tot
        ob_ref[...] = tot.astype(BF16)

    blk = (None, tr, PACK_COLS)
    spec = pl.BlockSpec(blk, lambda s, i, c_ref: (s, i, 0))
    return pl.pallas_call(
        body, name="add_my_half",
        grid_spec=pltpu.PrefetchScalarGridSpec(
            num_scalar_prefetch=1, grid=(N_CHIPS, nt),
            in_specs=[pl.BlockSpec((None,) + blk, lambda s, i, c_ref: (s, c_ref[0], i, 0)), spec],
            out_specs=[spec, spec]),
        out_shape=[jax.ShapeDtypeStruct(out, F32), jax.ShapeDtypeStruct(out, BF16)],
        compiler_params=_cparams(("parallel", "parallel")),
    )(c_idx, g, r1)


def _sum_chips(part, r2, chip_idx):
    tr = 256

    def body(c_ref, p_ref, r_ref, o_ref):
        o_ref[...] = ((p_ref[...] + r_ref[0]) + r_ref[1]) + r_ref[2]

    return pl.pallas_call(
        body, name="sum_chips",
        grid_spec=pltpu.PrefetchScalarGridSpec(
            num_scalar_prefetch=1, grid=(r2.shape[1] // tr,),
            in_specs=[pl.BlockSpec((None, tr, PACK_COLS), lambda i, c_ref: (c_ref[0], i, 0)),
                      pl.BlockSpec((N_CHIPS - 1, tr, PACK_COLS), lambda i, c_ref: (0, i, 0))],
            out_specs=pl.BlockSpec((tr, PACK_COLS), lambda i, c_ref: (i, 0))),
        out_shape=jax.ShapeDtypeStruct(r2.shape[1:], F32),
        compiler_params=_cparams(("parallel",)),
    )(chip_idx, part, r2)


def _join_halves(red):
    def body(r_ref, out_ref, send_sem, recv_sem):
        x, y, c = _position()
        cp = pltpu.make_async_remote_copy(src_ref=r_ref, dst_ref=out_ref, send_sem=send_sem, recv_sem=recv_sem,
                                          device_id=(x, y, 1 - c), device_id_type=MESH)
        cp.start()
        cp.wait()

    return pl.pallas_call(
        body, name="join_halves", in_specs=[ANY], out_specs=ANY,
        out_shape=jax.ShapeDtypeStruct(red.shape, red.dtype),
        scratch_shapes=[pltpu.SemaphoreType.DMA, pltpu.SemaphoreType.DMA],
    )(red)


def _adamw_math(w, g, m, v):
    m = ADAM_B1 * m + (1.0 - ADAM_B1) * g
    v = ADAM_B2 * v + (1.0 - ADAM_B2) * (g * g)
    m_hat = m / (1.0 - ADAM_B1 ** ADAM_STEP)
    v_hat = v / (1.0 - ADAM_B2 ** ADAM_STEP)
    delta = -ADAM_LR * (m_hat / (jnp.sqrt(v_hat) + ADAM_EPS) + ADAM_WD * w)
    return delta, m, v


def _adamw(name, w, g, m, v):
    tr = min(256, w.shape[0])

    def body(w_ref, g_ref, m_ref, v_ref, d_ref, nm_ref, nv_ref):
        d_ref[...], nm_ref[...], nv_ref[...] = _adamw_math(w_ref[...], g_ref[...], m_ref[...], v_ref[...])

    blk = pl.BlockSpec((tr, w.shape[1]), lambda i: (i, 0))
    return pl.pallas_call(
        body, name="adamw_" + name, grid=(w.shape[0] // tr,), in_specs=[blk] * 4, out_specs=[blk] * 3,
        out_shape=[jax.ShapeDtypeStruct(w.shape, F32)] * 3, compiler_params=_cparams(("parallel",)),
    )(w, g, m, v)


def _small_step(gpk, wpk, mpk, vpk, part):
    row, col, width = SMALL["ret_decay_logit"]

    def body(g_ref, w_ref, m_ref, v_ref, p_ref, og_ref, od_ref, om_ref, ov_ref, land_ref, gbuf, send_sems, recv_sems,
             big_send, big_recv):
        x, y, c = _position()
        me = 4 * x + 2 * y + c
        chips = _other_chips(x, y)
        big = [pltpu.make_async_remote_copy(
            src_ref=p_ref.at[2 * chip[0] + chip[1]], dst_ref=land_ref.at[j], send_sem=big_send.at[j], recv_sem=big_recv.at[j],
            device_id=(*chip, c), device_id_type=MESH) for j, chip in enumerate(chips)]
        for cp in big:
            cp.start()
        gbuf[me] = g_ref[...]
        sends = []
        for k in range(1, 8):
            to = (x ^ (k >> 2), y ^ ((k >> 1) & 1), c ^ (k & 1))
            cp = pltpu.make_async_remote_copy(src_ref=g_ref, dst_ref=gbuf.at[me], send_sem=send_sems.at[k - 1],
                                              recv_sem=recv_sems.at[k - 1], device_id=to, device_id_type=MESH)
            cp.start()
            sends.append(cp)
        for k in range(1, 8):
            frm = me ^ k
            pltpu.make_async_remote_copy(src_ref=g_ref, dst_ref=gbuf.at[frm], send_sem=send_sems.at[k - 1],
                                         recv_sem=recv_sems.at[k - 1], device_id=(x, y, c), device_id_type=MESH).wait_recv()
        for cp in sends:
            cp.wait_send()
        tot = gbuf[0]
        for d in range(1, 8):
            tot = tot + gbuf[d]
        w = w_ref[...]
        r_i = lax.broadcasted_iota(jnp.int32, w.shape, 0)
        c_i = lax.broadcasted_iota(jnp.int32, w.shape, 1)
        is_logit = (r_i == row) & (c_i >= col) & (c_i < col + width)
        g = jnp.where(is_logit, tot * _sigmoid(-w), tot)
        og_ref[...] = g
        od_ref[...], om_ref[...], ov_ref[...] = _adamw_math(w, g, m_ref[...], v_ref[...])
        for cp in big:
            cp.wait()

    vm = pl.BlockSpec(memory_space=pltpu.VMEM)
    shp = jax.ShapeDtypeStruct(gpk.shape, F32)
    return pl.pallas_call(
        body, name="small_step", in_specs=[vm] * 4 + [ANY], out_specs=[vm] * 4 + [ANY],
        out_shape=[shp] * 4 + [jax.ShapeDtypeStruct((N_CHIPS - 1,) + part.shape[1:], part.dtype)],
        scratch_shapes=[pltpu.VMEM((8,) + gpk.shape, F32), pltpu.SemaphoreType.DMA((7,)), pltpu.SemaphoreType.DMA((7,)),
                        pltpu.SemaphoreType.DMA((3,)), pltpu.SemaphoreType.DMA((3,))],
    )(gpk, wpk, mpk, vpk, part)


def _pack_small(parts):
    rows = [[] for _ in range(SMALL_ROWS)]
    for n, (r, col, width) in sorted(SMALL.items(), key=lambda kv: (kv[1][0], kv[1][1])):
        rows[r].append((col, parts[n].reshape(-1).astype(F32)))
    out = []
    for r in range(SMALL_ROWS):
        segs, pos = [], 0
        for col, vec in rows[r]:
            assert col == pos
            segs.append(vec)
            pos += vec.shape[0]
        if pos < PACK_COLS:
            segs.append(jnp.zeros((PACK_COLS - pos,), F32))
        out.append(jnp.concatenate(segs))
    return jnp.stack(out)


def _unpack_small(pk, shapes):
    return {n: pk[r, col:col + width].reshape(shapes[n]) for n, (r, col, width) in SMALL.items()}


WEIGHTS = ("mix_norm", "w_in", "attn_q_norm", "attn_k_norm", "ret_decay_logit", "ret_norm_gain", "w_attn_o", "w_ret_o", "w_out",
           "mlp_norm", "w_up", "w_down", "ple_norm", "w_ple_gate", "w_ple", "final_norm")


def kernel(x, p, mix_norm, w_in, attn_q_norm, attn_k_norm, ret_decay_logit, ret_norm_gain, w_attn_o, w_ret_o, w_out, mlp_norm, w_up, w_down, ple_norm, w_ple_gate, w_ple, final_norm, loss_target, m_mix_norm, m_w_in, m_attn_q_norm, m_attn_k_norm, m_ret_decay_logit, m_ret_norm_gain, m_w_attn_o, m_w_ret_o, m_w_out, m_mlp_norm, m_w_up, m_w_down, m_ple_norm, m_w_ple_gate, m_w_ple, m_final_norm, v_mix_norm, v_w_in, v_attn_q_norm, v_attn_k_norm, v_ret_decay_logit, v_ret_norm_gain, v_w_attn_o, v_w_ret_o, v_w_out, v_mlp_norm, v_w_up, v_w_down, v_ple_norm, v_w_ple_gate, v_w_ple, v_final_norm):
    args = dict(locals())
    wts = {n: args[n] for n in WEIGHTS}
    ms = {n: args["m_" + n] for n in WEIGHTS}
    vs = {n: args["v_" + n] for n in WEIGHTS}
    shapes = {n: wts[n].shape for n in WEIGHTS}
    big_names = [n for n, _ in BIG]
    xi, yi, ci = _position()
    c_idx = ci.astype(jnp.int32).reshape(1)

    chip_idx = (2 * xi + yi).astype(jnp.int32)
    slab_b = _pack_shard({n: wts[n][0].astype(BF16) for n in big_names})
    others, hb = _gather_weights(slab_b, x[0], wts["mix_norm"].reshape(1, D_MODEL))
    gathered = lax.dynamic_update_slice(others, slab_b[None], (chip_idx, 0, 0))
    full, r0 = {}, 0
    for n, axis in BIG:
        shard = shapes[n][1:]
        rows = math.prod(shard) // PACK_COLS
        if axis == 0 and shard[1] == PACK_COLS:
            full[n] = gathered[:, r0:r0 + rows].reshape(N_CHIPS * shard[0], shard[1])
        else:
            full[n] = jnp.concatenate([gathered[k, r0:r0 + rows].reshape(shard) for k in range(N_CHIPS)], axis=axis)
        r0 += rows
    small = {n: wts[n].reshape(wts[n].shape[1:] if wts[n].ndim > 1 else wts[n].shape) for n in SMALL}

    loss_part, grad_x, gw, gs = _local_step(x[0], p[0, 0], loss_target[0], full, small, hb)
    loss = lax.psum(loss_part, ("x", "y", "c"))

    slabs = jnp.stack([_pack_shard({n: _shard_of(gw[n], axis, k) for n, axis in BIG}) for k in range(N_CHIPS)])
    halves = slabs.reshape(N_CHIPS, 2, HALF_ROWS, PACK_COLS)
    chip_part, chip_part_b = _add_my_half(halves, _exchange_halves(halves), c_idx)
    *sm_out, landed = _small_step(_pack_small(gs), _pack_small({n: wts[n] for n in SMALL}), _pack_small({n: ms[n] for n in SMALL}),
                                  _pack_small({n: vs[n] for n in SMALL}), chip_part_b)
    mine = _sum_chips(chip_part, landed, chip_idx.reshape(1))
    both = jnp.stack([mine, _join_halves(mine)])
    reduced = jnp.where(ci == 0, both, both[::-1]).reshape(2 * HALF_ROWS, PACK_COLS)
    g_big = _unpack_shard(reduced, {n: shapes[n][1:] for n in big_names})
    big_out = [{}, {}, {}, {}]
    for n in big_names:
        big_out[0][n] = g_big[n][None]
        for kind, a in enumerate(_adamw(n, wts[n][0], g_big[n], ms[n][0], vs[n][0])):
            big_out[kind + 1][n] = a[None]

    small_out = [_unpack_small(a, {n: shapes[n] for n in SMALL}) for a in sm_out]

    outs = [loss, grad_x[None]]
    for kind in range(4):
        for n in WEIGHTS:
            outs.append(small_out[kind][n] if n in SMALL else big_out[kind][n])
    return tuple(outs)
```
